```python
import math
import jax, jax.numpy as jnp
from jax import lax
import numpy as np

D_MODEL = 1024
BATCH = 4
SEQ = 4096
DEPTH = 4
DEC_BATCH = 8
DEC_SEQ = 16
PAST_LEN = 4096

CHUNK = 64
N_META = 16
Q_BLOCK = 128
N_EVEN = (DEPTH + 1) // 2
N_ODD = DEPTH // 2
S5_WIDTH = 512
S5_CH = 16
S5_GROUPS = S5_WIDTH // S5_CH
S5_STATE = 64
MLA_HEADS = 8
MLA_Q_RANK = 256
MLA_KV_RANK = 128
MLA_NOPE = 64
MLA_ROPE = 32
MLA_V = 64
FOX_HEADS = 8
FOX_DIM = 64
FOX_W = FOX_HEADS * FOX_DIM
RET_HEADS = 4
RET_DK = 64
RET_DV = 128
RET_QK = RET_HEADS * RET_DK
RET_VW = RET_HEADS * RET_DV
D_FF = 2816
ROPE_BASE = 10000.0
ALPHA = (2.0 * DEPTH) ** 0.25
BETA = (8.0 * DEPTH) ** -0.25
EPS = 1e-5
NEG = -1e30
EVEN_IN = S5_WIDTH + MLA_Q_RANK + MLA_KV_RANK + MLA_ROPE
EVEN_OUT = S5_WIDTH + MLA_HEADS * MLA_V
ODD_IN = 3 * FOX_W + FOX_HEADS + 2 * RET_QK + 2 * RET_VW
ODD_OUT = FOX_W + RET_VW

kernel_name = 'hybrid_streaming_encoder_step'

F32 = jnp.float32


def _layer_norm(x, g, b):
    xf = x.astype(F32)
    mu = xf.mean(-1, keepdims=True)
    var = jnp.square(xf - mu).mean(-1, keepdims=True)
    return ((xf - mu) * lax.rsqrt(var + EPS) * g.astype(F32) + b.astype(F32)).astype(x.dtype)


def _rms_norm(x, g):
    xf = x.astype(F32)
    return (xf * lax.rsqrt(jnp.square(xf).mean(-1, keepdims=True) + EPS) * g.astype(F32)).astype(x.dtype)


def _post_ln(x, sub, g, b):
    return _layer_norm(ALPHA * x + sub.astype(x.dtype), g, b)


def _swiglu(x, wg, wu, wd):
    hid = jax.nn.silu(jnp.einsum('bld,df->blf', x, wg)) * jnp.einsum('bld,df->blf', x, wu)
    return jnp.einsum('blf,fd->bld', hid, wd)


def _rope(x, pos):
    half = x.shape[-1] // 2
    inv = ROPE_BASE ** (-jnp.arange(half, dtype=F32) / half)
    ang = pos.astype(F32)[:, None] * inv[None, :]
    cos, sin = jnp.cos(ang)[:, None, :], jnp.sin(ang)[:, None, :]
    x1, x2 = x[..., :half].astype(F32), x[..., half:].astype(F32)
    return jnp.concatenate([x1 * cos - x2 * sin, x1 * sin + x2 * cos], axis=-1).astype(x.dtype)


def _attend(q, k, v, scale, bias, mask):
    s = jnp.einsum('bhqd,bhkd->bhqk', q, k, preferred_element_type=F32) * scale + bias
    p = jax.nn.softmax(jnp.where(mask, s, NEG), axis=-1)
    return jnp.einsum('bhqk,bhkd->bhqd', p, v.astype(F32))


def _sweep(block_fn, q_args, q_axes, n_q):
    n_blk = -(-n_q // Q_BLOCK)
    pad = n_blk * Q_BLOCK - n_q

    def split(a, ax):
        a = jnp.pad(a, [(0, pad) if i == ax else (0, 0) for i in range(a.ndim)])
        a = a.reshape(a.shape[:ax] + (n_blk, Q_BLOCK) + a.shape[ax + 1:])
        return jnp.moveaxis(a, ax, 0)

    blocks = tuple(split(a, ax) for a, ax in zip(q_args, q_axes))
    out = lax.map(lambda bl: block_fn(*bl), blocks)
    out = jnp.moveaxis(out, 0, 2)
    return out.reshape(out.shape[:2] + (n_blk * Q_BLOCK,) + out.shape[4:])[:, :, :n_q]


def _linear_combine(e1, e2):
    a1, b1 = e1
    a2, b2 = e2
    return a1 * a2, a2 * b1 + b2


def _s5(u, h0, a_re, a_im, b_re, b_im, c_re, c_im, d_skip, log_dt, w_glu, b_glu):
    bsz, seq, _ = u.shape
    uf = u.astype(F32)
    lam = lax.complex(a_re.astype(F32), a_im.astype(F32))
    lam_dt = lam * jnp.exp(log_dt.astype(F32))[:, None]
    a_bar = jnp.exp(lam_dt)
    b_bar = ((a_bar - 1.0) / lam)[..., None] * lax.complex(b_re.astype(F32), b_im.astype(F32))
    ug = uf.reshape(bsz, seq, S5_GROUPS, S5_CH).astype(jnp.complex64)
    bu = jnp.einsum('gnc,blgc->blgn', b_bar, ug)
    _, hs = lax.associative_scan(_linear_combine, (jnp.broadcast_to(a_bar, bu.shape), bu), axis=1)
    steps = jnp.arange(1, seq + 1, dtype=F32)[:, None, None]
    hs = hs + jnp.exp(lam_dt[None] * steps)[None] * h0[:, None]
    c = lax.complex(c_re.astype(F32), c_im.astype(F32))
    y = jnp.einsum('gcn,blgn->blgc', c, hs).real.reshape(bsz, seq, S5_WIDTH) + d_skip.astype(F32) * uf
    g = jax.nn.gelu(y)
    out = g * jax.nn.sigmoid(g @ w_glu.astype(F32) + b_glu.astype(F32))
    return out.astype(u.dtype), hs[:, -1]


def _mla(q_lat, c_kv, k_pe, pos, past_ckv, past_kpe, q_norm, kv_norm, w_uq, w_ukv):
    bsz, seq, _ = q_lat.shape
    q = jnp.einsum('blr,re->ble', _rms_norm(q_lat, q_norm), w_uq).reshape(bsz, seq, MLA_HEADS, MLA_NOPE + MLA_ROPE)
    q = jnp.concatenate([q[..., :MLA_NOPE], _rope(q[..., MLA_NOPE:], pos)], axis=-1)
    ckv_rows = _rms_norm(c_kv, kv_norm)
    kpe_rows = _rope(k_pe[:, :, None, :], pos)[:, :, 0, :]
    if past_ckv is None:
        ckv_all, kpe_all = ckv_rows, kpe_rows
    else:
        ckv_all = jnp.concatenate([past_ckv, ckv_rows], axis=1)
        kpe_all = jnp.concatenate([past_kpe, kpe_rows], axis=1)
    n_k = ckv_all.shape[1]
    kv = jnp.einsum('bkr,re->bke', ckv_all, w_ukv).reshape(bsz, n_k, MLA_HEADS, MLA_NOPE + MLA_V)
    k = jnp.concatenate([kv[..., :MLA_NOPE], jnp.broadcast_to(kpe_all[:, :, None, :], (bsz, n_k, MLA_HEADS, MLA_ROPE))], axis=-1)
    qt = q.transpose(0, 2, 1, 3)
    kt = k.transpose(0, 2, 1, 3)
    vt = kv[..., MLA_NOPE:].transpose(0, 2, 1, 3)
    scale = (MLA_NOPE + MLA_ROPE) ** -0.5
    if past_ckv is None:
        cid = (pos - N_META) // CHUNK
        o = _sweep(lambda qb, cb: _attend(qb, kt, vt, scale, 0.0, cb[:, None] >= cid[None, :]), (qt, cid), (2, 0), seq)
    else:
        o = _attend(qt, kt, vt, scale, 0.0, True)
    o = o.transpose(0, 2, 1, 3).reshape(bsz, seq, MLA_HEADS * MLA_V).astype(q_lat.dtype)
    return o, ckv_rows, kpe_rows


def _fox(q, k, v, f_logit, b_f, past_k, past_v, past_logf):
    bsz, seq = q.shape[:2]
    logf = jax.nn.log_sigmoid((f_logit + b_f).astype(F32))
    if past_k is None:
        k_all, v_all, lf_all = k, v, logf
    else:
        k_all = jnp.concatenate([past_k, k], axis=1)
        v_all = jnp.concatenate([past_v, v], axis=1)
        lf_all = jnp.concatenate([past_logf.astype(F32), logf], axis=1)
    n_k = k_all.shape[1]
    f_cum = jnp.cumsum(lf_all, axis=1).transpose(0, 2, 1)
    f_q = f_cum[:, :, n_k - seq:]
    k_idx = jnp.arange(n_k, dtype=jnp.int32)
    q_idx = jnp.arange(n_k - seq, n_k, dtype=jnp.int32)
    qt, kt, vt = q.transpose(0, 2, 1, 3), k_all.transpose(0, 2, 1, 3), v_all.transpose(0, 2, 1, 3)
    scale = FOX_DIM ** -0.5

    def blk(qb, fb, ib):
        return _attend(qb, kt, vt, scale, fb[..., :, None] - f_cum[..., None, :], ib[:, None] >= k_idx[None, :])

    if past_k is None:
        o = _sweep(blk, (qt, f_q, q_idx), (2, 2, 0), seq)
    else:
        o = blk(qt, f_q, q_idx)
    o = o.transpose(0, 2, 1, 3).reshape(bsz, seq, FOX_W).astype(q.dtype)
    return o, k, v, logf


def _retention_chunk(q, k, v, s0, log_gamma):
    c = q.shape[2]
    j = jnp.arange(c, dtype=F32)
    diff = j[:, None] - j[None, :]
    lg = log_gamma[:, None, None]
    decay = jnp.where(diff >= 0, jnp.exp(lg * jnp.maximum(diff, 0.0)), 0.0)
    scores = jnp.einsum('bhid,bhjd->bhij', q, k) * decay
    q_dec = q * jnp.exp(log_gamma[:, None] * (j + 1.0))[None, :, :, None]
    out = jnp.einsum('bhij,bhjv->bhiv', scores, v) + jnp.einsum('bhid,bhdv->bhiv', q_dec, s0)
    k_dec = k * jnp.exp(log_gamma[:, None] * (c - 1.0 - j))[None, :, :, None]
    s_new = jnp.exp(log_gamma * c)[None, :, None, None] * s0 + jnp.einsum('bhjd,bhjv->bhdv', k_dec, v)
    return out, s_new


def _retention(q, k, v, g, pos, s0):
    bsz, seq = q.shape[:2]
    qt = _rope(q, pos).astype(F32).transpose(0, 2, 1, 3)
    kt = (_rope(k, pos).astype(F32) * RET_DK ** -0.5).transpose(0, 2, 1, 3)
    vt = v.astype(F32).transpose(0, 2, 1, 3)
    log_gamma = jnp.log(1.0 - jnp.exp2(-5.0 - jnp.arange(RET_HEADS, dtype=F32)))
    if s0 is None:
        pad = CHUNK - N_META
        n_c = (seq + pad) // CHUNK

        def chunks(a):
            a = jnp.pad(a, ((0, 0), (0, 0), (pad, 0), (0, 0)))
            return jnp.moveaxis(a.reshape(bsz, RET_HEADS, n_c, CHUNK, a.shape[-1]), 2, 0)

        def step(s, qkv):
            o_c, s = _retention_chunk(qkv[0], qkv[1], qkv[2], s, log_gamma)
            return s, o_c

        s_last, o = lax.scan(step, jnp.zeros((bsz, RET_HEADS, RET_DK, RET_DV), F32), (chunks(qt), chunks(kt), chunks(vt)))
        o = jnp.moveaxis(o, 0, 2).reshape(bsz, RET_HEADS, n_c * CHUNK, RET_DV)[:, :, pad:]
    else:
        o, s_last = _retention_chunk(qt, kt, vt, s0.astype(F32), log_gamma)
    mu = o.mean(-1, keepdims=True)
    var = jnp.square(o - mu).mean(-1, keepdims=True)
    gn = ((o - mu) * lax.rsqrt(var + EPS)).transpose(0, 2, 1, 3).reshape(bsz, seq, RET_VW)
    return (jax.nn.silu(g.astype(F32)) * gn).astype(g.dtype), s_last


def _even_mixing(x, pos, e, past, w):
    bsz, seq, _ = x.shape
    h = jnp.einsum('bld,de->ble', x, w['even_w_in'][e])
    u, q_lat, c_kv, k_pe = jnp.split(h, [S5_WIDTH, S5_WIDTH + MLA_Q_RANK, S5_WIDTH + MLA_Q_RANK + MLA_KV_RANK], axis=-1)
    if past is None:
        h0 = jnp.zeros((bsz, S5_GROUPS, S5_STATE), jnp.complex64)
        p_ckv = p_kpe = None
    else:
        h0 = lax.complex(past['state_s5_re'][e].astype(F32), past['state_s5_im'][e].astype(F32))
        p_ckv, p_kpe = past['cache_mla_ckv'][e], past['cache_mla_kpe'][e]
    s5_out, h_last = _s5(u, h0, w['s5_a_re'][e], w['s5_a_im'][e], w['s5_b_re'][e], w['s5_b_im'][e],
                         w['s5_c_re'][e], w['s5_c_im'][e], w['s5_d'][e], w['s5_log_dt'][e],
                         w['s5_w_glu'][e], w['s5_b_glu'][e])
    mla_out, ckv_rows, kpe_rows = _mla(q_lat, c_kv, k_pe, pos, p_ckv, p_kpe, w['mla_q_norm'][e],
                                       w['mla_kv_norm'][e], w['mla_w_uq'][e], w['mla_w_ukv'][e])
    mix = jnp.einsum('ble,ed->bld', jnp.concatenate([s5_out, mla_out], axis=-1), w['even_w_out'][e])
    return mix, (ckv_rows, kpe_rows, jnp.real(h_last), jnp.imag(h_last))


def _odd_mixing(x, pos, o, past, w):
    bsz, seq, _ = x.shape
    h = jnp.einsum('bld,de->ble', x, w['odd_w_in'][o])
    cuts = np.cumsum([FOX_W, FOX_W, FOX_W, FOX_HEADS, RET_QK, RET_QK, RET_VW]).tolist()
    fq, fk, fv, f_logit, rq, rk, rv, rg = jnp.split(h, cuts, axis=-1)
    if past is None:
        pk = pv = plf = s0 = None
    else:
        pk, pv, plf = past['cache_fox_k'][o], past['cache_fox_v'][o], past['cache_fox_logf'][o]
        s0 = past['state_ret'][o]
    fox_out, k_rows, v_rows, lf_rows = _fox(fq.reshape(bsz, seq, FOX_HEADS, FOX_DIM), fk.reshape(bsz, seq, FOX_HEADS, FOX_DIM),
                                            fv.reshape(bsz, seq, FOX_HEADS, FOX_DIM), f_logit, w['fox_b_f'][o], pk, pv, plf)
    ret_out, s_last = _retention(rq.reshape(bsz, seq, RET_HEADS, RET_DK), rk.reshape(bsz, seq, RET_HEADS, RET_DK),
                                 rv.reshape(bsz, seq, RET_HEADS, RET_DV), rg, pos, s0)
    mix = jnp.einsum('ble,ed->bld', jnp.concatenate([fox_out, ret_out], axis=-1), w['odd_w_out'][o])
    return mix, (k_rows, v_rows, lf_rows, s_last)


def _trunk(x, pos, past, w):
    rows = {n: [] for n in ('mla_ckv', 'mla_kpe', 's5_re', 's5_im', 'fox_k', 'fox_v', 'fox_logf', 'ret')}
    for l in range(DEPTH):
        x = _post_ln(x, 0.5 * _swiglu(x, w['ffn_w_gate'][l, 0], w['ffn_w_up'][l, 0], w['ffn_w_down'][l, 0]),
                     w['ln_g'][l, 0], w['ln_b'][l, 0])
        if l % 2 == 0:
            mix, st = _even_mixing(x, pos, l // 2, past, w)
            names = ('mla_ckv', 'mla_kpe', 's5_re', 's5_im')
        else:
            mix, st = _odd_mixing(x, pos, l // 2, past, w)
            names = ('fox_k', 'fox_v', 'fox_logf', 'ret')
        for n, a in zip(names, st):
            rows[n].append(a)
        x = _post_ln(x, mix, w['ln_g'][l, 1], w['ln_b'][l, 1])
        x = _post_ln(x, 0.5 * _swiglu(x, w['ffn_w_gate'][l, 1], w['ffn_w_up'][l, 1], w['ffn_w_down'][l, 1]),
                     w['ln_g'][l, 2], w['ln_b'][l, 2])
    return x, {n: jnp.stack(a) for n, a in rows.items()}


def setup_inputs(seed: int = 0) -> dict:
    key = jax.random.key(seed)
    ks = jax.random.split(key, 40)

    def nrm(i, shape, scale=1.0):
        return scale * jax.random.normal(ks[i], shape, F32)

    def gain(i, shape):
        return 1.0 + 0.02 * jax.random.normal(ks[i], shape, F32)

    n_idx = jnp.arange(S5_STATE, dtype=F32)
    return {
        'x_prompt': nrm(0, (BATCH, SEQ, D_MODEL)),
        'x_sample': nrm(1, (DEC_BATCH, DEC_SEQ, D_MODEL)),
        'cache_mla_ckv': nrm(2, (N_EVEN, DEC_BATCH, PAST_LEN, MLA_KV_RANK)),
        'cache_mla_kpe': nrm(3, (N_EVEN, DEC_BATCH, PAST_LEN, MLA_ROPE)),
        'cache_fox_k': nrm(4, (N_ODD, DEC_BATCH, PAST_LEN, FOX_HEADS, FOX_DIM)),
        'cache_fox_v': nrm(5, (N_ODD, DEC_BATCH, PAST_LEN, FOX_HEADS, FOX_DIM)),
        'cache_fox_logf': jax.nn.log_sigmoid(jax.random.uniform(ks[6], (N_ODD, DEC_BATCH, PAST_LEN, FOX_HEADS), F32, 2.0, 5.0)),
        'state_s5_re': nrm(7, (N_EVEN, DEC_BATCH, S5_GROUPS, S5_STATE), 0.3),
        'state_s5_im': nrm(8, (N_EVEN, DEC_BATCH, S5_GROUPS, S5_STATE), 0.3),
        'state_ret': nrm(9, (N_ODD, DEC_BATCH, RET_HEADS, RET_DK, RET_DV), 0.3),
        'meta_tokens': nrm(10, (N_META, D_MODEL)),
        'ln_g': gain(11, (DEPTH, 3, D_MODEL)),
        'ln_b': nrm(12, (DEPTH, 3, D_MODEL), 0.02),
        'ffn_w_gate': nrm(13, (DEPTH, 2, D_MODEL, D_FF), D_MODEL ** -0.5),
        'ffn_w_up': nrm(14, (DEPTH, 2, D_MODEL, D_FF), D_MODEL ** -0.5),
        'ffn_w_down': nrm(15, (DEPTH, 2, D_FF, D_MODEL), BETA * D_FF ** -0.5),
        'even_w_in': nrm(16, (N_EVEN, D_MODEL, EVEN_IN), D_MODEL ** -0.5),
        'even_w_out': nrm(17, (N_EVEN, EVEN_OUT, D_MODEL), BETA * EVEN_OUT ** -0.5),
        's5_a_re': -0.5 + 0.01 * jax.random.normal(ks[18], (N_EVEN, S5_GROUPS, S5_STATE), F32),
        's5_a_im': math.pi * n_idx + 0.01 * jax.random.normal(ks[19], (N_EVEN, S5_GROUPS, S5_STATE), F32),
        's5_b_re': nrm(20, (N_EVEN, S5_GROUPS, S5_STATE, S5_CH), (2.0 * S5_CH) ** -0.5),
        's5_b_im': nrm(21, (N_EVEN, S5_GROUPS, S5_STATE, S5_CH), (2.0 * S5_CH) ** -0.5),
        's5_c_re': nrm(22, (N_EVEN, S5_GROUPS, S5_CH, S5_STATE), (2.0 * S5_STATE) ** -0.5),
        's5_c_im': nrm(23, (N_EVEN, S5_GROUPS, S5_CH, S5_STATE), (2.0 * S5_STATE) ** -0.5),
        's5_d': nrm(24, (N_EVEN, S5_WIDTH)),
        's5_log_dt': jax.random.uniform(ks[25], (N_EVEN, S5_GROUPS), F32, math.log(1e-3), math.log(1e-1)),
        's5_w_glu': nrm(26, (N_EVEN, S5_WIDTH, S5_WIDTH), S5_WIDTH ** -0.5),
        's5_b_glu': nrm(27, (N_EVEN, S5_WIDTH), 0.02),
        'mla_q_norm': gain(28, (N_EVEN, MLA_Q_RANK)),
        'mla_kv_norm': gain(29, (N_EVEN, MLA_KV_RANK)),
        'mla_w_uq': nrm(30, (N_EVEN, MLA_Q_RANK, MLA_HEADS * (MLA_NOPE + MLA_ROPE)), MLA_Q_RANK ** -0.5),
        'mla_w_ukv': nrm(31, (N_EVEN, MLA_KV_RANK, MLA_HEADS * (MLA_NOPE + MLA_V)), MLA_KV_RANK ** -0.5),
        'odd_w_in': nrm(32, (N_ODD, D_MODEL, ODD_IN), D_MODEL ** -0.5),
        'odd_w_out': nrm(33, (N_ODD, ODD_OUT, D_MODEL), BETA * ODD_OUT ** -0.5),
        'fox_b_f': jax.random.uniform(ks[34], (N_ODD, FOX_HEADS), F32, 2.0, 5.0),
    }


def reference(x_prompt, x_sample, cache_mla_ckv, cache_mla_kpe, cache_fox_k, cache_fox_v, cache_fox_logf,
              state_s5_re, state_s5_im, state_ret, meta_tokens, ln_g, ln_b, ffn_w_gate, ffn_w_up, ffn_w_down,
              even_w_in, even_w_out, s5_a_re, s5_a_im, s5_b_re, s5_b_im, s5_c_re, s5_c_im, s5_d, s5_log_dt,
              s5_w_glu, s5_b_glu, mla_q_norm, mla_kv_norm, mla_w_uq, mla_w_ukv, odd_w_in, odd_w_out, fox_b_f):
    w = dict(ln_g=ln_g, ln_b=ln_b, ffn_w_gate=ffn_w_gate, ffn_w_up=ffn_w_up, ffn_w_down=ffn_w_down,
             even_w_in=even_w_in, even_w_out=even_w_out, s5_a_re=s5_a_re, s5_a_im=s5_a_im, s5_b_re=s5_b_re,
             s5_b_im=s5_b_im, s5_c_re=s5_c_re, s5_c_im=s5_c_im, s5_d=s5_d, s5_log_dt=s5_log_dt,
             s5_w_glu=s5_w_glu, s5_b_glu=s5_b_glu, mla_q_norm=mla_q_norm, mla_kv_norm=mla_kv_norm,
             mla_w_uq=mla_w_uq, mla_w_ukv=mla_w_ukv, odd_w_in=odd_w_in, odd_w_out=odd_w_out, fox_b_f=fox_b_f)
    past = dict(cache_mla_ckv=cache_mla_ckv, cache_mla_kpe=cache_mla_kpe, cache_fox_k=cache_fox_k,
                cache_fox_v=cache_fox_v, cache_fox_logf=cache_fox_logf, state_s5_re=state_s5_re,
                state_s5_im=state_s5_im, state_ret=state_ret)
    bsz, seq, _ = x_prompt.shape
    meta = jnp.broadcast_to(meta_tokens[None].astype(x_prompt.dtype), (bsz, N_META, D_MODEL))
    pos_p = jnp.arange(N_META + seq, dtype=jnp.int32)
    y_p, st_p = _trunk(jnp.concatenate([meta, x_prompt], axis=1), pos_p, None, w)
    y_prompt = y_p[:, N_META:]
    past_len = cache_fox_k.shape[2]
    pos_s = N_META + past_len + jnp.arange(x_sample.shape[1], dtype=jnp.int32)
    y_sample, st_s = _trunk(x_sample, pos_s, past, w)
    return (y_prompt, y_sample,
            st_p['mla_ckv'], st_p['mla_kpe'], st_p['fox_k'], st_p['fox_v'], st_p['fox_logf'],
            st_p['s5_re'], st_p['s5_im'], st_p['ret'],
            st_s['mla_ckv'], st_s['mla_kpe'], st_s['fox_k'], st_s['fox_v'], st_s['fox_logf'],
            st_s['s5_re'], st_s['s5_im'], st_s['ret'])
```

```python
import functools
import math

import jax
import jax.numpy as jnp
from jax import lax
from jax.experimental import pallas as pl
from jax.experimental.pallas import tpu as pltpu

F32 = jnp.float32
BF16 = jnp.bfloat16

D_MODEL = 1024
DEPTH = 4
CHUNK = 64
CHUNK_SHIFT = 6
N_META = 16
S5_WIDTH = 512
S5_CH = 16
S5_GROUPS = S5_WIDTH // S5_CH
S5_STATE = 64
S5_HALF = S5_GROUPS * S5_STATE
MLA_HEADS = 8
MLA_Q_RANK = 256
MLA_KV_RANK = 128
MLA_NOPE = 64
MLA_ROPE = 32
MLA_V = 64
MLA_QK_PAD = 128
FOX_HEADS = 8
FOX_DIM = 64
FOX_W = FOX_HEADS * FOX_DIM
RET_HEADS = 4
RET_DK = 64
RET_DV = 128
RET_QK = RET_HEADS * RET_DK
RET_VW = RET_HEADS * RET_DV
D_FF = 2816
ROPE_BASE = 10000.0
ALPHA = (2.0 * DEPTH) ** 0.25
EPS = 1e-5
NEG = -1e30
EVEN_IN = S5_WIDTH + MLA_Q_RANK + MLA_KV_RANK + MLA_ROPE
EVEN_IN_PAD = 1024
ODD_IN_PAD = 3200
ODD_FLOGIT_COL = 3072

PAD = CHUNK - N_META
LANE = 128
MXU_N = 256
VMEM_LIMIT = 56 * 1024 * 1024


def _cparams(n_grid, vmem=None):
    return pltpu.CompilerParams(dimension_semantics=("arbitrary",) * n_grid, vmem_limit_bytes=vmem)


def _const_spec(shape):
    nd = len(shape)
    return pl.BlockSpec(shape, lambda *_: (0,) * nd)


def _layer_norm_rows(z, g, b):
    mu = jnp.mean(z, axis=-1, keepdims=True)
    zc = z - mu
    var = jnp.mean(zc * zc, axis=-1, keepdims=True)
    return zc * lax.rsqrt(var + EPS) * g + b


def _ffn_body(x_ref, wg_ref, wu_ref, wd_ref, g_ref, b_ref, o_ref, hid_ref, *, fc):
    x = x_ref[...]
    xb = x.astype(BF16)
    for c in range(D_FF // fc):
        sl = slice(c * fc, (c + 1) * fc)
        hg = jnp.dot(xb, wg_ref[:, sl], preferred_element_type=F32)
        hu = jnp.dot(xb, wu_ref[:, sl], preferred_element_type=F32)
        hid_ref[:, sl] = (hg * jax.nn.sigmoid(hg) * hu).astype(BF16)
    y = jnp.dot(hid_ref[...], wd_ref[...], preferred_element_type=F32)
    o_ref[...] = _layer_norm_rows(ALPHA * x + 0.5 * y, g_ref[...], b_ref[...])


def _ffn_ln(x, wg, wu, wd, g, b, tm):
    rows = x.shape[0]
    return pl.pallas_call(
        functools.partial(_ffn_body, fc=MXU_N),
        grid=(rows // tm,),
        in_specs=[
            pl.BlockSpec((tm, D_MODEL), lambda i: (i, 0)),
            pl.BlockSpec((D_MODEL, D_FF), lambda i: (0, 0), pipeline_mode=pl.Buffered(1)),
            pl.BlockSpec((D_MODEL, D_FF), lambda i: (0, 0), pipeline_mode=pl.Buffered(1)),
            pl.BlockSpec((D_FF, D_MODEL), lambda i: (0, 0), pipeline_mode=pl.Buffered(1)),
            _const_spec((1, D_MODEL)),
            _const_spec((1, D_MODEL)),
        ],
        out_specs=pl.BlockSpec((tm, D_MODEL), lambda i: (i, 0)),
        out_shape=jax.ShapeDtypeStruct((rows, D_MODEL), F32),
        scratch_shapes=[pltpu.VMEM((tm, D_FF), BF16)],
        compiler_params=_cparams(1, VMEM_LIMIT),
        name="ffn_ln",
    )(x, wg, wu, wd, g, b)


def _proj_body(x_ref, w_ref, o_ref, *rest, n16):
    y = jnp.dot(x_ref[...].astype(BF16), w_ref[...], preferred_element_type=F32)
    o_ref[...] = y
    if n16:
        rest[0][...] = y[:, :n16].astype(BF16)


def _proj(x, w, tm, n16=0):
    rows, n = x.shape[0], w.shape[1]
    out_shape = [jax.ShapeDtypeStruct((rows, n), F32)]
    out_specs = [pl.BlockSpec((tm, n), lambda i: (i, 0))]
    if n16:
        out_shape.append(jax.ShapeDtypeStruct((rows, n16), BF16))
        out_specs.append(pl.BlockSpec((tm, n16), lambda i: (i, 0)))
    return pl.pallas_call(
        functools.partial(_proj_body, n16=n16),
        grid=(rows // tm,),
        in_specs=[pl.BlockSpec((tm, D_MODEL), lambda i: (i, 0)), _const_spec(w.shape)],
        out_specs=out_specs,
        out_shape=out_shape,
        compiler_params=_cparams(1, VMEM_LIMIT),
        name="in_proj",
    )(x, w)


def _outproj_body(x_ref, a1_ref, a2_ref, w_ref, g_ref, b_ref, o_ref):
    k1 = a1_ref.shape[1]
    y = jnp.dot(a1_ref[...], w_ref[:k1, :], preferred_element_type=F32)
    y = y + jnp.dot(a2_ref[...], w_ref[k1:, :], preferred_element_type=F32)
    o_ref[...] = _layer_norm_rows(ALPHA * x_ref[...] + y, g_ref[...], b_ref[...])


def _outproj_ln(x, a1, a2, w, g, b, tm):
    rows = x.shape[0]
    return pl.pallas_call(
        _outproj_body,
        grid=(rows // tm,),
        in_specs=[
            pl.BlockSpec((tm, D_MODEL), lambda i: (i, 0)),
            pl.BlockSpec((tm, a1.shape[1]), lambda i: (i, 0)),
            pl.BlockSpec((tm, a2.shape[1]), lambda i: (i, 0)),
            _const_spec(w.shape),
            _const_spec((1, D_MODEL)),
            _const_spec((1, D_MODEL)),
        ],
        out_specs=pl.BlockSpec((tm, D_MODEL), lambda i: (i, 0)),
        out_shape=jax.ShapeDtypeStruct((rows, D_MODEL), F32),
        compiler_params=_cparams(1, VMEM_LIMIT),
        name="out_proj_ln",
    )(x, a1, a2, w, g, b)


def _s5_body(u_ref, h0r_ref, h0i_ref, ar_ref, ai_ref, bbig_ref, cbig_ref, d_ref, wglu_ref, bglu_ref,
             o_ref, hlr_ref, hli_ref, hs_ref, st_ref, *, first_row):
    t = pl.program_id(1)
    rt = u_ref.shape[1]

    @pl.when(t == 0)
    def _():
        st_ref[0:1, :] = h0r_ref[0]
        st_ref[1:2, :] = h0i_ref[0]

    u = u_ref[0]
    if first_row:
        rows = t * rt + lax.broadcasted_iota(jnp.int32, (rt, 1), 0)
        u = jnp.where(rows >= first_row, u, 0.0)
    hs_ref[...] = jnp.dot(u.astype(BF16), bbig_ref[...], preferred_element_type=F32)

    ar = ar_ref[...]
    ai = ai_ref[...]

    def group(gi, carry):
        hr, hi = carry
        base = pl.multiple_of(gi * 8, 8)
        for j in range(8):
            row = hs_ref[pl.ds(base + j, 1), :]
            nr = ar * hr - ai * hi + row[:, :S5_HALF]
            ni = ar * hi + ai * hr + row[:, S5_HALF:]
            hr, hi = nr, ni
            hs_ref[pl.ds(base + j, 1), :] = jnp.concatenate([hr, hi], axis=1)
        return hr, hi

    hr, hi = lax.fori_loop(0, rt // 8, group, (st_ref[0:1, :], st_ref[1:2, :]))
    st_ref[0:1, :] = hr
    st_ref[1:2, :] = hi

    y = jnp.dot(hs_ref[...].astype(BF16), cbig_ref[...], preferred_element_type=F32) + d_ref[...] * u
    g = jax.nn.gelu(y)
    gate = jnp.dot(g.astype(BF16), wglu_ref[...], preferred_element_type=F32) + bglu_ref[...]
    o_ref[0] = (g * jax.nn.sigmoid(gate)).astype(o_ref.dtype)

    @pl.when(t == pl.num_programs(1) - 1)
    def _():
        hlr_ref[0] = hr
        hli_ref[0] = hi


def _s5(h, h0r, h0i, p, rt, first_row):
    bsz, seq, _ = h.shape
    state_spec = pl.BlockSpec((1, 1, S5_HALF), lambda b, t: (b, 0, 0))
    return pl.pallas_call(
        functools.partial(_s5_body, first_row=first_row),
        grid=(bsz, seq // rt),
        in_specs=[
            pl.BlockSpec((1, rt, S5_WIDTH), lambda b, t: (b, t, 0)),
            state_spec, state_spec,
            _const_spec((1, S5_HALF)), _const_spec((1, S5_HALF)),
            _const_spec((S5_WIDTH, 2 * S5_HALF)), _const_spec((2 * S5_HALF, S5_WIDTH)),
            _const_spec((1, S5_WIDTH)), _const_spec((S5_WIDTH, S5_WIDTH)), _const_spec((1, S5_WIDTH)),
        ],
        out_specs=[pl.BlockSpec((1, rt, S5_WIDTH), lambda b, t: (b, t, 0)), state_spec, state_spec],
        out_shape=[jax.ShapeDtypeStruct((bsz, seq, S5_WIDTH), BF16),
                   jax.ShapeDtypeStruct((bsz, 1, S5_HALF), F32),
                   jax.ShapeDtypeStruct((bsz, 1, S5_HALF), F32)],
        scratch_shapes=[pltpu.VMEM((rt, 2 * S5_HALF), F32), pltpu.VMEM((2, S5_HALF), F32)],
        compiler_params=_cparams(2, VMEM_LIMIT),
        name="s5",
    )(h, h0r, h0i, p["a_re"], p["a_im"], p["bbig"], p["cbig"], p["d"], p["w_glu"], p["b_glu"])


def _rope_lanes(x, c, s1, s2, half):
    n = x.shape[-1]
    return x * c + pltpu.roll(x, n - half, 1) * s1 + pltpu.roll(x, half, 1) * s2


def _mla_rows_body(h_ref, qn_ref, kn_ref, wq_ref, cq_ref, s1q_ref, s2q_ref, ck_ref, s1k_ref, s2k_ref,
                   q_ref, ckv_ref, kpe_ref):
    h = h_ref[0]
    q_lat = h[:, :MLA_Q_RANK]
    q_lat = q_lat * lax.rsqrt(jnp.mean(q_lat * q_lat, axis=-1, keepdims=True) + EPS) * qn_ref[...]
    q = jnp.dot(q_lat.astype(BF16), wq_ref[...], preferred_element_type=F32)
    cq, s1q, s2q = cq_ref[...], s1q_ref[...], s2q_ref[...]
    for hd in range(MLA_HEADS):
        sl = slice(hd * MLA_QK_PAD, (hd + 1) * MLA_QK_PAD)
        q_ref[0, :, sl] = _rope_lanes(q[:, sl], cq, s1q, s2q, MLA_ROPE // 2).astype(BF16)
    c_kv = h[:, MLA_Q_RANK:MLA_Q_RANK + MLA_KV_RANK]
    ckv_ref[0] = c_kv * lax.rsqrt(jnp.mean(c_kv * c_kv, axis=-1, keepdims=True) + EPS) * kn_ref[...]
    k_pe = h[:, MLA_Q_RANK + MLA_KV_RANK:]
    kpe_ref[0] = _rope_lanes(k_pe, ck_ref[...], s1k_ref[...], s2k_ref[...], MLA_ROPE // 2)[:, :MLA_ROPE]


def _mla_rows(h, p, tabs, tm):
    bsz, seq, _ = h.shape
    tab_spec = pl.BlockSpec((tm, LANE), lambda b, t: (t, 0))
    return pl.pallas_call(
        _mla_rows_body,
        grid=(bsz, seq // tm),
        in_specs=[
            pl.BlockSpec((1, tm, EVEN_IN_PAD - S5_WIDTH), lambda b, t: (b, t, 1)),
            _const_spec((1, MLA_Q_RANK)), _const_spec((1, MLA_KV_RANK)),
            _const_spec((MLA_Q_RANK, MLA_HEADS * MLA_QK_PAD)),
        ] + [tab_spec] * 6,
        out_specs=[
            pl.BlockSpec((1, tm, MLA_HEADS * MLA_QK_PAD), lambda b, t: (b, t, 0)),
            pl.BlockSpec((1, tm, MLA_KV_RANK), lambda b, t: (b, t, 0)),
            pl.BlockSpec((1, tm, MLA_ROPE), lambda b, t: (b, t, 0)),
        ],
        out_shape=[jax.ShapeDtypeStruct((bsz, seq, MLA_HEADS * MLA_QK_PAD), BF16),
                   jax.ShapeDtypeStruct((bsz, seq, MLA_KV_RANK), F32),
                   jax.ShapeDtypeStruct((bsz, seq, MLA_ROPE), F32)],
        compiler_params=_cparams(2, VMEM_LIMIT),
        name="mla_rows",
    )(h, p["q_norm"], p["kv_norm"], p["wq"], *tabs)


def _mla_kv_body(ckv_ref, kpe_ref, wkn_ref, wkp_ref, wv_ref, k_ref, v_ref):
    ckv = ckv_ref[0].astype(BF16)
    k = jnp.dot(ckv, wkn_ref[...], preferred_element_type=F32)
    k = k + jnp.dot(kpe_ref[0].astype(BF16), wkp_ref[...], preferred_element_type=F32)
    k_ref[0] = k.astype(BF16)
    v_ref[0] = jnp.dot(ckv, wv_ref[...], preferred_element_type=F32).astype(BF16)


def _mla_kv(ckv, kpe, p, tm):
    bsz, seq, _ = ckv.shape
    return pl.pallas_call(
        _mla_kv_body,
        grid=(bsz, seq // tm),
        in_specs=[
            pl.BlockSpec((1, tm, MLA_KV_RANK), lambda b, t: (b, t, 0)),
            pl.BlockSpec((1, tm, MLA_ROPE), lambda b, t: (b, t, 0)),
            _const_spec(p["wkn"].shape), _const_spec(p["wkp"].shape), _const_spec(p["wv"].shape),
        ],
        out_specs=[pl.BlockSpec((1, tm, MLA_HEADS * MLA_QK_PAD), lambda b, t: (b, t, 0)),
                   pl.BlockSpec((1, tm, MLA_HEADS * MLA_V), lambda b, t: (b, t, 0))],
        out_shape=[jax.ShapeDtypeStruct((bsz, seq, MLA_HEADS * MLA_QK_PAD), BF16),
                   jax.ShapeDtypeStruct((bsz, seq, MLA_HEADS * MLA_V), BF16)],
        compiler_params=_cparams(2, VMEM_LIMIT),
        name="mla_kv",
    )(ckv, kpe, p["wkn"], p["wkp"], p["wv"])


def _flash_body(*refs, heads, dq, dv, tk, q_off, causal):
    if causal:
        q_ref, k_ref, v_ref, fq_ref, fk_ref, o_ref = refs
    else:
        q_ref, k_ref, v_ref, o_ref = refs
    tq = q_ref.shape[1]
    n_keys = k_ref.shape[1]
    n_full = n_keys // tk
    rem = n_keys - n_full * tk
    r0 = q_off + pl.program_id(1) * tq
    r_max = r0 + tq - 1
    n_loop = jnp.minimum(n_full, r_max // tk + 1)
    rows = r0 + lax.broadcasted_iota(jnp.int32, (tq, 1), 0)

    for hd in range(heads):
        q_h = q_ref[0, :, hd * dq:(hd + 1) * dq]
        if causal:
            fq_h = fq_ref[0, :, hd:hd + 1]

        def tile(carry, k_t, v_t, fk_t, c0):
            m, l, acc = carry
            width = k_t.shape[0]
            s = lax.dot_general(q_h, k_t, (((1,), (1,)), ((), ())), preferred_element_type=F32)
            cols = c0 + lax.broadcasted_iota(jnp.int32, (1, width), 1)
            if causal:
                s = s + (fq_h - fk_t)
                ok = (cols <= rows) & (cols >= PAD)
            else:
                ok = ((cols >> CHUNK_SHIFT) <= (rows >> CHUNK_SHIFT)) & (cols >= PAD)
            s = jnp.where(ok, s, NEG)
            m_new = jnp.maximum(m, jnp.max(s, axis=-1, keepdims=True))
            a = jnp.exp(m - m_new)
            p = jnp.exp(s - m_new)
            l = a * l + jnp.sum(p, axis=-1, keepdims=True)
            acc = a * acc + jnp.dot(p.astype(BF16), v_t, preferred_element_type=F32)
            return m_new, l, acc

        def full_tile(j, carry):
            start = pl.multiple_of(j * tk, tk)
            k_t = k_ref[0, pl.ds(start, tk), hd * dq:(hd + 1) * dq]
            v_t = v_ref[0, pl.ds(start, tk), hd * dv:(hd + 1) * dv]
            fk_t = fk_ref[0, hd, pl.ds(j, 1), :] if causal else None
            return tile(carry, k_t, v_t, fk_t, j * tk)

        init = (jnp.full((tq, 1), NEG, F32), jnp.zeros((tq, 1), F32), jnp.zeros((tq, dv), F32))
        carry = lax.fori_loop(0, n_loop, full_tile, init)
        if rem:
            def rem_tile(carry):
                k_t = k_ref[0, n_full * tk:, hd * dq:(hd + 1) * dq]
                v_t = v_ref[0, n_full * tk:, hd * dv:(hd + 1) * dv]
                fk_t = fk_ref[0, hd, n_full:n_full + 1, :rem] if causal else None
                return tile(carry, k_t, v_t, fk_t, n_full * tk)

            carry = lax.cond(r_max >= n_full * tk, rem_tile, lambda c: c, carry)
        m, l, acc = carry
        o_ref[0, :, hd * dv:(hd + 1) * dv] = (acc / l).astype(o_ref.dtype)


def _flash(q, k, v, fq, fk, *, heads, dq, dv, tq, q_off, causal):
    bsz, n_q, _ = q.shape
    n_keys = k.shape[1]
    tk = MXU_N
    in_specs = [
        pl.BlockSpec((1, tq, heads * dq), lambda b, i: (b, i, 0)),
        pl.BlockSpec((1, n_keys, heads * dq), lambda b, i: (b, 0, 0)),
        pl.BlockSpec((1, n_keys, heads * dv), lambda b, i: (b, 0, 0)),
    ]
    args = [q, k, v]
    if causal:
        in_specs += [pl.BlockSpec((1, tq, heads), lambda b, i: (b, i, 0)),
                     pl.BlockSpec((1,) + fk.shape[1:], lambda b, i: (b, 0, 0, 0))]
        args += [fq, fk]
    return pl.pallas_call(
        functools.partial(_flash_body, heads=heads, dq=dq, dv=dv, tk=tk, q_off=q_off, causal=causal),
        grid=(bsz, n_q // tq),
        in_specs=in_specs,
        out_specs=pl.BlockSpec((1, tq, heads * dv), lambda b, i: (b, i, 0)),
        out_shape=jax.ShapeDtypeStruct((bsz, n_q, heads * dv), BF16),
        compiler_params=_cparams(2, VMEM_LIMIT),
        name="flash_causal" if causal else "flash_chunk",
    )(*args)


def _gate_body(x_ref, b_ref, lf_ref, fc_ref, *, new_start, tile):
    n_rows = x_ref.shape[1]
    tri = (lax.broadcasted_iota(jnp.int32, (tile, tile), 0)
           >= lax.broadcasted_iota(jnp.int32, (tile, tile), 1)).astype(F32)
    carry = jnp.zeros((1, x_ref.shape[2]), F32)
    for i in range(n_rows // tile):
        sl = slice(i * tile, (i + 1) * tile)
        x = x_ref[0, sl, :]
        rows = i * tile + lax.broadcasted_iota(jnp.int32, (tile, 1), 0)
        z = x + b_ref[...]
        lf = jnp.where(rows >= new_start, jnp.minimum(z, 0.0) - jnp.log1p(jnp.exp(-jnp.abs(z))), x)
        lf = jnp.where(rows >= PAD, lf, 0.0)
        lf_ref[0, sl, :] = lf
        cs = jnp.dot(tri, lf, preferred_element_type=F32, precision=lax.Precision.HIGHEST) + carry
        fc_ref[0, sl, :] = cs
        carry = cs[tile - 1:tile, :]


def _gate(x, b_f, new_start, tile):
    bsz, n_rows, heads = x.shape
    spec = pl.BlockSpec((1, n_rows, heads), lambda b: (b, 0, 0))
    return pl.pallas_call(
        functools.partial(_gate_body, new_start=new_start, tile=tile),
        grid=(bsz,),
        in_specs=[spec, _const_spec((1, heads))],
        out_specs=[spec, spec],
        out_shape=[jax.ShapeDtypeStruct(x.shape, F32)] * 2,
        compiler_params=_cparams(1),
        name="fox_gate",
    )(x, b_f)


RET_LOG_GAMMA = tuple(math.log(1.0 - 2.0 ** (-5.0 - h)) for h in range(RET_HEADS))


def _ret_body(rq_ref, rk_ref, rv_ref, rg_ref, s0_ref, cq_ref, s1q_ref, s2q_ref, ck_ref, s1k_ref, s2k_ref,
              o_ref, sl_ref, st_ref):
    c = pl.program_id(1)
    ct = rq_ref.shape[1]

    @pl.when(c == 0)
    def _():
        st_ref[...] = s0_ref[0]

    q = _rope_lanes(rq_ref[0], cq_ref[...], s1q_ref[...], s2q_ref[...], RET_DK // 2)
    k = _rope_lanes(rk_ref[0], ck_ref[...], s1k_ref[...], s2k_ref[...], RET_DK // 2)
    v = rv_ref[0].astype(BF16)
    g = rg_ref[0]
    j = lax.broadcasted_iota(jnp.int32, (ct, 1), 0).astype(F32)
    diff = (lax.broadcasted_iota(jnp.int32, (ct, ct), 0)
            - lax.broadcasted_iota(jnp.int32, (ct, ct), 1)).astype(F32)
    for hd in range(RET_HEADS):
        lg = RET_LOG_GAMMA[hd]
        q_h = q[:, hd * RET_DK:(hd + 1) * RET_DK]
        k_h = k[:, hd * RET_DK:(hd + 1) * RET_DK]
        v_h = v[:, hd * RET_DV:(hd + 1) * RET_DV]
        decay = jnp.where(diff >= 0.0, jnp.exp(lg * jnp.maximum(diff, 0.0)), 0.0)
        scores = lax.dot_general(q_h.astype(BF16), k_h.astype(BF16), (((1,), (1,)), ((), ())),
                                 preferred_element_type=F32) * decay
        s_h = st_ref[hd]
        out = jnp.dot(scores.astype(BF16), v_h, preferred_element_type=F32)
        out = out + jnp.dot((q_h * jnp.exp(lg * (j + 1.0))).astype(BF16), s_h.astype(BF16),
                            preferred_element_type=F32)
        k_dec = (k_h * jnp.exp(lg * (ct - 1.0 - j))).astype(BF16)
        st_ref[hd] = math.exp(lg * ct) * s_h + lax.dot_general(
            k_dec, v_h, (((0,), (0,)), ((), ())), preferred_element_type=F32)
        mu = jnp.mean(out, axis=-1, keepdims=True)
        oc = out - mu
        var = jnp.mean(oc * oc, axis=-1, keepdims=True)
        g_h = g[:, hd * RET_DV:(hd + 1) * RET_DV]
        o_ref[0, :, hd * RET_DV:(hd + 1) * RET_DV] = (
            g_h * jax.nn.sigmoid(g_h) * (oc * lax.rsqrt(var + EPS))).astype(o_ref.dtype)

    @pl.when(c == pl.num_programs(1) - 1)
    def _():
        sl_ref[0] = st_ref[...]


def _retention(h, s0, tabs, ct):
    bsz, seq, _ = h.shape
    tab_spec = pl.BlockSpec((ct, RET_QK), lambda b, c: (c, 0))
    st_spec = pl.BlockSpec((1, RET_HEADS, RET_DK, RET_DV), lambda b, c: (b, 0, 0, 0))
    rq_blk = (3 * FOX_W) // RET_QK
    rv_blk = (3 * FOX_W + 2 * RET_QK) // RET_VW
    return pl.pallas_call(
        _ret_body,
        grid=(bsz, seq // ct),
        in_specs=[
            pl.BlockSpec((1, ct, RET_QK), lambda b, c: (b, c, rq_blk)),
            pl.BlockSpec((1, ct, RET_QK), lambda b, c: (b, c, rq_blk + 1)),
            pl.BlockSpec((1, ct, RET_VW), lambda b, c: (b, c, rv_blk)),
            pl.BlockSpec((1, ct, RET_VW), lambda b, c: (b, c, rv_blk + 1)),
            st_spec,
        ] + [tab_spec] * 6,
        out_specs=[pl.BlockSpec((1, ct, RET_VW), lambda b, c: (b, c, 0)), st_spec],
        out_shape=[jax.ShapeDtypeStruct((bsz, seq, RET_VW), BF16),
                   jax.ShapeDtypeStruct((bsz, RET_HEADS, RET_DK, RET_DV), F32)],
        scratch_shapes=[pltpu.VMEM((RET_HEADS, RET_DK, RET_DV), F32)],
        compiler_params=_cparams(2, VMEM_LIMIT),
        name="retention",
    )(h, h, h, h, s0, *tabs)


def _rope_tables(pos, half, width, group, offset, scale=1.0, valid=None):
    inv = ROPE_BASE ** (-jnp.arange(half, dtype=F32) / half)
    ang = pos.astype(F32)[:, None] * inv[None, :]
    cos, sin = jnp.cos(ang), jnp.sin(ang)
    n = pos.shape[0]
    one, zero = jnp.ones((n, 1), F32), jnp.zeros((n, 1), F32)

    def lanes(first, second, other):
        grp = jnp.concatenate([jnp.broadcast_to(other, (n, offset)), first, second,
                               jnp.broadcast_to(other, (n, group - offset - 2 * half))], axis=1)
        return jnp.tile(grp, (1, width // group))

    tabs = (lanes(cos, cos, one), lanes(-sin, 0.0 * sin, zero), lanes(0.0 * sin, sin, zero))
    if valid is not None:
        tabs = tuple(jnp.where(valid[:, None], t, 0.0) for t in tabs)
    return tuple(t * scale for t in tabs)


def _even_params(e, w):
    p = {}
    p["w_in"] = jnp.pad(w["even_w_in"][e], ((0, 0), (0, EVEN_IN_PAD - EVEN_IN))).astype(BF16)
    p["w_out"] = w["even_w_out"][e].astype(BF16)
    lam_re, lam_im = w["s5_a_re"][e].astype(F32), w["s5_a_im"][e].astype(F32)
    dt = jnp.exp(w["s5_log_dt"][e].astype(F32))[:, None]
    mag = jnp.exp(lam_re * dt)
    abar_re, abar_im = mag * jnp.cos(lam_im * dt), mag * jnp.sin(lam_im * dt)
    den = lam_re * lam_re + lam_im * lam_im
    f_re = ((abar_re - 1.0) * lam_re + abar_im * lam_im) / den
    f_im = (abar_im * lam_re - (abar_re - 1.0) * lam_im) / den
    b_re, b_im = w["s5_b_re"][e].astype(F32), w["s5_b_im"][e].astype(F32)
    bb_re = f_re[..., None] * b_re - f_im[..., None] * b_im
    bb_im = f_re[..., None] * b_im + f_im[..., None] * b_re
    eye = jnp.eye(S5_GROUPS, dtype=F32)

    def in_blocks(x):
        return jnp.einsum("gnc,gh->gchn", x, eye).reshape(S5_WIDTH, S5_HALF)

    def out_blocks(x):
        return jnp.einsum("gcn,gh->gnhc", x, eye).reshape(S5_HALF, S5_WIDTH)

    p["bbig"] = jnp.concatenate([in_blocks(bb_re), in_blocks(bb_im)], axis=1).astype(BF16)
    p["cbig"] = jnp.concatenate([out_blocks(w["s5_c_re"][e].astype(F32)),
                                 out_blocks(-w["s5_c_im"][e].astype(F32))], axis=0).astype(BF16)
    p["a_re"] = abar_re.reshape(1, S5_HALF)
    p["a_im"] = abar_im.reshape(1, S5_HALF)
    p["d"] = w["s5_d"][e].astype(F32).reshape(1, S5_WIDTH)
    p["w_glu"] = w["s5_w_glu"][e].astype(BF16)
    p["b_glu"] = w["s5_b_glu"][e].astype(F32).reshape(1, S5_WIDTH)
    p["q_norm"] = w["mla_q_norm"][e].astype(F32).reshape(1, MLA_Q_RANK)
    p["kv_norm"] = w["mla_kv_norm"][e].astype(F32).reshape(1, MLA_KV_RANK)
    wq = w["mla_w_uq"][e].reshape(MLA_Q_RANK, MLA_HEADS, MLA_NOPE + MLA_ROPE)
    wq = jnp.pad(wq, ((0, 0), (0, 0), (0, MLA_QK_PAD - MLA_NOPE - MLA_ROPE)))
    p["wq"] = wq.reshape(MLA_Q_RANK, MLA_HEADS * MLA_QK_PAD).astype(BF16)
    wkv = w["mla_w_ukv"][e].reshape(MLA_KV_RANK, MLA_HEADS, MLA_NOPE + MLA_V)
    wkn = jnp.pad(wkv[:, :, :MLA_NOPE], ((0, 0), (0, 0), (0, MLA_QK_PAD - MLA_NOPE)))
    p["wkn"] = wkn.reshape(MLA_KV_RANK, MLA_HEADS * MLA_QK_PAD).astype(BF16)
    place = jnp.pad(jnp.eye(MLA_ROPE, dtype=F32), ((0, 0), (MLA_NOPE, MLA_QK_PAD - MLA_NOPE - MLA_ROPE)))
    p["wkp"] = jnp.tile(place, (1, MLA_HEADS)).astype(BF16)
    p["wv"] = wkv[:, :, MLA_NOPE:].reshape(MLA_KV_RANK, MLA_HEADS * MLA_V).astype(BF16)
    return p


def _odd_params(o, w):
    w_in = w["odd_w_in"][o]
    c_logit = 3 * FOX_W
    cols = jnp.concatenate([
        w_in[:, :FOX_W] * (FOX_DIM ** -0.5),
        w_in[:, FOX_W:c_logit],
        w_in[:, c_logit + FOX_HEADS:],
        w_in[:, c_logit:c_logit + FOX_HEADS],
    ], axis=1)
    p = {"w_in": jnp.pad(cols, ((0, 0), (0, ODD_IN_PAD - cols.shape[1]))).astype(BF16)}
    p["w_out"] = w["odd_w_out"][o].astype(BF16)
    p["b_f"] = w["fox_b_f"][o].astype(F32).reshape(1, FOX_HEADS)
    return p


def _fk_tiles(fcum):
    bsz, n_keys, heads = fcum.shape
    n_t = -(-n_keys // MXU_N)
    fk = jnp.pad(fcum.transpose(0, 2, 1), ((0, 0), (0, 0), (0, n_t * MXU_N - n_keys)))
    return fk.reshape(bsz, heads, n_t, MXU_N)


def _trunk(x, pos, valid, past, w, ffn, evens, odds, *, tm, rt, tq, ct, kt, first_row):
    bsz, seq, _ = x.shape
    rows = bsz * seq
    n_keys = seq if past is None else PAD + past["cache_fox_k"].shape[2] + seq
    q_off = n_keys - seq
    st = {n: [] for n in ("mla_ckv", "mla_kpe", "s5_re", "s5_im", "fox_k", "fox_v", "fox_logf", "ret")}
    mla_scale = (MLA_NOPE + MLA_ROPE) ** -0.5
    q_tabs = _rope_tables(pos, MLA_ROPE // 2, MLA_QK_PAD, MLA_QK_PAD, MLA_NOPE, scale=mla_scale)
    k_tabs = _rope_tables(pos, MLA_ROPE // 2, LANE, LANE, 0)
    rq_tabs = _rope_tables(pos, RET_DK // 2, RET_QK, RET_DK, 0)
    rk_tabs = _rope_tables(pos, RET_DK // 2, RET_QK, RET_DK, 0, scale=RET_DK ** -0.5, valid=valid)
    x2 = x.reshape(rows, D_MODEL)

    def ln(l, i):
        return w["ln_g"][l, i].reshape(1, D_MODEL), w["ln_b"][l, i].reshape(1, D_MODEL)

    def front(a):
        return jnp.pad(a, ((0, 0), (PAD, 0), (0, 0)))

    for l in range(DEPTH):
        x2 = _ffn_ln(x2, *ffn[l][0], *ln(l, 0), tm)
        if l % 2 == 0:
            e = l // 2
            p = evens[e]
            h = _proj(x2, p["w_in"], tm)[0].reshape(bsz, seq, EVEN_IN_PAD)
            if past is None:
                h0r = h0i = jnp.zeros((bsz, 1, S5_HALF), F32)
            else:
                h0r = past["state_s5_re"][e].astype(F32).reshape(bsz, 1, S5_HALF)
                h0i = past["state_s5_im"][e].astype(F32).reshape(bsz, 1, S5_HALF)
            s5_out, hlr, hli = _s5(h, h0r, h0i, p, rt, first_row)
            q, ckv, kpe = _mla_rows(h, p, q_tabs + k_tabs, rt)
            if past is None:
                ckv_all, kpe_all = ckv, kpe
            else:
                ckv_all = front(jnp.concatenate([past["cache_mla_ckv"][e].astype(F32), ckv], axis=1))
                kpe_all = front(jnp.concatenate([past["cache_mla_kpe"][e].astype(F32), kpe], axis=1))
            k_att, v_att = _mla_kv(ckv_all, kpe_all, p, kt)
            mla_out = _flash(q, k_att, v_att, None, None, heads=MLA_HEADS, dq=MLA_QK_PAD, dv=MLA_V,
                             tq=tq, q_off=q_off, causal=False)
            x2 = _outproj_ln(x2, s5_out.reshape(rows, S5_WIDTH), mla_out.reshape(rows, MLA_HEADS * MLA_V),
                             p["w_out"], *ln(l, 1), tm)
            st["mla_ckv"].append(ckv)
            st["mla_kpe"].append(kpe)
            st["s5_re"].append(hlr.reshape(bsz, S5_GROUPS, S5_STATE))
            st["s5_im"].append(hli.reshape(bsz, S5_GROUPS, S5_STATE))
        else:
            o = l // 2
            p = odds[o]
            h, qkv = _proj(x2, p["w_in"], tm, n16=3 * FOX_W)
            h = h.reshape(bsz, seq, ODD_IN_PAD)
            qkv = qkv.reshape(bsz, seq, 3 * FOX_W)
            f_logit = h[:, :, ODD_FLOGIT_COL:ODD_FLOGIT_COL + FOX_HEADS]
            fq16, fk16, fv16 = qkv[:, :, :FOX_W], qkv[:, :, FOX_W:2 * FOX_W], qkv[:, :, 2 * FOX_W:]
            if past is None:
                gate_in, k_all, v_all = f_logit, fk16, fv16
                s0 = jnp.zeros((bsz, RET_HEADS, RET_DK, RET_DV), F32)
            else:
                gate_in = front(jnp.concatenate([past["cache_fox_logf"][o].astype(F32), f_logit], axis=1))
                n_past = past["cache_fox_k"].shape[2]
                k_all = front(jnp.concatenate([past["cache_fox_k"][o].reshape(bsz, n_past, FOX_W).astype(BF16),
                                               fk16], axis=1))
                v_all = front(jnp.concatenate([past["cache_fox_v"][o].reshape(bsz, n_past, FOX_W).astype(BF16),
                                               fv16], axis=1))
                s0 = past["state_ret"][o].astype(F32)
            logf, fcum = _gate(gate_in, p["b_f"], q_off, kt)
            fox_out = _flash(fq16, k_all, v_all, fcum[:, q_off:], _fk_tiles(fcum), heads=FOX_HEADS, dq=FOX_DIM,
                             dv=FOX_DIM, tq=tq, q_off=q_off, causal=True)
            ret_out, s_last = _retention(h, s0, rq_tabs + rk_tabs, ct)
            x2 = _outproj_ln(x2, fox_out.reshape(rows, FOX_W), ret_out.reshape(rows, RET_VW),
                             p["w_out"], *ln(l, 1), tm)
            st["fox_k"].append(h[:, :, FOX_W:2 * FOX_W].reshape(bsz, seq, FOX_HEADS, FOX_DIM))
            st["fox_v"].append(h[:, :, 2 * FOX_W:3 * FOX_W].reshape(bsz, seq, FOX_HEADS, FOX_DIM))
            st["fox_logf"].append(logf[:, q_off:])
            st["ret"].append(s_last)
        x2 = _ffn_ln(x2, *ffn[l][1], *ln(l, 2), tm)
    return x2.reshape(bsz, seq, D_MODEL), {n: jnp.stack(a) for n, a in st.items()}


def kernel(x_prompt, x_sample, cache_mla_ckv, cache_mla_kpe, cache_fox_k, cache_fox_v, cache_fox_logf,
           state_s5_re, state_s5_im, state_ret, meta_tokens, ln_g, ln_b, ffn_w_gate, ffn_w_up, ffn_w_down,
           even_w_in, even_w_out, s5_a_re, s5_a_im, s5_b_re, s5_b_im, s5_c_re, s5_c_im, s5_d, s5_log_dt,
           s5_w_glu, s5_b_glu, mla_q_norm, mla_kv_norm, mla_w_uq, mla_w_ukv, odd_w_in, odd_w_out, fox_b_f):
    w = dict(ln_g=ln_g.astype(F32), ln_b=ln_b.astype(F32), even_w_in=even_w_in, even_w_out=even_w_out,
             s5_a_re=s5_a_re, s5_a_im=s5_a_im, s5_b_re=s5_b_re, s5_b_im=s5_b_im, s5_c_re=s5_c_re,
             s5_c_im=s5_c_im, s5_d=s5_d, s5_log_dt=s5_log_dt, s5_w_glu=s5_w_glu, s5_b_glu=s5_b_glu,
             mla_q_norm=mla_q_norm, mla_kv_norm=mla_kv_norm, mla_w_uq=mla_w_uq, mla_w_ukv=mla_w_ukv,
             odd_w_in=odd_w_in, odd_w_out=odd_w_out, fox_b_f=fox_b_f)
    past = dict(cache_mla_ckv=cache_mla_ckv, cache_mla_kpe=cache_mla_kpe, cache_fox_k=cache_fox_k,
                cache_fox_v=cache_fox_v, cache_fox_logf=cache_fox_logf, state_s5_re=state_s5_re,
                state_s5_im=state_s5_im, state_ret=state_ret)
    ffn = [[(ffn_w_gate[l, i].astype(BF16), ffn_w_up[l, i].astype(BF16), ffn_w_down[l, i].astype(BF16))
            for i in range(2)] for l in range(DEPTH)]
    evens = [_even_params(e, w) for e in range((DEPTH + 1) // 2)]
    odds = [_odd_params(o, w) for o in range(DEPTH // 2)]

    bsz, seq, _ = x_prompt.shape
    meta = jnp.broadcast_to(meta_tokens[None].astype(x_prompt.dtype), (bsz, N_META, D_MODEL))
    xp = jnp.concatenate([jnp.zeros((bsz, PAD, D_MODEL), x_prompt.dtype), meta, x_prompt], axis=1)
    n_rows = PAD + N_META + seq
    idx = jnp.arange(n_rows, dtype=jnp.int32)
    y_p, st_p = _trunk(xp, jnp.maximum(idx - PAD, 0), idx >= PAD, None, w, ffn, evens, odds,
                       tm=640, rt=320, tq=320, ct=320, kt=320, first_row=PAD)
    d_bsz, d_seq, _ = x_sample.shape
    past_len = cache_fox_k.shape[2]
    pos_s = N_META + past_len + jnp.arange(d_seq, dtype=jnp.int32)
    y_s, st_s = _trunk(x_sample, pos_s, None, past, w, ffn, evens, odds,
                       tm=d_bsz * d_seq, rt=d_seq, tq=d_seq, ct=d_seq, kt=320, first_row=0)

    def real(a):
        return a[:, :, PAD:]

    return (y_p[:, PAD + N_META:], y_s,
            real(st_p["mla_ckv"]), real(st_p["mla_kpe"]), real(st_p["fox_k"]), real(st_p["fox_v"]),
            real(st_p["fox_logf"]), st_p["s5_re"], st_p["s5_im"], st_p["ret"],
            st_s["mla_ckv"], st_s["mla_kpe"], st_s["fox_k"], st_s["fox_v"], st_s["fox_logf"],
            st_s["s5_re"], st_s["s5_im"], st_s["ret"])
```

```python
import functools
import math

import jax
import jax.numpy as jnp
from jax import lax
from jax.experimental import pallas as pl
from jax.experimental.pallas import tpu as pltpu

F32 = jnp.float32
BF16 = jnp.bfloat16

D_MODEL = 1024
DEPTH = 4
CHUNK = 64
CHUNK_SHIFT = 6
N_META = 16
S5_WIDTH = 512
S5_CH = 16
S5_GROUPS = S5_WIDTH // S5_CH
S5_STATE = 64
S5_HALF = S5_GROUPS * S5_STATE
MLA_HEADS = 8
MLA_Q_RANK = 256
MLA_KV_RANK = 128
MLA_NOPE = 64
MLA_ROPE = 32
MLA_V = 64
MLA_QK_PAD = 128
FOX_HEADS = 8
FOX_DIM = 64
FOX_W = FOX_HEADS * FOX_DIM
RET_HEADS = 4
RET_DK = 64
RET_DV = 128
RET_QK = RET_HEADS * RET_DK
RET_VW = RET_HEADS * RET_DV
D_FF = 2816
ROPE_BASE = 10000.0
ALPHA = (2.0 * DEPTH) ** 0.25
EPS = 1e-5
NEG = -1e30
EVEN_IN = S5_WIDTH + MLA_Q_RANK + MLA_KV_RANK + MLA_ROPE
EVEN_IN_PAD = 1024
ODD_IN_PAD = 3200
ODD_FLOGIT_COL = 3072

PAD = CHUNK - N_META
LANE = 128
MXU_N = 256
VMEM_LIMIT = 56 * 1024 * 1024


def _cparams(n_grid, vmem=None):
    return pltpu.CompilerParams(dimension_semantics=("arbitrary",) * n_grid, vmem_limit_bytes=vmem)


def _const_spec(shape):
    nd = len(shape)
    return pl.BlockSpec(shape, lambda *_: (0,) * nd)


def _layer_norm_rows(z, g, b):
    mu = jnp.mean(z, axis=-1, keepdims=True)
    zc = z - mu
    var = jnp.mean(zc * zc, axis=-1, keepdims=True)
    return zc * lax.rsqrt(var + EPS) * g + b


def _ffn_body(x_ref, wg_ref, wu_ref, wd_ref, g_ref, b_ref, o_ref, hid_ref, *, fc):
    x = x_ref[...]
    xb = x.astype(BF16)
    for c in range(D_FF // fc):
        sl = slice(c * fc, (c + 1) * fc)
        hg = jnp.dot(xb, wg_ref[:, sl], preferred_element_type=F32)
        hu = jnp.dot(xb, wu_ref[:, sl], preferred_element_type=F32)
        hid_ref[:, sl] = (hg * jax.nn.sigmoid(hg) * hu).astype(BF16)
    y = jnp.dot(hid_ref[...], wd_ref[...], preferred_element_type=F32)
    o_ref[...] = _layer_norm_rows(ALPHA * x + 0.5 * y, g_ref[...], b_ref[...])


def _ffn_ln(x, wg, wu, wd, g, b, tm):
    rows = x.shape[0]
    return pl.pallas_call(
        functools.partial(_ffn_body, fc=MXU_N),
        grid=(rows // tm,),
        in_specs=[
            pl.BlockSpec((tm, D_MODEL), lambda i: (i, 0)),
            pl.BlockSpec((D_MODEL, D_FF), lambda i: (0, 0), pipeline_mode=pl.Buffered(1)),
            pl.BlockSpec((D_MODEL, D_FF), lambda i: (0, 0), pipeline_mode=pl.Buffered(1)),
            pl.BlockSpec((D_FF, D_MODEL), lambda i: (0, 0), pipeline_mode=pl.Buffered(1)),
            _const_spec((1, D_MODEL)),
            _const_spec((1, D_MODEL)),
        ],
        out_specs=pl.BlockSpec((tm, D_MODEL), lambda i: (i, 0)),
        out_shape=jax.ShapeDtypeStruct((rows, D_MODEL), F32),
        scratch_shapes=[pltpu.VMEM((tm, D_FF), BF16)],
        compiler_params=_cparams(1, VMEM_LIMIT),
        name="ffn_ln",
    )(x, wg, wu, wd, g, b)


def _proj_body(x_ref, w_ref, o_ref, *rest, n16):
    y = jnp.dot(x_ref[...].astype(BF16), w_ref[...], preferred_element_type=F32)
    o_ref[...] = y
    if n16:
        rest[0][...] = y[:, :n16].astype(BF16)


def _proj(x, w, tm, n16=0):
    rows, n = x.shape[0], w.shape[1]
    out_shape = [jax.ShapeDtypeStruct((rows, n), F32)]
    out_specs = [pl.BlockSpec((tm, n), lambda i: (i, 0))]
    if n16:
        out_shape.append(jax.ShapeDtypeStruct((rows, n16), BF16))
        out_specs.append(pl.BlockSpec((tm, n16), lambda i: (i, 0)))
    return pl.pallas_call(
        functools.partial(_proj_body, n16=n16),
        grid=(rows // tm,),
        in_specs=[pl.BlockSpec((tm, D_MODEL), lambda i: (i, 0)), _const_spec(w.shape)],
        out_specs=out_specs,
        out_shape=out_shape,
        compiler_params=_cparams(1, VMEM_LIMIT),
        name="in_proj",
    )(x, w)


def _outproj_body(x_ref, a1_ref, a2_ref, w_ref, g_ref, b_ref, o_ref):
    k1 = a1_ref.shape[1]
    y = jnp.dot(a1_ref[...], w_ref[:k1, :], preferred_element_type=F32)
    y = y + jnp.dot(a2_ref[...], w_ref[k1:, :], preferred_element_type=F32)
    o_ref[...] = _layer_norm_rows(ALPHA * x_ref[...] + y, g_ref[...], b_ref[...])


def _outproj_ln(x, a1, a2, w, g, b, tm):
    rows = x.shape[0]
    return pl.pallas_call(
        _outproj_body,
        grid=(rows // tm,),
        in_specs=[
            pl.BlockSpec((tm, D_MODEL), lambda i: (i, 0)),
            pl.BlockSpec((tm, a1.shape[1]), lambda i: (i, 0)),
            pl.BlockSpec((tm, a2.shape[1]), lambda i: (i, 0)),
            _const_spec(w.shape),
            _const_spec((1, D_MODEL)),
            _const_spec((1, D_MODEL)),
        ],
        out_specs=pl.BlockSpec((tm, D_MODEL), lambda i: (i, 0)),
        out_shape=jax.ShapeDtypeStruct((rows, D_MODEL), F32),
        compiler_params=_cparams(1, VMEM_LIMIT),
        name="out_proj_ln",
    )(x, a1, a2, w, g, b)


def _s5_body(u_ref, h0r_ref, h0i_ref, ar_ref, ai_ref, bbig_ref, cbig_ref, d_ref, wglu_ref, bglu_ref,
             o_ref, hlr_ref, hli_ref, hs_ref, st_ref, *, first_row):
    t = pl.program_id(1)
    rt = u_ref.shape[1]

    @pl.when(t == 0)
    def _():
        st_ref[0:1, :] = h0r_ref[0]
        st_ref[1:2, :] = h0i_ref[0]

    u = u_ref[0]
    if first_row:
        rows = t * rt + lax.broadcasted_iota(jnp.int32, (rt, 1), 0)
        u = jnp.where(rows >= first_row, u, 0.0)
    hs_ref[...] = jnp.dot(u.astype(BF16), bbig_ref[...], preferred_element_type=F32)

    ar = ar_ref[...]
    ai = ai_ref[...]

    def group(gi, carry):
        hr, hi = carry
        base = pl.multiple_of(gi * 8, 8)
        for j in range(8):
            row = hs_ref[pl.ds(base + j, 1), :]
            nr = ar * hr - ai * hi + row[:, :S5_HALF]
            ni = ar * hi + ai * hr + row[:, S5_HALF:]
            hr, hi = nr, ni
            hs_ref[pl.ds(base + j, 1), :] = jnp.concatenate([hr, hi], axis=1)
        return hr, hi

    hr, hi = lax.fori_loop(0, rt // 8, group, (st_ref[0:1, :], st_ref[1:2, :]))
    st_ref[0:1, :] = hr
    st_ref[1:2, :] = hi

    y = jnp.dot(hs_ref[...].astype(BF16), cbig_ref[...], preferred_element_type=F32) + d_ref[...] * u
    g = jax.nn.gelu(y)
    gate = jnp.dot(g.astype(BF16), wglu_ref[...], preferred_element_type=F32) + bglu_ref[...]
    o_ref[0] = (g * jax.nn.sigmoid(gate)).astype(o_ref.dtype)

    @pl.when(t == pl.num_programs(1) - 1)
    def _():
        hlr_ref[0] = hr
        hli_ref[0] = hi


def _s5(h, h0r, h0i, p, rt, first_row):
    bsz, seq, _ = h.shape
    state_spec = pl.BlockSpec((1, 1, S5_HALF), lambda b, t: (b, 0, 0))
    return pl.pallas_call(
        functools.partial(_s5_body, first_row=first_row),
        grid=(bsz, seq // rt),
        in_specs=[
            pl.BlockSpec((1, rt, S5_WIDTH), lambda b, t: (b, t, 0)),
            state_spec, state_spec,
            _const_spec((1, S5_HALF)), _const_spec((1, S5_HALF)),
            _const_spec((S5_WIDTH, 2 * S5_HALF)), _const_spec((2 * S5_HALF, S5_WIDTH)),
            _const_spec((1, S5_WIDTH)), _const_spec((S5_WIDTH, S5_WIDTH)), _const_spec((1, S5_WIDTH)),
        ],
        out_specs=[pl.BlockSpec((1, rt, S5_WIDTH), lambda b, t: (b, t, 0)), state_spec, state_spec],
        out_shape=[jax.ShapeDtypeStruct((bsz, seq, S5_WIDTH), BF16),
                   jax.ShapeDtypeStruct((bsz, 1, S5_HALF), F32),
                   jax.ShapeDtypeStruct((bsz, 1, S5_HALF), F32)],
        scratch_shapes=[pltpu.VMEM((rt, 2 * S5_HALF), F32), pltpu.VMEM((2, S5_HALF), F32)],
        compiler_params=_cparams(2, VMEM_LIMIT),
        name="s5",
    )(h, h0r, h0i, p["a_re"], p["a_im"], p["bbig"], p["cbig"], p["d"], p["w_glu"], p["b_glu"])


def _rope_lanes(x, c, s1, s2, half):
    n = x.shape[-1]
    return x * c + pltpu.roll(x, n - half, 1) * s1 + pltpu.roll(x, half, 1) * s2


def _mla_rows_body(h_ref, qn_ref, kn_ref, wq_ref, cq_ref, s1q_ref, s2q_ref, ck_ref, s1k_ref, s2k_ref,
                   q_ref, ckv_ref, kpe_ref):
    h = h_ref[0]
    q_lat = h[:, :MLA_Q_RANK]
    q_lat = q_lat * lax.rsqrt(jnp.mean(q_lat * q_lat, axis=-1, keepdims=True) + EPS) * qn_ref[...]
    q = jnp.dot(q_lat.astype(BF16), wq_ref[...], preferred_element_type=F32)
    cq, s1q, s2q = cq_ref[...], s1q_ref[...], s2q_ref[...]
    for hd in range(MLA_HEADS):
        sl = slice(hd * MLA_QK_PAD, (hd + 1) * MLA_QK_PAD)
        q_ref[0, :, sl] = _rope_lanes(q[:, sl], cq, s1q, s2q, MLA_ROPE // 2).astype(BF16)
    c_kv = h[:, MLA_Q_RANK:MLA_Q_RANK + MLA_KV_RANK]
    ckv_ref[0] = c_kv * lax.rsqrt(jnp.mean(c_kv * c_kv, axis=-1, keepdims=True) + EPS) * kn_ref[...]
    k_pe = h[:, MLA_Q_RANK + MLA_KV_RANK:]
    kpe_ref[0] = _rope_lanes(k_pe, ck_ref[...], s1k_ref[...], s2k_ref[...], MLA_ROPE // 2)[:, :MLA_ROPE]


def _mla_rows(h, p, tabs, tm):
    bsz, seq, _ = h.shape
    tab_spec = pl.BlockSpec((tm, LANE), lambda b, t: (t, 0))
    return pl.pallas_call(
        _mla_rows_body,
        grid=(bsz, seq // tm),
        in_specs=[
            pl.BlockSpec((1, tm, EVEN_IN_PAD - S5_WIDTH), lambda b, t: (b, t, 1)),
            _const_spec((1, MLA_Q_RANK)), _const_spec((1, MLA_KV_RANK)),
            _const_spec((MLA_Q_RANK, MLA_HEADS * MLA_QK_PAD)),
        ] + [tab_spec] * 6,
        out_specs=[
            pl.BlockSpec((1, tm, MLA_HEADS * MLA_QK_PAD), lambda b, t: (b, t, 0)),
            pl.BlockSpec((1, tm, MLA_KV_RANK), lambda b, t: (b, t, 0)),
            pl.BlockSpec((1, tm, MLA_ROPE), lambda b, t: (b, t, 0)),
        ],
        out_shape=[jax.ShapeDtypeStruct((bsz, seq, MLA_HEADS * MLA_QK_PAD), BF16),
                   jax.ShapeDtypeStruct((bsz, seq, MLA_KV_RANK), F32),
                   jax.ShapeDtypeStruct((bsz, seq, MLA_ROPE), F32)],
        compiler_params=_cparams(2, VMEM_LIMIT),
        name="mla_rows",
    )(h, p["q_norm"], p["kv_norm"], p["wq"], *tabs)


def _mla_kv_body(ckv_ref, kpe_ref, wkn_ref, wkp_ref, wv_ref, k_ref, v_ref):
    ckv = ckv_ref[0].astype(BF16)
    k = jnp.dot(ckv, wkn_ref[...], preferred_element_type=F32)
    k = k + jnp.dot(kpe_ref[0].astype(BF16), wkp_ref[...], preferred_element_type=F32)
    k_ref[0] = k.astype(BF16)
    v_ref[0] = jnp.dot(ckv, wv_ref[...], preferred_element_type=F32).astype(BF16)


def _mla_kv(ckv, kpe, p, tm):
    bsz, seq, _ = ckv.shape
    return pl.pallas_call(
        _mla_kv_body,
        grid=(bsz, seq // tm),
        in_specs=[
            pl.BlockSpec((1, tm, MLA_KV_RANK), lambda b, t: (b, t, 0)),
            pl.BlockSpec((1, tm, MLA_ROPE), lambda b, t: (b, t, 0)),
            _const_spec(p["wkn"].shape), _const_spec(p["wkp"].shape), _const_spec(p["wv"].shape),
        ],
        out_specs=[pl.BlockSpec((1, tm, MLA_HEADS * MLA_QK_PAD), lambda b, t: (b, t, 0)),
                   pl.BlockSpec((1, tm, MLA_HEADS * MLA_V), lambda b, t: (b, t, 0))],
        out_shape=[jax.ShapeDtypeStruct((bsz, seq, MLA_HEADS * MLA_QK_PAD), BF16),
                   jax.ShapeDtypeStruct((bsz, seq, MLA_HEADS * MLA_V), BF16)],
        compiler_params=_cparams(2, VMEM_LIMIT),
        name="mla_kv",
    )(ckv, kpe, p["wkn"], p["wkp"], p["wv"])


def _flash_body(*refs, heads, dq, dv, q_off, causal):
    if causal:
        qt_ref, k_ref, vt_ref, fq_ref, fk_ref, ot_ref = refs
    else:
        qt_ref, k_ref, vt_ref, ot_ref = refs
    tq = qt_ref.shape[2]
    n_tiles, tk = vt_ref.shape[2], vt_ref.shape[4]
    r0 = q_off + pl.program_id(1) * tq
    j_last = jnp.minimum((r0 + tq - 1) // tk, n_tiles - 1)
    q_row = r0 + lax.broadcasted_iota(jnp.int32, (1, tq), 1)
    last = q_row if causal else ((q_row >> CHUNK_SHIFT) << CHUNK_SHIFT) + (CHUNK - 1)
    q_t = [qt_ref[0, hd * dq:(hd + 1) * dq, :] for hd in range(heads)]

    def tile(j, carry):
        start = pl.multiple_of(j * tk, tk)
        k_row = start + lax.broadcasted_iota(jnp.int32, (tk, 1), 0)
        ok = (k_row <= last) & (k_row >= PAD)
        scores = [jnp.dot(k_ref[0, pl.ds(start, tk), hd * dq:(hd + 1) * dq], q_t[hd],
                          preferred_element_type=F32) for hd in range(heads)]
        probs = []
        for hd in range(heads):
            m, l, _ = carry[hd]
            s = scores[hd]
            if causal:
                s = s + (fq_ref[0, hd:hd + 1, :] - fk_ref[0, pl.ds(start, tk), hd:hd + 1])
            s = jnp.where(ok, s, NEG)
            m_new = jnp.maximum(m, jnp.max(s, axis=0, keepdims=True))
            a = jnp.exp(m - m_new)
            p = jnp.exp(s - m_new)
            probs.append((m_new, a * l + jnp.sum(p, axis=0, keepdims=True), a, p.astype(BF16)))
        new = []
        for hd in range(heads):
            m_new, l, a, p = probs[hd]
            acc = a * carry[hd][2] + jnp.dot(vt_ref[0, hd, j], p, preferred_element_type=F32)
            new.append((m_new, l, acc))
        return tuple(new)

    init = tuple((jnp.full((1, tq), NEG, F32), jnp.zeros((1, tq), F32), jnp.zeros((dv, tq), F32))
                 for _ in range(heads))
    carry = lax.fori_loop(0, j_last + 1, tile, init)
    for hd in range(heads):
        _, l, acc = carry[hd]
        ot_ref[0, hd * dv:(hd + 1) * dv, :] = (acc / l).astype(ot_ref.dtype)


def _round_up(n, m):
    return -(-n // m) * m


def _flash(q, k, v, fcum, *, heads, dq, dv, q_off, causal):
    bsz, n_q, _ = q.shape
    n_keys = k.shape[1]
    tk = MXU_N
    tq = MXU_N if n_q > MXU_N else n_q
    n_qp, n_kp = _round_up(n_q, tq), _round_up(n_keys, tk)

    def pad_rows(a, n):
        return jnp.pad(a, ((0, 0), (0, n - a.shape[1]), (0, 0)))

    q_t = pad_rows(q, n_qp).transpose(0, 2, 1)
    v_t = pad_rows(v, n_kp).reshape(bsz, n_kp // tk, tk, heads, dv).transpose(0, 3, 1, 4, 2)
    in_specs = [
        pl.BlockSpec((1, heads * dq, tq), lambda b, i: (b, 0, i)),
        pl.BlockSpec((1, n_kp, heads * dq), lambda b, i: (b, 0, 0)),
        pl.BlockSpec((1,) + v_t.shape[1:], lambda b, i: (b, 0, 0, 0, 0)),
    ]
    args = [q_t, pad_rows(k, n_kp), v_t]
    if causal:
        in_specs += [pl.BlockSpec((1, heads, tq), lambda b, i: (b, 0, i)),
                     pl.BlockSpec((1, n_kp, heads), lambda b, i: (b, 0, 0))]
        args += [pad_rows(fcum[:, q_off:], n_qp).transpose(0, 2, 1), pad_rows(fcum, n_kp)]
    o_t = pl.pallas_call(
        functools.partial(_flash_body, heads=heads, dq=dq, dv=dv, q_off=q_off, causal=causal),
        grid=(bsz, n_qp // tq),
        in_specs=in_specs,
        out_specs=pl.BlockSpec((1, heads * dv, tq), lambda b, i: (b, 0, i)),
        out_shape=jax.ShapeDtypeStruct((bsz, heads * dv, n_qp), BF16),
        compiler_params=_cparams(2, VMEM_LIMIT),
        name="flash_causal" if causal else "flash_chunk",
    )(*args)
    return o_t.transpose(0, 2, 1)[:, :n_q]


def _gate_body(x_ref, b_ref, lf_ref, fc_ref, *, new_start, tile):
    n_rows = x_ref.shape[1]
    tri = (lax.broadcasted_iota(jnp.int32, (tile, tile), 0)
           >= lax.broadcasted_iota(jnp.int32, (tile, tile), 1)).astype(F32)
    carry = jnp.zeros((1, x_ref.shape[2]), F32)
    for i in range(n_rows // tile):
        sl = slice(i * tile, (i + 1) * tile)
        x = x_ref[0, sl, :]
        rows = i * tile + lax.broadcasted_iota(jnp.int32, (tile, 1), 0)
        z = x + b_ref[...]
        lf = jnp.where(rows >= new_start, jnp.minimum(z, 0.0) - jnp.log1p(jnp.exp(-jnp.abs(z))), x)
        lf = jnp.where(rows >= PAD, lf, 0.0)
        lf_ref[0, sl, :] = lf
        cs = jnp.dot(tri, lf, preferred_element_type=F32, precision=lax.Precision.HIGHEST) + carry
        fc_ref[0, sl, :] = cs
        carry = cs[tile - 1:tile, :]


def _gate(x, b_f, new_start, tile):
    bsz, n_rows, heads = x.shape
    spec = pl.BlockSpec((1, n_rows, heads), lambda b: (b, 0, 0))
    return pl.pallas_call(
        functools.partial(_gate_body, new_start=new_start, tile=tile),
        grid=(bsz,),
        in_specs=[spec, _const_spec((1, heads))],
        out_specs=[spec, spec],
        out_shape=[jax.ShapeDtypeStruct(x.shape, F32)] * 2,
        compiler_params=_cparams(1),
        name="fox_gate",
    )(x, b_f)


RET_LOG_GAMMA = tuple(math.log(1.0 - 2.0 ** (-5.0 - h)) for h in range(RET_HEADS))


def _ret_body(rq_ref, rk_ref, rv_ref, rg_ref, s0_ref, cq_ref, s1q_ref, s2q_ref, ck_ref, s1k_ref, s2k_ref,
              o_ref, sl_ref, st_ref):
    c = pl.program_id(1)
    ct = rq_ref.shape[1]

    @pl.when(c == 0)
    def _():
        st_ref[...] = s0_ref[0]

    q = _rope_lanes(rq_ref[0], cq_ref[...], s1q_ref[...], s2q_ref[...], RET_DK // 2)
    k = _rope_lanes(rk_ref[0], ck_ref[...], s1k_ref[...], s2k_ref[...], RET_DK // 2)
    v = rv_ref[0].astype(BF16)
    g = rg_ref[0]
    j = lax.broadcasted_iota(jnp.int32, (ct, 1), 0).astype(F32)
    diff = (lax.broadcasted_iota(jnp.int32, (ct, ct), 0)
            - lax.broadcasted_iota(jnp.int32, (ct, ct), 1)).astype(F32)
    for hd in range(RET_HEADS):
        lg = RET_LOG_GAMMA[hd]
        q_h = q[:, hd * RET_DK:(hd + 1) * RET_DK]
        k_h = k[:, hd * RET_DK:(hd + 1) * RET_DK]
        v_h = v[:, hd * RET_DV:(hd + 1) * RET_DV]
        decay = jnp.where(diff >= 0.0, jnp.exp(lg * jnp.maximum(diff, 0.0)), 0.0)
        scores = lax.dot_general(q_h.astype(BF16), k_h.astype(BF16), (((1,), (1,)), ((), ())),
                                 preferred_element_type=F32) * decay
        s_h = st_ref[hd]
        out = jnp.dot(scores.astype(BF16), v_h, preferred_element_type=F32)
        out = out + jnp.dot((q_h * jnp.exp(lg * (j + 1.0))).astype(BF16), s_h.astype(BF16),
                            preferred_element_type=F32)
        k_dec = (k_h * jnp.exp(lg * (ct - 1.0 - j))).astype(BF16)
        st_ref[hd] = math.exp(lg * ct) * s_h + lax.dot_general(
            k_dec, v_h, (((0,), (0,)), ((), ())), preferred_element_type=F32)
        mu = jnp.mean(out, axis=-1, keepdims=True)
        oc = out - mu
        var = jnp.mean(oc * oc, axis=-1, keepdims=True)
        g_h = g[:, hd * RET_DV:(hd + 1) * RET_DV]
        o_ref[0, :, hd * RET_DV:(hd + 1) * RET_DV] = (
            g_h * jax.nn.sigmoid(g_h) * (oc * lax.rsqrt(var + EPS))).astype(o_ref.dtype)

    @pl.when(c == pl.num_programs(1) - 1)
    def _():
        sl_ref[0] = st_ref[...]


def _retention(h, s0, tabs, ct):
    bsz, seq, _ = h.shape
    tab_spec = pl.BlockSpec((ct, RET_QK), lambda b, c: (c, 0))
    st_spec = pl.BlockSpec((1, RET_HEADS, RET_DK, RET_DV), lambda b, c: (b, 0, 0, 0))
    rq_blk = (3 * FOX_W) // RET_QK
    rv_blk = (3 * FOX_W + 2 * RET_QK) // RET_VW
    return pl.pallas_call(
        _ret_body,
        grid=(bsz, seq // ct),
        in_specs=[
            pl.BlockSpec((1, ct, RET_QK), lambda b, c: (b, c, rq_blk)),
            pl.BlockSpec((1, ct, RET_QK), lambda b, c: (b, c, rq_blk + 1)),
            pl.BlockSpec((1, ct, RET_VW), lambda b, c: (b, c, rv_blk)),
            pl.BlockSpec((1, ct, RET_VW), lambda b, c: (b, c, rv_blk + 1)),
            st_spec,
        ] + [tab_spec] * 6,
        out_specs=[pl.BlockSpec((1, ct, RET_VW), lambda b, c: (b, c, 0)), st_spec],
        out_shape=[jax.ShapeDtypeStruct((bsz, seq, RET_VW), BF16),
                   jax.ShapeDtypeStruct((bsz, RET_HEADS, RET_DK, RET_DV), F32)],
        scratch_shapes=[pltpu.VMEM((RET_HEADS, RET_DK, RET_DV), F32)],
        compiler_params=_cparams(2, VMEM_LIMIT),
        name="retention",
    )(h, h, h, h, s0, *tabs)


def _rope_tables(pos, half, width, group, offset, scale=1.0, valid=None):
    inv = ROPE_BASE ** (-jnp.arange(half, dtype=F32) / half)
    ang = pos.astype(F32)[:, None] * inv[None, :]
    cos, sin = jnp.cos(ang), jnp.sin(ang)
    n = pos.shape[0]
    one, zero = jnp.ones((n, 1), F32), jnp.zeros((n, 1), F32)

    def lanes(first, second, other):
        grp = jnp.concatenate([jnp.broadcast_to(other, (n, offset)), first, second,
                               jnp.broadcast_to(other, (n, group - offset - 2 * half))], axis=1)
        return jnp.tile(grp, (1, width // group))

    tabs = (lanes(cos, cos, one), lanes(-sin, 0.0 * sin, zero), lanes(0.0 * sin, sin, zero))
    if valid is not None:
        tabs = tuple(jnp.where(valid[:, None], t, 0.0) for t in tabs)
    return tuple(t * scale for t in tabs)


def _even_params(e, w):
    p = {}
    p["w_in"] = jnp.pad(w["even_w_in"][e], ((0, 0), (0, EVEN_IN_PAD - EVEN_IN))).astype(BF16)
    p["w_out"] = w["even_w_out"][e].astype(BF16)
    lam_re, lam_im = w["s5_a_re"][e].astype(F32), w["s5_a_im"][e].astype(F32)
    dt = jnp.exp(w["s5_log_dt"][e].astype(F32))[:, None]
    mag = jnp.exp(lam_re * dt)
    abar_re, abar_im = mag * jnp.cos(lam_im * dt), mag * jnp.sin(lam_im * dt)
    den = lam_re * lam_re + lam_im * lam_im
    f_re = ((abar_re - 1.0) * lam_re + abar_im * lam_im) / den
    f_im = (abar_im * lam_re - (abar_re - 1.0) * lam_im) / den
    b_re, b_im = w["s5_b_re"][e].astype(F32), w["s5_b_im"][e].astype(F32)
    bb_re = f_re[..., None] * b_re - f_im[..., None] * b_im
    bb_im = f_re[..., None] * b_im + f_im[..., None] * b_re
    eye = jnp.eye(S5_GROUPS, dtype=F32)

    def in_blocks(x):
        return jnp.einsum("gnc,gh->gchn", x, eye).reshape(S5_WIDTH, S5_HALF)

    def out_blocks(x):
        return jnp.einsum("gcn,gh->gnhc", x, eye).reshape(S5_HALF, S5_WIDTH)

    p["bbig"] = jnp.concatenate([in_blocks(bb_re), in_blocks(bb_im)], axis=1).astype(BF16)
    p["cbig"] = jnp.concatenate([out_blocks(w["s5_c_re"][e].astype(F32)),
                                 out_blocks(-w["s5_c_im"][e].astype(F32))], axis=0).astype(BF16)
    p["a_re"] = abar_re.reshape(1, S5_HALF)
    p["a_im"] = abar_im.reshape(1, S5_HALF)
    p["d"] = w["s5_d"][e].astype(F32).reshape(1, S5_WIDTH)
    p["w_glu"] = w["s5_w_glu"][e].astype(BF16)
    p["b_glu"] = w["s5_b_glu"][e].astype(F32).reshape(1, S5_WIDTH)
    p["q_norm"] = w["mla_q_norm"][e].astype(F32).reshape(1, MLA_Q_RANK)
    p["kv_norm"] = w["mla_kv_norm"][e].astype(F32).reshape(1, MLA_KV_RANK)
    wq = w["mla_w_uq"][e].reshape(MLA_Q_RANK, MLA_HEADS, MLA_NOPE + MLA_ROPE)
    wq = jnp.pad(wq, ((0, 0), (0, 0), (0, MLA_QK_PAD - MLA_NOPE - MLA_ROPE)))
    p["wq"] = wq.reshape(MLA_Q_RANK, MLA_HEADS * MLA_QK_PAD).astype(BF16)
    wkv = w["mla_w_ukv"][e].reshape(MLA_KV_RANK, MLA_HEADS, MLA_NOPE + MLA_V)
    wkn = jnp.pad(wkv[:, :, :MLA_NOPE], ((0, 0), (0, 0), (0, MLA_QK_PAD - MLA_NOPE)))
    p["wkn"] = wkn.reshape(MLA_KV_RANK, MLA_HEADS * MLA_QK_PAD).astype(BF16)
    place = jnp.pad(jnp.eye(MLA_ROPE, dtype=F32), ((0, 0), (MLA_NOPE, MLA_QK_PAD - MLA_NOPE - MLA_ROPE)))
    p["wkp"] = jnp.tile(place, (1, MLA_HEADS)).astype(BF16)
    p["wv"] = wkv[:, :, MLA_NOPE:].reshape(MLA_KV_RANK, MLA_HEADS * MLA_V).astype(BF16)
    return p


def _odd_params(o, w):
    w_in = w["odd_w_in"][o]
    c_logit = 3 * FOX_W
    cols = jnp.concatenate([
        w_in[:, :FOX_W] * (FOX_DIM ** -0.5),
        w_in[:, FOX_W:c_logit],
        w_in[:, c_logit + FOX_HEADS:],
        w_in[:, c_logit:c_logit + FOX_HEADS],
    ], axis=1)
    p = {"w_in": jnp.pad(cols, ((0, 0), (0, ODD_IN_PAD - cols.shape[1]))).astype(BF16)}
    p["w_out"] = w["odd_w_out"][o].astype(BF16)
    p["b_f"] = w["fox_b_f"][o].astype(F32).reshape(1, FOX_HEADS)
    return p


def _trunk(x, pos, valid, past, w, ffn, evens, odds, *, tm, rt, ct, kt, first_row):
    bsz, seq, _ = x.shape
    rows = bsz * seq
    n_keys = seq if past is None else PAD + past["cache_fox_k"].shape[2] + seq
    q_off = n_keys - seq
    st = {n: [] for n in ("mla_ckv", "mla_kpe", "s5_re", "s5_im", "fox_k", "fox_v", "fox_logf", "ret")}
    mla_scale = (MLA_NOPE + MLA_ROPE) ** -0.5
    q_tabs = _rope_tables(pos, MLA_ROPE // 2, MLA_QK_PAD, MLA_QK_PAD, MLA_NOPE, scale=mla_scale)
    k_tabs = _rope_tables(pos, MLA_ROPE // 2, LANE, LANE, 0)
    rq_tabs = _rope_tables(pos, RET_DK // 2, RET_QK, RET_DK, 0)
    rk_tabs = _rope_tables(pos, RET_DK // 2, RET_QK, RET_DK, 0, scale=RET_DK ** -0.5, valid=valid)
    x2 = x.reshape(rows, D_MODEL)

    def ln(l, i):
        return w["ln_g"][l, i].reshape(1, D_MODEL), w["ln_b"][l, i].reshape(1, D_MODEL)

    def front(a):
        return jnp.pad(a, ((0, 0), (PAD, 0), (0, 0)))

    for l in range(DEPTH):
        x2 = _ffn_ln(x2, *ffn[l][0], *ln(l, 0), tm)
        if l % 2 == 0:
            e = l // 2
            p = evens[e]
            h = _proj(x2, p["w_in"], tm)[0].reshape(bsz, seq, EVEN_IN_PAD)
            if past is None:
                h0r = h0i = jnp.zeros((bsz, 1, S5_HALF), F32)
            else:
                h0r = past["state_s5_re"][e].astype(F32).reshape(bsz, 1, S5_HALF)
                h0i = past["state_s5_im"][e].astype(F32).reshape(bsz, 1, S5_HALF)
            s5_out, hlr, hli = _s5(h, h0r, h0i, p, rt, first_row)
            q, ckv, kpe = _mla_rows(h, p, q_tabs + k_tabs, rt)
            if past is None:
                ckv_all, kpe_all = ckv, kpe
            else:
                ckv_all = front(jnp.concatenate([past["cache_mla_ckv"][e].astype(F32), ckv], axis=1))
                kpe_all = front(jnp.concatenate([past["cache_mla_kpe"][e].astype(F32), kpe], axis=1))
            k_att, v_att = _mla_kv(ckv_all, kpe_all, p, kt)
            mla_out = _flash(q, k_att, v_att, None, heads=MLA_HEADS, dq=MLA_QK_PAD, dv=MLA_V,
                             q_off=q_off, causal=False)
            x2 = _outproj_ln(x2, s5_out.reshape(rows, S5_WIDTH), mla_out.reshape(rows, MLA_HEADS * MLA_V),
                             p["w_out"], *ln(l, 1), tm)
            st["mla_ckv"].append(ckv)
            st["mla_kpe"].append(kpe)
            st["s5_re"].append(hlr.reshape(bsz, S5_GROUPS, S5_STATE))
            st["s5_im"].append(hli.reshape(bsz, S5_GROUPS, S5_STATE))
        else:
            o = l // 2
            p = odds[o]
            h, qkv = _proj(x2, p["w_in"], tm, n16=3 * FOX_W)
            h = h.reshape(bsz, seq, ODD_IN_PAD)
            qkv = qkv.reshape(bsz, seq, 3 * FOX_W)
            f_logit = h[:, :, ODD_FLOGIT_COL:ODD_FLOGIT_COL + FOX_HEADS]
            fq16, fk16, fv16 = qkv[:, :, :FOX_W], qkv[:, :, FOX_W:2 * FOX_W], qkv[:, :, 2 * FOX_W:]
            if past is None:
                gate_in, k_all, v_all = f_logit, fk16, fv16
                s0 = jnp.zeros((bsz, RET_HEADS, RET_DK, RET_DV), F32)
            else:
                gate_in = front(jnp.concatenate([past["cache_fox_logf"][o].astype(F32), f_logit], axis=1))
                n_past = past["cache_fox_k"].shape[2]
                k_all = front(jnp.concatenate([past["cache_fox_k"][o].reshape(bsz, n_past, FOX_W).astype(BF16),
                                               fk16], axis=1))
                v_all = front(jnp.concatenate([past["cache_fox_v"][o].reshape(bsz, n_past, FOX_W).astype(BF16),
                                               fv16], axis=1))
                s0 = past["state_ret"][o].astype(F32)
            logf, fcum = _gate(gate_in, p["b_f"], q_off, kt)
            fox_out = _flash(fq16, k_all, v_all, fcum, heads=FOX_HEADS, dq=FOX_DIM, dv=FOX_DIM,
                             q_off=q_off, causal=True)
            ret_out, s_last = _retention(h, s0, rq_tabs + rk_tabs, ct)
            x2 = _outproj_ln(x2, fox_out.reshape(rows, FOX_W), ret_out.reshape(rows, RET_VW),
                             p["w_out"], *ln(l, 1), tm)
            st["fox_k"].append(h[:, :, FOX_W:2 * FOX_W].reshape(bsz, seq, FOX_HEADS, FOX_DIM))
            st["fox_v"].append(h[:, :, 2 * FOX_W:3 * FOX_W].reshape(bsz, seq, FOX_HEADS, FOX_DIM))
            st["fox_logf"].append(logf[:, q_off:])
            st["ret"].append(s_last)
        x2 = _ffn_ln(x2, *ffn[l][1], *ln(l, 2), tm)
    return x2.reshape(bsz, seq, D_MODEL), {n: jnp.stack(a) for n, a in st.items()}


def kernel(x_prompt, x_sample, cache_mla_ckv, cache_mla_kpe, cache_fox_k, cache_fox_v, cache_fox_logf,
           state_s5_re, state_s5_im, state_ret, meta_tokens, ln_g, ln_b, ffn_w_gate, ffn_w_up, ffn_w_down,
           even_w_in, even_w_out, s5_a_re, s5_a_im, s5_b_re, s5_b_im, s5_c_re, s5_c_im, s5_d, s5_log_dt,
           s5_w_glu, s5_b_glu, mla_q_norm, mla_kv_norm, mla_w_uq, mla_w_ukv, odd_w_in, odd_w_out, fox_b_f):
    w = dict(ln_g=ln_g.astype(F32), ln_b=ln_b.astype(F32), even_w_in=even_w_in, even_w_out=even_w_out,
             s5_a_re=s5_a_re, s5_a_im=s5_a_im, s5_b_re=s5_b_re, s5_b_im=s5_b_im, s5_c_re=s5_c_re,
             s5_c_im=s5_c_im, s5_d=s5_d, s5_log_dt=s5_log_dt, s5_w_glu=s5_w_glu, s5_b_glu=s5_b_glu,
             mla_q_norm=mla_q_norm, mla_kv_norm=mla_kv_norm, mla_w_uq=mla_w_uq, mla_w_ukv=mla_w_ukv,
             odd_w_in=odd_w_in, odd_w_out=odd_w_out, fox_b_f=fox_b_f)
    past = dict(cache_mla_ckv=cache_mla_ckv, cache_mla_kpe=cache_mla_kpe, cache_fox_k=cache_fox_k,
                cache_fox_v=cache_fox_v, cache_fox_logf=cache_fox_logf, state_s5_re=state_s5_re,
                state_s5_im=state_s5_im, state_ret=state_ret)
    ffn = [[(ffn_w_gate[l, i].astype(BF16), ffn_w_up[l, i].astype(BF16), ffn_w_down[l, i].astype(BF16))
            for i in range(2)] for l in range(DEPTH)]
    evens = [_even_params(e, w) for e in range((DEPTH + 1) // 2)]
    odds = [_odd_params(o, w) for o in range(DEPTH // 2)]

    bsz, seq, _ = x_prompt.shape
    meta = jnp.broadcast_to(meta_tokens[None].astype(x_prompt.dtype), (bsz, N_META, D_MODEL))
    xp = jnp.concatenate([jnp.zeros((bsz, PAD, D_MODEL), x_prompt.dtype), meta, x_prompt], axis=1)
    n_rows = PAD + N_META + seq
    idx = jnp.arange(n_rows, dtype=jnp.int32)
    y_p, st_p = _trunk(xp, jnp.maximum(idx - PAD, 0), idx >= PAD, None, w, ffn, evens, odds,
                       tm=640, rt=320, ct=320, kt=320, first_row=PAD)
    d_bsz, d_seq, _ = x_sample.shape
    past_len = cache_fox_k.shape[2]
    pos_s = N_META + past_len + jnp.arange(d_seq, dtype=jnp.int32)
    y_s, st_s = _trunk(x_sample, pos_s, None, past, w, ffn, evens, odds,
                       tm=d_bsz * d_seq, rt=d_seq, ct=d_seq, kt=320, first_row=0)

    def real(a):
        return a[:, :, PAD:]

    return (y_p[:, PAD + N_META:], y_s,
            real(st_p["mla_ckv"]), real(st_p["mla_kpe"]), real(st_p["fox_k"]), real(st_p["fox_v"]),
            real(st_p["fox_logf"]), st_p["s5_re"], st_p["s5_im"], st_p["ret"],
            st_s["mla_ckv"], st_s["mla_kpe"], st_s["fox_k"], st_s["fox_v"], st_s["fox_logf"],
            st_s["s5_re"], st_s["s5_im"], st_s["ret"])
```

```python
import functools
import math

import jax
import jax.numpy as jnp
from jax import lax
from jax.experimental import pallas as pl
from jax.experimental.pallas import tpu as pltpu

F32 = jnp.float32
BF16 = jnp.bfloat16

D_MODEL = 1024
DEPTH = 4
CHUNK = 64
CHUNK_SHIFT = 6
N_META = 16
S5_WIDTH = 512
S5_CH = 16
S5_GROUPS = S5_WIDTH // S5_CH
S5_STATE = 64
S5_HALF = S5_GROUPS * S5_STATE
MLA_HEADS = 8
MLA_Q_RANK = 256
MLA_KV_RANK = 128
MLA_NOPE = 64
MLA_ROPE = 32
MLA_V = 64
MLA_QK_PAD = 128
FOX_HEADS = 8
FOX_DIM = 64
FOX_W = FOX_HEADS * FOX_DIM
RET_HEADS = 4
RET_DK = 64
RET_DV = 128
RET_QK = RET_HEADS * RET_DK
RET_VW = RET_HEADS * RET_DV
D_FF = 2816
ROPE_BASE = 10000.0
ALPHA = (2.0 * DEPTH) ** 0.25
EPS = 1e-5
NEG = -1e30
EVEN_IN = S5_WIDTH + MLA_Q_RANK + MLA_KV_RANK + MLA_ROPE
EVEN_IN_PAD = 1024
ODD_IN_PAD = 3200
ODD_H = ODD_IN_PAD - FOX_W
ODD_LOGIT_COL = 2 * FOX_W + 2 * RET_QK + 2 * RET_VW

PAD = CHUNK - N_META
LANE = 128
SUBLANE = 8
TILE = 256
VMEM_LIMIT = 56 * 1024 * 1024


def _cparams(n_grid, vmem=None):
    return pltpu.CompilerParams(dimension_semantics=("arbitrary",) * n_grid, vmem_limit_bytes=vmem)


def _const_spec(shape):
    nd = len(shape)
    return pl.BlockSpec(shape, lambda *_: (0,) * nd)


def _round_up(n, m):
    return -(-n // m) * m


def _layer_norm_rows(z, g, b):
    mu = jnp.mean(z, axis=-1, keepdims=True)
    zc = z - mu
    var = jnp.mean(zc * zc, axis=-1, keepdims=True)
    return zc * lax.rsqrt(var + EPS) * g + b


def _ffn_body(x_ref, wg_ref, wu_ref, wd_ref, g_ref, b_ref, o_ref, hid_ref, *, fc):
    x = x_ref[...]
    xb = x.astype(BF16)
    for c in range(D_FF // fc):
        sl = slice(c * fc, (c + 1) * fc)
        hg = jnp.dot(xb, wg_ref[:, sl], preferred_element_type=F32)
        hu = jnp.dot(xb, wu_ref[:, sl], preferred_element_type=F32)
        hid_ref[:, sl] = (hg * jax.nn.sigmoid(hg) * hu).astype(BF16)
    y = jnp.dot(hid_ref[...], wd_ref[...], preferred_element_type=F32)
    o_ref[...] = _layer_norm_rows(ALPHA * x + 0.5 * y, g_ref[...], b_ref[...])


def _ffn_ln(x, wg, wu, wd, g, b, tm):
    rows = x.shape[0]
    return pl.pallas_call(
        functools.partial(_ffn_body, fc=TILE),
        grid=(rows // tm,),
        in_specs=[
            pl.BlockSpec((tm, D_MODEL), lambda i: (i, 0)),
            pl.BlockSpec((D_MODEL, D_FF), lambda i: (0, 0), pipeline_mode=pl.Buffered(1)),
            pl.BlockSpec((D_MODEL, D_FF), lambda i: (0, 0), pipeline_mode=pl.Buffered(1)),
            pl.BlockSpec((D_FF, D_MODEL), lambda i: (0, 0), pipeline_mode=pl.Buffered(1)),
            _const_spec((1, D_MODEL)),
            _const_spec((1, D_MODEL)),
        ],
        out_specs=pl.BlockSpec((tm, D_MODEL), lambda i: (i, 0)),
        out_shape=jax.ShapeDtypeStruct((rows, D_MODEL), F32),
        scratch_shapes=[pltpu.VMEM((tm, D_FF), BF16)],
        compiler_params=_cparams(1, VMEM_LIMIT),
        name="ffn_ln",
    )(x, wg, wu, wd, g, b)


def _proj_body(x_ref, w_ref, o_ref):
    o_ref[...] = jnp.dot(x_ref[...].astype(BF16), w_ref[...], preferred_element_type=F32)


def _proj(x, w, tm):
    rows, n = x.shape[0], w.shape[1]
    return pl.pallas_call(
        _proj_body,
        grid=(rows // tm,),
        in_specs=[pl.BlockSpec((tm, D_MODEL), lambda i: (i, 0)), _const_spec(w.shape)],
        out_specs=pl.BlockSpec((tm, n), lambda i: (i, 0)),
        out_shape=jax.ShapeDtypeStruct((rows, n), F32),
        compiler_params=_cparams(1, VMEM_LIMIT),
        name="in_proj_even",
    )(x, w)


def _proj_odd_body(x_ref, w_ref, h_ref, q_ref, k_ref, v_ref, *, attn_layout):
    y = jnp.dot(x_ref[...].astype(BF16), w_ref[...], preferred_element_type=F32)
    h_ref[...] = y[:, FOX_W:]
    k_ref[...] = y[:, FOX_W:2 * FOX_W].astype(BF16)
    if attn_layout:
        q_ref[0] = y[:, :FOX_W].T.astype(BF16)
        v_t = y[:, 2 * FOX_W:3 * FOX_W].T
        for hd in range(FOX_HEADS):
            v_ref[0, hd, 0] = v_t[hd * FOX_DIM:(hd + 1) * FOX_DIM, :].astype(BF16)
    else:
        q_ref[...] = y[:, :FOX_W].astype(BF16)
        v_ref[...] = y[:, 2 * FOX_W:3 * FOX_W].astype(BF16)


def _proj_odd(x, w, bsz, tm, attn_layout):
    rows = x.shape[0]
    seq = rows // bsz
    row_spec = lambda n: pl.BlockSpec((tm, n), lambda i: (i, 0))
    if attn_layout:
        assert tm == TILE and seq % TILE == 0
        n_t = seq // TILE
        q_shape = jax.ShapeDtypeStruct((bsz, FOX_W, seq), BF16)
        q_spec = pl.BlockSpec((1, FOX_W, TILE), lambda i: (i // n_t, 0, i % n_t))
        v_shape = jax.ShapeDtypeStruct((bsz, FOX_HEADS, n_t, FOX_DIM, TILE), BF16)
        v_spec = pl.BlockSpec((1, FOX_HEADS, 1, FOX_DIM, TILE), lambda i: (i // n_t, 0, i % n_t, 0, 0))
    else:
        q_shape = v_shape = jax.ShapeDtypeStruct((rows, FOX_W), BF16)
        q_spec = v_spec = row_spec(FOX_W)
    return pl.pallas_call(
        functools.partial(_proj_odd_body, attn_layout=attn_layout),
        grid=(rows // tm,),
        in_specs=[row_spec(D_MODEL), _const_spec(w.shape)],
        out_specs=[row_spec(ODD_H), q_spec, row_spec(FOX_W), v_spec],
        out_shape=[jax.ShapeDtypeStruct((rows, ODD_H), F32), q_shape,
                   jax.ShapeDtypeStruct((rows, FOX_W), BF16), v_shape],
        compiler_params=_cparams(1, VMEM_LIMIT),
        name="in_proj_odd",
    )(x, w)


def _outproj_body(x_ref, a1_ref, a2_ref, w_ref, g_ref, b_ref, o_ref):
    k1 = a1_ref.shape[1]
    y = jnp.dot(a1_ref[...], w_ref[:k1, :], preferred_element_type=F32)
    y = y + jnp.dot(a2_ref[...], w_ref[k1:, :], preferred_element_type=F32)
    o_ref[...] = _layer_norm_rows(ALPHA * x_ref[...] + y, g_ref[...], b_ref[...])


def _outproj_ln(x, a1, a2, w, g, b, tm):
    rows = x.shape[0]
    return pl.pallas_call(
        _outproj_body,
        grid=(rows // tm,),
        in_specs=[
            pl.BlockSpec((tm, D_MODEL), lambda i: (i, 0)),
            pl.BlockSpec((tm, a1.shape[1]), lambda i: (i, 0)),
            pl.BlockSpec((tm, a2.shape[1]), lambda i: (i, 0)),
            _const_spec(w.shape),
            _const_spec((1, D_MODEL)),
            _const_spec((1, D_MODEL)),
        ],
        out_specs=pl.BlockSpec((tm, D_MODEL), lambda i: (i, 0)),
        out_shape=jax.ShapeDtypeStruct((rows, D_MODEL), F32),
        compiler_params=_cparams(1, VMEM_LIMIT),
        name="out_proj_ln",
    )(x, a1, a2, w, g, b)


def _s5_body(u_ref, h0r_ref, h0i_ref, ar_ref, ai_ref, asr_ref, asi_ref, bbig_ref, cbig_ref, d_ref, wglu_ref,
             bglu_ref, o_ref, hlr_ref, hli_ref, hs_ref, st_ref, *, first_row, last_seg):
    t = pl.program_id(1)
    rt = u_ref.shape[1]
    seg = rt // SUBLANE

    @pl.when(t == 0)
    def _():
        st_ref[0:1, :] = h0r_ref[0]
        st_ref[1:2, :] = h0i_ref[0]

    u = u_ref[0]
    if first_row:
        rows = t * rt + lax.broadcasted_iota(jnp.int32, (rt, 1), 0)
        u = jnp.where(rows >= first_row, u, 0.0)
    i0 = lax.broadcasted_iota(jnp.int32, (rt, rt), 0)
    i1 = lax.broadcasted_iota(jnp.int32, (rt, rt), 1)
    regroup = (i1 == (i0 & (SUBLANE - 1)) * seg + (i0 >> 3)).astype(BF16)
    restore = (i0 == (i1 & (SUBLANE - 1)) * seg + (i1 >> 3)).astype(BF16)
    ub = jnp.dot(regroup, u.astype(BF16), preferred_element_type=F32).astype(BF16)

    half_w, half_s = S5_WIDTH // 2, S5_HALF // 2
    for kb in range(2):
        for part in range(2):
            c0 = part * S5_HALF + kb * half_s
            hs_ref[:, c0:c0 + half_s] = jnp.dot(
                ub[:, kb * half_w:(kb + 1) * half_w], bbig_ref[kb * half_w:(kb + 1) * half_w, c0:c0 + half_s],
                preferred_element_type=F32)

    ar = jnp.broadcast_to(ar_ref[...], (SUBLANE, S5_HALF))
    ai = jnp.broadcast_to(ai_ref[...], (SUBLANE, S5_HALF))

    def step(k, carry, store):
        hr, hi = carry
        base = pl.multiple_of(k * SUBLANE, SUBLANE)
        blk = hs_ref[pl.ds(base, SUBLANE), :]
        nr = ar * hr - ai * hi + blk[:, :S5_HALF]
        ni = ar * hi + ai * hr + blk[:, S5_HALF:]
        if store:
            hs_ref[pl.ds(base, SUBLANE), :] = jnp.concatenate([nr, ni], axis=1)
        return nr, ni

    zero = jnp.zeros((SUBLANE, S5_HALF), F32)
    er, ei = lax.fori_loop(0, seg, functools.partial(step, store=False), (zero, zero))
    asr, asi = asr_ref[...], asi_ref[...]
    sr, si = st_ref[0:1, :], st_ref[1:2, :]
    start_r, start_i = [], []
    for j in range(SUBLANE):
        start_r.append(sr)
        start_i.append(si)
        sr, si = asr * sr - asi * si + er[j:j + 1], asr * si + asi * sr + ei[j:j + 1]
    st_ref[0:1, :] = sr
    st_ref[1:2, :] = si
    lax.fori_loop(0, seg, functools.partial(step, store=True),
                  (jnp.concatenate(start_r, axis=0), jnp.concatenate(start_i, axis=0)))

    hb = hs_ref[...].astype(BF16)
    ys = []
    for kb in range(2):
        re_rows = slice(kb * half_s, (kb + 1) * half_s)
        im_rows = slice(S5_HALF + kb * half_s, S5_HALF + (kb + 1) * half_s)
        cols = slice(kb * half_w, (kb + 1) * half_w)
        ys.append(jnp.dot(hb[:, re_rows], cbig_ref[re_rows, cols], preferred_element_type=F32)
                  + jnp.dot(hb[:, im_rows], cbig_ref[im_rows, cols], preferred_element_type=F32))
    y = jnp.concatenate(ys, axis=1)
    y_hi = y.astype(BF16)
    y_lo = (y - y_hi.astype(F32)).astype(BF16)
    y = (jnp.dot(restore, y_hi, preferred_element_type=F32) + jnp.dot(restore, y_lo, preferred_element_type=F32)
         + d_ref[...] * u)
    g = jax.nn.gelu(y)
    gate = jnp.dot(g.astype(BF16), wglu_ref[...], preferred_element_type=F32) + bglu_ref[...]
    o_ref[0] = (g * jax.nn.sigmoid(gate)).astype(o_ref.dtype)

    @pl.when(t == pl.num_programs(1) - 1)
    def _():
        hlr_ref[0] = sr if last_seg == SUBLANE else start_r[last_seg]
        hli_ref[0] = si if last_seg == SUBLANE else start_i[last_seg]


def _s5(h, h0r, h0i, p, rt, first_row, n_real):
    bsz, seq, _ = h.shape
    seg = rt // SUBLANE
    real_in_last = n_real - (seq - rt)
    assert 0 < real_in_last <= rt and real_in_last % seg == 0
    as_re = (jnp.exp(p["lam_dt_re"] * seg) * jnp.cos(p["lam_dt_im"] * seg)).reshape(1, S5_HALF)
    as_im = (jnp.exp(p["lam_dt_re"] * seg) * jnp.sin(p["lam_dt_im"] * seg)).reshape(1, S5_HALF)
    state_spec = pl.BlockSpec((1, 1, S5_HALF), lambda b, t: (b, 0, 0))
    return pl.pallas_call(
        functools.partial(_s5_body, first_row=first_row, last_seg=real_in_last // seg),
        grid=(bsz, seq // rt),
        in_specs=[
            pl.BlockSpec((1, rt, S5_WIDTH), lambda b, t: (b, t, 0)),
            state_spec, state_spec,
            _const_spec((1, S5_HALF)), _const_spec((1, S5_HALF)), _const_spec((1, S5_HALF)), _const_spec((1, S5_HALF)),
            _const_spec((S5_WIDTH, 2 * S5_HALF)), _const_spec((2 * S5_HALF, S5_WIDTH)),
            _const_spec((1, S5_WIDTH)), _const_spec((S5_WIDTH, S5_WIDTH)), _const_spec((1, S5_WIDTH)),
        ],
        out_specs=[pl.BlockSpec((1, rt, S5_WIDTH), lambda b, t: (b, t, 0)), state_spec, state_spec],
        out_shape=[jax.ShapeDtypeStruct((bsz, seq, S5_WIDTH), BF16),
                   jax.ShapeDtypeStruct((bsz, 1, S5_HALF), F32),
                   jax.ShapeDtypeStruct((bsz, 1, S5_HALF), F32)],
        scratch_shapes=[pltpu.VMEM((rt, 2 * S5_HALF), F32), pltpu.VMEM((2, S5_HALF), F32)],
        compiler_params=_cparams(2, VMEM_LIMIT),
        name="s5",
    )(h, h0r, h0i, p["a_re"], p["a_im"], as_re, as_im, p["bbig"], p["cbig"], p["d"], p["w_glu"], p["b_glu"])


def _rope_lanes(x, c, s1, s2, half):
    n = x.shape[-1]
    return x * c + pltpu.roll(x, n - half, 1) * s1 + pltpu.roll(x, half, 1) * s2


def _mla_rows_body(h_ref, qn_ref, kn_ref, wq_ref, cq_ref, s1q_ref, s2q_ref, ck_ref, s1k_ref, s2k_ref,
                   q_ref, ckv_ref, kpe_ref, *, attn_layout):
    h = h_ref[0]
    q_lat = h[:, :MLA_Q_RANK]
    q_lat = q_lat * lax.rsqrt(jnp.mean(q_lat * q_lat, axis=-1, keepdims=True) + EPS) * qn_ref[...]
    q = jnp.dot(q_lat.astype(BF16), wq_ref[...], preferred_element_type=F32)
    cq, s1q, s2q = cq_ref[...], s1q_ref[...], s2q_ref[...]
    for hd in range(MLA_HEADS):
        sl = slice(hd * MLA_QK_PAD, (hd + 1) * MLA_QK_PAD)
        q_h = _rope_lanes(q[:, sl], cq, s1q, s2q, MLA_ROPE // 2)
        if attn_layout:
            q_ref[0, sl, :] = q_h.T.astype(BF16)
        else:
            q_ref[0, :, sl] = q_h.astype(BF16)
    c_kv = h[:, MLA_Q_RANK:MLA_Q_RANK + MLA_KV_RANK]
    ckv_ref[0] = c_kv * lax.rsqrt(jnp.mean(c_kv * c_kv, axis=-1, keepdims=True) + EPS) * kn_ref[...]
    k_pe = h[:, MLA_Q_RANK + MLA_KV_RANK:]
    kpe_ref[0] = _rope_lanes(k_pe, ck_ref[...], s1k_ref[...], s2k_ref[...], MLA_ROPE // 2)[:, :MLA_ROPE]


def _mla_rows(h, p, tabs, tm, attn_layout):
    bsz, seq, _ = h.shape
    tab_spec = pl.BlockSpec((tm, LANE), lambda b, t: (t, 0))
    n_q = MLA_HEADS * MLA_QK_PAD
    if attn_layout:
        q_shape, q_spec = (bsz, n_q, seq), pl.BlockSpec((1, n_q, tm), lambda b, t: (b, 0, t))
    else:
        q_shape, q_spec = (bsz, seq, n_q), pl.BlockSpec((1, tm, n_q), lambda b, t: (b, t, 0))
    return pl.pallas_call(
        functools.partial(_mla_rows_body, attn_layout=attn_layout),
        grid=(bsz, seq // tm),
        in_specs=[
            pl.BlockSpec((1, tm, EVEN_IN_PAD - S5_WIDTH), lambda b, t: (b, t, 1)),
            _const_spec((1, MLA_Q_RANK)), _const_spec((1, MLA_KV_RANK)),
            _const_spec((MLA_Q_RANK, n_q)),
        ] + [tab_spec] * 6,
        out_specs=[
            q_spec,
            pl.BlockSpec((1, tm, MLA_KV_RANK), lambda b, t: (b, t, 0)),
            pl.BlockSpec((1, tm, MLA_ROPE), lambda b, t: (b, t, 0)),
        ],
        out_shape=[jax.ShapeDtypeStruct(q_shape, BF16),
                   jax.ShapeDtypeStruct((bsz, seq, MLA_KV_RANK), F32),
                   jax.ShapeDtypeStruct((bsz, seq, MLA_ROPE), F32)],
        compiler_params=_cparams(2, VMEM_LIMIT),
        name="mla_rows",
    )(h, p["q_norm"], p["kv_norm"], p["wq"], *tabs)


def _mla_kv_body(ckv_ref, kpe_ref, wkn_ref, wkp_ref, wv_ref, k_ref, v_ref):
    ckv = ckv_ref[0].astype(BF16)
    k = jnp.dot(ckv, wkn_ref[...], preferred_element_type=F32)
    k = k + jnp.dot(kpe_ref[0].astype(BF16), wkp_ref[...], preferred_element_type=F32)
    k_ref[0] = k.astype(BF16)
    v_t = jnp.dot(ckv, wv_ref[...], preferred_element_type=F32).T
    for hd in range(MLA_HEADS):
        v_ref[0, hd, 0] = v_t[hd * MLA_V:(hd + 1) * MLA_V, :].astype(BF16)


def _mla_kv(ckv, kpe, p):
    bsz, seq, _ = ckv.shape
    n_t = seq // TILE
    return pl.pallas_call(
        _mla_kv_body,
        grid=(bsz, n_t),
        in_specs=[
            pl.BlockSpec((1, TILE, MLA_KV_RANK), lambda b, t: (b, t, 0)),
            pl.BlockSpec((1, TILE, MLA_ROPE), lambda b, t: (b, t, 0)),
            _const_spec(p["wkn"].shape), _const_spec(p["wkp"].shape), _const_spec(p["wv"].shape),
        ],
        out_specs=[pl.BlockSpec((1, TILE, MLA_HEADS * MLA_QK_PAD), lambda b, t: (b, t, 0)),
                   pl.BlockSpec((1, MLA_HEADS, 1, MLA_V, TILE), lambda b, t: (b, 0, t, 0, 0))],
        out_shape=[jax.ShapeDtypeStruct((bsz, seq, MLA_HEADS * MLA_QK_PAD), BF16),
                   jax.ShapeDtypeStruct((bsz, MLA_HEADS, n_t, MLA_V, TILE), BF16)],
        compiler_params=_cparams(2, VMEM_LIMIT),
        name="mla_kv",
    )(ckv, kpe, p["wkn"], p["wkp"], p["wv"])


def _flash_body(*refs, heads, dq, dv, q_off, causal, rows_out):
    if causal:
        qt_ref, k_ref, vt_ref, fq_ref, fk_ref, o_ref = refs
    else:
        qt_ref, k_ref, vt_ref, o_ref = refs
    tq = qt_ref.shape[2]
    n_tiles, tk = vt_ref.shape[2], vt_ref.shape[4]
    r0 = q_off + pl.program_id(1) * tq
    j_last = jnp.minimum((r0 + tq - 1) // tk, n_tiles - 1)
    q_row = r0 + lax.broadcasted_iota(jnp.int32, (1, tq), 1)
    last = q_row if causal else ((q_row >> CHUNK_SHIFT) << CHUNK_SHIFT) + (CHUNK - 1)
    q_t = [qt_ref[0, hd * dq:(hd + 1) * dq, :] for hd in range(heads)]

    def tile(j, carry):
        start = pl.multiple_of(j * tk, tk)
        k_row = start + lax.broadcasted_iota(jnp.int32, (tk, 1), 0)
        ok = (k_row <= last) & (k_row >= PAD)
        scores = [jnp.dot(k_ref[0, pl.ds(start, tk), hd * dq:(hd + 1) * dq], q_t[hd],
                          preferred_element_type=F32) for hd in range(heads)]
        probs = []
        for hd in range(heads):
            m, l, _ = carry[hd]
            s = scores[hd]
            if causal:
                s = s + (fq_ref[0, hd:hd + 1, :] - fk_ref[0, pl.ds(start, tk), hd:hd + 1])
            s = jnp.where(ok, s, NEG)
            m_new = jnp.maximum(m, jnp.max(s, axis=0, keepdims=True))
            a = jnp.exp(m - m_new)
            p = jnp.exp(s - m_new)
            probs.append((m_new, a * l + jnp.sum(p, axis=0, keepdims=True), a, p.astype(BF16)))
        new = []
        for hd in range(heads):
            m_new, l, a, p = probs[hd]
            acc = a * carry[hd][2] + jnp.dot(vt_ref[0, hd, j], p, preferred_element_type=F32)
            new.append((m_new, l, acc))
        return tuple(new)

    init = tuple((jnp.full((1, tq), NEG, F32), jnp.zeros((1, tq), F32), jnp.zeros((dv, tq), F32))
                 for _ in range(heads))
    carry = lax.fori_loop(0, j_last + 1, tile, init)
    outs = [carry[hd][2] / carry[hd][1] for hd in range(heads)]
    if rows_out:
        per = LANE // dv
        for g in range(heads // per):
            o_ref[0, :, g * LANE:(g + 1) * LANE] = jnp.concatenate(
                outs[g * per:(g + 1) * per], axis=0).T.astype(o_ref.dtype)
    else:
        for hd in range(heads):
            o_ref[0, hd * dv:(hd + 1) * dv, :] = outs[hd].astype(o_ref.dtype)


def _flash(q_t, k, v_t, fq_t, fk, *, heads, dq, dv, q_off, causal):
    bsz, _, n_q = q_t.shape
    n_keys = k.shape[1]
    rows_out = n_q % TILE == 0
    tq = TILE if rows_out else n_q
    in_specs = [
        pl.BlockSpec((1, heads * dq, tq), lambda b, i: (b, 0, i)),
        pl.BlockSpec((1, n_keys, heads * dq), lambda b, i: (b, 0, 0)),
        pl.BlockSpec((1,) + v_t.shape[1:], lambda b, i: (b, 0, 0, 0, 0)),
    ]
    args = [q_t, k, v_t]
    if causal:
        in_specs += [pl.BlockSpec((1, heads, tq), lambda b, i: (b, 0, i)),
                     pl.BlockSpec((1, n_keys, heads), lambda b, i: (b, 0, 0))]
        args += [fq_t, fk]
    if rows_out:
        out_shape, out_spec = (bsz, n_q, heads * dv), pl.BlockSpec((1, tq, heads * dv), lambda b, i: (b, i, 0))
    else:
        out_shape, out_spec = (bsz, heads * dv, n_q), pl.BlockSpec((1, heads * dv, tq), lambda b, i: (b, 0, i))
    return pl.pallas_call(
        functools.partial(_flash_body, heads=heads, dq=dq, dv=dv, q_off=q_off, causal=causal, rows_out=rows_out),
        grid=(bsz, n_q // tq),
        in_specs=in_specs,
        out_specs=out_spec,
        out_shape=jax.ShapeDtypeStruct(out_shape, BF16),
        compiler_params=_cparams(2, VMEM_LIMIT),
        name="flash_causal" if causal else "flash_chunk",
    )(*args)


def _gate_body(x_ref, b_ref, lf_ref, fc_ref, *, new_start):
    n_rows = x_ref.shape[1]
    tri = (lax.broadcasted_iota(jnp.int32, (TILE, TILE), 0)
           >= lax.broadcasted_iota(jnp.int32, (TILE, TILE), 1)).astype(F32)
    carry = jnp.zeros((1, x_ref.shape[2]), F32)
    for i in range(n_rows // TILE):
        sl = slice(i * TILE, (i + 1) * TILE)
        x = x_ref[0, sl, :]
        rows = i * TILE + lax.broadcasted_iota(jnp.int32, (TILE, 1), 0)
        z = x + b_ref[...]
        lf = jnp.where(rows >= new_start, jnp.minimum(z, 0.0) - jnp.log1p(jnp.exp(-jnp.abs(z))), x)
        lf = jnp.where(rows >= PAD, lf, 0.0)
        lf_ref[0, sl, :] = lf
        cs = jnp.dot(tri, lf, preferred_element_type=F32, precision=lax.Precision.HIGHEST) + carry
        fc_ref[0, sl, :] = cs
        carry = cs[TILE - 1:TILE, :]


def _gate(x, b_f, new_start):
    bsz, n_rows, heads = x.shape
    spec = pl.BlockSpec((1, n_rows, heads), lambda b: (b, 0, 0))
    return pl.pallas_call(
        functools.partial(_gate_body, new_start=new_start),
        grid=(bsz,),
        in_specs=[spec, _const_spec((1, heads))],
        out_specs=[spec, spec],
        out_shape=[jax.ShapeDtypeStruct(x.shape, F32)] * 2,
        compiler_params=_cparams(1),
        name="fox_gate",
    )(x, b_f)


RET_LOG_GAMMA = tuple(math.log(1.0 - 2.0 ** (-5.0 - h)) for h in range(RET_HEADS))


def _ret_body(rq_ref, rk_ref, rv_ref, rg_ref, s0_ref, cq_ref, s1q_ref, s2q_ref, ck_ref, s1k_ref, s2k_ref,
              o_ref, sl_ref, st_ref, *, n_tail):
    c = pl.program_id(1)
    ct = rq_ref.shape[1]

    @pl.when(c == 0)
    def _():
        st_ref[...] = s0_ref[0]

    q = _rope_lanes(rq_ref[0], cq_ref[...], s1q_ref[...], s2q_ref[...], RET_DK // 2)
    k = _rope_lanes(rk_ref[0], ck_ref[...], s1k_ref[...], s2k_ref[...], RET_DK // 2)
    v = rv_ref[0].astype(BF16)
    g = rg_ref[0]
    j = lax.broadcasted_iota(jnp.int32, (ct, 1), 0).astype(F32)
    diff = (lax.broadcasted_iota(jnp.int32, (ct, ct), 0)
            - lax.broadcasted_iota(jnp.int32, (ct, ct), 1)).astype(F32)
    for hd in range(RET_HEADS):
        lg = RET_LOG_GAMMA[hd]
        q_h = q[:, hd * RET_DK:(hd + 1) * RET_DK]
        k_h = k[:, hd * RET_DK:(hd + 1) * RET_DK]
        v_h = v[:, hd * RET_DV:(hd + 1) * RET_DV]
        decay = jnp.where(diff >= 0.0, jnp.exp(lg * jnp.maximum(diff, 0.0)), 0.0)
        scores = lax.dot_general(q_h.astype(BF16), k_h.astype(BF16), (((1,), (1,)), ((), ())),
                                 preferred_element_type=F32) * decay
        s_h = st_ref[hd]
        out = jnp.dot(scores.astype(BF16), v_h, preferred_element_type=F32)
        out = out + jnp.dot((q_h * jnp.exp(lg * (j + 1.0))).astype(BF16), s_h.astype(BF16),
                            preferred_element_type=F32)
        k_dec = (k_h * jnp.exp(lg * (ct - 1.0 - j))).astype(BF16)
        st_ref[hd] = math.exp(lg * ct) * s_h + lax.dot_general(
            k_dec, v_h, (((0,), (0,)), ((), ())), preferred_element_type=F32)
        mu = jnp.mean(out, axis=-1, keepdims=True)
        oc = out - mu
        var = jnp.mean(oc * oc, axis=-1, keepdims=True)
        g_h = g[:, hd * RET_DV:(hd + 1) * RET_DV]
        o_ref[0, :, hd * RET_DV:(hd + 1) * RET_DV] = (
            g_h * jax.nn.sigmoid(g_h) * (oc * lax.rsqrt(var + EPS))).astype(o_ref.dtype)

    @pl.when(c == pl.num_programs(1) - 1)
    def _():
        for hd in range(RET_HEADS):
            sl_ref[0, hd] = st_ref[hd] * math.exp(-RET_LOG_GAMMA[hd] * n_tail)


def _retention(h, s0, tabs, ct, n_tail):
    bsz, seq, _ = h.shape
    tab_spec = pl.BlockSpec((ct, RET_QK), lambda b, c: (c, 0))
    st_spec = pl.BlockSpec((1, RET_HEADS, RET_DK, RET_DV), lambda b, c: (b, 0, 0, 0))
    rq_blk = (2 * FOX_W) // RET_QK
    rv_blk = (2 * FOX_W + 2 * RET_QK) // RET_VW
    return pl.pallas_call(
        functools.partial(_ret_body, n_tail=n_tail),
        grid=(bsz, seq // ct),
        in_specs=[
            pl.BlockSpec((1, ct, RET_QK), lambda b, c: (b, c, rq_blk)),
            pl.BlockSpec((1, ct, RET_QK), lambda b, c: (b, c, rq_blk + 1)),
            pl.BlockSpec((1, ct, RET_VW), lambda b, c: (b, c, rv_blk)),
            pl.BlockSpec((1, ct, RET_VW), lambda b, c: (b, c, rv_blk + 1)),
            st_spec,
        ] + [tab_spec] * 6,
        out_specs=[pl.BlockSpec((1, ct, RET_VW), lambda b, c: (b, c, 0)), st_spec],
        out_shape=[jax.ShapeDtypeStruct((bsz, seq, RET_VW), BF16),
                   jax.ShapeDtypeStruct((bsz, RET_HEADS, RET_DK, RET_DV), F32)],
        scratch_shapes=[pltpu.VMEM((RET_HEADS, RET_DK, RET_DV), F32)],
        compiler_params=_cparams(2, VMEM_LIMIT),
        name="retention",
    )(h, h, h, h, s0, *tabs)


def _rope_tables(pos, half, width, group, offset, scale=1.0, valid=None):
    inv = ROPE_BASE ** (-jnp.arange(half, dtype=F32) / half)
    ang = pos.astype(F32)[:, None] * inv[None, :]
    cos, sin = jnp.cos(ang), jnp.sin(ang)
    n = pos.shape[0]
    one, zero = jnp.ones((n, 1), F32), jnp.zeros((n, 1), F32)

    def lanes(first, second, other):
        grp = jnp.concatenate([jnp.broadcast_to(other, (n, offset)), first, second,
                               jnp.broadcast_to(other, (n, group - offset - 2 * half))], axis=1)
        return jnp.tile(grp, (1, width // group))

    tabs = (lanes(cos, cos, one), lanes(-sin, 0.0 * sin, zero), lanes(0.0 * sin, sin, zero))
    if valid is not None:
        tabs = tuple(jnp.where(valid[:, None], t, 0.0) for t in tabs)
    return tuple(t * scale for t in tabs)


def _even_params(e, w):
    p = {}
    p["w_in"] = jnp.pad(w["even_w_in"][e], ((0, 0), (0, EVEN_IN_PAD - EVEN_IN))).astype(BF16)
    p["w_out"] = w["even_w_out"][e].astype(BF16)
    lam_re, lam_im = w["s5_a_re"][e].astype(F32), w["s5_a_im"][e].astype(F32)
    dt = jnp.exp(w["s5_log_dt"][e].astype(F32))[:, None]
    p["lam_dt_re"], p["lam_dt_im"] = lam_re * dt, lam_im * dt
    mag = jnp.exp(lam_re * dt)
    abar_re, abar_im = mag * jnp.cos(lam_im * dt), mag * jnp.sin(lam_im * dt)
    den = lam_re * lam_re + lam_im * lam_im
    f_re = ((abar_re - 1.0) * lam_re + abar_im * lam_im) / den
    f_im = (abar_im * lam_re - (abar_re - 1.0) * lam_im) / den
    b_re, b_im = w["s5_b_re"][e].astype(F32), w["s5_b_im"][e].astype(F32)
    bb_re = f_re[..., None] * b_re - f_im[..., None] * b_im
    bb_im = f_re[..., None] * b_im + f_im[..., None] * b_re
    eye = jnp.eye(S5_GROUPS, dtype=F32)

    def in_blocks(x):
        return jnp.einsum("gnc,gh->gchn", x, eye).reshape(S5_WIDTH, S5_HALF)

    def out_blocks(x):
        return jnp.einsum("gcn,gh->gnhc", x, eye).reshape(S5_HALF, S5_WIDTH)

    p["bbig"] = jnp.concatenate([in_blocks(bb_re), in_blocks(bb_im)], axis=1).astype(BF16)
    p["cbig"] = jnp.concatenate([out_blocks(w["s5_c_re"][e].astype(F32)),
                                 out_blocks(-w["s5_c_im"][e].astype(F32))], axis=0).astype(BF16)
    p["a_re"] = abar_re.reshape(1, S5_HALF)
    p["a_im"] = abar_im.reshape(1, S5_HALF)
    p["d"] = w["s5_d"][e].astype(F32).reshape(1, S5_WIDTH)
    p["w_glu"] = w["s5_w_glu"][e].astype(BF16)
    p["b_glu"] = w["s5_b_glu"][e].astype(F32).reshape(1, S5_WIDTH)
    p["q_norm"] = w["mla_q_norm"][e].astype(F32).reshape(1, MLA_Q_RANK)
    p["kv_norm"] = w["mla_kv_norm"][e].astype(F32).reshape(1, MLA_KV_RANK)
    wq = w["mla_w_uq"][e].reshape(MLA_Q_RANK, MLA_HEADS, MLA_NOPE + MLA_ROPE)
    wq = jnp.pad(wq, ((0, 0), (0, 0), (0, MLA_QK_PAD - MLA_NOPE - MLA_ROPE)))
    p["wq"] = wq.reshape(MLA_Q_RANK, MLA_HEADS * MLA_QK_PAD).astype(BF16)
    wkv = w["mla_w_ukv"][e].reshape(MLA_KV_RANK, MLA_HEADS, MLA_NOPE + MLA_V)
    wkn = jnp.pad(wkv[:, :, :MLA_NOPE], ((0, 0), (0, 0), (0, MLA_QK_PAD - MLA_NOPE)))
    p["wkn"] = wkn.reshape(MLA_KV_RANK, MLA_HEADS * MLA_QK_PAD).astype(BF16)
    place = jnp.pad(jnp.eye(MLA_ROPE, dtype=F32), ((0, 0), (MLA_NOPE, MLA_QK_PAD - MLA_NOPE - MLA_ROPE)))
    p["wkp"] = jnp.tile(place, (1, MLA_HEADS)).astype(BF16)
    p["wv"] = wkv[:, :, MLA_NOPE:].reshape(MLA_KV_RANK, MLA_HEADS * MLA_V).astype(BF16)
    return p


def _odd_params(o, w):
    w_in = w["odd_w_in"][o]
    c_logit = 3 * FOX_W
    cols = jnp.concatenate([
        w_in[:, :FOX_W] * (FOX_DIM ** -0.5),
        w_in[:, FOX_W:c_logit],
        w_in[:, c_logit + FOX_HEADS:],
        w_in[:, c_logit:c_logit + FOX_HEADS],
    ], axis=1)
    p = {"w_in": jnp.pad(cols, ((0, 0), (0, ODD_IN_PAD - cols.shape[1]))).astype(BF16)}
    p["w_out"] = w["odd_w_out"][o].astype(BF16)
    p["b_f"] = w["fox_b_f"][o].astype(F32).reshape(1, FOX_HEADS)
    return p


def _value_tiles(v, heads, dv):
    bsz, n_keys, _ = v.shape
    return v.reshape(bsz, n_keys // TILE, TILE, heads, dv).transpose(0, 3, 1, 4, 2)


def _trunk(x, pos, n_real, past, w, ffn, evens, odds):
    bsz, seq, _ = x.shape
    rows = bsz * seq
    prompt = past is None
    first_row = PAD if prompt else 0
    if prompt:
        n_keys, q_off = seq, 0
        tm = 2 * TILE if rows % (2 * TILE) == 0 else TILE
        rt = TILE
    else:
        n_past = past["cache_fox_k"].shape[2]
        q_off = PAD + n_past
        n_keys = _round_up(q_off + seq, TILE)
        tm, rt = rows, seq
    idx = jnp.arange(seq, dtype=jnp.int32)
    valid = (idx >= first_row) & (idx < n_real)
    st = {n: [] for n in ("mla_ckv", "mla_kpe", "s5_re", "s5_im", "fox_k", "fox_v", "fox_logf", "ret")}
    mla_scale = (MLA_NOPE + MLA_ROPE) ** -0.5
    q_tabs = _rope_tables(pos, MLA_ROPE // 2, MLA_QK_PAD, MLA_QK_PAD, MLA_NOPE, scale=mla_scale)
    k_tabs = _rope_tables(pos, MLA_ROPE // 2, LANE, LANE, 0)
    rq_tabs = _rope_tables(pos, RET_DK // 2, RET_QK, RET_DK, 0)
    rk_tabs = _rope_tables(pos, RET_DK // 2, RET_QK, RET_DK, 0, scale=RET_DK ** -0.5, valid=valid)
    x2 = x.reshape(rows, D_MODEL)

    def ln(l, i):
        return w["ln_g"][l, i].reshape(1, D_MODEL), w["ln_b"][l, i].reshape(1, D_MODEL)

    def key_rows(cached, new):
        a = jnp.concatenate([cached.astype(new.dtype), new], axis=1)
        return jnp.pad(a, ((0, 0), (PAD, n_keys - PAD - a.shape[1]), (0, 0)))

    for l in range(DEPTH):
        x2 = _ffn_ln(x2, *ffn[l][0], *ln(l, 0), tm)
        if l % 2 == 0:
            e = l // 2
            p = evens[e]
            h = _proj(x2, p["w_in"], tm).reshape(bsz, seq, EVEN_IN_PAD)
            if prompt:
                h0r = h0i = jnp.zeros((bsz, 1, S5_HALF), F32)
            else:
                h0r = past["state_s5_re"][e].astype(F32).reshape(bsz, 1, S5_HALF)
                h0i = past["state_s5_im"][e].astype(F32).reshape(bsz, 1, S5_HALF)
            s5_out, hlr, hli = _s5(h, h0r, h0i, p, rt, first_row, n_real)
            q, ckv, kpe = _mla_rows(h, p, q_tabs + k_tabs, rt, attn_layout=prompt)
            if prompt:
                k_att, v_att = _mla_kv(ckv, kpe, p)
                mla_out = _flash(q, k_att, v_att, None, None, heads=MLA_HEADS, dq=MLA_QK_PAD, dv=MLA_V,
                                 q_off=q_off, causal=False)
            else:
                k_att, v_att = _mla_kv(key_rows(past["cache_mla_ckv"][e], ckv),
                                       key_rows(past["cache_mla_kpe"][e], kpe), p)
                mla_out = _flash(q.transpose(0, 2, 1), k_att, v_att, None, None, heads=MLA_HEADS, dq=MLA_QK_PAD,
                                 dv=MLA_V, q_off=q_off, causal=False).transpose(0, 2, 1)
            x2 = _outproj_ln(x2, s5_out.reshape(rows, S5_WIDTH), mla_out.reshape(rows, MLA_HEADS * MLA_V),
                             p["w_out"], *ln(l, 1), tm)
            st["mla_ckv"].append(ckv)
            st["mla_kpe"].append(kpe)
            st["s5_re"].append(hlr.reshape(bsz, S5_GROUPS, S5_STATE))
            st["s5_im"].append(hli.reshape(bsz, S5_GROUPS, S5_STATE))
        else:
            o = l // 2
            p = odds[o]
            h, fq, fk16, fv = _proj_odd(x2, p["w_in"], bsz, TILE if prompt else tm, attn_layout=prompt)
            h = h.reshape(bsz, seq, ODD_H)
            fk16 = fk16.reshape(bsz, seq, FOX_W)
            f_logit = h[:, :, ODD_LOGIT_COL:ODD_LOGIT_COL + FOX_HEADS]
            if prompt:
                logf, fcum = _gate(f_logit, p["b_f"], 0)
                fox_out = _flash(fq, fk16, fv, fcum.transpose(0, 2, 1), fcum, heads=FOX_HEADS, dq=FOX_DIM,
                                 dv=FOX_DIM, q_off=q_off, causal=True)
                s0 = jnp.zeros((bsz, RET_HEADS, RET_DK, RET_DV), F32)
            else:
                logf, fcum = _gate(key_rows(past["cache_fox_logf"][o].astype(F32), f_logit), p["b_f"], q_off)
                k_all = key_rows(past["cache_fox_k"][o].reshape(bsz, n_past, FOX_W), fk16)
                v_all = key_rows(past["cache_fox_v"][o].reshape(bsz, n_past, FOX_W), fv.reshape(bsz, seq, FOX_W))
                fox_out = _flash(fq.reshape(bsz, seq, FOX_W).transpose(0, 2, 1), k_all,
                                 _value_tiles(v_all, FOX_HEADS, FOX_DIM),
                                 fcum[:, q_off:q_off + seq].transpose(0, 2, 1), fcum, heads=FOX_HEADS, dq=FOX_DIM,
                                 dv=FOX_DIM, q_off=q_off, causal=True).transpose(0, 2, 1)
                s0 = past["state_ret"][o].astype(F32)
            ret_out, s_last = _retention(h, s0, rq_tabs + rk_tabs, rt, seq - n_real)
            x2 = _outproj_ln(x2, fox_out.reshape(rows, FOX_W), ret_out.reshape(rows, RET_VW),
                             p["w_out"], *ln(l, 1), tm)
            st["fox_k"].append(h[:, :, :FOX_W].reshape(bsz, seq, FOX_HEADS, FOX_DIM))
            st["fox_v"].append(h[:, :, FOX_W:2 * FOX_W].reshape(bsz, seq, FOX_HEADS, FOX_DIM))
            st["fox_logf"].append(logf[:, q_off:q_off + seq])
            st["ret"].append(s_last)
        x2 = _ffn_ln(x2, *ffn[l][1], *ln(l, 2), tm)
    return x2.reshape(bsz, seq, D_MODEL), {n: jnp.stack(a) for n, a in st.items()}


def kernel(x_prompt, x_sample, cache_mla_ckv, cache_mla_kpe, cache_fox_k, cache_fox_v, cache_fox_logf,
           state_s5_re, state_s5_im, state_ret, meta_tokens, ln_g, ln_b, ffn_w_gate, ffn_w_up, ffn_w_down,
           even_w_in, even_w_out, s5_a_re, s5_a_im, s5_b_re, s5_b_im, s5_c_re, s5_c_im, s5_d, s5_log_dt,
           s5_w_glu, s5_b_glu, mla_q_norm, mla_kv_norm, mla_w_uq, mla_w_ukv, odd_w_in, odd_w_out, fox_b_f):
    w = dict(ln_g=ln_g.astype(F32), ln_b=ln_b.astype(F32), even_w_in=even_w_in, even_w_out=even_w_out,
             s5_a_re=s5_a_re, s5_a_im=s5_a_im, s5_b_re=s5_b_re, s5_b_im=s5_b_im, s5_c_re=s5_c_re,
             s5_c_im=s5_c_im, s5_d=s5_d, s5_log_dt=s5_log_dt, s5_w_glu=s5_w_glu, s5_b_glu=s5_b_glu,
             mla_q_norm=mla_q_norm, mla_kv_norm=mla_kv_norm, mla_w_uq=mla_w_uq, mla_w_ukv=mla_w_ukv,
             odd_w_in=odd_w_in, odd_w_out=odd_w_out, fox_b_f=fox_b_f)
    past = dict(cache_mla_ckv=cache_mla_ckv, cache_mla_kpe=cache_mla_kpe, cache_fox_k=cache_fox_k,
                cache_fox_v=cache_fox_v, cache_fox_logf=cache_fox_logf, state_s5_re=state_s5_re,
                state_s5_im=state_s5_im, state_ret=state_ret)
    ffn = [[(ffn_w_gate[l, i].astype(BF16), ffn_w_up[l, i].astype(BF16), ffn_w_down[l, i].astype(BF16))
            for i in range(2)] for l in range(DEPTH)]
    evens = [_even_params(e, w) for e in range((DEPTH + 1) // 2)]
    odds = [_odd_params(o, w) for o in range(DEPTH // 2)]

    bsz, seq, _ = x_prompt.shape
    n_real = PAD + N_META + seq
    n_rows = _round_up(n_real, TILE)
    meta = jnp.broadcast_to(meta_tokens[None].astype(x_prompt.dtype), (bsz, N_META, D_MODEL))
    xp = jnp.concatenate([jnp.zeros((bsz, PAD, D_MODEL), x_prompt.dtype), meta, x_prompt,
                          jnp.zeros((bsz, n_rows - n_real, D_MODEL), x_prompt.dtype)], axis=1)
    pos_p = jnp.maximum(jnp.arange(n_rows, dtype=jnp.int32) - PAD, 0)
    y_p, st_p = _trunk(xp, pos_p, n_real, None, w, ffn, evens, odds)
    d_seq = x_sample.shape[1]
    pos_s = N_META + cache_fox_k.shape[2] + jnp.arange(d_seq, dtype=jnp.int32)
    y_s, st_s = _trunk(x_sample, pos_s, d_seq, past, w, ffn, evens, odds)

    def real(a):
        return a[:, :, PAD:n_real]

    return (y_p[:, PAD + N_META:n_real], y_s,
            real(st_p["mla_ckv"]), real(st_p["mla_kpe"]), real(st_p["fox_k"]), real(st_p["fox_v"]),
            real(st_p["fox_logf"]), st_p["s5_re"], st_p["s5_im"], st_p["ret"],
            st_s["mla_ckv"], st_s["mla_kpe"], st_s["fox_k"], st_s["fox_v"], st_s["fox_logf"],
            st_s["s5_re"], st_s["s5_im"], st_s["ret"])
```

```python
import functools
import math

import jax
import jax.numpy as jnp
from jax import lax
from jax.experimental import pallas as pl
from jax.experimental.pallas import tpu as pltpu

F32 = jnp.float32
BF16 = jnp.bfloat16

D_MODEL = 1024
DEPTH = 4
CHUNK = 64
CHUNK_SHIFT = 6
N_META = 16
S5_WIDTH = 512
S5_CH = 16
S5_GROUPS = S5_WIDTH // S5_CH
S5_STATE = 64
S5_HALF = S5_GROUPS * S5_STATE
MLA_HEADS = 8
MLA_Q_RANK = 256
MLA_KV_RANK = 128
MLA_NOPE = 64
MLA_ROPE = 32
MLA_V = 64
MLA_QK_PAD = 128
FOX_HEADS = 8
FOX_DIM = 64
FOX_W = FOX_HEADS * FOX_DIM
RET_HEADS = 4
RET_DK = 64
RET_DV = 128
RET_QK = RET_HEADS * RET_DK
RET_VW = RET_HEADS * RET_DV
D_FF = 2816
ROPE_BASE = 10000.0
ALPHA = (2.0 * DEPTH) ** 0.25
EPS = 1e-5
NEG = -1e30
LOG2E = math.log2(math.e)
EVEN_IN = S5_WIDTH + MLA_Q_RANK + MLA_KV_RANK + MLA_ROPE
EVEN_IN_PAD = 1024
ODD_IN_PAD = 3200
ODD_H = ODD_IN_PAD - FOX_W
ODD_LOGIT_COL = 2 * FOX_W + 2 * RET_QK + 2 * RET_VW

PAD = CHUNK - N_META
LANE = 128
SUBLANE = 8
TILE = 256
VMEM_LIMIT = 56 * 1024 * 1024


def _cparams(n_grid, vmem=None):
    return pltpu.CompilerParams(dimension_semantics=("arbitrary",) * n_grid, vmem_limit_bytes=vmem)


def _const_spec(shape):
    nd = len(shape)
    return pl.BlockSpec(shape, lambda *_: (0,) * nd)


def _round_up(n, m):
    return -(-n // m) * m


def _layer_norm_rows(z, g, b):
    mu = jnp.mean(z, axis=-1, keepdims=True)
    zc = z - mu
    var = jnp.mean(zc * zc, axis=-1, keepdims=True)
    return zc * lax.rsqrt(var + EPS) * g + b


def _ffn_body(x_ref, wg_ref, wu_ref, wd_ref, g_ref, b_ref, o_ref, hid_ref, *, fc):
    x = x_ref[...]
    xb = x.astype(BF16)
    for c in range(D_FF // fc):
        sl = slice(c * fc, (c + 1) * fc)
        hg = jnp.dot(xb, wg_ref[:, sl], preferred_element_type=F32)
        hu = jnp.dot(xb, wu_ref[:, sl], preferred_element_type=F32)
        hid_ref[:, sl] = (hg * jax.nn.sigmoid(hg) * hu).astype(BF16)
    y = jnp.dot(hid_ref[...], wd_ref[...], preferred_element_type=F32)
    o_ref[...] = _layer_norm_rows(ALPHA * x + 0.5 * y, g_ref[...], b_ref[...])


def _ffn_ln(x, wg, wu, wd, g, b, tm):
    rows = x.shape[0]
    return pl.pallas_call(
        functools.partial(_ffn_body, fc=TILE),
        grid=(rows // tm,),
        in_specs=[
            pl.BlockSpec((tm, D_MODEL), lambda i: (i, 0)),
            pl.BlockSpec((D_MODEL, D_FF), lambda i: (0, 0), pipeline_mode=pl.Buffered(1)),
            pl.BlockSpec((D_MODEL, D_FF), lambda i: (0, 0), pipeline_mode=pl.Buffered(1)),
            pl.BlockSpec((D_FF, D_MODEL), lambda i: (0, 0), pipeline_mode=pl.Buffered(1)),
            _const_spec((1, D_MODEL)),
            _const_spec((1, D_MODEL)),
        ],
        out_specs=pl.BlockSpec((tm, D_MODEL), lambda i: (i, 0)),
        out_shape=jax.ShapeDtypeStruct((rows, D_MODEL), F32),
        scratch_shapes=[pltpu.VMEM((tm, D_FF), BF16)],
        compiler_params=_cparams(1, VMEM_LIMIT),
        name="ffn_ln",
    )(x, wg, wu, wd, g, b)


def _proj_body(x_ref, w_ref, o_ref):
    o_ref[...] = jnp.dot(x_ref[...].astype(BF16), w_ref[...], preferred_element_type=F32)


def _proj(x, w, tm):
    rows, n = x.shape[0], w.shape[1]
    return pl.pallas_call(
        _proj_body,
        grid=(rows // tm,),
        in_specs=[pl.BlockSpec((tm, D_MODEL), lambda i: (i, 0)), _const_spec(w.shape)],
        out_specs=pl.BlockSpec((tm, n), lambda i: (i, 0)),
        out_shape=jax.ShapeDtypeStruct((rows, n), F32),
        compiler_params=_cparams(1, VMEM_LIMIT),
        name="in_proj_even",
    )(x, w)


def _proj_odd_body(x_ref, w_ref, h_ref, q_ref, k_ref, v_ref, *, attn_layout):
    y = jnp.dot(x_ref[...].astype(BF16), w_ref[...], preferred_element_type=F32)
    h_ref[...] = y[:, FOX_W:]
    k_ref[...] = y[:, FOX_W:2 * FOX_W].astype(BF16)
    q = y[:, :FOX_W] * LOG2E
    if attn_layout:
        q_ref[0] = q.T.astype(BF16)
        v_t = y[:, 2 * FOX_W:3 * FOX_W].T
        for hd in range(FOX_HEADS):
            v_ref[0, hd, 0] = v_t[hd * FOX_DIM:(hd + 1) * FOX_DIM, :].astype(BF16)
    else:
        q_ref[...] = q.astype(BF16)
        v_ref[...] = y[:, 2 * FOX_W:3 * FOX_W].astype(BF16)


def _proj_odd(x, w, bsz, tm, attn_layout):
    rows = x.shape[0]
    seq = rows // bsz
    row_spec = lambda n: pl.BlockSpec((tm, n), lambda i: (i, 0))
    if attn_layout:
        assert tm == TILE and seq % TILE == 0
        n_t = seq // TILE
        q_shape = jax.ShapeDtypeStruct((bsz, FOX_W, seq), BF16)
        q_spec = pl.BlockSpec((1, FOX_W, TILE), lambda i: (i // n_t, 0, i % n_t))
        v_shape = jax.ShapeDtypeStruct((bsz, FOX_HEADS, n_t, FOX_DIM, TILE), BF16)
        v_spec = pl.BlockSpec((1, FOX_HEADS, 1, FOX_DIM, TILE), lambda i: (i // n_t, 0, i % n_t, 0, 0))
    else:
        q_shape = v_shape = jax.ShapeDtypeStruct((rows, FOX_W), BF16)
        q_spec = v_spec = row_spec(FOX_W)
    return pl.pallas_call(
        functools.partial(_proj_odd_body, attn_layout=attn_layout),
        grid=(rows // tm,),
        in_specs=[row_spec(D_MODEL), _const_spec(w.shape)],
        out_specs=[row_spec(ODD_H), q_spec, row_spec(FOX_W), v_spec],
        out_shape=[jax.ShapeDtypeStruct((rows, ODD_H), F32), q_shape,
                   jax.ShapeDtypeStruct((rows, FOX_W), BF16), v_shape],
        compiler_params=_cparams(1, VMEM_LIMIT),
        name="in_proj_odd",
    )(x, w)


def _outproj_body(x_ref, a1_ref, a2_ref, w_ref, g_ref, b_ref, o_ref):
    k1 = a1_ref.shape[1]
    y = jnp.dot(a1_ref[...], w_ref[:k1, :], preferred_element_type=F32)
    y = y + jnp.dot(a2_ref[...], w_ref[k1:, :], preferred_element_type=F32)
    o_ref[...] = _layer_norm_rows(ALPHA * x_ref[...] + y, g_ref[...], b_ref[...])


def _outproj_ln(x, a1, a2, w, g, b, tm):
    rows = x.shape[0]
    return pl.pallas_call(
        _outproj_body,
        grid=(rows // tm,),
        in_specs=[
            pl.BlockSpec((tm, D_MODEL), lambda i: (i, 0)),
            pl.BlockSpec((tm, a1.shape[1]), lambda i: (i, 0)),
            pl.BlockSpec((tm, a2.shape[1]), lambda i: (i, 0)),
            _const_spec(w.shape),
            _const_spec((1, D_MODEL)),
            _const_spec((1, D_MODEL)),
        ],
        out_specs=pl.BlockSpec((tm, D_MODEL), lambda i: (i, 0)),
        out_shape=jax.ShapeDtypeStruct((rows, D_MODEL), F32),
        compiler_params=_cparams(1, VMEM_LIMIT),
        name="out_proj_ln",
    )(x, a1, a2, w, g, b)


def _s5_body(u_ref, h0r_ref, h0i_ref, ar_ref, ai_ref, asr_ref, asi_ref, bbig_ref, cbig_ref, d_ref, wglu_ref,
             bglu_ref, o_ref, hlr_ref, hli_ref, hs_ref, st_ref, *, first_row, last_seg):
    t = pl.program_id(1)
    rt = u_ref.shape[1]
    seg = rt // SUBLANE

    @pl.when(t == 0)
    def _():
        st_ref[0:1, :] = h0r_ref[0]
        st_ref[1:2, :] = h0i_ref[0]

    u = u_ref[0]
    if first_row:
        rows = t * rt + lax.broadcasted_iota(jnp.int32, (rt, 1), 0)
        u = jnp.where(rows >= first_row, u, 0.0)
    i0 = lax.broadcasted_iota(jnp.int32, (rt, rt), 0)
    i1 = lax.broadcasted_iota(jnp.int32, (rt, rt), 1)
    regroup = (i1 == (i0 & (SUBLANE - 1)) * seg + (i0 >> 3)).astype(BF16)
    restore = (i0 == (i1 & (SUBLANE - 1)) * seg + (i1 >> 3)).astype(BF16)
    ub = jnp.dot(regroup, u.astype(BF16), preferred_element_type=F32).astype(BF16)

    half_w, half_s = S5_WIDTH // 2, S5_HALF // 2
    for kb in range(2):
        for part in range(2):
            c0 = part * S5_HALF + kb * half_s
            hs_ref[:, c0:c0 + half_s] = jnp.dot(
                ub[:, kb * half_w:(kb + 1) * half_w], bbig_ref[kb * half_w:(kb + 1) * half_w, c0:c0 + half_s],
                preferred_element_type=F32)

    ar = jnp.broadcast_to(ar_ref[...], (SUBLANE, S5_HALF))
    ai = jnp.broadcast_to(ai_ref[...], (SUBLANE, S5_HALF))

    def step(k, carry, store):
        hr, hi = carry
        base = pl.multiple_of(k * SUBLANE, SUBLANE)
        blk = hs_ref[pl.ds(base, SUBLANE), :]
        nr = ar * hr - ai * hi + blk[:, :S5_HALF]
        ni = ar * hi + ai * hr + blk[:, S5_HALF:]
        if store:
            hs_ref[pl.ds(base, SUBLANE), :] = jnp.concatenate([nr, ni], axis=1)
        return nr, ni

    zero = jnp.zeros((SUBLANE, S5_HALF), F32)
    er, ei = lax.fori_loop(0, seg, functools.partial(step, store=False), (zero, zero))
    asr, asi = asr_ref[...], asi_ref[...]
    sr, si = st_ref[0:1, :], st_ref[1:2, :]
    start_r, start_i = [], []
    for j in range(SUBLANE):
        start_r.append(sr)
        start_i.append(si)
        sr, si = asr * sr - asi * si + er[j:j + 1], asr * si + asi * sr + ei[j:j + 1]
    st_ref[0:1, :] = sr
    st_ref[1:2, :] = si
    lax.fori_loop(0, seg, functools.partial(step, store=True),
                  (jnp.concatenate(start_r, axis=0), jnp.concatenate(start_i, axis=0)))

    hb = hs_ref[...].astype(BF16)
    ys = []
    for kb in range(2):
        re_rows = slice(kb * half_s, (kb + 1) * half_s)
        im_rows = slice(S5_HALF + kb * half_s, S5_HALF + (kb + 1) * half_s)
        cols = slice(kb * half_w, (kb + 1) * half_w)
        ys.append(jnp.dot(hb[:, re_rows], cbig_ref[re_rows, cols], preferred_element_type=F32)
                  + jnp.dot(hb[:, im_rows], cbig_ref[im_rows, cols], preferred_element_type=F32))
    y = jnp.concatenate(ys, axis=1)
    y_hi = y.astype(BF16)
    y_lo = (y - y_hi.astype(F32)).astype(BF16)
    y = (jnp.dot(restore, y_hi, preferred_element_type=F32) + jnp.dot(restore, y_lo, preferred_element_type=F32)
         + d_ref[...] * u)
    g = jax.nn.gelu(y)
    gate = jnp.dot(g.astype(BF16), wglu_ref[...], preferred_element_type=F32) + bglu_ref[...]
    o_ref[0] = (g * jax.nn.sigmoid(gate)).astype(o_ref.dtype)

    @pl.when(t == pl.num_programs(1) - 1)
    def _():
        hlr_ref[0] = sr if last_seg == SUBLANE else start_r[last_seg]
        hli_ref[0] = si if last_seg == SUBLANE else start_i[last_seg]


def _s5(h, h0r, h0i, p, rt, first_row, n_real):
    bsz, seq, _ = h.shape
    seg = rt // SUBLANE
    real_in_last = n_real - (seq - rt)
    assert 0 < real_in_last <= rt and real_in_last % seg == 0
    as_re = (jnp.exp(p["lam_dt_re"] * seg) * jnp.cos(p["lam_dt_im"] * seg)).reshape(1, S5_HALF)
    as_im = (jnp.exp(p["lam_dt_re"] * seg) * jnp.sin(p["lam_dt_im"] * seg)).reshape(1, S5_HALF)
    state_spec = pl.BlockSpec((1, 1, S5_HALF), lambda b, t: (b, 0, 0))
    return pl.pallas_call(
        functools.partial(_s5_body, first_row=first_row, last_seg=real_in_last // seg),
        grid=(bsz, seq // rt),
        in_specs=[
            pl.BlockSpec((1, rt, S5_WIDTH), lambda b, t: (b, t, 0)),
            state_spec, state_spec,
            _const_spec((1, S5_HALF)), _const_spec((1, S5_HALF)), _const_spec((1, S5_HALF)), _const_spec((1, S5_HALF)),
            _const_spec((S5_WIDTH, 2 * S5_HALF)), _const_spec((2 * S5_HALF, S5_WIDTH)),
            _const_spec((1, S5_WIDTH)), _const_spec((S5_WIDTH, S5_WIDTH)), _const_spec((1, S5_WIDTH)),
        ],
        out_specs=[pl.BlockSpec((1, rt, S5_WIDTH), lambda b, t: (b, t, 0)), state_spec, state_spec],
        out_shape=[jax.ShapeDtypeStruct((bsz, seq, S5_WIDTH), BF16),
                   jax.ShapeDtypeStruct((bsz, 1, S5_HALF), F32),
                   jax.ShapeDtypeStruct((bsz, 1, S5_HALF), F32)],
        scratch_shapes=[pltpu.VMEM((rt, 2 * S5_HALF), F32), pltpu.VMEM((2, S5_HALF), F32)],
        compiler_params=_cparams(2, VMEM_LIMIT),
        name="s5",
    )(h, h0r, h0i, p["a_re"], p["a_im"], as_re, as_im, p["bbig"], p["cbig"], p["d"], p["w_glu"], p["b_glu"])


def _rope_lanes(x, c, s1, s2, half):
    n = x.shape[-1]
    return x * c + pltpu.roll(x, n - half, 1) * s1 + pltpu.roll(x, half, 1) * s2


def _mla_rows_body(h_ref, qn_ref, kn_ref, wq_ref, cq_ref, s1q_ref, s2q_ref, ck_ref, s1k_ref, s2k_ref,
                   q_ref, ckv_ref, kpe_ref, *, attn_layout):
    h = h_ref[0]
    q_lat = h[:, :MLA_Q_RANK]
    q_lat = q_lat * lax.rsqrt(jnp.mean(q_lat * q_lat, axis=-1, keepdims=True) + EPS) * qn_ref[...]
    q = jnp.dot(q_lat.astype(BF16), wq_ref[...], preferred_element_type=F32)
    cq, s1q, s2q = cq_ref[...], s1q_ref[...], s2q_ref[...]
    for hd in range(MLA_HEADS):
        sl = slice(hd * MLA_QK_PAD, (hd + 1) * MLA_QK_PAD)
        q_h = _rope_lanes(q[:, sl], cq, s1q, s2q, MLA_ROPE // 2)
        if attn_layout:
            q_ref[0, sl, :] = q_h.T.astype(BF16)
        else:
            q_ref[0, :, sl] = q_h.astype(BF16)
    c_kv = h[:, MLA_Q_RANK:MLA_Q_RANK + MLA_KV_RANK]
    ckv_ref[0] = c_kv * lax.rsqrt(jnp.mean(c_kv * c_kv, axis=-1, keepdims=True) + EPS) * kn_ref[...]
    k_pe = h[:, MLA_Q_RANK + MLA_KV_RANK:]
    kpe_ref[0] = _rope_lanes(k_pe, ck_ref[...], s1k_ref[...], s2k_ref[...], MLA_ROPE // 2)[:, :MLA_ROPE]


def _mla_rows(h, p, tabs, tm, attn_layout):
    bsz, seq, _ = h.shape
    tab_spec = pl.BlockSpec((tm, LANE), lambda b, t: (t, 0))
    n_q = MLA_HEADS * MLA_QK_PAD
    if attn_layout:
        q_shape, q_spec = (bsz, n_q, seq), pl.BlockSpec((1, n_q, tm), lambda b, t: (b, 0, t))
    else:
        q_shape, q_spec = (bsz, seq, n_q), pl.BlockSpec((1, tm, n_q), lambda b, t: (b, t, 0))
    return pl.pallas_call(
        functools.partial(_mla_rows_body, attn_layout=attn_layout),
        grid=(bsz, seq // tm),
        in_specs=[
            pl.BlockSpec((1, tm, EVEN_IN_PAD - S5_WIDTH), lambda b, t: (b, t, 1)),
            _const_spec((1, MLA_Q_RANK)), _const_spec((1, MLA_KV_RANK)),
            _const_spec((MLA_Q_RANK, n_q)),
        ] + [tab_spec] * 6,
        out_specs=[
            q_spec,
            pl.BlockSpec((1, tm, MLA_KV_RANK), lambda b, t: (b, t, 0)),
            pl.BlockSpec((1, tm, MLA_ROPE), lambda b, t: (b, t, 0)),
        ],
        out_shape=[jax.ShapeDtypeStruct(q_shape, BF16),
                   jax.ShapeDtypeStruct((bsz, seq, MLA_KV_RANK), F32),
                   jax.ShapeDtypeStruct((bsz, seq, MLA_ROPE), F32)],
        compiler_params=_cparams(2, VMEM_LIMIT),
        name="mla_rows",
    )(h, p["q_norm"], p["kv_norm"], p["wq"], *tabs)


def _mla_kv_body(ckv_ref, kpe_ref, wkn_ref, wkp_ref, wv_ref, k_ref, v_ref):
    ckv = ckv_ref[0].astype(BF16)
    k = jnp.dot(ckv, wkn_ref[...], preferred_element_type=F32)
    k = k + jnp.dot(kpe_ref[0].astype(BF16), wkp_ref[...], preferred_element_type=F32)
    k_ref[0] = k.astype(BF16)
    v_t = jnp.dot(ckv, wv_ref[...], preferred_element_type=F32).T
    for hd in range(MLA_HEADS):
        v_ref[0, hd, 0] = v_t[hd * MLA_V:(hd + 1) * MLA_V, :].astype(BF16)


def _mla_kv(ckv, kpe, p):
    bsz, seq, _ = ckv.shape
    n_t = seq // TILE
    return pl.pallas_call(
        _mla_kv_body,
        grid=(bsz, n_t),
        in_specs=[
            pl.BlockSpec((1, TILE, MLA_KV_RANK), lambda b, t: (b, t, 0)),
            pl.BlockSpec((1, TILE, MLA_ROPE), lambda b, t: (b, t, 0)),
            _const_spec(p["wkn"].shape), _const_spec(p["wkp"].shape), _const_spec(p["wv"].shape),
        ],
        out_specs=[pl.BlockSpec((1, TILE, MLA_HEADS * MLA_QK_PAD), lambda b, t: (b, t, 0)),
                   pl.BlockSpec((1, MLA_HEADS, 1, MLA_V, TILE), lambda b, t: (b, 0, t, 0, 0))],
        out_shape=[jax.ShapeDtypeStruct((bsz, seq, MLA_HEADS * MLA_QK_PAD), BF16),
                   jax.ShapeDtypeStruct((bsz, MLA_HEADS, n_t, MLA_V, TILE), BF16)],
        compiler_params=_cparams(2, VMEM_LIMIT),
        name="mla_kv",
    )(ckv, kpe, p["wkn"], p["wkp"], p["wv"])


def _flash_body(*refs, heads, dq, dv, causal):
    if causal:
        qt_ref, k_ref, vt_ref, fq_ref, fk_ref, o_ref = refs
    else:
        qt_ref, k_ref, vt_ref, o_ref = refs
    tq = qt_ref.shape[2]
    n_tiles, tk = vt_ref.shape[2], vt_ref.shape[4]
    r0 = pl.program_id(1) * tq
    j_last = jnp.minimum((r0 + tq - 1) // tk, n_tiles - 1)
    q_row = r0 + lax.broadcasted_iota(jnp.int32, (1, tq), 1)
    last = q_row if causal else ((q_row >> CHUNK_SHIFT) << CHUNK_SHIFT) + (CHUNK - 1)
    first_last = r0 if causal else ((r0 >> CHUNK_SHIFT) << CHUNK_SHIFT) + (CHUNK - 1)
    n_open = jnp.clip((first_last + 1) // tk, 1, j_last + 1)
    q_t = [qt_ref[0, hd * dq:(hd + 1) * dq, :] for hd in range(heads)]

    def tile(j, carry, masked):
        start = pl.multiple_of(j * tk, tk)
        if masked:
            k_row = start + lax.broadcasted_iota(jnp.int32, (tk, 1), 0)
            ok = (k_row <= last) & (k_row >= PAD)
        scores = [jnp.dot(k_ref[0, pl.ds(start, tk), hd * dq:(hd + 1) * dq], q_t[hd],
                          preferred_element_type=F32) for hd in range(heads)]
        probs = []
        for hd in range(heads):
            m, l, _ = carry[hd]
            s = scores[hd]
            if causal:
                s = s + (fq_ref[0, hd:hd + 1, :] - fk_ref[0, pl.ds(start, tk), hd:hd + 1])
            if masked:
                s = jnp.where(ok, s, NEG)
            m_new = jnp.maximum(m, jnp.max(s, axis=0, keepdims=True))
            a = jnp.exp2(m - m_new)
            p = jnp.exp2(s - m_new)
            probs.append((m_new, a * l + jnp.sum(p, axis=0, keepdims=True), a, p.astype(BF16)))
        new = []
        for hd in range(heads):
            m_new, l, a, p = probs[hd]
            acc = a * carry[hd][2] + jnp.dot(vt_ref[0, hd, j], p, preferred_element_type=F32)
            new.append((m_new, l, acc))
        return tuple(new)

    init = tuple((jnp.full((1, tq), NEG, F32), jnp.zeros((1, tq), F32), jnp.zeros((dv, tq), F32))
                 for _ in range(heads))
    carry = tile(0, init, True)
    carry = lax.fori_loop(1, n_open, functools.partial(tile, masked=False), carry)
    carry = lax.fori_loop(n_open, j_last + 1, functools.partial(tile, masked=True), carry)
    outs = [carry[hd][2] / carry[hd][1] for hd in range(heads)]
    per = LANE // dv
    for g in range(heads // per):
        o_ref[0, :, g * LANE:(g + 1) * LANE] = jnp.concatenate(
            outs[g * per:(g + 1) * per], axis=0).T.astype(o_ref.dtype)


def _flash(q_t, k, v_t, fq_t, fk, *, heads, dq, dv, causal):
    bsz, _, n_q = q_t.shape
    n_keys = k.shape[1]
    in_specs = [
        pl.BlockSpec((1, heads * dq, TILE), lambda b, i: (b, 0, i)),
        pl.BlockSpec((1, n_keys, heads * dq), lambda b, i: (b, 0, 0)),
        pl.BlockSpec((1,) + v_t.shape[1:], lambda b, i: (b, 0, 0, 0, 0)),
    ]
    args = [q_t, k, v_t]
    if causal:
        in_specs += [pl.BlockSpec((1, heads, TILE), lambda b, i: (b, 0, i)),
                     pl.BlockSpec((1, n_keys, heads), lambda b, i: (b, 0, 0))]
        args += [fq_t, fk]
    return pl.pallas_call(
        functools.partial(_flash_body, heads=heads, dq=dq, dv=dv, causal=causal),
        grid=(bsz, n_q // TILE),
        in_specs=in_specs,
        out_specs=pl.BlockSpec((1, TILE, heads * dv), lambda b, i: (b, i, 0)),
        out_shape=jax.ShapeDtypeStruct((bsz, n_q, heads * dv), BF16),
        compiler_params=_cparams(2, VMEM_LIMIT),
        name="flash_causal" if causal else "flash_chunk",
    )(*args)


def _online_softmax(s, m_ref, l_ref, idx):
    m_old = m_ref[idx]
    m_new = jnp.maximum(m_old, jnp.max(s, axis=-1, keepdims=True))
    a = jnp.exp2(m_old - m_new)
    p = jnp.exp2(s - m_new)
    m_ref[idx] = m_new
    l_ref[idx] = a * l_ref[idx] + jnp.sum(p, axis=-1, keepdims=True)
    return a, p


def _dot_nt(a, b):
    return lax.dot_general(a, b, (((1,), (1,)), ((), ())), preferred_element_type=F32)


def _fox_decode_body(q_ref, fq_ref, kc_ref, vc_ref, fkc_ref, kn_ref, vn_ref, fkn_ref, o_ref, m_ref, l_ref, acc_ref):
    t = pl.program_id(1)
    n_q = q_ref.shape[1]

    @pl.when(t == 0)
    def _():
        m_ref[...] = jnp.full(m_ref.shape, NEG, F32)
        l_ref[...] = jnp.zeros(l_ref.shape, F32)
        acc_ref[...] = jnp.zeros(acc_ref.shape, F32)

    def attend(k, v, fk_rows, ok):
        cols = [slice(hd * FOX_DIM, (hd + 1) * FOX_DIM) for hd in range(FOX_HEADS)]
        scores = [_dot_nt(q_ref[0, :, sl], k[:, sl]) for sl in cols]
        probs = []
        for hd in range(FOX_HEADS):
            s = scores[hd] + (fq_ref[0, :, hd:hd + 1] - fk_rows[hd:hd + 1, :])
            if ok is not None:
                s = jnp.where(ok, s, NEG)
            probs.append(_online_softmax(s, m_ref, l_ref, hd))
        for hd in range(FOX_HEADS):
            a, p = probs[hd]
            acc_ref[hd] = a * acc_ref[hd] + jnp.dot(p.astype(BF16), v[:, cols[hd]], preferred_element_type=F32)

    attend(kc_ref[0, 0].astype(BF16), vc_ref[0, 0].astype(BF16), fkc_ref[0], None)

    @pl.when(t == pl.num_programs(1) - 1)
    def _():
        causal = (lax.broadcasted_iota(jnp.int32, (n_q, n_q), 1) <= lax.broadcasted_iota(jnp.int32, (n_q, n_q), 0))
        attend(kn_ref[0], vn_ref[0], fkn_ref[0], causal)
        for hd in range(FOX_HEADS):
            o_ref[0, :, hd * FOX_DIM:(hd + 1) * FOX_DIM] = (acc_ref[hd] / l_ref[hd]).astype(o_ref.dtype)


def _fox_decode(q, k_new, v_new, cache_k, cache_v, layer, fcum, tk):
    bsz, n_q, width = q.shape
    n_past = cache_k.shape[2]
    fq = fcum[:, n_past:n_past + n_q]
    fk_t = fcum.transpose(0, 2, 1)
    row_spec = pl.BlockSpec((1, n_q, width), lambda b, t: (b, 0, 0))
    cache_spec = pl.BlockSpec((1, 1, tk, width), lambda b, t: (layer, b, t, 0))
    return pl.pallas_call(
        _fox_decode_body,
        grid=(bsz, n_past // tk),
        in_specs=[row_spec, pl.BlockSpec((1, n_q, FOX_HEADS), lambda b, t: (b, 0, 0)),
                  cache_spec, cache_spec, pl.BlockSpec((1, FOX_HEADS, tk), lambda b, t: (b, 0, t)),
                  row_spec, row_spec, pl.BlockSpec((1, FOX_HEADS, n_q), lambda b, t: (b, 0, 0))],
        out_specs=row_spec,
        out_shape=jax.ShapeDtypeStruct((bsz, n_q, width), BF16),
        scratch_shapes=[pltpu.VMEM((FOX_HEADS, n_q, 1), F32), pltpu.VMEM((FOX_HEADS, n_q, 1), F32),
                        pltpu.VMEM((FOX_HEADS, n_q, FOX_DIM), F32)],
        compiler_params=_cparams(2, VMEM_LIMIT),
        name="fox_decode",
    )(q, fq, cache_k, cache_v, fk_t[:, :, :n_past], k_new, v_new, fk_t[:, :, n_past:n_past + n_q])


def _mla_decode_body(q_ref, wkn_ref, wv_ref, cc_ref, pc_ref, cn_ref, pn_ref, o_ref, qa_ref, qr_ref,
                     m_ref, l_ref, acc_ref):
    t = pl.program_id(1)
    n_q = q_ref.shape[1]

    @pl.when(t == 0)
    def _():
        for hd in range(MLA_HEADS):
            c0 = hd * MLA_QK_PAD
            rows = slice(hd * n_q, (hd + 1) * n_q)
            qa_ref[rows, :] = _dot_nt(q_ref[0, :, c0:c0 + MLA_NOPE],
                                      wkn_ref[:, c0:c0 + MLA_NOPE]).astype(BF16)
            qr_ref[rows, :] = q_ref[0, :, c0 + MLA_NOPE:c0 + MLA_NOPE + MLA_ROPE]
        m_ref[...] = jnp.full(m_ref.shape, NEG, F32)
        l_ref[...] = jnp.zeros(l_ref.shape, F32)
        acc_ref[...] = jnp.zeros(acc_ref.shape, F32)

    def attend(ckv, kpe):
        s = _dot_nt(qa_ref[...], ckv) + _dot_nt(qr_ref[...], kpe)
        a, p = _online_softmax(s, m_ref, l_ref, 0)
        acc_ref[0] = a * acc_ref[0] + jnp.dot(p.astype(BF16), ckv, preferred_element_type=F32)

    attend(cc_ref[0, 0].astype(BF16), pc_ref[0, 0].astype(BF16))

    @pl.when(t == pl.num_programs(1) - 1)
    def _():
        attend(cn_ref[0].astype(BF16), pn_ref[0].astype(BF16))
        lat = (acc_ref[0] / l_ref[0]).astype(BF16)
        for hd in range(MLA_HEADS):
            cols = slice(hd * MLA_V, (hd + 1) * MLA_V)
            o_ref[0, :, cols] = jnp.dot(lat[hd * n_q:(hd + 1) * n_q, :], wv_ref[:, cols],
                                        preferred_element_type=F32).astype(o_ref.dtype)


def _mla_decode(q, ckv_new, kpe_new, cache_ckv, cache_kpe, layer, p, tk):
    bsz, n_q, _ = q.shape
    n_past = cache_ckv.shape[2]
    stacked = MLA_HEADS * n_q
    return pl.pallas_call(
        _mla_decode_body,
        grid=(bsz, n_past // tk),
        in_specs=[pl.BlockSpec((1, n_q, MLA_HEADS * MLA_QK_PAD), lambda b, t: (b, 0, 0)),
                  _const_spec(p["wkn"].shape), _const_spec(p["wv"].shape),
                  pl.BlockSpec((1, 1, tk, MLA_KV_RANK), lambda b, t: (layer, b, t, 0)),
                  pl.BlockSpec((1, 1, tk, MLA_ROPE), lambda b, t: (layer, b, t, 0)),
                  pl.BlockSpec((1, n_q, MLA_KV_RANK), lambda b, t: (b, 0, 0)),
                  pl.BlockSpec((1, n_q, MLA_ROPE), lambda b, t: (b, 0, 0))],
        out_specs=pl.BlockSpec((1, n_q, MLA_HEADS * MLA_V), lambda b, t: (b, 0, 0)),
        out_shape=jax.ShapeDtypeStruct((bsz, n_q, MLA_HEADS * MLA_V), BF16),
        scratch_shapes=[pltpu.VMEM((stacked, MLA_KV_RANK), BF16), pltpu.VMEM((stacked, MLA_ROPE), BF16),
                        pltpu.VMEM((1, stacked, 1), F32), pltpu.VMEM((1, stacked, 1), F32),
                        pltpu.VMEM((1, stacked, MLA_KV_RANK), F32)],
        compiler_params=_cparams(2, VMEM_LIMIT),
        name="mla_decode",
    )(q, p["wkn"], p["wv"], cache_ckv, cache_kpe, ckv_new, kpe_new)


def _gate_body(x_ref, b_ref, lf_ref, fc_ref, *, new_start, first_row):
    n_rows = x_ref.shape[1]
    tri = (lax.broadcasted_iota(jnp.int32, (TILE, TILE), 0)
           >= lax.broadcasted_iota(jnp.int32, (TILE, TILE), 1)).astype(F32)
    carry = jnp.zeros((1, x_ref.shape[2]), F32)
    for i in range(n_rows // TILE):
        sl = slice(i * TILE, (i + 1) * TILE)
        x = x_ref[0, sl, :]
        rows = i * TILE + lax.broadcasted_iota(jnp.int32, (TILE, 1), 0)
        z = x + b_ref[...]
        lf = jnp.where(rows >= new_start, jnp.minimum(z, 0.0) - jnp.log1p(jnp.exp(-jnp.abs(z))), x)
        if first_row:
            lf = jnp.where(rows >= first_row, lf, 0.0)
        lf_ref[0, sl, :] = lf
        cs = jnp.dot(tri, lf, preferred_element_type=F32, precision=lax.Precision.HIGHEST) + carry
        fc_ref[0, sl, :] = cs
        carry = cs[TILE - 1:TILE, :]


def _gate(x, b_f, new_start, first_row):
    bsz, n_rows, heads = x.shape
    spec = pl.BlockSpec((1, n_rows, heads), lambda b: (b, 0, 0))
    return pl.pallas_call(
        functools.partial(_gate_body, new_start=new_start, first_row=first_row),
        grid=(bsz,),
        in_specs=[spec, _const_spec((1, heads))],
        out_specs=[spec, spec],
        out_shape=[jax.ShapeDtypeStruct(x.shape, F32)] * 2,
        compiler_params=_cparams(1),
        name="fox_gate",
    )(x, b_f)


RET_LOG_GAMMA = tuple(math.log(1.0 - 2.0 ** (-5.0 - h)) for h in range(RET_HEADS))


def _ret_body(rq_ref, rk_ref, rv_ref, rg_ref, s0_ref, cq_ref, s1q_ref, s2q_ref, ck_ref, s1k_ref, s2k_ref,
              o_ref, sl_ref, st_ref, *, n_tail):
    c = pl.program_id(1)
    ct = rq_ref.shape[1]

    @pl.when(c == 0)
    def _():
        st_ref[...] = s0_ref[0]

    q = _rope_lanes(rq_ref[0], cq_ref[...], s1q_ref[...], s2q_ref[...], RET_DK // 2)
    k = _rope_lanes(rk_ref[0], ck_ref[...], s1k_ref[...], s2k_ref[...], RET_DK // 2)
    v = rv_ref[0].astype(BF16)
    g = rg_ref[0]
    j = lax.broadcasted_iota(jnp.int32, (ct, 1), 0).astype(F32)
    diff = (lax.broadcasted_iota(jnp.int32, (ct, ct), 0)
            - lax.broadcasted_iota(jnp.int32, (ct, ct), 1)).astype(F32)
    for hd in range(RET_HEADS):
        lg = RET_LOG_GAMMA[hd]
        q_h = q[:, hd * RET_DK:(hd + 1) * RET_DK]
        k_h = k[:, hd * RET_DK:(hd + 1) * RET_DK]
        v_h = v[:, hd * RET_DV:(hd + 1) * RET_DV]
        decay = jnp.where(diff >= 0.0, jnp.exp(lg * jnp.maximum(diff, 0.0)), 0.0)
        scores = lax.dot_general(q_h.astype(BF16), k_h.astype(BF16), (((1,), (1,)), ((), ())),
                                 preferred_element_type=F32) * decay
        s_h = st_ref[hd]
        out = jnp.dot(scores.astype(BF16), v_h, preferred_element_type=F32)
        out = out + jnp.dot((q_h * jnp.exp(lg * (j + 1.0))).astype(BF16), s_h.astype(BF16),
                            preferred_element_type=F32)
        k_dec = (k_h * jnp.exp(lg * (ct - 1.0 - j))).astype(BF16)
        st_ref[hd] = math.exp(lg * ct) * s_h + lax.dot_general(
            k_dec, v_h, (((0,), (0,)), ((), ())), preferred_element_type=F32)
        mu = jnp.mean(out, axis=-1, keepdims=True)
        oc = out - mu
        var = jnp.mean(oc * oc, axis=-1, keepdims=True)
        g_h = g[:, hd * RET_DV:(hd + 1) * RET_DV]
        o_ref[0, :, hd * RET_DV:(hd + 1) * RET_DV] = (
            g_h * jax.nn.sigmoid(g_h) * (oc * lax.rsqrt(var + EPS))).astype(o_ref.dtype)

    @pl.when(c == pl.num_programs(1) - 1)
    def _():
        for hd in range(RET_HEADS):
            sl_ref[0, hd] = st_ref[hd] * math.exp(-RET_LOG_GAMMA[hd] * n_tail)


def _retention(h, s0, tabs, ct, n_tail):
    bsz, seq, _ = h.shape
    tab_spec = pl.BlockSpec((ct, RET_QK), lambda b, c: (c, 0))
    st_spec = pl.BlockSpec((1, RET_HEADS, RET_DK, RET_DV), lambda b, c: (b, 0, 0, 0))
    rq_blk = (2 * FOX_W) // RET_QK
    rv_blk = (2 * FOX_W + 2 * RET_QK) // RET_VW
    return pl.pallas_call(
        functools.partial(_ret_body, n_tail=n_tail),
        grid=(bsz, seq // ct),
        in_specs=[
            pl.BlockSpec((1, ct, RET_QK), lambda b, c: (b, c, rq_blk)),
            pl.BlockSpec((1, ct, RET_QK), lambda b, c: (b, c, rq_blk + 1)),
            pl.BlockSpec((1, ct, RET_VW), lambda b, c: (b, c, rv_blk)),
            pl.BlockSpec((1, ct, RET_VW), lambda b, c: (b, c, rv_blk + 1)),
            st_spec,
        ] + [tab_spec] * 6,
        out_specs=[pl.BlockSpec((1, ct, RET_VW), lambda b, c: (b, c, 0)), st_spec],
        out_shape=[jax.ShapeDtypeStruct((bsz, seq, RET_VW), BF16),
                   jax.ShapeDtypeStruct((bsz, RET_HEADS, RET_DK, RET_DV), F32)],
        scratch_shapes=[pltpu.VMEM((RET_HEADS, RET_DK, RET_DV), F32)],
        compiler_params=_cparams(2, VMEM_LIMIT),
        name="retention",
    )(h, h, h, h, s0, *tabs)


def _rope_tables(pos, half, width, group, offset, scale=1.0, valid=None):
    inv = ROPE_BASE ** (-jnp.arange(half, dtype=F32) / half)
    ang = pos.astype(F32)[:, None] * inv[None, :]
    cos, sin = jnp.cos(ang), jnp.sin(ang)
    n = pos.shape[0]
    one, zero = jnp.ones((n, 1), F32), jnp.zeros((n, 1), F32)

    def lanes(first, second, other):
        grp = jnp.concatenate([jnp.broadcast_to(other, (n, offset)), first, second,
                               jnp.broadcast_to(other, (n, group - offset - 2 * half))], axis=1)
        return jnp.tile(grp, (1, width // group))

    tabs = (lanes(cos, cos, one), lanes(-sin, 0.0 * sin, zero), lanes(0.0 * sin, sin, zero))
    if valid is not None:
        tabs = tuple(jnp.where(valid[:, None], t, 0.0) for t in tabs)
    return tuple(t * scale for t in tabs)


def _even_params(e, w):
    p = {}
    p["w_in"] = jnp.pad(w["even_w_in"][e], ((0, 0), (0, EVEN_IN_PAD - EVEN_IN))).astype(BF16)
    p["w_out"] = w["even_w_out"][e].astype(BF16)
    lam_re, lam_im = w["s5_a_re"][e].astype(F32), w["s5_a_im"][e].astype(F32)
    dt = jnp.exp(w["s5_log_dt"][e].astype(F32))[:, None]
    p["lam_dt_re"], p["lam_dt_im"] = lam_re * dt, lam_im * dt
    mag = jnp.exp(lam_re * dt)
    abar_re, abar_im = mag * jnp.cos(lam_im * dt), mag * jnp.sin(lam_im * dt)
    den = lam_re * lam_re + lam_im * lam_im
    f_re = ((abar_re - 1.0) * lam_re + abar_im * lam_im) / den
    f_im = (abar_im * lam_re - (abar_re - 1.0) * lam_im) / den
    b_re, b_im = w["s5_b_re"][e].astype(F32), w["s5_b_im"][e].astype(F32)
    bb_re = f_re[..., None] * b_re - f_im[..., None] * b_im
    bb_im = f_re[..., None] * b_im + f_im[..., None] * b_re
    eye = jnp.eye(S5_GROUPS, dtype=F32)

    def in_blocks(x):
        return jnp.einsum("gnc,gh->gchn", x, eye).reshape(S5_WIDTH, S5_HALF)

    def out_blocks(x):
        return jnp.einsum("gcn,gh->gnhc", x, eye).reshape(S5_HALF, S5_WIDTH)

    p["bbig"] = jnp.concatenate([in_blocks(bb_re), in_blocks(bb_im)], axis=1).astype(BF16)
    p["cbig"] = jnp.concatenate([out_blocks(w["s5_c_re"][e].astype(F32)),
                                 out_blocks(-w["s5_c_im"][e].astype(F32))], axis=0).astype(BF16)
    p["a_re"] = abar_re.reshape(1, S5_HALF)
    p["a_im"] = abar_im.reshape(1, S5_HALF)
    p["d"] = w["s5_d"][e].astype(F32).reshape(1, S5_WIDTH)
    p["w_glu"] = w["s5_w_glu"][e].astype(BF16)
    p["b_glu"] = w["s5_b_glu"][e].astype(F32).reshape(1, S5_WIDTH)
    p["q_norm"] = w["mla_q_norm"][e].astype(F32).reshape(1, MLA_Q_RANK)
    p["kv_norm"] = w["mla_kv_norm"][e].astype(F32).reshape(1, MLA_KV_RANK)
    wq = w["mla_w_uq"][e].reshape(MLA_Q_RANK, MLA_HEADS, MLA_NOPE + MLA_ROPE)
    wq = jnp.pad(wq, ((0, 0), (0, 0), (0, MLA_QK_PAD - MLA_NOPE - MLA_ROPE)))
    p["wq"] = wq.reshape(MLA_Q_RANK, MLA_HEADS * MLA_QK_PAD).astype(BF16)
    wkv = w["mla_w_ukv"][e].reshape(MLA_KV_RANK, MLA_HEADS, MLA_NOPE + MLA_V)
    wkn = jnp.pad(wkv[:, :, :MLA_NOPE], ((0, 0), (0, 0), (0, MLA_QK_PAD - MLA_NOPE)))
    p["wkn"] = wkn.reshape(MLA_KV_RANK, MLA_HEADS * MLA_QK_PAD).astype(BF16)
    place = jnp.pad(jnp.eye(MLA_ROPE, dtype=F32), ((0, 0), (MLA_NOPE, MLA_QK_PAD - MLA_NOPE - MLA_ROPE)))
    p["wkp"] = jnp.tile(place, (1, MLA_HEADS)).astype(BF16)
    p["wv"] = wkv[:, :, MLA_NOPE:].reshape(MLA_KV_RANK, MLA_HEADS * MLA_V).astype(BF16)
    return p


def _odd_params(o, w):
    w_in = w["odd_w_in"][o]
    c_logit = 3 * FOX_W
    cols = jnp.concatenate([
        w_in[:, :FOX_W] * (FOX_DIM ** -0.5),
        w_in[:, FOX_W:c_logit],
        w_in[:, c_logit + FOX_HEADS:],
        w_in[:, c_logit:c_logit + FOX_HEADS],
    ], axis=1)
    p = {"w_in": jnp.pad(cols, ((0, 0), (0, ODD_IN_PAD - cols.shape[1]))).astype(BF16)}
    p["w_out"] = w["odd_w_out"][o].astype(BF16)
    p["b_f"] = w["fox_b_f"][o].astype(F32).reshape(1, FOX_HEADS)
    return p


def _cache_tile(n_past):
    for tk in (512, 256, 128):
        if n_past % tk == 0:
            return tk
    raise ValueError("cache length must be a multiple of 128")


def _trunk(x, pos, n_real, past, w, ffn, evens, odds):
    bsz, seq, _ = x.shape
    rows = bsz * seq
    prompt = past is None
    first_row = PAD if prompt else 0
    if prompt:
        tm = 2 * TILE if rows % (2 * TILE) == 0 else TILE
        rt = TILE
    else:
        n_past = past["cache_fox_k"].shape[2]
        tk = _cache_tile(n_past)
        cache_k = past["cache_fox_k"].reshape(past["cache_fox_k"].shape[:3] + (FOX_W,))
        cache_v = past["cache_fox_v"].reshape(past["cache_fox_v"].shape[:3] + (FOX_W,))
        tm, rt = rows, seq
    idx = jnp.arange(seq, dtype=jnp.int32)
    valid = (idx >= first_row) & (idx < n_real)
    st = {n: [] for n in ("mla_ckv", "mla_kpe", "s5_re", "s5_im", "fox_k", "fox_v", "fox_logf", "ret")}
    mla_scale = (MLA_NOPE + MLA_ROPE) ** -0.5 * LOG2E
    q_tabs = _rope_tables(pos, MLA_ROPE // 2, MLA_QK_PAD, MLA_QK_PAD, MLA_NOPE, scale=mla_scale)
    k_tabs = _rope_tables(pos, MLA_ROPE // 2, LANE, LANE, 0)
    rq_tabs = _rope_tables(pos, RET_DK // 2, RET_QK, RET_DK, 0)
    rk_tabs = _rope_tables(pos, RET_DK // 2, RET_QK, RET_DK, 0, scale=RET_DK ** -0.5, valid=valid)
    x2 = x.reshape(rows, D_MODEL)

    def ln(l, i):
        return w["ln_g"][l, i].reshape(1, D_MODEL), w["ln_b"][l, i].reshape(1, D_MODEL)

    for l in range(DEPTH):
        x2 = _ffn_ln(x2, *ffn[l][0], *ln(l, 0), tm)
        if l % 2 == 0:
            e = l // 2
            p = evens[e]
            h = _proj(x2, p["w_in"], tm).reshape(bsz, seq, EVEN_IN_PAD)
            if prompt:
                h0r = h0i = jnp.zeros((bsz, 1, S5_HALF), F32)
            else:
                h0r = past["state_s5_re"][e].astype(F32).reshape(bsz, 1, S5_HALF)
                h0i = past["state_s5_im"][e].astype(F32).reshape(bsz, 1, S5_HALF)
            s5_out, hlr, hli = _s5(h, h0r, h0i, p, rt, first_row, n_real)
            q, ckv, kpe = _mla_rows(h, p, q_tabs + k_tabs, rt, attn_layout=prompt)
            if prompt:
                k_att, v_att = _mla_kv(ckv, kpe, p)
                mla_out = _flash(q, k_att, v_att, None, None, heads=MLA_HEADS, dq=MLA_QK_PAD, dv=MLA_V,
                                 causal=False)
            else:
                mla_out = _mla_decode(q, ckv, kpe, past["cache_mla_ckv"], past["cache_mla_kpe"], e, p, tk)
            x2 = _outproj_ln(x2, s5_out.reshape(rows, S5_WIDTH), mla_out.reshape(rows, MLA_HEADS * MLA_V),
                             p["w_out"], *ln(l, 1), tm)
            st["mla_ckv"].append(ckv)
            st["mla_kpe"].append(kpe)
            st["s5_re"].append(hlr.reshape(bsz, S5_GROUPS, S5_STATE))
            st["s5_im"].append(hli.reshape(bsz, S5_GROUPS, S5_STATE))
        else:
            o = l // 2
            p = odds[o]
            h, fq, fk16, fv = _proj_odd(x2, p["w_in"], bsz, TILE if prompt else tm, attn_layout=prompt)
            h = h.reshape(bsz, seq, ODD_H)
            fk16 = fk16.reshape(bsz, seq, FOX_W)
            f_logit = h[:, :, ODD_LOGIT_COL:ODD_LOGIT_COL + FOX_HEADS]
            if prompt:
                q_off = 0
                logf, fcum = _gate(f_logit, p["b_f"], 0, PAD)
                fcum = fcum * LOG2E
                fox_out = _flash(fq, fk16, fv, fcum.transpose(0, 2, 1), fcum, heads=FOX_HEADS, dq=FOX_DIM,
                                 dv=FOX_DIM, causal=True)
                s0 = jnp.zeros((bsz, RET_HEADS, RET_DK, RET_DV), F32)
            else:
                q_off = n_past
                gates = jnp.concatenate([past["cache_fox_logf"][o].astype(F32), f_logit], axis=1)
                gates = jnp.pad(gates, ((0, 0), (0, _round_up(n_past + seq, TILE) - n_past - seq), (0, 0)))
                logf, fcum = _gate(gates, p["b_f"], n_past, 0)
                fox_out = _fox_decode(fq.reshape(bsz, seq, FOX_W), fk16, fv.reshape(bsz, seq, FOX_W),
                                      cache_k, cache_v, o, fcum * LOG2E, tk)
                s0 = past["state_ret"][o].astype(F32)
            ret_out, s_last = _retention(h, s0, rq_tabs + rk_tabs, rt, seq - n_real)
            x2 = _outproj_ln(x2, fox_out.reshape(rows, FOX_W), ret_out.reshape(rows, RET_VW),
                             p["w_out"], *ln(l, 1), tm)
            st["fox_k"].append(h[:, :, :FOX_W].reshape(bsz, seq, FOX_HEADS, FOX_DIM))
            st["fox_v"].append(h[:, :, FOX_W:2 * FOX_W].reshape(bsz, seq, FOX_HEADS, FOX_DIM))
            st["fox_logf"].append(logf[:, q_off:q_off + seq])
            st["ret"].append(s_last)
        x2 = _ffn_ln(x2, *ffn[l][1], *ln(l, 2), tm)
    return x2.reshape(bsz, seq, D_MODEL), {n: jnp.stack(a) for n, a in st.items()}


def kernel(x_prompt, x_sample, cache_mla_ckv, cache_mla_kpe, cache_fox_k, cache_fox_v, cache_fox_logf,
           state_s5_re, state_s5_im, state_ret, meta_tokens, ln_g, ln_b, ffn_w_gate, ffn_w_up, ffn_w_down,
           even_w_in, even_w_out, s5_a_re, s5_a_im, s5_b_re, s5_b_im, s5_c_re, s5_c_im, s5_d, s5_log_dt,
           s5_w_glu, s5_b_glu, mla_q_norm, mla_kv_norm, mla_w_uq, mla_w_ukv, odd_w_in, odd_w_out, fox_b_f):
    w = dict(ln_g=ln_g.astype(F32), ln_b=ln_b.astype(F32), even_w_in=even_w_in, even_w_out=even_w_out,
             s5_a_re=s5_a_re, s5_a_im=s5_a_im, s5_b_re=s5_b_re, s5_b_im=s5_b_im, s5_c_re=s5_c_re,
             s5_c_im=s5_c_im, s5_d=s5_d, s5_log_dt=s5_log_dt, s5_w_glu=s5_w_glu, s5_b_glu=s5_b_glu,
             mla_q_norm=mla_q_norm, mla_kv_norm=mla_kv_norm, mla_w_uq=mla_w_uq, mla_w_ukv=mla_w_ukv,
             odd_w_in=odd_w_in, odd_w_out=odd_w_out, fox_b_f=fox_b_f)
    past = dict(cache_mla_ckv=cache_mla_ckv, cache_mla_kpe=cache_mla_kpe, cache_fox_k=cache_fox_k,
                cache_fox_v=cache_fox_v, cache_fox_logf=cache_fox_logf, state_s5_re=state_s5_re,
                state_s5_im=state_s5_im, state_ret=state_ret)
    ffn = [[(ffn_w_gate[l, i].astype(BF16), ffn_w_up[l, i].astype(BF16), ffn_w_down[l, i].astype(BF16))
            for i in range(2)] for l in range(DEPTH)]
    evens = [_even_params(e, w) for e in range((DEPTH + 1) // 2)]
    odds = [_odd_params(o, w) for o in range(DEPTH // 2)]

    bsz, seq, _ = x_prompt.shape
    n_real = PAD + N_META + seq
    n_rows = _round_up(n_real, TILE)
    meta = jnp.broadcast_to(meta_tokens[None].astype(x_prompt.dtype), (bsz, N_META, D_MODEL))
    xp = jnp.concatenate([jnp.zeros((bsz, PAD, D_MODEL), x_prompt.dtype), meta, x_prompt,
                          jnp.zeros((bsz, n_rows - n_real, D_MODEL), x_prompt.dtype)], axis=1)
    pos_p = jnp.maximum(jnp.arange(n_rows, dtype=jnp.int32) - PAD, 0)
    y_p, st_p = _trunk(xp, pos_p, n_real, None, w, ffn, evens, odds)
    d_seq = x_sample.shape[1]
    pos_s = N_META + cache_fox_k.shape[2] + jnp.arange(d_seq, dtype=jnp.int32)
    y_s, st_s = _trunk(x_sample, pos_s, d_seq, past, w, ffn, evens, odds)

    def real(a):
        return a[:, :, PAD:n_real]

    return (y_p[:, PAD + N_META:n_real], y_s,
            real(st_p["mla_ckv"]), real(st_p["mla_kpe"]), real(st_p["fox_k"]), real(st_p["fox_v"]),
            real(st_p["fox_logf"]), st_p["s5_re"], st_p["s5_im"], st_p["ret"],
            st_s["mla_ckv"], st_s["mla_kpe"], st_s["fox_k"], st_s["fox_v"], st_s["fox_logf"],
            st_s["s5_re"], st_s["s5_im"], st_s["ret"])
```

```python
import functools
import math

import jax
import jax.numpy as jnp
from jax import lax
from jax.experimental import pallas as pl
from jax.experimental.pallas import tpu as pltpu

F32 = jnp.float32
BF16 = jnp.bfloat16

D_MODEL = 1024
DEPTH = 4
CHUNK = 64
CHUNK_SHIFT = 6
N_META = 16
S5_WIDTH = 512
S5_CH = 16
S5_GROUPS = S5_WIDTH // S5_CH
S5_STATE = 64
S5_HALF = S5_GROUPS * S5_STATE
MLA_HEADS = 8
MLA_Q_RANK = 256
MLA_KV_RANK = 128
MLA_NOPE = 64
MLA_ROPE = 32
MLA_V = 64
MLA_QK_PAD = 128
FOX_HEADS = 8
FOX_DIM = 64
FOX_W = FOX_HEADS * FOX_DIM
RET_HEADS = 4
RET_DK = 64
RET_DV = 128
RET_QK = RET_HEADS * RET_DK
RET_VW = RET_HEADS * RET_DV
D_FF = 2816
ROPE_BASE = 10000.0
ALPHA = (2.0 * DEPTH) ** 0.25
EPS = 1e-5
NEG = -1e30
LOG2E = math.log2(math.e)
EVEN_IN = S5_WIDTH + MLA_Q_RANK + MLA_KV_RANK + MLA_ROPE
EVEN_IN_PAD = 1024
ODD_IN_PAD = 3200
ODD_H = ODD_IN_PAD - FOX_W
ODD_LOGIT_COL = 2 * FOX_W + 2 * RET_QK + 2 * RET_VW

PAD = CHUNK - N_META
LANE = 128
SUBLANE = 8
TILE = 256
VMEM_LIMIT = 56 * 1024 * 1024


def _cparams(n_grid, vmem=None):
    return pltpu.CompilerParams(dimension_semantics=("arbitrary",) * n_grid, vmem_limit_bytes=vmem)


def _const_spec(shape):
    nd = len(shape)
    return pl.BlockSpec(shape, lambda *_: (0,) * nd)


def _round_up(n, m):
    return -(-n // m) * m


def _layer_norm_rows(z, g, b):
    mu = jnp.mean(z, axis=-1, keepdims=True)
    zc = z - mu
    var = jnp.mean(zc * zc, axis=-1, keepdims=True)
    return zc * lax.rsqrt(var + EPS) * g + b


def _ffn_rows(x, wg_ref, wu_ref, wd_ref, g_ref, b_ref, hid_ref):
    xb = x.astype(BF16)
    for c in range(D_FF // TILE):
        sl = slice(c * TILE, (c + 1) * TILE)
        hg = jnp.dot(xb, wg_ref[:, sl], preferred_element_type=F32)
        hu = jnp.dot(xb, wu_ref[:, sl], preferred_element_type=F32)
        hid_ref[:, sl] = (hg * jax.nn.sigmoid(hg) * hu).astype(BF16)
    y = jnp.dot(hid_ref[...], wd_ref[...], preferred_element_type=F32)
    return _layer_norm_rows(ALPHA * x + 0.5 * y, g_ref[...], b_ref[...])


def _ffn_body(x_ref, wg_ref, wu_ref, wd_ref, g_ref, b_ref, o_ref, hid_ref):
    o_ref[...] = _ffn_rows(x_ref[...], wg_ref, wu_ref, wd_ref, g_ref, b_ref, hid_ref)


def _ffn_weight_specs():
    once = dict(pipeline_mode=pl.Buffered(1))
    return [pl.BlockSpec((D_MODEL, D_FF), lambda i: (0, 0), **once),
            pl.BlockSpec((D_MODEL, D_FF), lambda i: (0, 0), **once),
            pl.BlockSpec((D_FF, D_MODEL), lambda i: (0, 0), **once),
            _const_spec((1, D_MODEL)), _const_spec((1, D_MODEL))]


def _ffn_ln(x, wg, wu, wd, g, b, tm):
    rows = x.shape[0]
    return pl.pallas_call(
        _ffn_body,
        grid=(rows // tm,),
        in_specs=[pl.BlockSpec((tm, D_MODEL), lambda i: (i, 0))] + _ffn_weight_specs(),
        out_specs=pl.BlockSpec((tm, D_MODEL), lambda i: (i, 0)),
        out_shape=jax.ShapeDtypeStruct((rows, D_MODEL), F32),
        scratch_shapes=[pltpu.VMEM((tm, D_FF), BF16)],
        compiler_params=_cparams(1, VMEM_LIMIT),
        name="ffn_ln",
    )(x, wg, wu, wd, g, b)


def _mix_ffn_body(x_ref, a1_ref, a2_ref, wo_ref, g1_ref, b1_ref, wg_ref, wu_ref, wd_ref, g2_ref, b2_ref,
                  o_ref, hid_ref):
    k1 = a1_ref.shape[1]
    y = jnp.dot(a1_ref[...], wo_ref[:k1, :], preferred_element_type=F32)
    y = y + jnp.dot(a2_ref[...], wo_ref[k1:, :], preferred_element_type=F32)
    x1 = _layer_norm_rows(ALPHA * x_ref[...] + y, g1_ref[...], b1_ref[...])
    o_ref[...] = _ffn_rows(x1, wg_ref, wu_ref, wd_ref, g2_ref, b2_ref, hid_ref)


def _mix_ffn_ln(x, a1, a2, w_out, ln1, ffn_w, ln2, tm, keep=None):
    rows = x.shape[0]
    if keep is None:
        n_steps = rows // tm
        row_spec = lambda n: pl.BlockSpec((tm, n), lambda i: (i, 0))
    else:
        seq, start, count = keep
        per = count // tm
        n_steps = (rows // seq) * per
        align = math.gcd(seq, start, tm)
        row_spec = lambda n: pl.BlockSpec(
            (pl.Element(tm), pl.Element(n)),
            lambda i: (pl.multiple_of((i // per) * seq + start + (i % per) * tm, align), 0))
    return pl.pallas_call(
        _mix_ffn_body,
        grid=(n_steps,),
        in_specs=[row_spec(D_MODEL), row_spec(a1.shape[1]), row_spec(a2.shape[1]),
                  pl.BlockSpec(w_out.shape, lambda i: (0, 0), pipeline_mode=pl.Buffered(1)),
                  _const_spec((1, D_MODEL)), _const_spec((1, D_MODEL))] + _ffn_weight_specs(),
        out_specs=pl.BlockSpec((tm, D_MODEL), lambda i: (i, 0)),
        out_shape=jax.ShapeDtypeStruct((n_steps * tm, D_MODEL), F32),
        scratch_shapes=[pltpu.VMEM((tm, D_FF), BF16)],
        compiler_params=_cparams(1, VMEM_LIMIT),
        name="mix_ffn_ln",
    )(x, a1, a2, w_out, *ln1, *ffn_w, *ln2)


def _proj_body(x_ref, w_ref, o_ref):
    o_ref[...] = jnp.dot(x_ref[...].astype(BF16), w_ref[...], preferred_element_type=F32)


def _proj(x, w, tm):
    rows, n = x.shape[0], w.shape[1]
    return pl.pallas_call(
        _proj_body,
        grid=(rows // tm,),
        in_specs=[pl.BlockSpec((tm, D_MODEL), lambda i: (i, 0)), _const_spec(w.shape)],
        out_specs=pl.BlockSpec((tm, n), lambda i: (i, 0)),
        out_shape=jax.ShapeDtypeStruct((rows, n), F32),
        compiler_params=_cparams(1, VMEM_LIMIT),
        name="in_proj_even",
    )(x, w)


def _proj_odd_body(x_ref, w_ref, h_ref, q_ref, k_ref, v_ref, *, attn_layout):
    y = jnp.dot(x_ref[...].astype(BF16), w_ref[...], preferred_element_type=F32)
    h_ref[...] = y[:, FOX_W:]
    k_ref[...] = y[:, FOX_W:2 * FOX_W].astype(BF16)
    q = y[:, :FOX_W] * LOG2E
    if attn_layout:
        q_ref[0] = q.T.astype(BF16)
        v_t = y[:, 2 * FOX_W:3 * FOX_W].T
        for hd in range(FOX_HEADS):
            v_ref[0, hd, 0] = v_t[hd * FOX_DIM:(hd + 1) * FOX_DIM, :].astype(BF16)
    else:
        q_ref[...] = q.astype(BF16)
        v_ref[...] = y[:, 2 * FOX_W:3 * FOX_W].astype(BF16)


def _proj_odd(x, w, bsz, tm, attn_layout):
    rows = x.shape[0]
    seq = rows // bsz
    row_spec = lambda n: pl.BlockSpec((tm, n), lambda i: (i, 0))
    if attn_layout:
        assert tm == TILE and seq % TILE == 0
        n_t = seq // TILE
        q_shape = jax.ShapeDtypeStruct((bsz, FOX_W, seq), BF16)
        q_spec = pl.BlockSpec((1, FOX_W, TILE), lambda i: (i // n_t, 0, i % n_t))
        v_shape = jax.ShapeDtypeStruct((bsz, FOX_HEADS, n_t, FOX_DIM, TILE), BF16)
        v_spec = pl.BlockSpec((1, FOX_HEADS, 1, FOX_DIM, TILE), lambda i: (i // n_t, 0, i % n_t, 0, 0))
    else:
        q_shape = v_shape = jax.ShapeDtypeStruct((rows, FOX_W), BF16)
        q_spec = v_spec = row_spec(FOX_W)
    return pl.pallas_call(
        functools.partial(_proj_odd_body, attn_layout=attn_layout),
        grid=(rows // tm,),
        in_specs=[row_spec(D_MODEL), _const_spec(w.shape)],
        out_specs=[row_spec(ODD_H), q_spec, row_spec(FOX_W), v_spec],
        out_shape=[jax.ShapeDtypeStruct((rows, ODD_H), F32), q_shape,
                   jax.ShapeDtypeStruct((rows, FOX_W), BF16), v_shape],
        compiler_params=_cparams(1, VMEM_LIMIT),
        name="in_proj_odd",
    )(x, w)


def _s5_body(u_ref, h0r_ref, h0i_ref, ar_ref, ai_ref, asr_ref, asi_ref, bbig_ref, cbig_ref, d_ref, wglu_ref,
             bglu_ref, o_ref, hlr_ref, hli_ref, hs_ref, st_ref, *, first_row, last_seg):
    t = pl.program_id(1)
    rt = u_ref.shape[1]
    seg = rt // SUBLANE

    @pl.when(t == 0)
    def _():
        st_ref[0:1, :] = h0r_ref[0]
        st_ref[1:2, :] = h0i_ref[0]

    u = u_ref[0]
    if first_row:
        rows = t * rt + lax.broadcasted_iota(jnp.int32, (rt, 1), 0)
        u = jnp.where(rows >= first_row, u, 0.0)
    i0 = lax.broadcasted_iota(jnp.int32, (rt, rt), 0)
    i1 = lax.broadcasted_iota(jnp.int32, (rt, rt), 1)
    regroup = (i1 == (i0 & (SUBLANE - 1)) * seg + (i0 >> 3)).astype(BF16)
    restore = (i0 == (i1 & (SUBLANE - 1)) * seg + (i1 >> 3)).astype(BF16)
    ub = jnp.dot(regroup, u.astype(BF16), preferred_element_type=F32).astype(BF16)

    half_w, half_s = S5_WIDTH // 2, S5_HALF // 2
    for kb in range(2):
        for part in range(2):
            c0 = part * S5_HALF + kb * half_s
            hs_ref[:, c0:c0 + half_s] = jnp.dot(
                ub[:, kb * half_w:(kb + 1) * half_w], bbig_ref[kb * half_w:(kb + 1) * half_w, c0:c0 + half_s],
                preferred_element_type=F32)

    ar = jnp.broadcast_to(ar_ref[...], (SUBLANE, S5_HALF))
    ai = jnp.broadcast_to(ai_ref[...], (SUBLANE, S5_HALF))

    def step(k, carry, store):
        hr, hi = carry
        base = pl.multiple_of(k * SUBLANE, SUBLANE)
        blk = hs_ref[pl.ds(base, SUBLANE), :]
        nr = ar * hr - ai * hi + blk[:, :S5_HALF]
        ni = ar * hi + ai * hr + blk[:, S5_HALF:]
        if store:
            hs_ref[pl.ds(base, SUBLANE), :] = jnp.concatenate([nr, ni], axis=1)
        return nr, ni

    zero = jnp.zeros((SUBLANE, S5_HALF), F32)
    er, ei = lax.fori_loop(0, seg, functools.partial(step, store=False), (zero, zero))
    asr, asi = asr_ref[...], asi_ref[...]
    sr, si = st_ref[0:1, :], st_ref[1:2, :]
    start_r, start_i = [], []
    for j in range(SUBLANE):
        start_r.append(sr)
        start_i.append(si)
        sr, si = asr * sr - asi * si + er[j:j + 1], asr * si + asi * sr + ei[j:j + 1]
    st_ref[0:1, :] = sr
    st_ref[1:2, :] = si
    lax.fori_loop(0, seg, functools.partial(step, store=True),
                  (jnp.concatenate(start_r, axis=0), jnp.concatenate(start_i, axis=0)))

    hb = hs_ref[...].astype(BF16)
    ys = []
    for kb in range(2):
        re_rows = slice(kb * half_s, (kb + 1) * half_s)
        im_rows = slice(S5_HALF + kb * half_s, S5_HALF + (kb + 1) * half_s)
        cols = slice(kb * half_w, (kb + 1) * half_w)
        ys.append(jnp.dot(hb[:, re_rows], cbig_ref[re_rows, cols], preferred_element_type=F32)
                  + jnp.dot(hb[:, im_rows], cbig_ref[im_rows, cols], preferred_element_type=F32))
    y = jnp.concatenate(ys, axis=1)
    y_hi = y.astype(BF16)
    y_lo = (y - y_hi.astype(F32)).astype(BF16)
    y = (jnp.dot(restore, y_hi, preferred_element_type=F32) + jnp.dot(restore, y_lo, preferred_element_type=F32)
         + d_ref[...] * u)
    g = jax.nn.gelu(y)
    gate = jnp.dot(g.astype(BF16), wglu_ref[...], preferred_element_type=F32) + bglu_ref[...]
    o_ref[0] = (g * jax.nn.sigmoid(gate)).astype(o_ref.dtype)

    @pl.when(t == pl.num_programs(1) - 1)
    def _():
        hlr_ref[0] = sr if last_seg == SUBLANE else start_r[last_seg]
        hli_ref[0] = si if last_seg == SUBLANE else start_i[last_seg]


def _s5(h, h0r, h0i, p, rt, first_row, n_real):
    bsz, seq, _ = h.shape
    seg = rt // SUBLANE
    real_in_last = n_real - (seq - rt)
    assert 0 < real_in_last <= rt and real_in_last % seg == 0
    as_re = (jnp.exp(p["lam_dt_re"] * seg) * jnp.cos(p["lam_dt_im"] * seg)).reshape(1, S5_HALF)
    as_im = (jnp.exp(p["lam_dt_re"] * seg) * jnp.sin(p["lam_dt_im"] * seg)).reshape(1, S5_HALF)
    state_spec = pl.BlockSpec((1, 1, S5_HALF), lambda b, t: (b, 0, 0))
    return pl.pallas_call(
        functools.partial(_s5_body, first_row=first_row, last_seg=real_in_last // seg),
        grid=(bsz, seq // rt),
        in_specs=[
            pl.BlockSpec((1, rt, S5_WIDTH), lambda b, t: (b, t, 0)),
            state_spec, state_spec,
            _const_spec((1, S5_HALF)), _const_spec((1, S5_HALF)), _const_spec((1, S5_HALF)), _const_spec((1, S5_HALF)),
            _const_spec((S5_WIDTH, 2 * S5_HALF)), _const_spec((2 * S5_HALF, S5_WIDTH)),
            _const_spec((1, S5_WIDTH)), _const_spec((S5_WIDTH, S5_WIDTH)), _const_spec((1, S5_WIDTH)),
        ],
        out_specs=[pl.BlockSpec((1, rt, S5_WIDTH), lambda b, t: (b, t, 0)), state_spec, state_spec],
        out_shape=[jax.ShapeDtypeStruct((bsz, seq, S5_WIDTH), BF16),
                   jax.ShapeDtypeStruct((bsz, 1, S5_HALF), F32),
                   jax.ShapeDtypeStruct((bsz, 1, S5_HALF), F32)],
        scratch_shapes=[pltpu.VMEM((rt, 2 * S5_HALF), F32), pltpu.VMEM((2, S5_HALF), F32)],
        compiler_params=_cparams(2, VMEM_LIMIT),
        name="s5",
    )(h, h0r, h0i, p["a_re"], p["a_im"], as_re, as_im, p["bbig"], p["cbig"], p["d"], p["w_glu"], p["b_glu"])


def _rope_lanes(x, c, s1, s2, half):
    n = x.shape[-1]
    return x * c + pltpu.roll(x, n - half, 1) * s1 + pltpu.roll(x, half, 1) * s2


def _mla_rows_body(h_ref, qn_ref, kn_ref, wq_ref, *refs, attn_layout, scale):
    q_tabs, (ck_ref, s1k_ref, s2k_ref, q_ref, ckv_ref, kpe_ref) = refs[:-6], refs[-6:]
    h = h_ref[0]
    q_lat = h[:, :MLA_Q_RANK]
    q_lat = (q_lat * lax.rsqrt(jnp.mean(q_lat * q_lat, axis=-1, keepdims=True) + EPS) * qn_ref[...]).astype(BF16)
    half = MLA_ROPE // 2
    if attn_layout:
        cos_t, sin_t = q_tabs[0][...], q_tabs[1][...]
        q_t = _dot_nt(wq_ref[...], q_lat)
        for hd in range(MLA_HEADS):
            r0 = hd * MLA_QK_PAD
            x1 = q_t[r0 + MLA_NOPE:r0 + MLA_NOPE + half, :]
            x2 = q_t[r0 + MLA_NOPE + half:r0 + MLA_NOPE + 2 * half, :]
            q_ref[0, r0:r0 + MLA_QK_PAD, :] = jnp.concatenate(
                [q_t[r0:r0 + MLA_NOPE, :] * scale, x1 * cos_t - x2 * sin_t, x1 * sin_t + x2 * cos_t,
                 q_t[r0 + MLA_NOPE + 2 * half:r0 + MLA_QK_PAD, :]], axis=0).astype(BF16)
    else:
        cq, s1q, s2q = (t[...] for t in q_tabs)
        q = jnp.dot(q_lat, wq_ref[...], preferred_element_type=F32)
        for hd in range(MLA_HEADS):
            sl = slice(hd * MLA_QK_PAD, (hd + 1) * MLA_QK_PAD)
            q_ref[0, :, sl] = _rope_lanes(q[:, sl], cq, s1q, s2q, half).astype(BF16)
    c_kv = h[:, MLA_Q_RANK:MLA_Q_RANK + MLA_KV_RANK]
    ckv_ref[0] = c_kv * lax.rsqrt(jnp.mean(c_kv * c_kv, axis=-1, keepdims=True) + EPS) * kn_ref[...]
    k_pe = h[:, MLA_Q_RANK + MLA_KV_RANK:]
    kpe_ref[0] = _rope_lanes(k_pe, ck_ref[...], s1k_ref[...], s2k_ref[...], MLA_ROPE // 2)[:, :MLA_ROPE]


def _mla_rows(h, p, pos, tm, attn_layout):
    bsz, seq, _ = h.shape
    tab_spec = pl.BlockSpec((tm, LANE), lambda b, t: (t, 0))
    n_q = MLA_HEADS * MLA_QK_PAD
    half = MLA_ROPE // 2
    scale = (MLA_NOPE + MLA_ROPE) ** -0.5 * LOG2E
    k_tabs = _rope_tables(pos, half, LANE, LANE, 0)
    if attn_layout:
        q_shape, q_spec = (bsz, n_q, seq), pl.BlockSpec((1, n_q, tm), lambda b, t: (b, 0, t))
        ang = (ROPE_BASE ** (-jnp.arange(half, dtype=F32) / half))[:, None] * pos.astype(F32)[None, :]
        q_tabs = (jnp.cos(ang) * scale, jnp.sin(ang) * scale)
        q_tab_specs = [pl.BlockSpec((half, tm), lambda b, t: (0, t))] * 2
        wq = p["wq"].T
    else:
        q_shape, q_spec = (bsz, seq, n_q), pl.BlockSpec((1, tm, n_q), lambda b, t: (b, t, 0))
        q_tabs = _rope_tables(pos, half, MLA_QK_PAD, MLA_QK_PAD, MLA_NOPE, scale=scale)
        q_tab_specs = [tab_spec] * 3
        wq = p["wq"]
    return pl.pallas_call(
        functools.partial(_mla_rows_body, attn_layout=attn_layout, scale=scale),
        grid=(bsz, seq // tm),
        in_specs=[
            pl.BlockSpec((1, tm, EVEN_IN_PAD - S5_WIDTH), lambda b, t: (b, t, 1)),
            _const_spec((1, MLA_Q_RANK)), _const_spec((1, MLA_KV_RANK)),
            _const_spec(wq.shape),
        ] + q_tab_specs + [tab_spec] * 3,
        out_specs=[
            q_spec,
            pl.BlockSpec((1, tm, MLA_KV_RANK), lambda b, t: (b, t, 0)),
            pl.BlockSpec((1, tm, MLA_ROPE), lambda b, t: (b, t, 0)),
        ],
        out_shape=[jax.ShapeDtypeStruct(q_shape, BF16),
                   jax.ShapeDtypeStruct((bsz, seq, MLA_KV_RANK), F32),
                   jax.ShapeDtypeStruct((bsz, seq, MLA_ROPE), F32)],
        compiler_params=_cparams(2, VMEM_LIMIT),
        name="mla_rows",
    )(h, p["q_norm"], p["kv_norm"], wq, *q_tabs, *k_tabs)


def _mla_kv_body(ckv_ref, kpe_ref, wkn_ref, wkp_ref, wv_ref, k_ref, v_ref):
    ckv = ckv_ref[0].astype(BF16)
    k = jnp.dot(ckv, wkn_ref[...], preferred_element_type=F32)
    k = k + jnp.dot(kpe_ref[0].astype(BF16), wkp_ref[...], preferred_element_type=F32)
    k_ref[0] = k.astype(BF16)
    v_t = jnp.dot(ckv, wv_ref[...], preferred_element_type=F32).T
    for hd in range(MLA_HEADS):
        v_ref[0, hd, 0] = v_t[hd * MLA_V:(hd + 1) * MLA_V, :].astype(BF16)


def _mla_kv(ckv, kpe, p):
    bsz, seq, _ = ckv.shape
    n_t = seq // TILE
    return pl.pallas_call(
        _mla_kv_body,
        grid=(bsz, n_t),
        in_specs=[
            pl.BlockSpec((1, TILE, MLA_KV_RANK), lambda b, t: (b, t, 0)),
            pl.BlockSpec((1, TILE, MLA_ROPE), lambda b, t: (b, t, 0)),
            _const_spec(p["wkn"].shape), _const_spec(p["wkp"].shape), _const_spec(p["wv"].shape),
        ],
        out_specs=[pl.BlockSpec((1, TILE, MLA_HEADS * MLA_QK_PAD), lambda b, t: (b, t, 0)),
                   pl.BlockSpec((1, MLA_HEADS, 1, MLA_V, TILE), lambda b, t: (b, 0, t, 0, 0))],
        out_shape=[jax.ShapeDtypeStruct((bsz, seq, MLA_HEADS * MLA_QK_PAD), BF16),
                   jax.ShapeDtypeStruct((bsz, MLA_HEADS, n_t, MLA_V, TILE), BF16)],
        compiler_params=_cparams(2, VMEM_LIMIT),
        name="mla_kv",
    )(ckv, kpe, p["wkn"], p["wkp"], p["wv"])


def _flash_body(*refs, heads, dq, dv, causal):
    if causal:
        qt_ref, k_ref, vt_ref, fq_ref, fk_ref, o_ref = refs
    else:
        qt_ref, k_ref, vt_ref, o_ref = refs
    tq = qt_ref.shape[2]
    n_tiles, tk = vt_ref.shape[2], vt_ref.shape[4]
    r0 = pl.program_id(1) * tq
    j_last = jnp.minimum((r0 + tq - 1) // tk, n_tiles - 1)
    q_row = r0 + lax.broadcasted_iota(jnp.int32, (1, tq), 1)
    last = q_row if causal else ((q_row >> CHUNK_SHIFT) << CHUNK_SHIFT) + (CHUNK - 1)
    first_last = r0 if causal else ((r0 >> CHUNK_SHIFT) << CHUNK_SHIFT) + (CHUNK - 1)
    n_open = jnp.clip((first_last + 1) // tk, 1, j_last + 1)
    q_t = [qt_ref[0, hd * dq:(hd + 1) * dq, :] for hd in range(heads)]

    def tile(j, carry, masked):
        start = pl.multiple_of(j * tk, tk)
        if masked:
            k_row = start + lax.broadcasted_iota(jnp.int32, (tk, 1), 0)
            ok = (k_row <= last) & (k_row >= PAD)
        scores = [jnp.dot(k_ref[0, pl.ds(start, tk), hd * dq:(hd + 1) * dq], q_t[hd],
                          preferred_element_type=F32) for hd in range(heads)]
        probs = []
        for hd in range(heads):
            m, l, _ = carry[hd]
            s = scores[hd]
            if causal:
                s = s + (fq_ref[0, hd:hd + 1, :] - fk_ref[0, pl.ds(start, tk), hd:hd + 1])
            if masked:
                s = jnp.where(ok, s, NEG)
            m_new = jnp.maximum(m, jnp.max(s, axis=0, keepdims=True))
            a = jnp.exp2(m - m_new)
            p = jnp.exp2(s - m_new)
            probs.append((m_new, a * l + jnp.sum(p, axis=0, keepdims=True), a, p.astype(BF16)))
        new = []
        for hd in range(heads):
            m_new, l, a, p = probs[hd]
            acc = a * carry[hd][2] + jnp.dot(vt_ref[0, hd, j], p, preferred_element_type=F32)
            new.append((m_new, l, acc))
        return tuple(new)

    init = tuple((jnp.full((1, tq), NEG, F32), jnp.zeros((1, tq), F32), jnp.zeros((dv, tq), F32))
                 for _ in range(heads))
    carry = tile(0, init, True)
    carry = lax.fori_loop(1, n_open, functools.partial(tile, masked=False), carry)
    carry = lax.fori_loop(n_open, j_last + 1, functools.partial(tile, masked=True), carry)
    outs = [carry[hd][2] / carry[hd][1] for hd in range(heads)]
    per = LANE // dv
    for g in range(heads // per):
        o_ref[0, :, g * LANE:(g + 1) * LANE] = jnp.concatenate(
            outs[g * per:(g + 1) * per], axis=0).T.astype(o_ref.dtype)


def _flash(q_t, k, v_t, fq_t, fk, *, heads, dq, dv, causal):
    bsz, _, n_q = q_t.shape
    n_keys = k.shape[1]
    in_specs = [
        pl.BlockSpec((1, heads * dq, TILE), lambda b, i: (b, 0, i)),
        pl.BlockSpec((1, n_keys, heads * dq), lambda b, i: (b, 0, 0)),
        pl.BlockSpec((1,) + v_t.shape[1:], lambda b, i: (b, 0, 0, 0, 0)),
    ]
    args = [q_t, k, v_t]
    if causal:
        in_specs += [pl.BlockSpec((1, heads, TILE), lambda b, i: (b, 0, i)),
                     pl.BlockSpec((1, n_keys, heads), lambda b, i: (b, 0, 0))]
        args += [fq_t, fk]
    return pl.pallas_call(
        functools.partial(_flash_body, heads=heads, dq=dq, dv=dv, causal=causal),
        grid=(bsz, n_q // TILE),
        in_specs=in_specs,
        out_specs=pl.BlockSpec((1, TILE, heads * dv), lambda b, i: (b, i, 0)),
        out_shape=jax.ShapeDtypeStruct((bsz, n_q, heads * dv), BF16),
        compiler_params=_cparams(2, VMEM_LIMIT),
        name="flash_causal" if causal else "flash_chunk",
    )(*args)


def _online_softmax(s, m_ref, l_ref, idx):
    m_old = m_ref[idx]
    m_new = jnp.maximum(m_old, jnp.max(s, axis=-1, keepdims=True))
    a = jnp.exp2(m_old - m_new)
    p = jnp.exp2(s - m_new)
    m_ref[idx] = m_new
    l_ref[idx] = a * l_ref[idx] + jnp.sum(p, axis=-1, keepdims=True)
    return a, p


def _dot_nt(a, b):
    return lax.dot_general(a, b, (((1,), (1,)), ((), ())), preferred_element_type=F32)


def _fox_decode_body(q_ref, fq_ref, kc_ref, vc_ref, fkc_ref, kn_ref, vn_ref, fkn_ref, o_ref, m_ref, l_ref, acc_ref):
    t = pl.program_id(1)
    n_q = q_ref.shape[1]

    @pl.when(t == 0)
    def _():
        m_ref[...] = jnp.full(m_ref.shape, NEG, F32)
        l_ref[...] = jnp.zeros(l_ref.shape, F32)
        acc_ref[...] = jnp.zeros(acc_ref.shape, F32)

    def attend(k, v, fk_rows, ok):
        cols = [slice(hd * FOX_DIM, (hd + 1) * FOX_DIM) for hd in range(FOX_HEADS)]
        scores = [_dot_nt(q_ref[0, :, sl], k[:, sl]) for sl in cols]
        probs = []
        for hd in range(FOX_HEADS):
            s = scores[hd] + (fq_ref[0, :, hd:hd + 1] - fk_rows[hd:hd + 1, :])
            if ok is not None:
                s = jnp.where(ok, s, NEG)
            probs.append(_online_softmax(s, m_ref, l_ref, hd))
        for hd in range(FOX_HEADS):
            a, p = probs[hd]
            acc_ref[hd] = a * acc_ref[hd] + jnp.dot(p.astype(BF16), v[:, cols[hd]], preferred_element_type=F32)

    attend(kc_ref[0, 0].astype(BF16), vc_ref[0, 0].astype(BF16), fkc_ref[0], None)

    @pl.when(t == pl.num_programs(1) - 1)
    def _():
        causal = (lax.broadcasted_iota(jnp.int32, (n_q, n_q), 1) <= lax.broadcasted_iota(jnp.int32, (n_q, n_q), 0))
        attend(kn_ref[0], vn_ref[0], fkn_ref[0], causal)
        for hd in range(FOX_HEADS):
            o_ref[0, :, hd * FOX_DIM:(hd + 1) * FOX_DIM] = (acc_ref[hd] / l_ref[hd]).astype(o_ref.dtype)


def _fox_decode(q, k_new, v_new, cache_k, cache_v, layer, fcum, tk):
    bsz, n_q, width = q.shape
    n_past = cache_k.shape[2]
    fq = fcum[:, n_past:n_past + n_q]
    fk_t = fcum.transpose(0, 2, 1)
    row_spec = pl.BlockSpec((1, n_q, width), lambda b, t: (b, 0, 0))
    cache_spec = pl.BlockSpec((1, 1, tk, width), lambda b, t: (layer, b, t, 0))
    return pl.pallas_call(
        _fox_decode_body,
        grid=(bsz, n_past // tk),
        in_specs=[row_spec, pl.BlockSpec((1, n_q, FOX_HEADS), lambda b, t: (b, 0, 0)),
                  cache_spec, cache_spec, pl.BlockSpec((1, FOX_HEADS, tk), lambda b, t: (b, 0, t)),
                  row_spec, row_spec, pl.BlockSpec((1, FOX_HEADS, n_q), lambda b, t: (b, 0, 0))],
        out_specs=row_spec,
        out_shape=jax.ShapeDtypeStruct((bsz, n_q, width), BF16),
        scratch_shapes=[pltpu.VMEM((FOX_HEADS, n_q, 1), F32), pltpu.VMEM((FOX_HEADS, n_q, 1), F32),
                        pltpu.VMEM((FOX_HEADS, n_q, FOX_DIM), F32)],
        compiler_params=_cparams(2, VMEM_LIMIT),
        name="fox_decode",
    )(q, fq, cache_k, cache_v, fk_t[:, :, :n_past], k_new, v_new, fk_t[:, :, n_past:n_past + n_q])


def _mla_decode_body(q_ref, wkn_ref, wv_ref, cc_ref, pc_ref, cn_ref, pn_ref, o_ref, qa_ref, qr_ref,
                     m_ref, l_ref, acc_ref):
    t = pl.program_id(1)
    n_q = q_ref.shape[1]

    @pl.when(t == 0)
    def _():
        for hd in range(MLA_HEADS):
            c0 = hd * MLA_QK_PAD
            rows = slice(hd * n_q, (hd + 1) * n_q)
            qa_ref[rows, :] = _dot_nt(q_ref[0, :, c0:c0 + MLA_NOPE],
                                      wkn_ref[:, c0:c0 + MLA_NOPE]).astype(BF16)
            qr_ref[rows, :] = q_ref[0, :, c0 + MLA_NOPE:c0 + MLA_NOPE + MLA_ROPE]
        m_ref[...] = jnp.full(m_ref.shape, NEG, F32)
        l_ref[...] = jnp.zeros(l_ref.shape, F32)
        acc_ref[...] = jnp.zeros(acc_ref.shape, F32)

    def attend(ckv, kpe):
        s = _dot_nt(qa_ref[...], ckv) + _dot_nt(qr_ref[...], kpe)
        a, p = _online_softmax(s, m_ref, l_ref, 0)
        acc_ref[0] = a * acc_ref[0] + jnp.dot(p.astype(BF16), ckv, preferred_element_type=F32)

    attend(cc_ref[0, 0].astype(BF16), pc_ref[0, 0].astype(BF16))

    @pl.when(t == pl.num_programs(1) - 1)
    def _():
        attend(cn_ref[0].astype(BF16), pn_ref[0].astype(BF16))
        lat = (acc_ref[0] / l_ref[0]).astype(BF16)
        for hd in range(MLA_HEADS):
            cols = slice(hd * MLA_V, (hd + 1) * MLA_V)
            o_ref[0, :, cols] = jnp.dot(lat[hd * n_q:(hd + 1) * n_q, :], wv_ref[:, cols],
                                        preferred_element_type=F32).astype(o_ref.dtype)


def _mla_decode(q, ckv_new, kpe_new, cache_ckv, cache_kpe, layer, p, tk):
    bsz, n_q, _ = q.shape
    n_past = cache_ckv.shape[2]
    stacked = MLA_HEADS * n_q
    return pl.pallas_call(
        _mla_decode_body,
        grid=(bsz, n_past // tk),
        in_specs=[pl.BlockSpec((1, n_q, MLA_HEADS * MLA_QK_PAD), lambda b, t: (b, 0, 0)),
                  _const_spec(p["wkn"].shape), _const_spec(p["wv"].shape),
                  pl.BlockSpec((1, 1, tk, MLA_KV_RANK), lambda b, t: (layer, b, t, 0)),
                  pl.BlockSpec((1, 1, tk, MLA_ROPE), lambda b, t: (layer, b, t, 0)),
                  pl.BlockSpec((1, n_q, MLA_KV_RANK), lambda b, t: (b, 0, 0)),
                  pl.BlockSpec((1, n_q, MLA_ROPE), lambda b, t: (b, 0, 0))],
        out_specs=pl.BlockSpec((1, n_q, MLA_HEADS * MLA_V), lambda b, t: (b, 0, 0)),
        out_shape=jax.ShapeDtypeStruct((bsz, n_q, MLA_HEADS * MLA_V), BF16),
        scratch_shapes=[pltpu.VMEM((stacked, MLA_KV_RANK), BF16), pltpu.VMEM((stacked, MLA_ROPE), BF16),
                        pltpu.VMEM((1, stacked, 1), F32), pltpu.VMEM((1, stacked, 1), F32),
                        pltpu.VMEM((1, stacked, MLA_KV_RANK), F32)],
        compiler_params=_cparams(2, VMEM_LIMIT),
        name="mla_decode",
    )(q, p["wkn"], p["wv"], cache_ckv, cache_kpe, ckv_new, kpe_new)


def _gate_body(x_ref, b_ref, lf_ref, fc_ref, *, new_start, first_row):
    n_rows = x_ref.shape[1]
    tri = (lax.broadcasted_iota(jnp.int32, (TILE, TILE), 0)
           >= lax.broadcasted_iota(jnp.int32, (TILE, TILE), 1)).astype(F32)
    carry = jnp.zeros((1, x_ref.shape[2]), F32)
    for i in range(n_rows // TILE):
        sl = slice(i * TILE, (i + 1) * TILE)
        x = x_ref[0, sl, :]
        rows = i * TILE + lax.broadcasted_iota(jnp.int32, (TILE, 1), 0)
        z = x + b_ref[...]
        lf = jnp.where(rows >= new_start, jnp.minimum(z, 0.0) - jnp.log1p(jnp.exp(-jnp.abs(z))), x)
        if first_row:
            lf = jnp.where(rows >= first_row, lf, 0.0)
        lf_ref[0, sl, :] = lf
        cs = jnp.dot(tri, lf, preferred_element_type=F32, precision=lax.Precision.HIGHEST) + carry
        fc_ref[0, sl, :] = cs
        carry = cs[TILE - 1:TILE, :]


def _gate(x, b_f, new_start, first_row):
    bsz, n_rows, heads = x.shape
    spec = pl.BlockSpec((1, n_rows, heads), lambda b: (b, 0, 0))
    return pl.pallas_call(
        functools.partial(_gate_body, new_start=new_start, first_row=first_row),
        grid=(bsz,),
        in_specs=[spec, _const_spec((1, heads))],
        out_specs=[spec, spec],
        out_shape=[jax.ShapeDtypeStruct(x.shape, F32)] * 2,
        compiler_params=_cparams(1),
        name="fox_gate",
    )(x, b_f)


RET_LOG_GAMMA = tuple(math.log(1.0 - 2.0 ** (-5.0 - h)) for h in range(RET_HEADS))


def _ret_body(rq_ref, rk_ref, rv_ref, rg_ref, s0_ref, dec_ref, cq_ref, s1q_ref, s2q_ref, ck_ref, s1k_ref, s2k_ref,
              o_ref, sl_ref, st_ref, *, n_tail):
    c = pl.program_id(1)
    ct = rq_ref.shape[1]

    @pl.when(c == 0)
    def _():
        st_ref[...] = s0_ref[0]

    q = _rope_lanes(rq_ref[0], cq_ref[...], s1q_ref[...], s2q_ref[...], RET_DK // 2)
    k = _rope_lanes(rk_ref[0], ck_ref[...], s1k_ref[...], s2k_ref[...], RET_DK // 2)
    v = rv_ref[0].astype(BF16)
    g = rg_ref[0]
    j = lax.broadcasted_iota(jnp.int32, (ct, 1), 0).astype(F32)
    for hd in range(RET_HEADS):
        lg = RET_LOG_GAMMA[hd]
        q_h = q[:, hd * RET_DK:(hd + 1) * RET_DK]
        k_h = k[:, hd * RET_DK:(hd + 1) * RET_DK]
        v_h = v[:, hd * RET_DV:(hd + 1) * RET_DV]
        scores = lax.dot_general(q_h.astype(BF16), k_h.astype(BF16), (((1,), (1,)), ((), ())),
                                 preferred_element_type=F32) * dec_ref[hd]
        s_h = st_ref[hd]
        out = jnp.dot(scores.astype(BF16), v_h, preferred_element_type=F32)
        out = out + jnp.dot((q_h * jnp.exp(lg * (j + 1.0))).astype(BF16), s_h.astype(BF16),
                            preferred_element_type=F32)
        k_dec = (k_h * jnp.exp(lg * (ct - 1.0 - j))).astype(BF16)
        st_ref[hd] = math.exp(lg * ct) * s_h + lax.dot_general(
            k_dec, v_h, (((0,), (0,)), ((), ())), preferred_element_type=F32)
        mu = jnp.mean(out, axis=-1, keepdims=True)
        oc = out - mu
        var = jnp.mean(oc * oc, axis=-1, keepdims=True)
        g_h = g[:, hd * RET_DV:(hd + 1) * RET_DV]
        o_ref[0, :, hd * RET_DV:(hd + 1) * RET_DV] = (
            g_h * jax.nn.sigmoid(g_h) * (oc * lax.rsqrt(var + EPS))).astype(o_ref.dtype)

    @pl.when(c == pl.num_programs(1) - 1)
    def _():
        for hd in range(RET_HEADS):
            sl_ref[0, hd] = st_ref[hd] * math.exp(-RET_LOG_GAMMA[hd] * n_tail)


def _retention(h, s0, tabs, ct, n_tail):
    bsz, seq, _ = h.shape
    tab_spec = pl.BlockSpec((ct, RET_QK), lambda b, c: (c, 0))
    st_spec = pl.BlockSpec((1, RET_HEADS, RET_DK, RET_DV), lambda b, c: (b, 0, 0, 0))
    rq_blk = (2 * FOX_W) // RET_QK
    rv_blk = (2 * FOX_W + 2 * RET_QK) // RET_VW
    diff = jnp.arange(ct, dtype=F32)[:, None] - jnp.arange(ct, dtype=F32)[None, :]
    decay = jnp.where(diff >= 0.0, jnp.exp(jnp.asarray(RET_LOG_GAMMA, F32)[:, None, None] * jnp.maximum(diff, 0.0)),
                      0.0)
    return pl.pallas_call(
        functools.partial(_ret_body, n_tail=n_tail),
        grid=(bsz, seq // ct),
        in_specs=[
            pl.BlockSpec((1, ct, RET_QK), lambda b, c: (b, c, rq_blk)),
            pl.BlockSpec((1, ct, RET_QK), lambda b, c: (b, c, rq_blk + 1)),
            pl.BlockSpec((1, ct, RET_VW), lambda b, c: (b, c, rv_blk)),
            pl.BlockSpec((1, ct, RET_VW), lambda b, c: (b, c, rv_blk + 1)),
            st_spec, _const_spec((RET_HEADS, ct, ct)),
        ] + [tab_spec] * 6,
        out_specs=[pl.BlockSpec((1, ct, RET_VW), lambda b, c: (b, c, 0)), st_spec],
        out_shape=[jax.ShapeDtypeStruct((bsz, seq, RET_VW), BF16),
                   jax.ShapeDtypeStruct((bsz, RET_HEADS, RET_DK, RET_DV), F32)],
        scratch_shapes=[pltpu.VMEM((RET_HEADS, RET_DK, RET_DV), F32)],
        compiler_params=_cparams(2, VMEM_LIMIT),
        name="retention",
    )(h, h, h, h, s0, decay, *tabs)


def _rope_tables(pos, half, width, group, offset, scale=1.0, valid=None):
    inv = ROPE_BASE ** (-jnp.arange(half, dtype=F32) / half)
    ang = pos.astype(F32)[:, None] * inv[None, :]
    cos, sin = jnp.cos(ang), jnp.sin(ang)
    n = pos.shape[0]
    one, zero = jnp.ones((n, 1), F32), jnp.zeros((n, 1), F32)

    def lanes(first, second, other):
        grp = jnp.concatenate([jnp.broadcast_to(other, (n, offset)), first, second,
                               jnp.broadcast_to(other, (n, group - offset - 2 * half))], axis=1)
        return jnp.tile(grp, (1, width // group))

    tabs = (lanes(cos, cos, one), lanes(-sin, 0.0 * sin, zero), lanes(0.0 * sin, sin, zero))
    if valid is not None:
        tabs = tuple(jnp.where(valid[:, None], t, 0.0) for t in tabs)
    return tuple(t * scale for t in tabs)


def _even_params(e, w):
    p = {}
    p["w_in"] = jnp.pad(w["even_w_in"][e], ((0, 0), (0, EVEN_IN_PAD - EVEN_IN))).astype(BF16)
    p["w_out"] = w["even_w_out"][e].astype(BF16)
    lam_re, lam_im = w["s5_a_re"][e].astype(F32), w["s5_a_im"][e].astype(F32)
    dt = jnp.exp(w["s5_log_dt"][e].astype(F32))[:, None]
    p["lam_dt_re"], p["lam_dt_im"] = lam_re * dt, lam_im * dt
    mag = jnp.exp(lam_re * dt)
    abar_re, abar_im = mag * jnp.cos(lam_im * dt), mag * jnp.sin(lam_im * dt)
    den = lam_re * lam_re + lam_im * lam_im
    f_re = ((abar_re - 1.0) * lam_re + abar_im * lam_im) / den
    f_im = (abar_im * lam_re - (abar_re - 1.0) * lam_im) / den
    b_re, b_im = w["s5_b_re"][e].astype(F32), w["s5_b_im"][e].astype(F32)
    bb_re = f_re[..., None] * b_re - f_im[..., None] * b_im
    bb_im = f_re[..., None] * b_im + f_im[..., None] * b_re
    eye = jnp.eye(S5_GROUPS, dtype=F32)

    def in_blocks(x):
        return jnp.einsum("gnc,gh->gchn", x, eye).reshape(S5_WIDTH, S5_HALF)

    def out_blocks(x):
        return jnp.einsum("gcn,gh->gnhc", x, eye).reshape(S5_HALF, S5_WIDTH)

    p["bbig"] = jnp.concatenate([in_blocks(bb_re), in_blocks(bb_im)], axis=1).astype(BF16)
    p["cbig"] = jnp.concatenate([out_blocks(w["s5_c_re"][e].astype(F32)),
                                 out_blocks(-w["s5_c_im"][e].astype(F32))], axis=0).astype(BF16)
    p["a_re"] = abar_re.reshape(1, S5_HALF)
    p["a_im"] = abar_im.reshape(1, S5_HALF)
    p["d"] = w["s5_d"][e].astype(F32).reshape(1, S5_WIDTH)
    p["w_glu"] = w["s5_w_glu"][e].astype(BF16)
    p["b_glu"] = w["s5_b_glu"][e].astype(F32).reshape(1, S5_WIDTH)
    p["q_norm"] = w["mla_q_norm"][e].astype(F32).reshape(1, MLA_Q_RANK)
    p["kv_norm"] = w["mla_kv_norm"][e].astype(F32).reshape(1, MLA_KV_RANK)
    wq = w["mla_w_uq"][e].reshape(MLA_Q_RANK, MLA_HEADS, MLA_NOPE + MLA_ROPE)
    wq = jnp.pad(wq, ((0, 0), (0, 0), (0, MLA_QK_PAD - MLA_NOPE - MLA_ROPE)))
    p["wq"] = wq.reshape(MLA_Q_RANK, MLA_HEADS * MLA_QK_PAD).astype(BF16)
    wkv = w["mla_w_ukv"][e].reshape(MLA_KV_RANK, MLA_HEADS, MLA_NOPE + MLA_V)
    wkn = jnp.pad(wkv[:, :, :MLA_NOPE], ((0, 0), (0, 0), (0, MLA_QK_PAD - MLA_NOPE)))
    p["wkn"] = wkn.reshape(MLA_KV_RANK, MLA_HEADS * MLA_QK_PAD).astype(BF16)
    place = jnp.pad(jnp.eye(MLA_ROPE, dtype=F32), ((0, 0), (MLA_NOPE, MLA_QK_PAD - MLA_NOPE - MLA_ROPE)))
    p["wkp"] = jnp.tile(place, (1, MLA_HEADS)).astype(BF16)
    p["wv"] = wkv[:, :, MLA_NOPE:].reshape(MLA_KV_RANK, MLA_HEADS * MLA_V).astype(BF16)
    return p


def _odd_params(o, w):
    w_in = w["odd_w_in"][o]
    c_logit = 3 * FOX_W
    cols = jnp.concatenate([
        w_in[:, :FOX_W] * (FOX_DIM ** -0.5),
        w_in[:, FOX_W:c_logit],
        w_in[:, c_logit + FOX_HEADS:],
        w_in[:, c_logit:c_logit + FOX_HEADS],
    ], axis=1)
    p = {"w_in": jnp.pad(cols, ((0, 0), (0, ODD_IN_PAD - cols.shape[1]))).astype(BF16)}
    p["w_out"] = w["odd_w_out"][o].astype(BF16)
    p["b_f"] = w["fox_b_f"][o].astype(F32).reshape(1, FOX_HEADS)
    return p


def _cache_tile(n_past):
    for tk in (512, 256, 128):
        if n_past % tk == 0:
            return tk
    raise ValueError("cache length must be a multiple of 128")


def _trunk(x, pos, n_real, out_rows, past, w, ffn, evens, odds):
    bsz, seq, _ = x.shape
    rows = bsz * seq
    prompt = past is None
    first_row = PAD if prompt else 0
    if prompt:
        tm = 2 * TILE if rows % (2 * TILE) == 0 else TILE
        rt = TILE
    else:
        n_past = past["cache_fox_k"].shape[2]
        tk = _cache_tile(n_past)
        cache_k = past["cache_fox_k"].reshape(past["cache_fox_k"].shape[:3] + (FOX_W,))
        cache_v = past["cache_fox_v"].reshape(past["cache_fox_v"].shape[:3] + (FOX_W,))
        tm, rt = rows, seq
    idx = jnp.arange(seq, dtype=jnp.int32)
    valid = (idx >= first_row) & (idx < n_real)
    st = {n: [] for n in ("mla_ckv", "mla_kpe", "s5_re", "s5_im", "fox_k", "fox_v", "fox_logf", "ret")}
    rq_tabs = _rope_tables(pos, RET_DK // 2, RET_QK, RET_DK, 0)
    rk_tabs = _rope_tables(pos, RET_DK // 2, RET_QK, RET_DK, 0, scale=RET_DK ** -0.5, valid=valid)
    x2 = x.reshape(rows, D_MODEL)

    def ln(l, i):
        return w["ln_g"][l, i].reshape(1, D_MODEL), w["ln_b"][l, i].reshape(1, D_MODEL)

    for l in range(DEPTH):
        x2 = _ffn_ln(x2, *ffn[l][0], *ln(l, 0), tm)
        if l % 2 == 0:
            e = l // 2
            p = evens[e]
            h = _proj(x2, p["w_in"], tm).reshape(bsz, seq, EVEN_IN_PAD)
            if prompt:
                h0r = h0i = jnp.zeros((bsz, 1, S5_HALF), F32)
            else:
                h0r = past["state_s5_re"][e].astype(F32).reshape(bsz, 1, S5_HALF)
                h0i = past["state_s5_im"][e].astype(F32).reshape(bsz, 1, S5_HALF)
            s5_out, hlr, hli = _s5(h, h0r, h0i, p, rt, first_row, n_real)
            q, ckv, kpe = _mla_rows(h, p, pos, rt, attn_layout=prompt)
            if prompt:
                k_att, v_att = _mla_kv(ckv, kpe, p)
                mla_out = _flash(q, k_att, v_att, None, None, heads=MLA_HEADS, dq=MLA_QK_PAD, dv=MLA_V,
                                 causal=False)
            else:
                mla_out = _mla_decode(q, ckv, kpe, past["cache_mla_ckv"], past["cache_mla_kpe"], e, p, tk)
            mix = (s5_out.reshape(rows, S5_WIDTH), mla_out.reshape(rows, MLA_HEADS * MLA_V), p["w_out"])
            st["mla_ckv"].append(ckv)
            st["mla_kpe"].append(kpe)
            st["s5_re"].append(hlr.reshape(bsz, S5_GROUPS, S5_STATE))
            st["s5_im"].append(hli.reshape(bsz, S5_GROUPS, S5_STATE))
        else:
            o = l // 2
            p = odds[o]
            h, fq, fk16, fv = _proj_odd(x2, p["w_in"], bsz, TILE if prompt else tm, attn_layout=prompt)
            h = h.reshape(bsz, seq, ODD_H)
            fk16 = fk16.reshape(bsz, seq, FOX_W)
            f_logit = h[:, :, ODD_LOGIT_COL:ODD_LOGIT_COL + FOX_HEADS]
            if prompt:
                q_off = 0
                logf, fcum = _gate(f_logit, p["b_f"], 0, PAD)
                fcum = fcum * LOG2E
                fox_out = _flash(fq, fk16, fv, fcum.transpose(0, 2, 1), fcum, heads=FOX_HEADS, dq=FOX_DIM,
                                 dv=FOX_DIM, causal=True)
                s0 = jnp.zeros((bsz, RET_HEADS, RET_DK, RET_DV), F32)
            else:
                q_off = n_past
                gates = jnp.concatenate([past["cache_fox_logf"][o].astype(F32), f_logit], axis=1)
                gates = jnp.pad(gates, ((0, 0), (0, _round_up(n_past + seq, TILE) - n_past - seq), (0, 0)))
                logf, fcum = _gate(gates, p["b_f"], n_past, 0)
                fox_out = _fox_decode(fq.reshape(bsz, seq, FOX_W), fk16, fv.reshape(bsz, seq, FOX_W),
                                      cache_k, cache_v, o, fcum * LOG2E, tk)
                s0 = past["state_ret"][o].astype(F32)
            ret_out, s_last = _retention(h, s0, rq_tabs + rk_tabs, rt, seq - n_real)
            mix = (fox_out.reshape(rows, FOX_W), ret_out.reshape(rows, RET_VW), p["w_out"])
            st["fox_k"].append(h[:, :, :FOX_W].reshape(bsz, seq, FOX_HEADS, FOX_DIM))
            st["fox_v"].append(h[:, :, FOX_W:2 * FOX_W].reshape(bsz, seq, FOX_HEADS, FOX_DIM))
            st["fox_logf"].append(logf[:, q_off:q_off + seq])
            st["ret"].append(s_last)
        keep = (seq,) + out_rows if l == DEPTH - 1 and out_rows != (0, seq) else None
        x2 = _mix_ffn_ln(x2, *mix, ln(l, 1), ffn[l][1], ln(l, 2), tm, keep)
    return x2.reshape(bsz, out_rows[1], D_MODEL), {n: jnp.stack(a) for n, a in st.items()}


def kernel(x_prompt, x_sample, cache_mla_ckv, cache_mla_kpe, cache_fox_k, cache_fox_v, cache_fox_logf,
           state_s5_re, state_s5_im, state_ret, meta_tokens, ln_g, ln_b, ffn_w_gate, ffn_w_up, ffn_w_down,
           even_w_in, even_w_out, s5_a_re, s5_a_im, s5_b_re, s5_b_im, s5_c_re, s5_c_im, s5_d, s5_log_dt,
           s5_w_glu, s5_b_glu, mla_q_norm, mla_kv_norm, mla_w_uq, mla_w_ukv, odd_w_in, odd_w_out, fox_b_f):
    w = dict(ln_g=ln_g.astype(F32), ln_b=ln_b.astype(F32), even_w_in=even_w_in, even_w_out=even_w_out,
             s5_a_re=s5_a_re, s5_a_im=s5_a_im, s5_b_re=s5_b_re, s5_b_im=s5_b_im, s5_c_re=s5_c_re,
             s5_c_im=s5_c_im, s5_d=s5_d, s5_log_dt=s5_log_dt, s5_w_glu=s5_w_glu, s5_b_glu=s5_b_glu,
             mla_q_norm=mla_q_norm, mla_kv_norm=mla_kv_norm, mla_w_uq=mla_w_uq, mla_w_ukv=mla_w_ukv,
             odd_w_in=odd_w_in, odd_w_out=odd_w_out, fox_b_f=fox_b_f)
    past = dict(cache_mla_ckv=cache_mla_ckv, cache_mla_kpe=cache_mla_kpe, cache_fox_k=cache_fox_k,
                cache_fox_v=cache_fox_v, cache_fox_logf=cache_fox_logf, state_s5_re=state_s5_re,
                state_s5_im=state_s5_im, state_ret=state_ret)
    ffn = [[(ffn_w_gate[l, i].astype(BF16), ffn_w_up[l, i].astype(BF16), ffn_w_down[l, i].astype(BF16))
            for i in range(2)] for l in range(DEPTH)]
    evens = [_even_params(e, w) for e in range((DEPTH + 1) // 2)]
    odds = [_odd_params(o, w) for o in range(DEPTH // 2)]

    bsz, seq, _ = x_prompt.shape
    n_real = PAD + N_META + seq
    n_rows = _round_up(n_real, TILE)
    meta = jnp.broadcast_to(meta_tokens[None].astype(x_prompt.dtype), (bsz, N_META, D_MODEL))
    xp = jnp.concatenate([jnp.zeros((bsz, PAD, D_MODEL), x_prompt.dtype), meta, x_prompt,
                          jnp.zeros((bsz, n_rows - n_real, D_MODEL), x_prompt.dtype)], axis=1)
    pos_p = jnp.maximum(jnp.arange(n_rows, dtype=jnp.int32) - PAD, 0)
    y_p, st_p = _trunk(xp, pos_p, n_real, (PAD + N_META, seq), None, w, ffn, evens, odds)
    d_seq = x_sample.shape[1]
    pos_s = N_META + cache_fox_k.shape[2] + jnp.arange(d_seq, dtype=jnp.int32)
    y_s, st_s = _trunk(x_sample, pos_s, d_seq, (0, d_seq), past, w, ffn, evens, odds)

    def real(a):
        return a[:, :, PAD:n_real]

    return (y_p, y_s,
            real(st_p["mla_ckv"]), real(st_p["mla_kpe"]), real(st_p["fox_k"]), real(st_p["fox_v"]),
            real(st_p["fox_logf"]), st_p["s5_re"], st_p["s5_im"], st_p["ret"],
            st_s["mla_ckv"], st_s["mla_kpe"], st_s["fox_k"], st_s["fox_v"], st_s["fox_logf"],
            st_s["s5_re"], st_s["s5_im"], st_s["ret"])
```

```python
import functools
import math

import jax
import jax.numpy as jnp
from jax import lax
from jax.experimental import pallas as pl
from jax.experimental.pallas import tpu as pltpu

F32 = jnp.float32
BF16 = jnp.bfloat16

D_MODEL = 1024
DEPTH = 4
CHUNK = 64
CHUNK_SHIFT = 6
N_META = 16
S5_WIDTH = 512
S5_CH = 16
S5_GROUPS = S5_WIDTH // S5_CH
S5_STATE = 64
S5_HALF = S5_GROUPS * S5_STATE
MLA_HEADS = 8
MLA_Q_RANK = 256
MLA_KV_RANK = 128
MLA_NOPE = 64
MLA_ROPE = 32
MLA_V = 64
MLA_QK_PAD = 128
FOX_HEADS = 8
FOX_DIM = 64
FOX_W = FOX_HEADS * FOX_DIM
RET_HEADS = 4
RET_DK = 64
RET_DV = 128
RET_QK = RET_HEADS * RET_DK
RET_VW = RET_HEADS * RET_DV
D_FF = 2816
ROPE_BASE = 10000.0
ALPHA = (2.0 * DEPTH) ** 0.25
EPS = 1e-5
NEG = -1e30
LOG2E = math.log2(math.e)
EVEN_IN = S5_WIDTH + MLA_Q_RANK + MLA_KV_RANK + MLA_ROPE
EVEN_IN_PAD = 1024
ODD_IN_PAD = 3200
ODD_H = ODD_IN_PAD - FOX_W
ODD_LOGIT_COL = 2 * FOX_W + 2 * RET_QK + 2 * RET_VW

PAD = CHUNK - N_META
LANE = 128
SUBLANE = 8
TILE = 256
VMEM_LIMIT = 56 * 1024 * 1024


def _cparams(n_grid, vmem=None):
    return pltpu.CompilerParams(dimension_semantics=("arbitrary",) * n_grid, vmem_limit_bytes=vmem)


def _const_spec(shape):
    nd = len(shape)
    return pl.BlockSpec(shape, lambda *_: (0,) * nd)


def _round_up(n, m):
    return -(-n // m) * m


def _layer_norm_rows(z, g, b):
    mu = jnp.mean(z, axis=-1, keepdims=True)
    zc = z - mu
    var = jnp.mean(zc * zc, axis=-1, keepdims=True)
    return zc * lax.rsqrt(var + EPS) * g + b


def _ffn_rows(x, wg_ref, wu_ref, wd_ref, g_ref, b_ref, hid_ref):
    xb = x.astype(BF16)
    for c in range(D_FF // TILE):
        sl = slice(c * TILE, (c + 1) * TILE)
        hg = jnp.dot(xb, wg_ref[:, sl], preferred_element_type=F32)
        hu = jnp.dot(xb, wu_ref[:, sl], preferred_element_type=F32)
        hid_ref[:, sl] = (hg * jax.nn.sigmoid(hg) * hu).astype(BF16)
    y = jnp.dot(hid_ref[...], wd_ref[...], preferred_element_type=F32)
    return _layer_norm_rows(ALPHA * x + 0.5 * y, g_ref[...], b_ref[...])


def _ffn_body(x_ref, wg_ref, wu_ref, wd_ref, g_ref, b_ref, o_ref, hid_ref):
    o_ref[...] = _ffn_rows(x_ref[...], wg_ref, wu_ref, wd_ref, g_ref, b_ref, hid_ref)


def _ffn_weight_specs():
    once = dict(pipeline_mode=pl.Buffered(1))
    return [pl.BlockSpec((D_MODEL, D_FF), lambda i: (0, 0), **once),
            pl.BlockSpec((D_MODEL, D_FF), lambda i: (0, 0), **once),
            pl.BlockSpec((D_FF, D_MODEL), lambda i: (0, 0), **once),
            _const_spec((1, D_MODEL)), _const_spec((1, D_MODEL))]


def _ffn_ln(x, wg, wu, wd, g, b, tm):
    rows = x.shape[0]
    return pl.pallas_call(
        _ffn_body,
        grid=(rows // tm,),
        in_specs=[pl.BlockSpec((tm, D_MODEL), lambda i: (i, 0))] + _ffn_weight_specs(),
        out_specs=pl.BlockSpec((tm, D_MODEL), lambda i: (i, 0)),
        out_shape=jax.ShapeDtypeStruct((rows, D_MODEL), F32),
        scratch_shapes=[pltpu.VMEM((tm, D_FF), BF16)],
        compiler_params=_cparams(1, VMEM_LIMIT),
        name="ffn_ln",
    )(x, wg, wu, wd, g, b)


def _mix_ffn_body(x_ref, a1_ref, a2_ref, wo_ref, g1_ref, b1_ref, wg_ref, wu_ref, wd_ref, g2_ref, b2_ref,
                  o_ref, hid_ref):
    k1 = a1_ref.shape[1]
    y = jnp.dot(a1_ref[...], wo_ref[:k1, :], preferred_element_type=F32)
    y = y + jnp.dot(a2_ref[...], wo_ref[k1:, :], preferred_element_type=F32)
    x1 = _layer_norm_rows(ALPHA * x_ref[...] + y, g1_ref[...], b1_ref[...])
    o_ref[...] = _ffn_rows(x1, wg_ref, wu_ref, wd_ref, g2_ref, b2_ref, hid_ref)


def _mix_ffn_ln(x, a1, a2, w_out, ln1, ffn_w, ln2, tm, keep=None):
    rows = x.shape[0]
    if keep is None:
        n_steps = rows // tm
        row_spec = lambda n: pl.BlockSpec((tm, n), lambda i: (i, 0))
    else:
        seq, start, count = keep
        per = count // tm
        n_steps = (rows // seq) * per
        align = math.gcd(seq, start, tm)
        row_spec = lambda n: pl.BlockSpec(
            (pl.Element(tm), pl.Element(n)),
            lambda i: (pl.multiple_of((i // per) * seq + start + (i % per) * tm, align), 0))
    return pl.pallas_call(
        _mix_ffn_body,
        grid=(n_steps,),
        in_specs=[row_spec(D_MODEL), row_spec(a1.shape[1]), row_spec(a2.shape[1]),
                  pl.BlockSpec(w_out.shape, lambda i: (0, 0), pipeline_mode=pl.Buffered(1)),
                  _const_spec((1, D_MODEL)), _const_spec((1, D_MODEL))] + _ffn_weight_specs(),
        out_specs=pl.BlockSpec((tm, D_MODEL), lambda i: (i, 0)),
        out_shape=jax.ShapeDtypeStruct((n_steps * tm, D_MODEL), F32),
        scratch_shapes=[pltpu.VMEM((tm, D_FF), BF16)],
        compiler_params=_cparams(1, VMEM_LIMIT),
        name="mix_ffn_ln",
    )(x, a1, a2, w_out, *ln1, *ffn_w, *ln2)


def _proj_body(x_ref, w_ref, o_ref):
    o_ref[...] = jnp.dot(x_ref[...].astype(BF16), w_ref[...], preferred_element_type=F32)


def _proj(x, w, tm):
    rows, n = x.shape[0], w.shape[1]
    return pl.pallas_call(
        _proj_body,
        grid=(rows // tm,),
        in_specs=[pl.BlockSpec((tm, D_MODEL), lambda i: (i, 0)), _const_spec(w.shape)],
        out_specs=pl.BlockSpec((tm, n), lambda i: (i, 0)),
        out_shape=jax.ShapeDtypeStruct((rows, n), F32),
        compiler_params=_cparams(1, VMEM_LIMIT),
        name="in_proj_even",
    )(x, w)


def _proj_odd_body(x_ref, w_ref, h_ref, q_ref, k_ref, v_ref, *, attn_layout):
    y = jnp.dot(x_ref[...].astype(BF16), w_ref[...], preferred_element_type=F32)
    h_ref[...] = y[:, FOX_W:]
    k_ref[...] = y[:, FOX_W:2 * FOX_W].astype(BF16)
    q = y[:, :FOX_W] * LOG2E
    if attn_layout:
        q_ref[0] = q.T.astype(BF16)
        v_t = y[:, 2 * FOX_W:3 * FOX_W].T
        for hd in range(FOX_HEADS):
            v_ref[0, hd, 0] = v_t[hd * FOX_DIM:(hd + 1) * FOX_DIM, :].astype(BF16)
    else:
        q_ref[...] = q.astype(BF16)
        v_ref[...] = y[:, 2 * FOX_W:3 * FOX_W].astype(BF16)


def _proj_odd(x, w, bsz, tm, attn_layout):
    rows = x.shape[0]
    seq = rows // bsz
    row_spec = lambda n: pl.BlockSpec((tm, n), lambda i: (i, 0))
    if attn_layout:
        assert tm == TILE and seq % TILE == 0
        n_t = seq // TILE
        q_shape = jax.ShapeDtypeStruct((bsz, FOX_W, seq), BF16)
        q_spec = pl.BlockSpec((1, FOX_W, TILE), lambda i: (i // n_t, 0, i % n_t))
        v_shape = jax.ShapeDtypeStruct((bsz, FOX_HEADS, n_t, FOX_DIM, TILE), BF16)
        v_spec = pl.BlockSpec((1, FOX_HEADS, 1, FOX_DIM, TILE), lambda i: (i // n_t, 0, i % n_t, 0, 0))
    else:
        q_shape = v_shape = jax.ShapeDtypeStruct((rows, FOX_W), BF16)
        q_spec = v_spec = row_spec(FOX_W)
    return pl.pallas_call(
        functools.partial(_proj_odd_body, attn_layout=attn_layout),
        grid=(rows // tm,),
        in_specs=[row_spec(D_MODEL), _const_spec(w.shape)],
        out_specs=[row_spec(ODD_H), q_spec, row_spec(FOX_W), v_spec],
        out_shape=[jax.ShapeDtypeStruct((rows, ODD_H), F32), q_shape,
                   jax.ShapeDtypeStruct((rows, FOX_W), BF16), v_shape],
        compiler_params=_cparams(1, VMEM_LIMIT),
        name="in_proj_odd",
    )(x, w)


def _s5_body(u_ref, h0r_ref, h0i_ref, ar_ref, ai_ref, asr_ref, asi_ref, bbig_ref, cbig_ref, d_ref, wglu_ref,
             bglu_ref, o_ref, hlr_ref, hli_ref, hs_ref, st_ref, *, first_row, last_seg):
    t = pl.program_id(1)
    rt = u_ref.shape[1]
    seg = rt // SUBLANE

    @pl.when(t == 0)
    def _():
        st_ref[0:1, :] = h0r_ref[0]
        st_ref[1:2, :] = h0i_ref[0]

    u = u_ref[0]
    if first_row:
        rows = t * rt + lax.broadcasted_iota(jnp.int32, (rt, 1), 0)
        u = jnp.where(rows >= first_row, u, 0.0)
    i0 = lax.broadcasted_iota(jnp.int32, (rt, rt), 0)
    i1 = lax.broadcasted_iota(jnp.int32, (rt, rt), 1)
    regroup = (i1 == (i0 & (SUBLANE - 1)) * seg + (i0 >> 3)).astype(BF16)
    restore = (i0 == (i1 & (SUBLANE - 1)) * seg + (i1 >> 3)).astype(BF16)
    ub = jnp.dot(regroup, u.astype(BF16), preferred_element_type=F32).astype(BF16)

    half_w, half_s = S5_WIDTH // 2, S5_HALF // 2
    for kb in range(2):
        for part in range(2):
            c0 = part * S5_HALF + kb * half_s
            hs_ref[:, c0:c0 + half_s] = jnp.dot(
                ub[:, kb * half_w:(kb + 1) * half_w], bbig_ref[kb * half_w:(kb + 1) * half_w, c0:c0 + half_s],
                preferred_element_type=F32)

    ar = jnp.broadcast_to(ar_ref[...], (SUBLANE, S5_HALF))
    ai = jnp.broadcast_to(ai_ref[...], (SUBLANE, S5_HALF))

    def step(k, carry, store):
        hr, hi = carry
        base = pl.multiple_of(k * SUBLANE, SUBLANE)
        blk = hs_ref[pl.ds(base, SUBLANE), :]
        nr = ar * hr - ai * hi + blk[:, :S5_HALF]
        ni = ar * hi + ai * hr + blk[:, S5_HALF:]
        if store:
            hs_ref[pl.ds(base, SUBLANE), :] = jnp.concatenate([nr, ni], axis=1)
        return nr, ni

    zero = jnp.zeros((SUBLANE, S5_HALF), F32)
    er, ei = lax.fori_loop(0, seg, functools.partial(step, store=False), (zero, zero))
    asr, asi = asr_ref[...], asi_ref[...]
    sr, si = st_ref[0:1, :], st_ref[1:2, :]
    start_r, start_i = [], []
    for j in range(SUBLANE):
        start_r.append(sr)
        start_i.append(si)
        sr, si = asr * sr - asi * si + er[j:j + 1], asr * si + asi * sr + ei[j:j + 1]
    st_ref[0:1, :] = sr
    st_ref[1:2, :] = si
    lax.fori_loop(0, seg, functools.partial(step, store=True),
                  (jnp.concatenate(start_r, axis=0), jnp.concatenate(start_i, axis=0)))

    hb = hs_ref[...].astype(BF16)
    ys = []
    for kb in range(2):
        re_rows = slice(kb * half_s, (kb + 1) * half_s)
        im_rows = slice(S5_HALF + kb * half_s, S5_HALF + (kb + 1) * half_s)
        cols = slice(kb * half_w, (kb + 1) * half_w)
        ys.append(jnp.dot(hb[:, re_rows], cbig_ref[re_rows, cols], preferred_element_type=F32)
                  + jnp.dot(hb[:, im_rows], cbig_ref[im_rows, cols], preferred_element_type=F32))
    y = jnp.concatenate(ys, axis=1)
    y_hi = y.astype(BF16)
    y_lo = (y - y_hi.astype(F32)).astype(BF16)
    y = (jnp.dot(restore, y_hi, preferred_element_type=F32) + jnp.dot(restore, y_lo, preferred_element_type=F32)
         + d_ref[...] * u)
    g = jax.nn.gelu(y)
    gate = jnp.dot(g.astype(BF16), wglu_ref[...], preferred_element_type=F32) + bglu_ref[...]
    o_ref[0] = (g * jax.nn.sigmoid(gate)).astype(o_ref.dtype)

    @pl.when(t == pl.num_programs(1) - 1)
    def _():
        hlr_ref[0] = sr if last_seg == SUBLANE else start_r[last_seg]
        hli_ref[0] = si if last_seg == SUBLANE else start_i[last_seg]


def _s5(h, h0r, h0i, p, rt, first_row, n_real):
    bsz, seq, _ = h.shape
    seg = rt // SUBLANE
    real_in_last = n_real - (seq - rt)
    assert 0 < real_in_last <= rt and real_in_last % seg == 0
    as_re = (jnp.exp(p["lam_dt_re"] * seg) * jnp.cos(p["lam_dt_im"] * seg)).reshape(1, S5_HALF)
    as_im = (jnp.exp(p["lam_dt_re"] * seg) * jnp.sin(p["lam_dt_im"] * seg)).reshape(1, S5_HALF)
    state_spec = pl.BlockSpec((1, 1, S5_HALF), lambda b, t: (b, 0, 0))
    return pl.pallas_call(
        functools.partial(_s5_body, first_row=first_row, last_seg=real_in_last // seg),
        grid=(bsz, seq // rt),
        in_specs=[
            pl.BlockSpec((1, rt, S5_WIDTH), lambda b, t: (b, t, 0)),
            state_spec, state_spec,
            _const_spec((1, S5_HALF)), _const_spec((1, S5_HALF)), _const_spec((1, S5_HALF)), _const_spec((1, S5_HALF)),
            _const_spec((S5_WIDTH, 2 * S5_HALF)), _const_spec((2 * S5_HALF, S5_WIDTH)),
            _const_spec((1, S5_WIDTH)), _const_spec((S5_WIDTH, S5_WIDTH)), _const_spec((1, S5_WIDTH)),
        ],
        out_specs=[pl.BlockSpec((1, rt, S5_WIDTH), lambda b, t: (b, t, 0)), state_spec, state_spec],
        out_shape=[jax.ShapeDtypeStruct((bsz, seq, S5_WIDTH), BF16),
                   jax.ShapeDtypeStruct((bsz, 1, S5_HALF), F32),
                   jax.ShapeDtypeStruct((bsz, 1, S5_HALF), F32)],
        scratch_shapes=[pltpu.VMEM((rt, 2 * S5_HALF), F32), pltpu.VMEM((2, S5_HALF), F32)],
        compiler_params=_cparams(2, VMEM_LIMIT),
        name="s5",
    )(h, h0r, h0i, p["a_re"], p["a_im"], as_re, as_im, p["bbig"], p["cbig"], p["d"], p["w_glu"], p["b_glu"])


def _rope_lanes(x, c, s1, s2, half):
    n = x.shape[-1]
    return x * c + pltpu.roll(x, n - half, 1) * s1 + pltpu.roll(x, half, 1) * s2


def _mla_rows_body(h_ref, qn_ref, kn_ref, wq_ref, *refs, attn_layout, scale):
    q_tabs, (ck_ref, s1k_ref, s2k_ref, q_ref, ckv_ref, kpe_ref) = refs[:-6], refs[-6:]
    h = h_ref[0]
    q_lat = h[:, :MLA_Q_RANK]
    q_lat = (q_lat * lax.rsqrt(jnp.mean(q_lat * q_lat, axis=-1, keepdims=True) + EPS) * qn_ref[...]).astype(BF16)
    half = MLA_ROPE // 2
    if attn_layout:
        cos_t, sin_t = q_tabs[0][...], q_tabs[1][...]
        q_t = _dot_nt(wq_ref[...], q_lat)
        for hd in range(MLA_HEADS):
            r0 = hd * MLA_QK_PAD
            x1 = q_t[r0 + MLA_NOPE:r0 + MLA_NOPE + half, :]
            x2 = q_t[r0 + MLA_NOPE + half:r0 + MLA_NOPE + 2 * half, :]
            q_ref[0, r0:r0 + MLA_QK_PAD, :] = jnp.concatenate(
                [q_t[r0:r0 + MLA_NOPE, :] * scale, x1 * cos_t - x2 * sin_t, x1 * sin_t + x2 * cos_t,
                 q_t[r0 + MLA_NOPE + 2 * half:r0 + MLA_QK_PAD, :]], axis=0).astype(BF16)
    else:
        cq, s1q, s2q = (t[...] for t in q_tabs)
        q = jnp.dot(q_lat, wq_ref[...], preferred_element_type=F32)
        for hd in range(MLA_HEADS):
            sl = slice(hd * MLA_QK_PAD, (hd + 1) * MLA_QK_PAD)
            q_ref[0, :, sl] = _rope_lanes(q[:, sl], cq, s1q, s2q, half).astype(BF16)
    c_kv = h[:, MLA_Q_RANK:MLA_Q_RANK + MLA_KV_RANK]
    ckv_ref[0] = c_kv * lax.rsqrt(jnp.mean(c_kv * c_kv, axis=-1, keepdims=True) + EPS) * kn_ref[...]
    k_pe = h[:, MLA_Q_RANK + MLA_KV_RANK:]
    kpe_ref[0] = _rope_lanes(k_pe, ck_ref[...], s1k_ref[...], s2k_ref[...], MLA_ROPE // 2)[:, :MLA_ROPE]


def _mla_rows(h, p, pos, tm, attn_layout):
    bsz, seq, _ = h.shape
    tab_spec = pl.BlockSpec((tm, LANE), lambda b, t: (t, 0))
    n_q = MLA_HEADS * MLA_QK_PAD
    half = MLA_ROPE // 2
    scale = (MLA_NOPE + MLA_ROPE) ** -0.5 * LOG2E
    k_tabs = _rope_tables(pos, half, LANE, LANE, 0)
    if attn_layout:
        q_shape, q_spec = (bsz, n_q, seq), pl.BlockSpec((1, n_q, tm), lambda b, t: (b, 0, t))
        ang = (ROPE_BASE ** (-jnp.arange(half, dtype=F32) / half))[:, None] * pos.astype(F32)[None, :]
        q_tabs = (jnp.cos(ang) * scale, jnp.sin(ang) * scale)
        q_tab_specs = [pl.BlockSpec((half, tm), lambda b, t: (0, t))] * 2
        wq = p["wq"].T
    else:
        q_shape, q_spec = (bsz, seq, n_q), pl.BlockSpec((1, tm, n_q), lambda b, t: (b, t, 0))
        q_tabs = _rope_tables(pos, half, MLA_QK_PAD, MLA_QK_PAD, MLA_NOPE, scale=scale)
        q_tab_specs = [tab_spec] * 3
        wq = p["wq"]
    return pl.pallas_call(
        functools.partial(_mla_rows_body, attn_layout=attn_layout, scale=scale),
        grid=(bsz, seq // tm),
        in_specs=[
            pl.BlockSpec((1, tm, EVEN_IN_PAD - S5_WIDTH), lambda b, t: (b, t, 1)),
            _const_spec((1, MLA_Q_RANK)), _const_spec((1, MLA_KV_RANK)),
            _const_spec(wq.shape),
        ] + q_tab_specs + [tab_spec] * 3,
        out_specs=[
            q_spec,
            pl.BlockSpec((1, tm, MLA_KV_RANK), lambda b, t: (b, t, 0)),
            pl.BlockSpec((1, tm, MLA_ROPE), lambda b, t: (b, t, 0)),
        ],
        out_shape=[jax.ShapeDtypeStruct(q_shape, BF16),
                   jax.ShapeDtypeStruct((bsz, seq, MLA_KV_RANK), F32),
                   jax.ShapeDtypeStruct((bsz, seq, MLA_ROPE), F32)],
        compiler_params=_cparams(2, VMEM_LIMIT),
        name="mla_rows",
    )(h, p["q_norm"], p["kv_norm"], wq, *q_tabs, *k_tabs)


def _mla_kv_body(ckv_ref, kpe_ref, wkn_ref, wkp_ref, wv_ref, k_ref, v_ref):
    ckv = ckv_ref[0].astype(BF16)
    k = jnp.dot(ckv, wkn_ref[...], preferred_element_type=F32)
    k = k + jnp.dot(kpe_ref[0].astype(BF16), wkp_ref[...], preferred_element_type=F32)
    k_ref[0] = k.astype(BF16)
    v_t = jnp.dot(ckv, wv_ref[...], preferred_element_type=F32).T
    for hd in range(MLA_HEADS):
        v_ref[0, hd, 0] = v_t[hd * MLA_V:(hd + 1) * MLA_V, :].astype(BF16)


def _mla_kv(ckv, kpe, p):
    bsz, seq, _ = ckv.shape
    n_t = seq // TILE
    return pl.pallas_call(
        _mla_kv_body,
        grid=(bsz, n_t),
        in_specs=[
            pl.BlockSpec((1, TILE, MLA_KV_RANK), lambda b, t: (b, t, 0)),
            pl.BlockSpec((1, TILE, MLA_ROPE), lambda b, t: (b, t, 0)),
            _const_spec(p["wkn"].shape), _const_spec(p["wkp"].shape), _const_spec(p["wv"].shape),
        ],
        out_specs=[pl.BlockSpec((1, TILE, MLA_HEADS * MLA_QK_PAD), lambda b, t: (b, t, 0)),
                   pl.BlockSpec((1, MLA_HEADS, 1, MLA_V, TILE), lambda b, t: (b, 0, t, 0, 0))],
        out_shape=[jax.ShapeDtypeStruct((bsz, seq, MLA_HEADS * MLA_QK_PAD), BF16),
                   jax.ShapeDtypeStruct((bsz, MLA_HEADS, n_t, MLA_V, TILE), BF16)],
        compiler_params=_cparams(2, VMEM_LIMIT),
        name="mla_kv",
    )(ckv, kpe, p["wkn"], p["wkp"], p["wv"])


def _flash_body(*refs, heads, dq, dv, causal):
    if causal:
        qt_ref, k_ref, vt_ref, fq_ref, fk_ref, o_ref = refs
    else:
        qt_ref, k_ref, vt_ref, o_ref = refs
    tq = qt_ref.shape[2]
    n_tiles, tk = vt_ref.shape[2], vt_ref.shape[4]
    r0 = pl.program_id(1) * tq
    j_last = jnp.minimum((r0 + tq - 1) // tk, n_tiles - 1)
    q_row = r0 + lax.broadcasted_iota(jnp.int32, (1, tq), 1)
    last = q_row if causal else ((q_row >> CHUNK_SHIFT) << CHUNK_SHIFT) + (CHUNK - 1)
    first_last = r0 if causal else ((r0 >> CHUNK_SHIFT) << CHUNK_SHIFT) + (CHUNK - 1)
    n_open = jnp.clip((first_last + 1) // tk, 1, j_last + 1)
    q_t = [qt_ref[0, hd * dq:(hd + 1) * dq, :] for hd in range(heads)]

    def tile(j, carry, masked):
        start = pl.multiple_of(j * tk, tk)
        if masked:
            k_row = start + lax.broadcasted_iota(jnp.int32, (tk, 1), 0)
            ok = (k_row <= last) & (k_row >= PAD)
        scores = [jnp.dot(k_ref[0, pl.ds(start, tk), hd * dq:(hd + 1) * dq], q_t[hd],
                          preferred_element_type=F32) for hd in range(heads)]
        probs = []
        for hd in range(heads):
            m, l, _ = carry[hd]
            s = scores[hd]
            if causal:
                s = s + (fq_ref[0, hd:hd + 1, :] - fk_ref[0, pl.ds(start, tk), hd:hd + 1])
            if masked:
                s = jnp.where(ok, s, NEG)
            m_new = jnp.maximum(m, jnp.max(s, axis=0, keepdims=True))
            a = jnp.exp2(m - m_new)
            p = jnp.exp2(s - m_new)
            probs.append((m_new, a * l + jnp.sum(p, axis=0, keepdims=True), a, p.astype(BF16)))
        new = []
        for hd in range(heads):
            m_new, l, a, p = probs[hd]
            acc = a * carry[hd][2] + jnp.dot(vt_ref[0, hd, j], p, preferred_element_type=F32)
            new.append((m_new, l, acc))
        return tuple(new)

    init = tuple((jnp.full((1, tq), NEG, F32), jnp.zeros((1, tq), F32), jnp.zeros((dv, tq), F32))
                 for _ in range(heads))
    carry = tile(0, init, True)
    carry = lax.fori_loop(1, n_open, functools.partial(tile, masked=False), carry)
    carry = lax.fori_loop(n_open, j_last + 1, functools.partial(tile, masked=True), carry)
    outs = [carry[hd][2] / carry[hd][1] for hd in range(heads)]
    per = LANE // dv
    for g in range(heads // per):
        o_ref[0, :, g * LANE:(g + 1) * LANE] = jnp.concatenate(
            outs[g * per:(g + 1) * per], axis=0).T.astype(o_ref.dtype)


def _flash(q_t, k, v_t, fq_t, fk, *, heads, dq, dv, causal):
    bsz, _, n_q = q_t.shape
    n_keys = k.shape[1]
    in_specs = [
        pl.BlockSpec((1, heads * dq, TILE), lambda b, i: (b, 0, i)),
        pl.BlockSpec((1, n_keys, heads * dq), lambda b, i: (b, 0, 0)),
        pl.BlockSpec((1,) + v_t.shape[1:], lambda b, i: (b, 0, 0, 0, 0)),
    ]
    args = [q_t, k, v_t]
    if causal:
        in_specs += [pl.BlockSpec((1, heads, TILE), lambda b, i: (b, 0, i)),
                     pl.BlockSpec((1, n_keys, heads), lambda b, i: (b, 0, 0))]
        args += [fq_t, fk]
    return pl.pallas_call(
        functools.partial(_flash_body, heads=heads, dq=dq, dv=dv, causal=causal),
        grid=(bsz, n_q // TILE),
        in_specs=in_specs,
        out_specs=pl.BlockSpec((1, TILE, heads * dv), lambda b, i: (b, i, 0)),
        out_shape=jax.ShapeDtypeStruct((bsz, n_q, heads * dv), BF16),
        compiler_params=_cparams(2, VMEM_LIMIT),
        name="flash_causal" if causal else "flash_chunk",
    )(*args)


def _online_softmax(s, m_ref, l_ref, idx):
    m_old = m_ref[idx]
    m_new = jnp.maximum(m_old, jnp.max(s, axis=-1, keepdims=True))
    a = jnp.exp2(m_old - m_new)
    p = jnp.exp2(s - m_new)
    m_ref[idx] = m_new
    l_ref[idx] = a * l_ref[idx] + jnp.sum(p, axis=-1, keepdims=True)
    return a, p


def _dot_nt(a, b):
    return lax.dot_general(a, b, (((1,), (1,)), ((), ())), preferred_element_type=F32)


def _fox_decode_body(q_ref, fq_ref, kc_ref, vc_ref, fkc_ref, kn_ref, vn_ref, fkn_ref, o_ref,
                     qs_ref, m_ref, l_ref, acc_ref):
    t = pl.program_id(1)
    n_q = q_ref.shape[1]
    stacked = FOX_HEADS * n_q
    heads_rows = [slice(hd * n_q, (hd + 1) * n_q) for hd in range(FOX_HEADS)]
    heads_cols = [slice(hd * FOX_DIM, (hd + 1) * FOX_DIM) for hd in range(FOX_HEADS)]

    @pl.when(t == 0)
    def _():
        for rows, cols in zip(heads_rows, heads_cols):
            qs_ref[rows, :] = q_ref[0, :, cols]
        m_ref[...] = jnp.full(m_ref.shape, NEG, F32)
        l_ref[...] = jnp.zeros(l_ref.shape, F32)
        acc_ref[...] = jnp.zeros(acc_ref.shape, F32)

    kc = kc_ref[0, 0].astype(BF16)
    n_pairs = kc.shape[0]
    s = _dot_nt(qs_ref[...], kc) + (fq_ref[0] - fkc_ref[0])
    row_head = lax.broadcasted_iota(jnp.int32, (stacked, 1), 0) >> (n_q.bit_length() - 1)
    col_head = lax.broadcasted_iota(jnp.int32, (1, n_pairs), 1) & (FOX_HEADS - 1)
    a, p = _online_softmax(jnp.where(row_head == col_head, s, NEG), m_ref, l_ref, 0)
    acc_ref[0] = a * acc_ref[0] + jnp.dot(p.astype(BF16), vc_ref[0, 0].astype(BF16), preferred_element_type=F32)

    @pl.when(t == pl.num_programs(1) - 1)
    def _():
        causal = (lax.broadcasted_iota(jnp.int32, (n_q, n_q), 1) <= lax.broadcasted_iota(jnp.int32, (n_q, n_q), 0))
        for hd, (rows, cols) in enumerate(zip(heads_rows, heads_cols)):
            s = _dot_nt(q_ref[0, :, cols], kn_ref[0, :, cols]) + (fq_ref[0, rows, :] - fkn_ref[0, hd:hd + 1, :])
            s = jnp.where(causal, s, NEG)
            m_old = m_ref[0, rows, :]
            m_new = jnp.maximum(m_old, jnp.max(s, axis=-1, keepdims=True))
            a = jnp.exp2(m_old - m_new)
            p = jnp.exp2(s - m_new)
            l = a * l_ref[0, rows, :] + jnp.sum(p, axis=-1, keepdims=True)
            acc = a * acc_ref[0, rows, :] + jnp.dot(p.astype(BF16), vn_ref[0, :, cols], preferred_element_type=F32)
            o_ref[0, :, cols] = (acc / l).astype(o_ref.dtype)


def _fox_decode(q, k_new, v_new, cache_k, cache_v, layer, fcum, tk):
    bsz, n_q, width = q.shape
    n_past = cache_k.shape[2]
    assert n_q & (n_q - 1) == 0 and FOX_HEADS & (FOX_HEADS - 1) == 0
    stacked = FOX_HEADS * n_q
    pairs = lambda c: c.reshape(c.shape[:2] + (n_past * FOX_HEADS, FOX_DIM))
    fq = fcum[:, n_past:n_past + n_q].transpose(0, 2, 1)
    row_spec = pl.BlockSpec((1, n_q, width), lambda b, t: (b, 0, 0))
    cache_spec = pl.BlockSpec((1, 1, tk * FOX_HEADS, FOX_DIM), lambda b, t: (layer, b, t, 0))
    return pl.pallas_call(
        _fox_decode_body,
        grid=(bsz, n_past // tk),
        in_specs=[row_spec, pl.BlockSpec((1, stacked, 1), lambda b, t: (b, 0, 0)),
                  cache_spec, cache_spec, pl.BlockSpec((1, 1, tk * FOX_HEADS), lambda b, t: (b, 0, t)),
                  row_spec, row_spec, pl.BlockSpec((1, FOX_HEADS, n_q), lambda b, t: (b, 0, 0))],
        out_specs=row_spec,
        out_shape=jax.ShapeDtypeStruct((bsz, n_q, width), BF16),
        scratch_shapes=[pltpu.VMEM((stacked, FOX_DIM), BF16), pltpu.VMEM((1, stacked, 1), F32),
                        pltpu.VMEM((1, stacked, 1), F32), pltpu.VMEM((1, stacked, FOX_DIM), F32)],
        compiler_params=_cparams(2, VMEM_LIMIT),
        name="fox_decode",
    )(q, fq.reshape(bsz, stacked, 1), pairs(cache_k), pairs(cache_v),
      fcum[:, :n_past].reshape(bsz, 1, n_past * FOX_HEADS), k_new, v_new, fq)


def _mla_decode_body(q_ref, wkn_ref, wv_ref, cc_ref, pc_ref, cn_ref, pn_ref, o_ref, qa_ref, qr_ref,
                     m_ref, l_ref, acc_ref):
    t = pl.program_id(1)
    n_q = q_ref.shape[1]

    @pl.when(t == 0)
    def _():
        for hd in range(MLA_HEADS):
            c0 = hd * MLA_QK_PAD
            rows = slice(hd * n_q, (hd + 1) * n_q)
            qa_ref[rows, :] = _dot_nt(q_ref[0, :, c0:c0 + MLA_NOPE],
                                      wkn_ref[:, c0:c0 + MLA_NOPE]).astype(BF16)
            qr_ref[rows, :] = q_ref[0, :, c0 + MLA_NOPE:c0 + MLA_NOPE + MLA_ROPE]
        m_ref[...] = jnp.full(m_ref.shape, NEG, F32)
        l_ref[...] = jnp.zeros(l_ref.shape, F32)
        acc_ref[...] = jnp.zeros(acc_ref.shape, F32)

    def attend(ckv, kpe):
        s = _dot_nt(qa_ref[...], ckv) + _dot_nt(qr_ref[...], kpe)
        a, p = _online_softmax(s, m_ref, l_ref, 0)
        acc_ref[0] = a * acc_ref[0] + jnp.dot(p.astype(BF16), ckv, preferred_element_type=F32)

    attend(cc_ref[0, 0].astype(BF16), pc_ref[0, 0].astype(BF16))

    @pl.when(t == pl.num_programs(1) - 1)
    def _():
        attend(cn_ref[0].astype(BF16), pn_ref[0].astype(BF16))
        lat = (acc_ref[0] / l_ref[0]).astype(BF16)
        for hd in range(MLA_HEADS):
            cols = slice(hd * MLA_V, (hd + 1) * MLA_V)
            o_ref[0, :, cols] = jnp.dot(lat[hd * n_q:(hd + 1) * n_q, :], wv_ref[:, cols],
                                        preferred_element_type=F32).astype(o_ref.dtype)


def _mla_decode(q, ckv_new, kpe_new, cache_ckv, cache_kpe, layer, p, tk):
    bsz, n_q, _ = q.shape
    n_past = cache_ckv.shape[2]
    stacked = MLA_HEADS * n_q
    return pl.pallas_call(
        _mla_decode_body,
        grid=(bsz, n_past // tk),
        in_specs=[pl.BlockSpec((1, n_q, MLA_HEADS * MLA_QK_PAD), lambda b, t: (b, 0, 0)),
                  _const_spec(p["wkn"].shape), _const_spec(p["wv"].shape),
                  pl.BlockSpec((1, 1, tk, MLA_KV_RANK), lambda b, t: (layer, b, t, 0)),
                  pl.BlockSpec((1, 1, tk, MLA_ROPE), lambda b, t: (layer, b, t, 0)),
                  pl.BlockSpec((1, n_q, MLA_KV_RANK), lambda b, t: (b, 0, 0)),
                  pl.BlockSpec((1, n_q, MLA_ROPE), lambda b, t: (b, 0, 0))],
        out_specs=pl.BlockSpec((1, n_q, MLA_HEADS * MLA_V), lambda b, t: (b, 0, 0)),
        out_shape=jax.ShapeDtypeStruct((bsz, n_q, MLA_HEADS * MLA_V), BF16),
        scratch_shapes=[pltpu.VMEM((stacked, MLA_KV_RANK), BF16), pltpu.VMEM((stacked, MLA_ROPE), BF16),
                        pltpu.VMEM((1, stacked, 1), F32), pltpu.VMEM((1, stacked, 1), F32),
                        pltpu.VMEM((1, stacked, MLA_KV_RANK), F32)],
        compiler_params=_cparams(2, VMEM_LIMIT),
        name="mla_decode",
    )(q, p["wkn"], p["wv"], cache_ckv, cache_kpe, ckv_new, kpe_new)


def _gate_body(x_ref, b_ref, lf_ref, fc_ref, *, new_start, first_row):
    n_rows = x_ref.shape[1]
    tri = (lax.broadcasted_iota(jnp.int32, (TILE, TILE), 0)
           >= lax.broadcasted_iota(jnp.int32, (TILE, TILE), 1)).astype(F32)
    carry = jnp.zeros((1, x_ref.shape[2]), F32)
    for i in range(n_rows // TILE):
        sl = slice(i * TILE, (i + 1) * TILE)
        x = x_ref[0, sl, :]
        rows = i * TILE + lax.broadcasted_iota(jnp.int32, (TILE, 1), 0)
        z = x + b_ref[...]
        lf = jnp.where(rows >= new_start, jnp.minimum(z, 0.0) - jnp.log1p(jnp.exp(-jnp.abs(z))), x)
        if first_row:
            lf = jnp.where(rows >= first_row, lf, 0.0)
        lf_ref[0, sl, :] = lf
        cs = jnp.dot(tri, lf, preferred_element_type=F32, precision=lax.Precision.HIGHEST) + carry
        fc_ref[0, sl, :] = cs
        carry = cs[TILE - 1:TILE, :]


def _gate(x, b_f, new_start, first_row):
    bsz, n_rows, heads = x.shape
    spec = pl.BlockSpec((1, n_rows, heads), lambda b: (b, 0, 0))
    return pl.pallas_call(
        functools.partial(_gate_body, new_start=new_start, first_row=first_row),
        grid=(bsz,),
        in_specs=[spec, _const_spec((1, heads))],
        out_specs=[spec, spec],
        out_shape=[jax.ShapeDtypeStruct(x.shape, F32)] * 2,
        compiler_params=_cparams(1),
        name="fox_gate",
    )(x, b_f)


RET_LOG_GAMMA = tuple(math.log(1.0 - 2.0 ** (-5.0 - h)) for h in range(RET_HEADS))


def _ret_body(rq_ref, rk_ref, rv_ref, rg_ref, s0_ref, dec_ref, cq_ref, s1q_ref, s2q_ref, ck_ref, s1k_ref, s2k_ref,
              o_ref, sl_ref, st_ref, *, n_tail):
    c = pl.program_id(1)
    ct = rq_ref.shape[1]

    @pl.when(c == 0)
    def _():
        st_ref[...] = s0_ref[0]

    q = _rope_lanes(rq_ref[0], cq_ref[...], s1q_ref[...], s2q_ref[...], RET_DK // 2)
    k = _rope_lanes(rk_ref[0], ck_ref[...], s1k_ref[...], s2k_ref[...], RET_DK // 2)
    v = rv_ref[0].astype(BF16)
    g = rg_ref[0]
    j = lax.broadcasted_iota(jnp.int32, (ct, 1), 0).astype(F32)
    for hd in range(RET_HEADS):
        lg = RET_LOG_GAMMA[hd]
        q_h = q[:, hd * RET_DK:(hd + 1) * RET_DK]
        k_h = k[:, hd * RET_DK:(hd + 1) * RET_DK]
        v_h = v[:, hd * RET_DV:(hd + 1) * RET_DV]
        scores = lax.dot_general(q_h.astype(BF16), k_h.astype(BF16), (((1,), (1,)), ((), ())),
                                 preferred_element_type=F32) * dec_ref[hd]
        s_h = st_ref[hd]
        out = jnp.dot(scores.astype(BF16), v_h, preferred_element_type=F32)
        out = out + jnp.dot((q_h * jnp.exp(lg * (j + 1.0))).astype(BF16), s_h.astype(BF16),
                            preferred_element_type=F32)
        k_dec = (k_h * jnp.exp(lg * (ct - 1.0 - j))).astype(BF16)
        st_ref[hd] = math.exp(lg * ct) * s_h + lax.dot_general(
            k_dec, v_h, (((0,), (0,)), ((), ())), preferred_element_type=F32)
        mu = jnp.mean(out, axis=-1, keepdims=True)
        oc = out - mu
        var = jnp.mean(oc * oc, axis=-1, keepdims=True)
        g_h = g[:, hd * RET_DV:(hd + 1) * RET_DV]
        o_ref[0, :, hd * RET_DV:(hd + 1) * RET_DV] = (
            g_h * jax.nn.sigmoid(g_h) * (oc * lax.rsqrt(var + EPS))).astype(o_ref.dtype)

    @pl.when(c == pl.num_programs(1) - 1)
    def _():
        for hd in range(RET_HEADS):
            sl_ref[0, hd] = st_ref[hd] * math.exp(-RET_LOG_GAMMA[hd] * n_tail)


def _retention(h, s0, tabs, ct, n_tail):
    bsz, seq, _ = h.shape
    tab_spec = pl.BlockSpec((ct, RET_QK), lambda b, c: (c, 0))
    st_spec = pl.BlockSpec((1, RET_HEADS, RET_DK, RET_DV), lambda b, c: (b, 0, 0, 0))
    rq_blk = (2 * FOX_W) // RET_QK
    rv_blk = (2 * FOX_W + 2 * RET_QK) // RET_VW
    diff = jnp.arange(ct, dtype=F32)[:, None] - jnp.arange(ct, dtype=F32)[None, :]
    decay = jnp.where(diff >= 0.0, jnp.exp(jnp.asarray(RET_LOG_GAMMA, F32)[:, None, None] * jnp.maximum(diff, 0.0)),
                      0.0)
    return pl.pallas_call(
        functools.partial(_ret_body, n_tail=n_tail),
        grid=(bsz, seq // ct),
        in_specs=[
            pl.BlockSpec((1, ct, RET_QK), lambda b, c: (b, c, rq_blk)),
            pl.BlockSpec((1, ct, RET_QK), lambda b, c: (b, c, rq_blk + 1)),
            pl.BlockSpec((1, ct, RET_VW), lambda b, c: (b, c, rv_blk)),
            pl.BlockSpec((1, ct, RET_VW), lambda b, c: (b, c, rv_blk + 1)),
            st_spec, _const_spec((RET_HEADS, ct, ct)),
        ] + [tab_spec] * 6,
        out_specs=[pl.BlockSpec((1, ct, RET_VW), lambda b, c: (b, c, 0)), st_spec],
        out_shape=[jax.ShapeDtypeStruct((bsz, seq, RET_VW), BF16),
                   jax.ShapeDtypeStruct((bsz, RET_HEADS, RET_DK, RET_DV), F32)],
        scratch_shapes=[pltpu.VMEM((RET_HEADS, RET_DK, RET_DV), F32)],
        compiler_params=_cparams(2, VMEM_LIMIT),
        name="retention",
    )(h, h, h, h, s0, decay, *tabs)


def _rope_tables(pos, half, width, group, offset, scale=1.0, valid=None):
    inv = ROPE_BASE ** (-jnp.arange(half, dtype=F32) / half)
    ang = pos.astype(F32)[:, None] * inv[None, :]
    cos, sin = jnp.cos(ang), jnp.sin(ang)
    n = pos.shape[0]
    one, zero = jnp.ones((n, 1), F32), jnp.zeros((n, 1), F32)

    def lanes(first, second, other):
        grp = jnp.concatenate([jnp.broadcast_to(other, (n, offset)), first, second,
                               jnp.broadcast_to(other, (n, group - offset - 2 * half))], axis=1)
        return jnp.tile(grp, (1, width // group))

    tabs = (lanes(cos, cos, one), lanes(-sin, 0.0 * sin, zero), lanes(0.0 * sin, sin, zero))
    if valid is not None:
        tabs = tuple(jnp.where(valid[:, None], t, 0.0) for t in tabs)
    return tuple(t * scale for t in tabs)


def _even_params(e, w):
    p = {}
    p["w_in"] = jnp.pad(w["even_w_in"][e], ((0, 0), (0, EVEN_IN_PAD - EVEN_IN))).astype(BF16)
    p["w_out"] = w["even_w_out"][e].astype(BF16)
    lam_re, lam_im = w["s5_a_re"][e].astype(F32), w["s5_a_im"][e].astype(F32)
    dt = jnp.exp(w["s5_log_dt"][e].astype(F32))[:, None]
    p["lam_dt_re"], p["lam_dt_im"] = lam_re * dt, lam_im * dt
    mag = jnp.exp(lam_re * dt)
    abar_re, abar_im = mag * jnp.cos(lam_im * dt), mag * jnp.sin(lam_im * dt)
    den = lam_re * lam_re + lam_im * lam_im
    f_re = ((abar_re - 1.0) * lam_re + abar_im * lam_im) / den
    f_im = (abar_im * lam_re - (abar_re - 1.0) * lam_im) / den
    b_re, b_im = w["s5_b_re"][e].astype(F32), w["s5_b_im"][e].astype(F32)
    bb_re = f_re[..., None] * b_re - f_im[..., None] * b_im
    bb_im = f_re[..., None] * b_im + f_im[..., None] * b_re
    eye = jnp.eye(S5_GROUPS, dtype=F32)

    def in_blocks(x):
        return jnp.einsum("gnc,gh->gchn", x, eye).reshape(S5_WIDTH, S5_HALF)

    def out_blocks(x):
        return jnp.einsum("gcn,gh->gnhc", x, eye).reshape(S5_HALF, S5_WIDTH)

    p["bbig"] = jnp.concatenate([in_blocks(bb_re), in_blocks(bb_im)], axis=1).astype(BF16)
    p["cbig"] = jnp.concatenate([out_blocks(w["s5_c_re"][e].astype(F32)),
                                 out_blocks(-w["s5_c_im"][e].astype(F32))], axis=0).astype(BF16)
    p["a_re"] = abar_re.reshape(1, S5_HALF)
    p["a_im"] = abar_im.reshape(1, S5_HALF)
    p["d"] = w["s5_d"][e].astype(F32).reshape(1, S5_WIDTH)
    p["w_glu"] = w["s5_w_glu"][e].astype(BF16)
    p["b_glu"] = w["s5_b_glu"][e].astype(F32).reshape(1, S5_WIDTH)
    p["q_norm"] = w["mla_q_norm"][e].astype(F32).reshape(1, MLA_Q_RANK)
    p["kv_norm"] = w["mla_kv_norm"][e].astype(F32).reshape(1, MLA_KV_RANK)
    wq = w["mla_w_uq"][e].reshape(MLA_Q_RANK, MLA_HEADS, MLA_NOPE + MLA_ROPE)
    wq = jnp.pad(wq, ((0, 0), (0, 0), (0, MLA_QK_PAD - MLA_NOPE - MLA_ROPE)))
    p["wq"] = wq.reshape(MLA_Q_RANK, MLA_HEADS * MLA_QK_PAD).astype(BF16)
    wkv = w["mla_w_ukv"][e].reshape(MLA_KV_RANK, MLA_HEADS, MLA_NOPE + MLA_V)
    wkn = jnp.pad(wkv[:, :, :MLA_NOPE], ((0, 0), (0, 0), (0, MLA_QK_PAD - MLA_NOPE)))
    p["wkn"] = wkn.reshape(MLA_KV_RANK, MLA_HEADS * MLA_QK_PAD).astype(BF16)
    place = jnp.pad(jnp.eye(MLA_ROPE, dtype=F32), ((0, 0), (MLA_NOPE, MLA_QK_PAD - MLA_NOPE - MLA_ROPE)))
    p["wkp"] = jnp.tile(place, (1, MLA_HEADS)).astype(BF16)
    p["wv"] = wkv[:, :, MLA_NOPE:].reshape(MLA_KV_RANK, MLA_HEADS * MLA_V).astype(BF16)
    return p


def _odd_params(o, w):
    w_in = w["odd_w_in"][o]
    c_logit = 3 * FOX_W
    cols = jnp.concatenate([
        w_in[:, :FOX_W] * (FOX_DIM ** -0.5),
        w_in[:, FOX_W:c_logit],
        w_in[:, c_logit + FOX_HEADS:],
        w_in[:, c_logit:c_logit + FOX_HEADS],
    ], axis=1)
    p = {"w_in": jnp.pad(cols, ((0, 0), (0, ODD_IN_PAD - cols.shape[1]))).astype(BF16)}
    p["w_out"] = w["odd_w_out"][o].astype(BF16)
    p["b_f"] = w["fox_b_f"][o].astype(F32).reshape(1, FOX_HEADS)
    return p


def _cache_tile(n_past):
    for tk in (512, 256, 128):
        if n_past % tk == 0:
            return tk
    raise ValueError("cache length must be a multiple of 128")


def _trunk(x, pos, n_real, out_rows, past, w, ffn, evens, odds):
    bsz, seq, _ = x.shape
    rows = bsz * seq
    prompt = past is None
    first_row = PAD if prompt else 0
    if prompt:
        tm = 2 * TILE if rows % (2 * TILE) == 0 else TILE
        rt = TILE
    else:
        n_past = past["cache_fox_k"].shape[2]
        tk = _cache_tile(n_past)
        tm, rt = rows, seq
    idx = jnp.arange(seq, dtype=jnp.int32)
    valid = (idx >= first_row) & (idx < n_real)
    st = {n: [] for n in ("mla_ckv", "mla_kpe", "s5_re", "s5_im", "fox_k", "fox_v", "fox_logf", "ret")}
    rq_tabs = _rope_tables(pos, RET_DK // 2, RET_QK, RET_DK, 0)
    rk_tabs = _rope_tables(pos, RET_DK // 2, RET_QK, RET_DK, 0, scale=RET_DK ** -0.5, valid=valid)
    x2 = x.reshape(rows, D_MODEL)

    def ln(l, i):
        return w["ln_g"][l, i].reshape(1, D_MODEL), w["ln_b"][l, i].reshape(1, D_MODEL)

    for l in range(DEPTH):
        x2 = _ffn_ln(x2, *ffn[l][0], *ln(l, 0), tm)
        if l % 2 == 0:
            e = l // 2
            p = evens[e]
            h = _proj(x2, p["w_in"], tm).reshape(bsz, seq, EVEN_IN_PAD)
            if prompt:
                h0r = h0i = jnp.zeros((bsz, 1, S5_HALF), F32)
            else:
                h0r = past["state_s5_re"][e].astype(F32).reshape(bsz, 1, S5_HALF)
                h0i = past["state_s5_im"][e].astype(F32).reshape(bsz, 1, S5_HALF)
            s5_out, hlr, hli = _s5(h, h0r, h0i, p, rt, first_row, n_real)
            q, ckv, kpe = _mla_rows(h, p, pos, rt, attn_layout=prompt)
            if prompt:
                k_att, v_att = _mla_kv(ckv, kpe, p)
                mla_out = _flash(q, k_att, v_att, None, None, heads=MLA_HEADS, dq=MLA_QK_PAD, dv=MLA_V,
                                 causal=False)
            else:
                mla_out = _mla_decode(q, ckv, kpe, past["cache_mla_ckv"], past["cache_mla_kpe"], e, p, tk)
            mix = (s5_out.reshape(rows, S5_WIDTH), mla_out.reshape(rows, MLA_HEADS * MLA_V), p["w_out"])
            st["mla_ckv"].append(ckv)
            st["mla_kpe"].append(kpe)
            st["s5_re"].append(hlr.reshape(bsz, S5_GROUPS, S5_STATE))
            st["s5_im"].append(hli.reshape(bsz, S5_GROUPS, S5_STATE))
        else:
            o = l // 2
            p = odds[o]
            h, fq, fk16, fv = _proj_odd(x2, p["w_in"], bsz, TILE if prompt else tm, attn_layout=prompt)
            h = h.reshape(bsz, seq, ODD_H)
            fk16 = fk16.reshape(bsz, seq, FOX_W)
            f_logit = h[:, :, ODD_LOGIT_COL:ODD_LOGIT_COL + FOX_HEADS]
            if prompt:
                q_off = 0
                logf, fcum = _gate(f_logit, p["b_f"], 0, PAD)
                fcum = fcum * LOG2E
                fox_out = _flash(fq, fk16, fv, fcum.transpose(0, 2, 1), fcum, heads=FOX_HEADS, dq=FOX_DIM,
                                 dv=FOX_DIM, causal=True)
                s0 = jnp.zeros((bsz, RET_HEADS, RET_DK, RET_DV), F32)
            else:
                q_off = n_past
                gates = jnp.concatenate([past["cache_fox_logf"][o].astype(F32), f_logit], axis=1)
                gates = jnp.pad(gates, ((0, 0), (0, _round_up(n_past + seq, TILE) - n_past - seq), (0, 0)))
                logf, fcum = _gate(gates, p["b_f"], n_past, 0)
                fox_out = _fox_decode(fq.reshape(bsz, seq, FOX_W), fk16, fv.reshape(bsz, seq, FOX_W),
                                      past["cache_fox_k"], past["cache_fox_v"], o, fcum * LOG2E, tk)
                s0 = past["state_ret"][o].astype(F32)
            ret_out, s_last = _retention(h, s0, rq_tabs + rk_tabs, rt, seq - n_real)
            mix = (fox_out.reshape(rows, FOX_W), ret_out.reshape(rows, RET_VW), p["w_out"])
            st["fox_k"].append(h[:, :, :FOX_W].reshape(bsz, seq, FOX_HEADS, FOX_DIM))
            st["fox_v"].append(h[:, :, FOX_W:2 * FOX_W].reshape(bsz, seq, FOX_HEADS, FOX_DIM))
            st["fox_logf"].append(logf[:, q_off:q_off + seq])
            st["ret"].append(s_last)
        keep = (seq,) + out_rows if l == DEPTH - 1 and out_rows != (0, seq) else None
        x2 = _mix_ffn_ln(x2, *mix, ln(l, 1), ffn[l][1], ln(l, 2), tm, keep)
    return x2.reshape(bsz, out_rows[1], D_MODEL), {n: jnp.stack(a) for n, a in st.items()}


def kernel(x_prompt, x_sample, cache_mla_ckv, cache_mla_kpe, cache_fox_k, cache_fox_v, cache_fox_logf,
           state_s5_re, state_s5_im, state_ret, meta_tokens, ln_g, ln_b, ffn_w_gate, ffn_w_up, ffn_w_down,
           even_w_in, even_w_out, s5_a_re, s5_a_im, s5_b_re, s5_b_im, s5_c_re, s5_c_im, s5_d, s5_log_dt,
           s5_w_glu, s5_b_glu, mla_q_norm, mla_kv_norm, mla_w_uq, mla_w_ukv, odd_w_in, odd_w_out, fox_b_f):
    w = dict(ln_g=ln_g.astype(F32), ln_b=ln_b.astype(F32), even_w_in=even_w_in, even_w_out=even_w_out,
             s5_a_re=s5_a_re, s5_a_im=s5_a_im, s5_b_re=s5_b_re, s5_b_im=s5_b_im, s5_c_re=s5_c_re,
             s5_c_im=s5_c_im, s5_d=s5_d, s5_log_dt=s5_log_dt, s5_w_glu=s5_w_glu, s5_b_glu=s5_b_glu,
             mla_q_norm=mla_q_norm, mla_kv_norm=mla_kv_norm, mla_w_uq=mla_w_uq, mla_w_ukv=mla_w_ukv,
             odd_w_in=odd_w_in, odd_w_out=odd_w_out, fox_b_f=fox_b_f)
    past = dict(cache_mla_ckv=cache_mla_ckv, cache_mla_kpe=cache_mla_kpe, cache_fox_k=cache_fox_k,
                cache_fox_v=cache_fox_v, cache_fox_logf=cache_fox_logf, state_s5_re=state_s5_re,
                state_s5_im=state_s5_im, state_ret=state_ret)
    ffn = [[(ffn_w_gate[l, i].astype(BF16), ffn_w_up[l, i].astype(BF16), ffn_w_down[l, i].astype(BF16))
            for i in range(2)] for l in range(DEPTH)]
    evens = [_even_params(e, w) for e in range((DEPTH + 1) // 2)]
    odds = [_odd_params(o, w) for o in range(DEPTH // 2)]

    bsz, seq, _ = x_prompt.shape
    n_real = PAD + N_META + seq
    n_rows = _round_up(n_real, TILE)
    meta = jnp.broadcast_to(meta_tokens[None].astype(x_prompt.dtype), (bsz, N_META, D_MODEL))
    xp = jnp.concatenate([jnp.zeros((bsz, PAD, D_MODEL), x_prompt.dtype), meta, x_prompt,
                          jnp.zeros((bsz, n_rows - n_real, D_MODEL), x_prompt.dtype)], axis=1)
    pos_p = jnp.maximum(jnp.arange(n_rows, dtype=jnp.int32) - PAD, 0)
    y_p, st_p = _trunk(xp, pos_p, n_real, (PAD + N_META, seq), None, w, ffn, evens, odds)
    d_seq = x_sample.shape[1]
    pos_s = N_META + cache_fox_k.shape[2] + jnp.arange(d_seq, dtype=jnp.int32)
    y_s, st_s = _trunk(x_sample, pos_s, d_seq, (0, d_seq), past, w, ffn, evens, odds)

    def real(a):
        return a[:, :, PAD:n_real]

    return (y_p, y_s,
            real(st_p["mla_ckv"]), real(st_p["mla_kpe"]), real(st_p["fox_k"]), real(st_p["fox_v"]),
            real(st_p["fox_logf"]), st_p["s5_re"], st_p["s5_im"], st_p["ret"],
            st_s["mla_ckv"], st_s["mla_kpe"], st_s["fox_k"], st_s["fox_v"], st_s["fox_logf"],
            st_s["s5_re"], st_s["s5_im"], st_s["ret"])
```

```python
import functools
import math

import jax
import jax.numpy as jnp
from jax import lax
from jax.experimental import pallas as pl
from jax.experimental.pallas import tpu as pltpu

F32 = jnp.float32
BF16 = jnp.bfloat16

D_MODEL = 1024
DEPTH = 4
CHUNK = 64
CHUNK_SHIFT = 6
N_META = 16
S5_WIDTH = 512
S5_CH = 16
S5_GROUPS = S5_WIDTH // S5_CH
S5_STATE = 64
S5_HALF = S5_GROUPS * S5_STATE
MLA_HEADS = 8
MLA_Q_RANK = 256
MLA_KV_RANK = 128
MLA_NOPE = 64
MLA_ROPE = 32
MLA_V = 64
MLA_QK_PAD = 128
FOX_HEADS = 8
FOX_DIM = 64
FOX_W = FOX_HEADS * FOX_DIM
RET_HEADS = 4
RET_DK = 64
RET_DV = 128
RET_QK = RET_HEADS * RET_DK
RET_VW = RET_HEADS * RET_DV
D_FF = 2816
ROPE_BASE = 10000.0
ALPHA = (2.0 * DEPTH) ** 0.25
EPS = 1e-5
NEG = -1e30
LOG2E = math.log2(math.e)
EVEN_IN = S5_WIDTH + MLA_Q_RANK + MLA_KV_RANK + MLA_ROPE
EVEN_IN_PAD = 1024
ODD_IN_PAD = 3200
ODD_H = ODD_IN_PAD - FOX_W
ODD_LOGIT_COL = 2 * FOX_W + 2 * RET_QK + 2 * RET_VW

PAD = CHUNK - N_META
LANE = 128
SUBLANE = 8
TILE = 256
VMEM_LIMIT = 56 * 1024 * 1024


def _cparams(n_grid, vmem=None):
    return pltpu.CompilerParams(dimension_semantics=("arbitrary",) * n_grid, vmem_limit_bytes=vmem)


def _const_spec(shape):
    nd = len(shape)
    return pl.BlockSpec(shape, lambda *_: (0,) * nd)


def _round_up(n, m):
    return -(-n // m) * m


def _layer_norm_rows(z, g, b):
    mu = jnp.mean(z, axis=-1, keepdims=True)
    zc = z - mu
    var = jnp.mean(zc * zc, axis=-1, keepdims=True)
    return zc * lax.rsqrt(var + EPS) * g + b


def _ffn_rows(x, wg_ref, wu_ref, wd_ref, g_ref, b_ref, hid_ref):
    xb = x.astype(BF16)
    for c in range(D_FF // TILE):
        sl = slice(c * TILE, (c + 1) * TILE)
        hg = jnp.dot(xb, wg_ref[:, sl], preferred_element_type=F32)
        hu = jnp.dot(xb, wu_ref[:, sl], preferred_element_type=F32)
        hid_ref[:, sl] = (hg * jax.nn.sigmoid(hg) * hu).astype(BF16)
    y = jnp.dot(hid_ref[...], wd_ref[...], preferred_element_type=F32)
    return _layer_norm_rows(ALPHA * x + 0.5 * y, g_ref[...], b_ref[...])


def _ffn_body(x_ref, wg_ref, wu_ref, wd_ref, g_ref, b_ref, o_ref, hid_ref):
    o_ref[...] = _ffn_rows(x_ref[...], wg_ref, wu_ref, wd_ref, g_ref, b_ref, hid_ref)


def _ffn_weight_specs():
    once = dict(pipeline_mode=pl.Buffered(1))
    return [pl.BlockSpec((D_MODEL, D_FF), lambda i: (0, 0), **once),
            pl.BlockSpec((D_MODEL, D_FF), lambda i: (0, 0), **once),
            pl.BlockSpec((D_FF, D_MODEL), lambda i: (0, 0), **once),
            _const_spec((1, D_MODEL)), _const_spec((1, D_MODEL))]


def _ffn_ln(x, wg, wu, wd, g, b, tm):
    rows = x.shape[0]
    return pl.pallas_call(
        _ffn_body,
        grid=(rows // tm,),
        in_specs=[pl.BlockSpec((tm, D_MODEL), lambda i: (i, 0))] + _ffn_weight_specs(),
        out_specs=pl.BlockSpec((tm, D_MODEL), lambda i: (i, 0)),
        out_shape=jax.ShapeDtypeStruct((rows, D_MODEL), F32),
        scratch_shapes=[pltpu.VMEM((tm, D_FF), BF16)],
        compiler_params=_cparams(1, VMEM_LIMIT),
        name="ffn_ln",
    )(x, wg, wu, wd, g, b)


def _mix_ffn_body(x_ref, a1_ref, a2_ref, wo_ref, g1_ref, b1_ref, wg_ref, wu_ref, wd_ref, g2_ref, b2_ref,
                  o_ref, hid_ref):
    k1 = a1_ref.shape[1]
    y = jnp.dot(a1_ref[...], wo_ref[:k1, :], preferred_element_type=F32)
    y = y + jnp.dot(a2_ref[...], wo_ref[k1:, :], preferred_element_type=F32)
    x1 = _layer_norm_rows(ALPHA * x_ref[...] + y, g1_ref[...], b1_ref[...])
    o_ref[...] = _ffn_rows(x1, wg_ref, wu_ref, wd_ref, g2_ref, b2_ref, hid_ref)


def _mix_ffn_ln(x, a1, a2, w_out, ln1, ffn_w, ln2, tm, keep=None):
    rows = x.shape[0]
    if keep is None:
        n_steps = rows // tm
        row_spec = lambda n: pl.BlockSpec((tm, n), lambda i: (i, 0))
    else:
        seq, start, count = keep
        per = count // tm
        n_steps = (rows // seq) * per
        align = math.gcd(seq, start, tm)
        row_spec = lambda n: pl.BlockSpec(
            (pl.Element(tm), pl.Element(n)),
            lambda i: (pl.multiple_of((i // per) * seq + start + (i % per) * tm, align), 0))
    return pl.pallas_call(
        _mix_ffn_body,
        grid=(n_steps,),
        in_specs=[row_spec(D_MODEL), row_spec(a1.shape[1]), row_spec(a2.shape[1]),
                  pl.BlockSpec(w_out.shape, lambda i: (0, 0), pipeline_mode=pl.Buffered(1)),
                  _const_spec((1, D_MODEL)), _const_spec((1, D_MODEL))] + _ffn_weight_specs(),
        out_specs=pl.BlockSpec((tm, D_MODEL), lambda i: (i, 0)),
        out_shape=jax.ShapeDtypeStruct((n_steps * tm, D_MODEL), F32),
        scratch_shapes=[pltpu.VMEM((tm, D_FF), BF16)],
        compiler_params=_cparams(1, VMEM_LIMIT),
        name="mix_ffn_ln",
    )(x, a1, a2, w_out, *ln1, *ffn_w, *ln2)


def _proj_body(x_ref, w_ref, o_ref):
    o_ref[...] = jnp.dot(x_ref[...].astype(BF16), w_ref[...], preferred_element_type=F32)


def _proj(x, w, tm):
    rows, n = x.shape[0], w.shape[1]
    return pl.pallas_call(
        _proj_body,
        grid=(rows // tm,),
        in_specs=[pl.BlockSpec((tm, D_MODEL), lambda i: (i, 0)), _const_spec(w.shape)],
        out_specs=pl.BlockSpec((tm, n), lambda i: (i, 0)),
        out_shape=jax.ShapeDtypeStruct((rows, n), F32),
        compiler_params=_cparams(1, VMEM_LIMIT),
        name="in_proj_even",
    )(x, w)


def _proj_odd_body(x_ref, w_ref, h_ref, q_ref, k_ref, v_ref, *, attn_layout):
    y = jnp.dot(x_ref[...].astype(BF16), w_ref[...], preferred_element_type=F32)
    h_ref[...] = y[:, FOX_W:]
    k_ref[...] = y[:, FOX_W:2 * FOX_W].astype(BF16)
    q = y[:, :FOX_W] * LOG2E
    if attn_layout:
        q_ref[0] = q.T.astype(BF16)
        v_t = y[:, 2 * FOX_W:3 * FOX_W].T
        for hd in range(FOX_HEADS):
            v_ref[0, hd, 0] = v_t[hd * FOX_DIM:(hd + 1) * FOX_DIM, :].astype(BF16)
    else:
        q_ref[...] = q.astype(BF16)
        v_ref[...] = y[:, 2 * FOX_W:3 * FOX_W].astype(BF16)


def _proj_odd(x, w, bsz, tm, attn_layout):
    rows = x.shape[0]
    seq = rows // bsz
    row_spec = lambda n: pl.BlockSpec((tm, n), lambda i: (i, 0))
    if attn_layout:
        assert tm == TILE and seq % TILE == 0
        n_t = seq // TILE
        q_shape = jax.ShapeDtypeStruct((bsz, FOX_W, seq), BF16)
        q_spec = pl.BlockSpec((1, FOX_W, TILE), lambda i: (i // n_t, 0, i % n_t))
        v_shape = jax.ShapeDtypeStruct((bsz, FOX_HEADS, n_t, FOX_DIM, TILE), BF16)
        v_spec = pl.BlockSpec((1, FOX_HEADS, 1, FOX_DIM, TILE), lambda i: (i // n_t, 0, i % n_t, 0, 0))
    else:
        q_shape = v_shape = jax.ShapeDtypeStruct((rows, FOX_W), BF16)
        q_spec = v_spec = row_spec(FOX_W)
    return pl.pallas_call(
        functools.partial(_proj_odd_body, attn_layout=attn_layout),
        grid=(rows // tm,),
        in_specs=[row_spec(D_MODEL), _const_spec(w.shape)],
        out_specs=[row_spec(ODD_H), q_spec, row_spec(FOX_W), v_spec],
        out_shape=[jax.ShapeDtypeStruct((rows, ODD_H), F32), q_shape,
                   jax.ShapeDtypeStruct((rows, FOX_W), BF16), v_shape],
        compiler_params=_cparams(1, VMEM_LIMIT),
        name="in_proj_odd",
    )(x, w)


def _s5_body(u_ref, h0r_ref, h0i_ref, ar_ref, ai_ref, asr_ref, asi_ref, bbig_ref, cbig_ref, d_ref, wglu_ref,
             bglu_ref, o_ref, hlr_ref, hli_ref, hs_ref, st_ref, *, first_row, last_seg):
    t = pl.program_id(1)
    rt = u_ref.shape[1]
    seg = rt // SUBLANE

    @pl.when(t == 0)
    def _():
        st_ref[0:1, :] = h0r_ref[0]
        st_ref[1:2, :] = h0i_ref[0]

    u = u_ref[0]
    if first_row:
        rows = t * rt + lax.broadcasted_iota(jnp.int32, (rt, 1), 0)
        u = jnp.where(rows >= first_row, u, 0.0)
    i0 = lax.broadcasted_iota(jnp.int32, (rt, rt), 0)
    i1 = lax.broadcasted_iota(jnp.int32, (rt, rt), 1)
    regroup = (i1 == (i0 & (SUBLANE - 1)) * seg + (i0 >> 3)).astype(BF16)
    restore = (i0 == (i1 & (SUBLANE - 1)) * seg + (i1 >> 3)).astype(BF16)
    ub = jnp.dot(regroup, u.astype(BF16), preferred_element_type=F32).astype(BF16)

    half_w, half_s = S5_WIDTH // 2, S5_HALF // 2
    for kb in range(2):
        for part in range(2):
            c0 = part * S5_HALF + kb * half_s
            hs_ref[:, c0:c0 + half_s] = jnp.dot(
                ub[:, kb * half_w:(kb + 1) * half_w], bbig_ref[kb * half_w:(kb + 1) * half_w, c0:c0 + half_s],
                preferred_element_type=F32)

    ar = jnp.broadcast_to(ar_ref[...], (SUBLANE, S5_HALF))
    ai = jnp.broadcast_to(ai_ref[...], (SUBLANE, S5_HALF))

    def step(k, carry, store):
        hr, hi = carry
        base = pl.multiple_of(k * SUBLANE, SUBLANE)
        blk = hs_ref[pl.ds(base, SUBLANE), :]
        nr = ar * hr - ai * hi + blk[:, :S5_HALF]
        ni = ar * hi + ai * hr + blk[:, S5_HALF:]
        if store:
            hs_ref[pl.ds(base, SUBLANE), :] = jnp.concatenate([nr, ni], axis=1)
        return nr, ni

    zero = jnp.zeros((SUBLANE, S5_HALF), F32)
    er, ei = lax.fori_loop(0, seg, functools.partial(step, store=False), (zero, zero))
    asr, asi = asr_ref[...], asi_ref[...]
    sr, si = st_ref[0:1, :], st_ref[1:2, :]
    start_r, start_i = [], []
    for j in range(SUBLANE):
        start_r.append(sr)
        start_i.append(si)
        sr, si = asr * sr - asi * si + er[j:j + 1], asr * si + asi * sr + ei[j:j + 1]
    st_ref[0:1, :] = sr
    st_ref[1:2, :] = si
    lax.fori_loop(0, seg, functools.partial(step, store=True),
                  (jnp.concatenate(start_r, axis=0), jnp.concatenate(start_i, axis=0)))

    hb = hs_ref[...].astype(BF16)
    ys = []
    for kb in range(2):
        re_rows = slice(kb * half_s, (kb + 1) * half_s)
        im_rows = slice(S5_HALF + kb * half_s, S5_HALF + (kb + 1) * half_s)
        cols = slice(kb * half_w, (kb + 1) * half_w)
        ys.append(jnp.dot(hb[:, re_rows], cbig_ref[re_rows, cols], preferred_element_type=F32)
                  + jnp.dot(hb[:, im_rows], cbig_ref[im_rows, cols], preferred_element_type=F32))
    y = jnp.concatenate(ys, axis=1)
    y_hi = y.astype(BF16)
    y_lo = (y - y_hi.astype(F32)).astype(BF16)
    y = (jnp.dot(restore, y_hi, preferred_element_type=F32) + jnp.dot(restore, y_lo, preferred_element_type=F32)
         + d_ref[...] * u)
    g = jax.nn.gelu(y)
    gate = jnp.dot(g.astype(BF16), wglu_ref[...], preferred_element_type=F32) + bglu_ref[...]
    o_ref[0] = (g * jax.nn.sigmoid(gate)).astype(o_ref.dtype)

    @pl.when(t == pl.num_programs(1) - 1)
    def _():
        hlr_ref[0] = sr if last_seg == SUBLANE else start_r[last_seg]
        hli_ref[0] = si if last_seg == SUBLANE else start_i[last_seg]


def _s5(h, h0r, h0i, p, rt, first_row, n_real):
    bsz, seq, _ = h.shape
    seg = rt // SUBLANE
    real_in_last = n_real - (seq - rt)
    assert 0 < real_in_last <= rt and real_in_last % seg == 0
    as_re = (jnp.exp(p["lam_dt_re"] * seg) * jnp.cos(p["lam_dt_im"] * seg)).reshape(1, S5_HALF)
    as_im = (jnp.exp(p["lam_dt_re"] * seg) * jnp.sin(p["lam_dt_im"] * seg)).reshape(1, S5_HALF)
    state_spec = pl.BlockSpec((1, 1, S5_HALF), lambda b, t: (b, 0, 0))
    return pl.pallas_call(
        functools.partial(_s5_body, first_row=first_row, last_seg=real_in_last // seg),
        grid=(bsz, seq // rt),
        in_specs=[
            pl.BlockSpec((1, rt, S5_WIDTH), lambda b, t: (b, t, 0)),
            state_spec, state_spec,
            _const_spec((1, S5_HALF)), _const_spec((1, S5_HALF)), _const_spec((1, S5_HALF)), _const_spec((1, S5_HALF)),
            _const_spec((S5_WIDTH, 2 * S5_HALF)), _const_spec((2 * S5_HALF, S5_WIDTH)),
            _const_spec((1, S5_WIDTH)), _const_spec((S5_WIDTH, S5_WIDTH)), _const_spec((1, S5_WIDTH)),
        ],
        out_specs=[pl.BlockSpec((1, rt, S5_WIDTH), lambda b, t: (b, t, 0)), state_spec, state_spec],
        out_shape=[jax.ShapeDtypeStruct((bsz, seq, S5_WIDTH), BF16),
                   jax.ShapeDtypeStruct((bsz, 1, S5_HALF), F32),
                   jax.ShapeDtypeStruct((bsz, 1, S5_HALF), F32)],
        scratch_shapes=[pltpu.VMEM((rt, 2 * S5_HALF), F32), pltpu.VMEM((2, S5_HALF), F32)],
        compiler_params=_cparams(2, VMEM_LIMIT),
        name="s5",
    )(h, h0r, h0i, p["a_re"], p["a_im"], as_re, as_im, p["bbig"], p["cbig"], p["d"], p["w_glu"], p["b_glu"])


def _rope_lanes(x, c, s1, s2, half):
    n = x.shape[-1]
    return x * c + pltpu.roll(x, n - half, 1) * s1 + pltpu.roll(x, half, 1) * s2


def _mla_rows_body(h_ref, qn_ref, kn_ref, wq_ref, *refs, attn_layout, scale):
    q_tabs, (ck_ref, s1k_ref, s2k_ref, q_ref, ckv_ref, kpe_ref) = refs[:-6], refs[-6:]
    h = h_ref[0]
    q_lat = h[:, :MLA_Q_RANK]
    q_lat = (q_lat * lax.rsqrt(jnp.mean(q_lat * q_lat, axis=-1, keepdims=True) + EPS) * qn_ref[...]).astype(BF16)
    half = MLA_ROPE // 2
    if attn_layout:
        cos_t, sin_t = q_tabs[0][...], q_tabs[1][...]
        q_t = _dot_nt(wq_ref[...], q_lat)
        for hd in range(MLA_HEADS):
            r0 = hd * MLA_QK_PAD
            x1 = q_t[r0 + MLA_NOPE:r0 + MLA_NOPE + half, :]
            x2 = q_t[r0 + MLA_NOPE + half:r0 + MLA_NOPE + 2 * half, :]
            q_ref[0, r0:r0 + MLA_QK_PAD, :] = jnp.concatenate(
                [q_t[r0:r0 + MLA_NOPE, :] * scale, x1 * cos_t - x2 * sin_t, x1 * sin_t + x2 * cos_t,
                 q_t[r0 + MLA_NOPE + 2 * half:r0 + MLA_QK_PAD, :]], axis=0).astype(BF16)
    else:
        cq, s1q, s2q = (t[...] for t in q_tabs)
        q = jnp.dot(q_lat, wq_ref[...], preferred_element_type=F32)
        for hd in range(MLA_HEADS):
            sl = slice(hd * MLA_QK_PAD, (hd + 1) * MLA_QK_PAD)
            q_ref[0, :, sl] = _rope_lanes(q[:, sl], cq, s1q, s2q, half).astype(BF16)
    c_kv = h[:, MLA_Q_RANK:MLA_Q_RANK + MLA_KV_RANK]
    ckv_ref[0] = c_kv * lax.rsqrt(jnp.mean(c_kv * c_kv, axis=-1, keepdims=True) + EPS) * kn_ref[...]
    k_pe = h[:, MLA_Q_RANK + MLA_KV_RANK:]
    kpe_ref[0] = _rope_lanes(k_pe, ck_ref[...], s1k_ref[...], s2k_ref[...], MLA_ROPE // 2)[:, :MLA_ROPE]


def _mla_rows(h, p, pos, tm, attn_layout):
    bsz, seq, _ = h.shape
    tab_spec = pl.BlockSpec((tm, LANE), lambda b, t: (t, 0))
    n_q = MLA_HEADS * MLA_QK_PAD
    half = MLA_ROPE // 2
    scale = (MLA_NOPE + MLA_ROPE) ** -0.5 * LOG2E
    k_tabs = _rope_tables(pos, half, LANE, LANE, 0)
    if attn_layout:
        q_shape, q_spec = (bsz, n_q, seq), pl.BlockSpec((1, n_q, tm), lambda b, t: (b, 0, t))
        ang = (ROPE_BASE ** (-jnp.arange(half, dtype=F32) / half))[:, None] * pos.astype(F32)[None, :]
        q_tabs = (jnp.cos(ang) * scale, jnp.sin(ang) * scale)
        q_tab_specs = [pl.BlockSpec((half, tm), lambda b, t: (0, t))] * 2
        wq = p["wq"].T
    else:
        q_shape, q_spec = (bsz, seq, n_q), pl.BlockSpec((1, tm, n_q), lambda b, t: (b, t, 0))
        q_tabs = _rope_tables(pos, half, MLA_QK_PAD, MLA_QK_PAD, MLA_NOPE, scale=scale)
        q_tab_specs = [tab_spec] * 3
        wq = p["wq"]
    return pl.pallas_call(
        functools.partial(_mla_rows_body, attn_layout=attn_layout, scale=scale),
        grid=(bsz, seq // tm),
        in_specs=[
            pl.BlockSpec((1, tm, EVEN_IN_PAD - S5_WIDTH), lambda b, t: (b, t, 1)),
            _const_spec((1, MLA_Q_RANK)), _const_spec((1, MLA_KV_RANK)),
            _const_spec(wq.shape),
        ] + q_tab_specs + [tab_spec] * 3,
        out_specs=[
            q_spec,
            pl.BlockSpec((1, tm, MLA_KV_RANK), lambda b, t: (b, t, 0)),
            pl.BlockSpec((1, tm, MLA_ROPE), lambda b, t: (b, t, 0)),
        ],
        out_shape=[jax.ShapeDtypeStruct(q_shape, BF16),
                   jax.ShapeDtypeStruct((bsz, seq, MLA_KV_RANK), F32),
                   jax.ShapeDtypeStruct((bsz, seq, MLA_ROPE), F32)],
        compiler_params=_cparams(2, VMEM_LIMIT),
        name="mla_rows",
    )(h, p["q_norm"], p["kv_norm"], wq, *q_tabs, *k_tabs)


def _mla_kv_body(ckv_ref, kpe_ref, wkn_ref, wkp_ref, wv_ref, k_ref, v_ref):
    ckv = ckv_ref[0].astype(BF16)
    k = jnp.dot(ckv, wkn_ref[...], preferred_element_type=F32)
    k = k + jnp.dot(kpe_ref[0].astype(BF16), wkp_ref[...], preferred_element_type=F32)
    k_ref[0] = k.astype(BF16)
    v_t = jnp.dot(ckv, wv_ref[...], preferred_element_type=F32).T
    for hd in range(MLA_HEADS):
        v_ref[0, hd, 0] = v_t[hd * MLA_V:(hd + 1) * MLA_V, :].astype(BF16)


def _mla_kv(ckv, kpe, p):
    bsz, seq, _ = ckv.shape
    n_t = seq // TILE
    return pl.pallas_call(
        _mla_kv_body,
        grid=(bsz, n_t),
        in_specs=[
            pl.BlockSpec((1, TILE, MLA_KV_RANK), lambda b, t: (b, t, 0)),
            pl.BlockSpec((1, TILE, MLA_ROPE), lambda b, t: (b, t, 0)),
            _const_spec(p["wkn"].shape), _const_spec(p["wkp"].shape), _const_spec(p["wv"].shape),
        ],
        out_specs=[pl.BlockSpec((1, TILE, MLA_HEADS * MLA_QK_PAD), lambda b, t: (b, t, 0)),
                   pl.BlockSpec((1, MLA_HEADS, 1, MLA_V, TILE), lambda b, t: (b, 0, t, 0, 0))],
        out_shape=[jax.ShapeDtypeStruct((bsz, seq, MLA_HEADS * MLA_QK_PAD), BF16),
                   jax.ShapeDtypeStruct((bsz, MLA_HEADS, n_t, MLA_V, TILE), BF16)],
        compiler_params=_cparams(2, VMEM_LIMIT),
        name="mla_kv",
    )(ckv, kpe, p["wkn"], p["wkp"], p["wv"])


def _flash_body(*refs, heads, dq, dv, causal):
    if causal:
        qt_ref, k_ref, vt_ref, fq_ref, fk_ref, o_ref = refs
    else:
        qt_ref, k_ref, vt_ref, o_ref = refs
    tq = qt_ref.shape[2]
    n_tiles, tk = vt_ref.shape[2], vt_ref.shape[4]
    r0 = pl.program_id(1) * tq
    j_last = jnp.minimum((r0 + tq - 1) // tk, n_tiles - 1)
    q_row = r0 + lax.broadcasted_iota(jnp.int32, (1, tq), 1)
    last = q_row if causal else ((q_row >> CHUNK_SHIFT) << CHUNK_SHIFT) + (CHUNK - 1)
    first_last = r0 if causal else ((r0 >> CHUNK_SHIFT) << CHUNK_SHIFT) + (CHUNK - 1)
    n_open = jnp.clip((first_last + 1) // tk, 1, j_last + 1)
    q_t = [qt_ref[0, hd * dq:(hd + 1) * dq, :] for hd in range(heads)]

    def tile(j, carry, masked):
        start = pl.multiple_of(j * tk, tk)
        if masked:
            k_row = start + lax.broadcasted_iota(jnp.int32, (tk, 1), 0)
            ok = (k_row <= last) & (k_row >= PAD)
        scores = [jnp.dot(k_ref[0, pl.ds(start, tk), hd * dq:(hd + 1) * dq], q_t[hd],
                          preferred_element_type=F32) for hd in range(heads)]
        probs = []
        for hd in range(heads):
            m, l, _ = carry[hd]
            s = scores[hd]
            if causal:
                s = s + (fq_ref[0, hd:hd + 1, :] - fk_ref[0, pl.ds(start, tk), hd:hd + 1])
            if masked:
                s = jnp.where(ok, s, NEG)
            m_new = jnp.maximum(m, jnp.max(s, axis=0, keepdims=True))
            a = jnp.exp2(m - m_new)
            p = jnp.exp2(s - m_new)
            probs.append((m_new, a * l + jnp.sum(p, axis=0, keepdims=True), a, p.astype(BF16)))
        new = []
        for hd in range(heads):
            m_new, l, a, p = probs[hd]
            acc = a * carry[hd][2] + jnp.dot(vt_ref[0, hd, j], p, preferred_element_type=F32)
            new.append((m_new, l, acc))
        return tuple(new)

    init = tuple((jnp.full((1, tq), NEG, F32), jnp.zeros((1, tq), F32), jnp.zeros((dv, tq), F32))
                 for _ in range(heads))
    carry = tile(0, init, True)
    carry = lax.fori_loop(1, n_open, functools.partial(tile, masked=False), carry)
    carry = lax.fori_loop(n_open, j_last + 1, functools.partial(tile, masked=True), carry)
    outs = [carry[hd][2] / carry[hd][1] for hd in range(heads)]
    per = LANE // dv
    for g in range(heads // per):
        o_ref[0, :, g * LANE:(g + 1) * LANE] = jnp.concatenate(
            outs[g * per:(g + 1) * per], axis=0).T.astype(o_ref.dtype)


def _flash(q_t, k, v_t, fq_t, fk, *, heads, dq, dv, causal):
    bsz, _, n_q = q_t.shape
    n_keys = k.shape[1]
    in_specs = [
        pl.BlockSpec((1, heads * dq, TILE), lambda b, i: (b, 0, i)),
        pl.BlockSpec((1, n_keys, heads * dq), lambda b, i: (b, 0, 0)),
        pl.BlockSpec((1,) + v_t.shape[1:], lambda b, i: (b, 0, 0, 0, 0)),
    ]
    args = [q_t, k, v_t]
    if causal:
        in_specs += [pl.BlockSpec((1, heads, TILE), lambda b, i: (b, 0, i)),
                     pl.BlockSpec((1, n_keys, heads), lambda b, i: (b, 0, 0))]
        args += [fq_t, fk]
    return pl.pallas_call(
        functools.partial(_flash_body, heads=heads, dq=dq, dv=dv, causal=causal),
        grid=(bsz, n_q // TILE),
        in_specs=in_specs,
        out_specs=pl.BlockSpec((1, TILE, heads * dv), lambda b, i: (b, i, 0)),
        out_shape=jax.ShapeDtypeStruct((bsz, n_q, heads * dv), BF16),
        compiler_params=_cparams(2, VMEM_LIMIT),
        name="flash_causal" if causal else "flash_chunk",
    )(*args)


def _online_softmax(s, m_ref, l_ref, idx):
    m_old = m_ref[idx]
    m_new = jnp.maximum(m_old, jnp.max(s, axis=-1, keepdims=True))
    a = jnp.exp2(m_old - m_new)
    p = jnp.exp2(s - m_new)
    m_ref[idx] = m_new
    l_ref[idx] = a * l_ref[idx] + jnp.sum(p, axis=-1, keepdims=True)
    return a, p


def _dot_nt(a, b):
    return lax.dot_general(a, b, (((1,), (1,)), ((), ())), preferred_element_type=F32)


def _fox_decode_body(q_ref, fq_ref, kc_ref, vc_ref, fkc_ref, kn_ref, vn_ref, fkn_ref, o_ref,
                     qs_ref, m_ref, l_ref, acc_ref):
    t = pl.program_id(1)
    n_q = q_ref.shape[1]
    stacked = FOX_HEADS * n_q
    heads_rows = [slice(hd * n_q, (hd + 1) * n_q) for hd in range(FOX_HEADS)]
    heads_cols = [slice(hd * FOX_DIM, (hd + 1) * FOX_DIM) for hd in range(FOX_HEADS)]

    @pl.when(t == 0)
    def _():
        for rows, cols in zip(heads_rows, heads_cols):
            qs_ref[rows, :] = q_ref[0, :, cols]
        m_ref[...] = jnp.full(m_ref.shape, NEG, F32)
        l_ref[...] = jnp.zeros(l_ref.shape, F32)
        acc_ref[...] = jnp.zeros(acc_ref.shape, F32)

    pairs = lambda c: c.reshape(c.shape[0] * FOX_HEADS, FOX_DIM)
    kc = pairs(kc_ref[0, 0]).astype(BF16)
    n_pairs = kc.shape[0]
    s = _dot_nt(qs_ref[...], kc) + (fq_ref[0] - fkc_ref[0])
    row_head = lax.broadcasted_iota(jnp.int32, (stacked, 1), 0) >> (n_q.bit_length() - 1)
    col_head = lax.broadcasted_iota(jnp.int32, (1, n_pairs), 1) & (FOX_HEADS - 1)
    a, p = _online_softmax(jnp.where(row_head == col_head, s, NEG), m_ref, l_ref, 0)
    acc_ref[0] = a * acc_ref[0] + jnp.dot(p.astype(BF16), pairs(vc_ref[0, 0]).astype(BF16),
                                        preferred_element_type=F32)

    @pl.when(t == pl.num_programs(1) - 1)
    def _():
        causal = (lax.broadcasted_iota(jnp.int32, (n_q, n_q), 1) <= lax.broadcasted_iota(jnp.int32, (n_q, n_q), 0))
        for hd, (rows, cols) in enumerate(zip(heads_rows, heads_cols)):
            s = _dot_nt(q_ref[0, :, cols], kn_ref[0, :, cols]) + (fq_ref[0, rows, :] - fkn_ref[0, hd:hd + 1, :])
            s = jnp.where(causal, s, NEG)
            m_old = m_ref[0, rows, :]
            m_new = jnp.maximum(m_old, jnp.max(s, axis=-1, keepdims=True))
            a = jnp.exp2(m_old - m_new)
            p = jnp.exp2(s - m_new)
            l = a * l_ref[0, rows, :] + jnp.sum(p, axis=-1, keepdims=True)
            acc = a * acc_ref[0, rows, :] + jnp.dot(p.astype(BF16), vn_ref[0, :, cols], preferred_element_type=F32)
            o_ref[0, :, cols] = (acc / l).astype(o_ref.dtype)


def _fox_decode(q, k_new, v_new, cache_k, cache_v, layer, fcum, tk):
    bsz, n_q, width = q.shape
    n_past = cache_k.shape[2]
    assert n_q & (n_q - 1) == 0 and FOX_HEADS & (FOX_HEADS - 1) == 0
    stacked = FOX_HEADS * n_q
    fq = fcum[:, n_past:n_past + n_q].transpose(0, 2, 1)
    row_spec = pl.BlockSpec((1, n_q, width), lambda b, t: (b, 0, 0))
    cache_spec = pl.BlockSpec((1, 1, tk, FOX_HEADS, FOX_DIM), lambda b, t: (layer, b, t, 0, 0))
    return pl.pallas_call(
        _fox_decode_body,
        grid=(bsz, n_past // tk),
        in_specs=[row_spec, pl.BlockSpec((1, stacked, 1), lambda b, t: (b, 0, 0)),
                  cache_spec, cache_spec, pl.BlockSpec((1, 1, tk * FOX_HEADS), lambda b, t: (b, 0, t)),
                  row_spec, row_spec, pl.BlockSpec((1, FOX_HEADS, n_q), lambda b, t: (b, 0, 0))],
        out_specs=row_spec,
        out_shape=jax.ShapeDtypeStruct((bsz, n_q, width), BF16),
        scratch_shapes=[pltpu.VMEM((stacked, FOX_DIM), BF16), pltpu.VMEM((1, stacked, 1), F32),
                        pltpu.VMEM((1, stacked, 1), F32), pltpu.VMEM((1, stacked, FOX_DIM), F32)],
        compiler_params=_cparams(2, VMEM_LIMIT),
        name="fox_decode",
    )(q, fq.reshape(bsz, stacked, 1), cache_k, cache_v,
      fcum[:, :n_past].reshape(bsz, 1, n_past * FOX_HEADS), k_new, v_new, fq)


def _mla_decode_body(q_ref, wkn_ref, wv_ref, cc_ref, pc_ref, cn_ref, pn_ref, o_ref, qa_ref, qr_ref,
                     m_ref, l_ref, acc_ref):
    t = pl.program_id(1)
    n_q = q_ref.shape[1]

    @pl.when(t == 0)
    def _():
        for hd in range(MLA_HEADS):
            c0 = hd * MLA_QK_PAD
            rows = slice(hd * n_q, (hd + 1) * n_q)
            qa_ref[rows, :] = _dot_nt(q_ref[0, :, c0:c0 + MLA_NOPE],
                                      wkn_ref[:, c0:c0 + MLA_NOPE]).astype(BF16)
            qr_ref[rows, :] = q_ref[0, :, c0 + MLA_NOPE:c0 + MLA_NOPE + MLA_ROPE]
        m_ref[...] = jnp.full(m_ref.shape, NEG, F32)
        l_ref[...] = jnp.zeros(l_ref.shape, F32)
        acc_ref[...] = jnp.zeros(acc_ref.shape, F32)

    def attend(ckv, kpe):
        s = _dot_nt(qa_ref[...], ckv) + _dot_nt(qr_ref[...], kpe)
        a, p = _online_softmax(s, m_ref, l_ref, 0)
        acc_ref[0] = a * acc_ref[0] + jnp.dot(p.astype(BF16), ckv, preferred_element_type=F32)

    attend(cc_ref[0, 0].astype(BF16), pc_ref[0, 0].astype(BF16))

    @pl.when(t == pl.num_programs(1) - 1)
    def _():
        attend(cn_ref[0].astype(BF16), pn_ref[0].astype(BF16))
        lat = (acc_ref[0] / l_ref[0]).astype(BF16)
        for hd in range(MLA_HEADS):
            cols = slice(hd * MLA_V, (hd + 1) * MLA_V)
            o_ref[0, :, cols] = jnp.dot(lat[hd * n_q:(hd + 1) * n_q, :], wv_ref[:, cols],
                                        preferred_element_type=F32).astype(o_ref.dtype)


def _mla_decode(q, ckv_new, kpe_new, cache_ckv, cache_kpe, layer, p, tk):
    bsz, n_q, _ = q.shape
    n_past = cache_ckv.shape[2]
    stacked = MLA_HEADS * n_q
    return pl.pallas_call(
        _mla_decode_body,
        grid=(bsz, n_past // tk),
        in_specs=[pl.BlockSpec((1, n_q, MLA_HEADS * MLA_QK_PAD), lambda b, t: (b, 0, 0)),
                  _const_spec(p["wkn"].shape), _const_spec(p["wv"].shape),
                  pl.BlockSpec((1, 1, tk, MLA_KV_RANK), lambda b, t: (layer, b, t, 0)),
                  pl.BlockSpec((1, 1, tk, MLA_ROPE), lambda b, t: (layer, b, t, 0)),
                  pl.BlockSpec((1, n_q, MLA_KV_RANK), lambda b, t: (b, 0, 0)),
                  pl.BlockSpec((1, n_q, MLA_ROPE), lambda b, t: (b, 0, 0))],
        out_specs=pl.BlockSpec((1, n_q, MLA_HEADS * MLA_V), lambda b, t: (b, 0, 0)),
        out_shape=jax.ShapeDtypeStruct((bsz, n_q, MLA_HEADS * MLA_V), BF16),
        scratch_shapes=[pltpu.VMEM((stacked, MLA_KV_RANK), BF16), pltpu.VMEM((stacked, MLA_ROPE), BF16),
                        pltpu.VMEM((1, stacked, 1), F32), pltpu.VMEM((1, stacked, 1), F32),
                        pltpu.VMEM((1, stacked, MLA_KV_RANK), F32)],
        compiler_params=_cparams(2, VMEM_LIMIT),
        name="mla_decode",
    )(q, p["wkn"], p["wv"], cache_ckv, cache_kpe, ckv_new, kpe_new)


def _gate_body(x_ref, b_ref, lf_ref, fc_ref, *, new_start, first_row):
    n_rows = x_ref.shape[1]
    tri = (lax.broadcasted_iota(jnp.int32, (TILE, TILE), 0)
           >= lax.broadcasted_iota(jnp.int32, (TILE, TILE), 1)).astype(F32)
    carry = jnp.zeros((1, x_ref.shape[2]), F32)
    for i in range(n_rows // TILE):
        sl = slice(i * TILE, (i + 1) * TILE)
        x = x_ref[0, sl, :]
        rows = i * TILE + lax.broadcasted_iota(jnp.int32, (TILE, 1), 0)
        z = x + b_ref[...]
        lf = jnp.where(rows >= new_start, jnp.minimum(z, 0.0) - jnp.log1p(jnp.exp(-jnp.abs(z))), x)
        if first_row:
            lf = jnp.where(rows >= first_row, lf, 0.0)
        lf_ref[0, sl, :] = lf
        cs = jnp.dot(tri, lf, preferred_element_type=F32, precision=lax.Precision.HIGHEST) + carry
        fc_ref[0, sl, :] = cs
        carry = cs[TILE - 1:TILE, :]


def _gate(x, b_f, new_start, first_row):
    bsz, n_rows, heads = x.shape
    spec = pl.BlockSpec((1, n_rows, heads), lambda b: (b, 0, 0))
    return pl.pallas_call(
        functools.partial(_gate_body, new_start=new_start, first_row=first_row),
        grid=(bsz,),
        in_specs=[spec, _const_spec((1, heads))],
        out_specs=[spec, spec],
        out_shape=[jax.ShapeDtypeStruct(x.shape, F32)] * 2,
        compiler_params=_cparams(1),
        name="fox_gate",
    )(x, b_f)


RET_LOG_GAMMA = tuple(math.log(1.0 - 2.0 ** (-5.0 - h)) for h in range(RET_HEADS))


def _ret_body(rq_ref, rk_ref, rv_ref, rg_ref, s0_ref, dec_ref, cq_ref, s1q_ref, s2q_ref, ck_ref, s1k_ref, s2k_ref,
              o_ref, sl_ref, st_ref, *, n_tail):
    c = pl.program_id(1)
    ct = rq_ref.shape[1]

    @pl.when(c == 0)
    def _():
        st_ref[...] = s0_ref[0]

    q = _rope_lanes(rq_ref[0], cq_ref[...], s1q_ref[...], s2q_ref[...], RET_DK // 2)
    k = _rope_lanes(rk_ref[0], ck_ref[...], s1k_ref[...], s2k_ref[...], RET_DK // 2)
    v = rv_ref[0].astype(BF16)
    g = rg_ref[0]
    j = lax.broadcasted_iota(jnp.int32, (ct, 1), 0).astype(F32)
    for hd in range(RET_HEADS):
        lg = RET_LOG_GAMMA[hd]
        q_h = q[:, hd * RET_DK:(hd + 1) * RET_DK]
        k_h = k[:, hd * RET_DK:(hd + 1) * RET_DK]
        v_h = v[:, hd * RET_DV:(hd + 1) * RET_DV]
        scores = lax.dot_general(q_h.astype(BF16), k_h.astype(BF16), (((1,), (1,)), ((), ())),
                                 preferred_element_type=F32) * dec_ref[hd]
        s_h = st_ref[hd]
        out = jnp.dot(scores.astype(BF16), v_h, preferred_element_type=F32)
        out = out + jnp.dot((q_h * jnp.exp(lg * (j + 1.0))).astype(BF16), s_h.astype(BF16),
                            preferred_element_type=F32)
        k_dec = (k_h * jnp.exp(lg * (ct - 1.0 - j))).astype(BF16)
        st_ref[hd] = math.exp(lg * ct) * s_h + lax.dot_general(
            k_dec, v_h, (((0,), (0,)), ((), ())), preferred_element_type=F32)
        mu = jnp.mean(out, axis=-1, keepdims=True)
        oc = out - mu
        var = jnp.mean(oc * oc, axis=-1, keepdims=True)
        g_h = g[:, hd * RET_DV:(hd + 1) * RET_DV]
        o_ref[0, :, hd * RET_DV:(hd + 1) * RET_DV] = (
            g_h * jax.nn.sigmoid(g_h) * (oc * lax.rsqrt(var + EPS))).astype(o_ref.dtype)

    @pl.when(c == pl.num_programs(1) - 1)
    def _():
        for hd in range(RET_HEADS):
            sl_ref[0, hd] = st_ref[hd] * math.exp(-RET_LOG_GAMMA[hd] * n_tail)


def _retention(h, s0, tabs, ct, n_tail):
    bsz, seq, _ = h.shape
    tab_spec = pl.BlockSpec((ct, RET_QK), lambda b, c: (c, 0))
    st_spec = pl.BlockSpec((1, RET_HEADS, RET_DK, RET_DV), lambda b, c: (b, 0, 0, 0))
    rq_blk = (2 * FOX_W) // RET_QK
    rv_blk = (2 * FOX_W + 2 * RET_QK) // RET_VW
    diff = jnp.arange(ct, dtype=F32)[:, None] - jnp.arange(ct, dtype=F32)[None, :]
    decay = jnp.where(diff >= 0.0, jnp.exp(jnp.asarray(RET_LOG_GAMMA, F32)[:, None, None] * jnp.maximum(diff, 0.0)),
                      0.0)
    return pl.pallas_call(
        functools.partial(_ret_body, n_tail=n_tail),
        grid=(bsz, seq // ct),
        in_specs=[
            pl.BlockSpec((1, ct, RET_QK), lambda b, c: (b, c, rq_blk)),
            pl.BlockSpec((1, ct, RET_QK), lambda b, c: (b, c, rq_blk + 1)),
            pl.BlockSpec((1, ct, RET_VW), lambda b, c: (b, c, rv_blk)),
            pl.BlockSpec((1, ct, RET_VW), lambda b, c: (b, c, rv_blk + 1)),
            st_spec, _const_spec((RET_HEADS, ct, ct)),
        ] + [tab_spec] * 6,
        out_specs=[pl.BlockSpec((1, ct, RET_VW), lambda b, c: (b, c, 0)), st_spec],
        out_shape=[jax.ShapeDtypeStruct((bsz, seq, RET_VW), BF16),
                   jax.ShapeDtypeStruct((bsz, RET_HEADS, RET_DK, RET_DV), F32)],
        scratch_shapes=[pltpu.VMEM((RET_HEADS, RET_DK, RET_DV), F32)],
        compiler_params=_cparams(2, VMEM_LIMIT),
        name="retention",
    )(h, h, h, h, s0, decay, *tabs)


def _rope_tables(pos, half, width, group, offset, scale=1.0, valid=None):
    inv = ROPE_BASE ** (-jnp.arange(half, dtype=F32) / half)
    ang = pos.astype(F32)[:, None] * inv[None, :]
    cos, sin = jnp.cos(ang), jnp.sin(ang)
    n = pos.shape[0]
    one, zero = jnp.ones((n, 1), F32), jnp.zeros((n, 1), F32)

    def lanes(first, second, other):
        grp = jnp.concatenate([jnp.broadcast_to(other, (n, offset)), first, second,
                               jnp.broadcast_to(other, (n, group - offset - 2 * half))], axis=1)
        return jnp.tile(grp, (1, width // group))

    tabs = (lanes(cos, cos, one), lanes(-sin, 0.0 * sin, zero), lanes(0.0 * sin, sin, zero))
    if valid is not None:
        tabs = tuple(jnp.where(valid[:, None], t, 0.0) for t in tabs)
    return tuple(t * scale for t in tabs)


def _even_params(e, w):
    p = {}
    p["w_in"] = jnp.pad(w["even_w_in"][e], ((0, 0), (0, EVEN_IN_PAD - EVEN_IN))).astype(BF16)
    p["w_out"] = w["even_w_out"][e].astype(BF16)
    lam_re, lam_im = w["s5_a_re"][e].astype(F32), w["s5_a_im"][e].astype(F32)
    dt = jnp.exp(w["s5_log_dt"][e].astype(F32))[:, None]
    p["lam_dt_re"], p["lam_dt_im"] = lam_re * dt, lam_im * dt
    mag = jnp.exp(lam_re * dt)
    abar_re, abar_im = mag * jnp.cos(lam_im * dt), mag * jnp.sin(lam_im * dt)
    den = lam_re * lam_re + lam_im * lam_im
    f_re = ((abar_re - 1.0) * lam_re + abar_im * lam_im) / den
    f_im = (abar_im * lam_re - (abar_re - 1.0) * lam_im) / den
    b_re, b_im = w["s5_b_re"][e].astype(F32), w["s5_b_im"][e].astype(F32)
    bb_re = f_re[..., None] * b_re - f_im[..., None] * b_im
    bb_im = f_re[..., None] * b_im + f_im[..., None] * b_re
    eye = jnp.eye(S5_GROUPS, dtype=F32)

    def in_blocks(x):
        return jnp.einsum("gnc,gh->gchn", x, eye).reshape(S5_WIDTH, S5_HALF)

    def out_blocks(x):
        return jnp.einsum("gcn,gh->gnhc", x, eye).reshape(S5_HALF, S5_WIDTH)

    p["bbig"] = jnp.concatenate([in_blocks(bb_re), in_blocks(bb_im)], axis=1).astype(BF16)
    p["cbig"] = jnp.concatenate([out_blocks(w["s5_c_re"][e].astype(F32)),
                                 out_blocks(-w["s5_c_im"][e].astype(F32))], axis=0).astype(BF16)
    p["a_re"] = abar_re.reshape(1, S5_HALF)
    p["a_im"] = abar_im.reshape(1, S5_HALF)
    p["d"] = w["s5_d"][e].astype(F32).reshape(1, S5_WIDTH)
    p["w_glu"] = w["s5_w_glu"][e].astype(BF16)
    p["b_glu"] = w["s5_b_glu"][e].astype(F32).reshape(1, S5_WIDTH)
    p["q_norm"] = w["mla_q_norm"][e].astype(F32).reshape(1, MLA_Q_RANK)
    p["kv_norm"] = w["mla_kv_norm"][e].astype(F32).reshape(1, MLA_KV_RANK)
    wq = w["mla_w_uq"][e].reshape(MLA_Q_RANK, MLA_HEADS, MLA_NOPE + MLA_ROPE)
    wq = jnp.pad(wq, ((0, 0), (0, 0), (0, MLA_QK_PAD - MLA_NOPE - MLA_ROPE)))
    p["wq"] = wq.reshape(MLA_Q_RANK, MLA_HEADS * MLA_QK_PAD).astype(BF16)
    wkv = w["mla_w_ukv"][e].reshape(MLA_KV_RANK, MLA_HEADS, MLA_NOPE + MLA_V)
    wkn = jnp.pad(wkv[:, :, :MLA_NOPE], ((0, 0), (0, 0), (0, MLA_QK_PAD - MLA_NOPE)))
    p["wkn"] = wkn.reshape(MLA_KV_RANK, MLA_HEADS * MLA_QK_PAD).astype(BF16)
    place = jnp.pad(jnp.eye(MLA_ROPE, dtype=F32), ((0, 0), (MLA_NOPE, MLA_QK_PAD - MLA_NOPE - MLA_ROPE)))
    p["wkp"] = jnp.tile(place, (1, MLA_HEADS)).astype(BF16)
    p["wv"] = wkv[:, :, MLA_NOPE:].reshape(MLA_KV_RANK, MLA_HEADS * MLA_V).astype(BF16)
    return p


def _odd_params(o, w):
    w_in = w["odd_w_in"][o]
    c_logit = 3 * FOX_W
    cols = jnp.concatenate([
        w_in[:, :FOX_W] * (FOX_DIM ** -0.5),
        w_in[:, FOX_W:c_logit],
        w_in[:, c_logit + FOX_HEADS:],
        w_in[:, c_logit:c_logit + FOX_HEADS],
    ], axis=1)
    p = {"w_in": jnp.pad(cols, ((0, 0), (0, ODD_IN_PAD - cols.shape[1]))).astype(BF16)}
    p["w_out"] = w["odd_w_out"][o].astype(BF16)
    p["b_f"] = w["fox_b_f"][o].astype(F32).reshape(1, FOX_HEADS)
    return p


def _cache_tile(n_past):
    for tk in (512, 256, 128):
        if n_past % tk == 0:
            return tk
    raise ValueError("cache length must be a multiple of 128")


def _trunk(x, pos, n_real, out_rows, past, w, ffn, evens, odds):
    bsz, seq, _ = x.shape
    rows = bsz * seq
    prompt = past is None
    first_row = PAD if prompt else 0
    if prompt:
        tm = 2 * TILE if rows % (2 * TILE) == 0 else TILE
        rt = TILE
    else:
        n_past = past["cache_fox_k"].shape[2]
        tk = _cache_tile(n_past)
        tm, rt = rows, seq
    idx = jnp.arange(seq, dtype=jnp.int32)
    valid = (idx >= first_row) & (idx < n_real)
    st = {n: [] for n in ("mla_ckv", "mla_kpe", "s5_re", "s5_im", "fox_k", "fox_v", "fox_logf", "ret")}
    rq_tabs = _rope_tables(pos, RET_DK // 2, RET_QK, RET_DK, 0)
    rk_tabs = _rope_tables(pos, RET_DK // 2, RET_QK, RET_DK, 0, scale=RET_DK ** -0.5, valid=valid)
    x2 = x.reshape(rows, D_MODEL)

    def ln(l, i):
        return w["ln_g"][l, i].reshape(1, D_MODEL), w["ln_b"][l, i].reshape(1, D_MODEL)

    for l in range(DEPTH):
        x2 = _ffn_ln(x2, *ffn[l][0], *ln(l, 0), tm)
        if l % 2 == 0:
            e = l // 2
            p = evens[e]
            h = _proj(x2, p["w_in"], tm).reshape(bsz, seq, EVEN_IN_PAD)
            if prompt:
                h0r = h0i = jnp.zeros((bsz, 1, S5_HALF), F32)
            else:
                h0r = past["state_s5_re"][e].astype(F32).reshape(bsz, 1, S5_HALF)
                h0i = past["state_s5_im"][e].astype(F32).reshape(bsz, 1, S5_HALF)
            s5_out, hlr, hli = _s5(h, h0r, h0i, p, rt, first_row, n_real)
            q, ckv, kpe = _mla_rows(h, p, pos, rt, attn_layout=prompt)
            if prompt:
                k_att, v_att = _mla_kv(ckv, kpe, p)
                mla_out = _flash(q, k_att, v_att, None, None, heads=MLA_HEADS, dq=MLA_QK_PAD, dv=MLA_V,
                                 causal=False)
            else:
                mla_out = _mla_decode(q, ckv, kpe, past["cache_mla_ckv"], past["cache_mla_kpe"], e, p, tk)
            mix = (s5_out.reshape(rows, S5_WIDTH), mla_out.reshape(rows, MLA_HEADS * MLA_V), p["w_out"])
            st["mla_ckv"].append(ckv)
            st["mla_kpe"].append(kpe)
            st["s5_re"].append(hlr.reshape(bsz, S5_GROUPS, S5_STATE))
            st["s5_im"].append(hli.reshape(bsz, S5_GROUPS, S5_STATE))
        else:
            o = l // 2
            p = odds[o]
            h, fq, fk16, fv = _proj_odd(x2, p["w_in"], bsz, TILE if prompt else tm, attn_layout=prompt)
            h = h.reshape(bsz, seq, ODD_H)
            fk16 = fk16.reshape(bsz, seq, FOX_W)
            f_logit = h[:, :, ODD_LOGIT_COL:ODD_LOGIT_COL + FOX_HEADS]
            if prompt:
                q_off = 0
                logf, fcum = _gate(f_logit, p["b_f"], 0, PAD)
                fcum = fcum * LOG2E
                fox_out = _flash(fq, fk16, fv, fcum.transpose(0, 2, 1), fcum, heads=FOX_HEADS, dq=FOX_DIM,
                                 dv=FOX_DIM, causal=True)
                s0 = jnp.zeros((bsz, RET_HEADS, RET_DK, RET_DV), F32)
            else:
                q_off = n_past
                gates = jnp.concatenate([past["cache_fox_logf"][o].astype(F32), f_logit], axis=1)
                gates = jnp.pad(gates, ((0, 0), (0, _round_up(n_past + seq, TILE) - n_past - seq), (0, 0)))
                logf, fcum = _gate(gates, p["b_f"], n_past, 0)
                fox_out = _fox_decode(fq.reshape(bsz, seq, FOX_W), fk16, fv.reshape(bsz, seq, FOX_W),
                                      past["cache_fox_k"], past["cache_fox_v"], o, fcum * LOG2E, tk)
                s0 = past["state_ret"][o].astype(F32)
            ret_out, s_last = _retention(h, s0, rq_tabs + rk_tabs, rt, seq - n_real)
            mix = (fox_out.reshape(rows, FOX_W), ret_out.reshape(rows, RET_VW), p["w_out"])
            st["fox_k"].append(h[:, :, :FOX_W].reshape(bsz, seq, FOX_HEADS, FOX_DIM))
            st["fox_v"].append(h[:, :, FOX_W:2 * FOX_W].reshape(bsz, seq, FOX_HEADS, FOX_DIM))
            st["fox_logf"].append(logf[:, q_off:q_off + seq])
            st["ret"].append(s_last)
        keep = (seq,) + out_rows if l == DEPTH - 1 and out_rows != (0, seq) else None
        x2 = _mix_ffn_ln(x2, *mix, ln(l, 1), ffn[l][1], ln(l, 2), tm, keep)
    return x2.reshape(bsz, out_rows[1], D_MODEL), {n: jnp.stack(a) for n, a in st.items()}


def kernel(x_prompt, x_sample, cache_mla_ckv, cache_mla_kpe, cache_fox_k, cache_fox_v, cache_fox_logf,
           state_s5_re, state_s5_im, state_ret, meta_tokens, ln_g, ln_b, ffn_w_gate, ffn_w_up, ffn_w_down,
           even_w_in, even_w_out, s5_a_re, s5_a_im, s5_b_re, s5_b_im, s5_c_re, s5_c_im, s5_d, s5_log_dt,
           s5_w_glu, s5_b_glu, mla_q_norm, mla_kv_norm, mla_w_uq, mla_w_ukv, odd_w_in, odd_w_out, fox_b_f):
    w = dict(ln_g=ln_g.astype(F32), ln_b=ln_b.astype(F32), even_w_in=even_w_in, even_w_out=even_w_out,
             s5_a_re=s5_a_re, s5_a_im=s5_a_im, s5_b_re=s5_b_re, s5_b_im=s5_b_im, s5_c_re=s5_c_re,
             s5_c_im=s5_c_im, s5_d=s5_d, s5_log_dt=s5_log_dt, s5_w_glu=s5_w_glu, s5_b_glu=s5_b_glu,
             mla_q_norm=mla_q_norm, mla_kv_norm=mla_kv_norm, mla_w_uq=mla_w_uq, mla_w_ukv=mla_w_ukv,
             odd_w_in=odd_w_in, odd_w_out=odd_w_out, fox_b_f=fox_b_f)
    past = dict(cache_mla_ckv=cache_mla_ckv, cache_mla_kpe=cache_mla_kpe, cache_fox_k=cache_fox_k,
                cache_fox_v=cache_fox_v, cache_fox_logf=cache_fox_logf, state_s5_re=state_s5_re,
                state_s5_im=state_s5_im, state_ret=state_ret)
    ffn = [[(ffn_w_gate[l, i].astype(BF16), ffn_w_up[l, i].astype(BF16), ffn_w_down[l, i].astype(BF16))
            for i in range(2)] for l in range(DEPTH)]
    evens = [_even_params(e, w) for e in range((DEPTH + 1) // 2)]
    odds = [_odd_params(o, w) for o in range(DEPTH // 2)]

    bsz, seq, _ = x_prompt.shape
    n_real = PAD + N_META + seq
    n_rows = _round_up(n_real, TILE)
    meta = jnp.broadcast_to(meta_tokens[None].astype(x_prompt.dtype), (bsz, N_META, D_MODEL))
    xp = jnp.concatenate([jnp.zeros((bsz, PAD, D_MODEL), x_prompt.dtype), meta, x_prompt,
                          jnp.zeros((bsz, n_rows - n_real, D_MODEL), x_prompt.dtype)], axis=1)
    pos_p = jnp.maximum(jnp.arange(n_rows, dtype=jnp.int32) - PAD, 0)
    y_p, st_p = _trunk(xp, pos_p, n_real, (PAD + N_META, seq), None, w, ffn, evens, odds)
    d_seq = x_sample.shape[1]
    pos_s = N_META + cache_fox_k.shape[2] + jnp.arange(d_seq, dtype=jnp.int32)
    y_s, st_s = _trunk(x_sample, pos_s, d_seq, (0, d_seq), past, w, ffn, evens, odds)

    def real(a):
        return a[:, :, PAD:n_real]

    return (y_p, y_s,
            real(st_p["mla_ckv"]), real(st_p["mla_kpe"]), real(st_p["fox_k"]), real(st_p["fox_v"]),
            real(st_p["fox_logf"]), st_p["s5_re"], st_p["s5_im"], st_p["ret"],
            st_s["mla_ckv"], st_s["mla_kpe"], st_s["fox_k"], st_s["fox_v"], st_s["fox_logf"],
            st_s["s5_re"], st_s["s5_im"], st_s["ret"])
```

```python
import functools
import math

import jax
import jax.numpy as jnp
from jax import lax
from jax.experimental import pallas as pl
from jax.experimental.pallas import tpu as pltpu

F32 = jnp.float32
BF16 = jnp.bfloat16

D_MODEL = 1024
DEPTH = 4
CHUNK = 64
CHUNK_SHIFT = 6
N_META = 16
S5_WIDTH = 512
S5_CH = 16
S5_GROUPS = S5_WIDTH // S5_CH
S5_STATE = 64
S5_HALF = S5_GROUPS * S5_STATE
MLA_HEADS = 8
MLA_Q_RANK = 256
MLA_KV_RANK = 128
MLA_NOPE = 64
MLA_ROPE = 32
MLA_V = 64
MLA_QK_PAD = 128
FOX_HEADS = 8
FOX_DIM = 64
FOX_W = FOX_HEADS * FOX_DIM
RET_HEADS = 4
RET_DK = 64
RET_DV = 128
RET_QK = RET_HEADS * RET_DK
RET_VW = RET_HEADS * RET_DV
D_FF = 2816
ROPE_BASE = 10000.0
ALPHA = (2.0 * DEPTH) ** 0.25
EPS = 1e-5
NEG = -1e30
LOG2E = math.log2(math.e)
EVEN_IN = S5_WIDTH + MLA_Q_RANK + MLA_KV_RANK + MLA_ROPE
EVEN_IN_PAD = 1024
ODD_IN_PAD = 3200
ODD_H = ODD_IN_PAD - FOX_W
ODD_LOGIT_COL = 2 * FOX_W + 2 * RET_QK + 2 * RET_VW

PAD = CHUNK - N_META
LANE = 128
SUBLANE = 8
TILE = 256
VMEM_LIMIT = 56 * 1024 * 1024


def _cparams(n_grid, vmem=None):
    return pltpu.CompilerParams(dimension_semantics=("arbitrary",) * n_grid, vmem_limit_bytes=vmem)


def _const_spec(shape):
    nd = len(shape)
    return pl.BlockSpec(shape, lambda *_: (0,) * nd)


def _round_up(n, m):
    return -(-n // m) * m


def _layer_norm_rows(z, g, b):
    mu = jnp.mean(z, axis=-1, keepdims=True)
    zc = z - mu
    var = jnp.mean(zc * zc, axis=-1, keepdims=True)
    return zc * lax.rsqrt(var + EPS) * g + b


def _ffn_rows(x, wg_ref, wu_ref, wd_ref, g_ref, b_ref, hid_ref):
    xb = x.astype(BF16)
    for c in range(D_FF // TILE):
        sl = slice(c * TILE, (c + 1) * TILE)
        hg = jnp.dot(xb, wg_ref[:, sl], preferred_element_type=F32)
        hu = jnp.dot(xb, wu_ref[:, sl], preferred_element_type=F32)
        hid_ref[:, sl] = (hg * jax.nn.sigmoid(hg) * hu).astype(BF16)
    y = jnp.dot(hid_ref[...], wd_ref[...], preferred_element_type=F32)
    return _layer_norm_rows(ALPHA * x + 0.5 * y, g_ref[...], b_ref[...])


def _ffn_body(x_ref, wg_ref, wu_ref, wd_ref, g_ref, b_ref, o_ref, hid_ref):
    o_ref[...] = _ffn_rows(x_ref[...], wg_ref, wu_ref, wd_ref, g_ref, b_ref, hid_ref)


def _ffn_weight_specs():
    once = dict(pipeline_mode=pl.Buffered(1))
    return [pl.BlockSpec((D_MODEL, D_FF), lambda i: (0, 0), **once),
            pl.BlockSpec((D_MODEL, D_FF), lambda i: (0, 0), **once),
            pl.BlockSpec((D_FF, D_MODEL), lambda i: (0, 0), **once),
            _const_spec((1, D_MODEL)), _const_spec((1, D_MODEL))]


def _ffn_ln(x, wg, wu, wd, g, b, tm):
    rows = x.shape[0]
    return pl.pallas_call(
        _ffn_body,
        grid=(rows // tm,),
        in_specs=[pl.BlockSpec((tm, D_MODEL), lambda i: (i, 0))] + _ffn_weight_specs(),
        out_specs=pl.BlockSpec((tm, D_MODEL), lambda i: (i, 0)),
        out_shape=jax.ShapeDtypeStruct((rows, D_MODEL), F32),
        scratch_shapes=[pltpu.VMEM((tm, D_FF), BF16)],
        compiler_params=_cparams(1, VMEM_LIMIT),
        name="ffn_ln",
    )(x, wg, wu, wd, g, b)


def _mix_ffn_body(x_ref, a1_ref, a2_ref, wo_ref, g1_ref, b1_ref, wg_ref, wu_ref, wd_ref, g2_ref, b2_ref,
                  o_ref, hid_ref):
    k1 = a1_ref.shape[1]
    y = jnp.dot(a1_ref[...], wo_ref[:k1, :], preferred_element_type=F32)
    y = y + jnp.dot(a2_ref[...], wo_ref[k1:, :], preferred_element_type=F32)
    x1 = _layer_norm_rows(ALPHA * x_ref[...] + y, g1_ref[...], b1_ref[...])
    o_ref[...] = _ffn_rows(x1, wg_ref, wu_ref, wd_ref, g2_ref, b2_ref, hid_ref)


def _mix_ffn_ln(x, a1, a2, w_out, ln1, ffn_w, ln2, tm, keep=None):
    rows = x.shape[0]
    if keep is None:
        n_steps = rows // tm
        row_spec = lambda n: pl.BlockSpec((tm, n), lambda i: (i, 0))
    else:
        seq, start, count = keep
        per = count // tm
        n_steps = (rows // seq) * per
        align = math.gcd(seq, start, tm)
        row_spec = lambda n: pl.BlockSpec(
            (pl.Element(tm), pl.Element(n)),
            lambda i: (pl.multiple_of((i // per) * seq + start + (i % per) * tm, align), 0))
    return pl.pallas_call(
        _mix_ffn_body,
        grid=(n_steps,),
        in_specs=[row_spec(D_MODEL), row_spec(a1.shape[1]), row_spec(a2.shape[1]),
                  pl.BlockSpec(w_out.shape, lambda i: (0, 0), pipeline_mode=pl.Buffered(1)),
                  _const_spec((1, D_MODEL)), _const_spec((1, D_MODEL))] + _ffn_weight_specs(),
        out_specs=pl.BlockSpec((tm, D_MODEL), lambda i: (i, 0)),
        out_shape=jax.ShapeDtypeStruct((n_steps * tm, D_MODEL), F32),
        scratch_shapes=[pltpu.VMEM((tm, D_FF), BF16)],
        compiler_params=_cparams(1, VMEM_LIMIT),
        name="mix_ffn_ln",
    )(x, a1, a2, w_out, *ln1, *ffn_w, *ln2)


def _proj_body(x_ref, w_ref, o_ref):
    o_ref[...] = jnp.dot(x_ref[...].astype(BF16), w_ref[...], preferred_element_type=F32)


def _proj(x, w, tm):
    rows, n = x.shape[0], w.shape[1]
    return pl.pallas_call(
        _proj_body,
        grid=(rows // tm,),
        in_specs=[pl.BlockSpec((tm, D_MODEL), lambda i: (i, 0)), _const_spec(w.shape)],
        out_specs=pl.BlockSpec((tm, n), lambda i: (i, 0)),
        out_shape=jax.ShapeDtypeStruct((rows, n), F32),
        compiler_params=_cparams(1, VMEM_LIMIT),
        name="in_proj_even",
    )(x, w)


def _proj_odd_body(x_ref, w_ref, h_ref, q_ref, k_ref, v_ref, *, attn_layout):
    y = jnp.dot(x_ref[...].astype(BF16), w_ref[...], preferred_element_type=F32)
    h_ref[...] = y[:, FOX_W:]
    k_ref[...] = y[:, FOX_W:2 * FOX_W].astype(BF16)
    q = y[:, :FOX_W] * LOG2E
    if attn_layout:
        q_ref[0] = q.T.astype(BF16)
        v_t = y[:, 2 * FOX_W:3 * FOX_W].T
        for hd in range(FOX_HEADS):
            v_ref[0, hd, 0] = v_t[hd * FOX_DIM:(hd + 1) * FOX_DIM, :].astype(BF16)
    else:
        q_ref[...] = q.astype(BF16)
        v_ref[...] = y[:, 2 * FOX_W:3 * FOX_W].astype(BF16)


def _proj_odd(x, w, bsz, tm, attn_layout):
    rows = x.shape[0]
    seq = rows // bsz
    row_spec = lambda n: pl.BlockSpec((tm, n), lambda i: (i, 0))
    if attn_layout:
        assert tm == TILE and seq % TILE == 0
        n_t = seq // TILE
        q_shape = jax.ShapeDtypeStruct((bsz, FOX_W, seq), BF16)
        q_spec = pl.BlockSpec((1, FOX_W, TILE), lambda i: (i // n_t, 0, i % n_t))
        v_shape = jax.ShapeDtypeStruct((bsz, FOX_HEADS, n_t, FOX_DIM, TILE), BF16)
        v_spec = pl.BlockSpec((1, FOX_HEADS, 1, FOX_DIM, TILE), lambda i: (i // n_t, 0, i % n_t, 0, 0))
    else:
        q_shape = v_shape = jax.ShapeDtypeStruct((rows, FOX_W), BF16)
        q_spec = v_spec = row_spec(FOX_W)
    return pl.pallas_call(
        functools.partial(_proj_odd_body, attn_layout=attn_layout),
        grid=(rows // tm,),
        in_specs=[row_spec(D_MODEL), _const_spec(w.shape)],
        out_specs=[row_spec(ODD_H), q_spec, row_spec(FOX_W), v_spec],
        out_shape=[jax.ShapeDtypeStruct((rows, ODD_H), F32), q_shape,
                   jax.ShapeDtypeStruct((rows, FOX_W), BF16), v_shape],
        compiler_params=_cparams(1, VMEM_LIMIT),
        name="in_proj_odd",
    )(x, w)


def _s5_body(u_ref, h0r_ref, h0i_ref, ar_ref, ai_ref, asr_ref, asi_ref, bbig_ref, cbig_ref, d_ref, wglu_ref,
             bglu_ref, o_ref, hlr_ref, hli_ref, hs_ref, st_ref, *, first_row, last_seg):
    t = pl.program_id(1)
    rt = u_ref.shape[1]
    seg = rt // SUBLANE

    @pl.when(t == 0)
    def _():
        st_ref[0:1, :] = h0r_ref[0]
        st_ref[1:2, :] = h0i_ref[0]

    u = u_ref[0]
    if first_row:
        rows = t * rt + lax.broadcasted_iota(jnp.int32, (rt, 1), 0)
        u = jnp.where(rows >= first_row, u, 0.0)
    i0 = lax.broadcasted_iota(jnp.int32, (rt, rt), 0)
    i1 = lax.broadcasted_iota(jnp.int32, (rt, rt), 1)
    regroup = (i1 == (i0 & (SUBLANE - 1)) * seg + (i0 >> 3)).astype(BF16)
    restore = (i0 == (i1 & (SUBLANE - 1)) * seg + (i1 >> 3)).astype(BF16)
    ub = jnp.dot(regroup, u.astype(BF16), preferred_element_type=F32).astype(BF16)

    half_w, half_s = S5_WIDTH // 2, S5_HALF // 2
    for kb in range(2):
        for part in range(2):
            c0 = part * S5_HALF + kb * half_s
            hs_ref[:, c0:c0 + half_s] = jnp.dot(
                ub[:, kb * half_w:(kb + 1) * half_w], bbig_ref[kb * half_w:(kb + 1) * half_w, c0:c0 + half_s],
                preferred_element_type=F32)

    ar = jnp.broadcast_to(ar_ref[...], (SUBLANE, S5_HALF))
    ai = jnp.broadcast_to(ai_ref[...], (SUBLANE, S5_HALF))

    def step(k, carry, store):
        hr, hi = carry
        base = pl.multiple_of(k * SUBLANE, SUBLANE)
        blk = hs_ref[pl.ds(base, SUBLANE), :]
        nr = ar * hr - ai * hi + blk[:, :S5_HALF]
        ni = ar * hi + ai * hr + blk[:, S5_HALF:]
        if store:
            hs_ref[pl.ds(base, SUBLANE), :] = jnp.concatenate([nr, ni], axis=1)
        return nr, ni

    zero = jnp.zeros((SUBLANE, S5_HALF), F32)
    er, ei = lax.fori_loop(0, seg, functools.partial(step, store=False), (zero, zero))
    asr, asi = asr_ref[...], asi_ref[...]
    sr, si = st_ref[0:1, :], st_ref[1:2, :]
    start_r, start_i = [], []
    for j in range(SUBLANE):
        start_r.append(sr)
        start_i.append(si)
        sr, si = asr * sr - asi * si + er[j:j + 1], asr * si + asi * sr + ei[j:j + 1]
    st_ref[0:1, :] = sr
    st_ref[1:2, :] = si
    lax.fori_loop(0, seg, functools.partial(step, store=True),
                  (jnp.concatenate(start_r, axis=0), jnp.concatenate(start_i, axis=0)))

    hb = hs_ref[...].astype(BF16)
    ys = []
    for kb in range(2):
        re_rows = slice(kb * half_s, (kb + 1) * half_s)
        im_rows = slice(S5_HALF + kb * half_s, S5_HALF + (kb + 1) * half_s)
        cols = slice(kb * half_w, (kb + 1) * half_w)
        ys.append(jnp.dot(hb[:, re_rows], cbig_ref[re_rows, cols], preferred_element_type=F32)
                  + jnp.dot(hb[:, im_rows], cbig_ref[im_rows, cols], preferred_element_type=F32))
    y = jnp.concatenate(ys, axis=1)
    y_hi = y.astype(BF16)
    y_lo = (y - y_hi.astype(F32)).astype(BF16)
    y = (jnp.dot(restore, y_hi, preferred_element_type=F32) + jnp.dot(restore, y_lo, preferred_element_type=F32)
         + d_ref[...] * u)
    g = jax.nn.gelu(y)
    gate = jnp.dot(g.astype(BF16), wglu_ref[...], preferred_element_type=F32) + bglu_ref[...]
    o_ref[0] = (g * jax.nn.sigmoid(gate)).astype(o_ref.dtype)

    @pl.when(t == pl.num_programs(1) - 1)
    def _():
        hlr_ref[0] = sr if last_seg == SUBLANE else start_r[last_seg]
        hli_ref[0] = si if last_seg == SUBLANE else start_i[last_seg]


def _s5(h, h0r, h0i, p, rt, first_row, n_real):
    bsz, seq, _ = h.shape
    seg = rt // SUBLANE
    real_in_last = n_real - (seq - rt)
    assert 0 < real_in_last <= rt and real_in_last % seg == 0
    as_re = (jnp.exp(p["lam_dt_re"] * seg) * jnp.cos(p["lam_dt_im"] * seg)).reshape(1, S5_HALF)
    as_im = (jnp.exp(p["lam_dt_re"] * seg) * jnp.sin(p["lam_dt_im"] * seg)).reshape(1, S5_HALF)
    state_spec = pl.BlockSpec((1, 1, S5_HALF), lambda b, t: (b, 0, 0))
    return pl.pallas_call(
        functools.partial(_s5_body, first_row=first_row, last_seg=real_in_last // seg),
        grid=(bsz, seq // rt),
        in_specs=[
            pl.BlockSpec((1, rt, S5_WIDTH), lambda b, t: (b, t, 0)),
            state_spec, state_spec,
            _const_spec((1, S5_HALF)), _const_spec((1, S5_HALF)), _const_spec((1, S5_HALF)), _const_spec((1, S5_HALF)),
            _const_spec((S5_WIDTH, 2 * S5_HALF)), _const_spec((2 * S5_HALF, S5_WIDTH)),
            _const_spec((1, S5_WIDTH)), _const_spec((S5_WIDTH, S5_WIDTH)), _const_spec((1, S5_WIDTH)),
        ],
        out_specs=[pl.BlockSpec((1, rt, S5_WIDTH), lambda b, t: (b, t, 0)), state_spec, state_spec],
        out_shape=[jax.ShapeDtypeStruct((bsz, seq, S5_WIDTH), BF16),
                   jax.ShapeDtypeStruct((bsz, 1, S5_HALF), F32),
                   jax.ShapeDtypeStruct((bsz, 1, S5_HALF), F32)],
        scratch_shapes=[pltpu.VMEM((rt, 2 * S5_HALF), F32), pltpu.VMEM((2, S5_HALF), F32)],
        compiler_params=_cparams(2, VMEM_LIMIT),
        name="s5",
    )(h, h0r, h0i, p["a_re"], p["a_im"], as_re, as_im, p["bbig"], p["cbig"], p["d"], p["w_glu"], p["b_glu"])


def _rope_lanes(x, c, s1, s2, half):
    n = x.shape[-1]
    return x * c + pltpu.roll(x, n - half, 1) * s1 + pltpu.roll(x, half, 1) * s2


def _mla_rows_body(h_ref, qn_ref, kn_ref, wq_ref, *refs, attn_layout, scale):
    q_tabs, (ck_ref, s1k_ref, s2k_ref, q_ref, ckv_ref, kpe_ref) = refs[:-6], refs[-6:]
    h = h_ref[0]
    q_lat = h[:, :MLA_Q_RANK]
    q_lat = (q_lat * lax.rsqrt(jnp.mean(q_lat * q_lat, axis=-1, keepdims=True) + EPS) * qn_ref[...]).astype(BF16)
    half = MLA_ROPE // 2
    if attn_layout:
        cos_t, sin_t = q_tabs[0][...], q_tabs[1][...]
        q_t = _dot_nt(wq_ref[...], q_lat)
        for hd in range(MLA_HEADS):
            r0 = hd * MLA_QK_PAD
            x1 = q_t[r0 + MLA_NOPE:r0 + MLA_NOPE + half, :]
            x2 = q_t[r0 + MLA_NOPE + half:r0 + MLA_NOPE + 2 * half, :]
            q_ref[0, r0:r0 + MLA_QK_PAD, :] = jnp.concatenate(
                [q_t[r0:r0 + MLA_NOPE, :] * scale, x1 * cos_t - x2 * sin_t, x1 * sin_t + x2 * cos_t,
                 q_t[r0 + MLA_NOPE + 2 * half:r0 + MLA_QK_PAD, :]], axis=0).astype(BF16)
    else:
        cq, s1q, s2q = (t[...] for t in q_tabs)
        q = jnp.dot(q_lat, wq_ref[...], preferred_element_type=F32)
        for hd in range(MLA_HEADS):
            sl = slice(hd * MLA_QK_PAD, (hd + 1) * MLA_QK_PAD)
            q_ref[0, :, sl] = _rope_lanes(q[:, sl], cq, s1q, s2q, half).astype(BF16)
    c_kv = h[:, MLA_Q_RANK:MLA_Q_RANK + MLA_KV_RANK]
    ckv_ref[0] = c_kv * lax.rsqrt(jnp.mean(c_kv * c_kv, axis=-1, keepdims=True) + EPS) * kn_ref[...]
    k_pe = h[:, MLA_Q_RANK + MLA_KV_RANK:]
    kpe_ref[0] = _rope_lanes(k_pe, ck_ref[...], s1k_ref[...], s2k_ref[...], MLA_ROPE // 2)[:, :MLA_ROPE]


def _mla_rows(h, p, pos, tm, attn_layout):
    bsz, seq, _ = h.shape
    tab_spec = pl.BlockSpec((tm, LANE), lambda b, t: (t, 0))
    n_q = MLA_HEADS * MLA_QK_PAD
    half = MLA_ROPE // 2
    scale = (MLA_NOPE + MLA_ROPE) ** -0.5 * LOG2E
    k_tabs = _rope_tables(pos, half, LANE, LANE, 0)
    if attn_layout:
        q_shape, q_spec = (bsz, n_q, seq), pl.BlockSpec((1, n_q, tm), lambda b, t: (b, 0, t))
        ang = (ROPE_BASE ** (-jnp.arange(half, dtype=F32) / half))[:, None] * pos.astype(F32)[None, :]
        q_tabs = (jnp.cos(ang) * scale, jnp.sin(ang) * scale)
        q_tab_specs = [pl.BlockSpec((half, tm), lambda b, t: (0, t))] * 2
        wq = p["wq"].T
    else:
        q_shape, q_spec = (bsz, seq, n_q), pl.BlockSpec((1, tm, n_q), lambda b, t: (b, t, 0))
        q_tabs = _rope_tables(pos, half, MLA_QK_PAD, MLA_QK_PAD, MLA_NOPE, scale=scale)
        q_tab_specs = [tab_spec] * 3
        wq = p["wq"]
    return pl.pallas_call(
        functools.partial(_mla_rows_body, attn_layout=attn_layout, scale=scale),
        grid=(bsz, seq // tm),
        in_specs=[
            pl.BlockSpec((1, tm, EVEN_IN_PAD - S5_WIDTH), lambda b, t: (b, t, 1)),
            _const_spec((1, MLA_Q_RANK)), _const_spec((1, MLA_KV_RANK)),
            _const_spec(wq.shape),
        ] + q_tab_specs + [tab_spec] * 3,
        out_specs=[
            q_spec,
            pl.BlockSpec((1, tm, MLA_KV_RANK), lambda b, t: (b, t, 0)),
            pl.BlockSpec((1, tm, MLA_ROPE), lambda b, t: (b, t, 0)),
        ],
        out_shape=[jax.ShapeDtypeStruct(q_shape, BF16),
                   jax.ShapeDtypeStruct((bsz, seq, MLA_KV_RANK), F32),
                   jax.ShapeDtypeStruct((bsz, seq, MLA_ROPE), F32)],
        compiler_params=_cparams(2, VMEM_LIMIT),
        name="mla_rows",
    )(h, p["q_norm"], p["kv_norm"], wq, *q_tabs, *k_tabs)


def _mla_kv_body(ckv_ref, kpe_ref, wkn_ref, wkp_ref, wv_ref, k_ref, v_ref):
    ckv = ckv_ref[0].astype(BF16)
    k = jnp.dot(ckv, wkn_ref[...], preferred_element_type=F32)
    k = k + jnp.dot(kpe_ref[0].astype(BF16), wkp_ref[...], preferred_element_type=F32)
    k_ref[0] = k.astype(BF16)
    v_t = jnp.dot(ckv, wv_ref[...], preferred_element_type=F32).T
    for hd in range(MLA_HEADS):
        v_ref[0, hd, 0] = v_t[hd * MLA_V:(hd + 1) * MLA_V, :].astype(BF16)


def _mla_kv(ckv, kpe, p):
    bsz, seq, _ = ckv.shape
    n_t = seq // TILE
    return pl.pallas_call(
        _mla_kv_body,
        grid=(bsz, n_t),
        in_specs=[
            pl.BlockSpec((1, TILE, MLA_KV_RANK), lambda b, t: (b, t, 0)),
            pl.BlockSpec((1, TILE, MLA_ROPE), lambda b, t: (b, t, 0)),
            _const_spec(p["wkn"].shape), _const_spec(p["wkp"].shape), _const_spec(p["wv"].shape),
        ],
        out_specs=[pl.BlockSpec((1, TILE, MLA_HEADS * MLA_QK_PAD), lambda b, t: (b, t, 0)),
                   pl.BlockSpec((1, MLA_HEADS, 1, MLA_V, TILE), lambda b, t: (b, 0, t, 0, 0))],
        out_shape=[jax.ShapeDtypeStruct((bsz, seq, MLA_HEADS * MLA_QK_PAD), BF16),
                   jax.ShapeDtypeStruct((bsz, MLA_HEADS, n_t, MLA_V, TILE), BF16)],
        compiler_params=_cparams(2, VMEM_LIMIT),
        name="mla_kv",
    )(ckv, kpe, p["wkn"], p["wkp"], p["wv"])


def _flash_body(*refs, heads, dq, dv, causal):
    if causal:
        qt_ref, k_ref, vt_ref, fq_ref, fk_ref, o_ref = refs
    else:
        qt_ref, k_ref, vt_ref, o_ref = refs
    tq = qt_ref.shape[2]
    n_tiles, tk = vt_ref.shape[2], vt_ref.shape[4]
    r0 = pl.program_id(1) * tq
    j_last = jnp.minimum((r0 + tq - 1) // tk, n_tiles - 1)
    q_row = r0 + lax.broadcasted_iota(jnp.int32, (1, tq), 1)
    last = q_row if causal else ((q_row >> CHUNK_SHIFT) << CHUNK_SHIFT) + (CHUNK - 1)
    first_last = r0 if causal else ((r0 >> CHUNK_SHIFT) << CHUNK_SHIFT) + (CHUNK - 1)
    n_open = jnp.clip((first_last + 1) // tk, 1, j_last + 1)
    q_t = [qt_ref[0, hd * dq:(hd + 1) * dq, :] for hd in range(heads)]

    def tile(j, carry, masked):
        start = pl.multiple_of(j * tk, tk)
        if masked:
            k_row = start + lax.broadcasted_iota(jnp.int32, (tk, 1), 0)
            ok = (k_row <= last) & (k_row >= PAD)
        scores = [jnp.dot(k_ref[0, pl.ds(start, tk), hd * dq:(hd + 1) * dq], q_t[hd],
                          preferred_element_type=F32) for hd in range(heads)]
        probs = []
        for hd in range(heads):
            m, l, _ = carry[hd]
            s = scores[hd]
            if causal:
                s = s + (fq_ref[0, hd:hd + 1, :] - fk_ref[0, pl.ds(start, tk), hd:hd + 1])
            if masked:
                s = jnp.where(ok, s, NEG)
            m_new = jnp.maximum(m, jnp.max(s, axis=0, keepdims=True))
            a = jnp.exp2(m - m_new)
            p = jnp.exp2(s - m_new)
            probs.append((m_new, a * l + jnp.sum(p, axis=0, keepdims=True), a, p.astype(BF16)))
        new = []
        for hd in range(heads):
            m_new, l, a, p = probs[hd]
            acc = a * carry[hd][2] + jnp.dot(vt_ref[0, hd, j], p, preferred_element_type=F32)
            new.append((m_new, l, acc))
        return tuple(new)

    init = tuple((jnp.full((1, tq), NEG, F32), jnp.zeros((1, tq), F32), jnp.zeros((dv, tq), F32))
                 for _ in range(heads))
    carry = tile(0, init, True)
    carry = lax.fori_loop(1, n_open, functools.partial(tile, masked=False), carry)
    carry = lax.fori_loop(n_open, j_last + 1, functools.partial(tile, masked=True), carry)
    outs = [carry[hd][2] / carry[hd][1] for hd in range(heads)]
    per = LANE // dv
    for g in range(heads // per):
        o_ref[0, :, g * LANE:(g + 1) * LANE] = jnp.concatenate(
            outs[g * per:(g + 1) * per], axis=0).T.astype(o_ref.dtype)


def _flash(q_t, k, v_t, fq_t, fk, *, heads, dq, dv, causal):
    bsz, _, n_q = q_t.shape
    n_keys = k.shape[1]
    in_specs = [
        pl.BlockSpec((1, heads * dq, TILE), lambda b, i: (b, 0, i)),
        pl.BlockSpec((1, n_keys, heads * dq), lambda b, i: (b, 0, 0)),
        pl.BlockSpec((1,) + v_t.shape[1:], lambda b, i: (b, 0, 0, 0, 0)),
    ]
    args = [q_t, k, v_t]
    if causal:
        in_specs += [pl.BlockSpec((1, heads, TILE), lambda b, i: (b, 0, i)),
                     pl.BlockSpec((1, n_keys, heads), lambda b, i: (b, 0, 0))]
        args += [fq_t, fk]
    return pl.pallas_call(
        functools.partial(_flash_body, heads=heads, dq=dq, dv=dv, causal=causal),
        grid=(bsz, n_q // TILE),
        in_specs=in_specs,
        out_specs=pl.BlockSpec((1, TILE, heads * dv), lambda b, i: (b, i, 0)),
        out_shape=jax.ShapeDtypeStruct((bsz, n_q, heads * dv), BF16),
        compiler_params=_cparams(2, VMEM_LIMIT),
        name="flash_causal" if causal else "flash_chunk",
    )(*args)


def _online_softmax(s, m_ref, l_ref, idx):
    m_old = m_ref[idx]
    m_new = jnp.maximum(m_old, jnp.max(s, axis=-1, keepdims=True))
    a = jnp.exp2(m_old - m_new)
    p = jnp.exp2(s - m_new)
    m_ref[idx] = m_new
    l_ref[idx] = a * l_ref[idx] + jnp.sum(p, axis=-1, keepdims=True)
    return a, p


def _dot_nt(a, b):
    return lax.dot_general(a, b, (((1,), (1,)), ((), ())), preferred_element_type=F32)


def _fox_decode_body(q_ref, fq_ref, kc_ref, vc_ref, fkc_ref, kn_ref, vn_ref, fkn_ref, o_ref, m_ref, l_ref, acc_ref):
    t = pl.program_id(1)
    n_q = q_ref.shape[1]
    cols = [slice(hd * FOX_DIM, (hd + 1) * FOX_DIM) for hd in range(FOX_HEADS)]

    @pl.when(t == 0)
    def _():
        m_ref[...] = jnp.full(m_ref.shape, NEG, F32)
        l_ref[...] = jnp.zeros(l_ref.shape, F32)
        acc_ref[...] = jnp.zeros(acc_ref.shape, F32)

    def attend(scores, fk_rows, values, ok):
        probs = []
        for hd in range(FOX_HEADS):
            s = scores[hd] + (fq_ref[0, :, hd:hd + 1] - fk_rows[hd:hd + 1, :])
            if ok is not None:
                s = jnp.where(ok, s, NEG)
            probs.append(_online_softmax(s, m_ref, l_ref, hd))
        for hd in range(FOX_HEADS):
            a, p = probs[hd]
            acc_ref[hd] = a * acc_ref[hd] + values(hd, p.astype(BF16))

    attend([jnp.dot(q_ref[0, :, cols[hd]], kc_ref[0, 0, hd].astype(BF16), preferred_element_type=F32)
            for hd in range(FOX_HEADS)],
           fkc_ref[0], lambda hd, p: _dot_nt(p, vc_ref[0, 0, hd].astype(BF16)), None)

    @pl.when(t == pl.num_programs(1) - 1)
    def _():
        causal = (lax.broadcasted_iota(jnp.int32, (n_q, n_q), 1) <= lax.broadcasted_iota(jnp.int32, (n_q, n_q), 0))
        attend([_dot_nt(q_ref[0, :, sl], kn_ref[0, :, sl]) for sl in cols], fkn_ref[0],
               lambda hd, p: jnp.dot(p, vn_ref[0, :, cols[hd]], preferred_element_type=F32), causal)
        for hd in range(FOX_HEADS):
            o_ref[0, :, cols[hd]] = (acc_ref[hd] / l_ref[hd]).astype(o_ref.dtype)


def _fox_decode(q, k_new, v_new, cache_k, cache_v, layer, fcum, tk):
    bsz, n_q, width = q.shape
    n_past = cache_k.shape[2]
    rows_minor = lambda c: c.transpose(0, 1, 3, 4, 2)
    fk_t = fcum.transpose(0, 2, 1)
    row_spec = pl.BlockSpec((1, n_q, width), lambda b, t: (b, 0, 0))
    cache_spec = pl.BlockSpec((1, 1, FOX_HEADS, FOX_DIM, tk), lambda b, t: (layer, b, 0, 0, t))
    return pl.pallas_call(
        _fox_decode_body,
        grid=(bsz, n_past // tk),
        in_specs=[row_spec, pl.BlockSpec((1, n_q, FOX_HEADS), lambda b, t: (b, 0, 0)),
                  cache_spec, cache_spec, pl.BlockSpec((1, FOX_HEADS, tk), lambda b, t: (b, 0, t)),
                  row_spec, row_spec, pl.BlockSpec((1, FOX_HEADS, n_q), lambda b, t: (b, 0, 0))],
        out_specs=row_spec,
        out_shape=jax.ShapeDtypeStruct((bsz, n_q, width), BF16),
        scratch_shapes=[pltpu.VMEM((FOX_HEADS, n_q, 1), F32), pltpu.VMEM((FOX_HEADS, n_q, 1), F32),
                        pltpu.VMEM((FOX_HEADS, n_q, FOX_DIM), F32)],
        compiler_params=_cparams(2, VMEM_LIMIT),
        name="fox_decode",
    )(q, fcum[:, n_past:n_past + n_q], rows_minor(cache_k), rows_minor(cache_v), fk_t[:, :, :n_past],
      k_new, v_new, fk_t[:, :, n_past:n_past + n_q])


def _mla_decode_body(q_ref, wkn_ref, wv_ref, cc_ref, pc_ref, cn_ref, pn_ref, o_ref, qa_ref, qr_ref,
                     m_ref, l_ref, acc_ref):
    t = pl.program_id(1)
    n_q = q_ref.shape[1]

    @pl.when(t == 0)
    def _():
        for hd in range(MLA_HEADS):
            c0 = hd * MLA_QK_PAD
            rows = slice(hd * n_q, (hd + 1) * n_q)
            qa_ref[rows, :] = _dot_nt(q_ref[0, :, c0:c0 + MLA_NOPE],
                                      wkn_ref[:, c0:c0 + MLA_NOPE]).astype(BF16)
            qr_ref[rows, :] = q_ref[0, :, c0 + MLA_NOPE:c0 + MLA_NOPE + MLA_ROPE]
        m_ref[...] = jnp.full(m_ref.shape, NEG, F32)
        l_ref[...] = jnp.zeros(l_ref.shape, F32)
        acc_ref[...] = jnp.zeros(acc_ref.shape, F32)

    def attend(ckv, rope_scores):
        a, p = _online_softmax(_dot_nt(qa_ref[...], ckv) + rope_scores, m_ref, l_ref, 0)
        acc_ref[0] = a * acc_ref[0] + jnp.dot(p.astype(BF16), ckv, preferred_element_type=F32)

    attend(cc_ref[0, 0].astype(BF16),
           jnp.dot(qr_ref[...], pc_ref[0, 0].astype(BF16), preferred_element_type=F32))

    @pl.when(t == pl.num_programs(1) - 1)
    def _():
        attend(cn_ref[0].astype(BF16), _dot_nt(qr_ref[...], pn_ref[0].astype(BF16)))
        lat = (acc_ref[0] / l_ref[0]).astype(BF16)
        for hd in range(MLA_HEADS):
            cols = slice(hd * MLA_V, (hd + 1) * MLA_V)
            o_ref[0, :, cols] = jnp.dot(lat[hd * n_q:(hd + 1) * n_q, :], wv_ref[:, cols],
                                        preferred_element_type=F32).astype(o_ref.dtype)


def _mla_decode(q, ckv_new, kpe_new, cache_ckv, cache_kpe, layer, p, tk):
    bsz, n_q, _ = q.shape
    n_past = cache_ckv.shape[2]
    stacked = MLA_HEADS * n_q
    return pl.pallas_call(
        _mla_decode_body,
        grid=(bsz, n_past // tk),
        in_specs=[pl.BlockSpec((1, n_q, MLA_HEADS * MLA_QK_PAD), lambda b, t: (b, 0, 0)),
                  _const_spec(p["wkn"].shape), _const_spec(p["wv"].shape),
                  pl.BlockSpec((1, 1, tk, MLA_KV_RANK), lambda b, t: (layer, b, t, 0)),
                  pl.BlockSpec((1, 1, MLA_ROPE, tk), lambda b, t: (layer, b, 0, t)),
                  pl.BlockSpec((1, n_q, MLA_KV_RANK), lambda b, t: (b, 0, 0)),
                  pl.BlockSpec((1, n_q, MLA_ROPE), lambda b, t: (b, 0, 0))],
        out_specs=pl.BlockSpec((1, n_q, MLA_HEADS * MLA_V), lambda b, t: (b, 0, 0)),
        out_shape=jax.ShapeDtypeStruct((bsz, n_q, MLA_HEADS * MLA_V), BF16),
        scratch_shapes=[pltpu.VMEM((stacked, MLA_KV_RANK), BF16), pltpu.VMEM((stacked, MLA_ROPE), BF16),
                        pltpu.VMEM((1, stacked, 1), F32), pltpu.VMEM((1, stacked, 1), F32),
                        pltpu.VMEM((1, stacked, MLA_KV_RANK), F32)],
        compiler_params=_cparams(2, VMEM_LIMIT),
        name="mla_decode",
    )(q, p["wkn"], p["wv"], cache_ckv, cache_kpe.transpose(0, 1, 3, 2), ckv_new, kpe_new)


def _gate_body(x_ref, b_ref, lf_ref, fc_ref, *, new_start, first_row):
    n_rows = x_ref.shape[1]
    tri = (lax.broadcasted_iota(jnp.int32, (TILE, TILE), 0)
           >= lax.broadcasted_iota(jnp.int32, (TILE, TILE), 1)).astype(F32)
    carry = jnp.zeros((1, x_ref.shape[2]), F32)
    for i in range(n_rows // TILE):
        sl = slice(i * TILE, (i + 1) * TILE)
        x = x_ref[0, sl, :]
        rows = i * TILE + lax.broadcasted_iota(jnp.int32, (TILE, 1), 0)
        z = x + b_ref[...]
        lf = jnp.where(rows >= new_start, jnp.minimum(z, 0.0) - jnp.log1p(jnp.exp(-jnp.abs(z))), x)
        if first_row:
            lf = jnp.where(rows >= first_row, lf, 0.0)
        lf_ref[0, sl, :] = lf
        cs = jnp.dot(tri, lf, preferred_element_type=F32, precision=lax.Precision.HIGHEST) + carry
        fc_ref[0, sl, :] = cs
        carry = cs[TILE - 1:TILE, :]


def _gate(x, b_f, new_start, first_row):
    bsz, n_rows, heads = x.shape
    spec = pl.BlockSpec((1, n_rows, heads), lambda b: (b, 0, 0))
    return pl.pallas_call(
        functools.partial(_gate_body, new_start=new_start, first_row=first_row),
        grid=(bsz,),
        in_specs=[spec, _const_spec((1, heads))],
        out_specs=[spec, spec],
        out_shape=[jax.ShapeDtypeStruct(x.shape, F32)] * 2,
        compiler_params=_cparams(1),
        name="fox_gate",
    )(x, b_f)


RET_LOG_GAMMA = tuple(math.log(1.0 - 2.0 ** (-5.0 - h)) for h in range(RET_HEADS))


def _ret_body(rq_ref, rk_ref, rv_ref, rg_ref, s0_ref, dec_ref, cq_ref, s1q_ref, s2q_ref, ck_ref, s1k_ref, s2k_ref,
              o_ref, sl_ref, st_ref, *, n_tail):
    c = pl.program_id(1)
    ct = rq_ref.shape[1]

    @pl.when(c == 0)
    def _():
        st_ref[...] = s0_ref[0]

    q = _rope_lanes(rq_ref[0], cq_ref[...], s1q_ref[...], s2q_ref[...], RET_DK // 2)
    k = _rope_lanes(rk_ref[0], ck_ref[...], s1k_ref[...], s2k_ref[...], RET_DK // 2)
    v = rv_ref[0].astype(BF16)
    g = rg_ref[0]
    j = lax.broadcasted_iota(jnp.int32, (ct, 1), 0).astype(F32)
    for hd in range(RET_HEADS):
        lg = RET_LOG_GAMMA[hd]
        q_h = q[:, hd * RET_DK:(hd + 1) * RET_DK]
        k_h = k[:, hd * RET_DK:(hd + 1) * RET_DK]
        v_h = v[:, hd * RET_DV:(hd + 1) * RET_DV]
        scores = lax.dot_general(q_h.astype(BF16), k_h.astype(BF16), (((1,), (1,)), ((), ())),
                                 preferred_element_type=F32) * dec_ref[hd]
        s_h = st_ref[hd]
        out = jnp.dot(scores.astype(BF16), v_h, preferred_element_type=F32)
        out = out + jnp.dot((q_h * jnp.exp(lg * (j + 1.0))).astype(BF16), s_h.astype(BF16),
                            preferred_element_type=F32)
        k_dec = (k_h * jnp.exp(lg * (ct - 1.0 - j))).astype(BF16)
        st_ref[hd] = math.exp(lg * ct) * s_h + lax.dot_general(
            k_dec, v_h, (((0,), (0,)), ((), ())), preferred_element_type=F32)
        mu = jnp.mean(out, axis=-1, keepdims=True)
        oc = out - mu
        var = jnp.mean(oc * oc, axis=-1, keepdims=True)
        g_h = g[:, hd * RET_DV:(hd + 1) * RET_DV]
        o_ref[0, :, hd * RET_DV:(hd + 1) * RET_DV] = (
            g_h * jax.nn.sigmoid(g_h) * (oc * lax.rsqrt(var + EPS))).astype(o_ref.dtype)

    @pl.when(c == pl.num_programs(1) - 1)
    def _():
        for hd in range(RET_HEADS):
            sl_ref[0, hd] = st_ref[hd] * math.exp(-RET_LOG_GAMMA[hd] * n_tail)


def _retention(h, s0, tabs, ct, n_tail):
    bsz, seq, _ = h.shape
    tab_spec = pl.BlockSpec((ct, RET_QK), lambda b, c: (c, 0))
    st_spec = pl.BlockSpec((1, RET_HEADS, RET_DK, RET_DV), lambda b, c: (b, 0, 0, 0))
    rq_blk = (2 * FOX_W) // RET_QK
    rv_blk = (2 * FOX_W + 2 * RET_QK) // RET_VW
    diff = jnp.arange(ct, dtype=F32)[:, None] - jnp.arange(ct, dtype=F32)[None, :]
    decay = jnp.where(diff >= 0.0, jnp.exp(jnp.asarray(RET_LOG_GAMMA, F32)[:, None, None] * jnp.maximum(diff, 0.0)),
                      0.0)
    return pl.pallas_call(
        functools.partial(_ret_body, n_tail=n_tail),
        grid=(bsz, seq // ct),
        in_specs=[
            pl.BlockSpec((1, ct, RET_QK), lambda b, c: (b, c, rq_blk)),
            pl.BlockSpec((1, ct, RET_QK), lambda b, c: (b, c, rq_blk + 1)),
            pl.BlockSpec((1, ct, RET_VW), lambda b, c: (b, c, rv_blk)),
            pl.BlockSpec((1, ct, RET_VW), lambda b, c: (b, c, rv_blk + 1)),
            st_spec, _const_spec((RET_HEADS, ct, ct)),
        ] + [tab_spec] * 6,
        out_specs=[pl.BlockSpec((1, ct, RET_VW), lambda b, c: (b, c, 0)), st_spec],
        out_shape=[jax.ShapeDtypeStruct((bsz, seq, RET_VW), BF16),
                   jax.ShapeDtypeStruct((bsz, RET_HEADS, RET_DK, RET_DV), F32)],
        scratch_shapes=[pltpu.VMEM((RET_HEADS, RET_DK, RET_DV), F32)],
        compiler_params=_cparams(2, VMEM_LIMIT),
        name="retention",
    )(h, h, h, h, s0, decay, *tabs)


def _rope_tables(pos, half, width, group, offset, scale=1.0, valid=None):
    inv = ROPE_BASE ** (-jnp.arange(half, dtype=F32) / half)
    ang = pos.astype(F32)[:, None] * inv[None, :]
    cos, sin = jnp.cos(ang), jnp.sin(ang)
    n = pos.shape[0]
    one, zero = jnp.ones((n, 1), F32), jnp.zeros((n, 1), F32)

    def lanes(first, second, other):
        grp = jnp.concatenate([jnp.broadcast_to(other, (n, offset)), first, second,
                               jnp.broadcast_to(other, (n, group - offset - 2 * half))], axis=1)
        return jnp.tile(grp, (1, width // group))

    tabs = (lanes(cos, cos, one), lanes(-sin, 0.0 * sin, zero), lanes(0.0 * sin, sin, zero))
    if valid is not None:
        tabs = tuple(jnp.where(valid[:, None], t, 0.0) for t in tabs)
    return tuple(t * scale for t in tabs)


def _even_params(e, w):
    p = {}
    p["w_in"] = jnp.pad(w["even_w_in"][e], ((0, 0), (0, EVEN_IN_PAD - EVEN_IN))).astype(BF16)
    p["w_out"] = w["even_w_out"][e].astype(BF16)
    lam_re, lam_im = w["s5_a_re"][e].astype(F32), w["s5_a_im"][e].astype(F32)
    dt = jnp.exp(w["s5_log_dt"][e].astype(F32))[:, None]
    p["lam_dt_re"], p["lam_dt_im"] = lam_re * dt, lam_im * dt
    mag = jnp.exp(lam_re * dt)
    abar_re, abar_im = mag * jnp.cos(lam_im * dt), mag * jnp.sin(lam_im * dt)
    den = lam_re * lam_re + lam_im * lam_im
    f_re = ((abar_re - 1.0) * lam_re + abar_im * lam_im) / den
    f_im = (abar_im * lam_re - (abar_re - 1.0) * lam_im) / den
    b_re, b_im = w["s5_b_re"][e].astype(F32), w["s5_b_im"][e].astype(F32)
    bb_re = f_re[..., None] * b_re - f_im[..., None] * b_im
    bb_im = f_re[..., None] * b_im + f_im[..., None] * b_re
    eye = jnp.eye(S5_GROUPS, dtype=F32)

    def in_blocks(x):
        return jnp.einsum("gnc,gh->gchn", x, eye).reshape(S5_WIDTH, S5_HALF)

    def out_blocks(x):
        return jnp.einsum("gcn,gh->gnhc", x, eye).reshape(S5_HALF, S5_WIDTH)

    p["bbig"] = jnp.concatenate([in_blocks(bb_re), in_blocks(bb_im)], axis=1).astype(BF16)
    p["cbig"] = jnp.concatenate([out_blocks(w["s5_c_re"][e].astype(F32)),
                                 out_blocks(-w["s5_c_im"][e].astype(F32))], axis=0).astype(BF16)
    p["a_re"] = abar_re.reshape(1, S5_HALF)
    p["a_im"] = abar_im.reshape(1, S5_HALF)
    p["d"] = w["s5_d"][e].astype(F32).reshape(1, S5_WIDTH)
    p["w_glu"] = w["s5_w_glu"][e].astype(BF16)
    p["b_glu"] = w["s5_b_glu"][e].astype(F32).reshape(1, S5_WIDTH)
    p["q_norm"] = w["mla_q_norm"][e].astype(F32).reshape(1, MLA_Q_RANK)
    p["kv_norm"] = w["mla_kv_norm"][e].astype(F32).reshape(1, MLA_KV_RANK)
    wq = w["mla_w_uq"][e].reshape(MLA_Q_RANK, MLA_HEADS, MLA_NOPE + MLA_ROPE)
    wq = jnp.pad(wq, ((0, 0), (0, 0), (0, MLA_QK_PAD - MLA_NOPE - MLA_ROPE)))
    p["wq"] = wq.reshape(MLA_Q_RANK, MLA_HEADS * MLA_QK_PAD).astype(BF16)
    wkv = w["mla_w_ukv"][e].reshape(MLA_KV_RANK, MLA_HEADS, MLA_NOPE + MLA_V)
    wkn = jnp.pad(wkv[:, :, :MLA_NOPE], ((0, 0), (0, 0), (0, MLA_QK_PAD - MLA_NOPE)))
    p["wkn"] = wkn.reshape(MLA_KV_RANK, MLA_HEADS * MLA_QK_PAD).astype(BF16)
    place = jnp.pad(jnp.eye(MLA_ROPE, dtype=F32), ((0, 0), (MLA_NOPE, MLA_QK_PAD - MLA_NOPE - MLA_ROPE)))
    p["wkp"] = jnp.tile(place, (1, MLA_HEADS)).astype(BF16)
    p["wv"] = wkv[:, :, MLA_NOPE:].reshape(MLA_KV_RANK, MLA_HEADS * MLA_V).astype(BF16)
    return p


def _odd_params(o, w):
    w_in = w["odd_w_in"][o]
    c_logit = 3 * FOX_W
    cols = jnp.concatenate([
        w_in[:, :FOX_W] * (FOX_DIM ** -0.5),
        w_in[:, FOX_W:c_logit],
        w_in[:, c_logit + FOX_HEADS:],
        w_in[:, c_logit:c_logit + FOX_HEADS],
    ], axis=1)
    p = {"w_in": jnp.pad(cols, ((0, 0), (0, ODD_IN_PAD - cols.shape[1]))).astype(BF16)}
    p["w_out"] = w["odd_w_out"][o].astype(BF16)
    p["b_f"] = w["fox_b_f"][o].astype(F32).reshape(1, FOX_HEADS)
    return p


def _cache_tile(n_past):
    for tk in (512, 256, 128):
        if n_past % tk == 0:
            return tk
    raise ValueError("cache length must be a multiple of 128")


def _trunk(x, pos, n_real, out_rows, past, w, ffn, evens, odds):
    bsz, seq, _ = x.shape
    rows = bsz * seq
    prompt = past is None
    first_row = PAD if prompt else 0
    if prompt:
        tm = 2 * TILE if rows % (2 * TILE) == 0 else TILE
        rt = TILE
    else:
        n_past = past["cache_fox_k"].shape[2]
        tk = _cache_tile(n_past)
        tm, rt = rows, seq
    idx = jnp.arange(seq, dtype=jnp.int32)
    valid = (idx >= first_row) & (idx < n_real)
    st = {n: [] for n in ("mla_ckv", "mla_kpe", "s5_re", "s5_im", "fox_k", "fox_v", "fox_logf", "ret")}
    rq_tabs = _rope_tables(pos, RET_DK // 2, RET_QK, RET_DK, 0)
    rk_tabs = _rope_tables(pos, RET_DK // 2, RET_QK, RET_DK, 0, scale=RET_DK ** -0.5, valid=valid)
    x2 = x.reshape(rows, D_MODEL)

    def ln(l, i):
        return w["ln_g"][l, i].reshape(1, D_MODEL), w["ln_b"][l, i].reshape(1, D_MODEL)

    for l in range(DEPTH):
        x2 = _ffn_ln(x2, *ffn[l][0], *ln(l, 0), tm)
        if l % 2 == 0:
            e = l // 2
            p = evens[e]
            h = _proj(x2, p["w_in"], tm).reshape(bsz, seq, EVEN_IN_PAD)
            if prompt:
                h0r = h0i = jnp.zeros((bsz, 1, S5_HALF), F32)
            else:
                h0r = past["state_s5_re"][e].astype(F32).reshape(bsz, 1, S5_HALF)
                h0i = past["state_s5_im"][e].astype(F32).reshape(bsz, 1, S5_HALF)
            s5_out, hlr, hli = _s5(h, h0r, h0i, p, rt, first_row, n_real)
            q, ckv, kpe = _mla_rows(h, p, pos, rt, attn_layout=prompt)
            if prompt:
                k_att, v_att = _mla_kv(ckv, kpe, p)
                mla_out = _flash(q, k_att, v_att, None, None, heads=MLA_HEADS, dq=MLA_QK_PAD, dv=MLA_V,
                                 causal=False)
            else:
                mla_out = _mla_decode(q, ckv, kpe, past["cache_mla_ckv"], past["cache_mla_kpe"], e, p, tk)
            mix = (s5_out.reshape(rows, S5_WIDTH), mla_out.reshape(rows, MLA_HEADS * MLA_V), p["w_out"])
            st["mla_ckv"].append(ckv)
            st["mla_kpe"].append(kpe)
            st["s5_re"].append(hlr.reshape(bsz, S5_GROUPS, S5_STATE))
            st["s5_im"].append(hli.reshape(bsz, S5_GROUPS, S5_STATE))
        else:
            o = l // 2
            p = odds[o]
            h, fq, fk16, fv = _proj_odd(x2, p["w_in"], bsz, TILE if prompt else tm, attn_layout=prompt)
            h = h.reshape(bsz, seq, ODD_H)
            fk16 = fk16.reshape(bsz, seq, FOX_W)
            f_logit = h[:, :, ODD_LOGIT_COL:ODD_LOGIT_COL + FOX_HEADS]
            if prompt:
                q_off = 0
                logf, fcum = _gate(f_logit, p["b_f"], 0, PAD)
                fcum = fcum * LOG2E
                fox_out = _flash(fq, fk16, fv, fcum.transpose(0, 2, 1), fcum, heads=FOX_HEADS, dq=FOX_DIM,
                                 dv=FOX_DIM, causal=True)
                s0 = jnp.zeros((bsz, RET_HEADS, RET_DK, RET_DV), F32)
            else:
                q_off = n_past
                gates = jnp.concatenate([past["cache_fox_logf"][o].astype(F32), f_logit], axis=1)
                gates = jnp.pad(gates, ((0, 0), (0, _round_up(n_past + seq, TILE) - n_past - seq), (0, 0)))
                logf, fcum = _gate(gates, p["b_f"], n_past, 0)
                fox_out = _fox_decode(fq.reshape(bsz, seq, FOX_W), fk16, fv.reshape(bsz, seq, FOX_W),
                                      past["cache_fox_k"], past["cache_fox_v"], o, fcum * LOG2E, tk)
                s0 = past["state_ret"][o].astype(F32)
            ret_out, s_last = _retention(h, s0, rq_tabs + rk_tabs, rt, seq - n_real)
            mix = (fox_out.reshape(rows, FOX_W), ret_out.reshape(rows, RET_VW), p["w_out"])
            st["fox_k"].append(h[:, :, :FOX_W].reshape(bsz, seq, FOX_HEADS, FOX_DIM))
            st["fox_v"].append(h[:, :, FOX_W:2 * FOX_W].reshape(bsz, seq, FOX_HEADS, FOX_DIM))
            st["fox_logf"].append(logf[:, q_off:q_off + seq])
            st["ret"].append(s_last)
        keep = (seq,) + out_rows if l == DEPTH - 1 and out_rows != (0, seq) else None
        x2 = _mix_ffn_ln(x2, *mix, ln(l, 1), ffn[l][1], ln(l, 2), tm, keep)
    return x2.reshape(bsz, out_rows[1], D_MODEL), {n: jnp.stack(a) for n, a in st.items()}


def kernel(x_prompt, x_sample, cache_mla_ckv, cache_mla_kpe, cache_fox_k, cache_fox_v, cache_fox_logf,
           state_s5_re, state_s5_im, state_ret, meta_tokens, ln_g, ln_b, ffn_w_gate, ffn_w_up, ffn_w_down,
           even_w_in, even_w_out, s5_a_re, s5_a_im, s5_b_re, s5_b_im, s5_c_re, s5_c_im, s5_d, s5_log_dt,
           s5_w_glu, s5_b_glu, mla_q_norm, mla_kv_norm, mla_w_uq, mla_w_ukv, odd_w_in, odd_w_out, fox_b_f):
    w = dict(ln_g=ln_g.astype(F32), ln_b=ln_b.astype(F32), even_w_in=even_w_in, even_w_out=even_w_out,
             s5_a_re=s5_a_re, s5_a_im=s5_a_im, s5_b_re=s5_b_re, s5_b_im=s5_b_im, s5_c_re=s5_c_re,
             s5_c_im=s5_c_im, s5_d=s5_d, s5_log_dt=s5_log_dt, s5_w_glu=s5_w_glu, s5_b_glu=s5_b_glu,
             mla_q_norm=mla_q_norm, mla_kv_norm=mla_kv_norm, mla_w_uq=mla_w_uq, mla_w_ukv=mla_w_ukv,
             odd_w_in=odd_w_in, odd_w_out=odd_w_out, fox_b_f=fox_b_f)
    past = dict(cache_mla_ckv=cache_mla_ckv, cache_mla_kpe=cache_mla_kpe, cache_fox_k=cache_fox_k,
                cache_fox_v=cache_fox_v, cache_fox_logf=cache_fox_logf, state_s5_re=state_s5_re,
                state_s5_im=state_s5_im, state_ret=state_ret)
    ffn = [[(ffn_w_gate[l, i].astype(BF16), ffn_w_up[l, i].astype(BF16), ffn_w_down[l, i].astype(BF16))
            for i in range(2)] for l in range(DEPTH)]
    evens = [_even_params(e, w) for e in range((DEPTH + 1) // 2)]
    odds = [_odd_params(o, w) for o in range(DEPTH // 2)]

    bsz, seq, _ = x_prompt.shape
    n_real = PAD + N_META + seq
    n_rows = _round_up(n_real, TILE)
    meta = jnp.broadcast_to(meta_tokens[None].astype(x_prompt.dtype), (bsz, N_META, D_MODEL))
    xp = jnp.concatenate([jnp.zeros((bsz, PAD, D_MODEL), x_prompt.dtype), meta, x_prompt,
                          jnp.zeros((bsz, n_rows - n_real, D_MODEL), x_prompt.dtype)], axis=1)
    pos_p = jnp.maximum(jnp.arange(n_rows, dtype=jnp.int32) - PAD, 0)
    y_p, st_p = _trunk(xp, pos_p, n_real, (PAD + N_META, seq), None, w, ffn, evens, odds)
    d_seq = x_sample.shape[1]
    pos_s = N_META + cache_fox_k.shape[2] + jnp.arange(d_seq, dtype=jnp.int32)
    y_s, st_s = _trunk(x_sample, pos_s, d_seq, (0, d_seq), past, w, ffn, evens, odds)

    def real(a):
        return a[:, :, PAD:n_real]

    return (y_p, y_s,
            real(st_p["mla_ckv"]), real(st_p["mla_kpe"]), real(st_p["fox_k"]), real(st_p["fox_v"]),
            real(st_p["fox_logf"]), st_p["s5_re"], st_p["s5_im"], st_p["ret"],
            st_s["mla_ckv"], st_s["mla_kpe"], st_s["fox_k"], st_s["fox_v"], st_s["fox_logf"],
            st_s["s5_re"], st_s["s5_im"], st_s["ret"])
```

```python
import functools
import math

import jax
import jax.numpy as jnp
from jax import lax
from jax.experimental import pallas as pl
from jax.experimental.pallas import tpu as pltpu

F32 = jnp.float32
BF16 = jnp.bfloat16

D_MODEL = 1024
DEPTH = 4
CHUNK = 64
CHUNK_SHIFT = 6
N_META = 16
S5_WIDTH = 512
S5_CH = 16
S5_GROUPS = S5_WIDTH // S5_CH
S5_STATE = 64
S5_HALF = S5_GROUPS * S5_STATE
MLA_HEADS = 8
MLA_Q_RANK = 256
MLA_KV_RANK = 128
MLA_NOPE = 64
MLA_ROPE = 32
MLA_V = 64
MLA_QK_PAD = 128
FOX_HEADS = 8
FOX_DIM = 64
FOX_W = FOX_HEADS * FOX_DIM
RET_HEADS = 4
RET_DK = 64
RET_DV = 128
RET_QK = RET_HEADS * RET_DK
RET_VW = RET_HEADS * RET_DV
D_FF = 2816
ROPE_BASE = 10000.0
ALPHA = (2.0 * DEPTH) ** 0.25
EPS = 1e-5
NEG = -1e30
LOG2E = math.log2(math.e)
EVEN_IN = S5_WIDTH + MLA_Q_RANK + MLA_KV_RANK + MLA_ROPE
EVEN_IN_PAD = 1024
ODD_IN_PAD = 3200
ODD_H = ODD_IN_PAD - FOX_W
ODD_LOGIT_COL = 2 * FOX_W + 2 * RET_QK + 2 * RET_VW

PAD = CHUNK - N_META
LANE = 128
SUBLANE = 8
TILE = 256
VMEM_LIMIT = 56 * 1024 * 1024


def _cparams(n_grid, vmem=None):
    return pltpu.CompilerParams(dimension_semantics=("arbitrary",) * n_grid, vmem_limit_bytes=vmem)


def _const_spec(shape):
    nd = len(shape)
    return pl.BlockSpec(shape, lambda *_: (0,) * nd)


def _round_up(n, m):
    return -(-n // m) * m


def _layer_norm_rows(z, g, b):
    mu = jnp.mean(z, axis=-1, keepdims=True)
    zc = z - mu
    var = jnp.mean(zc * zc, axis=-1, keepdims=True)
    return zc * lax.rsqrt(var + EPS) * g + b


def _ffn_rows(x, wg_ref, wu_ref, wd_ref, g_ref, b_ref, hid_ref):
    xb = x.astype(BF16)
    for c in range(D_FF // TILE):
        sl = slice(c * TILE, (c + 1) * TILE)
        hg = jnp.dot(xb, wg_ref[:, sl], preferred_element_type=F32)
        hu = jnp.dot(xb, wu_ref[:, sl], preferred_element_type=F32)
        hid_ref[:, sl] = (hg * jax.nn.sigmoid(hg) * hu).astype(BF16)
    y = jnp.dot(hid_ref[...], wd_ref[...], preferred_element_type=F32)
    return _layer_norm_rows(ALPHA * x + 0.5 * y, g_ref[...], b_ref[...])


def _ffn_body(x_ref, wg_ref, wu_ref, wd_ref, g_ref, b_ref, o_ref, hid_ref):
    o_ref[...] = _ffn_rows(x_ref[...], wg_ref, wu_ref, wd_ref, g_ref, b_ref, hid_ref)


def _ffn_weight_specs(which):
    once = dict(pipeline_mode=pl.Buffered(1))
    pick = lambda i: which + (0, 0)
    return [pl.BlockSpec((None, None, D_MODEL, D_FF), pick, **once),
            pl.BlockSpec((None, None, D_MODEL, D_FF), pick, **once),
            pl.BlockSpec((None, None, D_FF, D_MODEL), pick, **once),
            _const_spec((1, D_MODEL)), _const_spec((1, D_MODEL))]


def _ffn_ln(x, ffn, which, g, b, tm):
    rows = x.shape[0]
    return pl.pallas_call(
        _ffn_body,
        grid=(rows // tm,),
        in_specs=[pl.BlockSpec((tm, D_MODEL), lambda i: (i, 0))] + _ffn_weight_specs(which),
        out_specs=pl.BlockSpec((tm, D_MODEL), lambda i: (i, 0)),
        out_shape=jax.ShapeDtypeStruct((rows, D_MODEL), F32),
        scratch_shapes=[pltpu.VMEM((tm, D_FF), BF16)],
        compiler_params=_cparams(1, VMEM_LIMIT),
        name="ffn_ln",
    )(x, *ffn, g, b)


def _mix_ffn_body(x_ref, a1_ref, a2_ref, wo_ref, g1_ref, b1_ref, wg_ref, wu_ref, wd_ref, g2_ref, b2_ref,
                  o_ref, hid_ref):
    k1 = a1_ref.shape[1]
    y = jnp.dot(a1_ref[...], wo_ref[:k1, :], preferred_element_type=F32)
    y = y + jnp.dot(a2_ref[...], wo_ref[k1:, :], preferred_element_type=F32)
    x1 = _layer_norm_rows(ALPHA * x_ref[...] + y, g1_ref[...], b1_ref[...])
    o_ref[...] = _ffn_rows(x1, wg_ref, wu_ref, wd_ref, g2_ref, b2_ref, hid_ref)


def _mix_ffn_ln(x, a1, a2, w_out, ln1, ffn, which, ln2, tm, keep=None):
    rows = x.shape[0]
    if keep is None:
        n_steps = rows // tm
        row_spec = lambda n: pl.BlockSpec((tm, n), lambda i: (i, 0))
    else:
        seq, start, count = keep
        per = count // tm
        n_steps = (rows // seq) * per
        align = math.gcd(seq, start, tm)
        row_spec = lambda n: pl.BlockSpec(
            (pl.Element(tm), pl.Element(n)),
            lambda i: (pl.multiple_of((i // per) * seq + start + (i % per) * tm, align), 0))
    return pl.pallas_call(
        _mix_ffn_body,
        grid=(n_steps,),
        in_specs=[row_spec(D_MODEL), row_spec(a1.shape[1]), row_spec(a2.shape[1]),
                  pl.BlockSpec(w_out.shape, lambda i: (0, 0), pipeline_mode=pl.Buffered(1)),
                  _const_spec((1, D_MODEL)), _const_spec((1, D_MODEL))] + _ffn_weight_specs(which),
        out_specs=pl.BlockSpec((tm, D_MODEL), lambda i: (i, 0)),
        out_shape=jax.ShapeDtypeStruct((n_steps * tm, D_MODEL), F32),
        scratch_shapes=[pltpu.VMEM((tm, D_FF), BF16)],
        compiler_params=_cparams(1, VMEM_LIMIT),
        name="mix_ffn_ln",
    )(x, a1, a2, w_out, *ln1, *ffn, *ln2)


def _proj_body(x_ref, w_ref, o_ref):
    o_ref[...] = jnp.dot(x_ref[...].astype(BF16), w_ref[...], preferred_element_type=F32)


def _proj(x, w, tm):
    rows, n = x.shape[0], w.shape[1]
    return pl.pallas_call(
        _proj_body,
        grid=(rows // tm,),
        in_specs=[pl.BlockSpec((tm, D_MODEL), lambda i: (i, 0)), _const_spec(w.shape)],
        out_specs=pl.BlockSpec((tm, n), lambda i: (i, 0)),
        out_shape=jax.ShapeDtypeStruct((rows, n), F32),
        compiler_params=_cparams(1, VMEM_LIMIT),
        name="in_proj_even",
    )(x, w)


def _proj_odd_body(x_ref, w_ref, h_ref, q_ref, k_ref, v_ref, *, attn_layout):
    y = jnp.dot(x_ref[...].astype(BF16), w_ref[...], preferred_element_type=F32)
    h_ref[...] = y[:, FOX_W:]
    k_ref[...] = y[:, FOX_W:2 * FOX_W].astype(BF16)
    q = y[:, :FOX_W] * LOG2E
    if attn_layout:
        q_ref[0] = q.T.astype(BF16)
        v_t = y[:, 2 * FOX_W:3 * FOX_W].T
        for hd in range(FOX_HEADS):
            v_ref[0, hd, 0] = v_t[hd * FOX_DIM:(hd + 1) * FOX_DIM, :].astype(BF16)
    else:
        q_ref[...] = q.astype(BF16)
        v_ref[...] = y[:, 2 * FOX_W:3 * FOX_W].astype(BF16)


def _proj_odd(x, w, bsz, tm, attn_layout):
    rows = x.shape[0]
    seq = rows // bsz
    row_spec = lambda n: pl.BlockSpec((tm, n), lambda i: (i, 0))
    if attn_layout:
        assert tm == TILE and seq % TILE == 0
        n_t = seq // TILE
        q_shape = jax.ShapeDtypeStruct((bsz, FOX_W, seq), BF16)
        q_spec = pl.BlockSpec((1, FOX_W, TILE), lambda i: (i // n_t, 0, i % n_t))
        v_shape = jax.ShapeDtypeStruct((bsz, FOX_HEADS, n_t, FOX_DIM, TILE), BF16)
        v_spec = pl.BlockSpec((1, FOX_HEADS, 1, FOX_DIM, TILE), lambda i: (i // n_t, 0, i % n_t, 0, 0))
    else:
        q_shape = v_shape = jax.ShapeDtypeStruct((rows, FOX_W), BF16)
        q_spec = v_spec = row_spec(FOX_W)
    return pl.pallas_call(
        functools.partial(_proj_odd_body, attn_layout=attn_layout),
        grid=(rows // tm,),
        in_specs=[row_spec(D_MODEL), _const_spec(w.shape)],
        out_specs=[row_spec(ODD_H), q_spec, row_spec(FOX_W), v_spec],
        out_shape=[jax.ShapeDtypeStruct((rows, ODD_H), F32), q_shape,
                   jax.ShapeDtypeStruct((rows, FOX_W), BF16), v_shape],
        compiler_params=_cparams(1, VMEM_LIMIT),
        name="in_proj_odd",
    )(x, w)


def _s5_body(u_ref, h0r_ref, h0i_ref, ar_ref, ai_ref, asr_ref, asi_ref, bbig_ref, cbig_ref, d_ref, wglu_ref,
             bglu_ref, o_ref, hlr_ref, hli_ref, hs_ref, st_ref, *, first_row, last_seg):
    t = pl.program_id(1)
    rt = u_ref.shape[1]
    seg = rt // SUBLANE

    @pl.when(t == 0)
    def _():
        st_ref[0:1, :] = h0r_ref[0]
        st_ref[1:2, :] = h0i_ref[0]

    u = u_ref[0]
    if first_row:
        rows = t * rt + lax.broadcasted_iota(jnp.int32, (rt, 1), 0)
        u = jnp.where(rows >= first_row, u, 0.0)
    i0 = lax.broadcasted_iota(jnp.int32, (rt, rt), 0)
    i1 = lax.broadcasted_iota(jnp.int32, (rt, rt), 1)
    regroup = (i1 == (i0 & (SUBLANE - 1)) * seg + (i0 >> 3)).astype(BF16)
    restore = (i0 == (i1 & (SUBLANE - 1)) * seg + (i1 >> 3)).astype(BF16)
    ub = jnp.dot(regroup, u.astype(BF16), preferred_element_type=F32).astype(BF16)

    half_w, half_s = S5_WIDTH // 2, S5_HALF // 2
    for kb in range(2):
        for part in range(2):
            c0 = part * S5_HALF + kb * half_s
            hs_ref[:, c0:c0 + half_s] = jnp.dot(
                ub[:, kb * half_w:(kb + 1) * half_w], bbig_ref[kb * half_w:(kb + 1) * half_w, c0:c0 + half_s],
                preferred_element_type=F32)

    def scan(lanes, start, store):
        re_l = lanes
        im_l = slice(S5_HALF + lanes.start, S5_HALF + lanes.stop)
        ar = jnp.broadcast_to(ar_ref[:, lanes], (SUBLANE, lanes.stop - lanes.start))
        ai = jnp.broadcast_to(ai_ref[:, lanes], (SUBLANE, lanes.stop - lanes.start))

        def step(k, carry):
            hr, hi = carry
            base = pl.multiple_of(k * SUBLANE, SUBLANE)
            nr = ar * hr - ai * hi + hs_ref[pl.ds(base, SUBLANE), re_l]
            ni = ar * hi + ai * hr + hs_ref[pl.ds(base, SUBLANE), im_l]
            if store:
                hs_ref[pl.ds(base, SUBLANE), re_l] = nr
                hs_ref[pl.ds(base, SUBLANE), im_l] = ni
            return nr, ni

        return lax.fori_loop(0, seg, step, start)

    slabs = [slice(c * half_s, (c + 1) * half_s) for c in range(2)]
    zero = jnp.zeros((SUBLANE, half_s), F32)
    ends = [scan(lanes, (zero, zero), False) for lanes in slabs]
    er = jnp.concatenate([e[0] for e in ends], axis=1)
    ei = jnp.concatenate([e[1] for e in ends], axis=1)
    asr, asi = asr_ref[...], asi_ref[...]
    sr, si = st_ref[0:1, :], st_ref[1:2, :]
    start_r, start_i = [], []
    for j in range(SUBLANE):
        start_r.append(sr)
        start_i.append(si)
        sr, si = asr * sr - asi * si + er[j:j + 1], asr * si + asi * sr + ei[j:j + 1]
    st_ref[0:1, :] = sr
    st_ref[1:2, :] = si
    start_r8, start_i8 = jnp.concatenate(start_r, axis=0), jnp.concatenate(start_i, axis=0)
    for lanes in slabs:
        scan(lanes, (start_r8[:, lanes], start_i8[:, lanes]), True)

    hb = hs_ref[...].astype(BF16)
    ys = []
    for kb in range(2):
        re_rows = slice(kb * half_s, (kb + 1) * half_s)
        im_rows = slice(S5_HALF + kb * half_s, S5_HALF + (kb + 1) * half_s)
        cols = slice(kb * half_w, (kb + 1) * half_w)
        ys.append(jnp.dot(hb[:, re_rows], cbig_ref[re_rows, cols], preferred_element_type=F32)
                  + jnp.dot(hb[:, im_rows], cbig_ref[im_rows, cols], preferred_element_type=F32))
    y = jnp.concatenate(ys, axis=1)
    y_hi = y.astype(BF16)
    y_lo = (y - y_hi.astype(F32)).astype(BF16)
    y = (jnp.dot(restore, y_hi, preferred_element_type=F32) + jnp.dot(restore, y_lo, preferred_element_type=F32)
         + d_ref[...] * u)
    g = jax.nn.gelu(y)
    gate = jnp.dot(g.astype(BF16), wglu_ref[...], preferred_element_type=F32) + bglu_ref[...]
    o_ref[0] = (g * jax.nn.sigmoid(gate)).astype(o_ref.dtype)

    @pl.when(t == pl.num_programs(1) - 1)
    def _():
        hlr_ref[0] = sr if last_seg == SUBLANE else start_r[last_seg]
        hli_ref[0] = si if last_seg == SUBLANE else start_i[last_seg]


def _s5(h, h0r, h0i, p, rt, first_row, n_real):
    bsz, seq, _ = h.shape
    seg = rt // SUBLANE
    real_in_last = n_real - (seq - rt)
    assert 0 < real_in_last <= rt and real_in_last % seg == 0
    as_re = (jnp.exp(p["lam_dt_re"] * seg) * jnp.cos(p["lam_dt_im"] * seg)).reshape(1, S5_HALF)
    as_im = (jnp.exp(p["lam_dt_re"] * seg) * jnp.sin(p["lam_dt_im"] * seg)).reshape(1, S5_HALF)
    state_spec = pl.BlockSpec((1, 1, S5_HALF), lambda b, t: (b, 0, 0))
    return pl.pallas_call(
        functools.partial(_s5_body, first_row=first_row, last_seg=real_in_last // seg),
        grid=(bsz, seq // rt),
        in_specs=[
            pl.BlockSpec((1, rt, S5_WIDTH), lambda b, t: (b, t, 0)),
            state_spec, state_spec,
            _const_spec((1, S5_HALF)), _const_spec((1, S5_HALF)), _const_spec((1, S5_HALF)), _const_spec((1, S5_HALF)),
            _const_spec((S5_WIDTH, 2 * S5_HALF)), _const_spec((2 * S5_HALF, S5_WIDTH)),
            _const_spec((1, S5_WIDTH)), _const_spec((S5_WIDTH, S5_WIDTH)), _const_spec((1, S5_WIDTH)),
        ],
        out_specs=[pl.BlockSpec((1, rt, S5_WIDTH), lambda b, t: (b, t, 0)), state_spec, state_spec],
        out_shape=[jax.ShapeDtypeStruct((bsz, seq, S5_WIDTH), BF16),
                   jax.ShapeDtypeStruct((bsz, 1, S5_HALF), F32),
                   jax.ShapeDtypeStruct((bsz, 1, S5_HALF), F32)],
        scratch_shapes=[pltpu.VMEM((rt, 2 * S5_HALF), F32), pltpu.VMEM((2, S5_HALF), F32)],
        compiler_params=_cparams(2, VMEM_LIMIT),
        name="s5",
    )(h, h0r, h0i, p["a_re"], p["a_im"], as_re, as_im, p["bbig"], p["cbig"], p["d"], p["w_glu"], p["b_glu"])


def _rope_lanes(x, c, s1, s2, half):
    n = x.shape[-1]
    return x * c + pltpu.roll(x, n - half, 1) * s1 + pltpu.roll(x, half, 1) * s2


def _mla_rows_body(h_ref, qn_ref, kn_ref, wq_ref, *refs, attn_layout, scale):
    q_tabs, (ck_ref, s1k_ref, s2k_ref, q_ref, ckv_ref, kpe_ref) = refs[:-6], refs[-6:]
    h = h_ref[0]
    q_lat = h[:, :MLA_Q_RANK]
    q_lat = (q_lat * lax.rsqrt(jnp.mean(q_lat * q_lat, axis=-1, keepdims=True) + EPS) * qn_ref[...]).astype(BF16)
    half = MLA_ROPE // 2
    if attn_layout:
        cos_t, sin_t = q_tabs[0][...], q_tabs[1][...]
        q_t = _dot_nt(wq_ref[...], q_lat)
        for hd in range(MLA_HEADS):
            r0 = hd * MLA_QK_PAD
            x1 = q_t[r0 + MLA_NOPE:r0 + MLA_NOPE + half, :]
            x2 = q_t[r0 + MLA_NOPE + half:r0 + MLA_NOPE + 2 * half, :]
            q_ref[0, r0:r0 + MLA_QK_PAD, :] = jnp.concatenate(
                [q_t[r0:r0 + MLA_NOPE, :] * scale, x1 * cos_t - x2 * sin_t, x1 * sin_t + x2 * cos_t,
                 q_t[r0 + MLA_NOPE + 2 * half:r0 + MLA_QK_PAD, :]], axis=0).astype(BF16)
    else:
        cq, s1q, s2q = (t[...] for t in q_tabs)
        q = jnp.dot(q_lat, wq_ref[...], preferred_element_type=F32)
        for hd in range(MLA_HEADS):
            sl = slice(hd * MLA_QK_PAD, (hd + 1) * MLA_QK_PAD)
            q_ref[0, :, sl] = _rope_lanes(q[:, sl], cq, s1q, s2q, half).astype(BF16)
    c_kv = h[:, MLA_Q_RANK:MLA_Q_RANK + MLA_KV_RANK]
    ckv_ref[0] = c_kv * lax.rsqrt(jnp.mean(c_kv * c_kv, axis=-1, keepdims=True) + EPS) * kn_ref[...]
    k_pe = h[:, MLA_Q_RANK + MLA_KV_RANK:]
    kpe_ref[0] = _rope_lanes(k_pe, ck_ref[...], s1k_ref[...], s2k_ref[...], MLA_ROPE // 2)[:, :MLA_ROPE]


def _mla_rows(h, p, pos, tm, attn_layout):
    bsz, seq, _ = h.shape
    tab_spec = pl.BlockSpec((tm, LANE), lambda b, t: (t, 0))
    n_q = MLA_HEADS * MLA_QK_PAD
    half = MLA_ROPE // 2
    scale = (MLA_NOPE + MLA_ROPE) ** -0.5 * LOG2E
    k_tabs = _rope_tables(pos, half, LANE, LANE, 0)
    if attn_layout:
        q_shape, q_spec = (bsz, n_q, seq), pl.BlockSpec((1, n_q, tm), lambda b, t: (b, 0, t))
        ang = (ROPE_BASE ** (-jnp.arange(half, dtype=F32) / half))[:, None] * pos.astype(F32)[None, :]
        q_tabs = (jnp.cos(ang) * scale, jnp.sin(ang) * scale)
        q_tab_specs = [pl.BlockSpec((half, tm), lambda b, t: (0, t))] * 2
        wq = p["wq"].T
    else:
        q_shape, q_spec = (bsz, seq, n_q), pl.BlockSpec((1, tm, n_q), lambda b, t: (b, t, 0))
        q_tabs = _rope_tables(pos, half, MLA_QK_PAD, MLA_QK_PAD, MLA_NOPE, scale=scale)
        q_tab_specs = [tab_spec] * 3
        wq = p["wq"]
    return pl.pallas_call(
        functools.partial(_mla_rows_body, attn_layout=attn_layout, scale=scale),
        grid=(bsz, seq // tm),
        in_specs=[
            pl.BlockSpec((1, tm, EVEN_IN_PAD - S5_WIDTH), lambda b, t: (b, t, 1)),
            _const_spec((1, MLA_Q_RANK)), _const_spec((1, MLA_KV_RANK)),
            _const_spec(wq.shape),
        ] + q_tab_specs + [tab_spec] * 3,
        out_specs=[
            q_spec,
            pl.BlockSpec((1, tm, MLA_KV_RANK), lambda b, t: (b, t, 0)),
            pl.BlockSpec((1, tm, MLA_ROPE), lambda b, t: (b, t, 0)),
        ],
        out_shape=[jax.ShapeDtypeStruct(q_shape, BF16),
                   jax.ShapeDtypeStruct((bsz, seq, MLA_KV_RANK), F32),
                   jax.ShapeDtypeStruct((bsz, seq, MLA_ROPE), F32)],
        compiler_params=_cparams(2, VMEM_LIMIT),
        name="mla_rows",
    )(h, p["q_norm"], p["kv_norm"], wq, *q_tabs, *k_tabs)


def _mla_kv_body(ckv_ref, kpe_ref, wkn_ref, wkp_ref, wv_ref, k_ref, v_ref):
    ckv = ckv_ref[0].astype(BF16)
    k = jnp.dot(ckv, wkn_ref[...], preferred_element_type=F32)
    k = k + jnp.dot(kpe_ref[0].astype(BF16), wkp_ref[...], preferred_element_type=F32)
    k_ref[0] = k.astype(BF16)
    v_t = jnp.dot(ckv, wv_ref[...], preferred_element_type=F32).T
    for hd in range(MLA_HEADS):
        v_ref[0, hd, 0] = v_t[hd * MLA_V:(hd + 1) * MLA_V, :].astype(BF16)


def _mla_kv(ckv, kpe, p):
    bsz, seq, _ = ckv.shape
    n_t = seq // TILE
    return pl.pallas_call(
        _mla_kv_body,
        grid=(bsz, n_t),
        in_specs=[
            pl.BlockSpec((1, TILE, MLA_KV_RANK), lambda b, t: (b, t, 0)),
            pl.BlockSpec((1, TILE, MLA_ROPE), lambda b, t: (b, t, 0)),
            _const_spec(p["wkn"].shape), _const_spec(p["wkp"].shape), _const_spec(p["wv"].shape),
        ],
        out_specs=[pl.BlockSpec((1, TILE, MLA_HEADS * MLA_QK_PAD), lambda b, t: (b, t, 0)),
                   pl.BlockSpec((1, MLA_HEADS, 1, MLA_V, TILE), lambda b, t: (b, 0, t, 0, 0))],
        out_shape=[jax.ShapeDtypeStruct((bsz, seq, MLA_HEADS * MLA_QK_PAD), BF16),
                   jax.ShapeDtypeStruct((bsz, MLA_HEADS, n_t, MLA_V, TILE), BF16)],
        compiler_params=_cparams(2, VMEM_LIMIT),
        name="mla_kv",
    )(ckv, kpe, p["wkn"], p["wkp"], p["wv"])


def _flash_body(*refs, heads, dq, dv, causal):
    if causal:
        qt_ref, k_ref, vt_ref, fq_ref, fk_ref, o_ref = refs
    else:
        qt_ref, k_ref, vt_ref, o_ref = refs
    tq = qt_ref.shape[2]
    n_tiles, tk = vt_ref.shape[2], vt_ref.shape[4]
    r0 = pl.program_id(1) * tq
    j_last = jnp.minimum((r0 + tq - 1) // tk, n_tiles - 1)
    q_row = r0 + lax.broadcasted_iota(jnp.int32, (1, tq), 1)
    last = q_row if causal else ((q_row >> CHUNK_SHIFT) << CHUNK_SHIFT) + (CHUNK - 1)
    first_last = r0 if causal else ((r0 >> CHUNK_SHIFT) << CHUNK_SHIFT) + (CHUNK - 1)
    n_open = jnp.clip((first_last + 1) // tk, 1, j_last + 1)
    q_t = [qt_ref[0, hd * dq:(hd + 1) * dq, :] for hd in range(heads)]

    def tile(j, carry, masked):
        start = pl.multiple_of(j * tk, tk)
        if masked:
            k_row = start + lax.broadcasted_iota(jnp.int32, (tk, 1), 0)
            ok = (k_row <= last) & (k_row >= PAD)
        scores = [jnp.dot(k_ref[0, pl.ds(start, tk), hd * dq:(hd + 1) * dq], q_t[hd],
                          preferred_element_type=F32) for hd in range(heads)]
        probs = []
        for hd in range(heads):
            m, l, _ = carry[hd]
            s = scores[hd]
            if causal:
                s = s + (fq_ref[0, hd:hd + 1, :] - fk_ref[0, pl.ds(start, tk), hd:hd + 1])
            if masked:
                s = jnp.where(ok, s, NEG)
            m_new = jnp.maximum(m, jnp.max(s, axis=0, keepdims=True))
            a = jnp.exp2(m - m_new)
            p = jnp.exp2(s - m_new)
            probs.append((m_new, a * l + jnp.sum(p, axis=0, keepdims=True), a, p.astype(BF16)))
        new = []
        for hd in range(heads):
            m_new, l, a, p = probs[hd]
            acc = a * carry[hd][2] + jnp.dot(vt_ref[0, hd, j], p, preferred_element_type=F32)
            new.append((m_new, l, acc))
        return tuple(new)

    init = tuple((jnp.full((1, tq), NEG, F32), jnp.zeros((1, tq), F32), jnp.zeros((dv, tq), F32))
                 for _ in range(heads))
    carry = tile(0, init, True)
    carry = lax.fori_loop(1, n_open, functools.partial(tile, masked=False), carry)
    carry = lax.fori_loop(n_open, j_last + 1, functools.partial(tile, masked=True), carry)
    outs = [carry[hd][2] / carry[hd][1] for hd in range(heads)]
    per = LANE // dv
    for g in range(heads // per):
        o_ref[0, :, g * LANE:(g + 1) * LANE] = jnp.concatenate(
            outs[g * per:(g + 1) * per], axis=0).T.astype(o_ref.dtype)


def _flash(q_t, k, v_t, fq_t, fk, *, heads, dq, dv, causal):
    bsz, _, n_q = q_t.shape
    n_keys = k.shape[1]
    in_specs = [
        pl.BlockSpec((1, heads * dq, TILE), lambda b, i: (b, 0, i)),
        pl.BlockSpec((1, n_keys, heads * dq), lambda b, i: (b, 0, 0)),
        pl.BlockSpec((1,) + v_t.shape[1:], lambda b, i: (b, 0, 0, 0, 0)),
    ]
    args = [q_t, k, v_t]
    if causal:
        in_specs += [pl.BlockSpec((1, heads, TILE), lambda b, i: (b, 0, i)),
                     pl.BlockSpec((1, n_keys, heads), lambda b, i: (b, 0, 0))]
        args += [fq_t, fk]
    return pl.pallas_call(
        functools.partial(_flash_body, heads=heads, dq=dq, dv=dv, causal=causal),
        grid=(bsz, n_q // TILE),
        in_specs=in_specs,
        out_specs=pl.BlockSpec((1, TILE, heads * dv), lambda b, i: (b, i, 0)),
        out_shape=jax.ShapeDtypeStruct((bsz, n_q, heads * dv), BF16),
        compiler_params=_cparams(2, VMEM_LIMIT),
        name="flash_causal" if causal else "flash_chunk",
    )(*args)


def _online_softmax(s, m_ref, l_ref, idx):
    m_old = m_ref[idx]
    m_new = jnp.maximum(m_old, jnp.max(s, axis=-1, keepdims=True))
    a = jnp.exp2(m_old - m_new)
    p = jnp.exp2(s - m_new)
    m_ref[idx] = m_new
    l_ref[idx] = a * l_ref[idx] + jnp.sum(p, axis=-1, keepdims=True)
    return a, p


def _dot_nt(a, b):
    return lax.dot_general(a, b, (((1,), (1,)), ((), ())), preferred_element_type=F32)


def _fox_decode_body(q_ref, fq_ref, kc_ref, vc_ref, fkc_ref, kn_ref, vn_ref, fkn_ref, o_ref, m_ref, l_ref, acc_ref):
    t = pl.program_id(1)
    n_q = q_ref.shape[1]
    cols = [slice(hd * FOX_DIM, (hd + 1) * FOX_DIM) for hd in range(FOX_HEADS)]

    @pl.when(t == 0)
    def _():
        m_ref[...] = jnp.full(m_ref.shape, NEG, F32)
        l_ref[...] = jnp.zeros(l_ref.shape, F32)
        acc_ref[...] = jnp.zeros(acc_ref.shape, F32)

    def attend(scores, fk_rows, values, ok):
        probs = []
        for hd in range(FOX_HEADS):
            s = scores[hd] + (fq_ref[0, :, hd:hd + 1] - fk_rows[hd:hd + 1, :])
            if ok is not None:
                s = jnp.where(ok, s, NEG)
            probs.append(_online_softmax(s, m_ref, l_ref, hd))
        for hd in range(FOX_HEADS):
            a, p = probs[hd]
            acc_ref[hd] = a * acc_ref[hd] + values(hd, p.astype(BF16))

    attend([jnp.dot(q_ref[0, :, cols[hd]], kc_ref[0, 0, hd].astype(BF16), preferred_element_type=F32)
            for hd in range(FOX_HEADS)],
           fkc_ref[0], lambda hd, p: _dot_nt(p, vc_ref[0, 0, hd].astype(BF16)), None)

    @pl.when(t == pl.num_programs(1) - 1)
    def _():
        causal = (lax.broadcasted_iota(jnp.int32, (n_q, n_q), 1) <= lax.broadcasted_iota(jnp.int32, (n_q, n_q), 0))
        attend([_dot_nt(q_ref[0, :, sl], kn_ref[0, :, sl]) for sl in cols], fkn_ref[0],
               lambda hd, p: jnp.dot(p, vn_ref[0, :, cols[hd]], preferred_element_type=F32), causal)
        for hd in range(FOX_HEADS):
            o_ref[0, :, cols[hd]] = (acc_ref[hd] / l_ref[hd]).astype(o_ref.dtype)


def _fox_decode(q, k_new, v_new, cache_k, cache_v, layer, fcum, tk):
    bsz, n_q, width = q.shape
    n_past = cache_k.shape[2]
    rows_minor = lambda c: c.transpose(0, 1, 3, 4, 2)
    fk_t = fcum.transpose(0, 2, 1)
    row_spec = pl.BlockSpec((1, n_q, width), lambda b, t: (b, 0, 0))
    cache_spec = pl.BlockSpec((1, 1, FOX_HEADS, FOX_DIM, tk), lambda b, t: (layer, b, 0, 0, t))
    return pl.pallas_call(
        _fox_decode_body,
        grid=(bsz, n_past // tk),
        in_specs=[row_spec, pl.BlockSpec((1, n_q, FOX_HEADS), lambda b, t: (b, 0, 0)),
                  cache_spec, cache_spec, pl.BlockSpec((1, FOX_HEADS, tk), lambda b, t: (b, 0, t)),
                  row_spec, row_spec, pl.BlockSpec((1, FOX_HEADS, n_q), lambda b, t: (b, 0, 0))],
        out_specs=row_spec,
        out_shape=jax.ShapeDtypeStruct((bsz, n_q, width), BF16),
        scratch_shapes=[pltpu.VMEM((FOX_HEADS, n_q, 1), F32), pltpu.VMEM((FOX_HEADS, n_q, 1), F32),
                        pltpu.VMEM((FOX_HEADS, n_q, FOX_DIM), F32)],
        compiler_params=_cparams(2, VMEM_LIMIT),
        name="fox_decode",
    )(q, fcum[:, n_past:n_past + n_q], rows_minor(cache_k), rows_minor(cache_v), fk_t[:, :, :n_past],
      k_new, v_new, fk_t[:, :, n_past:n_past + n_q])


def _mla_decode_body(q_ref, wkn_ref, wv_ref, cc_ref, pc_ref, cn_ref, pn_ref, o_ref, qa_ref, qr_ref,
                     m_ref, l_ref, acc_ref):
    t = pl.program_id(1)
    n_q = q_ref.shape[1]

    @pl.when(t == 0)
    def _():
        for hd in range(MLA_HEADS):
            c0 = hd * MLA_QK_PAD
            rows = slice(hd * n_q, (hd + 1) * n_q)
            qa_ref[rows, :] = _dot_nt(q_ref[0, :, c0:c0 + MLA_NOPE],
                                      wkn_ref[:, c0:c0 + MLA_NOPE]).astype(BF16)
            qr_ref[rows, :] = q_ref[0, :, c0 + MLA_NOPE:c0 + MLA_NOPE + MLA_ROPE]
        m_ref[...] = jnp.full(m_ref.shape, NEG, F32)
        l_ref[...] = jnp.zeros(l_ref.shape, F32)
        acc_ref[...] = jnp.zeros(acc_ref.shape, F32)

    def attend(ckv, rope_scores):
        a, p = _online_softmax(_dot_nt(qa_ref[...], ckv) + rope_scores, m_ref, l_ref, 0)
        acc_ref[0] = a * acc_ref[0] + jnp.dot(p.astype(BF16), ckv, preferred_element_type=F32)

    attend(cc_ref[0, 0].astype(BF16),
           jnp.dot(qr_ref[...], pc_ref[0, 0].astype(BF16), preferred_element_type=F32))

    @pl.when(t == pl.num_programs(1) - 1)
    def _():
        attend(cn_ref[0].astype(BF16), _dot_nt(qr_ref[...], pn_ref[0].astype(BF16)))
        lat = (acc_ref[0] / l_ref[0]).astype(BF16)
        for hd in range(MLA_HEADS):
            cols = slice(hd * MLA_V, (hd + 1) * MLA_V)
            o_ref[0, :, cols] = jnp.dot(lat[hd * n_q:(hd + 1) * n_q, :], wv_ref[:, cols],
                                        preferred_element_type=F32).astype(o_ref.dtype)


def _mla_decode(q, ckv_new, kpe_new, cache_ckv, cache_kpe, layer, p, tk):
    bsz, n_q, _ = q.shape
    n_past = cache_ckv.shape[2]
    stacked = MLA_HEADS * n_q
    return pl.pallas_call(
        _mla_decode_body,
        grid=(bsz, n_past // tk),
        in_specs=[pl.BlockSpec((1, n_q, MLA_HEADS * MLA_QK_PAD), lambda b, t: (b, 0, 0)),
                  _const_spec(p["wkn"].shape), _const_spec(p["wv"].shape),
                  pl.BlockSpec((1, 1, tk, MLA_KV_RANK), lambda b, t: (layer, b, t, 0)),
                  pl.BlockSpec((1, 1, MLA_ROPE, tk), lambda b, t: (layer, b, 0, t)),
                  pl.BlockSpec((1, n_q, MLA_KV_RANK), lambda b, t: (b, 0, 0)),
                  pl.BlockSpec((1, n_q, MLA_ROPE), lambda b, t: (b, 0, 0))],
        out_specs=pl.BlockSpec((1, n_q, MLA_HEADS * MLA_V), lambda b, t: (b, 0, 0)),
        out_shape=jax.ShapeDtypeStruct((bsz, n_q, MLA_HEADS * MLA_V), BF16),
        scratch_shapes=[pltpu.VMEM((stacked, MLA_KV_RANK), BF16), pltpu.VMEM((stacked, MLA_ROPE), BF16),
                        pltpu.VMEM((1, stacked, 1), F32), pltpu.VMEM((1, stacked, 1), F32),
                        pltpu.VMEM((1, stacked, MLA_KV_RANK), F32)],
        compiler_params=_cparams(2, VMEM_LIMIT),
        name="mla_decode",
    )(q, p["wkn"], p["wv"], cache_ckv, cache_kpe.transpose(0, 1, 3, 2), ckv_new, kpe_new)


def _gate_body(x_ref, b_ref, lf_ref, fc_ref, *, new_start, first_row):
    n_rows = x_ref.shape[1]
    tri = (lax.broadcasted_iota(jnp.int32, (TILE, TILE), 0)
           >= lax.broadcasted_iota(jnp.int32, (TILE, TILE), 1)).astype(F32)
    carry = jnp.zeros((1, x_ref.shape[2]), F32)
    for i in range(n_rows // TILE):
        sl = slice(i * TILE, (i + 1) * TILE)
        x = x_ref[0, sl, :]
        rows = i * TILE + lax.broadcasted_iota(jnp.int32, (TILE, 1), 0)
        z = x + b_ref[...]
        lf = jnp.where(rows >= new_start, jnp.minimum(z, 0.0) - jnp.log1p(jnp.exp(-jnp.abs(z))), x)
        if first_row:
            lf = jnp.where(rows >= first_row, lf, 0.0)
        lf_ref[0, sl, :] = lf
        cs = jnp.dot(tri, lf, preferred_element_type=F32, precision=lax.Precision.HIGHEST) + carry
        fc_ref[0, sl, :] = cs
        carry = cs[TILE - 1:TILE, :]


def _gate(x, b_f, new_start, first_row):
    bsz, n_rows, heads = x.shape
    spec = pl.BlockSpec((1, n_rows, heads), lambda b: (b, 0, 0))
    return pl.pallas_call(
        functools.partial(_gate_body, new_start=new_start, first_row=first_row),
        grid=(bsz,),
        in_specs=[spec, _const_spec((1, heads))],
        out_specs=[spec, spec],
        out_shape=[jax.ShapeDtypeStruct(x.shape, F32)] * 2,
        compiler_params=_cparams(1),
        name="fox_gate",
    )(x, b_f)


RET_LOG_GAMMA = tuple(math.log(1.0 - 2.0 ** (-5.0 - h)) for h in range(RET_HEADS))


def _ret_body(rq_ref, rk_ref, rv_ref, rg_ref, s0_ref, dec_ref, cq_ref, s1q_ref, s2q_ref, ck_ref, s1k_ref, s2k_ref,
              o_ref, sl_ref, st_ref, *, n_tail):
    c = pl.program_id(1)
    ct = rq_ref.shape[1]

    @pl.when(c == 0)
    def _():
        st_ref[...] = s0_ref[0]

    q = _rope_lanes(rq_ref[0], cq_ref[...], s1q_ref[...], s2q_ref[...], RET_DK // 2)
    k = _rope_lanes(rk_ref[0], ck_ref[...], s1k_ref[...], s2k_ref[...], RET_DK // 2)
    v = rv_ref[0].astype(BF16)
    g = rg_ref[0]
    j = lax.broadcasted_iota(jnp.int32, (ct, 1), 0).astype(F32)
    for hd in range(RET_HEADS):
        lg = RET_LOG_GAMMA[hd]
        q_h = q[:, hd * RET_DK:(hd + 1) * RET_DK]
        k_h = k[:, hd * RET_DK:(hd + 1) * RET_DK]
        v_h = v[:, hd * RET_DV:(hd + 1) * RET_DV]
        scores = lax.dot_general(q_h.astype(BF16), k_h.astype(BF16), (((1,), (1,)), ((), ())),
                                 preferred_element_type=F32) * dec_ref[hd]
        s_h = st_ref[hd]
        out = jnp.dot(scores.astype(BF16), v_h, preferred_element_type=F32)
        out = out + jnp.dot((q_h * jnp.exp(lg * (j + 1.0))).astype(BF16), s_h.astype(BF16),
                            preferred_element_type=F32)
        k_dec = (k_h * jnp.exp(lg * (ct - 1.0 - j))).astype(BF16)
        st_ref[hd] = math.exp(lg * ct) * s_h + lax.dot_general(
            k_dec, v_h, (((0,), (0,)), ((), ())), preferred_element_type=F32)
        mu = jnp.mean(out, axis=-1, keepdims=True)
        oc = out - mu
        var = jnp.mean(oc * oc, axis=-1, keepdims=True)
        g_h = g[:, hd * RET_DV:(hd + 1) * RET_DV]
        o_ref[0, :, hd * RET_DV:(hd + 1) * RET_DV] = (
            g_h * jax.nn.sigmoid(g_h) * (oc * lax.rsqrt(var + EPS))).astype(o_ref.dtype)

    @pl.when(c == pl.num_programs(1) - 1)
    def _():
        for hd in range(RET_HEADS):
            sl_ref[0, hd] = st_ref[hd] * math.exp(-RET_LOG_GAMMA[hd] * n_tail)


def _retention(h, s0, tabs, ct, n_tail):
    bsz, seq, _ = h.shape
    tab_spec = pl.BlockSpec((ct, RET_QK), lambda b, c: (c, 0))
    st_spec = pl.BlockSpec((1, RET_HEADS, RET_DK, RET_DV), lambda b, c: (b, 0, 0, 0))
    rq_blk = (2 * FOX_W) // RET_QK
    rv_blk = (2 * FOX_W + 2 * RET_QK) // RET_VW
    diff = jnp.arange(ct, dtype=F32)[:, None] - jnp.arange(ct, dtype=F32)[None, :]
    decay = jnp.where(diff >= 0.0, jnp.exp(jnp.asarray(RET_LOG_GAMMA, F32)[:, None, None] * jnp.maximum(diff, 0.0)),
                      0.0)
    return pl.pallas_call(
        functools.partial(_ret_body, n_tail=n_tail),
        grid=(bsz, seq // ct),
        in_specs=[
            pl.BlockSpec((1, ct, RET_QK), lambda b, c: (b, c, rq_blk)),
            pl.BlockSpec((1, ct, RET_QK), lambda b, c: (b, c, rq_blk + 1)),
            pl.BlockSpec((1, ct, RET_VW), lambda b, c: (b, c, rv_blk)),
            pl.BlockSpec((1, ct, RET_VW), lambda b, c: (b, c, rv_blk + 1)),
            st_spec, _const_spec((RET_HEADS, ct, ct)),
        ] + [tab_spec] * 6,
        out_specs=[pl.BlockSpec((1, ct, RET_VW), lambda b, c: (b, c, 0)), st_spec],
        out_shape=[jax.ShapeDtypeStruct((bsz, seq, RET_VW), BF16),
                   jax.ShapeDtypeStruct((bsz, RET_HEADS, RET_DK, RET_DV), F32)],
        scratch_shapes=[pltpu.VMEM((RET_HEADS, RET_DK, RET_DV), F32)],
        compiler_params=_cparams(2, VMEM_LIMIT),
        name="retention",
    )(h, h, h, h, s0, decay, *tabs)


def _rope_tables(pos, half, width, group, offset, scale=1.0, valid=None):
    inv = ROPE_BASE ** (-jnp.arange(half, dtype=F32) / half)
    ang = pos.astype(F32)[:, None] * inv[None, :]
    cos, sin = jnp.cos(ang), jnp.sin(ang)
    n = pos.shape[0]
    one, zero = jnp.ones((n, 1), F32), jnp.zeros((n, 1), F32)

    def lanes(first, second, other):
        grp = jnp.concatenate([jnp.broadcast_to(other, (n, offset)), first, second,
                               jnp.broadcast_to(other, (n, group - offset - 2 * half))], axis=1)
        return jnp.tile(grp, (1, width // group))

    tabs = (lanes(cos, cos, one), lanes(-sin, 0.0 * sin, zero), lanes(0.0 * sin, sin, zero))
    if valid is not None:
        tabs = tuple(jnp.where(valid[:, None], t, 0.0) for t in tabs)
    return tuple(t * scale for t in tabs)


def _even_params(e, w):
    p = {}
    p["w_in"] = jnp.pad(w["even_w_in"][e], ((0, 0), (0, EVEN_IN_PAD - EVEN_IN))).astype(BF16)
    p["w_out"] = w["even_w_out"][e].astype(BF16)
    lam_re, lam_im = w["s5_a_re"][e].astype(F32), w["s5_a_im"][e].astype(F32)
    dt = jnp.exp(w["s5_log_dt"][e].astype(F32))[:, None]
    p["lam_dt_re"], p["lam_dt_im"] = lam_re * dt, lam_im * dt
    mag = jnp.exp(lam_re * dt)
    abar_re, abar_im = mag * jnp.cos(lam_im * dt), mag * jnp.sin(lam_im * dt)
    den = lam_re * lam_re + lam_im * lam_im
    f_re = ((abar_re - 1.0) * lam_re + abar_im * lam_im) / den
    f_im = (abar_im * lam_re - (abar_re - 1.0) * lam_im) / den
    b_re, b_im = w["s5_b_re"][e].astype(F32), w["s5_b_im"][e].astype(F32)
    bb_re = f_re[..., None] * b_re - f_im[..., None] * b_im
    bb_im = f_re[..., None] * b_im + f_im[..., None] * b_re
    eye = jnp.eye(S5_GROUPS, dtype=F32)

    def in_blocks(x):
        return jnp.einsum("gnc,gh->gchn", x, eye).reshape(S5_WIDTH, S5_HALF)

    def out_blocks(x):
        return jnp.einsum("gcn,gh->gnhc", x, eye).reshape(S5_HALF, S5_WIDTH)

    p["bbig"] = jnp.concatenate([in_blocks(bb_re), in_blocks(bb_im)], axis=1).astype(BF16)
    p["cbig"] = jnp.concatenate([out_blocks(w["s5_c_re"][e].astype(F32)),
                                 out_blocks(-w["s5_c_im"][e].astype(F32))], axis=0).astype(BF16)
    p["a_re"] = abar_re.reshape(1, S5_HALF)
    p["a_im"] = abar_im.reshape(1, S5_HALF)
    p["d"] = w["s5_d"][e].astype(F32).reshape(1, S5_WIDTH)
    p["w_glu"] = w["s5_w_glu"][e].astype(BF16)
    p["b_glu"] = w["s5_b_glu"][e].astype(F32).reshape(1, S5_WIDTH)
    p["q_norm"] = w["mla_q_norm"][e].astype(F32).reshape(1, MLA_Q_RANK)
    p["kv_norm"] = w["mla_kv_norm"][e].astype(F32).reshape(1, MLA_KV_RANK)
    wq = w["mla_w_uq"][e].reshape(MLA_Q_RANK, MLA_HEADS, MLA_NOPE + MLA_ROPE)
    wq = jnp.pad(wq, ((0, 0), (0, 0), (0, MLA_QK_PAD - MLA_NOPE - MLA_ROPE)))
    p["wq"] = wq.reshape(MLA_Q_RANK, MLA_HEADS * MLA_QK_PAD).astype(BF16)
    wkv = w["mla_w_ukv"][e].reshape(MLA_KV_RANK, MLA_HEADS, MLA_NOPE + MLA_V)
    wkn = jnp.pad(wkv[:, :, :MLA_NOPE], ((0, 0), (0, 0), (0, MLA_QK_PAD - MLA_NOPE)))
    p["wkn"] = wkn.reshape(MLA_KV_RANK, MLA_HEADS * MLA_QK_PAD).astype(BF16)
    place = jnp.pad(jnp.eye(MLA_ROPE, dtype=F32), ((0, 0), (MLA_NOPE, MLA_QK_PAD - MLA_NOPE - MLA_ROPE)))
    p["wkp"] = jnp.tile(place, (1, MLA_HEADS)).astype(BF16)
    p["wv"] = wkv[:, :, MLA_NOPE:].reshape(MLA_KV_RANK, MLA_HEADS * MLA_V).astype(BF16)
    return p


def _odd_params(o, w):
    w_in = w["odd_w_in"][o]
    c_logit = 3 * FOX_W
    cols = jnp.concatenate([
        w_in[:, :FOX_W] * (FOX_DIM ** -0.5),
        w_in[:, FOX_W:c_logit],
        w_in[:, c_logit + FOX_HEADS:],
        w_in[:, c_logit:c_logit + FOX_HEADS],
    ], axis=1)
    p = {"w_in": jnp.pad(cols, ((0, 0), (0, ODD_IN_PAD - cols.shape[1]))).astype(BF16)}
    p["w_out"] = w["odd_w_out"][o].astype(BF16)
    p["b_f"] = w["fox_b_f"][o].astype(F32).reshape(1, FOX_HEADS)
    return p


def _cache_tile(n_past):
    for tk in (512, 256, 128):
        if n_past % tk == 0:
            return tk
    raise ValueError("cache length must be a multiple of 128")


def _trunk(x, pos, n_real, out_rows, past, w, ffn, evens, odds):
    bsz, seq, _ = x.shape
    rows = bsz * seq
    prompt = past is None
    first_row = PAD if prompt else 0
    if prompt:
        tm = 2 * TILE if rows % (2 * TILE) == 0 else TILE
        rt = TILE
    else:
        n_past = past["cache_fox_k"].shape[2]
        tk = _cache_tile(n_past)
        tm, rt = rows, seq
    idx = jnp.arange(seq, dtype=jnp.int32)
    valid = (idx >= first_row) & (idx < n_real)
    st = {n: [] for n in ("mla_ckv", "mla_kpe", "s5_re", "s5_im", "fox_k", "fox_v", "fox_logf", "ret")}
    rq_tabs = _rope_tables(pos, RET_DK // 2, RET_QK, RET_DK, 0)
    rk_tabs = _rope_tables(pos, RET_DK // 2, RET_QK, RET_DK, 0, scale=RET_DK ** -0.5, valid=valid)
    x2 = x.reshape(rows, D_MODEL)

    def ln(l, i):
        return w["ln_g"][l, i].reshape(1, D_MODEL), w["ln_b"][l, i].reshape(1, D_MODEL)

    for l in range(DEPTH):
        x2 = _ffn_ln(x2, ffn, (l, 0), *ln(l, 0), tm)
        if l % 2 == 0:
            e = l // 2
            p = evens[e]
            h = _proj(x2, p["w_in"], tm).reshape(bsz, seq, EVEN_IN_PAD)
            if prompt:
                h0r = h0i = jnp.zeros((bsz, 1, S5_HALF), F32)
            else:
                h0r = past["state_s5_re"][e].astype(F32).reshape(bsz, 1, S5_HALF)
                h0i = past["state_s5_im"][e].astype(F32).reshape(bsz, 1, S5_HALF)
            s5_out, hlr, hli = _s5(h, h0r, h0i, p, rt, first_row, n_real)
            q, ckv, kpe = _mla_rows(h, p, pos, rt, attn_layout=prompt)
            if prompt:
                k_att, v_att = _mla_kv(ckv, kpe, p)
                mla_out = _flash(q, k_att, v_att, None, None, heads=MLA_HEADS, dq=MLA_QK_PAD, dv=MLA_V,
                                 causal=False)
            else:
                mla_out = _mla_decode(q, ckv, kpe, past["cache_mla_ckv"], past["cache_mla_kpe"], e, p, tk)
            mix = (s5_out.reshape(rows, S5_WIDTH), mla_out.reshape(rows, MLA_HEADS * MLA_V), p["w_out"])
            st["mla_ckv"].append(ckv)
            st["mla_kpe"].append(kpe)
            st["s5_re"].append(hlr.reshape(bsz, S5_GROUPS, S5_STATE))
            st["s5_im"].append(hli.reshape(bsz, S5_GROUPS, S5_STATE))
        else:
            o = l // 2
            p = odds[o]
            h, fq, fk16, fv = _proj_odd(x2, p["w_in"], bsz, TILE if prompt else tm, attn_layout=prompt)
            h = h.reshape(bsz, seq, ODD_H)
            fk16 = fk16.reshape(bsz, seq, FOX_W)
            f_logit = h[:, :, ODD_LOGIT_COL:ODD_LOGIT_COL + FOX_HEADS]
            if prompt:
                q_off = 0
                logf, fcum = _gate(f_logit, p["b_f"], 0, PAD)
                fcum = fcum * LOG2E
                fox_out = _flash(fq, fk16, fv, fcum.transpose(0, 2, 1), fcum, heads=FOX_HEADS, dq=FOX_DIM,
                                 dv=FOX_DIM, causal=True)
                s0 = jnp.zeros((bsz, RET_HEADS, RET_DK, RET_DV), F32)
            else:
                q_off = n_past
                gates = jnp.concatenate([past["cache_fox_logf"][o].astype(F32), f_logit], axis=1)
                gates = jnp.pad(gates, ((0, 0), (0, _round_up(n_past + seq, TILE) - n_past - seq), (0, 0)))
                logf, fcum = _gate(gates, p["b_f"], n_past, 0)
                fox_out = _fox_decode(fq.reshape(bsz, seq, FOX_W), fk16, fv.reshape(bsz, seq, FOX_W),
                                      past["cache_fox_k"], past["cache_fox_v"], o, fcum * LOG2E, tk)
                s0 = past["state_ret"][o].astype(F32)
            ret_out, s_last = _retention(h, s0, rq_tabs + rk_tabs, rt, seq - n_real)
            mix = (fox_out.reshape(rows, FOX_W), ret_out.reshape(rows, RET_VW), p["w_out"])
            st["fox_k"].append(h[:, :, :FOX_W].reshape(bsz, seq, FOX_HEADS, FOX_DIM))
            st["fox_v"].append(h[:, :, FOX_W:2 * FOX_W].reshape(bsz, seq, FOX_HEADS, FOX_DIM))
            st["fox_logf"].append(logf[:, q_off:q_off + seq])
            st["ret"].append(s_last)
        keep = (seq,) + out_rows if l == DEPTH - 1 and out_rows != (0, seq) else None
        x2 = _mix_ffn_ln(x2, *mix, ln(l, 1), ffn, (l, 1), ln(l, 2), tm, keep)
    return x2.reshape(bsz, out_rows[1], D_MODEL), {n: jnp.stack(a) for n, a in st.items()}


def kernel(x_prompt, x_sample, cache_mla_ckv, cache_mla_kpe, cache_fox_k, cache_fox_v, cache_fox_logf,
           state_s5_re, state_s5_im, state_ret, meta_tokens, ln_g, ln_b, ffn_w_gate, ffn_w_up, ffn_w_down,
           even_w_in, even_w_out, s5_a_re, s5_a_im, s5_b_re, s5_b_im, s5_c_re, s5_c_im, s5_d, s5_log_dt,
           s5_w_glu, s5_b_glu, mla_q_norm, mla_kv_norm, mla_w_uq, mla_w_ukv, odd_w_in, odd_w_out, fox_b_f):
    w = dict(ln_g=ln_g.astype(F32), ln_b=ln_b.astype(F32), even_w_in=even_w_in, even_w_out=even_w_out,
             s5_a_re=s5_a_re, s5_a_im=s5_a_im, s5_b_re=s5_b_re, s5_b_im=s5_b_im, s5_c_re=s5_c_re,
             s5_c_im=s5_c_im, s5_d=s5_d, s5_log_dt=s5_log_dt, s5_w_glu=s5_w_glu, s5_b_glu=s5_b_glu,
             mla_q_norm=mla_q_norm, mla_kv_norm=mla_kv_norm, mla_w_uq=mla_w_uq, mla_w_ukv=mla_w_ukv,
             odd_w_in=odd_w_in, odd_w_out=odd_w_out, fox_b_f=fox_b_f)
    past = dict(cache_mla_ckv=cache_mla_ckv, cache_mla_kpe=cache_mla_kpe, cache_fox_k=cache_fox_k,
                cache_fox_v=cache_fox_v, cache_fox_logf=cache_fox_logf, state_s5_re=state_s5_re,
                state_s5_im=state_s5_im, state_ret=state_ret)
    ffn = (ffn_w_gate.astype(BF16), ffn_w_up.astype(BF16), ffn_w_down.astype(BF16))
    evens = [_even_params(e, w) for e in range((DEPTH + 1) // 2)]
    odds = [_odd_params(o, w) for o in range(DEPTH // 2)]

    bsz, seq, _ = x_prompt.shape
    n_real = PAD + N_META + seq
    n_rows = _round_up(n_real, TILE)
    meta = jnp.broadcast_to(meta_tokens[None].astype(x_prompt.dtype), (bsz, N_META, D_MODEL))
    xp = jnp.concatenate([jnp.zeros((bsz, PAD, D_MODEL), x_prompt.dtype), meta, x_prompt,
                          jnp.zeros((bsz, n_rows - n_real, D_MODEL), x_prompt.dtype)], axis=1)
    pos_p = jnp.maximum(jnp.arange(n_rows, dtype=jnp.int32) - PAD, 0)
    y_p, st_p = _trunk(xp, pos_p, n_real, (PAD + N_META, seq), None, w, ffn, evens, odds)
    d_seq = x_sample.shape[1]
    pos_s = N_META + cache_fox_k.shape[2] + jnp.arange(d_seq, dtype=jnp.int32)
    y_s, st_s = _trunk(x_sample, pos_s, d_seq, (0, d_seq), past, w, ffn, evens, odds)

    def real(a):
        return a[:, :, PAD:n_real]

    return (y_p, y_s,
            real(st_p["mla_ckv"]), real(st_p["mla_kpe"]), real(st_p["fox_k"]), real(st_p["fox_v"]),
            real(st_p["fox_logf"]), st_p["s5_re"], st_p["s5_im"], st_p["ret"],
            st_s["mla_ckv"], st_s["mla_kpe"], st_s["fox_k"], st_s["fox_v"], st_s["fox_logf"],
            st_s["s5_re"], st_s["s5_im"], st_s["ret"])
```

```python
import functools
import math

import jax
import jax.numpy as jnp
from jax import lax
from jax.experimental import pallas as pl
from jax.experimental.pallas import tpu as pltpu

F32 = jnp.float32
BF16 = jnp.bfloat16

D_MODEL = 1024
DEPTH = 4
CHUNK = 64
CHUNK_SHIFT = 6
N_META = 16
S5_WIDTH = 512
S5_CH = 16
S5_GROUPS = S5_WIDTH // S5_CH
S5_STATE = 64
S5_HALF = S5_GROUPS * S5_STATE
MLA_HEADS = 8
MLA_Q_RANK = 256
MLA_KV_RANK = 128
MLA_NOPE = 64
MLA_ROPE = 32
MLA_V = 64
MLA_QK_PAD = 128
FOX_HEADS = 8
FOX_DIM = 64
FOX_W = FOX_HEADS * FOX_DIM
RET_HEADS = 4
RET_DK = 64
RET_DV = 128
RET_QK = RET_HEADS * RET_DK
RET_VW = RET_HEADS * RET_DV
D_FF = 2816
ROPE_BASE = 10000.0
ALPHA = (2.0 * DEPTH) ** 0.25
EPS = 1e-5
NEG = -1e30
LOG2E = math.log2(math.e)
EVEN_IN = S5_WIDTH + MLA_Q_RANK + MLA_KV_RANK + MLA_ROPE
EVEN_IN_PAD = 1024
ODD_IN_PAD = 3200
ODD_H = ODD_IN_PAD - FOX_W
ODD_LOGIT_COL = 2 * FOX_W + 2 * RET_QK + 2 * RET_VW

PAD = CHUNK - N_META
LANE = 128
SUBLANE = 8
TILE = 256
VMEM_LIMIT = 56 * 1024 * 1024


def _cparams(n_grid, vmem=None):
    return pltpu.CompilerParams(dimension_semantics=("arbitrary",) * n_grid, vmem_limit_bytes=vmem)


def _const_spec(shape):
    nd = len(shape)
    return pl.BlockSpec(shape, lambda *_: (0,) * nd)


def _round_up(n, m):
    return -(-n // m) * m


def _layer_norm_rows(z, g, b):
    mu = jnp.mean(z, axis=-1, keepdims=True)
    zc = z - mu
    var = jnp.mean(zc * zc, axis=-1, keepdims=True)
    return zc * lax.rsqrt(var + EPS) * g + b


def _ffn_rows(x, wg_ref, wu_ref, wd_ref, g_ref, b_ref, hid_ref):
    xb = x.astype(BF16)
    for c in range(D_FF // TILE):
        sl = slice(c * TILE, (c + 1) * TILE)
        hg = jnp.dot(xb, wg_ref[:, sl], preferred_element_type=F32)
        hu = jnp.dot(xb, wu_ref[:, sl], preferred_element_type=F32)
        hid_ref[:, sl] = (hg * jax.nn.sigmoid(hg) * hu).astype(BF16)
    y = jnp.dot(hid_ref[...], wd_ref[...], preferred_element_type=F32)
    return _layer_norm_rows(ALPHA * x + 0.5 * y, g_ref[...], b_ref[...])


def _ffn_body(x_ref, wg_ref, wu_ref, wd_ref, g_ref, b_ref, o_ref, hid_ref):
    o_ref[...] = _ffn_rows(x_ref[...], wg_ref, wu_ref, wd_ref, g_ref, b_ref, hid_ref)


def _ffn_weight_specs(which):
    once = dict(pipeline_mode=pl.Buffered(1))
    pick = lambda i: which + (0, 0)
    return [pl.BlockSpec((None, None, D_MODEL, D_FF), pick, **once),
            pl.BlockSpec((None, None, D_MODEL, D_FF), pick, **once),
            pl.BlockSpec((None, None, D_FF, D_MODEL), pick, **once),
            _const_spec((1, D_MODEL)), _const_spec((1, D_MODEL))]


def _ffn_ln(x, ffn, which, g, b, tm):
    rows = x.shape[0]
    return pl.pallas_call(
        _ffn_body,
        grid=(rows // tm,),
        in_specs=[pl.BlockSpec((tm, D_MODEL), lambda i: (i, 0))] + _ffn_weight_specs(which),
        out_specs=pl.BlockSpec((tm, D_MODEL), lambda i: (i, 0)),
        out_shape=jax.ShapeDtypeStruct((rows, D_MODEL), F32),
        scratch_shapes=[pltpu.VMEM((tm, D_FF), BF16)],
        compiler_params=_cparams(1, VMEM_LIMIT),
        name="ffn_ln",
    )(x, *ffn, g, b)


def _mix_ffn_body(x_ref, a1_ref, a2_ref, wo_ref, g1_ref, b1_ref, wg_ref, wu_ref, wd_ref, g2_ref, b2_ref,
                  o_ref, hid_ref):
    k1 = a1_ref.shape[1]
    y = jnp.dot(a1_ref[...], wo_ref[:k1, :], preferred_element_type=F32)
    y = y + jnp.dot(a2_ref[...], wo_ref[k1:, :], preferred_element_type=F32)
    x1 = _layer_norm_rows(ALPHA * x_ref[...] + y, g1_ref[...], b1_ref[...])
    o_ref[...] = _ffn_rows(x1, wg_ref, wu_ref, wd_ref, g2_ref, b2_ref, hid_ref)


def _mix_ffn_ln(x, a1, a2, w_out, ln1, ffn, which, ln2, tm, keep=None):
    rows = x.shape[0]
    if keep is None:
        n_steps = rows // tm
        row_spec = lambda n: pl.BlockSpec((tm, n), lambda i: (i, 0))
    else:
        seq, start, count = keep
        assert count % tm == 0
        per = count // tm
        n_steps = (rows // seq) * per
        align = math.gcd(seq, start, tm)
        row_spec = lambda n: pl.BlockSpec(
            (pl.Element(tm), pl.Element(n)),
            lambda i: (pl.multiple_of((i // per) * seq + start + (i % per) * tm, align), 0))
    return pl.pallas_call(
        _mix_ffn_body,
        grid=(n_steps,),
        in_specs=[row_spec(D_MODEL), row_spec(a1.shape[1]), row_spec(a2.shape[1]),
                  pl.BlockSpec(w_out.shape, lambda i: (0, 0), pipeline_mode=pl.Buffered(1)),
                  _const_spec((1, D_MODEL)), _const_spec((1, D_MODEL))] + _ffn_weight_specs(which),
        out_specs=pl.BlockSpec((tm, D_MODEL), lambda i: (i, 0)),
        out_shape=jax.ShapeDtypeStruct((n_steps * tm, D_MODEL), F32),
        scratch_shapes=[pltpu.VMEM((tm, D_FF), BF16)],
        compiler_params=_cparams(1, VMEM_LIMIT),
        name="mix_ffn_ln",
    )(x, a1, a2, w_out, *ln1, *ffn, *ln2)


def _proj_body(x_ref, w_ref, o_ref):
    o_ref[...] = jnp.dot(x_ref[...].astype(BF16), w_ref[...], preferred_element_type=F32)


def _proj(x, w, tm):
    rows, n = x.shape[0], w.shape[1]
    return pl.pallas_call(
        _proj_body,
        grid=(rows // tm,),
        in_specs=[pl.BlockSpec((tm, D_MODEL), lambda i: (i, 0)), _const_spec(w.shape)],
        out_specs=pl.BlockSpec((tm, n), lambda i: (i, 0)),
        out_shape=jax.ShapeDtypeStruct((rows, n), F32),
        compiler_params=_cparams(1, VMEM_LIMIT),
        name="in_proj_even",
    )(x, w)


def _proj_odd_body(x_ref, w_ref, h_ref, q_ref, k_ref, v_ref, *, attn_layout):
    y = jnp.dot(x_ref[...].astype(BF16), w_ref[...], preferred_element_type=F32)
    h_ref[...] = y[:, FOX_W:]
    k_ref[...] = y[:, FOX_W:2 * FOX_W].astype(BF16)
    q = y[:, :FOX_W] * LOG2E
    if attn_layout:
        q_ref[0] = q.T.astype(BF16)
        v_t = y[:, 2 * FOX_W:3 * FOX_W].T
        for hd in range(FOX_HEADS):
            v_ref[0, hd, 0] = v_t[hd * FOX_DIM:(hd + 1) * FOX_DIM, :].astype(BF16)
    else:
        q_ref[...] = q.astype(BF16)
        v_ref[...] = y[:, 2 * FOX_W:3 * FOX_W].astype(BF16)


def _proj_odd(x, w, bsz, tm, attn_layout):
    rows = x.shape[0]
    seq = rows // bsz
    row_spec = lambda n: pl.BlockSpec((tm, n), lambda i: (i, 0))
    if attn_layout:
        assert tm == TILE and seq % TILE == 0
        n_t = seq // TILE
        q_shape = jax.ShapeDtypeStruct((bsz, FOX_W, seq), BF16)
        q_spec = pl.BlockSpec((1, FOX_W, TILE), lambda i: (i // n_t, 0, i % n_t))
        v_shape = jax.ShapeDtypeStruct((bsz, FOX_HEADS, n_t, FOX_DIM, TILE), BF16)
        v_spec = pl.BlockSpec((1, FOX_HEADS, 1, FOX_DIM, TILE), lambda i: (i // n_t, 0, i % n_t, 0, 0))
    else:
        q_shape = v_shape = jax.ShapeDtypeStruct((rows, FOX_W), BF16)
        q_spec = v_spec = row_spec(FOX_W)
    return pl.pallas_call(
        functools.partial(_proj_odd_body, attn_layout=attn_layout),
        grid=(rows // tm,),
        in_specs=[row_spec(D_MODEL), _const_spec(w.shape)],
        out_specs=[row_spec(ODD_H), q_spec, row_spec(FOX_W), v_spec],
        out_shape=[jax.ShapeDtypeStruct((rows, ODD_H), F32), q_shape,
                   jax.ShapeDtypeStruct((rows, FOX_W), BF16), v_shape],
        compiler_params=_cparams(1, VMEM_LIMIT),
        name="in_proj_odd",
    )(x, w)


def _s5_body(u_ref, h0r_ref, h0i_ref, ar_ref, ai_ref, asr_ref, asi_ref, bbig_ref, cbig_ref, d_ref, wglu_ref,
             bglu_ref, o_ref, hlr_ref, hli_ref, hs_ref, st_ref, *, first_row, last_seg):
    t = pl.program_id(1)
    rt = u_ref.shape[1]
    seg = rt // SUBLANE

    @pl.when(t == 0)
    def _():
        st_ref[0:1, :] = h0r_ref[0]
        st_ref[1:2, :] = h0i_ref[0]

    u = u_ref[0]
    if first_row:
        rows = t * rt + lax.broadcasted_iota(jnp.int32, (rt, 1), 0)
        u = jnp.where(rows >= first_row, u, 0.0)
    i0 = lax.broadcasted_iota(jnp.int32, (rt, rt), 0)
    i1 = lax.broadcasted_iota(jnp.int32, (rt, rt), 1)
    regroup = (i1 == (i0 & (SUBLANE - 1)) * seg + (i0 >> 3)).astype(BF16)
    restore = (i0 == (i1 & (SUBLANE - 1)) * seg + (i1 >> 3)).astype(BF16)
    ub = jnp.dot(regroup, u.astype(BF16), preferred_element_type=F32).astype(BF16)

    half_w, half_s = S5_WIDTH // 2, S5_HALF // 2
    for kb in range(2):
        for part in range(2):
            c0 = part * S5_HALF + kb * half_s
            hs_ref[:, c0:c0 + half_s] = jnp.dot(
                ub[:, kb * half_w:(kb + 1) * half_w], bbig_ref[kb * half_w:(kb + 1) * half_w, c0:c0 + half_s],
                preferred_element_type=F32)

    def scan(lanes, start, store):
        re_l = lanes
        im_l = slice(S5_HALF + lanes.start, S5_HALF + lanes.stop)
        ar = jnp.broadcast_to(ar_ref[:, lanes], (SUBLANE, lanes.stop - lanes.start))
        ai = jnp.broadcast_to(ai_ref[:, lanes], (SUBLANE, lanes.stop - lanes.start))

        def step(k, carry):
            hr, hi = carry
            base = pl.multiple_of(k * SUBLANE, SUBLANE)
            nr = ar * hr - ai * hi + hs_ref[pl.ds(base, SUBLANE), re_l]
            ni = ar * hi + ai * hr + hs_ref[pl.ds(base, SUBLANE), im_l]
            if store:
                hs_ref[pl.ds(base, SUBLANE), re_l] = nr
                hs_ref[pl.ds(base, SUBLANE), im_l] = ni
            return nr, ni

        return lax.fori_loop(0, seg, step, start)

    slabs = [slice(c * half_s, (c + 1) * half_s) for c in range(2)]
    zero = jnp.zeros((SUBLANE, half_s), F32)
    ends = [scan(lanes, (zero, zero), False) for lanes in slabs]
    er = jnp.concatenate([e[0] for e in ends], axis=1)
    ei = jnp.concatenate([e[1] for e in ends], axis=1)
    asr, asi = asr_ref[...], asi_ref[...]
    sr, si = st_ref[0:1, :], st_ref[1:2, :]
    start_r, start_i = [], []
    for j in range(SUBLANE):
        start_r.append(sr)
        start_i.append(si)
        sr, si = asr * sr - asi * si + er[j:j + 1], asr * si + asi * sr + ei[j:j + 1]
    st_ref[0:1, :] = sr
    st_ref[1:2, :] = si
    start_r8, start_i8 = jnp.concatenate(start_r, axis=0), jnp.concatenate(start_i, axis=0)
    for lanes in slabs:
        scan(lanes, (start_r8[:, lanes], start_i8[:, lanes]), True)

    hb = hs_ref[...].astype(BF16)
    ys = []
    for kb in range(2):
        re_rows = slice(kb * half_s, (kb + 1) * half_s)
        im_rows = slice(S5_HALF + kb * half_s, S5_HALF + (kb + 1) * half_s)
        cols = slice(kb * half_w, (kb + 1) * half_w)
        ys.append(jnp.dot(hb[:, re_rows], cbig_ref[re_rows, cols], preferred_element_type=F32)
                  + jnp.dot(hb[:, im_rows], cbig_ref[im_rows, cols], preferred_element_type=F32))
    y = jnp.concatenate(ys, axis=1)
    y_hi = y.astype(BF16)
    y_lo = (y - y_hi.astype(F32)).astype(BF16)
    y = (jnp.dot(restore, y_hi, preferred_element_type=F32) + jnp.dot(restore, y_lo, preferred_element_type=F32)
         + d_ref[...] * u)
    g = jax.nn.gelu(y)
    gate = jnp.dot(g.astype(BF16), wglu_ref[...], preferred_element_type=F32) + bglu_ref[...]
    o_ref[0] = (g * jax.nn.sigmoid(gate)).astype(o_ref.dtype)

    @pl.when(t == pl.num_programs(1) - 1)
    def _():
        hlr_ref[0] = sr if last_seg == SUBLANE else start_r[last_seg]
        hli_ref[0] = si if last_seg == SUBLANE else start_i[last_seg]


def _s5(h, h0r, h0i, p, rt, first_row, n_real):
    bsz, seq, _ = h.shape
    seg = rt // SUBLANE
    real_in_last = n_real - (seq - rt)
    assert 0 < real_in_last <= rt and real_in_last % seg == 0
    as_re = (jnp.exp(p["lam_dt_re"] * seg) * jnp.cos(p["lam_dt_im"] * seg)).reshape(1, S5_HALF)
    as_im = (jnp.exp(p["lam_dt_re"] * seg) * jnp.sin(p["lam_dt_im"] * seg)).reshape(1, S5_HALF)
    state_spec = pl.BlockSpec((1, 1, S5_HALF), lambda b, t: (b, 0, 0))
    return pl.pallas_call(
        functools.partial(_s5_body, first_row=first_row, last_seg=real_in_last // seg),
        grid=(bsz, seq // rt),
        in_specs=[
            pl.BlockSpec((1, rt, S5_WIDTH), lambda b, t: (b, t, 0)),
            state_spec, state_spec,
            _const_spec((1, S5_HALF)), _const_spec((1, S5_HALF)), _const_spec((1, S5_HALF)), _const_spec((1, S5_HALF)),
            _const_spec((S5_WIDTH, 2 * S5_HALF)), _const_spec((2 * S5_HALF, S5_WIDTH)),
            _const_spec((1, S5_WIDTH)), _const_spec((S5_WIDTH, S5_WIDTH)), _const_spec((1, S5_WIDTH)),
        ],
        out_specs=[pl.BlockSpec((1, rt, S5_WIDTH), lambda b, t: (b, t, 0)), state_spec, state_spec],
        out_shape=[jax.ShapeDtypeStruct((bsz, seq, S5_WIDTH), BF16),
                   jax.ShapeDtypeStruct((bsz, 1, S5_HALF), F32),
                   jax.ShapeDtypeStruct((bsz, 1, S5_HALF), F32)],
        scratch_shapes=[pltpu.VMEM((rt, 2 * S5_HALF), F32), pltpu.VMEM((2, S5_HALF), F32)],
        compiler_params=_cparams(2, VMEM_LIMIT),
        name="s5",
    )(h, h0r, h0i, p["a_re"], p["a_im"], as_re, as_im, p["bbig"], p["cbig"], p["d"], p["w_glu"], p["b_glu"])


def _rope_lanes(x, c, s1, s2, half):
    n = x.shape[-1]
    return x * c + pltpu.roll(x, n - half, 1) * s1 + pltpu.roll(x, half, 1) * s2


def _mla_rows_body(h_ref, qn_ref, kn_ref, wq_ref, *refs, attn_layout, scale):
    q_tabs, (ck_ref, s1k_ref, s2k_ref, q_ref, ckv_ref, kpe_ref) = refs[:-6], refs[-6:]
    h = h_ref[0]
    q_lat = h[:, :MLA_Q_RANK]
    q_lat = (q_lat * lax.rsqrt(jnp.mean(q_lat * q_lat, axis=-1, keepdims=True) + EPS) * qn_ref[...]).astype(BF16)
    half = MLA_ROPE // 2
    if attn_layout:
        cos_t, sin_t = q_tabs[0][...], q_tabs[1][...]
        q_t = _dot_nt(wq_ref[...], q_lat)
        for hd in range(MLA_HEADS):
            r0 = hd * MLA_QK_PAD
            x1 = q_t[r0 + MLA_NOPE:r0 + MLA_NOPE + half, :]
            x2 = q_t[r0 + MLA_NOPE + half:r0 + MLA_NOPE + 2 * half, :]
            q_ref[0, r0:r0 + MLA_QK_PAD, :] = jnp.concatenate(
                [q_t[r0:r0 + MLA_NOPE, :] * scale, x1 * cos_t - x2 * sin_t, x1 * sin_t + x2 * cos_t,
                 q_t[r0 + MLA_NOPE + 2 * half:r0 + MLA_QK_PAD, :]], axis=0).astype(BF16)
    else:
        cq, s1q, s2q = (t[...] for t in q_tabs)
        q = jnp.dot(q_lat, wq_ref[...], preferred_element_type=F32)
        for hd in range(MLA_HEADS):
            sl = slice(hd * MLA_QK_PAD, (hd + 1) * MLA_QK_PAD)
            q_ref[0, :, sl] = _rope_lanes(q[:, sl], cq, s1q, s2q, half).astype(BF16)
    c_kv = h[:, MLA_Q_RANK:MLA_Q_RANK + MLA_KV_RANK]
    ckv_ref[0] = c_kv * lax.rsqrt(jnp.mean(c_kv * c_kv, axis=-1, keepdims=True) + EPS) * kn_ref[...]
    k_pe = h[:, MLA_Q_RANK + MLA_KV_RANK:]
    kpe_ref[0] = _rope_lanes(k_pe, ck_ref[...], s1k_ref[...], s2k_ref[...], MLA_ROPE // 2)[:, :MLA_ROPE]


def _mla_rows(h, p, pos, tm, attn_layout):
    bsz, seq, _ = h.shape
    tab_spec = pl.BlockSpec((tm, LANE), lambda b, t: (t, 0))
    n_q = MLA_HEADS * MLA_QK_PAD
    half = MLA_ROPE // 2
    scale = (MLA_NOPE + MLA_ROPE) ** -0.5 * LOG2E
    k_tabs = _rope_tables(pos, half, LANE, LANE, 0)
    if attn_layout:
        q_shape, q_spec = (bsz, n_q, seq), pl.BlockSpec((1, n_q, tm), lambda b, t: (b, 0, t))
        ang = (ROPE_BASE ** (-jnp.arange(half, dtype=F32) / half))[:, None] * pos.astype(F32)[None, :]
        q_tabs = (jnp.cos(ang) * scale, jnp.sin(ang) * scale)
        q_tab_specs = [pl.BlockSpec((half, tm), lambda b, t: (0, t))] * 2
        wq = p["wq"].T
    else:
        q_shape, q_spec = (bsz, seq, n_q), pl.BlockSpec((1, tm, n_q), lambda b, t: (b, t, 0))
        q_tabs = _rope_tables(pos, half, MLA_QK_PAD, MLA_QK_PAD, MLA_NOPE, scale=scale)
        q_tab_specs = [tab_spec] * 3
        wq = p["wq"]
    return pl.pallas_call(
        functools.partial(_mla_rows_body, attn_layout=attn_layout, scale=scale),
        grid=(bsz, seq // tm),
        in_specs=[
            pl.BlockSpec((1, tm, EVEN_IN_PAD - S5_WIDTH), lambda b, t: (b, t, 1)),
            _const_spec((1, MLA_Q_RANK)), _const_spec((1, MLA_KV_RANK)),
            _const_spec(wq.shape),
        ] + q_tab_specs + [tab_spec] * 3,
        out_specs=[
            q_spec,
            pl.BlockSpec((1, tm, MLA_KV_RANK), lambda b, t: (b, t, 0)),
            pl.BlockSpec((1, tm, MLA_ROPE), lambda b, t: (b, t, 0)),
        ],
        out_shape=[jax.ShapeDtypeStruct(q_shape, BF16),
                   jax.ShapeDtypeStruct((bsz, seq, MLA_KV_RANK), F32),
                   jax.ShapeDtypeStruct((bsz, seq, MLA_ROPE), F32)],
        compiler_params=_cparams(2, VMEM_LIMIT),
        name="mla_rows",
    )(h, p["q_norm"], p["kv_norm"], wq, *q_tabs, *k_tabs)


def _mla_kv_body(ckv_ref, kpe_ref, wkn_ref, wkp_ref, wv_ref, k_ref, v_ref):
    ckv = ckv_ref[0].astype(BF16)
    k = jnp.dot(ckv, wkn_ref[...], preferred_element_type=F32)
    k = k + jnp.dot(kpe_ref[0].astype(BF16), wkp_ref[...], preferred_element_type=F32)
    k_ref[0] = k.astype(BF16)
    v_t = jnp.dot(ckv, wv_ref[...], preferred_element_type=F32).T
    for hd in range(MLA_HEADS):
        v_ref[0, hd, 0] = v_t[hd * MLA_V:(hd + 1) * MLA_V, :].astype(BF16)


def _mla_kv(ckv, kpe, p):
    bsz, seq, _ = ckv.shape
    n_t = seq // TILE
    return pl.pallas_call(
        _mla_kv_body,
        grid=(bsz, n_t),
        in_specs=[
            pl.BlockSpec((1, TILE, MLA_KV_RANK), lambda b, t: (b, t, 0)),
            pl.BlockSpec((1, TILE, MLA_ROPE), lambda b, t: (b, t, 0)),
            _const_spec(p["wkn"].shape), _const_spec(p["wkp"].shape), _const_spec(p["wv"].shape),
        ],
        out_specs=[pl.BlockSpec((1, TILE, MLA_HEADS * MLA_QK_PAD), lambda b, t: (b, t, 0)),
                   pl.BlockSpec((1, MLA_HEADS, 1, MLA_V, TILE), lambda b, t: (b, 0, t, 0, 0))],
        out_shape=[jax.ShapeDtypeStruct((bsz, seq, MLA_HEADS * MLA_QK_PAD), BF16),
                   jax.ShapeDtypeStruct((bsz, MLA_HEADS, n_t, MLA_V, TILE), BF16)],
        compiler_params=_cparams(2, VMEM_LIMIT),
        name="mla_kv",
    )(ckv, kpe, p["wkn"], p["wkp"], p["wv"])


def _flash_body(*refs, heads, dq, dv, causal):
    if causal:
        qt_ref, k_ref, vt_ref, fq_ref, fk_ref, o_ref = refs
    else:
        qt_ref, k_ref, vt_ref, o_ref = refs
    tq = qt_ref.shape[2]
    n_tiles, tk = vt_ref.shape[2], vt_ref.shape[4]
    r0 = pl.program_id(1) * tq
    j_last = jnp.minimum((r0 + tq - 1) // tk, n_tiles - 1)
    q_row = r0 + lax.broadcasted_iota(jnp.int32, (1, tq), 1)
    last = q_row if causal else ((q_row >> CHUNK_SHIFT) << CHUNK_SHIFT) + (CHUNK - 1)
    first_last = r0 if causal else ((r0 >> CHUNK_SHIFT) << CHUNK_SHIFT) + (CHUNK - 1)
    n_open = jnp.clip((first_last + 1) // tk, 1, j_last + 1)
    q_t = [qt_ref[0, hd * dq:(hd + 1) * dq, :] for hd in range(heads)]

    def tile(j, carry, masked, wide):
        width = 2 * tk if wide else tk
        start = pl.multiple_of(j * tk, tk)
        if masked:
            k_row = start + lax.broadcasted_iota(jnp.int32, (width, 1), 0)
            ok = (k_row <= last) & (k_row >= PAD)
        scores = [jnp.dot(k_ref[0, pl.ds(start, width), hd * dq:(hd + 1) * dq], q_t[hd],
                          preferred_element_type=F32) for hd in range(heads)]
        probs = []
        for hd in range(heads):
            m, l, _ = carry[hd]
            s = scores[hd]
            if causal:
                s = s + (fq_ref[0, hd:hd + 1, :] - fk_ref[0, pl.ds(start, width), hd:hd + 1])
            if masked:
                s = jnp.where(ok, s, NEG)
            m_new = jnp.maximum(m, jnp.max(s, axis=0, keepdims=True))
            a = jnp.exp2(m - m_new)
            p = jnp.exp2(s - m_new)
            probs.append((m_new, a * l + jnp.sum(p, axis=0, keepdims=True), a, p.astype(BF16)))
        new = []
        for hd in range(heads):
            m_new, l, a, p = probs[hd]
            acc = a * carry[hd][2] + jnp.dot(vt_ref[0, hd, j], p[:tk], preferred_element_type=F32)
            if wide:
                acc = acc + jnp.dot(vt_ref[0, hd, j + 1], p[tk:], preferred_element_type=F32)
            new.append((m_new, l, acc))
        return tuple(new)

    init = tuple((jnp.full((1, tq), NEG, F32), jnp.zeros((1, tq), F32), jnp.zeros((dv, tq), F32))
                 for _ in range(heads))
    carry = tile(0, init, True, False)
    n_wide = (n_open - 1) // 2
    carry = lax.fori_loop(0, n_wide, lambda i, c: tile(1 + 2 * i, c, False, True), carry)
    carry = lax.fori_loop(1 + 2 * n_wide, n_open, functools.partial(tile, masked=False, wide=False), carry)
    carry = lax.fori_loop(n_open, j_last + 1, functools.partial(tile, masked=True, wide=False), carry)
    outs = [carry[hd][2] / carry[hd][1] for hd in range(heads)]
    per = LANE // dv
    for g in range(heads // per):
        o_ref[0, :, g * LANE:(g + 1) * LANE] = jnp.concatenate(
            outs[g * per:(g + 1) * per], axis=0).T.astype(o_ref.dtype)


def _flash(q_t, k, v_t, fq_t, fk, *, heads, dq, dv, causal):
    bsz, _, n_q = q_t.shape
    n_keys = k.shape[1]
    in_specs = [
        pl.BlockSpec((1, heads * dq, TILE), lambda b, i: (b, 0, i)),
        pl.BlockSpec((1, n_keys, heads * dq), lambda b, i: (b, 0, 0)),
        pl.BlockSpec((1,) + v_t.shape[1:], lambda b, i: (b, 0, 0, 0, 0)),
    ]
    args = [q_t, k, v_t]
    if causal:
        in_specs += [pl.BlockSpec((1, heads, TILE), lambda b, i: (b, 0, i)),
                     pl.BlockSpec((1, n_keys, heads), lambda b, i: (b, 0, 0))]
        args += [fq_t, fk]
    return pl.pallas_call(
        functools.partial(_flash_body, heads=heads, dq=dq, dv=dv, causal=causal),
        grid=(bsz, n_q // TILE),
        in_specs=in_specs,
        out_specs=pl.BlockSpec((1, TILE, heads * dv), lambda b, i: (b, i, 0)),
        out_shape=jax.ShapeDtypeStruct((bsz, n_q, heads * dv), BF16),
        compiler_params=_cparams(2, VMEM_LIMIT),
        name="flash_causal" if causal else "flash_chunk",
    )(*args)


def _online_softmax(s, m_ref, l_ref, idx):
    m_old = m_ref[idx]
    m_new = jnp.maximum(m_old, jnp.max(s, axis=-1, keepdims=True))
    a = jnp.exp2(m_old - m_new)
    p = jnp.exp2(s - m_new)
    m_ref[idx] = m_new
    l_ref[idx] = a * l_ref[idx] + jnp.sum(p, axis=-1, keepdims=True)
    return a, p


def _dot_nt(a, b):
    return lax.dot_general(a, b, (((1,), (1,)), ((), ())), preferred_element_type=F32)


def _fox_decode_body(q_ref, fq_ref, kc_ref, vc_ref, fkc_ref, kn_ref, vn_ref, fkn_ref, o_ref, m_ref, l_ref, acc_ref):
    t = pl.program_id(1)
    n_q = q_ref.shape[1]
    cols = [slice(hd * FOX_DIM, (hd + 1) * FOX_DIM) for hd in range(FOX_HEADS)]

    @pl.when(t == 0)
    def _():
        m_ref[...] = jnp.full(m_ref.shape, NEG, F32)
        l_ref[...] = jnp.zeros(l_ref.shape, F32)
        acc_ref[...] = jnp.zeros(acc_ref.shape, F32)

    def attend(scores, fk_rows, values, ok):
        probs = []
        for hd in range(FOX_HEADS):
            s = scores[hd] + (fq_ref[0, :, hd:hd + 1] - fk_rows[hd:hd + 1, :])
            if ok is not None:
                s = jnp.where(ok, s, NEG)
            probs.append(_online_softmax(s, m_ref, l_ref, hd))
        for hd in range(FOX_HEADS):
            a, p = probs[hd]
            acc_ref[hd] = a * acc_ref[hd] + values(hd, p.astype(BF16))

    attend([jnp.dot(q_ref[0, :, cols[hd]], kc_ref[0, 0, hd].astype(BF16), preferred_element_type=F32)
            for hd in range(FOX_HEADS)],
           fkc_ref[0], lambda hd, p: _dot_nt(p, vc_ref[0, 0, hd].astype(BF16)), None)

    @pl.when(t == pl.num_programs(1) - 1)
    def _():
        causal = (lax.broadcasted_iota(jnp.int32, (n_q, n_q), 1) <= lax.broadcasted_iota(jnp.int32, (n_q, n_q), 0))
        attend([_dot_nt(q_ref[0, :, sl], kn_ref[0, :, sl]) for sl in cols], fkn_ref[0],
               lambda hd, p: jnp.dot(p, vn_ref[0, :, cols[hd]], preferred_element_type=F32), causal)
        for hd in range(FOX_HEADS):
            o_ref[0, :, cols[hd]] = (acc_ref[hd] / l_ref[hd]).astype(o_ref.dtype)


def _fox_decode(q, k_new, v_new, cache_k, cache_v, layer, fcum, tk):
    bsz, n_q, width = q.shape
    n_past = cache_k.shape[2]
    rows_minor = lambda c: c.transpose(0, 1, 3, 4, 2)
    fk_t = fcum.transpose(0, 2, 1)
    row_spec = pl.BlockSpec((1, n_q, width), lambda b, t: (b, 0, 0))
    cache_spec = pl.BlockSpec((1, 1, FOX_HEADS, FOX_DIM, tk), lambda b, t: (layer, b, 0, 0, t))
    return pl.pallas_call(
        _fox_decode_body,
        grid=(bsz, n_past // tk),
        in_specs=[row_spec, pl.BlockSpec((1, n_q, FOX_HEADS), lambda b, t: (b, 0, 0)),
                  cache_spec, cache_spec, pl.BlockSpec((1, FOX_HEADS, tk), lambda b, t: (b, 0, t)),
                  row_spec, row_spec, pl.BlockSpec((1, FOX_HEADS, n_q), lambda b, t: (b, 0, 0))],
        out_specs=row_spec,
        out_shape=jax.ShapeDtypeStruct((bsz, n_q, width), BF16),
        scratch_shapes=[pltpu.VMEM((FOX_HEADS, n_q, 1), F32), pltpu.VMEM((FOX_HEADS, n_q, 1), F32),
                        pltpu.VMEM((FOX_HEADS, n_q, FOX_DIM), F32)],
        compiler_params=_cparams(2, VMEM_LIMIT),
        name="fox_decode",
    )(q, fcum[:, n_past:n_past + n_q], rows_minor(cache_k), rows_minor(cache_v), fk_t[:, :, :n_past],
      k_new, v_new, fk_t[:, :, n_past:n_past + n_q])


def _mla_decode_body(q_ref, wkn_ref, wv_ref, cc_ref, pc_ref, cn_ref, pn_ref, o_ref, qa_ref, qr_ref,
                     m_ref, l_ref, acc_ref):
    t = pl.program_id(1)
    n_q = q_ref.shape[1]

    @pl.when(t == 0)
    def _():
        for hd in range(MLA_HEADS):
            c0 = hd * MLA_QK_PAD
            rows = slice(hd * n_q, (hd + 1) * n_q)
            qa_ref[rows, :] = _dot_nt(q_ref[0, :, c0:c0 + MLA_NOPE],
                                      wkn_ref[:, c0:c0 + MLA_NOPE]).astype(BF16)
            qr_ref[rows, :] = q_ref[0, :, c0 + MLA_NOPE:c0 + MLA_NOPE + MLA_ROPE]
        m_ref[...] = jnp.full(m_ref.shape, NEG, F32)
        l_ref[...] = jnp.zeros(l_ref.shape, F32)
        acc_ref[...] = jnp.zeros(acc_ref.shape, F32)

    def attend(ckv, rope_scores):
        a, p = _online_softmax(_dot_nt(qa_ref[...], ckv) + rope_scores, m_ref, l_ref, 0)
        acc_ref[0] = a * acc_ref[0] + jnp.dot(p.astype(BF16), ckv, preferred_element_type=F32)

    attend(cc_ref[0, 0].astype(BF16),
           jnp.dot(qr_ref[...], pc_ref[0, 0].astype(BF16), preferred_element_type=F32))

    @pl.when(t == pl.num_programs(1) - 1)
    def _():
        attend(cn_ref[0].astype(BF16), _dot_nt(qr_ref[...], pn_ref[0].astype(BF16)))
        lat = (acc_ref[0] / l_ref[0]).astype(BF16)
        for hd in range(MLA_HEADS):
            cols = slice(hd * MLA_V, (hd + 1) * MLA_V)
            o_ref[0, :, cols] = jnp.dot(lat[hd * n_q:(hd + 1) * n_q, :], wv_ref[:, cols],
                                        preferred_element_type=F32).astype(o_ref.dtype)


def _mla_decode(q, ckv_new, kpe_new, cache_ckv, cache_kpe, layer, p, tk):
    bsz, n_q, _ = q.shape
    n_past = cache_ckv.shape[2]
    stacked = MLA_HEADS * n_q
    return pl.pallas_call(
        _mla_decode_body,
        grid=(bsz, n_past // tk),
        in_specs=[pl.BlockSpec((1, n_q, MLA_HEADS * MLA_QK_PAD), lambda b, t: (b, 0, 0)),
                  _const_spec(p["wkn"].shape), _const_spec(p["wv"].shape),
                  pl.BlockSpec((1, 1, tk, MLA_KV_RANK), lambda b, t: (layer, b, t, 0)),
                  pl.BlockSpec((1, 1, MLA_ROPE, tk), lambda b, t: (layer, b, 0, t)),
                  pl.BlockSpec((1, n_q, MLA_KV_RANK), lambda b, t: (b, 0, 0)),
                  pl.BlockSpec((1, n_q, MLA_ROPE), lambda b, t: (b, 0, 0))],
        out_specs=pl.BlockSpec((1, n_q, MLA_HEADS * MLA_V), lambda b, t: (b, 0, 0)),
        out_shape=jax.ShapeDtypeStruct((bsz, n_q, MLA_HEADS * MLA_V), BF16),
        scratch_shapes=[pltpu.VMEM((stacked, MLA_KV_RANK), BF16), pltpu.VMEM((stacked, MLA_ROPE), BF16),
                        pltpu.VMEM((1, stacked, 1), F32), pltpu.VMEM((1, stacked, 1), F32),
                        pltpu.VMEM((1, stacked, MLA_KV_RANK), F32)],
        compiler_params=_cparams(2, VMEM_LIMIT),
        name="mla_decode",
    )(q, p["wkn"], p["wv"], cache_ckv, cache_kpe.transpose(0, 1, 3, 2), ckv_new, kpe_new)


def _gate_body(x_ref, b_ref, lf_ref, fc_ref, *, new_start, first_row):
    n_rows = x_ref.shape[1]
    tri = (lax.broadcasted_iota(jnp.int32, (TILE, TILE), 0)
           >= lax.broadcasted_iota(jnp.int32, (TILE, TILE), 1)).astype(F32)
    carry = jnp.zeros((1, x_ref.shape[2]), F32)
    for i in range(n_rows // TILE):
        sl = slice(i * TILE, (i + 1) * TILE)
        x = x_ref[0, sl, :]
        rows = i * TILE + lax.broadcasted_iota(jnp.int32, (TILE, 1), 0)
        z = x + b_ref[...]
        lf = jnp.where(rows >= new_start, jnp.minimum(z, 0.0) - jnp.log1p(jnp.exp(-jnp.abs(z))), x)
        if first_row:
            lf = jnp.where(rows >= first_row, lf, 0.0)
        lf_ref[0, sl, :] = lf
        cs = jnp.dot(tri, lf, preferred_element_type=F32, precision=lax.Precision.HIGHEST) + carry
        fc_ref[0, sl, :] = cs
        carry = cs[TILE - 1:TILE, :]


def _gate(x, b_f, new_start, first_row):
    bsz, n_rows, heads = x.shape
    spec = pl.BlockSpec((1, n_rows, heads), lambda b: (b, 0, 0))
    return pl.pallas_call(
        functools.partial(_gate_body, new_start=new_start, first_row=first_row),
        grid=(bsz,),
        in_specs=[spec, _const_spec((1, heads))],
        out_specs=[spec, spec],
        out_shape=[jax.ShapeDtypeStruct(x.shape, F32)] * 2,
        compiler_params=_cparams(1),
        name="fox_gate",
    )(x, b_f)


RET_LOG_GAMMA = tuple(math.log(1.0 - 2.0 ** (-5.0 - h)) for h in range(RET_HEADS))


def _ret_body(rq_ref, rk_ref, rv_ref, rg_ref, s0_ref, dec_ref, cq_ref, s1q_ref, s2q_ref, ck_ref, s1k_ref, s2k_ref,
              o_ref, sl_ref, st_ref, *, n_tail):
    c = pl.program_id(1)
    ct = rq_ref.shape[1]

    @pl.when(c == 0)
    def _():
        st_ref[...] = s0_ref[0]

    q = _rope_lanes(rq_ref[0], cq_ref[...], s1q_ref[...], s2q_ref[...], RET_DK // 2)
    k = _rope_lanes(rk_ref[0], ck_ref[...], s1k_ref[...], s2k_ref[...], RET_DK // 2)
    v = rv_ref[0].astype(BF16)
    g = rg_ref[0]
    j = lax.broadcasted_iota(jnp.int32, (ct, 1), 0).astype(F32)
    for hd in range(RET_HEADS):
        lg = RET_LOG_GAMMA[hd]
        q_h = q[:, hd * RET_DK:(hd + 1) * RET_DK]
        k_h = k[:, hd * RET_DK:(hd + 1) * RET_DK]
        v_h = v[:, hd * RET_DV:(hd + 1) * RET_DV]
        scores = lax.dot_general(q_h.astype(BF16), k_h.astype(BF16), (((1,), (1,)), ((), ())),
                                 preferred_element_type=F32) * dec_ref[hd]
        s_h = st_ref[hd]
        out = jnp.dot(scores.astype(BF16), v_h, preferred_element_type=F32)
        out = out + jnp.dot((q_h * jnp.exp(lg * (j + 1.0))).astype(BF16), s_h.astype(BF16),
                            preferred_element_type=F32)
        k_dec = (k_h * jnp.exp(lg * (ct - 1.0 - j))).astype(BF16)
        st_ref[hd] = math.exp(lg * ct) * s_h + lax.dot_general(
            k_dec, v_h, (((0,), (0,)), ((), ())), preferred_element_type=F32)
        mu = jnp.mean(out, axis=-1, keepdims=True)
        oc = out - mu
        var = jnp.mean(oc * oc, axis=-1, keepdims=True)
        g_h = g[:, hd * RET_DV:(hd + 1) * RET_DV]
        o_ref[0, :, hd * RET_DV:(hd + 1) * RET_DV] = (
            g_h * jax.nn.sigmoid(g_h) * (oc * lax.rsqrt(var + EPS))).astype(o_ref.dtype)

    @pl.when(c == pl.num_programs(1) - 1)
    def _():
        for hd in range(RET_HEADS):
            sl_ref[0, hd] = st_ref[hd] * math.exp(-RET_LOG_GAMMA[hd] * n_tail)


def _retention(h, s0, tabs, ct, n_tail):
    bsz, seq, _ = h.shape
    tab_spec = pl.BlockSpec((ct, RET_QK), lambda b, c: (c, 0))
    st_spec = pl.BlockSpec((1, RET_HEADS, RET_DK, RET_DV), lambda b, c: (b, 0, 0, 0))
    rq_blk = (2 * FOX_W) // RET_QK
    rv_blk = (2 * FOX_W + 2 * RET_QK) // RET_VW
    diff = jnp.arange(ct, dtype=F32)[:, None] - jnp.arange(ct, dtype=F32)[None, :]
    decay = jnp.where(diff >= 0.0, jnp.exp(jnp.asarray(RET_LOG_GAMMA, F32)[:, None, None] * jnp.maximum(diff, 0.0)),
                      0.0)
    return pl.pallas_call(
        functools.partial(_ret_body, n_tail=n_tail),
        grid=(bsz, seq // ct),
        in_specs=[
            pl.BlockSpec((1, ct, RET_QK), lambda b, c: (b, c, rq_blk)),
            pl.BlockSpec((1, ct, RET_QK), lambda b, c: (b, c, rq_blk + 1)),
            pl.BlockSpec((1, ct, RET_VW), lambda b, c: (b, c, rv_blk)),
            pl.BlockSpec((1, ct, RET_VW), lambda b, c: (b, c, rv_blk + 1)),
            st_spec, _const_spec((RET_HEADS, ct, ct)),
        ] + [tab_spec] * 6,
        out_specs=[pl.BlockSpec((1, ct, RET_VW), lambda b, c: (b, c, 0)), st_spec],
        out_shape=[jax.ShapeDtypeStruct((bsz, seq, RET_VW), BF16),
                   jax.ShapeDtypeStruct((bsz, RET_HEADS, RET_DK, RET_DV), F32)],
        scratch_shapes=[pltpu.VMEM((RET_HEADS, RET_DK, RET_DV), F32)],
        compiler_params=_cparams(2, VMEM_LIMIT),
        name="retention",
    )(h, h, h, h, s0, decay, *tabs)


def _rope_tables(pos, half, width, group, offset, scale=1.0, valid=None):
    inv = ROPE_BASE ** (-jnp.arange(half, dtype=F32) / half)
    ang = pos.astype(F32)[:, None] * inv[None, :]
    cos, sin = jnp.cos(ang), jnp.sin(ang)
    n = pos.shape[0]
    one, zero = jnp.ones((n, 1), F32), jnp.zeros((n, 1), F32)

    def lanes(first, second, other):
        grp = jnp.concatenate([jnp.broadcast_to(other, (n, offset)), first, second,
                               jnp.broadcast_to(other, (n, group - offset - 2 * half))], axis=1)
        return jnp.tile(grp, (1, width // group))

    tabs = (lanes(cos, cos, one), lanes(-sin, 0.0 * sin, zero), lanes(0.0 * sin, sin, zero))
    if valid is not None:
        tabs = tuple(jnp.where(valid[:, None], t, 0.0) for t in tabs)
    return tuple(t * scale for t in tabs)


def _even_params(e, w):
    p = {}
    p["w_in"] = jnp.pad(w["even_w_in"][e], ((0, 0), (0, EVEN_IN_PAD - EVEN_IN))).astype(BF16)
    p["w_out"] = w["even_w_out"][e].astype(BF16)
    lam_re, lam_im = w["s5_a_re"][e].astype(F32), w["s5_a_im"][e].astype(F32)
    dt = jnp.exp(w["s5_log_dt"][e].astype(F32))[:, None]
    p["lam_dt_re"], p["lam_dt_im"] = lam_re * dt, lam_im * dt
    mag = jnp.exp(lam_re * dt)
    abar_re, abar_im = mag * jnp.cos(lam_im * dt), mag * jnp.sin(lam_im * dt)
    den = lam_re * lam_re + lam_im * lam_im
    f_re = ((abar_re - 1.0) * lam_re + abar_im * lam_im) / den
    f_im = (abar_im * lam_re - (abar_re - 1.0) * lam_im) / den
    b_re, b_im = w["s5_b_re"][e].astype(F32), w["s5_b_im"][e].astype(F32)
    bb_re = f_re[..., None] * b_re - f_im[..., None] * b_im
    bb_im = f_re[..., None] * b_im + f_im[..., None] * b_re
    eye = jnp.eye(S5_GROUPS, dtype=F32)

    def in_blocks(x):
        return jnp.einsum("gnc,gh->gchn", x, eye).reshape(S5_WIDTH, S5_HALF)

    def out_blocks(x):
        return jnp.einsum("gcn,gh->gnhc", x, eye).reshape(S5_HALF, S5_WIDTH)

    p["bbig"] = jnp.concatenate([in_blocks(bb_re), in_blocks(bb_im)], axis=1).astype(BF16)
    p["cbig"] = jnp.concatenate([out_blocks(w["s5_c_re"][e].astype(F32)),
                                 out_blocks(-w["s5_c_im"][e].astype(F32))], axis=0).astype(BF16)
    p["a_re"] = abar_re.reshape(1, S5_HALF)
    p["a_im"] = abar_im.reshape(1, S5_HALF)
    p["d"] = w["s5_d"][e].astype(F32).reshape(1, S5_WIDTH)
    p["w_glu"] = w["s5_w_glu"][e].astype(BF16)
    p["b_glu"] = w["s5_b_glu"][e].astype(F32).reshape(1, S5_WIDTH)
    p["q_norm"] = w["mla_q_norm"][e].astype(F32).reshape(1, MLA_Q_RANK)
    p["kv_norm"] = w["mla_kv_norm"][e].astype(F32).reshape(1, MLA_KV_RANK)
    wq = w["mla_w_uq"][e].reshape(MLA_Q_RANK, MLA_HEADS, MLA_NOPE + MLA_ROPE)
    wq = jnp.pad(wq, ((0, 0), (0, 0), (0, MLA_QK_PAD - MLA_NOPE - MLA_ROPE)))
    p["wq"] = wq.reshape(MLA_Q_RANK, MLA_HEADS * MLA_QK_PAD).astype(BF16)
    wkv = w["mla_w_ukv"][e].reshape(MLA_KV_RANK, MLA_HEADS, MLA_NOPE + MLA_V)
    wkn = jnp.pad(wkv[:, :, :MLA_NOPE], ((0, 0), (0, 0), (0, MLA_QK_PAD - MLA_NOPE)))
    p["wkn"] = wkn.reshape(MLA_KV_RANK, MLA_HEADS * MLA_QK_PAD).astype(BF16)
    place = jnp.pad(jnp.eye(MLA_ROPE, dtype=F32), ((0, 0), (MLA_NOPE, MLA_QK_PAD - MLA_NOPE - MLA_ROPE)))
    p["wkp"] = jnp.tile(place, (1, MLA_HEADS)).astype(BF16)
    p["wv"] = wkv[:, :, MLA_NOPE:].reshape(MLA_KV_RANK, MLA_HEADS * MLA_V).astype(BF16)
    return p


def _odd_params(o, w):
    w_in = w["odd_w_in"][o]
    c_logit = 3 * FOX_W
    cols = jnp.concatenate([
        w_in[:, :FOX_W] * (FOX_DIM ** -0.5),
        w_in[:, FOX_W:c_logit],
        w_in[:, c_logit + FOX_HEADS:],
        w_in[:, c_logit:c_logit + FOX_HEADS],
    ], axis=1)
    p = {"w_in": jnp.pad(cols, ((0, 0), (0, ODD_IN_PAD - cols.shape[1]))).astype(BF16)}
    p["w_out"] = w["odd_w_out"][o].astype(BF16)
    p["b_f"] = w["fox_b_f"][o].astype(F32).reshape(1, FOX_HEADS)
    return p


def _cache_tile(n_past):
    for tk in (512, 256, 128):
        if n_past % tk == 0:
            return tk
    raise ValueError("cache length must be a multiple of 128")


def _trunk(x, pos, n_real, out_rows, past, w, ffn, evens, odds):
    bsz, seq, _ = x.shape
    rows = bsz * seq
    prompt = past is None
    first_row = PAD if prompt else 0
    if prompt:
        tm = 2 * TILE if rows % (2 * TILE) == 0 else TILE
        rt = TILE
    else:
        n_past = past["cache_fox_k"].shape[2]
        tk = _cache_tile(n_past)
        tm, rt = rows, seq
    idx = jnp.arange(seq, dtype=jnp.int32)
    valid = (idx >= first_row) & (idx < n_real)
    st = {n: [] for n in ("mla_ckv", "mla_kpe", "s5_re", "s5_im", "fox_k", "fox_v", "fox_logf", "ret")}
    rq_tabs = _rope_tables(pos, RET_DK // 2, RET_QK, RET_DK, 0)
    rk_tabs = _rope_tables(pos, RET_DK // 2, RET_QK, RET_DK, 0, scale=RET_DK ** -0.5, valid=valid)
    x2 = x.reshape(rows, D_MODEL)

    def ln(l, i):
        return w["ln_g"][l, i].reshape(1, D_MODEL), w["ln_b"][l, i].reshape(1, D_MODEL)

    for l in range(DEPTH):
        x2 = _ffn_ln(x2, ffn, (l, 0), *ln(l, 0), tm)
        if l % 2 == 0:
            e = l // 2
            p = evens[e]
            h = _proj(x2, p["w_in"], tm).reshape(bsz, seq, EVEN_IN_PAD)
            if prompt:
                h0r = h0i = jnp.zeros((bsz, 1, S5_HALF), F32)
            else:
                h0r = past["state_s5_re"][e].astype(F32).reshape(bsz, 1, S5_HALF)
                h0i = past["state_s5_im"][e].astype(F32).reshape(bsz, 1, S5_HALF)
            s5_out, hlr, hli = _s5(h, h0r, h0i, p, rt, first_row, n_real)
            q, ckv, kpe = _mla_rows(h, p, pos, rt, attn_layout=prompt)
            if prompt:
                k_att, v_att = _mla_kv(ckv, kpe, p)
                mla_out = _flash(q, k_att, v_att, None, None, heads=MLA_HEADS, dq=MLA_QK_PAD, dv=MLA_V,
                                 causal=False)
            else:
                mla_out = _mla_decode(q, ckv, kpe, past["cache_mla_ckv"], past["cache_mla_kpe"], e, p, tk)
            mix = (s5_out.reshape(rows, S5_WIDTH), mla_out.reshape(rows, MLA_HEADS * MLA_V), p["w_out"])
            st["mla_ckv"].append(ckv)
            st["mla_kpe"].append(kpe)
            st["s5_re"].append(hlr.reshape(bsz, S5_GROUPS, S5_STATE))
            st["s5_im"].append(hli.reshape(bsz, S5_GROUPS, S5_STATE))
        else:
            o = l // 2
            p = odds[o]
            h, fq, fk16, fv = _proj_odd(x2, p["w_in"], bsz, TILE if prompt else tm, attn_layout=prompt)
            h = h.reshape(bsz, seq, ODD_H)
            fk16 = fk16.reshape(bsz, seq, FOX_W)
            f_logit = h[:, :, ODD_LOGIT_COL:ODD_LOGIT_COL + FOX_HEADS]
            if prompt:
                q_off = 0
                logf, fcum = _gate(f_logit, p["b_f"], 0, PAD)
                fcum = fcum * LOG2E
                fox_out = _flash(fq, fk16, fv, fcum.transpose(0, 2, 1), fcum, heads=FOX_HEADS, dq=FOX_DIM,
                                 dv=FOX_DIM, causal=True)
                s0 = jnp.zeros((bsz, RET_HEADS, RET_DK, RET_DV), F32)
            else:
                q_off = n_past
                gates = jnp.concatenate([past["cache_fox_logf"][o].astype(F32), f_logit], axis=1)
                gates = jnp.pad(gates, ((0, 0), (0, _round_up(n_past + seq, TILE) - n_past - seq), (0, 0)))
                logf, fcum = _gate(gates, p["b_f"], n_past, 0)
                fox_out = _fox_decode(fq.reshape(bsz, seq, FOX_W), fk16, fv.reshape(bsz, seq, FOX_W),
                                      past["cache_fox_k"], past["cache_fox_v"], o, fcum * LOG2E, tk)
                s0 = past["state_ret"][o].astype(F32)
            ret_out, s_last = _retention(h, s0, rq_tabs + rk_tabs, rt, seq - n_real)
            mix = (fox_out.reshape(rows, FOX_W), ret_out.reshape(rows, RET_VW), p["w_out"])
            st["fox_k"].append(h[:, :, :FOX_W].reshape(bsz, seq, FOX_HEADS, FOX_DIM))
            st["fox_v"].append(h[:, :, FOX_W:2 * FOX_W].reshape(bsz, seq, FOX_HEADS, FOX_DIM))
            st["fox_logf"].append(logf[:, q_off:q_off + seq])
            st["ret"].append(s_last)
        keep = (seq,) + out_rows if l == DEPTH - 1 and out_rows != (0, seq) else None
        x2 = _mix_ffn_ln(x2, *mix, ln(l, 1), ffn, (l, 1), ln(l, 2), tm, keep)
    return x2.reshape(bsz, out_rows[1], D_MODEL), {n: jnp.stack(a) for n, a in st.items()}


def kernel(x_prompt, x_sample, cache_mla_ckv, cache_mla_kpe, cache_fox_k, cache_fox_v, cache_fox_logf,
           state_s5_re, state_s5_im, state_ret, meta_tokens, ln_g, ln_b, ffn_w_gate, ffn_w_up, ffn_w_down,
           even_w_in, even_w_out, s5_a_re, s5_a_im, s5_b_re, s5_b_im, s5_c_re, s5_c_im, s5_d, s5_log_dt,
           s5_w_glu, s5_b_glu, mla_q_norm, mla_kv_norm, mla_w_uq, mla_w_ukv, odd_w_in, odd_w_out, fox_b_f):
    w = dict(ln_g=ln_g.astype(F32), ln_b=ln_b.astype(F32), even_w_in=even_w_in, even_w_out=even_w_out,
             s5_a_re=s5_a_re, s5_a_im=s5_a_im, s5_b_re=s5_b_re, s5_b_im=s5_b_im, s5_c_re=s5_c_re,
             s5_c_im=s5_c_im, s5_d=s5_d, s5_log_dt=s5_log_dt, s5_w_glu=s5_w_glu, s5_b_glu=s5_b_glu,
             mla_q_norm=mla_q_norm, mla_kv_norm=mla_kv_norm, mla_w_uq=mla_w_uq, mla_w_ukv=mla_w_ukv,
             odd_w_in=odd_w_in, odd_w_out=odd_w_out, fox_b_f=fox_b_f)
    past = dict(cache_mla_ckv=cache_mla_ckv, cache_mla_kpe=cache_mla_kpe, cache_fox_k=cache_fox_k,
                cache_fox_v=cache_fox_v, cache_fox_logf=cache_fox_logf, state_s5_re=state_s5_re,
                state_s5_im=state_s5_im, state_ret=state_ret)
    ffn = (ffn_w_gate.astype(BF16), ffn_w_up.astype(BF16), ffn_w_down.astype(BF16))
    evens = [_even_params(e, w) for e in range((DEPTH + 1) // 2)]
    odds = [_odd_params(o, w) for o in range(DEPTH // 2)]

    bsz, seq, _ = x_prompt.shape
    n_real = PAD + N_META + seq
    n_rows = _round_up(n_real, TILE)
    meta = jnp.broadcast_to(meta_tokens[None].astype(x_prompt.dtype), (bsz, N_META, D_MODEL))
    xp = jnp.concatenate([jnp.zeros((bsz, PAD, D_MODEL), x_prompt.dtype), meta, x_prompt,
                          jnp.zeros((bsz, n_rows - n_real, D_MODEL), x_prompt.dtype)], axis=1)
    pos_p = jnp.maximum(jnp.arange(n_rows, dtype=jnp.int32) - PAD, 0)
    y_p, st_p = _trunk(xp, pos_p, n_real, (PAD + N_META, seq), None, w, ffn, evens, odds)
    d_seq = x_sample.shape[1]
    pos_s = N_META + cache_fox_k.shape[2] + jnp.arange(d_seq, dtype=jnp.int32)
    y_s, st_s = _trunk(x_sample, pos_s, d_seq, (0, d_seq), past, w, ffn, evens, odds)

    def real(a):
        return a[:, :, PAD:n_real]

    return (y_p, y_s,
            real(st_p["mla_ckv"]), real(st_p["mla_kpe"]), real(st_p["fox_k"]), real(st_p["fox_v"]),
            real(st_p["fox_logf"]), st_p["s5_re"], st_p["s5_im"], st_p["ret"],
            st_s["mla_ckv"], st_s["mla_kpe"], st_s["fox_k"], st_s["fox_v"], st_s["fox_logf"],
            st_s["s5_re"], st_s["s5_im"], st_s["ret"])
```

```python
import functools
import math

import jax
import jax.numpy as jnp
from jax import lax
from jax.experimental import pallas as pl
from jax.experimental.pallas import tpu as pltpu

F32 = jnp.float32
BF16 = jnp.bfloat16

D_MODEL = 1024
DEPTH = 4
CHUNK = 64
CHUNK_SHIFT = 6
N_META = 16
S5_WIDTH = 512
S5_CH = 16
S5_GROUPS = S5_WIDTH // S5_CH
S5_STATE = 64
S5_HALF = S5_GROUPS * S5_STATE
MLA_HEADS = 8
MLA_Q_RANK = 256
MLA_KV_RANK = 128
MLA_NOPE = 64
MLA_ROPE = 32
MLA_V = 64
MLA_QK_PAD = 128
FOX_HEADS = 8
FOX_DIM = 64
FOX_W = FOX_HEADS * FOX_DIM
RET_HEADS = 4
RET_DK = 64
RET_DV = 128
RET_QK = RET_HEADS * RET_DK
RET_VW = RET_HEADS * RET_DV
D_FF = 2816
ROPE_BASE = 10000.0
ALPHA = (2.0 * DEPTH) ** 0.25
EPS = 1e-5
NEG = -1e30
LOG2E = math.log2(math.e)
EVEN_IN = S5_WIDTH + MLA_Q_RANK + MLA_KV_RANK + MLA_ROPE
EVEN_IN_PAD = 1024
ODD_IN_PAD = 3200
ODD_H = ODD_IN_PAD - 3 * FOX_W
ODD_LOGIT_COL = 2 * RET_QK + 2 * RET_VW

PAD = CHUNK - N_META
LANE = 128
SUBLANE = 8
TILE = 256
VMEM_LIMIT = 56 * 1024 * 1024


def _cparams(n_grid, vmem=None):
    return pltpu.CompilerParams(dimension_semantics=("arbitrary",) * n_grid, vmem_limit_bytes=vmem)


def _const_spec(shape):
    nd = len(shape)
    return pl.BlockSpec(shape, lambda *_: (0,) * nd)


def _round_up(n, m):
    return -(-n // m) * m


def _layer_norm_rows(z, g, b):
    mu = jnp.mean(z, axis=-1, keepdims=True)
    zc = z - mu
    var = jnp.mean(zc * zc, axis=-1, keepdims=True)
    return zc * lax.rsqrt(var + EPS) * g + b


def _ffn_rows(x, wg_ref, wu_ref, wd_ref, g_ref, b_ref, hid_ref):
    xb = x.astype(BF16)
    for c in range(D_FF // TILE):
        sl = slice(c * TILE, (c + 1) * TILE)
        hg = jnp.dot(xb, wg_ref[:, sl], preferred_element_type=F32)
        hu = jnp.dot(xb, wu_ref[:, sl], preferred_element_type=F32)
        hid_ref[:, sl] = (hg * jax.nn.sigmoid(hg) * hu).astype(BF16)
    y = jnp.dot(hid_ref[...], wd_ref[...], preferred_element_type=F32)
    return _layer_norm_rows(ALPHA * x + 0.5 * y, g_ref[...], b_ref[...])


def _ffn_body(x_ref, wg_ref, wu_ref, wd_ref, g_ref, b_ref, o_ref, hid_ref):
    o_ref[...] = _ffn_rows(x_ref[...], wg_ref, wu_ref, wd_ref, g_ref, b_ref, hid_ref)


def _ffn_weight_specs(which):
    once = dict(pipeline_mode=pl.Buffered(1))
    pick = lambda i: which + (0, 0)
    return [pl.BlockSpec((None, None, D_MODEL, D_FF), pick, **once),
            pl.BlockSpec((None, None, D_MODEL, D_FF), pick, **once),
            pl.BlockSpec((None, None, D_FF, D_MODEL), pick, **once),
            _const_spec((1, D_MODEL)), _const_spec((1, D_MODEL))]


def _ffn_ln(x, ffn, which, g, b, tm):
    rows = x.shape[0]
    return pl.pallas_call(
        _ffn_body,
        grid=(rows // tm,),
        in_specs=[pl.BlockSpec((tm, D_MODEL), lambda i: (i, 0))] + _ffn_weight_specs(which),
        out_specs=pl.BlockSpec((tm, D_MODEL), lambda i: (i, 0)),
        out_shape=jax.ShapeDtypeStruct((rows, D_MODEL), F32),
        scratch_shapes=[pltpu.VMEM((tm, D_FF), BF16)],
        compiler_params=_cparams(1, VMEM_LIMIT),
        name="ffn_ln",
    )(x, *ffn, g, b)


def _mix_ffn_body(x_ref, a1_ref, a2_ref, wo_ref, g1_ref, b1_ref, wg_ref, wu_ref, wd_ref, g2_ref, b2_ref,
                  o_ref, hid_ref):
    k1 = a1_ref.shape[1]
    y = jnp.dot(a1_ref[...], wo_ref[:k1, :], preferred_element_type=F32)
    y = y + jnp.dot(a2_ref[...], wo_ref[k1:, :], preferred_element_type=F32)
    x1 = _layer_norm_rows(ALPHA * x_ref[...] + y, g1_ref[...], b1_ref[...])
    o_ref[...] = _ffn_rows(x1, wg_ref, wu_ref, wd_ref, g2_ref, b2_ref, hid_ref)


def _mix_ffn_ln(x, a1, a2, w_out, ln1, ffn, which, ln2, tm, keep=None):
    rows = x.shape[0]
    if keep is None:
        n_steps = rows // tm
        row_spec = lambda n: pl.BlockSpec((tm, n), lambda i: (i, 0))
    else:
        seq, start, count = keep
        assert count % tm == 0
        per = count // tm
        n_steps = (rows // seq) * per
        align = math.gcd(seq, start, tm)
        row_spec = lambda n: pl.BlockSpec(
            (pl.Element(tm), pl.Element(n)),
            lambda i: (pl.multiple_of((i // per) * seq + start + (i % per) * tm, align), 0))
    return pl.pallas_call(
        _mix_ffn_body,
        grid=(n_steps,),
        in_specs=[row_spec(D_MODEL), row_spec(a1.shape[1]), row_spec(a2.shape[1]),
                  pl.BlockSpec(w_out.shape, lambda i: (0, 0), pipeline_mode=pl.Buffered(1)),
                  _const_spec((1, D_MODEL)), _const_spec((1, D_MODEL))] + _ffn_weight_specs(which),
        out_specs=pl.BlockSpec((tm, D_MODEL), lambda i: (i, 0)),
        out_shape=jax.ShapeDtypeStruct((n_steps * tm, D_MODEL), F32),
        scratch_shapes=[pltpu.VMEM((tm, D_FF), BF16)],
        compiler_params=_cparams(1, VMEM_LIMIT),
        name="mix_ffn_ln",
    )(x, a1, a2, w_out, *ln1, *ffn, *ln2)


def _proj_body(x_ref, w_ref, o_ref):
    o_ref[...] = jnp.dot(x_ref[...].astype(BF16), w_ref[...], preferred_element_type=F32)


def _proj(x, w, tm):
    rows, n = x.shape[0], w.shape[1]
    return pl.pallas_call(
        _proj_body,
        grid=(rows // tm,),
        in_specs=[pl.BlockSpec((tm, D_MODEL), lambda i: (i, 0)), _const_spec(w.shape)],
        out_specs=pl.BlockSpec((tm, n), lambda i: (i, 0)),
        out_shape=jax.ShapeDtypeStruct((rows, n), F32),
        compiler_params=_cparams(1, VMEM_LIMIT),
        name="in_proj_even",
    )(x, w)


def _proj_odd_body(x_ref, w_ref, h_ref, k32_ref, v32_ref, q_ref, k_ref, v_ref, *, attn_layout):
    y = jnp.dot(x_ref[...].astype(BF16), w_ref[...], preferred_element_type=F32)
    h_ref[...] = y[:, 3 * FOX_W:]
    q = y[:, :FOX_W] * LOG2E
    k, v = y[:, FOX_W:2 * FOX_W], y[:, 2 * FOX_W:3 * FOX_W]
    k_ref[...] = k.astype(BF16)
    if attn_layout:
        q_ref[0] = q.T.astype(BF16)
        k_t, v_t = k.T, v.T
        for hd in range(FOX_HEADS):
            rows = slice(hd * FOX_DIM, (hd + 1) * FOX_DIM)
            k32_ref[0, hd] = k_t[rows, :]
            v32_ref[0, hd] = v_t[rows, :]
            v_ref[0, hd, 0] = v_t[rows, :].astype(BF16)
    else:
        k32_ref[...] = k
        v32_ref[...] = v
        q_ref[...] = q.astype(BF16)
        v_ref[...] = v.astype(BF16)


def _proj_odd(x, w, bsz, tm, attn_layout):
    rows = x.shape[0]
    seq = rows // bsz
    row_spec = lambda n: pl.BlockSpec((tm, n), lambda i: (i, 0))
    if attn_layout:
        assert tm == TILE and seq % TILE == 0
        n_t = seq // TILE
        kv32_shape = jax.ShapeDtypeStruct((bsz, FOX_HEADS, FOX_DIM, seq), F32)
        kv32_spec = pl.BlockSpec((1, FOX_HEADS, FOX_DIM, TILE), lambda i: (i // n_t, 0, 0, i % n_t))
        q_shape = jax.ShapeDtypeStruct((bsz, FOX_W, seq), BF16)
        q_spec = pl.BlockSpec((1, FOX_W, TILE), lambda i: (i // n_t, 0, i % n_t))
        v_shape = jax.ShapeDtypeStruct((bsz, FOX_HEADS, n_t, FOX_DIM, TILE), BF16)
        v_spec = pl.BlockSpec((1, FOX_HEADS, 1, FOX_DIM, TILE), lambda i: (i // n_t, 0, i % n_t, 0, 0))
    else:
        kv32_shape, kv32_spec = jax.ShapeDtypeStruct((rows, FOX_W), F32), row_spec(FOX_W)
        q_shape = v_shape = jax.ShapeDtypeStruct((rows, FOX_W), BF16)
        q_spec = v_spec = row_spec(FOX_W)
    return pl.pallas_call(
        functools.partial(_proj_odd_body, attn_layout=attn_layout),
        grid=(rows // tm,),
        in_specs=[row_spec(D_MODEL), _const_spec(w.shape)],
        out_specs=[row_spec(ODD_H), kv32_spec, kv32_spec, q_spec, row_spec(FOX_W), v_spec],
        out_shape=[jax.ShapeDtypeStruct((rows, ODD_H), F32), kv32_shape, kv32_shape, q_shape,
                   jax.ShapeDtypeStruct((rows, FOX_W), BF16), v_shape],
        compiler_params=_cparams(1, VMEM_LIMIT),
        name="in_proj_odd",
    )(x, w)


def _s5_body(u_ref, h0r_ref, h0i_ref, ar_ref, ai_ref, asr_ref, asi_ref, bbig_ref, cbig_ref, d_ref, wglu_ref,
             bglu_ref, o_ref, hlr_ref, hli_ref, hs_ref, st_ref, *, first_row, last_seg):
    t = pl.program_id(1)
    rt = u_ref.shape[1]
    seg = rt // SUBLANE

    @pl.when(t == 0)
    def _():
        st_ref[0:1, :] = h0r_ref[0]
        st_ref[1:2, :] = h0i_ref[0]

    u = u_ref[0]
    if first_row:
        rows = t * rt + lax.broadcasted_iota(jnp.int32, (rt, 1), 0)
        u = jnp.where(rows >= first_row, u, 0.0)
    i0 = lax.broadcasted_iota(jnp.int32, (rt, rt), 0)
    i1 = lax.broadcasted_iota(jnp.int32, (rt, rt), 1)
    regroup = (i1 == (i0 & (SUBLANE - 1)) * seg + (i0 >> 3)).astype(BF16)
    restore = (i0 == (i1 & (SUBLANE - 1)) * seg + (i1 >> 3)).astype(BF16)
    ub = jnp.dot(regroup, u.astype(BF16), preferred_element_type=F32).astype(BF16)

    half_w, half_s = S5_WIDTH // 2, S5_HALF // 2
    for kb in range(2):
        for part in range(2):
            c0 = part * S5_HALF + kb * half_s
            hs_ref[:, c0:c0 + half_s] = jnp.dot(
                ub[:, kb * half_w:(kb + 1) * half_w], bbig_ref[kb * half_w:(kb + 1) * half_w, c0:c0 + half_s],
                preferred_element_type=F32)

    def scan(lanes, start, store):
        re_l = lanes
        im_l = slice(S5_HALF + lanes.start, S5_HALF + lanes.stop)
        ar = jnp.broadcast_to(ar_ref[:, lanes], (SUBLANE, lanes.stop - lanes.start))
        ai = jnp.broadcast_to(ai_ref[:, lanes], (SUBLANE, lanes.stop - lanes.start))

        def step(k, carry):
            hr, hi = carry
            base = pl.multiple_of(k * SUBLANE, SUBLANE)
            nr = ar * hr - ai * hi + hs_ref[pl.ds(base, SUBLANE), re_l]
            ni = ar * hi + ai * hr + hs_ref[pl.ds(base, SUBLANE), im_l]
            if store:
                hs_ref[pl.ds(base, SUBLANE), re_l] = nr
                hs_ref[pl.ds(base, SUBLANE), im_l] = ni
            return nr, ni

        return lax.fori_loop(0, seg, step, start)

    slabs = [slice(c * half_s, (c + 1) * half_s) for c in range(2)]
    zero = jnp.zeros((SUBLANE, half_s), F32)
    ends = [scan(lanes, (zero, zero), False) for lanes in slabs]
    er = jnp.concatenate([e[0] for e in ends], axis=1)
    ei = jnp.concatenate([e[1] for e in ends], axis=1)
    asr, asi = asr_ref[...], asi_ref[...]
    sr, si = st_ref[0:1, :], st_ref[1:2, :]
    start_r, start_i = [], []
    for j in range(SUBLANE):
        start_r.append(sr)
        start_i.append(si)
        sr, si = asr * sr - asi * si + er[j:j + 1], asr * si + asi * sr + ei[j:j + 1]
    st_ref[0:1, :] = sr
    st_ref[1:2, :] = si
    start_r8, start_i8 = jnp.concatenate(start_r, axis=0), jnp.concatenate(start_i, axis=0)
    for lanes in slabs:
        scan(lanes, (start_r8[:, lanes], start_i8[:, lanes]), True)

    hb = hs_ref[...].astype(BF16)
    ys = []
    for kb in range(2):
        re_rows = slice(kb * half_s, (kb + 1) * half_s)
        im_rows = slice(S5_HALF + kb * half_s, S5_HALF + (kb + 1) * half_s)
        cols = slice(kb * half_w, (kb + 1) * half_w)
        ys.append(jnp.dot(hb[:, re_rows], cbig_ref[re_rows, cols], preferred_element_type=F32)
                  + jnp.dot(hb[:, im_rows], cbig_ref[im_rows, cols], preferred_element_type=F32))
    y = jnp.concatenate(ys, axis=1)
    y_hi = y.astype(BF16)
    y_lo = (y - y_hi.astype(F32)).astype(BF16)
    y = (jnp.dot(restore, y_hi, preferred_element_type=F32) + jnp.dot(restore, y_lo, preferred_element_type=F32)
         + d_ref[...] * u)
    g = jax.nn.gelu(y)
    gate = jnp.dot(g.astype(BF16), wglu_ref[...], preferred_element_type=F32) + bglu_ref[...]
    o_ref[0] = (g * jax.nn.sigmoid(gate)).astype(o_ref.dtype)

    @pl.when(t == pl.num_programs(1) - 1)
    def _():
        hlr_ref[0] = sr if last_seg == SUBLANE else start_r[last_seg]
        hli_ref[0] = si if last_seg == SUBLANE else start_i[last_seg]


def _s5(h, h0r, h0i, p, rt, first_row, n_real):
    bsz, seq, _ = h.shape
    seg = rt // SUBLANE
    real_in_last = n_real - (seq - rt)
    assert 0 < real_in_last <= rt and real_in_last % seg == 0
    as_re = (jnp.exp(p["lam_dt_re"] * seg) * jnp.cos(p["lam_dt_im"] * seg)).reshape(1, S5_HALF)
    as_im = (jnp.exp(p["lam_dt_re"] * seg) * jnp.sin(p["lam_dt_im"] * seg)).reshape(1, S5_HALF)
    state_spec = pl.BlockSpec((1, 1, S5_HALF), lambda b, t: (b, 0, 0))
    return pl.pallas_call(
        functools.partial(_s5_body, first_row=first_row, last_seg=real_in_last // seg),
        grid=(bsz, seq // rt),
        in_specs=[
            pl.BlockSpec((1, rt, S5_WIDTH), lambda b, t: (b, t, 0)),
            state_spec, state_spec,
            _const_spec((1, S5_HALF)), _const_spec((1, S5_HALF)), _const_spec((1, S5_HALF)), _const_spec((1, S5_HALF)),
            _const_spec((S5_WIDTH, 2 * S5_HALF)), _const_spec((2 * S5_HALF, S5_WIDTH)),
            _const_spec((1, S5_WIDTH)), _const_spec((S5_WIDTH, S5_WIDTH)), _const_spec((1, S5_WIDTH)),
        ],
        out_specs=[pl.BlockSpec((1, rt, S5_WIDTH), lambda b, t: (b, t, 0)), state_spec, state_spec],
        out_shape=[jax.ShapeDtypeStruct((bsz, seq, S5_WIDTH), BF16),
                   jax.ShapeDtypeStruct((bsz, 1, S5_HALF), F32),
                   jax.ShapeDtypeStruct((bsz, 1, S5_HALF), F32)],
        scratch_shapes=[pltpu.VMEM((rt, 2 * S5_HALF), F32), pltpu.VMEM((2, S5_HALF), F32)],
        compiler_params=_cparams(2, VMEM_LIMIT),
        name="s5",
    )(h, h0r, h0i, p["a_re"], p["a_im"], as_re, as_im, p["bbig"], p["cbig"], p["d"], p["w_glu"], p["b_glu"])


def _rope_lanes(x, c, s1, s2, half):
    n = x.shape[-1]
    return x * c + pltpu.roll(x, n - half, 1) * s1 + pltpu.roll(x, half, 1) * s2


def _mla_rows_body(h_ref, qn_ref, kn_ref, wq_ref, *refs, attn_layout, scale):
    q_tabs, (ck_ref, s1k_ref, s2k_ref, q_ref, ckv_ref, kpe_ref) = refs[:-6], refs[-6:]
    h = h_ref[0]
    q_lat = h[:, :MLA_Q_RANK]
    q_lat = (q_lat * lax.rsqrt(jnp.mean(q_lat * q_lat, axis=-1, keepdims=True) + EPS) * qn_ref[...]).astype(BF16)
    half = MLA_ROPE // 2
    if attn_layout:
        cos_t, sin_t = q_tabs[0][...], q_tabs[1][...]
        q_t = _dot_nt(wq_ref[...], q_lat)
        for hd in range(MLA_HEADS):
            r0 = hd * MLA_QK_PAD
            x1 = q_t[r0 + MLA_NOPE:r0 + MLA_NOPE + half, :]
            x2 = q_t[r0 + MLA_NOPE + half:r0 + MLA_NOPE + 2 * half, :]
            q_ref[0, r0:r0 + MLA_QK_PAD, :] = jnp.concatenate(
                [q_t[r0:r0 + MLA_NOPE, :] * scale, x1 * cos_t - x2 * sin_t, x1 * sin_t + x2 * cos_t,
                 q_t[r0 + MLA_NOPE + 2 * half:r0 + MLA_QK_PAD, :]], axis=0).astype(BF16)
    else:
        cq, s1q, s2q = (t[...] for t in q_tabs)
        q = jnp.dot(q_lat, wq_ref[...], preferred_element_type=F32)
        for hd in range(MLA_HEADS):
            sl = slice(hd * MLA_QK_PAD, (hd + 1) * MLA_QK_PAD)
            q_ref[0, :, sl] = _rope_lanes(q[:, sl], cq, s1q, s2q, half).astype(BF16)
    c_kv = h[:, MLA_Q_RANK:MLA_Q_RANK + MLA_KV_RANK]
    ckv_ref[0] = c_kv * lax.rsqrt(jnp.mean(c_kv * c_kv, axis=-1, keepdims=True) + EPS) * kn_ref[...]
    k_pe = h[:, MLA_Q_RANK + MLA_KV_RANK:]
    kpe_ref[0] = _rope_lanes(k_pe, ck_ref[...], s1k_ref[...], s2k_ref[...], MLA_ROPE // 2)[:, :MLA_ROPE]


def _mla_rows(h, p, pos, tm, attn_layout):
    bsz, seq, _ = h.shape
    tab_spec = pl.BlockSpec((tm, LANE), lambda b, t: (t, 0))
    n_q = MLA_HEADS * MLA_QK_PAD
    half = MLA_ROPE // 2
    scale = (MLA_NOPE + MLA_ROPE) ** -0.5 * LOG2E
    k_tabs = _rope_tables(pos, half, LANE, LANE, 0)
    if attn_layout:
        q_shape, q_spec = (bsz, n_q, seq), pl.BlockSpec((1, n_q, tm), lambda b, t: (b, 0, t))
        ang = (ROPE_BASE ** (-jnp.arange(half, dtype=F32) / half))[:, None] * pos.astype(F32)[None, :]
        q_tabs = (jnp.cos(ang) * scale, jnp.sin(ang) * scale)
        q_tab_specs = [pl.BlockSpec((half, tm), lambda b, t: (0, t))] * 2
        wq = p["wq"].T
    else:
        q_shape, q_spec = (bsz, seq, n_q), pl.BlockSpec((1, tm, n_q), lambda b, t: (b, t, 0))
        q_tabs = _rope_tables(pos, half, MLA_QK_PAD, MLA_QK_PAD, MLA_NOPE, scale=scale)
        q_tab_specs = [tab_spec] * 3
        wq = p["wq"]
    return pl.pallas_call(
        functools.partial(_mla_rows_body, attn_layout=attn_layout, scale=scale),
        grid=(bsz, seq // tm),
        in_specs=[
            pl.BlockSpec((1, tm, EVEN_IN_PAD - S5_WIDTH), lambda b, t: (b, t, 1)),
            _const_spec((1, MLA_Q_RANK)), _const_spec((1, MLA_KV_RANK)),
            _const_spec(wq.shape),
        ] + q_tab_specs + [tab_spec] * 3,
        out_specs=[
            q_spec,
            pl.BlockSpec((1, tm, MLA_KV_RANK), lambda b, t: (b, t, 0)),
            pl.BlockSpec((1, tm, MLA_ROPE), lambda b, t: (b, t, 0)),
        ],
        out_shape=[jax.ShapeDtypeStruct(q_shape, BF16),
                   jax.ShapeDtypeStruct((bsz, seq, MLA_KV_RANK), F32),
                   jax.ShapeDtypeStruct((bsz, seq, MLA_ROPE), F32)],
        compiler_params=_cparams(2, VMEM_LIMIT),
        name="mla_rows",
    )(h, p["q_norm"], p["kv_norm"], wq, *q_tabs, *k_tabs)


def _mla_kv_body(ckv_ref, kpe_ref, wkn_ref, wkp_ref, wv_ref, k_ref, v_ref):
    ckv = ckv_ref[0].astype(BF16)
    k = jnp.dot(ckv, wkn_ref[...], preferred_element_type=F32)
    k = k + jnp.dot(kpe_ref[0].astype(BF16), wkp_ref[...], preferred_element_type=F32)
    k_ref[0] = k.astype(BF16)
    v_t = jnp.dot(ckv, wv_ref[...], preferred_element_type=F32).T
    for hd in range(MLA_HEADS):
        v_ref[0, hd, 0] = v_t[hd * MLA_V:(hd + 1) * MLA_V, :].astype(BF16)


def _mla_kv(ckv, kpe, p):
    bsz, seq, _ = ckv.shape
    n_t = seq // TILE
    return pl.pallas_call(
        _mla_kv_body,
        grid=(bsz, n_t),
        in_specs=[
            pl.BlockSpec((1, TILE, MLA_KV_RANK), lambda b, t: (b, t, 0)),
            pl.BlockSpec((1, TILE, MLA_ROPE), lambda b, t: (b, t, 0)),
            _const_spec(p["wkn"].shape), _const_spec(p["wkp"].shape), _const_spec(p["wv"].shape),
        ],
        out_specs=[pl.BlockSpec((1, TILE, MLA_HEADS * MLA_QK_PAD), lambda b, t: (b, t, 0)),
                   pl.BlockSpec((1, MLA_HEADS, 1, MLA_V, TILE), lambda b, t: (b, 0, t, 0, 0))],
        out_shape=[jax.ShapeDtypeStruct((bsz, seq, MLA_HEADS * MLA_QK_PAD), BF16),
                   jax.ShapeDtypeStruct((bsz, MLA_HEADS, n_t, MLA_V, TILE), BF16)],
        compiler_params=_cparams(2, VMEM_LIMIT),
        name="mla_kv",
    )(ckv, kpe, p["wkn"], p["wkp"], p["wv"])


def _flash_body(*refs, heads, dq, dv, causal):
    if causal:
        qt_ref, k_ref, vt_ref, fq_ref, fk_ref, o_ref = refs
    else:
        qt_ref, k_ref, vt_ref, o_ref = refs
    tq = qt_ref.shape[2]
    n_tiles, tk = vt_ref.shape[2], vt_ref.shape[4]
    r0 = pl.program_id(1) * tq
    j_last = jnp.minimum((r0 + tq - 1) // tk, n_tiles - 1)
    q_row = r0 + lax.broadcasted_iota(jnp.int32, (1, tq), 1)
    last = q_row if causal else ((q_row >> CHUNK_SHIFT) << CHUNK_SHIFT) + (CHUNK - 1)
    first_last = r0 if causal else ((r0 >> CHUNK_SHIFT) << CHUNK_SHIFT) + (CHUNK - 1)
    n_open = jnp.clip((first_last + 1) // tk, 1, j_last + 1)
    q_t = [qt_ref[0, hd * dq:(hd + 1) * dq, :] for hd in range(heads)]

    def tile(j, carry, masked, wide):
        width = 2 * tk if wide else tk
        start = pl.multiple_of(j * tk, tk)
        if masked:
            k_row = start + lax.broadcasted_iota(jnp.int32, (width, 1), 0)
            ok = (k_row <= last) & (k_row >= PAD)
        scores = [jnp.dot(k_ref[0, pl.ds(start, width), hd * dq:(hd + 1) * dq], q_t[hd],
                          preferred_element_type=F32) for hd in range(heads)]
        probs = []
        for hd in range(heads):
            m, l, _ = carry[hd]
            s = scores[hd]
            if causal:
                s = s + (fq_ref[0, hd:hd + 1, :] - fk_ref[0, pl.ds(start, width), hd:hd + 1])
            if masked:
                s = jnp.where(ok, s, NEG)
            m_new = jnp.maximum(m, jnp.max(s, axis=0, keepdims=True))
            a = jnp.exp2(m - m_new)
            p = jnp.exp2(s - m_new)
            probs.append((m_new, a * l + jnp.sum(p, axis=0, keepdims=True), a, p.astype(BF16)))
        new = []
        for hd in range(heads):
            m_new, l, a, p = probs[hd]
            acc = a * carry[hd][2] + jnp.dot(vt_ref[0, hd, j], p[:tk], preferred_element_type=F32)
            if wide:
                acc = acc + jnp.dot(vt_ref[0, hd, j + 1], p[tk:], preferred_element_type=F32)
            new.append((m_new, l, acc))
        return tuple(new)

    init = tuple((jnp.full((1, tq), NEG, F32), jnp.zeros((1, tq), F32), jnp.zeros((dv, tq), F32))
                 for _ in range(heads))
    carry = tile(0, init, True, False)
    n_wide = (n_open - 1) // 2
    carry = lax.fori_loop(0, n_wide, lambda i, c: tile(1 + 2 * i, c, False, True), carry)
    carry = lax.fori_loop(1 + 2 * n_wide, n_open, functools.partial(tile, masked=False, wide=False), carry)
    carry = lax.fori_loop(n_open, j_last + 1, functools.partial(tile, masked=True, wide=False), carry)
    outs = [carry[hd][2] / carry[hd][1] for hd in range(heads)]
    per = LANE // dv
    for g in range(heads // per):
        o_ref[0, :, g * LANE:(g + 1) * LANE] = jnp.concatenate(
            outs[g * per:(g + 1) * per], axis=0).T.astype(o_ref.dtype)


def _flash(q_t, k, v_t, fq_t, fk, *, heads, dq, dv, causal):
    bsz, _, n_q = q_t.shape
    n_keys = k.shape[1]
    in_specs = [
        pl.BlockSpec((1, heads * dq, TILE), lambda b, i: (b, 0, i)),
        pl.BlockSpec((1, n_keys, heads * dq), lambda b, i: (b, 0, 0)),
        pl.BlockSpec((1,) + v_t.shape[1:], lambda b, i: (b, 0, 0, 0, 0)),
    ]
    args = [q_t, k, v_t]
    if causal:
        in_specs += [pl.BlockSpec((1, heads, TILE), lambda b, i: (b, 0, i)),
                     pl.BlockSpec((1, n_keys, heads), lambda b, i: (b, 0, 0))]
        args += [fq_t, fk]
    return pl.pallas_call(
        functools.partial(_flash_body, heads=heads, dq=dq, dv=dv, causal=causal),
        grid=(bsz, n_q // TILE),
        in_specs=in_specs,
        out_specs=pl.BlockSpec((1, TILE, heads * dv), lambda b, i: (b, i, 0)),
        out_shape=jax.ShapeDtypeStruct((bsz, n_q, heads * dv), BF16),
        compiler_params=_cparams(2, VMEM_LIMIT),
        name="flash_causal" if causal else "flash_chunk",
    )(*args)


def _online_softmax(s, m_ref, l_ref, idx):
    m_old = m_ref[idx]
    m_new = jnp.maximum(m_old, jnp.max(s, axis=-1, keepdims=True))
    a = jnp.exp2(m_old - m_new)
    p = jnp.exp2(s - m_new)
    m_ref[idx] = m_new
    l_ref[idx] = a * l_ref[idx] + jnp.sum(p, axis=-1, keepdims=True)
    return a, p


def _dot_nt(a, b):
    return lax.dot_general(a, b, (((1,), (1,)), ((), ())), preferred_element_type=F32)


def _fox_decode_body(q_ref, fq_ref, kc_ref, vc_ref, fkc_ref, kn_ref, vn_ref, fkn_ref, o_ref, m_ref, l_ref, acc_ref):
    t = pl.program_id(1)
    n_q = q_ref.shape[1]
    cols = [slice(hd * FOX_DIM, (hd + 1) * FOX_DIM) for hd in range(FOX_HEADS)]

    @pl.when(t == 0)
    def _():
        m_ref[...] = jnp.full(m_ref.shape, NEG, F32)
        l_ref[...] = jnp.zeros(l_ref.shape, F32)
        acc_ref[...] = jnp.zeros(acc_ref.shape, F32)

    def attend(scores, fk_rows, values, ok):
        probs = []
        for hd in range(FOX_HEADS):
            s = scores[hd] + (fq_ref[0, :, hd:hd + 1] - fk_rows[hd:hd + 1, :])
            if ok is not None:
                s = jnp.where(ok, s, NEG)
            probs.append(_online_softmax(s, m_ref, l_ref, hd))
        for hd in range(FOX_HEADS):
            a, p = probs[hd]
            acc_ref[hd] = a * acc_ref[hd] + values(hd, p.astype(BF16))

    attend([jnp.dot(q_ref[0, :, cols[hd]], kc_ref[0, 0, hd].astype(BF16), preferred_element_type=F32)
            for hd in range(FOX_HEADS)],
           fkc_ref[0], lambda hd, p: _dot_nt(p, vc_ref[0, 0, hd].astype(BF16)), None)

    @pl.when(t == pl.num_programs(1) - 1)
    def _():
        causal = (lax.broadcasted_iota(jnp.int32, (n_q, n_q), 1) <= lax.broadcasted_iota(jnp.int32, (n_q, n_q), 0))
        attend([_dot_nt(q_ref[0, :, sl], kn_ref[0, :, sl]) for sl in cols], fkn_ref[0],
               lambda hd, p: jnp.dot(p, vn_ref[0, :, cols[hd]], preferred_element_type=F32), causal)
        for hd in range(FOX_HEADS):
            o_ref[0, :, cols[hd]] = (acc_ref[hd] / l_ref[hd]).astype(o_ref.dtype)


def _fox_decode(q, k_new, v_new, cache_k, cache_v, layer, fcum, tk):
    bsz, n_q, width = q.shape
    n_past = cache_k.shape[2]
    rows_minor = lambda c: c.transpose(0, 1, 3, 4, 2)
    fk_t = fcum.transpose(0, 2, 1)
    row_spec = pl.BlockSpec((1, n_q, width), lambda b, t: (b, 0, 0))
    cache_spec = pl.BlockSpec((1, 1, FOX_HEADS, FOX_DIM, tk), lambda b, t: (layer, b, 0, 0, t))
    return pl.pallas_call(
        _fox_decode_body,
        grid=(bsz, n_past // tk),
        in_specs=[row_spec, pl.BlockSpec((1, n_q, FOX_HEADS), lambda b, t: (b, 0, 0)),
                  cache_spec, cache_spec, pl.BlockSpec((1, FOX_HEADS, tk), lambda b, t: (b, 0, t)),
                  row_spec, row_spec, pl.BlockSpec((1, FOX_HEADS, n_q), lambda b, t: (b, 0, 0))],
        out_specs=row_spec,
        out_shape=jax.ShapeDtypeStruct((bsz, n_q, width), BF16),
        scratch_shapes=[pltpu.VMEM((FOX_HEADS, n_q, 1), F32), pltpu.VMEM((FOX_HEADS, n_q, 1), F32),
                        pltpu.VMEM((FOX_HEADS, n_q, FOX_DIM), F32)],
        compiler_params=_cparams(2, VMEM_LIMIT),
        name="fox_decode",
    )(q, fcum[:, n_past:n_past + n_q], rows_minor(cache_k), rows_minor(cache_v), fk_t[:, :, :n_past],
      k_new, v_new, fk_t[:, :, n_past:n_past + n_q])


def _mla_decode_body(q_ref, wkn_ref, wv_ref, cc_ref, pc_ref, cn_ref, pn_ref, o_ref, qa_ref, qr_ref,
                     m_ref, l_ref, acc_ref):
    t = pl.program_id(1)
    n_q = q_ref.shape[1]

    @pl.when(t == 0)
    def _():
        for hd in range(MLA_HEADS):
            c0 = hd * MLA_QK_PAD
            rows = slice(hd * n_q, (hd + 1) * n_q)
            qa_ref[rows, :] = _dot_nt(q_ref[0, :, c0:c0 + MLA_NOPE],
                                      wkn_ref[:, c0:c0 + MLA_NOPE]).astype(BF16)
            qr_ref[rows, :] = q_ref[0, :, c0 + MLA_NOPE:c0 + MLA_NOPE + MLA_ROPE]
        m_ref[...] = jnp.full(m_ref.shape, NEG, F32)
        l_ref[...] = jnp.zeros(l_ref.shape, F32)
        acc_ref[...] = jnp.zeros(acc_ref.shape, F32)

    def attend(ckv, rope_scores):
        a, p = _online_softmax(_dot_nt(qa_ref[...], ckv) + rope_scores, m_ref, l_ref, 0)
        acc_ref[0] = a * acc_ref[0] + jnp.dot(p.astype(BF16), ckv, preferred_element_type=F32)

    attend(cc_ref[0, 0].astype(BF16),
           jnp.dot(qr_ref[...], pc_ref[0, 0].astype(BF16), preferred_element_type=F32))

    @pl.when(t == pl.num_programs(1) - 1)
    def _():
        attend(cn_ref[0].astype(BF16), _dot_nt(qr_ref[...], pn_ref[0].astype(BF16)))
        lat = (acc_ref[0] / l_ref[0]).astype(BF16)
        for hd in range(MLA_HEADS):
            cols = slice(hd * MLA_V, (hd + 1) * MLA_V)
            o_ref[0, :, cols] = jnp.dot(lat[hd * n_q:(hd + 1) * n_q, :], wv_ref[:, cols],
                                        preferred_element_type=F32).astype(o_ref.dtype)


def _mla_decode(q, ckv_new, kpe_new, cache_ckv, cache_kpe, layer, p, tk):
    bsz, n_q, _ = q.shape
    n_past = cache_ckv.shape[2]
    stacked = MLA_HEADS * n_q
    return pl.pallas_call(
        _mla_decode_body,
        grid=(bsz, n_past // tk),
        in_specs=[pl.BlockSpec((1, n_q, MLA_HEADS * MLA_QK_PAD), lambda b, t: (b, 0, 0)),
                  _const_spec(p["wkn"].shape), _const_spec(p["wv"].shape),
                  pl.BlockSpec((1, 1, tk, MLA_KV_RANK), lambda b, t: (layer, b, t, 0)),
                  pl.BlockSpec((1, 1, MLA_ROPE, tk), lambda b, t: (layer, b, 0, t)),
                  pl.BlockSpec((1, n_q, MLA_KV_RANK), lambda b, t: (b, 0, 0)),
                  pl.BlockSpec((1, n_q, MLA_ROPE), lambda b, t: (b, 0, 0))],
        out_specs=pl.BlockSpec((1, n_q, MLA_HEADS * MLA_V), lambda b, t: (b, 0, 0)),
        out_shape=jax.ShapeDtypeStruct((bsz, n_q, MLA_HEADS * MLA_V), BF16),
        scratch_shapes=[pltpu.VMEM((stacked, MLA_KV_RANK), BF16), pltpu.VMEM((stacked, MLA_ROPE), BF16),
                        pltpu.VMEM((1, stacked, 1), F32), pltpu.VMEM((1, stacked, 1), F32),
                        pltpu.VMEM((1, stacked, MLA_KV_RANK), F32)],
        compiler_params=_cparams(2, VMEM_LIMIT),
        name="mla_decode",
    )(q, p["wkn"], p["wv"], cache_ckv, cache_kpe.transpose(0, 1, 3, 2), ckv_new, kpe_new)


def _gate_body(x_ref, b_ref, lf_ref, fc_ref, *, new_start, first_row):
    n_rows = x_ref.shape[1]
    tri = (lax.broadcasted_iota(jnp.int32, (TILE, TILE), 0)
           >= lax.broadcasted_iota(jnp.int32, (TILE, TILE), 1)).astype(F32)
    carry = jnp.zeros((1, x_ref.shape[2]), F32)
    for i in range(n_rows // TILE):
        sl = slice(i * TILE, (i + 1) * TILE)
        x = x_ref[0, sl, :]
        rows = i * TILE + lax.broadcasted_iota(jnp.int32, (TILE, 1), 0)
        z = x + b_ref[...]
        lf = jnp.where(rows >= new_start, jnp.minimum(z, 0.0) - jnp.log1p(jnp.exp(-jnp.abs(z))), x)
        if first_row:
            lf = jnp.where(rows >= first_row, lf, 0.0)
        lf_ref[0, sl, :] = lf
        cs = jnp.dot(tri, lf, preferred_element_type=F32, precision=lax.Precision.HIGHEST) + carry
        fc_ref[0, sl, :] = cs
        carry = cs[TILE - 1:TILE, :]


def _gate(x, b_f, new_start, first_row):
    bsz, n_rows, heads = x.shape
    spec = pl.BlockSpec((1, n_rows, heads), lambda b: (b, 0, 0))
    return pl.pallas_call(
        functools.partial(_gate_body, new_start=new_start, first_row=first_row),
        grid=(bsz,),
        in_specs=[spec, _const_spec((1, heads))],
        out_specs=[spec, spec],
        out_shape=[jax.ShapeDtypeStruct(x.shape, F32)] * 2,
        compiler_params=_cparams(1),
        name="fox_gate",
    )(x, b_f)


RET_LOG_GAMMA = tuple(math.log(1.0 - 2.0 ** (-5.0 - h)) for h in range(RET_HEADS))


def _ret_body(rq_ref, rk_ref, rv_ref, rg_ref, s0_ref, dec_ref, cq_ref, s1q_ref, s2q_ref, ck_ref, s1k_ref, s2k_ref,
              o_ref, sl_ref, st_ref, *, n_tail):
    c = pl.program_id(1)
    ct = rq_ref.shape[1]

    @pl.when(c == 0)
    def _():
        st_ref[...] = s0_ref[0]

    q = _rope_lanes(rq_ref[0], cq_ref[...], s1q_ref[...], s2q_ref[...], RET_DK // 2)
    k = _rope_lanes(rk_ref[0], ck_ref[...], s1k_ref[...], s2k_ref[...], RET_DK // 2)
    v = rv_ref[0].astype(BF16)
    g = rg_ref[0]
    j = lax.broadcasted_iota(jnp.int32, (ct, 1), 0).astype(F32)
    for hd in range(RET_HEADS):
        lg = RET_LOG_GAMMA[hd]
        q_h = q[:, hd * RET_DK:(hd + 1) * RET_DK]
        k_h = k[:, hd * RET_DK:(hd + 1) * RET_DK]
        v_h = v[:, hd * RET_DV:(hd + 1) * RET_DV]
        scores = lax.dot_general(q_h.astype(BF16), k_h.astype(BF16), (((1,), (1,)), ((), ())),
                                 preferred_element_type=F32) * dec_ref[hd]
        s_h = st_ref[hd]
        out = jnp.dot(scores.astype(BF16), v_h, preferred_element_type=F32)
        out = out + jnp.dot((q_h * jnp.exp(lg * (j + 1.0))).astype(BF16), s_h.astype(BF16),
                            preferred_element_type=F32)
        k_dec = (k_h * jnp.exp(lg * (ct - 1.0 - j))).astype(BF16)
        st_ref[hd] = math.exp(lg * ct) * s_h + lax.dot_general(
            k_dec, v_h, (((0,), (0,)), ((), ())), preferred_element_type=F32)
        mu = jnp.mean(out, axis=-1, keepdims=True)
        oc = out - mu
        var = jnp.mean(oc * oc, axis=-1, keepdims=True)
        g_h = g[:, hd * RET_DV:(hd + 1) * RET_DV]
        o_ref[0, :, hd * RET_DV:(hd + 1) * RET_DV] = (
            g_h * jax.nn.sigmoid(g_h) * (oc * lax.rsqrt(var + EPS))).astype(o_ref.dtype)

    @pl.when(c == pl.num_programs(1) - 1)
    def _():
        for hd in range(RET_HEADS):
            sl_ref[0, hd] = st_ref[hd] * math.exp(-RET_LOG_GAMMA[hd] * n_tail)


def _retention(h, s0, tabs, ct, n_tail):
    bsz, seq, _ = h.shape
    tab_spec = pl.BlockSpec((ct, RET_QK), lambda b, c: (c, 0))
    st_spec = pl.BlockSpec((1, RET_HEADS, RET_DK, RET_DV), lambda b, c: (b, 0, 0, 0))
    rq_blk = 0
    rv_blk = (2 * RET_QK) // RET_VW
    diff = jnp.arange(ct, dtype=F32)[:, None] - jnp.arange(ct, dtype=F32)[None, :]
    decay = jnp.where(diff >= 0.0, jnp.exp(jnp.asarray(RET_LOG_GAMMA, F32)[:, None, None] * jnp.maximum(diff, 0.0)),
                      0.0)
    return pl.pallas_call(
        functools.partial(_ret_body, n_tail=n_tail),
        grid=(bsz, seq // ct),
        in_specs=[
            pl.BlockSpec((1, ct, RET_QK), lambda b, c: (b, c, rq_blk)),
            pl.BlockSpec((1, ct, RET_QK), lambda b, c: (b, c, rq_blk + 1)),
            pl.BlockSpec((1, ct, RET_VW), lambda b, c: (b, c, rv_blk)),
            pl.BlockSpec((1, ct, RET_VW), lambda b, c: (b, c, rv_blk + 1)),
            st_spec, _const_spec((RET_HEADS, ct, ct)),
        ] + [tab_spec] * 6,
        out_specs=[pl.BlockSpec((1, ct, RET_VW), lambda b, c: (b, c, 0)), st_spec],
        out_shape=[jax.ShapeDtypeStruct((bsz, seq, RET_VW), BF16),
                   jax.ShapeDtypeStruct((bsz, RET_HEADS, RET_DK, RET_DV), F32)],
        scratch_shapes=[pltpu.VMEM((RET_HEADS, RET_DK, RET_DV), F32)],
        compiler_params=_cparams(2, VMEM_LIMIT),
        name="retention",
    )(h, h, h, h, s0, decay, *tabs)


def _rope_tables(pos, half, width, group, offset, scale=1.0, valid=None):
    inv = ROPE_BASE ** (-jnp.arange(half, dtype=F32) / half)
    ang = pos.astype(F32)[:, None] * inv[None, :]
    cos, sin = jnp.cos(ang), jnp.sin(ang)
    n = pos.shape[0]
    one, zero = jnp.ones((n, 1), F32), jnp.zeros((n, 1), F32)

    def lanes(first, second, other):
        grp = jnp.concatenate([jnp.broadcast_to(other, (n, offset)), first, second,
                               jnp.broadcast_to(other, (n, group - offset - 2 * half))], axis=1)
        return jnp.tile(grp, (1, width // group))

    tabs = (lanes(cos, cos, one), lanes(-sin, 0.0 * sin, zero), lanes(0.0 * sin, sin, zero))
    if valid is not None:
        tabs = tuple(jnp.where(valid[:, None], t, 0.0) for t in tabs)
    return tuple(t * scale for t in tabs)


def _even_params(e, w):
    p = {}
    p["w_in"] = jnp.pad(w["even_w_in"][e], ((0, 0), (0, EVEN_IN_PAD - EVEN_IN))).astype(BF16)
    p["w_out"] = w["even_w_out"][e].astype(BF16)
    lam_re, lam_im = w["s5_a_re"][e].astype(F32), w["s5_a_im"][e].astype(F32)
    dt = jnp.exp(w["s5_log_dt"][e].astype(F32))[:, None]
    p["lam_dt_re"], p["lam_dt_im"] = lam_re * dt, lam_im * dt
    mag = jnp.exp(lam_re * dt)
    abar_re, abar_im = mag * jnp.cos(lam_im * dt), mag * jnp.sin(lam_im * dt)
    den = lam_re * lam_re + lam_im * lam_im
    f_re = ((abar_re - 1.0) * lam_re + abar_im * lam_im) / den
    f_im = (abar_im * lam_re - (abar_re - 1.0) * lam_im) / den
    b_re, b_im = w["s5_b_re"][e].astype(F32), w["s5_b_im"][e].astype(F32)
    bb_re = f_re[..., None] * b_re - f_im[..., None] * b_im
    bb_im = f_re[..., None] * b_im + f_im[..., None] * b_re
    eye = jnp.eye(S5_GROUPS, dtype=F32)

    def in_blocks(x):
        return jnp.einsum("gnc,gh->gchn", x, eye).reshape(S5_WIDTH, S5_HALF)

    def out_blocks(x):
        return jnp.einsum("gcn,gh->gnhc", x, eye).reshape(S5_HALF, S5_WIDTH)

    p["bbig"] = jnp.concatenate([in_blocks(bb_re), in_blocks(bb_im)], axis=1).astype(BF16)
    p["cbig"] = jnp.concatenate([out_blocks(w["s5_c_re"][e].astype(F32)),
                                 out_blocks(-w["s5_c_im"][e].astype(F32))], axis=0).astype(BF16)
    p["a_re"] = abar_re.reshape(1, S5_HALF)
    p["a_im"] = abar_im.reshape(1, S5_HALF)
    p["d"] = w["s5_d"][e].astype(F32).reshape(1, S5_WIDTH)
    p["w_glu"] = w["s5_w_glu"][e].astype(BF16)
    p["b_glu"] = w["s5_b_glu"][e].astype(F32).reshape(1, S5_WIDTH)
    p["q_norm"] = w["mla_q_norm"][e].astype(F32).reshape(1, MLA_Q_RANK)
    p["kv_norm"] = w["mla_kv_norm"][e].astype(F32).reshape(1, MLA_KV_RANK)
    wq = w["mla_w_uq"][e].reshape(MLA_Q_RANK, MLA_HEADS, MLA_NOPE + MLA_ROPE)
    wq = jnp.pad(wq, ((0, 0), (0, 0), (0, MLA_QK_PAD - MLA_NOPE - MLA_ROPE)))
    p["wq"] = wq.reshape(MLA_Q_RANK, MLA_HEADS * MLA_QK_PAD).astype(BF16)
    wkv = w["mla_w_ukv"][e].reshape(MLA_KV_RANK, MLA_HEADS, MLA_NOPE + MLA_V)
    wkn = jnp.pad(wkv[:, :, :MLA_NOPE], ((0, 0), (0, 0), (0, MLA_QK_PAD - MLA_NOPE)))
    p["wkn"] = wkn.reshape(MLA_KV_RANK, MLA_HEADS * MLA_QK_PAD).astype(BF16)
    place = jnp.pad(jnp.eye(MLA_ROPE, dtype=F32), ((0, 0), (MLA_NOPE, MLA_QK_PAD - MLA_NOPE - MLA_ROPE)))
    p["wkp"] = jnp.tile(place, (1, MLA_HEADS)).astype(BF16)
    p["wv"] = wkv[:, :, MLA_NOPE:].reshape(MLA_KV_RANK, MLA_HEADS * MLA_V).astype(BF16)
    return p


def _odd_params(o, w):
    w_in = w["odd_w_in"][o]
    c_logit = 3 * FOX_W
    cols = jnp.concatenate([
        w_in[:, :FOX_W] * (FOX_DIM ** -0.5),
        w_in[:, FOX_W:c_logit],
        w_in[:, c_logit + FOX_HEADS:],
        w_in[:, c_logit:c_logit + FOX_HEADS],
    ], axis=1)
    p = {"w_in": jnp.pad(cols, ((0, 0), (0, ODD_IN_PAD - cols.shape[1]))).astype(BF16)}
    p["w_out"] = w["odd_w_out"][o].astype(BF16)
    p["b_f"] = w["fox_b_f"][o].astype(F32).reshape(1, FOX_HEADS)
    return p


def _cache_tile(n_past):
    for tk in (512, 256, 128):
        if n_past % tk == 0:
            return tk
    raise ValueError("cache length must be a multiple of 128")


def _trunk(x, pos, n_real, out_rows, past, w, ffn, evens, odds):
    bsz, seq, _ = x.shape
    rows = bsz * seq
    prompt = past is None
    first_row = PAD if prompt else 0
    if prompt:
        tm = 2 * TILE if rows % (2 * TILE) == 0 else TILE
        rt = TILE
    else:
        n_past = past["cache_fox_k"].shape[2]
        tk = _cache_tile(n_past)
        tm, rt = rows, seq
    idx = jnp.arange(seq, dtype=jnp.int32)
    valid = (idx >= first_row) & (idx < n_real)
    st = {n: [] for n in ("mla_ckv", "mla_kpe", "s5_re", "s5_im", "fox_k", "fox_v", "fox_logf", "ret")}
    rq_tabs = _rope_tables(pos, RET_DK // 2, RET_QK, RET_DK, 0)
    rk_tabs = _rope_tables(pos, RET_DK // 2, RET_QK, RET_DK, 0, scale=RET_DK ** -0.5, valid=valid)
    x2 = x.reshape(rows, D_MODEL)

    def ln(l, i):
        return w["ln_g"][l, i].reshape(1, D_MODEL), w["ln_b"][l, i].reshape(1, D_MODEL)

    for l in range(DEPTH):
        x2 = _ffn_ln(x2, ffn, (l, 0), *ln(l, 0), tm)
        if l % 2 == 0:
            e = l // 2
            p = evens[e]
            h = _proj(x2, p["w_in"], tm).reshape(bsz, seq, EVEN_IN_PAD)
            if prompt:
                h0r = h0i = jnp.zeros((bsz, 1, S5_HALF), F32)
            else:
                h0r = past["state_s5_re"][e].astype(F32).reshape(bsz, 1, S5_HALF)
                h0i = past["state_s5_im"][e].astype(F32).reshape(bsz, 1, S5_HALF)
            s5_out, hlr, hli = _s5(h, h0r, h0i, p, rt, first_row, n_real)
            q, ckv, kpe = _mla_rows(h, p, pos, rt, attn_layout=prompt)
            if prompt:
                k_att, v_att = _mla_kv(ckv, kpe, p)
                mla_out = _flash(q, k_att, v_att, None, None, heads=MLA_HEADS, dq=MLA_QK_PAD, dv=MLA_V,
                                 causal=False)
            else:
                mla_out = _mla_decode(q, ckv, kpe, past["cache_mla_ckv"], past["cache_mla_kpe"], e, p, tk)
            mix = (s5_out.reshape(rows, S5_WIDTH), mla_out.reshape(rows, MLA_HEADS * MLA_V), p["w_out"])
            st["mla_ckv"].append(ckv)
            st["mla_kpe"].append(kpe)
            st["s5_re"].append(hlr.reshape(bsz, S5_GROUPS, S5_STATE))
            st["s5_im"].append(hli.reshape(bsz, S5_GROUPS, S5_STATE))
        else:
            o = l // 2
            p = odds[o]
            h, k32, v32, fq, fk16, fv = _proj_odd(x2, p["w_in"], bsz, TILE if prompt else tm, attn_layout=prompt)
            h = h.reshape(bsz, seq, ODD_H)
            fk16 = fk16.reshape(bsz, seq, FOX_W)
            f_logit = h[:, :, ODD_LOGIT_COL:ODD_LOGIT_COL + FOX_HEADS]
            if prompt:
                q_off = 0
                logf, fcum = _gate(f_logit, p["b_f"], 0, PAD)
                fcum = fcum * LOG2E
                fox_out = _flash(fq, fk16, fv, fcum.transpose(0, 2, 1), fcum, heads=FOX_HEADS, dq=FOX_DIM,
                                 dv=FOX_DIM, causal=True)
                s0 = jnp.zeros((bsz, RET_HEADS, RET_DK, RET_DV), F32)
            else:
                q_off = n_past
                gates = jnp.concatenate([past["cache_fox_logf"][o].astype(F32), f_logit], axis=1)
                gates = jnp.pad(gates, ((0, 0), (0, _round_up(n_past + seq, TILE) - n_past - seq), (0, 0)))
                logf, fcum = _gate(gates, p["b_f"], n_past, 0)
                fox_out = _fox_decode(fq.reshape(bsz, seq, FOX_W), fk16, fv.reshape(bsz, seq, FOX_W),
                                      past["cache_fox_k"], past["cache_fox_v"], o, fcum * LOG2E, tk)
                s0 = past["state_ret"][o].astype(F32)
            ret_out, s_last = _retention(h, s0, rq_tabs + rk_tabs, rt, seq - n_real)
            mix = (fox_out.reshape(rows, FOX_W), ret_out.reshape(rows, RET_VW), p["w_out"])
            if prompt:
                k32, v32 = k32.transpose(0, 3, 1, 2), v32.transpose(0, 3, 1, 2)
            st["fox_k"].append(k32.reshape(bsz, seq, FOX_HEADS, FOX_DIM))
            st["fox_v"].append(v32.reshape(bsz, seq, FOX_HEADS, FOX_DIM))
            st["fox_logf"].append(logf[:, q_off:q_off + seq])
            st["ret"].append(s_last)
        keep = (seq,) + out_rows if l == DEPTH - 1 and out_rows != (0, seq) else None
        x2 = _mix_ffn_ln(x2, *mix, ln(l, 1), ffn, (l, 1), ln(l, 2), tm, keep)
    return x2.reshape(bsz, out_rows[1], D_MODEL), {n: jnp.stack(a) for n, a in st.items()}


def kernel(x_prompt, x_sample, cache_mla_ckv, cache_mla_kpe, cache_fox_k, cache_fox_v, cache_fox_logf,
           state_s5_re, state_s5_im, state_ret, meta_tokens, ln_g, ln_b, ffn_w_gate, ffn_w_up, ffn_w_down,
           even_w_in, even_w_out, s5_a_re, s5_a_im, s5_b_re, s5_b_im, s5_c_re, s5_c_im, s5_d, s5_log_dt,
           s5_w_glu, s5_b_glu, mla_q_norm, mla_kv_norm, mla_w_uq, mla_w_ukv, odd_w_in, odd_w_out, fox_b_f):
    w = dict(ln_g=ln_g.astype(F32), ln_b=ln_b.astype(F32), even_w_in=even_w_in, even_w_out=even_w_out,
             s5_a_re=s5_a_re, s5_a_im=s5_a_im, s5_b_re=s5_b_re, s5_b_im=s5_b_im, s5_c_re=s5_c_re,
             s5_c_im=s5_c_im, s5_d=s5_d, s5_log_dt=s5_log_dt, s5_w_glu=s5_w_glu, s5_b_glu=s5_b_glu,
             mla_q_norm=mla_q_norm, mla_kv_norm=mla_kv_norm, mla_w_uq=mla_w_uq, mla_w_ukv=mla_w_ukv,
             odd_w_in=odd_w_in, odd_w_out=odd_w_out, fox_b_f=fox_b_f)
    past = dict(cache_mla_ckv=cache_mla_ckv, cache_mla_kpe=cache_mla_kpe, cache_fox_k=cache_fox_k,
                cache_fox_v=cache_fox_v, cache_fox_logf=cache_fox_logf, state_s5_re=state_s5_re,
                state_s5_im=state_s5_im, state_ret=state_ret)
    ffn = (ffn_w_gate.astype(BF16), ffn_w_up.astype(BF16), ffn_w_down.astype(BF16))
    evens = [_even_params(e, w) for e in range((DEPTH + 1) // 2)]
    odds = [_odd_params(o, w) for o in range(DEPTH // 2)]

    bsz, seq, _ = x_prompt.shape
    n_real = PAD + N_META + seq
    n_rows = _round_up(n_real, TILE)
    meta = jnp.broadcast_to(meta_tokens[None].astype(x_prompt.dtype), (bsz, N_META, D_MODEL))
    xp = jnp.concatenate([jnp.zeros((bsz, PAD, D_MODEL), x_prompt.dtype), meta, x_prompt,
                          jnp.zeros((bsz, n_rows - n_real, D_MODEL), x_prompt.dtype)], axis=1)
    pos_p = jnp.maximum(jnp.arange(n_rows, dtype=jnp.int32) - PAD, 0)
    y_p, st_p = _trunk(xp, pos_p, n_real, (PAD + N_META, seq), None, w, ffn, evens, odds)
    d_seq = x_sample.shape[1]
    pos_s = N_META + cache_fox_k.shape[2] + jnp.arange(d_seq, dtype=jnp.int32)
    y_s, st_s = _trunk(x_sample, pos_s, d_seq, (0, d_seq), past, w, ffn, evens, odds)

    def real(a):
        return a[:, :, PAD:n_real]

    return (y_p, y_s,
            real(st_p["mla_ckv"]), real(st_p["mla_kpe"]), real(st_p["fox_k"]), real(st_p["fox_v"]),
            real(st_p["fox_logf"]), st_p["s5_re"], st_p["s5_im"], st_p["ret"],
            st_s["mla_ckv"], st_s["mla_kpe"], st_s["fox_k"], st_s["fox_v"], st_s["fox_logf"],
            st_s["s5_re"], st_s["s5_im"], st_s["ret"])
```

```python
import functools
import math

import jax
import jax.numpy as jnp
from jax import lax
from jax.experimental import pallas as pl
from jax.experimental.pallas import tpu as pltpu

F32 = jnp.float32
BF16 = jnp.bfloat16

D_MODEL = 1024
DEPTH = 4
CHUNK = 64
CHUNK_SHIFT = 6
N_META = 16
S5_WIDTH = 512
S5_CH = 16
S5_GROUPS = S5_WIDTH // S5_CH
S5_STATE = 64
S5_HALF = S5_GROUPS * S5_STATE
MLA_HEADS = 8
MLA_Q_RANK = 256
MLA_KV_RANK = 128
MLA_NOPE = 64
MLA_ROPE = 32
MLA_V = 64
MLA_QK_PAD = 128
FOX_HEADS = 8
FOX_DIM = 64
FOX_W = FOX_HEADS * FOX_DIM
RET_HEADS = 4
RET_DK = 64
RET_DV = 128
RET_QK = RET_HEADS * RET_DK
RET_VW = RET_HEADS * RET_DV
D_FF = 2816
ROPE_BASE = 10000.0
ALPHA = (2.0 * DEPTH) ** 0.25
EPS = 1e-5
NEG = -1e30
LOG2E = math.log2(math.e)
EVEN_IN = S5_WIDTH + MLA_Q_RANK + MLA_KV_RANK + MLA_ROPE
EVEN_IN_PAD = 1024
ODD_IN_PAD = 3200
ODD_H = ODD_IN_PAD - 3 * FOX_W
ODD_LOGIT_COL = 2 * RET_QK + 2 * RET_VW

PAD = CHUNK - N_META
LANE = 128
SUBLANE = 8
TILE = 256
VMEM_LIMIT = 56 * 1024 * 1024


def _cparams(n_grid, vmem=None):
    return pltpu.CompilerParams(dimension_semantics=("arbitrary",) * n_grid, vmem_limit_bytes=vmem)


def _const_spec(shape):
    nd = len(shape)
    return pl.BlockSpec(shape, lambda *_: (0,) * nd)


def _round_up(n, m):
    return -(-n // m) * m


def _layer_norm_rows(z, g, b):
    mu = jnp.mean(z, axis=-1, keepdims=True)
    zc = z - mu
    var = jnp.mean(zc * zc, axis=-1, keepdims=True)
    return zc * lax.rsqrt(var + EPS) * g + b


def _ffn_rows(x, wg_ref, wu_ref, wd_ref, g_ref, b_ref, hid_ref):
    xb = x.astype(BF16)
    for c in range(D_FF // TILE):
        sl = slice(c * TILE, (c + 1) * TILE)
        hg = jnp.dot(xb, wg_ref[:, sl], preferred_element_type=F32)
        hu = jnp.dot(xb, wu_ref[:, sl], preferred_element_type=F32)
        hid_ref[:, sl] = (hg * jax.nn.sigmoid(hg) * hu).astype(BF16)
    y = jnp.dot(hid_ref[...], wd_ref[...], preferred_element_type=F32)
    return _layer_norm_rows(ALPHA * x + 0.5 * y, g_ref[...], b_ref[...])


def _ffn_body(x_ref, wg_ref, wu_ref, wd_ref, g_ref, b_ref, o_ref, hid_ref):
    o_ref[...] = _ffn_rows(x_ref[...], wg_ref, wu_ref, wd_ref, g_ref, b_ref, hid_ref)


def _ffn_weight_specs(which):
    once = dict(pipeline_mode=pl.Buffered(1))
    pick = lambda i: which + (0, 0)
    return [pl.BlockSpec((None, None, D_MODEL, D_FF), pick, **once),
            pl.BlockSpec((None, None, D_MODEL, D_FF), pick, **once),
            pl.BlockSpec((None, None, D_FF, D_MODEL), pick, **once),
            _const_spec((1, D_MODEL)), _const_spec((1, D_MODEL))]


def _ffn_ln(x, ffn, which, g, b, tm):
    rows = x.shape[0]
    return pl.pallas_call(
        _ffn_body,
        grid=(rows // tm,),
        in_specs=[pl.BlockSpec((tm, D_MODEL), lambda i: (i, 0))] + _ffn_weight_specs(which),
        out_specs=pl.BlockSpec((tm, D_MODEL), lambda i: (i, 0)),
        out_shape=jax.ShapeDtypeStruct((rows, D_MODEL), F32),
        scratch_shapes=[pltpu.VMEM((tm, D_FF), BF16)],
        compiler_params=_cparams(1, VMEM_LIMIT),
        name="ffn_ln",
    )(x, *ffn, g, b)


def _mix_ffn_body(x_ref, a1_ref, a2_ref, wo_ref, g1_ref, b1_ref, wg_ref, wu_ref, wd_ref, g2_ref, b2_ref,
                  o_ref, hid_ref):
    k1 = a1_ref.shape[1]
    y = jnp.dot(a1_ref[...], wo_ref[:k1, :], preferred_element_type=F32)
    y = y + jnp.dot(a2_ref[...], wo_ref[k1:, :], preferred_element_type=F32)
    x1 = _layer_norm_rows(ALPHA * x_ref[...] + y, g1_ref[...], b1_ref[...])
    o_ref[...] = _ffn_rows(x1, wg_ref, wu_ref, wd_ref, g2_ref, b2_ref, hid_ref)


def _mix_ffn_ln(x, a1, a2, w_out, ln1, ffn, which, ln2, tm, keep=None):
    rows = x.shape[0]
    if keep is None:
        n_steps = rows // tm
        row_spec = lambda n: pl.BlockSpec((tm, n), lambda i: (i, 0))
    else:
        seq, start, count = keep
        assert count % tm == 0
        per = count // tm
        n_steps = (rows // seq) * per
        align = math.gcd(seq, start, tm)
        row_spec = lambda n: pl.BlockSpec(
            (pl.Element(tm), pl.Element(n)),
            lambda i: (pl.multiple_of((i // per) * seq + start + (i % per) * tm, align), 0))
    return pl.pallas_call(
        _mix_ffn_body,
        grid=(n_steps,),
        in_specs=[row_spec(D_MODEL), row_spec(a1.shape[1]), row_spec(a2.shape[1]),
                  pl.BlockSpec(w_out.shape, lambda i: (0, 0), pipeline_mode=pl.Buffered(1)),
                  _const_spec((1, D_MODEL)), _const_spec((1, D_MODEL))] + _ffn_weight_specs(which),
        out_specs=pl.BlockSpec((tm, D_MODEL), lambda i: (i, 0)),
        out_shape=jax.ShapeDtypeStruct((n_steps * tm, D_MODEL), F32),
        scratch_shapes=[pltpu.VMEM((tm, D_FF), BF16)],
        compiler_params=_cparams(1, VMEM_LIMIT),
        name="mix_ffn_ln",
    )(x, a1, a2, w_out, *ln1, *ffn, *ln2)


def _proj_body(x_ref, w_ref, o_ref):
    o_ref[...] = jnp.dot(x_ref[...].astype(BF16), w_ref[...], preferred_element_type=F32)


def _proj(x, w, tm):
    rows, n = x.shape[0], w.shape[1]
    return pl.pallas_call(
        _proj_body,
        grid=(rows // tm,),
        in_specs=[pl.BlockSpec((tm, D_MODEL), lambda i: (i, 0)), _const_spec(w.shape)],
        out_specs=pl.BlockSpec((tm, n), lambda i: (i, 0)),
        out_shape=jax.ShapeDtypeStruct((rows, n), F32),
        compiler_params=_cparams(1, VMEM_LIMIT),
        name="in_proj_even",
    )(x, w)


def _proj_odd_body(x_ref, w_ref, h_ref, k32_ref, v32_ref, q_ref, k_ref, v_ref, *, attn_layout):
    y = jnp.dot(x_ref[...].astype(BF16), w_ref[...], preferred_element_type=F32)
    h_ref[...] = y[:, 3 * FOX_W:]
    q = y[:, :FOX_W] * LOG2E
    k, v = y[:, FOX_W:2 * FOX_W], y[:, 2 * FOX_W:3 * FOX_W]
    k_ref[...] = k.astype(BF16)
    if attn_layout:
        q_ref[0] = q.T.astype(BF16)
        k_t, v_t = k.T, v.T
        for hd in range(FOX_HEADS):
            rows = slice(hd * FOX_DIM, (hd + 1) * FOX_DIM)
            k32_ref[0, hd] = k_t[rows, :]
            v32_ref[0, hd] = v_t[rows, :]
            v_ref[0, hd, 0] = v_t[rows, :].astype(BF16)
    else:
        k32_ref[...] = k
        v32_ref[...] = v
        q_ref[...] = q.astype(BF16)
        v_ref[...] = v.astype(BF16)


def _proj_odd(x, w, bsz, tm, attn_layout):
    rows = x.shape[0]
    seq = rows // bsz
    row_spec = lambda n: pl.BlockSpec((tm, n), lambda i: (i, 0))
    if attn_layout:
        assert tm == TILE and seq % TILE == 0
        n_t = seq // TILE
        kv32_shape = jax.ShapeDtypeStruct((bsz, FOX_HEADS, FOX_DIM, seq), F32)
        kv32_spec = pl.BlockSpec((1, FOX_HEADS, FOX_DIM, TILE), lambda i: (i // n_t, 0, 0, i % n_t))
        q_shape = jax.ShapeDtypeStruct((bsz, FOX_W, seq), BF16)
        q_spec = pl.BlockSpec((1, FOX_W, TILE), lambda i: (i // n_t, 0, i % n_t))
        v_shape = jax.ShapeDtypeStruct((bsz, FOX_HEADS, n_t, FOX_DIM, TILE), BF16)
        v_spec = pl.BlockSpec((1, FOX_HEADS, 1, FOX_DIM, TILE), lambda i: (i // n_t, 0, i % n_t, 0, 0))
    else:
        kv32_shape, kv32_spec = jax.ShapeDtypeStruct((rows, FOX_W), F32), row_spec(FOX_W)
        q_shape = v_shape = jax.ShapeDtypeStruct((rows, FOX_W), BF16)
        q_spec = v_spec = row_spec(FOX_W)
    return pl.pallas_call(
        functools.partial(_proj_odd_body, attn_layout=attn_layout),
        grid=(rows // tm,),
        in_specs=[row_spec(D_MODEL), _const_spec(w.shape)],
        out_specs=[row_spec(ODD_H), kv32_spec, kv32_spec, q_spec, row_spec(FOX_W), v_spec],
        out_shape=[jax.ShapeDtypeStruct((rows, ODD_H), F32), kv32_shape, kv32_shape, q_shape,
                   jax.ShapeDtypeStruct((rows, FOX_W), BF16), v_shape],
        compiler_params=_cparams(1, VMEM_LIMIT),
        name="in_proj_odd",
    )(x, w)


def _s5_body(u_ref, h0r_ref, h0i_ref, ar_ref, ai_ref, asr_ref, asi_ref, bbig_ref, cbig_ref, d_ref, wglu_ref,
             bglu_ref, o_ref, hlr_ref, hli_ref, hs_ref, st_ref, *, first_row, last_seg):
    t = pl.program_id(1)
    rt = u_ref.shape[1]
    seg = rt // SUBLANE

    @pl.when(t == 0)
    def _():
        st_ref[0:1, :] = h0r_ref[0]
        st_ref[1:2, :] = h0i_ref[0]

    u = u_ref[0]
    if first_row:
        rows = t * rt + lax.broadcasted_iota(jnp.int32, (rt, 1), 0)
        u = jnp.where(rows >= first_row, u, 0.0)
    i0 = lax.broadcasted_iota(jnp.int32, (rt, rt), 0)
    i1 = lax.broadcasted_iota(jnp.int32, (rt, rt), 1)
    regroup = (i1 == (i0 & (SUBLANE - 1)) * seg + (i0 >> 3)).astype(BF16)
    restore = (i0 == (i1 & (SUBLANE - 1)) * seg + (i1 >> 3)).astype(BF16)
    ub = jnp.dot(regroup, u.astype(BF16), preferred_element_type=F32).astype(BF16)

    half_w, half_s = S5_WIDTH // 2, S5_HALF // 2
    for kb in range(2):
        for part in range(2):
            c0 = part * S5_HALF + kb * half_s
            hs_ref[:, c0:c0 + half_s] = jnp.dot(
                ub[:, kb * half_w:(kb + 1) * half_w], bbig_ref[kb * half_w:(kb + 1) * half_w, c0:c0 + half_s],
                preferred_element_type=F32)

    def scan(lanes, start, store):
        re_l = lanes
        im_l = slice(S5_HALF + lanes.start, S5_HALF + lanes.stop)
        ar = jnp.broadcast_to(ar_ref[:, lanes], (SUBLANE, lanes.stop - lanes.start))
        ai = jnp.broadcast_to(ai_ref[:, lanes], (SUBLANE, lanes.stop - lanes.start))

        def step(k, carry):
            hr, hi = carry
            base = k * SUBLANE
            nr = ar * hr - ai * hi + hs_ref[pl.ds(base, SUBLANE), re_l]
            ni = ar * hi + ai * hr + hs_ref[pl.ds(base, SUBLANE), im_l]
            if store:
                hs_ref[pl.ds(base, SUBLANE), re_l] = nr
                hs_ref[pl.ds(base, SUBLANE), im_l] = ni
            return nr, ni

        carry = start
        for k in range(seg):
            carry = step(k, carry)
        return carry

    slabs = [slice(c * half_s, (c + 1) * half_s) for c in range(2)]
    zero = jnp.zeros((SUBLANE, half_s), F32)
    ends = [scan(lanes, (zero, zero), False) for lanes in slabs]
    er = jnp.concatenate([e[0] for e in ends], axis=1)
    ei = jnp.concatenate([e[1] for e in ends], axis=1)
    asr, asi = asr_ref[...], asi_ref[...]
    sr, si = st_ref[0:1, :], st_ref[1:2, :]
    start_r, start_i = [], []
    for j in range(SUBLANE):
        start_r.append(sr)
        start_i.append(si)
        sr, si = asr * sr - asi * si + er[j:j + 1], asr * si + asi * sr + ei[j:j + 1]
    st_ref[0:1, :] = sr
    st_ref[1:2, :] = si
    start_r8, start_i8 = jnp.concatenate(start_r, axis=0), jnp.concatenate(start_i, axis=0)
    for lanes in slabs:
        scan(lanes, (start_r8[:, lanes], start_i8[:, lanes]), True)

    hb = hs_ref[...].astype(BF16)
    ys = []
    for kb in range(2):
        re_rows = slice(kb * half_s, (kb + 1) * half_s)
        im_rows = slice(S5_HALF + kb * half_s, S5_HALF + (kb + 1) * half_s)
        cols = slice(kb * half_w, (kb + 1) * half_w)
        ys.append(jnp.dot(hb[:, re_rows], cbig_ref[re_rows, cols], preferred_element_type=F32)
                  + jnp.dot(hb[:, im_rows], cbig_ref[im_rows, cols], preferred_element_type=F32))
    y = jnp.concatenate(ys, axis=1)
    y_hi = y.astype(BF16)
    y_lo = (y - y_hi.astype(F32)).astype(BF16)
    y = (jnp.dot(restore, y_hi, preferred_element_type=F32) + jnp.dot(restore, y_lo, preferred_element_type=F32)
         + d_ref[...] * u)
    g = jax.nn.gelu(y)
    gate = jnp.dot(g.astype(BF16), wglu_ref[...], preferred_element_type=F32) + bglu_ref[...]
    o_ref[0] = (g * jax.nn.sigmoid(gate)).astype(o_ref.dtype)

    @pl.when(t == pl.num_programs(1) - 1)
    def _():
        hlr_ref[0] = sr if last_seg == SUBLANE else start_r[last_seg]
        hli_ref[0] = si if last_seg == SUBLANE else start_i[last_seg]


def _s5(h, h0r, h0i, p, rt, first_row, n_real):
    bsz, seq, _ = h.shape
    seg = rt // SUBLANE
    real_in_last = n_real - (seq - rt)
    assert 0 < real_in_last <= rt and real_in_last % seg == 0
    as_re = (jnp.exp(p["lam_dt_re"] * seg) * jnp.cos(p["lam_dt_im"] * seg)).reshape(1, S5_HALF)
    as_im = (jnp.exp(p["lam_dt_re"] * seg) * jnp.sin(p["lam_dt_im"] * seg)).reshape(1, S5_HALF)
    state_spec = pl.BlockSpec((1, 1, S5_HALF), lambda b, t: (b, 0, 0))
    return pl.pallas_call(
        functools.partial(_s5_body, first_row=first_row, last_seg=real_in_last // seg),
        grid=(bsz, seq // rt),
        in_specs=[
            pl.BlockSpec((1, rt, S5_WIDTH), lambda b, t: (b, t, 0)),
            state_spec, state_spec,
            _const_spec((1, S5_HALF)), _const_spec((1, S5_HALF)), _const_spec((1, S5_HALF)), _const_spec((1, S5_HALF)),
            _const_spec((S5_WIDTH, 2 * S5_HALF)), _const_spec((2 * S5_HALF, S5_WIDTH)),
            _const_spec((1, S5_WIDTH)), _const_spec((S5_WIDTH, S5_WIDTH)), _const_spec((1, S5_WIDTH)),
        ],
        out_specs=[pl.BlockSpec((1, rt, S5_WIDTH), lambda b, t: (b, t, 0)), state_spec, state_spec],
        out_shape=[jax.ShapeDtypeStruct((bsz, seq, S5_WIDTH), BF16),
                   jax.ShapeDtypeStruct((bsz, 1, S5_HALF), F32),
                   jax.ShapeDtypeStruct((bsz, 1, S5_HALF), F32)],
        scratch_shapes=[pltpu.VMEM((rt, 2 * S5_HALF), F32), pltpu.VMEM((2, S5_HALF), F32)],
        compiler_params=_cparams(2, VMEM_LIMIT),
        name="s5",
    )(h, h0r, h0i, p["a_re"], p["a_im"], as_re, as_im, p["bbig"], p["cbig"], p["d"], p["w_glu"], p["b_glu"])


def _rope_lanes(x, c, s1, s2, half):
    n = x.shape[-1]
    return x * c + pltpu.roll(x, n - half, 1) * s1 + pltpu.roll(x, half, 1) * s2


def _mla_rows_body(h_ref, qn_ref, kn_ref, wq_ref, *refs, attn_layout, scale):
    q_tabs, (ck_ref, s1k_ref, s2k_ref, q_ref, ckv_ref, kpe_ref) = refs[:-6], refs[-6:]
    h = h_ref[0]
    q_lat = h[:, :MLA_Q_RANK]
    q_lat = (q_lat * lax.rsqrt(jnp.mean(q_lat * q_lat, axis=-1, keepdims=True) + EPS) * qn_ref[...]).astype(BF16)
    half = MLA_ROPE // 2
    if attn_layout:
        cos_t, sin_t = q_tabs[0][...], q_tabs[1][...]
        q_t = _dot_nt(wq_ref[...], q_lat)
        for hd in range(MLA_HEADS):
            r0 = hd * MLA_QK_PAD
            x1 = q_t[r0 + MLA_NOPE:r0 + MLA_NOPE + half, :]
            x2 = q_t[r0 + MLA_NOPE + half:r0 + MLA_NOPE + 2 * half, :]
            q_ref[0, r0:r0 + MLA_QK_PAD, :] = jnp.concatenate(
                [q_t[r0:r0 + MLA_NOPE, :] * scale, x1 * cos_t - x2 * sin_t, x1 * sin_t + x2 * cos_t,
                 q_t[r0 + MLA_NOPE + 2 * half:r0 + MLA_QK_PAD, :]], axis=0).astype(BF16)
    else:
        cq, s1q, s2q = (t[...] for t in q_tabs)
        q = jnp.dot(q_lat, wq_ref[...], preferred_element_type=F32)
        for hd in range(MLA_HEADS):
            sl = slice(hd * MLA_QK_PAD, (hd + 1) * MLA_QK_PAD)
            q_ref[0, :, sl] = _rope_lanes(q[:, sl], cq, s1q, s2q, half).astype(BF16)
    c_kv = h[:, MLA_Q_RANK:MLA_Q_RANK + MLA_KV_RANK]
    ckv_ref[0] = c_kv * lax.rsqrt(jnp.mean(c_kv * c_kv, axis=-1, keepdims=True) + EPS) * kn_ref[...]
    k_pe = h[:, MLA_Q_RANK + MLA_KV_RANK:]
    kpe_ref[0] = _rope_lanes(k_pe, ck_ref[...], s1k_ref[...], s2k_ref[...], MLA_ROPE // 2)[:, :MLA_ROPE]


def _mla_rows(h, p, pos, tm, attn_layout):
    bsz, seq, _ = h.shape
    tab_spec = pl.BlockSpec((tm, LANE), lambda b, t: (t, 0))
    n_q = MLA_HEADS * MLA_QK_PAD
    half = MLA_ROPE // 2
    scale = (MLA_NOPE + MLA_ROPE) ** -0.5 * LOG2E
    k_tabs = _rope_tables(pos, half, LANE, LANE, 0)
    if attn_layout:
        q_shape, q_spec = (bsz, n_q, seq), pl.BlockSpec((1, n_q, tm), lambda b, t: (b, 0, t))
        ang = (ROPE_BASE ** (-jnp.arange(half, dtype=F32) / half))[:, None] * pos.astype(F32)[None, :]
        q_tabs = (jnp.cos(ang) * scale, jnp.sin(ang) * scale)
        q_tab_specs = [pl.BlockSpec((half, tm), lambda b, t: (0, t))] * 2
        wq = p["wq"].T
    else:
        q_shape, q_spec = (bsz, seq, n_q), pl.BlockSpec((1, tm, n_q), lambda b, t: (b, t, 0))
        q_tabs = _rope_tables(pos, half, MLA_QK_PAD, MLA_QK_PAD, MLA_NOPE, scale=scale)
        q_tab_specs = [tab_spec] * 3
        wq = p["wq"]
    return pl.pallas_call(
        functools.partial(_mla_rows_body, attn_layout=attn_layout, scale=scale),
        grid=(bsz, seq // tm),
        in_specs=[
            pl.BlockSpec((1, tm, EVEN_IN_PAD - S5_WIDTH), lambda b, t: (b, t, 1)),
            _const_spec((1, MLA_Q_RANK)), _const_spec((1, MLA_KV_RANK)),
            _const_spec(wq.shape),
        ] + q_tab_specs + [tab_spec] * 3,
        out_specs=[
            q_spec,
            pl.BlockSpec((1, tm, MLA_KV_RANK), lambda b, t: (b, t, 0)),
            pl.BlockSpec((1, tm, MLA_ROPE), lambda b, t: (b, t, 0)),
        ],
        out_shape=[jax.ShapeDtypeStruct(q_shape, BF16),
                   jax.ShapeDtypeStruct((bsz, seq, MLA_KV_RANK), F32),
                   jax.ShapeDtypeStruct((bsz, seq, MLA_ROPE), F32)],
        compiler_params=_cparams(2, VMEM_LIMIT),
        name="mla_rows",
    )(h, p["q_norm"], p["kv_norm"], wq, *q_tabs, *k_tabs)


def _mla_kv_body(ckv_ref, kpe_ref, wkn_ref, wkp_ref, wv_ref, k_ref, v_ref):
    ckv = ckv_ref[0].astype(BF16)
    k = jnp.dot(ckv, wkn_ref[...], preferred_element_type=F32)
    k = k + jnp.dot(kpe_ref[0].astype(BF16), wkp_ref[...], preferred_element_type=F32)
    k_ref[0] = k.astype(BF16)
    v_t = jnp.dot(ckv, wv_ref[...], preferred_element_type=F32).T
    for hd in range(MLA_HEADS):
        v_ref[0, hd, 0] = v_t[hd * MLA_V:(hd + 1) * MLA_V, :].astype(BF16)


def _mla_kv(ckv, kpe, p):
    bsz, seq, _ = ckv.shape
    n_t = seq // TILE
    return pl.pallas_call(
        _mla_kv_body,
        grid=(bsz, n_t),
        in_specs=[
            pl.BlockSpec((1, TILE, MLA_KV_RANK), lambda b, t: (b, t, 0)),
            pl.BlockSpec((1, TILE, MLA_ROPE), lambda b, t: (b, t, 0)),
            _const_spec(p["wkn"].shape), _const_spec(p["wkp"].shape), _const_spec(p["wv"].shape),
        ],
        out_specs=[pl.BlockSpec((1, TILE, MLA_HEADS * MLA_QK_PAD), lambda b, t: (b, t, 0)),
                   pl.BlockSpec((1, MLA_HEADS, 1, MLA_V, TILE), lambda b, t: (b, 0, t, 0, 0))],
        out_shape=[jax.ShapeDtypeStruct((bsz, seq, MLA_HEADS * MLA_QK_PAD), BF16),
                   jax.ShapeDtypeStruct((bsz, MLA_HEADS, n_t, MLA_V, TILE), BF16)],
        compiler_params=_cparams(2, VMEM_LIMIT),
        name="mla_kv",
    )(ckv, kpe, p["wkn"], p["wkp"], p["wv"])


def _flash_body(*refs, heads, dq, dv, causal):
    if causal:
        qt_ref, k_ref, vt_ref, fq_ref, fk_ref, o_ref = refs
    else:
        qt_ref, k_ref, vt_ref, o_ref = refs
    tq = qt_ref.shape[2]
    n_tiles, tk = vt_ref.shape[2], vt_ref.shape[4]
    r0 = pl.program_id(1) * tq
    j_last = jnp.minimum((r0 + tq - 1) // tk, n_tiles - 1)
    q_row = r0 + lax.broadcasted_iota(jnp.int32, (1, tq), 1)
    last = q_row if causal else ((q_row >> CHUNK_SHIFT) << CHUNK_SHIFT) + (CHUNK - 1)
    first_last = r0 if causal else ((r0 >> CHUNK_SHIFT) << CHUNK_SHIFT) + (CHUNK - 1)
    n_open = jnp.clip((first_last + 1) // tk, 1, j_last + 1)
    q_t = [qt_ref[0, hd * dq:(hd + 1) * dq, :] for hd in range(heads)]

    def tile(j, carry, masked, wide):
        width = 2 * tk if wide else tk
        start = pl.multiple_of(j * tk, tk)
        if masked:
            k_row = start + lax.broadcasted_iota(jnp.int32, (width, 1), 0)
            ok = (k_row <= last) & (k_row >= PAD)
        scores = [jnp.dot(k_ref[0, pl.ds(start, width), hd * dq:(hd + 1) * dq], q_t[hd],
                          preferred_element_type=F32) for hd in range(heads)]
        probs = []
        for hd in range(heads):
            m, l, _ = carry[hd]
            s = scores[hd]
            if causal:
                s = s + (fq_ref[0, hd:hd + 1, :] - fk_ref[0, pl.ds(start, width), hd:hd + 1])
            if masked:
                s = jnp.where(ok, s, NEG)
            m_new = jnp.maximum(m, jnp.max(s, axis=0, keepdims=True))
            a = jnp.exp2(m - m_new)
            p = jnp.exp2(s - m_new)
            probs.append((m_new, a * l + jnp.sum(p, axis=0, keepdims=True), a, p.astype(BF16)))
        new = []
        for hd in range(heads):
            m_new, l, a, p = probs[hd]
            acc = a * carry[hd][2] + jnp.dot(vt_ref[0, hd, j], p[:tk], preferred_element_type=F32)
            if wide:
                acc = acc + jnp.dot(vt_ref[0, hd, j + 1], p[tk:], preferred_element_type=F32)
            new.append((m_new, l, acc))
        return tuple(new)

    init = tuple((jnp.full((1, tq), NEG, F32), jnp.zeros((1, tq), F32), jnp.zeros((dv, tq), F32))
                 for _ in range(heads))
    carry = tile(0, init, True, False)
    n_wide = (n_open - 1) // 2
    carry = lax.fori_loop(0, n_wide, lambda i, c: tile(1 + 2 * i, c, False, True), carry)
    carry = lax.fori_loop(1 + 2 * n_wide, n_open, functools.partial(tile, masked=False, wide=False), carry)
    carry = lax.fori_loop(n_open, j_last + 1, functools.partial(tile, masked=True, wide=False), carry)
    outs = [carry[hd][2] / carry[hd][1] for hd in range(heads)]
    per = LANE // dv
    for g in range(heads // per):
        o_ref[0, :, g * LANE:(g + 1) * LANE] = jnp.concatenate(
            outs[g * per:(g + 1) * per], axis=0).T.astype(o_ref.dtype)


def _flash(q_t, k, v_t, fq_t, fk, *, heads, dq, dv, causal):
    bsz, _, n_q = q_t.shape
    n_keys = k.shape[1]
    in_specs = [
        pl.BlockSpec((1, heads * dq, TILE), lambda b, i: (b, 0, i)),
        pl.BlockSpec((1, n_keys, heads * dq), lambda b, i: (b, 0, 0)),
        pl.BlockSpec((1,) + v_t.shape[1:], lambda b, i: (b, 0, 0, 0, 0)),
    ]
    args = [q_t, k, v_t]
    if causal:
        in_specs += [pl.BlockSpec((1, heads, TILE), lambda b, i: (b, 0, i)),
                     pl.BlockSpec((1, n_keys, heads), lambda b, i: (b, 0, 0))]
        args += [fq_t, fk]
    return pl.pallas_call(
        functools.partial(_flash_body, heads=heads, dq=dq, dv=dv, causal=causal),
        grid=(bsz, n_q // TILE),
        in_specs=in_specs,
        out_specs=pl.BlockSpec((1, TILE, heads * dv), lambda b, i: (b, i, 0)),
        out_shape=jax.ShapeDtypeStruct((bsz, n_q, heads * dv), BF16),
        compiler_params=_cparams(2, VMEM_LIMIT),
        name="flash_causal" if causal else "flash_chunk",
    )(*args)


def _online_softmax(s, m_ref, l_ref, idx):
    m_old = m_ref[idx]
    m_new = jnp.maximum(m_old, jnp.max(s, axis=-1, keepdims=True))
    a = jnp.exp2(m_old - m_new)
    p = jnp.exp2(s - m_new)
    m_ref[idx] = m_new
    l_ref[idx] = a * l_ref[idx] + jnp.sum(p, axis=-1, keepdims=True)
    return a, p


def _dot_nt(a, b):
    return lax.dot_general(a, b, (((1,), (1,)), ((), ())), preferred_element_type=F32)


def _fox_decode_body(q_ref, fq_ref, kc_ref, vc_ref, fkc_ref, kn_ref, vn_ref, fkn_ref, o_ref, m_ref, l_ref, acc_ref):
    t = pl.program_id(1)
    n_q = q_ref.shape[1]
    cols = [slice(hd * FOX_DIM, (hd + 1) * FOX_DIM) for hd in range(FOX_HEADS)]

    @pl.when(t == 0)
    def _():
        m_ref[...] = jnp.full(m_ref.shape, NEG, F32)
        l_ref[...] = jnp.zeros(l_ref.shape, F32)
        acc_ref[...] = jnp.zeros(acc_ref.shape, F32)

    def attend(scores, fk_rows, values, ok):
        probs = []
        for hd in range(FOX_HEADS):
            s = scores[hd] + (fq_ref[0, :, hd:hd + 1] - fk_rows[hd:hd + 1, :])
            if ok is not None:
                s = jnp.where(ok, s, NEG)
            probs.append(_online_softmax(s, m_ref, l_ref, hd))
        for hd in range(FOX_HEADS):
            a, p = probs[hd]
            acc_ref[hd] = a * acc_ref[hd] + values(hd, p.astype(BF16))

    attend([jnp.dot(q_ref[0, :, cols[hd]], kc_ref[0, 0, hd].astype(BF16), preferred_element_type=F32)
            for hd in range(FOX_HEADS)],
           fkc_ref[0], lambda hd, p: _dot_nt(p, vc_ref[0, 0, hd].astype(BF16)), None)

    @pl.when(t == pl.num_programs(1) - 1)
    def _():
        causal = (lax.broadcasted_iota(jnp.int32, (n_q, n_q), 1) <= lax.broadcasted_iota(jnp.int32, (n_q, n_q), 0))
        attend([_dot_nt(q_ref[0, :, sl], kn_ref[0, :, sl]) for sl in cols], fkn_ref[0],
               lambda hd, p: jnp.dot(p, vn_ref[0, :, cols[hd]], preferred_element_type=F32), causal)
        for hd in range(FOX_HEADS):
            o_ref[0, :, cols[hd]] = (acc_ref[hd] / l_ref[hd]).astype(o_ref.dtype)


def _fox_decode(q, k_new, v_new, cache_k, cache_v, layer, fcum, tk):
    bsz, n_q, width = q.shape
    n_past = cache_k.shape[2]
    rows_minor = lambda c: c.transpose(0, 1, 3, 4, 2)
    fk_t = fcum.transpose(0, 2, 1)
    row_spec = pl.BlockSpec((1, n_q, width), lambda b, t: (b, 0, 0))
    cache_spec = pl.BlockSpec((1, 1, FOX_HEADS, FOX_DIM, tk), lambda b, t: (layer, b, 0, 0, t))
    return pl.pallas_call(
        _fox_decode_body,
        grid=(bsz, n_past // tk),
        in_specs=[row_spec, pl.BlockSpec((1, n_q, FOX_HEADS), lambda b, t: (b, 0, 0)),
                  cache_spec, cache_spec, pl.BlockSpec((1, FOX_HEADS, tk), lambda b, t: (b, 0, t)),
                  row_spec, row_spec, pl.BlockSpec((1, FOX_HEADS, n_q), lambda b, t: (b, 0, 0))],
        out_specs=row_spec,
        out_shape=jax.ShapeDtypeStruct((bsz, n_q, width), BF16),
        scratch_shapes=[pltpu.VMEM((FOX_HEADS, n_q, 1), F32), pltpu.VMEM((FOX_HEADS, n_q, 1), F32),
                        pltpu.VMEM((FOX_HEADS, n_q, FOX_DIM), F32)],
        compiler_params=_cparams(2, VMEM_LIMIT),
        name="fox_decode",
    )(q, fcum[:, n_past:n_past + n_q], rows_minor(cache_k), rows_minor(cache_v), fk_t[:, :, :n_past],
      k_new, v_new, fk_t[:, :, n_past:n_past + n_q])


def _mla_decode_body(q_ref, wkn_ref, wv_ref, cc_ref, pc_ref, cn_ref, pn_ref, o_ref, qa_ref, qr_ref,
                     m_ref, l_ref, acc_ref):
    t = pl.program_id(1)
    n_q = q_ref.shape[1]

    @pl.when(t == 0)
    def _():
        for hd in range(MLA_HEADS):
            c0 = hd * MLA_QK_PAD
            rows = slice(hd * n_q, (hd + 1) * n_q)
            qa_ref[rows, :] = _dot_nt(q_ref[0, :, c0:c0 + MLA_NOPE],
                                      wkn_ref[:, c0:c0 + MLA_NOPE]).astype(BF16)
            qr_ref[rows, :] = q_ref[0, :, c0 + MLA_NOPE:c0 + MLA_NOPE + MLA_ROPE]
        m_ref[...] = jnp.full(m_ref.shape, NEG, F32)
        l_ref[...] = jnp.zeros(l_ref.shape, F32)
        acc_ref[...] = jnp.zeros(acc_ref.shape, F32)

    def attend(ckv, rope_scores):
        a, p = _online_softmax(_dot_nt(qa_ref[...], ckv) + rope_scores, m_ref, l_ref, 0)
        acc_ref[0] = a * acc_ref[0] + jnp.dot(p.astype(BF16), ckv, preferred_element_type=F32)

    attend(cc_ref[0, 0].astype(BF16),
           jnp.dot(qr_ref[...], pc_ref[0, 0].astype(BF16), preferred_element_type=F32))

    @pl.when(t == pl.num_programs(1) - 1)
    def _():
        attend(cn_ref[0].astype(BF16), _dot_nt(qr_ref[...], pn_ref[0].astype(BF16)))
        lat = (acc_ref[0] / l_ref[0]).astype(BF16)
        for hd in range(MLA_HEADS):
            cols = slice(hd * MLA_V, (hd + 1) * MLA_V)
            o_ref[0, :, cols] = jnp.dot(lat[hd * n_q:(hd + 1) * n_q, :], wv_ref[:, cols],
                                        preferred_element_type=F32).astype(o_ref.dtype)


def _mla_decode(q, ckv_new, kpe_new, cache_ckv, cache_kpe, layer, p, tk):
    bsz, n_q, _ = q.shape
    n_past = cache_ckv.shape[2]
    stacked = MLA_HEADS * n_q
    return pl.pallas_call(
        _mla_decode_body,
        grid=(bsz, n_past // tk),
        in_specs=[pl.BlockSpec((1, n_q, MLA_HEADS * MLA_QK_PAD), lambda b, t: (b, 0, 0)),
                  _const_spec(p["wkn"].shape), _const_spec(p["wv"].shape),
                  pl.BlockSpec((1, 1, tk, MLA_KV_RANK), lambda b, t: (layer, b, t, 0)),
                  pl.BlockSpec((1, 1, MLA_ROPE, tk), lambda b, t: (layer, b, 0, t)),
                  pl.BlockSpec((1, n_q, MLA_KV_RANK), lambda b, t: (b, 0, 0)),
                  pl.BlockSpec((1, n_q, MLA_ROPE), lambda b, t: (b, 0, 0))],
        out_specs=pl.BlockSpec((1, n_q, MLA_HEADS * MLA_V), lambda b, t: (b, 0, 0)),
        out_shape=jax.ShapeDtypeStruct((bsz, n_q, MLA_HEADS * MLA_V), BF16),
        scratch_shapes=[pltpu.VMEM((stacked, MLA_KV_RANK), BF16), pltpu.VMEM((stacked, MLA_ROPE), BF16),
                        pltpu.VMEM((1, stacked, 1), F32), pltpu.VMEM((1, stacked, 1), F32),
                        pltpu.VMEM((1, stacked, MLA_KV_RANK), F32)],
        compiler_params=_cparams(2, VMEM_LIMIT),
        name="mla_decode",
    )(q, p["wkn"], p["wv"], cache_ckv, cache_kpe.transpose(0, 1, 3, 2), ckv_new, kpe_new)


def _gate_body(x_ref, b_ref, lf_ref, fc_ref, *, new_start, first_row):
    n_rows = x_ref.shape[1]
    tri = (lax.broadcasted_iota(jnp.int32, (TILE, TILE), 0)
           >= lax.broadcasted_iota(jnp.int32, (TILE, TILE), 1)).astype(F32)
    carry = jnp.zeros((1, x_ref.shape[2]), F32)
    for i in range(n_rows // TILE):
        sl = slice(i * TILE, (i + 1) * TILE)
        x = x_ref[0, sl, :]
        rows = i * TILE + lax.broadcasted_iota(jnp.int32, (TILE, 1), 0)
        z = x + b_ref[...]
        lf = jnp.where(rows >= new_start, jnp.minimum(z, 0.0) - jnp.log1p(jnp.exp(-jnp.abs(z))), x)
        if first_row:
            lf = jnp.where(rows >= first_row, lf, 0.0)
        lf_ref[0, sl, :] = lf
        cs = jnp.dot(tri, lf, preferred_element_type=F32, precision=lax.Precision.HIGHEST) + carry
        fc_ref[0, sl, :] = cs
        carry = cs[TILE - 1:TILE, :]


def _gate(x, b_f, new_start, first_row):
    bsz, n_rows, heads = x.shape
    spec = pl.BlockSpec((1, n_rows, heads), lambda b: (b, 0, 0))
    return pl.pallas_call(
        functools.partial(_gate_body, new_start=new_start, first_row=first_row),
        grid=(bsz,),
        in_specs=[spec, _const_spec((1, heads))],
        out_specs=[spec, spec],
        out_shape=[jax.ShapeDtypeStruct(x.shape, F32)] * 2,
        compiler_params=_cparams(1),
        name="fox_gate",
    )(x, b_f)


RET_LOG_GAMMA = tuple(math.log(1.0 - 2.0 ** (-5.0 - h)) for h in range(RET_HEADS))


def _ret_body(rq_ref, rk_ref, rv_ref, rg_ref, s0_ref, dec_ref, cq_ref, s1q_ref, s2q_ref, ck_ref, s1k_ref, s2k_ref,
              o_ref, sl_ref, st_ref, *, n_tail):
    c = pl.program_id(1)
    ct = rq_ref.shape[1]

    @pl.when(c == 0)
    def _():
        st_ref[...] = s0_ref[0]

    q = _rope_lanes(rq_ref[0], cq_ref[...], s1q_ref[...], s2q_ref[...], RET_DK // 2)
    k = _rope_lanes(rk_ref[0], ck_ref[...], s1k_ref[...], s2k_ref[...], RET_DK // 2)
    v = rv_ref[0].astype(BF16)
    g = rg_ref[0]
    j = lax.broadcasted_iota(jnp.int32, (ct, 1), 0).astype(F32)
    for hd in range(RET_HEADS):
        lg = RET_LOG_GAMMA[hd]
        q_h = q[:, hd * RET_DK:(hd + 1) * RET_DK]
        k_h = k[:, hd * RET_DK:(hd + 1) * RET_DK]
        v_h = v[:, hd * RET_DV:(hd + 1) * RET_DV]
        scores = lax.dot_general(q_h.astype(BF16), k_h.astype(BF16), (((1,), (1,)), ((), ())),
                                 preferred_element_type=F32) * dec_ref[hd]
        s_h = st_ref[hd]
        out = jnp.dot(scores.astype(BF16), v_h, preferred_element_type=F32)
        out = out + jnp.dot((q_h * jnp.exp(lg * (j + 1.0))).astype(BF16), s_h.astype(BF16),
                            preferred_element_type=F32)
        k_dec = (k_h * jnp.exp(lg * (ct - 1.0 - j))).astype(BF16)
        st_ref[hd] = math.exp(lg * ct) * s_h + lax.dot_general(
            k_dec, v_h, (((0,), (0,)), ((), ())), preferred_element_type=F32)
        mu = jnp.mean(out, axis=-1, keepdims=True)
        oc = out - mu
        var = jnp.mean(oc * oc, axis=-1, keepdims=True)
        g_h = g[:, hd * RET_DV:(hd + 1) * RET_DV]
        o_ref[0, :, hd * RET_DV:(hd + 1) * RET_DV] = (
            g_h * jax.nn.sigmoid(g_h) * (oc * lax.rsqrt(var + EPS))).astype(o_ref.dtype)

    @pl.when(c == pl.num_programs(1) - 1)
    def _():
        for hd in range(RET_HEADS):
            sl_ref[0, hd] = st_ref[hd] * math.exp(-RET_LOG_GAMMA[hd] * n_tail)


def _retention(h, s0, tabs, ct, n_tail):
    bsz, seq, _ = h.shape
    tab_spec = pl.BlockSpec((ct, RET_QK), lambda b, c: (c, 0))
    st_spec = pl.BlockSpec((1, RET_HEADS, RET_DK, RET_DV), lambda b, c: (b, 0, 0, 0))
    rq_blk = 0
    rv_blk = (2 * RET_QK) // RET_VW
    diff = jnp.arange(ct, dtype=F32)[:, None] - jnp.arange(ct, dtype=F32)[None, :]
    decay = jnp.where(diff >= 0.0, jnp.exp(jnp.asarray(RET_LOG_GAMMA, F32)[:, None, None] * jnp.maximum(diff, 0.0)),
                      0.0)
    return pl.pallas_call(
        functools.partial(_ret_body, n_tail=n_tail),
        grid=(bsz, seq // ct),
        in_specs=[
            pl.BlockSpec((1, ct, RET_QK), lambda b, c: (b, c, rq_blk)),
            pl.BlockSpec((1, ct, RET_QK), lambda b, c: (b, c, rq_blk + 1)),
            pl.BlockSpec((1, ct, RET_VW), lambda b, c: (b, c, rv_blk)),
            pl.BlockSpec((1, ct, RET_VW), lambda b, c: (b, c, rv_blk + 1)),
            st_spec, _const_spec((RET_HEADS, ct, ct)),
        ] + [tab_spec] * 6,
        out_specs=[pl.BlockSpec((1, ct, RET_VW), lambda b, c: (b, c, 0)), st_spec],
        out_shape=[jax.ShapeDtypeStruct((bsz, seq, RET_VW), BF16),
                   jax.ShapeDtypeStruct((bsz, RET_HEADS, RET_DK, RET_DV), F32)],
        scratch_shapes=[pltpu.VMEM((RET_HEADS, RET_DK, RET_DV), F32)],
        compiler_params=_cparams(2, VMEM_LIMIT),
        name="retention",
    )(h, h, h, h, s0, decay, *tabs)


def _rope_tables(pos, half, width, group, offset, scale=1.0, valid=None):
    inv = ROPE_BASE ** (-jnp.arange(half, dtype=F32) / half)
    ang = pos.astype(F32)[:, None] * inv[None, :]
    cos, sin = jnp.cos(ang), jnp.sin(ang)
    n = pos.shape[0]
    one, zero = jnp.ones((n, 1), F32), jnp.zeros((n, 1), F32)

    def lanes(first, second, other):
        grp = jnp.concatenate([jnp.broadcast_to(other, (n, offset)), first, second,
                               jnp.broadcast_to(other, (n, group - offset - 2 * half))], axis=1)
        return jnp.tile(grp, (1, width // group))

    tabs = (lanes(cos, cos, one), lanes(-sin, 0.0 * sin, zero), lanes(0.0 * sin, sin, zero))
    if valid is not None:
        tabs = tuple(jnp.where(valid[:, None], t, 0.0) for t in tabs)
    return tuple(t * scale for t in tabs)


def _even_params(e, w):
    p = {}
    p["w_in"] = jnp.pad(w["even_w_in"][e], ((0, 0), (0, EVEN_IN_PAD - EVEN_IN))).astype(BF16)
    p["w_out"] = w["even_w_out"][e].astype(BF16)
    lam_re, lam_im = w["s5_a_re"][e].astype(F32), w["s5_a_im"][e].astype(F32)
    dt = jnp.exp(w["s5_log_dt"][e].astype(F32))[:, None]
    p["lam_dt_re"], p["lam_dt_im"] = lam_re * dt, lam_im * dt
    mag = jnp.exp(lam_re * dt)
    abar_re, abar_im = mag * jnp.cos(lam_im * dt), mag * jnp.sin(lam_im * dt)
    den = lam_re * lam_re + lam_im * lam_im
    f_re = ((abar_re - 1.0) * lam_re + abar_im * lam_im) / den
    f_im = (abar_im * lam_re - (abar_re - 1.0) * lam_im) / den
    b_re, b_im = w["s5_b_re"][e].astype(F32), w["s5_b_im"][e].astype(F32)
    bb_re = f_re[..., None] * b_re - f_im[..., None] * b_im
    bb_im = f_re[..., None] * b_im + f_im[..., None] * b_re
    eye = jnp.eye(S5_GROUPS, dtype=F32)

    def in_blocks(x):
        return jnp.einsum("gnc,gh->gchn", x, eye).reshape(S5_WIDTH, S5_HALF)

    def out_blocks(x):
        return jnp.einsum("gcn,gh->gnhc", x, eye).reshape(S5_HALF, S5_WIDTH)

    p["bbig"] = jnp.concatenate([in_blocks(bb_re), in_blocks(bb_im)], axis=1).astype(BF16)
    p["cbig"] = jnp.concatenate([out_blocks(w["s5_c_re"][e].astype(F32)),
                                 out_blocks(-w["s5_c_im"][e].astype(F32))], axis=0).astype(BF16)
    p["a_re"] = abar_re.reshape(1, S5_HALF)
    p["a_im"] = abar_im.reshape(1, S5_HALF)
    p["d"] = w["s5_d"][e].astype(F32).reshape(1, S5_WIDTH)
    p["w_glu"] = w["s5_w_glu"][e].astype(BF16)
    p["b_glu"] = w["s5_b_glu"][e].astype(F32).reshape(1, S5_WIDTH)
    p["q_norm"] = w["mla_q_norm"][e].astype(F32).reshape(1, MLA_Q_RANK)
    p["kv_norm"] = w["mla_kv_norm"][e].astype(F32).reshape(1, MLA_KV_RANK)
    wq = w["mla_w_uq"][e].reshape(MLA_Q_RANK, MLA_HEADS, MLA_NOPE + MLA_ROPE)
    wq = jnp.pad(wq, ((0, 0), (0, 0), (0, MLA_QK_PAD - MLA_NOPE - MLA_ROPE)))
    p["wq"] = wq.reshape(MLA_Q_RANK, MLA_HEADS * MLA_QK_PAD).astype(BF16)
    wkv = w["mla_w_ukv"][e].reshape(MLA_KV_RANK, MLA_HEADS, MLA_NOPE + MLA_V)
    wkn = jnp.pad(wkv[:, :, :MLA_NOPE], ((0, 0), (0, 0), (0, MLA_QK_PAD - MLA_NOPE)))
    p["wkn"] = wkn.reshape(MLA_KV_RANK, MLA_HEADS * MLA_QK_PAD).astype(BF16)
    place = jnp.pad(jnp.eye(MLA_ROPE, dtype=F32), ((0, 0), (MLA_NOPE, MLA_QK_PAD - MLA_NOPE - MLA_ROPE)))
    p["wkp"] = jnp.tile(place, (1, MLA_HEADS)).astype(BF16)
    p["wv"] = wkv[:, :, MLA_NOPE:].reshape(MLA_KV_RANK, MLA_HEADS * MLA_V).astype(BF16)
    return p


def _odd_params(o, w):
    w_in = w["odd_w_in"][o]
    c_logit = 3 * FOX_W
    cols = jnp.concatenate([
        w_in[:, :FOX_W] * (FOX_DIM ** -0.5),
        w_in[:, FOX_W:c_logit],
        w_in[:, c_logit + FOX_HEADS:],
        w_in[:, c_logit:c_logit + FOX_HEADS],
    ], axis=1)
    p = {"w_in": jnp.pad(cols, ((0, 0), (0, ODD_IN_PAD - cols.shape[1]))).astype(BF16)}
    p["w_out"] = w["odd_w_out"][o].astype(BF16)
    p["b_f"] = w["fox_b_f"][o].astype(F32).reshape(1, FOX_HEADS)
    return p


def _cache_tile(n_past):
    for tk in (512, 256, 128):
        if n_past % tk == 0:
            return tk
    raise ValueError("cache length must be a multiple of 128")


def _trunk(x, pos, n_real, out_rows, past, w, ffn, evens, odds):
    bsz, seq, _ = x.shape
    rows = bsz * seq
    prompt = past is None
    first_row = PAD if prompt else 0
    if prompt:
        tm = 2 * TILE if rows % (2 * TILE) == 0 else TILE
        rt = TILE
    else:
        n_past = past["cache_fox_k"].shape[2]
        tk = _cache_tile(n_past)
        tm, rt = rows, seq
    idx = jnp.arange(seq, dtype=jnp.int32)
    valid = (idx >= first_row) & (idx < n_real)
    st = {n: [] for n in ("mla_ckv", "mla_kpe", "s5_re", "s5_im", "fox_k", "fox_v", "fox_logf", "ret")}
    rq_tabs = _rope_tables(pos, RET_DK // 2, RET_QK, RET_DK, 0)
    rk_tabs = _rope_tables(pos, RET_DK // 2, RET_QK, RET_DK, 0, scale=RET_DK ** -0.5, valid=valid)
    x2 = x.reshape(rows, D_MODEL)

    def ln(l, i):
        return w["ln_g"][l, i].reshape(1, D_MODEL), w["ln_b"][l, i].reshape(1, D_MODEL)

    for l in range(DEPTH):
        x2 = _ffn_ln(x2, ffn, (l, 0), *ln(l, 0), tm)
        if l % 2 == 0:
            e = l // 2
            p = evens[e]
            h = _proj(x2, p["w_in"], tm).reshape(bsz, seq, EVEN_IN_PAD)
            if prompt:
                h0r = h0i = jnp.zeros((bsz, 1, S5_HALF), F32)
            else:
                h0r = past["state_s5_re"][e].astype(F32).reshape(bsz, 1, S5_HALF)
                h0i = past["state_s5_im"][e].astype(F32).reshape(bsz, 1, S5_HALF)
            s5_out, hlr, hli = _s5(h, h0r, h0i, p, rt, first_row, n_real)
            q, ckv, kpe = _mla_rows(h, p, pos, rt, attn_layout=prompt)
            if prompt:
                k_att, v_att = _mla_kv(ckv, kpe, p)
                mla_out = _flash(q, k_att, v_att, None, None, heads=MLA_HEADS, dq=MLA_QK_PAD, dv=MLA_V,
                                 causal=False)
            else:
                mla_out = _mla_decode(q, ckv, kpe, past["cache_mla_ckv"], past["cache_mla_kpe"], e, p, tk)
            mix = (s5_out.reshape(rows, S5_WIDTH), mla_out.reshape(rows, MLA_HEADS * MLA_V), p["w_out"])
            st["mla_ckv"].append(ckv)
            st["mla_kpe"].append(kpe)
            st["s5_re"].append(hlr.reshape(bsz, S5_GROUPS, S5_STATE))
            st["s5_im"].append(hli.reshape(bsz, S5_GROUPS, S5_STATE))
        else:
            o = l // 2
            p = odds[o]
            h, k32, v32, fq, fk16, fv = _proj_odd(x2, p["w_in"], bsz, TILE if prompt else tm, attn_layout=prompt)
            h = h.reshape(bsz, seq, ODD_H)
            fk16 = fk16.reshape(bsz, seq, FOX_W)
            f_logit = h[:, :, ODD_LOGIT_COL:ODD_LOGIT_COL + FOX_HEADS]
            if prompt:
                q_off = 0
                logf, fcum = _gate(f_logit, p["b_f"], 0, PAD)
                fcum = fcum * LOG2E
                fox_out = _flash(fq, fk16, fv, fcum.transpose(0, 2, 1), fcum, heads=FOX_HEADS, dq=FOX_DIM,
                                 dv=FOX_DIM, causal=True)
                s0 = jnp.zeros((bsz, RET_HEADS, RET_DK, RET_DV), F32)
            else:
                q_off = n_past
                gates = jnp.concatenate([past["cache_fox_logf"][o].astype(F32), f_logit], axis=1)
                gates = jnp.pad(gates, ((0, 0), (0, _round_up(n_past + seq, TILE) - n_past - seq), (0, 0)))
                logf, fcum = _gate(gates, p["b_f"], n_past, 0)
                fox_out = _fox_decode(fq.reshape(bsz, seq, FOX_W), fk16, fv.reshape(bsz, seq, FOX_W),
                                      past["cache_fox_k"], past["cache_fox_v"], o, fcum * LOG2E, tk)
                s0 = past["state_ret"][o].astype(F32)
            ret_out, s_last = _retention(h, s0, rq_tabs + rk_tabs, rt, seq - n_real)
            mix = (fox_out.reshape(rows, FOX_W), ret_out.reshape(rows, RET_VW), p["w_out"])
            if prompt:
                k32, v32 = k32.transpose(0, 3, 1, 2), v32.transpose(0, 3, 1, 2)
            st["fox_k"].append(k32.reshape(bsz, seq, FOX_HEADS, FOX_DIM))
            st["fox_v"].append(v32.reshape(bsz, seq, FOX_HEADS, FOX_DIM))
            st["fox_logf"].append(logf[:, q_off:q_off + seq])
            st["ret"].append(s_last)
        keep = (seq,) + out_rows if l == DEPTH - 1 and out_rows != (0, seq) else None
        x2 = _mix_ffn_ln(x2, *mix, ln(l, 1), ffn, (l, 1), ln(l, 2), tm, keep)
    return x2.reshape(bsz, out_rows[1], D_MODEL), {n: jnp.stack(a) for n, a in st.items()}


def kernel(x_prompt, x_sample, cache_mla_ckv, cache_mla_kpe, cache_fox_k, cache_fox_v, cache_fox_logf,
           state_s5_re, state_s5_im, state_ret, meta_tokens, ln_g, ln_b, ffn_w_gate, ffn_w_up, ffn_w_down,
           even_w_in, even_w_out, s5_a_re, s5_a_im, s5_b_re, s5_b_im, s5_c_re, s5_c_im, s5_d, s5_log_dt,
           s5_w_glu, s5_b_glu, mla_q_norm, mla_kv_norm, mla_w_uq, mla_w_ukv, odd_w_in, odd_w_out, fox_b_f):
    w = dict(ln_g=ln_g.astype(F32), ln_b=ln_b.astype(F32), even_w_in=even_w_in, even_w_out=even_w_out,
             s5_a_re=s5_a_re, s5_a_im=s5_a_im, s5_b_re=s5_b_re, s5_b_im=s5_b_im, s5_c_re=s5_c_re,
             s5_c_im=s5_c_im, s5_d=s5_d, s5_log_dt=s5_log_dt, s5_w_glu=s5_w_glu, s5_b_glu=s5_b_glu,
             mla_q_norm=mla_q_norm, mla_kv_norm=mla_kv_norm, mla_w_uq=mla_w_uq, mla_w_ukv=mla_w_ukv,
             odd_w_in=odd_w_in, odd_w_out=odd_w_out, fox_b_f=fox_b_f)
    past = dict(cache_mla_ckv=cache_mla_ckv, cache_mla_kpe=cache_mla_kpe, cache_fox_k=cache_fox_k,
                cache_fox_v=cache_fox_v, cache_fox_logf=cache_fox_logf, state_s5_re=state_s5_re,
                state_s5_im=state_s5_im, state_ret=state_ret)
    ffn = (ffn_w_gate.astype(BF16), ffn_w_up.astype(BF16), ffn_w_down.astype(BF16))
    evens = [_even_params(e, w) for e in range((DEPTH + 1) // 2)]
    odds = [_odd_params(o, w) for o in range(DEPTH // 2)]

    bsz, seq, _ = x_prompt.shape
    n_real = PAD + N_META + seq
    n_rows = _round_up(n_real, TILE)
    meta = jnp.broadcast_to(meta_tokens[None].astype(x_prompt.dtype), (bsz, N_META, D_MODEL))
    xp = jnp.concatenate([jnp.zeros((bsz, PAD, D_MODEL), x_prompt.dtype), meta, x_prompt,
                          jnp.zeros((bsz, n_rows - n_real, D_MODEL), x_prompt.dtype)], axis=1)
    pos_p = jnp.maximum(jnp.arange(n_rows, dtype=jnp.int32) - PAD, 0)
    y_p, st_p = _trunk(xp, pos_p, n_real, (PAD + N_META, seq), None, w, ffn, evens, odds)
    d_seq = x_sample.shape[1]
    pos_s = N_META + cache_fox_k.shape[2] + jnp.arange(d_seq, dtype=jnp.int32)
    y_s, st_s = _trunk(x_sample, pos_s, d_seq, (0, d_seq), past, w, ffn, evens, odds)

    def real(a):
        return a[:, :, PAD:n_real]

    return (y_p, y_s,
            real(st_p["mla_ckv"]), real(st_p["mla_kpe"]), real(st_p["fox_k"]), real(st_p["fox_v"]),
            real(st_p["fox_logf"]), st_p["s5_re"], st_p["s5_im"], st_p["ret"],
            st_s["mla_ckv"], st_s["mla_kpe"], st_s["fox_k"], st_s["fox_v"], st_s["fox_logf"],
            st_s["s5_re"], st_s["s5_im"], st_s["ret"])
```

```python
import functools
import math

import jax
import jax.numpy as jnp
import numpy as np
from jax import lax
from jax.experimental import pallas as pl
from jax.experimental.pallas import tpu as pltpu

F32 = jnp.float32
BF16 = jnp.bfloat16

D_MODEL = 1024
DEPTH = 4
CHUNK = 64
CHUNK_SHIFT = 6
N_META = 16
S5_WIDTH = 512
S5_CH = 16
S5_GROUPS = S5_WIDTH // S5_CH
S5_STATE = 64
S5_HALF = S5_GROUPS * S5_STATE
MLA_HEADS = 8
MLA_Q_RANK = 256
MLA_KV_RANK = 128
MLA_NOPE = 64
MLA_ROPE = 32
MLA_V = 64
MLA_QK_PAD = 128
FOX_HEADS = 8
FOX_DIM = 64
FOX_W = FOX_HEADS * FOX_DIM
RET_HEADS = 4
RET_DK = 64
RET_DV = 128
RET_QK = RET_HEADS * RET_DK
RET_VW = RET_HEADS * RET_DV
D_FF = 2816
ROPE_BASE = 10000.0
ALPHA = (2.0 * DEPTH) ** 0.25
EPS = 1e-5
NEG = -1e30
LOG2E = math.log2(math.e)
EVEN_IN = S5_WIDTH + MLA_Q_RANK + MLA_KV_RANK + MLA_ROPE
EVEN_IN_PAD = 1024
ODD_IN_PAD = 3200
ODD_H = ODD_IN_PAD - 3 * FOX_W
ODD_LOGIT_COL = 2 * RET_QK + 2 * RET_VW

PAD = CHUNK - N_META
LANE = 128
SUBLANE = 8
TILE = 256
VMEM_LIMIT = 56 * 1024 * 1024


def _cparams(n_grid, vmem=None):
    return pltpu.CompilerParams(dimension_semantics=("arbitrary",) * n_grid, vmem_limit_bytes=vmem)


def _const_spec(shape):
    nd = len(shape)
    return pl.BlockSpec(shape, lambda *_: (0,) * nd)


def _round_up(n, m):
    return -(-n // m) * m


def _layer_norm_rows(z, g, b):
    mu = jnp.mean(z, axis=-1, keepdims=True)
    zc = z - mu
    var = jnp.mean(zc * zc, axis=-1, keepdims=True)
    return zc * lax.rsqrt(var + EPS) * g + b


def _ffn_rows(x, wg_ref, wu_ref, wd_ref, g_ref, b_ref, hid_ref):
    xb = x.astype(BF16)
    for c in range(D_FF // TILE):
        sl = slice(c * TILE, (c + 1) * TILE)
        hg = jnp.dot(xb, wg_ref[:, sl], preferred_element_type=F32)
        hu = jnp.dot(xb, wu_ref[:, sl], preferred_element_type=F32)
        hid_ref[:, sl] = (hg * jax.nn.sigmoid(hg) * hu).astype(BF16)
    y = jnp.dot(hid_ref[...], wd_ref[...], preferred_element_type=F32)
    return _layer_norm_rows(ALPHA * x + 0.5 * y, g_ref[...], b_ref[...])


def _ffn_body(x_ref, wg_ref, wu_ref, wd_ref, g_ref, b_ref, o_ref, hid_ref):
    o_ref[...] = _ffn_rows(x_ref[...], wg_ref, wu_ref, wd_ref, g_ref, b_ref, hid_ref)


def _ffn_weight_specs(which):
    once = dict(pipeline_mode=pl.Buffered(1))
    pick = lambda i: which + (0, 0)
    return [pl.BlockSpec((None, None, D_MODEL, D_FF), pick, **once),
            pl.BlockSpec((None, None, D_MODEL, D_FF), pick, **once),
            pl.BlockSpec((None, None, D_FF, D_MODEL), pick, **once),
            _const_spec((1, D_MODEL)), _const_spec((1, D_MODEL))]


def _ffn_ln(x, ffn, which, g, b, tm):
    rows = x.shape[0]
    return pl.pallas_call(
        _ffn_body,
        grid=(rows // tm,),
        in_specs=[pl.BlockSpec((tm, D_MODEL), lambda i: (i, 0))] + _ffn_weight_specs(which),
        out_specs=pl.BlockSpec((tm, D_MODEL), lambda i: (i, 0)),
        out_shape=jax.ShapeDtypeStruct((rows, D_MODEL), F32),
        scratch_shapes=[pltpu.VMEM((tm, D_FF), BF16)],
        compiler_params=_cparams(1, VMEM_LIMIT),
        name="ffn_ln",
    )(x, *ffn, g, b)


def _mix_ffn_body(x_ref, a1_ref, a2_ref, wo_ref, g1_ref, b1_ref, wg_ref, wu_ref, wd_ref, g2_ref, b2_ref,
                  o_ref, hid_ref):
    k1 = a1_ref.shape[1]
    y = jnp.dot(a1_ref[...], wo_ref[:k1, :], preferred_element_type=F32)
    y = y + jnp.dot(a2_ref[...], wo_ref[k1:, :], preferred_element_type=F32)
    x1 = _layer_norm_rows(ALPHA * x_ref[...] + y, g1_ref[...], b1_ref[...])
    o_ref[...] = _ffn_rows(x1, wg_ref, wu_ref, wd_ref, g2_ref, b2_ref, hid_ref)


def _mix_ffn_ln(x, a1, a2, w_out, ln1, ffn, which, ln2, tm, keep=None):
    rows = x.shape[0]
    if keep is None:
        n_steps = rows // tm
        row_spec = lambda n: pl.BlockSpec((tm, n), lambda i: (i, 0))
    else:
        seq, start, count = keep
        assert count % tm == 0
        per = count // tm
        n_steps = (rows // seq) * per
        align = math.gcd(seq, start, tm)
        row_spec = lambda n: pl.BlockSpec(
            (pl.Element(tm), pl.Element(n)),
            lambda i: (pl.multiple_of((i // per) * seq + start + (i % per) * tm, align), 0))
    return pl.pallas_call(
        _mix_ffn_body,
        grid=(n_steps,),
        in_specs=[row_spec(D_MODEL), row_spec(a1.shape[1]), row_spec(a2.shape[1]),
                  pl.BlockSpec(w_out.shape, lambda i: (0, 0), pipeline_mode=pl.Buffered(1)),
                  _const_spec((1, D_MODEL)), _const_spec((1, D_MODEL))] + _ffn_weight_specs(which),
        out_specs=pl.BlockSpec((tm, D_MODEL), lambda i: (i, 0)),
        out_shape=jax.ShapeDtypeStruct((n_steps * tm, D_MODEL), F32),
        scratch_shapes=[pltpu.VMEM((tm, D_FF), BF16)],
        compiler_params=_cparams(1, VMEM_LIMIT),
        name="mix_ffn_ln",
    )(x, a1, a2, w_out, *ln1, *ffn, *ln2)


def _proj_body(x_ref, w_ref, o_ref):
    o_ref[...] = jnp.dot(x_ref[...].astype(BF16), w_ref[...], preferred_element_type=F32)


def _proj(x, w, tm):
    rows, n = x.shape[0], w.shape[1]
    return pl.pallas_call(
        _proj_body,
        grid=(rows // tm,),
        in_specs=[pl.BlockSpec((tm, D_MODEL), lambda i: (i, 0)), _const_spec(w.shape)],
        out_specs=pl.BlockSpec((tm, n), lambda i: (i, 0)),
        out_shape=jax.ShapeDtypeStruct((rows, n), F32),
        compiler_params=_cparams(1, VMEM_LIMIT),
        name="in_proj_even",
    )(x, w)


def _proj_odd_body(x_ref, w_ref, h_ref, k32_ref, v32_ref, q_ref, k_ref, v_ref, *, attn_layout):
    y = jnp.dot(x_ref[...].astype(BF16), w_ref[...], preferred_element_type=F32)
    h_ref[...] = y[:, 3 * FOX_W:]
    q = y[:, :FOX_W] * LOG2E
    k, v = y[:, FOX_W:2 * FOX_W], y[:, 2 * FOX_W:3 * FOX_W]
    k_ref[...] = k.astype(BF16)
    if attn_layout:
        q_ref[0] = q.T.astype(BF16)
        k_t, v_t = k.T, v.T
        for hd in range(FOX_HEADS):
            rows = slice(hd * FOX_DIM, (hd + 1) * FOX_DIM)
            k32_ref[0, hd] = k_t[rows, :]
            v32_ref[0, hd] = v_t[rows, :]
            v_ref[0, hd, 0] = v_t[rows, :].astype(BF16)
    else:
        k32_ref[...] = k
        v32_ref[...] = v
        q_ref[...] = q.astype(BF16)
        v_ref[...] = v.astype(BF16)


def _proj_odd(x, w, bsz, tm, attn_layout):
    rows = x.shape[0]
    seq = rows // bsz
    row_spec = lambda n: pl.BlockSpec((tm, n), lambda i: (i, 0))
    if attn_layout:
        assert tm == TILE and seq % TILE == 0
        n_t = seq // TILE
        kv32_shape = jax.ShapeDtypeStruct((bsz, FOX_HEADS, FOX_DIM, seq), F32)
        kv32_spec = pl.BlockSpec((1, FOX_HEADS, FOX_DIM, TILE), lambda i: (i // n_t, 0, 0, i % n_t))
        q_shape = jax.ShapeDtypeStruct((bsz, FOX_W, seq), BF16)
        q_spec = pl.BlockSpec((1, FOX_W, TILE), lambda i: (i // n_t, 0, i % n_t))
        v_shape = jax.ShapeDtypeStruct((bsz, FOX_HEADS, n_t, FOX_DIM, TILE), BF16)
        v_spec = pl.BlockSpec((1, FOX_HEADS, 1, FOX_DIM, TILE), lambda i: (i // n_t, 0, i % n_t, 0, 0))
    else:
        kv32_shape, kv32_spec = jax.ShapeDtypeStruct((rows, FOX_W), F32), row_spec(FOX_W)
        q_shape = v_shape = jax.ShapeDtypeStruct((rows, FOX_W), BF16)
        q_spec = v_spec = row_spec(FOX_W)
    return pl.pallas_call(
        functools.partial(_proj_odd_body, attn_layout=attn_layout),
        grid=(rows // tm,),
        in_specs=[row_spec(D_MODEL), _const_spec(w.shape)],
        out_specs=[row_spec(ODD_H), kv32_spec, kv32_spec, q_spec, row_spec(FOX_W), v_spec],
        out_shape=[jax.ShapeDtypeStruct((rows, ODD_H), F32), kv32_shape, kv32_shape, q_shape,
                   jax.ShapeDtypeStruct((rows, FOX_W), BF16), v_shape],
        compiler_params=_cparams(1, VMEM_LIMIT),
        name="in_proj_odd",
    )(x, w)


def _s5_body(u_ref, h0r_ref, h0i_ref, ar_ref, ai_ref, asr_ref, asi_ref, bbig_ref, cbig_ref, d_ref, wglu_ref,
             bglu_ref, o_ref, hlr_ref, hli_ref, hs_ref, st_ref, *, first_row, last_seg):
    t = pl.program_id(1)
    rt = u_ref.shape[1]
    seg = rt // SUBLANE

    @pl.when(t == 0)
    def _():
        st_ref[0:1, :] = h0r_ref[0]
        st_ref[1:2, :] = h0i_ref[0]

    u = u_ref[0]
    if first_row:
        rows = t * rt + lax.broadcasted_iota(jnp.int32, (rt, 1), 0)
        u = jnp.where(rows >= first_row, u, 0.0)
    i0 = lax.broadcasted_iota(jnp.int32, (rt, rt), 0)
    i1 = lax.broadcasted_iota(jnp.int32, (rt, rt), 1)
    regroup = (i1 == (i0 & (SUBLANE - 1)) * seg + (i0 >> 3)).astype(BF16)
    restore = (i0 == (i1 & (SUBLANE - 1)) * seg + (i1 >> 3)).astype(BF16)
    ub = jnp.dot(regroup, u.astype(BF16), preferred_element_type=F32).astype(BF16)

    half_w, half_s = S5_WIDTH // 2, S5_HALF // 2
    for kb in range(2):
        for part in range(2):
            c0 = part * S5_HALF + kb * half_s
            hs_ref[:, c0:c0 + half_s] = jnp.dot(
                ub[:, kb * half_w:(kb + 1) * half_w], bbig_ref[kb * half_w:(kb + 1) * half_w, c0:c0 + half_s],
                preferred_element_type=F32)

    def scan(lanes, start, store):
        re_l = lanes
        im_l = slice(S5_HALF + lanes.start, S5_HALF + lanes.stop)
        ar = jnp.broadcast_to(ar_ref[:, lanes], (SUBLANE, lanes.stop - lanes.start))
        ai = jnp.broadcast_to(ai_ref[:, lanes], (SUBLANE, lanes.stop - lanes.start))

        def step(k, carry):
            hr, hi = carry
            base = k * SUBLANE
            nr = ar * hr - ai * hi + hs_ref[pl.ds(base, SUBLANE), re_l]
            ni = ar * hi + ai * hr + hs_ref[pl.ds(base, SUBLANE), im_l]
            if store:
                hs_ref[pl.ds(base, SUBLANE), re_l] = nr
                hs_ref[pl.ds(base, SUBLANE), im_l] = ni
            return nr, ni

        carry = start
        for k in range(seg):
            carry = step(k, carry)
        return carry

    slabs = [slice(c * half_s, (c + 1) * half_s) for c in range(2)]
    zero = jnp.zeros((SUBLANE, half_s), F32)
    ends = [scan(lanes, (zero, zero), False) for lanes in slabs]
    er = jnp.concatenate([e[0] for e in ends], axis=1)
    ei = jnp.concatenate([e[1] for e in ends], axis=1)
    asr, asi = asr_ref[...], asi_ref[...]
    sr, si = st_ref[0:1, :], st_ref[1:2, :]
    start_r, start_i = [], []
    for j in range(SUBLANE):
        start_r.append(sr)
        start_i.append(si)
        sr, si = asr * sr - asi * si + er[j:j + 1], asr * si + asi * sr + ei[j:j + 1]
    st_ref[0:1, :] = sr
    st_ref[1:2, :] = si
    start_r8, start_i8 = jnp.concatenate(start_r, axis=0), jnp.concatenate(start_i, axis=0)
    for lanes in slabs:
        scan(lanes, (start_r8[:, lanes], start_i8[:, lanes]), True)

    hb = hs_ref[...].astype(BF16)
    ys = []
    for kb in range(2):
        re_rows = slice(kb * half_s, (kb + 1) * half_s)
        im_rows = slice(S5_HALF + kb * half_s, S5_HALF + (kb + 1) * half_s)
        cols = slice(kb * half_w, (kb + 1) * half_w)
        ys.append(jnp.dot(hb[:, re_rows], cbig_ref[re_rows, cols], preferred_element_type=F32)
                  + jnp.dot(hb[:, im_rows], cbig_ref[im_rows, cols], preferred_element_type=F32))
    y = jnp.concatenate(ys, axis=1)
    y_hi = y.astype(BF16)
    y_lo = (y - y_hi.astype(F32)).astype(BF16)
    y = (jnp.dot(restore, y_hi, preferred_element_type=F32) + jnp.dot(restore, y_lo, preferred_element_type=F32)
         + d_ref[...] * u)
    g = jax.nn.gelu(y)
    gate = jnp.dot(g.astype(BF16), wglu_ref[...], preferred_element_type=F32) + bglu_ref[...]
    o_ref[0] = (g * jax.nn.sigmoid(gate)).astype(o_ref.dtype)

    @pl.when(t == pl.num_programs(1) - 1)
    def _():
        hlr_ref[0] = sr if last_seg == SUBLANE else start_r[last_seg]
        hli_ref[0] = si if last_seg == SUBLANE else start_i[last_seg]


def _s5(h, h0r, h0i, p, rt, first_row, n_real):
    bsz, seq, _ = h.shape
    seg = rt // SUBLANE
    real_in_last = n_real - (seq - rt)
    assert 0 < real_in_last <= rt and real_in_last % seg == 0
    as_re = (jnp.exp(p["lam_dt_re"] * seg) * jnp.cos(p["lam_dt_im"] * seg)).reshape(1, S5_HALF)
    as_im = (jnp.exp(p["lam_dt_re"] * seg) * jnp.sin(p["lam_dt_im"] * seg)).reshape(1, S5_HALF)
    state_spec = pl.BlockSpec((1, 1, S5_HALF), lambda b, t: (b, 0, 0))
    return pl.pallas_call(
        functools.partial(_s5_body, first_row=first_row, last_seg=real_in_last // seg),
        grid=(bsz, seq // rt),
        in_specs=[
            pl.BlockSpec((1, rt, S5_WIDTH), lambda b, t: (b, t, 0)),
            state_spec, state_spec,
            _const_spec((1, S5_HALF)), _const_spec((1, S5_HALF)), _const_spec((1, S5_HALF)), _const_spec((1, S5_HALF)),
            _const_spec((S5_WIDTH, 2 * S5_HALF)), _const_spec((2 * S5_HALF, S5_WIDTH)),
            _const_spec((1, S5_WIDTH)), _const_spec((S5_WIDTH, S5_WIDTH)), _const_spec((1, S5_WIDTH)),
        ],
        out_specs=[pl.BlockSpec((1, rt, S5_WIDTH), lambda b, t: (b, t, 0)), state_spec, state_spec],
        out_shape=[jax.ShapeDtypeStruct((bsz, seq, S5_WIDTH), BF16),
                   jax.ShapeDtypeStruct((bsz, 1, S5_HALF), F32),
                   jax.ShapeDtypeStruct((bsz, 1, S5_HALF), F32)],
        scratch_shapes=[pltpu.VMEM((rt, 2 * S5_HALF), F32), pltpu.VMEM((2, S5_HALF), F32)],
        compiler_params=_cparams(2, VMEM_LIMIT),
        name="s5",
    )(h, h0r, h0i, p["a_re"], p["a_im"], as_re, as_im, p["bbig"], p["cbig"], p["d"], p["w_glu"], p["b_glu"])


def _rope_lanes(x, c, s1, s2, half):
    n = x.shape[-1]
    return x * c + pltpu.roll(x, n - half, 1) * s1 + pltpu.roll(x, half, 1) * s2


def _rms_rows(x, g):
    return x * lax.rsqrt(jnp.mean(x * x, axis=-1, keepdims=True) + EPS) * g


def _mla_latents(h, kn_ref, ck_ref, s1k_ref, s2k_ref, ckv_ref, kpe_ref):
    ckv = _rms_rows(h[:, MLA_Q_RANK:MLA_Q_RANK + MLA_KV_RANK], kn_ref[...])
    kpe = _rope_lanes(h[:, MLA_Q_RANK + MLA_KV_RANK:], ck_ref[...], s1k_ref[...], s2k_ref[...],
                      MLA_ROPE // 2)[:, :MLA_ROPE]
    ckv_ref[0] = ckv
    kpe_ref[0] = kpe
    return ckv, kpe


def _mla_rows_body(h_ref, qn_ref, kn_ref, wq_ref, cq_ref, s1q_ref, s2q_ref, ck_ref, s1k_ref, s2k_ref,
                   q_ref, ckv_ref, kpe_ref):
    h = h_ref[0]
    q_lat = _rms_rows(h[:, :MLA_Q_RANK], qn_ref[...]).astype(BF16)
    q = jnp.dot(q_lat, wq_ref[...], preferred_element_type=F32)
    cq, s1q, s2q = cq_ref[...], s1q_ref[...], s2q_ref[...]
    for hd in range(MLA_HEADS):
        sl = slice(hd * MLA_QK_PAD, (hd + 1) * MLA_QK_PAD)
        q_ref[0, :, sl] = _rope_lanes(q[:, sl], cq, s1q, s2q, MLA_ROPE // 2).astype(BF16)
    _mla_latents(h, kn_ref, ck_ref, s1k_ref, s2k_ref, ckv_ref, kpe_ref)


def _even_rows_body(x_ref, win_ref, qn_ref, kn_ref, wqt_ref, cos_ref, sin_ref, ck_ref, s1k_ref, s2k_ref,
                    wkn_ref, wkp_ref, wv_ref, u_ref, q_ref, ckv_ref, kpe_ref, k_ref, v_ref, *, scale):
    y = jnp.dot(x_ref[0].astype(BF16), win_ref[...], preferred_element_type=F32)
    u_ref[0] = y[:, :S5_WIDTH]
    h = y[:, S5_WIDTH:]
    q_lat = _rms_rows(h[:, :MLA_Q_RANK], qn_ref[...]).astype(BF16)
    q_t = _dot_nt(wqt_ref[...], q_lat)
    cos_t, sin_t = cos_ref[...], sin_ref[...]
    half = MLA_ROPE // 2
    for hd in range(MLA_HEADS):
        r0 = hd * MLA_QK_PAD
        x1 = q_t[r0 + MLA_NOPE:r0 + MLA_NOPE + half, :]
        x2 = q_t[r0 + MLA_NOPE + half:r0 + MLA_NOPE + 2 * half, :]
        q_ref[0, r0:r0 + MLA_QK_PAD, :] = jnp.concatenate(
            [q_t[r0:r0 + MLA_NOPE, :] * scale, x1 * cos_t - x2 * sin_t, x1 * sin_t + x2 * cos_t,
             q_t[r0 + MLA_NOPE + 2 * half:r0 + MLA_QK_PAD, :]], axis=0).astype(BF16)
    ckv, kpe = _mla_latents(h, kn_ref, ck_ref, s1k_ref, s2k_ref, ckv_ref, kpe_ref)
    ckv = ckv.astype(BF16)
    k = jnp.dot(ckv, wkn_ref[...], preferred_element_type=F32)
    k = k + jnp.dot(kpe.astype(BF16), wkp_ref[...], preferred_element_type=F32)
    k_ref[0] = k.astype(BF16)
    v_t = jnp.dot(ckv, wv_ref[...], preferred_element_type=F32).T
    for hd in range(MLA_HEADS):
        v_ref[0, hd, 0] = v_t[hd * MLA_V:(hd + 1) * MLA_V, :].astype(BF16)


def _mla_scale():
    return (MLA_NOPE + MLA_ROPE) ** -0.5 * LOG2E


def _even_rows(x, p, pos):
    bsz, seq, _ = x.shape
    n_t = seq // TILE
    n_q = MLA_HEADS * MLA_QK_PAD
    half = MLA_ROPE // 2
    scale = _mla_scale()
    ang = (ROPE_BASE ** (-np.arange(half, dtype=np.float32) / half))[:, None] * pos.astype(np.float32)[None, :]
    q_tabs = ((np.cos(ang) * scale).astype(np.float32), (np.sin(ang) * scale).astype(np.float32))
    row_spec = lambda n: pl.BlockSpec((1, TILE, n), lambda b, t: (b, t, 0))
    consts = [p["w_in"], p["q_norm"], p["kv_norm"], p["wq"].T]
    weights = [p["wkn"], p["wkp"], p["wv"]]
    return pl.pallas_call(
        functools.partial(_even_rows_body, scale=scale),
        grid=(bsz, n_t),
        in_specs=[row_spec(D_MODEL)] + [_const_spec(c.shape) for c in consts]
        + [pl.BlockSpec((half, TILE), lambda b, t: (0, t))] * 2
        + [pl.BlockSpec((TILE, LANE), lambda b, t: (t, 0))] * 3 + [_const_spec(c.shape) for c in weights],
        out_specs=[row_spec(S5_WIDTH), pl.BlockSpec((1, n_q, TILE), lambda b, t: (b, 0, t)),
                   row_spec(MLA_KV_RANK), row_spec(MLA_ROPE), row_spec(n_q),
                   pl.BlockSpec((1, MLA_HEADS, 1, MLA_V, TILE), lambda b, t: (b, 0, t, 0, 0))],
        out_shape=[jax.ShapeDtypeStruct((bsz, seq, S5_WIDTH), F32),
                   jax.ShapeDtypeStruct((bsz, n_q, seq), BF16),
                   jax.ShapeDtypeStruct((bsz, seq, MLA_KV_RANK), F32),
                   jax.ShapeDtypeStruct((bsz, seq, MLA_ROPE), F32),
                   jax.ShapeDtypeStruct((bsz, seq, n_q), BF16),
                   jax.ShapeDtypeStruct((bsz, MLA_HEADS, n_t, MLA_V, TILE), BF16)],
        compiler_params=_cparams(2, VMEM_LIMIT),
        name="even_rows",
    )(x, *consts, *q_tabs, *_rope_tables(pos, half, LANE, LANE, 0), *weights)


def _mla_rows(h, p, pos, tm):
    bsz, seq, _ = h.shape
    tab_spec = pl.BlockSpec((tm, LANE), lambda b, t: (t, 0))
    n_q = MLA_HEADS * MLA_QK_PAD
    half = MLA_ROPE // 2
    tabs = (_rope_tables(pos, half, MLA_QK_PAD, MLA_QK_PAD, MLA_NOPE, scale=_mla_scale())
            + _rope_tables(pos, half, LANE, LANE, 0))
    return pl.pallas_call(
        _mla_rows_body,
        grid=(bsz, seq // tm),
        in_specs=[
            pl.BlockSpec((1, tm, EVEN_IN_PAD - S5_WIDTH), lambda b, t: (b, t, 1)),
            _const_spec((1, MLA_Q_RANK)), _const_spec((1, MLA_KV_RANK)),
            _const_spec(p["wq"].shape),
        ] + [tab_spec] * 6,
        out_specs=[
            pl.BlockSpec((1, tm, n_q), lambda b, t: (b, t, 0)),
            pl.BlockSpec((1, tm, MLA_KV_RANK), lambda b, t: (b, t, 0)),
            pl.BlockSpec((1, tm, MLA_ROPE), lambda b, t: (b, t, 0)),
        ],
        out_shape=[jax.ShapeDtypeStruct((bsz, seq, n_q), BF16),
                   jax.ShapeDtypeStruct((bsz, seq, MLA_KV_RANK), F32),
                   jax.ShapeDtypeStruct((bsz, seq, MLA_ROPE), F32)],
        compiler_params=_cparams(2, VMEM_LIMIT),
        name="mla_rows",
    )(h, p["q_norm"], p["kv_norm"], p["wq"], *tabs)


def _flash_body(*refs, heads, dq, dv, causal):
    if causal:
        qt_ref, k_ref, vt_ref, fq_ref, fk_ref, o_ref = refs
    else:
        qt_ref, k_ref, vt_ref, o_ref = refs
    tq = qt_ref.shape[2]
    n_tiles, tk = vt_ref.shape[2], vt_ref.shape[4]
    r0 = pl.program_id(1) * tq
    j_last = jnp.minimum((r0 + tq - 1) // tk, n_tiles - 1)
    q_row = r0 + lax.broadcasted_iota(jnp.int32, (1, tq), 1)
    last = q_row if causal else ((q_row >> CHUNK_SHIFT) << CHUNK_SHIFT) + (CHUNK - 1)
    first_last = r0 if causal else ((r0 >> CHUNK_SHIFT) << CHUNK_SHIFT) + (CHUNK - 1)
    n_open = jnp.clip((first_last + 1) // tk, 1, j_last + 1)
    q_t = [qt_ref[0, hd * dq:(hd + 1) * dq, :] for hd in range(heads)]

    def tile(j, carry, masked, wide):
        width = 2 * tk if wide else tk
        start = pl.multiple_of(j * tk, tk)
        if masked:
            k_row = start + lax.broadcasted_iota(jnp.int32, (width, 1), 0)
            ok = (k_row <= last) & (k_row >= PAD)
        scores = [jnp.dot(k_ref[0, pl.ds(start, width), hd * dq:(hd + 1) * dq], q_t[hd],
                          preferred_element_type=F32) for hd in range(heads)]
        probs = []
        for hd in range(heads):
            m, l, _ = carry[hd]
            s = scores[hd]
            if causal:
                s = s + (fq_ref[0, hd:hd + 1, :] - fk_ref[0, pl.ds(start, width), hd:hd + 1])
            if masked:
                s = jnp.where(ok, s, NEG)
            m_new = jnp.maximum(m, jnp.max(s, axis=0, keepdims=True))
            a = jnp.exp2(m - m_new)
            p = jnp.exp2(s - m_new)
            probs.append((m_new, a * l + jnp.sum(p, axis=0, keepdims=True), a, p.astype(BF16)))
        new = []
        for hd in range(heads):
            m_new, l, a, p = probs[hd]
            acc = a * carry[hd][2] + jnp.dot(vt_ref[0, hd, j], p[:tk], preferred_element_type=F32)
            if wide:
                acc = acc + jnp.dot(vt_ref[0, hd, j + 1], p[tk:], preferred_element_type=F32)
            new.append((m_new, l, acc))
        return tuple(new)

    init = tuple((jnp.full((1, tq), NEG, F32), jnp.zeros((1, tq), F32), jnp.zeros((dv, tq), F32))
                 for _ in range(heads))
    carry = tile(0, init, True, False)
    n_wide = (n_open - 1) // 2
    carry = lax.fori_loop(0, n_wide, lambda i, c: tile(1 + 2 * i, c, False, True), carry)
    carry = lax.fori_loop(1 + 2 * n_wide, n_open, functools.partial(tile, masked=False, wide=False), carry)
    carry = lax.fori_loop(n_open, j_last + 1, functools.partial(tile, masked=True, wide=False), carry)
    outs = [carry[hd][2] / carry[hd][1] for hd in range(heads)]
    per = LANE // dv
    for g in range(heads // per):
        o_ref[0, :, g * LANE:(g + 1) * LANE] = jnp.concatenate(
            outs[g * per:(g + 1) * per], axis=0).T.astype(o_ref.dtype)


def _flash(q_t, k, v_t, fq_t, fk, *, heads, dq, dv, causal):
    bsz, _, n_q = q_t.shape
    n_keys = k.shape[1]
    in_specs = [
        pl.BlockSpec((1, heads * dq, TILE), lambda b, i: (b, 0, i)),
        pl.BlockSpec((1, n_keys, heads * dq), lambda b, i: (b, 0, 0)),
        pl.BlockSpec((1,) + v_t.shape[1:], lambda b, i: (b, 0, 0, 0, 0)),
    ]
    args = [q_t, k, v_t]
    if causal:
        in_specs += [pl.BlockSpec((1, heads, TILE), lambda b, i: (b, 0, i)),
                     pl.BlockSpec((1, n_keys, heads), lambda b, i: (b, 0, 0))]
        args += [fq_t, fk]
    return pl.pallas_call(
        functools.partial(_flash_body, heads=heads, dq=dq, dv=dv, causal=causal),
        grid=(bsz, n_q // TILE),
        in_specs=in_specs,
        out_specs=pl.BlockSpec((1, TILE, heads * dv), lambda b, i: (b, i, 0)),
        out_shape=jax.ShapeDtypeStruct((bsz, n_q, heads * dv), BF16),
        compiler_params=_cparams(2, VMEM_LIMIT),
        name="flash_causal" if causal else "flash_chunk",
    )(*args)


def _online_softmax(s, m_ref, l_ref, idx):
    m_old = m_ref[idx]
    m_new = jnp.maximum(m_old, jnp.max(s, axis=-1, keepdims=True))
    a = jnp.exp2(m_old - m_new)
    p = jnp.exp2(s - m_new)
    m_ref[idx] = m_new
    l_ref[idx] = a * l_ref[idx] + jnp.sum(p, axis=-1, keepdims=True)
    return a, p


def _dot_nt(a, b):
    return lax.dot_general(a, b, (((1,), (1,)), ((), ())), preferred_element_type=F32)


def _fox_decode_body(q_ref, fq_ref, kc_ref, vc_ref, fkc_ref, kn_ref, vn_ref, fkn_ref, o_ref, m_ref, l_ref, acc_ref):
    t = pl.program_id(1)
    n_q = q_ref.shape[1]
    cols = [slice(hd * FOX_DIM, (hd + 1) * FOX_DIM) for hd in range(FOX_HEADS)]

    @pl.when(t == 0)
    def _():
        m_ref[...] = jnp.full(m_ref.shape, NEG, F32)
        l_ref[...] = jnp.zeros(l_ref.shape, F32)
        acc_ref[...] = jnp.zeros(acc_ref.shape, F32)

    def attend(scores, fk_rows, values, ok):
        probs = []
        for hd in range(FOX_HEADS):
            s = scores[hd] + (fq_ref[0, :, hd:hd + 1] - fk_rows[hd:hd + 1, :])
            if ok is not None:
                s = jnp.where(ok, s, NEG)
            probs.append(_online_softmax(s, m_ref, l_ref, hd))
        for hd in range(FOX_HEADS):
            a, p = probs[hd]
            acc_ref[hd] = a * acc_ref[hd] + values(hd, p.astype(BF16))

    attend([jnp.dot(q_ref[0, :, cols[hd]], kc_ref[0, 0, hd].astype(BF16), preferred_element_type=F32)
            for hd in range(FOX_HEADS)],
           fkc_ref[0], lambda hd, p: _dot_nt(p, vc_ref[0, 0, hd].astype(BF16)), None)

    @pl.when(t == pl.num_programs(1) - 1)
    def _():
        causal = (lax.broadcasted_iota(jnp.int32, (n_q, n_q), 1) <= lax.broadcasted_iota(jnp.int32, (n_q, n_q), 0))
        attend([_dot_nt(q_ref[0, :, sl], kn_ref[0, :, sl]) for sl in cols], fkn_ref[0],
               lambda hd, p: jnp.dot(p, vn_ref[0, :, cols[hd]], preferred_element_type=F32), causal)
        for hd in range(FOX_HEADS):
            o_ref[0, :, cols[hd]] = (acc_ref[hd] / l_ref[hd]).astype(o_ref.dtype)


def _fox_decode(q, k_new, v_new, cache_k, cache_v, layer, fcum, tk):
    bsz, n_q, width = q.shape
    n_past = cache_k.shape[2]
    rows_minor = lambda c: c.transpose(0, 1, 3, 4, 2)
    fk_t = fcum.transpose(0, 2, 1)
    row_spec = pl.BlockSpec((1, n_q, width), lambda b, t: (b, 0, 0))
    cache_spec = pl.BlockSpec((1, 1, FOX_HEADS, FOX_DIM, tk), lambda b, t: (layer, b, 0, 0, t))
    return pl.pallas_call(
        _fox_decode_body,
        grid=(bsz, n_past // tk),
        in_specs=[row_spec, pl.BlockSpec((1, n_q, FOX_HEADS), lambda b, t: (b, 0, 0)),
                  cache_spec, cache_spec, pl.BlockSpec((1, FOX_HEADS, tk), lambda b, t: (b, 0, t)),
                  row_spec, row_spec, pl.BlockSpec((1, FOX_HEADS, n_q), lambda b, t: (b, 0, 0))],
        out_specs=row_spec,
        out_shape=jax.ShapeDtypeStruct((bsz, n_q, width), BF16),
        scratch_shapes=[pltpu.VMEM((FOX_HEADS, n_q, 1), F32), pltpu.VMEM((FOX_HEADS, n_q, 1), F32),
                        pltpu.VMEM((FOX_HEADS, n_q, FOX_DIM), F32)],
        compiler_params=_cparams(2, VMEM_LIMIT),
        name="fox_decode",
    )(q, fcum[:, n_past:n_past + n_q], rows_minor(cache_k), rows_minor(cache_v), fk_t[:, :, :n_past],
      k_new, v_new, fk_t[:, :, n_past:n_past + n_q])


def _mla_decode_body(q_ref, wkn_ref, wv_ref, cc_ref, pc_ref, cn_ref, pn_ref, o_ref, qa_ref, qr_ref,
                     m_ref, l_ref, acc_ref):
    t = pl.program_id(1)
    n_q = q_ref.shape[1]

    @pl.when(t == 0)
    def _():
        for hd in range(MLA_HEADS):
            c0 = hd * MLA_QK_PAD
            rows = slice(hd * n_q, (hd + 1) * n_q)
            qa_ref[rows, :] = _dot_nt(q_ref[0, :, c0:c0 + MLA_NOPE],
                                      wkn_ref[:, c0:c0 + MLA_NOPE]).astype(BF16)
            qr_ref[rows, :] = q_ref[0, :, c0 + MLA_NOPE:c0 + MLA_NOPE + MLA_ROPE]
        m_ref[...] = jnp.full(m_ref.shape, NEG, F32)
        l_ref[...] = jnp.zeros(l_ref.shape, F32)
        acc_ref[...] = jnp.zeros(acc_ref.shape, F32)

    def attend(ckv, rope_scores):
        a, p = _online_softmax(_dot_nt(qa_ref[...], ckv) + rope_scores, m_ref, l_ref, 0)
        acc_ref[0] = a * acc_ref[0] + jnp.dot(p.astype(BF16), ckv, preferred_element_type=F32)

    attend(cc_ref[0, 0].astype(BF16),
           jnp.dot(qr_ref[...], pc_ref[0, 0].astype(BF16), preferred_element_type=F32))

    @pl.when(t == pl.num_programs(1) - 1)
    def _():
        attend(cn_ref[0].astype(BF16), _dot_nt(qr_ref[...], pn_ref[0].astype(BF16)))
        lat = (acc_ref[0] / l_ref[0]).astype(BF16)
        for hd in range(MLA_HEADS):
            cols = slice(hd * MLA_V, (hd + 1) * MLA_V)
            o_ref[0, :, cols] = jnp.dot(lat[hd * n_q:(hd + 1) * n_q, :], wv_ref[:, cols],
                                        preferred_element_type=F32).astype(o_ref.dtype)


def _mla_decode(q, ckv_new, kpe_new, cache_ckv, cache_kpe, layer, p, tk):
    bsz, n_q, _ = q.shape
    n_past = cache_ckv.shape[2]
    stacked = MLA_HEADS * n_q
    return pl.pallas_call(
        _mla_decode_body,
        grid=(bsz, n_past // tk),
        in_specs=[pl.BlockSpec((1, n_q, MLA_HEADS * MLA_QK_PAD), lambda b, t: (b, 0, 0)),
                  _const_spec(p["wkn"].shape), _const_spec(p["wv"].shape),
                  pl.BlockSpec((1, 1, tk, MLA_KV_RANK), lambda b, t: (layer, b, t, 0)),
                  pl.BlockSpec((1, 1, MLA_ROPE, tk), lambda b, t: (layer, b, 0, t)),
                  pl.BlockSpec((1, n_q, MLA_KV_RANK), lambda b, t: (b, 0, 0)),
                  pl.BlockSpec((1, n_q, MLA_ROPE), lambda b, t: (b, 0, 0))],
        out_specs=pl.BlockSpec((1, n_q, MLA_HEADS * MLA_V), lambda b, t: (b, 0, 0)),
        out_shape=jax.ShapeDtypeStruct((bsz, n_q, MLA_HEADS * MLA_V), BF16),
        scratch_shapes=[pltpu.VMEM((stacked, MLA_KV_RANK), BF16), pltpu.VMEM((stacked, MLA_ROPE), BF16),
                        pltpu.VMEM((1, stacked, 1), F32), pltpu.VMEM((1, stacked, 1), F32),
                        pltpu.VMEM((1, stacked, MLA_KV_RANK), F32)],
        compiler_params=_cparams(2, VMEM_LIMIT),
        name="mla_decode",
    )(q, p["wkn"], p["wv"], cache_ckv, cache_kpe.transpose(0, 1, 3, 2), ckv_new, kpe_new)


def _gate_body(x_ref, b_ref, lf_ref, fc_ref, *, new_start, first_row):
    n_rows = x_ref.shape[1]
    tri = (lax.broadcasted_iota(jnp.int32, (TILE, TILE), 0)
           >= lax.broadcasted_iota(jnp.int32, (TILE, TILE), 1)).astype(F32)
    carry = jnp.zeros((1, x_ref.shape[2]), F32)
    for i in range(n_rows // TILE):
        sl = slice(i * TILE, (i + 1) * TILE)
        x = x_ref[0, sl, :]
        rows = i * TILE + lax.broadcasted_iota(jnp.int32, (TILE, 1), 0)
        z = x + b_ref[...]
        lf = jnp.where(rows >= new_start, jnp.minimum(z, 0.0) - jnp.log1p(jnp.exp(-jnp.abs(z))), x)
        if first_row:
            lf = jnp.where(rows >= first_row, lf, 0.0)
        lf_ref[0, sl, :] = lf
        cs = jnp.dot(tri, lf, preferred_element_type=F32, precision=lax.Precision.HIGHEST) + carry
        fc_ref[0, sl, :] = cs
        carry = cs[TILE - 1:TILE, :]


def _gate(x, b_f, new_start, first_row):
    bsz, n_rows, heads = x.shape
    spec = pl.BlockSpec((1, n_rows, heads), lambda b: (b, 0, 0))
    return pl.pallas_call(
        functools.partial(_gate_body, new_start=new_start, first_row=first_row),
        grid=(bsz,),
        in_specs=[spec, _const_spec((1, heads))],
        out_specs=[spec, spec],
        out_shape=[jax.ShapeDtypeStruct(x.shape, F32)] * 2,
        compiler_params=_cparams(1),
        name="fox_gate",
    )(x, b_f)


RET_LOG_GAMMA = tuple(math.log(1.0 - 2.0 ** (-5.0 - h)) for h in range(RET_HEADS))


def _ret_body(rq_ref, rk_ref, rv_ref, rg_ref, s0_ref, dec_ref, cq_ref, s1q_ref, s2q_ref, ck_ref, s1k_ref, s2k_ref,
              o_ref, sl_ref, st_ref, *, n_tail):
    c = pl.program_id(1)
    ct = rq_ref.shape[1]

    @pl.when(c == 0)
    def _():
        st_ref[...] = s0_ref[0]

    q = _rope_lanes(rq_ref[0], cq_ref[...], s1q_ref[...], s2q_ref[...], RET_DK // 2)
    k = _rope_lanes(rk_ref[0], ck_ref[...], s1k_ref[...], s2k_ref[...], RET_DK // 2)
    v = rv_ref[0].astype(BF16)
    g = rg_ref[0]
    j = lax.broadcasted_iota(jnp.int32, (ct, 1), 0).astype(F32)
    for hd in range(RET_HEADS):
        lg = RET_LOG_GAMMA[hd]
        q_h = q[:, hd * RET_DK:(hd + 1) * RET_DK]
        k_h = k[:, hd * RET_DK:(hd + 1) * RET_DK]
        v_h = v[:, hd * RET_DV:(hd + 1) * RET_DV]
        scores = lax.dot_general(q_h.astype(BF16), k_h.astype(BF16), (((1,), (1,)), ((), ())),
                                 preferred_element_type=F32) * dec_ref[hd]
        s_h = st_ref[hd]
        out = jnp.dot(scores.astype(BF16), v_h, preferred_element_type=F32)
        out = out + jnp.dot((q_h * jnp.exp(lg * (j + 1.0))).astype(BF16), s_h.astype(BF16),
                            preferred_element_type=F32)
        k_dec = (k_h * jnp.exp(lg * (ct - 1.0 - j))).astype(BF16)
        st_ref[hd] = math.exp(lg * ct) * s_h + lax.dot_general(
            k_dec, v_h, (((0,), (0,)), ((), ())), preferred_element_type=F32)
        mu = jnp.mean(out, axis=-1, keepdims=True)
        oc = out - mu
        var = jnp.mean(oc * oc, axis=-1, keepdims=True)
        g_h = g[:, hd * RET_DV:(hd + 1) * RET_DV]
        o_ref[0, :, hd * RET_DV:(hd + 1) * RET_DV] = (
            g_h * jax.nn.sigmoid(g_h) * (oc * lax.rsqrt(var + EPS))).astype(o_ref.dtype)

    @pl.when(c == pl.num_programs(1) - 1)
    def _():
        for hd in range(RET_HEADS):
            sl_ref[0, hd] = st_ref[hd] * math.exp(-RET_LOG_GAMMA[hd] * n_tail)


def _retention(h, s0, tabs, ct, n_tail):
    bsz, seq, _ = h.shape
    tab_spec = pl.BlockSpec((ct, RET_QK), lambda b, c: (c, 0))
    st_spec = pl.BlockSpec((1, RET_HEADS, RET_DK, RET_DV), lambda b, c: (b, 0, 0, 0))
    rq_blk = 0
    rv_blk = (2 * RET_QK) // RET_VW
    diff = np.arange(ct, dtype=np.float32)[:, None] - np.arange(ct, dtype=np.float32)[None, :]
    decay = np.where(diff >= 0.0, np.exp(np.asarray(RET_LOG_GAMMA, np.float32)[:, None, None] * np.maximum(diff, 0.0)),
                     np.float32(0.0)).astype(np.float32)
    return pl.pallas_call(
        functools.partial(_ret_body, n_tail=n_tail),
        grid=(bsz, seq // ct),
        in_specs=[
            pl.BlockSpec((1, ct, RET_QK), lambda b, c: (b, c, rq_blk)),
            pl.BlockSpec((1, ct, RET_QK), lambda b, c: (b, c, rq_blk + 1)),
            pl.BlockSpec((1, ct, RET_VW), lambda b, c: (b, c, rv_blk)),
            pl.BlockSpec((1, ct, RET_VW), lambda b, c: (b, c, rv_blk + 1)),
            st_spec, _const_spec((RET_HEADS, ct, ct)),
        ] + [tab_spec] * 6,
        out_specs=[pl.BlockSpec((1, ct, RET_VW), lambda b, c: (b, c, 0)), st_spec],
        out_shape=[jax.ShapeDtypeStruct((bsz, seq, RET_VW), BF16),
                   jax.ShapeDtypeStruct((bsz, RET_HEADS, RET_DK, RET_DV), F32)],
        scratch_shapes=[pltpu.VMEM((RET_HEADS, RET_DK, RET_DV), F32)],
        compiler_params=_cparams(2, VMEM_LIMIT),
        name="retention",
    )(h, h, h, h, s0, decay, *tabs)


def _rope_tables(pos, half, width, group, offset, scale=1.0, valid=None):
    f32 = np.float32
    inv = f32(ROPE_BASE) ** (-np.arange(half, dtype=f32) / f32(half))
    ang = pos.astype(f32)[:, None] * inv[None, :]
    cos, sin = np.cos(ang), np.sin(ang)
    n = pos.shape[0]
    one, zero = np.ones((n, 1), f32), np.zeros((n, 1), f32)

    def lanes(first, second, other):
        grp = np.concatenate([np.broadcast_to(other, (n, offset)), first, second,
                              np.broadcast_to(other, (n, group - offset - 2 * half))], axis=1)
        return np.tile(grp, (1, width // group))

    tabs = (lanes(cos, cos, one), lanes(-sin, 0.0 * sin, zero), lanes(0.0 * sin, sin, zero))
    if valid is not None:
        tabs = tuple(np.where(valid[:, None], t, f32(0.0)) for t in tabs)
    return tuple((t * f32(scale)).astype(f32) for t in tabs)


def _even_params(e, w):
    p = {}
    p["w_in"] = jnp.pad(w["even_w_in"][e], ((0, 0), (0, EVEN_IN_PAD - EVEN_IN))).astype(BF16)
    p["w_out"] = w["even_w_out"][e].astype(BF16)
    lam_re, lam_im = w["s5_a_re"][e].astype(F32), w["s5_a_im"][e].astype(F32)
    dt = jnp.exp(w["s5_log_dt"][e].astype(F32))[:, None]
    p["lam_dt_re"], p["lam_dt_im"] = lam_re * dt, lam_im * dt
    mag = jnp.exp(lam_re * dt)
    abar_re, abar_im = mag * jnp.cos(lam_im * dt), mag * jnp.sin(lam_im * dt)
    den = lam_re * lam_re + lam_im * lam_im
    f_re = ((abar_re - 1.0) * lam_re + abar_im * lam_im) / den
    f_im = (abar_im * lam_re - (abar_re - 1.0) * lam_im) / den
    b_re, b_im = w["s5_b_re"][e].astype(F32), w["s5_b_im"][e].astype(F32)
    bb_re = f_re[..., None] * b_re - f_im[..., None] * b_im
    bb_im = f_re[..., None] * b_im + f_im[..., None] * b_re
    eye = jnp.eye(S5_GROUPS, dtype=F32)

    def in_blocks(x):
        return jnp.einsum("gnc,gh->gchn", x, eye).reshape(S5_WIDTH, S5_HALF)

    def out_blocks(x):
        return jnp.einsum("gcn,gh->gnhc", x, eye).reshape(S5_HALF, S5_WIDTH)

    p["bbig"] = jnp.concatenate([in_blocks(bb_re), in_blocks(bb_im)], axis=1).astype(BF16)
    p["cbig"] = jnp.concatenate([out_blocks(w["s5_c_re"][e].astype(F32)),
                                 out_blocks(-w["s5_c_im"][e].astype(F32))], axis=0).astype(BF16)
    p["a_re"] = abar_re.reshape(1, S5_HALF)
    p["a_im"] = abar_im.reshape(1, S5_HALF)
    p["d"] = w["s5_d"][e].astype(F32).reshape(1, S5_WIDTH)
    p["w_glu"] = w["s5_w_glu"][e].astype(BF16)
    p["b_glu"] = w["s5_b_glu"][e].astype(F32).reshape(1, S5_WIDTH)
    p["q_norm"] = w["mla_q_norm"][e].astype(F32).reshape(1, MLA_Q_RANK)
    p["kv_norm"] = w["mla_kv_norm"][e].astype(F32).reshape(1, MLA_KV_RANK)
    wq = w["mla_w_uq"][e].reshape(MLA_Q_RANK, MLA_HEADS, MLA_NOPE + MLA_ROPE)
    wq = jnp.pad(wq, ((0, 0), (0, 0), (0, MLA_QK_PAD - MLA_NOPE - MLA_ROPE)))
    p["wq"] = wq.reshape(MLA_Q_RANK, MLA_HEADS * MLA_QK_PAD).astype(BF16)
    wkv = w["mla_w_ukv"][e].reshape(MLA_KV_RANK, MLA_HEADS, MLA_NOPE + MLA_V)
    wkn = jnp.pad(wkv[:, :, :MLA_NOPE], ((0, 0), (0, 0), (0, MLA_QK_PAD - MLA_NOPE)))
    p["wkn"] = wkn.reshape(MLA_KV_RANK, MLA_HEADS * MLA_QK_PAD).astype(BF16)
    place = jnp.pad(jnp.eye(MLA_ROPE, dtype=F32), ((0, 0), (MLA_NOPE, MLA_QK_PAD - MLA_NOPE - MLA_ROPE)))
    p["wkp"] = jnp.tile(place, (1, MLA_HEADS)).astype(BF16)
    p["wv"] = wkv[:, :, MLA_NOPE:].reshape(MLA_KV_RANK, MLA_HEADS * MLA_V).astype(BF16)
    return p


def _odd_params(o, w):
    w_in = w["odd_w_in"][o]
    c_logit = 3 * FOX_W
    cols = jnp.concatenate([
        w_in[:, :FOX_W] * (FOX_DIM ** -0.5),
        w_in[:, FOX_W:c_logit],
        w_in[:, c_logit + FOX_HEADS:],
        w_in[:, c_logit:c_logit + FOX_HEADS],
    ], axis=1)
    p = {"w_in": jnp.pad(cols, ((0, 0), (0, ODD_IN_PAD - cols.shape[1]))).astype(BF16)}
    p["w_out"] = w["odd_w_out"][o].astype(BF16)
    p["b_f"] = w["fox_b_f"][o].astype(F32).reshape(1, FOX_HEADS)
    return p


def _cache_tile(n_past):
    for tk in (512, 256, 128):
        if n_past % tk == 0:
            return tk
    raise ValueError("cache length must be a multiple of 128")


def _trunk(x, pos, n_real, out_rows, past, w, ffn, evens, odds):
    bsz, seq, _ = x.shape
    rows = bsz * seq
    prompt = past is None
    first_row = PAD if prompt else 0
    if prompt:
        tm = 2 * TILE if rows % (2 * TILE) == 0 else TILE
        rt = TILE
    else:
        n_past = past["cache_fox_k"].shape[2]
        tk = _cache_tile(n_past)
        tm, rt = rows, seq
    idx = np.arange(seq)
    valid = (idx >= first_row) & (idx < n_real)
    st = {n: [] for n in ("mla_ckv", "mla_kpe", "s5_re", "s5_im", "fox_k", "fox_v", "fox_logf", "ret")}
    rq_tabs = _rope_tables(pos, RET_DK // 2, RET_QK, RET_DK, 0)
    rk_tabs = _rope_tables(pos, RET_DK // 2, RET_QK, RET_DK, 0, scale=RET_DK ** -0.5, valid=valid)
    x2 = x.reshape(rows, D_MODEL)

    def ln(l, i):
        return w["ln_g"][l, i].reshape(1, D_MODEL), w["ln_b"][l, i].reshape(1, D_MODEL)

    for l in range(DEPTH):
        x2 = _ffn_ln(x2, ffn, (l, 0), *ln(l, 0), tm)
        if l % 2 == 0:
            e = l // 2
            p = evens[e]
            if prompt:
                u, q, ckv, kpe, k_att, v_att = _even_rows(x2.reshape(bsz, seq, D_MODEL), p, pos)
                h0r = h0i = jnp.zeros((bsz, 1, S5_HALF), F32)
                mla_out = _flash(q, k_att, v_att, None, None, heads=MLA_HEADS, dq=MLA_QK_PAD, dv=MLA_V,
                                 causal=False)
            else:
                u = _proj(x2, p["w_in"], tm).reshape(bsz, seq, EVEN_IN_PAD)
                h0r = past["state_s5_re"][e].astype(F32).reshape(bsz, 1, S5_HALF)
                h0i = past["state_s5_im"][e].astype(F32).reshape(bsz, 1, S5_HALF)
                q, ckv, kpe = _mla_rows(u, p, pos, rt)
                mla_out = _mla_decode(q, ckv, kpe, past["cache_mla_ckv"], past["cache_mla_kpe"], e, p, tk)
            s5_out, hlr, hli = _s5(u, h0r, h0i, p, rt, first_row, n_real)
            mix = (s5_out.reshape(rows, S5_WIDTH), mla_out.reshape(rows, MLA_HEADS * MLA_V), p["w_out"])
            st["mla_ckv"].append(ckv)
            st["mla_kpe"].append(kpe)
            st["s5_re"].append(hlr.reshape(bsz, S5_GROUPS, S5_STATE))
            st["s5_im"].append(hli.reshape(bsz, S5_GROUPS, S5_STATE))
        else:
            o = l // 2
            p = odds[o]
            h, k32, v32, fq, fk16, fv = _proj_odd(x2, p["w_in"], bsz, TILE if prompt else tm, attn_layout=prompt)
            h = h.reshape(bsz, seq, ODD_H)
            fk16 = fk16.reshape(bsz, seq, FOX_W)
            f_logit = h[:, :, ODD_LOGIT_COL:ODD_LOGIT_COL + FOX_HEADS]
            if prompt:
                q_off = 0
                logf, fcum = _gate(f_logit, p["b_f"], 0, PAD)
                fcum = fcum * LOG2E
                fox_out = _flash(fq, fk16, fv, fcum.transpose(0, 2, 1), fcum, heads=FOX_HEADS, dq=FOX_DIM,
                                 dv=FOX_DIM, causal=True)
                s0 = jnp.zeros((bsz, RET_HEADS, RET_DK, RET_DV), F32)
            else:
                q_off = n_past
                gates = jnp.concatenate([past["cache_fox_logf"][o].astype(F32), f_logit], axis=1)
                gates = jnp.pad(gates, ((0, 0), (0, _round_up(n_past + seq, TILE) - n_past - seq), (0, 0)))
                logf, fcum = _gate(gates, p["b_f"], n_past, 0)
                fox_out = _fox_decode(fq.reshape(bsz, seq, FOX_W), fk16, fv.reshape(bsz, seq, FOX_W),
                                      past["cache_fox_k"], past["cache_fox_v"], o, fcum * LOG2E, tk)
                s0 = past["state_ret"][o].astype(F32)
            ret_out, s_last = _retention(h, s0, rq_tabs + rk_tabs, rt, seq - n_real)
            mix = (fox_out.reshape(rows, FOX_W), ret_out.reshape(rows, RET_VW), p["w_out"])
            if prompt:
                k32, v32 = k32.transpose(0, 3, 1, 2), v32.transpose(0, 3, 1, 2)
            st["fox_k"].append(k32.reshape(bsz, seq, FOX_HEADS, FOX_DIM))
            st["fox_v"].append(v32.reshape(bsz, seq, FOX_HEADS, FOX_DIM))
            st["fox_logf"].append(logf[:, q_off:q_off + seq])
            st["ret"].append(s_last)
        keep = (seq,) + out_rows if l == DEPTH - 1 and out_rows != (0, seq) else None
        x2 = _mix_ffn_ln(x2, *mix, ln(l, 1), ffn, (l, 1), ln(l, 2), tm, keep)
    return x2.reshape(bsz, out_rows[1], D_MODEL), {n: jnp.stack(a) for n, a in st.items()}


def kernel(x_prompt, x_sample, cache_mla_ckv, cache_mla_kpe, cache_fox_k, cache_fox_v, cache_fox_logf,
           state_s5_re, state_s5_im, state_ret, meta_tokens, ln_g, ln_b, ffn_w_gate, ffn_w_up, ffn_w_down,
           even_w_in, even_w_out, s5_a_re, s5_a_im, s5_b_re, s5_b_im, s5_c_re, s5_c_im, s5_d, s5_log_dt,
           s5_w_glu, s5_b_glu, mla_q_norm, mla_kv_norm, mla_w_uq, mla_w_ukv, odd_w_in, odd_w_out, fox_b_f):
    w = dict(ln_g=ln_g.astype(F32), ln_b=ln_b.astype(F32), even_w_in=even_w_in, even_w_out=even_w_out,
             s5_a_re=s5_a_re, s5_a_im=s5_a_im, s5_b_re=s5_b_re, s5_b_im=s5_b_im, s5_c_re=s5_c_re,
             s5_c_im=s5_c_im, s5_d=s5_d, s5_log_dt=s5_log_dt, s5_w_glu=s5_w_glu, s5_b_glu=s5_b_glu,
             mla_q_norm=mla_q_norm, mla_kv_norm=mla_kv_norm, mla_w_uq=mla_w_uq, mla_w_ukv=mla_w_ukv,
             odd_w_in=odd_w_in, odd_w_out=odd_w_out, fox_b_f=fox_b_f)
    past = dict(cache_mla_ckv=cache_mla_ckv, cache_mla_kpe=cache_mla_kpe, cache_fox_k=cache_fox_k,
                cache_fox_v=cache_fox_v, cache_fox_logf=cache_fox_logf, state_s5_re=state_s5_re,
                state_s5_im=state_s5_im, state_ret=state_ret)
    ffn = (ffn_w_gate.astype(BF16), ffn_w_up.astype(BF16), ffn_w_down.astype(BF16))
    evens = [_even_params(e, w) for e in range((DEPTH + 1) // 2)]
    odds = [_odd_params(o, w) for o in range(DEPTH // 2)]

    bsz, seq, _ = x_prompt.shape
    n_real = PAD + N_META + seq
    n_rows = _round_up(n_real, TILE)
    meta = jnp.broadcast_to(meta_tokens[None].astype(x_prompt.dtype), (bsz, N_META, D_MODEL))
    xp = jnp.concatenate([jnp.zeros((bsz, PAD, D_MODEL), x_prompt.dtype), meta, x_prompt,
                          jnp.zeros((bsz, n_rows - n_real, D_MODEL), x_prompt.dtype)], axis=1)
    pos_p = np.maximum(np.arange(n_rows) - PAD, 0)
    y_p, st_p = _trunk(xp, pos_p, n_real, (PAD + N_META, seq), None, w, ffn, evens, odds)
    d_seq = x_sample.shape[1]
    pos_s = N_META + cache_fox_k.shape[2] + np.arange(d_seq)
    y_s, st_s = _trunk(x_sample, pos_s, d_seq, (0, d_seq), past, w, ffn, evens, odds)

    def real(a):
        return a[:, :, PAD:n_real]

    return (y_p, y_s,
            real(st_p["mla_ckv"]), real(st_p["mla_kpe"]), real(st_p["fox_k"]), real(st_p["fox_v"]),
            real(st_p["fox_logf"]), st_p["s5_re"], st_p["s5_im"], st_p["ret"],
            st_s["mla_ckv"], st_s["mla_kpe"], st_s["fox_k"], st_s["fox_v"], st_s["fox_logf"],
            st_s["s5_re"], st_s["s5_im"], st_s["ret"])
```

```python
import functools
import math

import jax
import jax.numpy as jnp
import numpy as np
from jax import lax
from jax.experimental import pallas as pl
from jax.experimental.pallas import tpu as pltpu

F32 = jnp.float32
BF16 = jnp.bfloat16

D_MODEL = 1024
DEPTH = 4
CHUNK = 64
CHUNK_SHIFT = 6
N_META = 16
S5_WIDTH = 512
S5_CH = 16
S5_GROUPS = S5_WIDTH // S5_CH
S5_STATE = 64
S5_HALF = S5_GROUPS * S5_STATE
MLA_HEADS = 8
MLA_Q_RANK = 256
MLA_KV_RANK = 128
MLA_NOPE = 64
MLA_ROPE = 32
MLA_V = 64
MLA_QK_PAD = 128
FOX_HEADS = 8
FOX_DIM = 64
FOX_W = FOX_HEADS * FOX_DIM
RET_HEADS = 4
RET_DK = 64
RET_DV = 128
RET_QK = RET_HEADS * RET_DK
RET_VW = RET_HEADS * RET_DV
D_FF = 2816
ROPE_BASE = 10000.0
ALPHA = (2.0 * DEPTH) ** 0.25
EPS = 1e-5
NEG = -1e30
LOG2E = math.log2(math.e)
EVEN_IN = S5_WIDTH + MLA_Q_RANK + MLA_KV_RANK + MLA_ROPE
EVEN_IN_PAD = 1024
ODD_IN_PAD = 3200
ODD_H = ODD_IN_PAD - 3 * FOX_W
ODD_LOGIT_COL = 2 * RET_QK + 2 * RET_VW

PAD = CHUNK - N_META
LANE = 128
SUBLANE = 8
TILE = 256
VMEM_LIMIT = 56 * 1024 * 1024


def _cparams(n_grid, vmem=None):
    return pltpu.CompilerParams(dimension_semantics=("arbitrary",) * n_grid, vmem_limit_bytes=vmem)


def _const_spec(shape):
    nd = len(shape)
    return pl.BlockSpec(shape, lambda *_: (0,) * nd)


def _round_up(n, m):
    return -(-n // m) * m


def _layer_norm_rows(z, g, b):
    mu = jnp.mean(z, axis=-1, keepdims=True)
    zc = z - mu
    var = jnp.mean(zc * zc, axis=-1, keepdims=True)
    return zc * lax.rsqrt(var + EPS) * g + b


def _ffn_rows(x, wg_ref, wu_ref, wd_ref, g_ref, b_ref, hid_ref):
    xb = x.astype(BF16)
    for c in range(D_FF // TILE):
        sl = slice(c * TILE, (c + 1) * TILE)
        hg = jnp.dot(xb, wg_ref[:, sl], preferred_element_type=F32)
        hu = jnp.dot(xb, wu_ref[:, sl], preferred_element_type=F32)
        hid_ref[:, sl] = (hg * jax.nn.sigmoid(hg) * hu).astype(BF16)
    y = jnp.dot(hid_ref[...], wd_ref[...], preferred_element_type=F32)
    return _layer_norm_rows(ALPHA * x + 0.5 * y, g_ref[...], b_ref[...])


def _ffn_body(x_ref, wg_ref, wu_ref, wd_ref, g_ref, b_ref, o_ref, hid_ref):
    o_ref[...] = _ffn_rows(x_ref[...], wg_ref, wu_ref, wd_ref, g_ref, b_ref, hid_ref)


def _ffn_weight_specs(which):
    once = dict(pipeline_mode=pl.Buffered(1))
    pick = lambda i: which + (0, 0)
    return [pl.BlockSpec((None, None, D_MODEL, D_FF), pick, **once),
            pl.BlockSpec((None, None, D_MODEL, D_FF), pick, **once),
            pl.BlockSpec((None, None, D_FF, D_MODEL), pick, **once),
            _const_spec((1, D_MODEL)), _const_spec((1, D_MODEL))]


def _ffn_ln(x, ffn, which, g, b, tm):
    rows = x.shape[0]
    return pl.pallas_call(
        _ffn_body,
        grid=(rows // tm,),
        in_specs=[pl.BlockSpec((tm, D_MODEL), lambda i: (i, 0))] + _ffn_weight_specs(which),
        out_specs=pl.BlockSpec((tm, D_MODEL), lambda i: (i, 0)),
        out_shape=jax.ShapeDtypeStruct((rows, D_MODEL), F32),
        scratch_shapes=[pltpu.VMEM((tm, D_FF), BF16)],
        compiler_params=_cparams(1, VMEM_LIMIT),
        name="ffn_ln",
    )(x, *ffn, g, b)


def _mix_ffn_body(x_ref, a1_ref, a2_ref, wo_ref, g1_ref, b1_ref, wg_ref, wu_ref, wd_ref, g2_ref, b2_ref,
                  o_ref, hid_ref):
    k1 = a1_ref.shape[1]
    y = jnp.dot(a1_ref[...], wo_ref[:k1, :], preferred_element_type=F32)
    y = y + jnp.dot(a2_ref[...], wo_ref[k1:, :], preferred_element_type=F32)
    x1 = _layer_norm_rows(ALPHA * x_ref[...] + y, g1_ref[...], b1_ref[...])
    o_ref[...] = _ffn_rows(x1, wg_ref, wu_ref, wd_ref, g2_ref, b2_ref, hid_ref)


def _mix_ffn_ln(x, a1, a2, w_out, ln1, ffn, which, ln2, tm, keep=None):
    rows = x.shape[0]
    if keep is None:
        n_steps = rows // tm
        row_spec = lambda n: pl.BlockSpec((tm, n), lambda i: (i, 0))
    else:
        seq, start, count = keep
        assert count % tm == 0
        per = count // tm
        n_steps = (rows // seq) * per
        align = math.gcd(seq, start, tm)
        row_spec = lambda n: pl.BlockSpec(
            (pl.Element(tm), pl.Element(n)),
            lambda i: (pl.multiple_of((i // per) * seq + start + (i % per) * tm, align), 0))
    return pl.pallas_call(
        _mix_ffn_body,
        grid=(n_steps,),
        in_specs=[row_spec(D_MODEL), row_spec(a1.shape[1]), row_spec(a2.shape[1]),
                  pl.BlockSpec(w_out.shape, lambda i: (0, 0), pipeline_mode=pl.Buffered(1)),
                  _const_spec((1, D_MODEL)), _const_spec((1, D_MODEL))] + _ffn_weight_specs(which),
        out_specs=pl.BlockSpec((tm, D_MODEL), lambda i: (i, 0)),
        out_shape=jax.ShapeDtypeStruct((n_steps * tm, D_MODEL), F32),
        scratch_shapes=[pltpu.VMEM((tm, D_FF), BF16)],
        compiler_params=_cparams(1, VMEM_LIMIT),
        name="mix_ffn_ln",
    )(x, a1, a2, w_out, *ln1, *ffn, *ln2)


def _proj_body(x_ref, w_ref, o_ref):
    o_ref[...] = jnp.dot(x_ref[...].astype(BF16), w_ref[...], preferred_element_type=F32)


def _proj(x, w, tm):
    rows, n = x.shape[0], w.shape[1]
    return pl.pallas_call(
        _proj_body,
        grid=(rows // tm,),
        in_specs=[pl.BlockSpec((tm, D_MODEL), lambda i: (i, 0)), _const_spec(w.shape)],
        out_specs=pl.BlockSpec((tm, n), lambda i: (i, 0)),
        out_shape=jax.ShapeDtypeStruct((rows, n), F32),
        compiler_params=_cparams(1, VMEM_LIMIT),
        name="in_proj_even",
    )(x, w)


def _proj_odd_body(x_ref, w_ref, h_ref, k32_ref, v32_ref, q_ref, k_ref, v_ref, *, attn_layout):
    y = jnp.dot(x_ref[...].astype(BF16), w_ref[...], preferred_element_type=F32)
    h_ref[...] = y[:, 3 * FOX_W:]
    q = y[:, :FOX_W] * LOG2E
    k, v = y[:, FOX_W:2 * FOX_W], y[:, 2 * FOX_W:3 * FOX_W]
    k_ref[...] = k.astype(BF16)
    if attn_layout:
        q_ref[0] = q.T.astype(BF16)
        k_t, v_t = k.T, v.T
        for hd in range(FOX_HEADS):
            rows = slice(hd * FOX_DIM, (hd + 1) * FOX_DIM)
            k32_ref[0, hd] = k_t[rows, :]
            v32_ref[0, hd] = v_t[rows, :]
            v_ref[0, hd, 0] = v_t[rows, :].astype(BF16)
    else:
        k32_ref[...] = k
        v32_ref[...] = v
        q_ref[...] = q.astype(BF16)
        v_ref[...] = v.astype(BF16)


def _proj_odd(x, w, bsz, tm, attn_layout):
    rows = x.shape[0]
    seq = rows // bsz
    row_spec = lambda n: pl.BlockSpec((tm, n), lambda i: (i, 0))
    if attn_layout:
        assert tm == TILE and seq % TILE == 0
        n_t = seq // TILE
        kv32_shape = jax.ShapeDtypeStruct((bsz, FOX_HEADS, FOX_DIM, seq), F32)
        kv32_spec = pl.BlockSpec((1, FOX_HEADS, FOX_DIM, TILE), lambda i: (i // n_t, 0, 0, i % n_t))
        q_shape = jax.ShapeDtypeStruct((bsz, FOX_W, seq), BF16)
        q_spec = pl.BlockSpec((1, FOX_W, TILE), lambda i: (i // n_t, 0, i % n_t))
        v_shape = jax.ShapeDtypeStruct((bsz, FOX_HEADS, n_t, FOX_DIM, TILE), BF16)
        v_spec = pl.BlockSpec((1, FOX_HEADS, 1, FOX_DIM, TILE), lambda i: (i // n_t, 0, i % n_t, 0, 0))
    else:
        kv32_shape, kv32_spec = jax.ShapeDtypeStruct((rows, FOX_W), F32), row_spec(FOX_W)
        q_shape = v_shape = jax.ShapeDtypeStruct((rows, FOX_W), BF16)
        q_spec = v_spec = row_spec(FOX_W)
    return pl.pallas_call(
        functools.partial(_proj_odd_body, attn_layout=attn_layout),
        grid=(rows // tm,),
        in_specs=[row_spec(D_MODEL), _const_spec(w.shape)],
        out_specs=[row_spec(ODD_H), kv32_spec, kv32_spec, q_spec, row_spec(FOX_W), v_spec],
        out_shape=[jax.ShapeDtypeStruct((rows, ODD_H), F32), kv32_shape, kv32_shape, q_shape,
                   jax.ShapeDtypeStruct((rows, FOX_W), BF16), v_shape],
        compiler_params=_cparams(1, VMEM_LIMIT),
        name="in_proj_odd",
    )(x, w)


def _s5_body(u_ref, h0r_ref, h0i_ref, ar_ref, ai_ref, asr_ref, asi_ref, bbig_ref, cbig_ref, d_ref, wglu_ref,
             bglu_ref, o_ref, hlr_ref, hli_ref, hs_ref, st_ref, *, first_row, last_seg):
    t = pl.program_id(1)
    rt = u_ref.shape[1]
    seg = rt // SUBLANE

    @pl.when(t == 0)
    def _():
        st_ref[0:1, :] = h0r_ref[0]
        st_ref[1:2, :] = h0i_ref[0]

    u = u_ref[0]
    if first_row:
        rows = t * rt + lax.broadcasted_iota(jnp.int32, (rt, 1), 0)
        u = jnp.where(rows >= first_row, u, 0.0)
    i0 = lax.broadcasted_iota(jnp.int32, (rt, rt), 0)
    i1 = lax.broadcasted_iota(jnp.int32, (rt, rt), 1)
    regroup = (i1 == (i0 & (SUBLANE - 1)) * seg + (i0 >> 3)).astype(BF16)
    restore = (i0 == (i1 & (SUBLANE - 1)) * seg + (i1 >> 3)).astype(BF16)
    ub = jnp.dot(regroup, u.astype(BF16), preferred_element_type=F32).astype(BF16)

    half_w, half_s = S5_WIDTH // 2, S5_HALF // 2
    for kb in range(2):
        for part in range(2):
            c0 = part * S5_HALF + kb * half_s
            hs_ref[:, c0:c0 + half_s] = jnp.dot(
                ub[:, kb * half_w:(kb + 1) * half_w], bbig_ref[kb * half_w:(kb + 1) * half_w, c0:c0 + half_s],
                preferred_element_type=F32)

    def scan(lanes, start, store):
        re_l = lanes
        im_l = slice(S5_HALF + lanes.start, S5_HALF + lanes.stop)
        ar = jnp.broadcast_to(ar_ref[:, lanes], (SUBLANE, lanes.stop - lanes.start))
        ai = jnp.broadcast_to(ai_ref[:, lanes], (SUBLANE, lanes.stop - lanes.start))

        def step(k, carry):
            hr, hi = carry
            base = k * SUBLANE
            nr = ar * hr - ai * hi + hs_ref[pl.ds(base, SUBLANE), re_l]
            ni = ar * hi + ai * hr + hs_ref[pl.ds(base, SUBLANE), im_l]
            if store:
                hs_ref[pl.ds(base, SUBLANE), re_l] = nr
                hs_ref[pl.ds(base, SUBLANE), im_l] = ni
            return nr, ni

        carry = start
        for k in range(seg):
            carry = step(k, carry)
        return carry

    slabs = [slice(c * half_s, (c + 1) * half_s) for c in range(2)]
    zero = jnp.zeros((SUBLANE, half_s), F32)
    ends = [scan(lanes, (zero, zero), False) for lanes in slabs]
    er = jnp.concatenate([e[0] for e in ends], axis=1)
    ei = jnp.concatenate([e[1] for e in ends], axis=1)
    asr, asi = asr_ref[...], asi_ref[...]
    sr, si = st_ref[0:1, :], st_ref[1:2, :]
    start_r, start_i = [], []
    for j in range(SUBLANE):
        start_r.append(sr)
        start_i.append(si)
        sr, si = asr * sr - asi * si + er[j:j + 1], asr * si + asi * sr + ei[j:j + 1]
    st_ref[0:1, :] = sr
    st_ref[1:2, :] = si
    start_r8, start_i8 = jnp.concatenate(start_r, axis=0), jnp.concatenate(start_i, axis=0)
    for lanes in slabs:
        scan(lanes, (start_r8[:, lanes], start_i8[:, lanes]), True)

    hb = hs_ref[...].astype(BF16)
    ys = []
    for kb in range(2):
        re_rows = slice(kb * half_s, (kb + 1) * half_s)
        im_rows = slice(S5_HALF + kb * half_s, S5_HALF + (kb + 1) * half_s)
        cols = slice(kb * half_w, (kb + 1) * half_w)
        ys.append(jnp.dot(hb[:, re_rows], cbig_ref[re_rows, cols], preferred_element_type=F32)
                  + jnp.dot(hb[:, im_rows], cbig_ref[im_rows, cols], preferred_element_type=F32))
    y = jnp.concatenate(ys, axis=1)
    y_hi = y.astype(BF16)
    y_lo = (y - y_hi.astype(F32)).astype(BF16)
    y = (jnp.dot(restore, y_hi, preferred_element_type=F32) + jnp.dot(restore, y_lo, preferred_element_type=F32)
         + d_ref[...] * u)
    g = jax.nn.gelu(y)
    gate = jnp.dot(g.astype(BF16), wglu_ref[...], preferred_element_type=F32) + bglu_ref[...]
    o_ref[0] = (g * jax.nn.sigmoid(gate)).astype(o_ref.dtype)

    @pl.when(t == pl.num_programs(1) - 1)
    def _():
        hlr_ref[0] = sr if last_seg == SUBLANE else start_r[last_seg]
        hli_ref[0] = si if last_seg == SUBLANE else start_i[last_seg]


def _s5(h, h0r, h0i, p, rt, first_row, n_real):
    bsz, seq, _ = h.shape
    seg = rt // SUBLANE
    real_in_last = n_real - (seq - rt)
    assert 0 < real_in_last <= rt and real_in_last % seg == 0
    as_re = (jnp.exp(p["lam_dt_re"] * seg) * jnp.cos(p["lam_dt_im"] * seg)).reshape(1, S5_HALF)
    as_im = (jnp.exp(p["lam_dt_re"] * seg) * jnp.sin(p["lam_dt_im"] * seg)).reshape(1, S5_HALF)
    state_spec = pl.BlockSpec((1, 1, S5_HALF), lambda b, t: (b, 0, 0))
    return pl.pallas_call(
        functools.partial(_s5_body, first_row=first_row, last_seg=real_in_last // seg),
        grid=(bsz, seq // rt),
        in_specs=[
            pl.BlockSpec((1, rt, S5_WIDTH), lambda b, t: (b, t, 0)),
            state_spec, state_spec,
            _const_spec((1, S5_HALF)), _const_spec((1, S5_HALF)), _const_spec((1, S5_HALF)), _const_spec((1, S5_HALF)),
            _const_spec((S5_WIDTH, 2 * S5_HALF)), _const_spec((2 * S5_HALF, S5_WIDTH)),
            _const_spec((1, S5_WIDTH)), _const_spec((S5_WIDTH, S5_WIDTH)), _const_spec((1, S5_WIDTH)),
        ],
        out_specs=[pl.BlockSpec((1, rt, S5_WIDTH), lambda b, t: (b, t, 0)), state_spec, state_spec],
        out_shape=[jax.ShapeDtypeStruct((bsz, seq, S5_WIDTH), BF16),
                   jax.ShapeDtypeStruct((bsz, 1, S5_HALF), F32),
                   jax.ShapeDtypeStruct((bsz, 1, S5_HALF), F32)],
        scratch_shapes=[pltpu.VMEM((rt, 2 * S5_HALF), F32), pltpu.VMEM((2, S5_HALF), F32)],
        compiler_params=_cparams(2, VMEM_LIMIT),
        name="s5",
    )(h, h0r, h0i, p["a_re"], p["a_im"], as_re, as_im, p["bbig"], p["cbig"], p["d"], p["w_glu"], p["b_glu"])


def _rope_lanes(x, c, s1, s2, half):
    n = x.shape[-1]
    return x * c + pltpu.roll(x, n - half, 1) * s1 + pltpu.roll(x, half, 1) * s2


def _rms_rows(x, g):
    return x * lax.rsqrt(jnp.mean(x * x, axis=-1, keepdims=True) + EPS) * g


def _mla_latents(h, kn_ref, ck_ref, s1k_ref, s2k_ref, ckv_ref, kpe_ref):
    ckv = _rms_rows(h[:, MLA_Q_RANK:MLA_Q_RANK + MLA_KV_RANK], kn_ref[...])
    kpe = _rope_lanes(h[:, MLA_Q_RANK + MLA_KV_RANK:], ck_ref[...], s1k_ref[...], s2k_ref[...],
                      MLA_ROPE // 2)[:, :MLA_ROPE]
    ckv_ref[0] = ckv
    kpe_ref[0] = kpe
    return ckv, kpe


def _mla_rows_body(h_ref, qn_ref, kn_ref, wq_ref, cq_ref, s1q_ref, s2q_ref, ck_ref, s1k_ref, s2k_ref,
                   q_ref, ckv_ref, kpe_ref):
    h = h_ref[0]
    q_lat = _rms_rows(h[:, :MLA_Q_RANK], qn_ref[...]).astype(BF16)
    q = jnp.dot(q_lat, wq_ref[...], preferred_element_type=F32)
    cq, s1q, s2q = cq_ref[...], s1q_ref[...], s2q_ref[...]
    for hd in range(MLA_HEADS):
        sl = slice(hd * MLA_QK_PAD, (hd + 1) * MLA_QK_PAD)
        q_ref[0, :, sl] = _rope_lanes(q[:, sl], cq, s1q, s2q, MLA_ROPE // 2).astype(BF16)
    _mla_latents(h, kn_ref, ck_ref, s1k_ref, s2k_ref, ckv_ref, kpe_ref)


def _even_rows_body(x_ref, win_ref, qn_ref, kn_ref, wqt_ref, cos_ref, sin_ref, ck_ref, s1k_ref, s2k_ref,
                    wkn_ref, wkp_ref, wv_ref, u_ref, q_ref, ckv_ref, kpe_ref, k_ref, v_ref, *, scale):
    y = jnp.dot(x_ref[0].astype(BF16), win_ref[...], preferred_element_type=F32)
    u_ref[0] = y[:, :S5_WIDTH]
    h = y[:, S5_WIDTH:]
    q_lat = _rms_rows(h[:, :MLA_Q_RANK], qn_ref[...]).astype(BF16)
    q_t = _dot_nt(wqt_ref[...], q_lat)
    cos_t, sin_t = cos_ref[...], sin_ref[...]
    half = MLA_ROPE // 2
    for hd in range(MLA_HEADS):
        r0 = hd * MLA_QK_PAD
        x1 = q_t[r0 + MLA_NOPE:r0 + MLA_NOPE + half, :]
        x2 = q_t[r0 + MLA_NOPE + half:r0 + MLA_NOPE + 2 * half, :]
        q_ref[0, r0:r0 + MLA_QK_PAD, :] = jnp.concatenate(
            [q_t[r0:r0 + MLA_NOPE, :] * scale, x1 * cos_t - x2 * sin_t, x1 * sin_t + x2 * cos_t,
             q_t[r0 + MLA_NOPE + 2 * half:r0 + MLA_QK_PAD, :]], axis=0).astype(BF16)
    ckv, kpe = _mla_latents(h, kn_ref, ck_ref, s1k_ref, s2k_ref, ckv_ref, kpe_ref)
    ckv = ckv.astype(BF16)
    k = jnp.dot(ckv, wkn_ref[...], preferred_element_type=F32)
    k = k + jnp.dot(kpe.astype(BF16), wkp_ref[...], preferred_element_type=F32)
    k_ref[0] = k.astype(BF16)
    v_t = jnp.dot(ckv, wv_ref[...], preferred_element_type=F32).T
    for hd in range(MLA_HEADS):
        v_ref[0, hd, 0] = v_t[hd * MLA_V:(hd + 1) * MLA_V, :].astype(BF16)


def _mla_scale():
    return (MLA_NOPE + MLA_ROPE) ** -0.5 * LOG2E


def _even_rows(x, p, pos):
    bsz, seq, _ = x.shape
    n_t = seq // TILE
    n_q = MLA_HEADS * MLA_QK_PAD
    half = MLA_ROPE // 2
    scale = _mla_scale()
    ang = (ROPE_BASE ** (-np.arange(half, dtype=np.float32) / half))[:, None] * pos.astype(np.float32)[None, :]
    q_tabs = ((np.cos(ang) * scale).astype(np.float32), (np.sin(ang) * scale).astype(np.float32))
    row_spec = lambda n: pl.BlockSpec((1, TILE, n), lambda b, t: (b, t, 0))
    consts = [p["w_in"], p["q_norm"], p["kv_norm"], p["wq"].T]
    weights = [p["wkn"], p["wkp"], p["wv"]]
    return pl.pallas_call(
        functools.partial(_even_rows_body, scale=scale),
        grid=(bsz, n_t),
        in_specs=[row_spec(D_MODEL)] + [_const_spec(c.shape) for c in consts]
        + [pl.BlockSpec((half, TILE), lambda b, t: (0, t))] * 2
        + [pl.BlockSpec((TILE, LANE), lambda b, t: (t, 0))] * 3 + [_const_spec(c.shape) for c in weights],
        out_specs=[row_spec(S5_WIDTH), pl.BlockSpec((1, n_q, TILE), lambda b, t: (b, 0, t)),
                   row_spec(MLA_KV_RANK), row_spec(MLA_ROPE), row_spec(n_q),
                   pl.BlockSpec((1, MLA_HEADS, 1, MLA_V, TILE), lambda b, t: (b, 0, t, 0, 0))],
        out_shape=[jax.ShapeDtypeStruct((bsz, seq, S5_WIDTH), F32),
                   jax.ShapeDtypeStruct((bsz, n_q, seq), BF16),
                   jax.ShapeDtypeStruct((bsz, seq, MLA_KV_RANK), F32),
                   jax.ShapeDtypeStruct((bsz, seq, MLA_ROPE), F32),
                   jax.ShapeDtypeStruct((bsz, seq, n_q), BF16),
                   jax.ShapeDtypeStruct((bsz, MLA_HEADS, n_t, MLA_V, TILE), BF16)],
        compiler_params=_cparams(2, VMEM_LIMIT),
        name="even_rows",
    )(x, *consts, *q_tabs, *_rope_tables(pos, half, LANE, LANE, 0), *weights)


def _mla_rows(h, p, pos, tm):
    bsz, seq, _ = h.shape
    tab_spec = pl.BlockSpec((tm, LANE), lambda b, t: (t, 0))
    n_q = MLA_HEADS * MLA_QK_PAD
    half = MLA_ROPE // 2
    tabs = (_rope_tables(pos, half, MLA_QK_PAD, MLA_QK_PAD, MLA_NOPE, scale=_mla_scale())
            + _rope_tables(pos, half, LANE, LANE, 0))
    return pl.pallas_call(
        _mla_rows_body,
        grid=(bsz, seq // tm),
        in_specs=[
            pl.BlockSpec((1, tm, EVEN_IN_PAD - S5_WIDTH), lambda b, t: (b, t, 1)),
            _const_spec((1, MLA_Q_RANK)), _const_spec((1, MLA_KV_RANK)),
            _const_spec(p["wq"].shape),
        ] + [tab_spec] * 6,
        out_specs=[
            pl.BlockSpec((1, tm, n_q), lambda b, t: (b, t, 0)),
            pl.BlockSpec((1, tm, MLA_KV_RANK), lambda b, t: (b, t, 0)),
            pl.BlockSpec((1, tm, MLA_ROPE), lambda b, t: (b, t, 0)),
        ],
        out_shape=[jax.ShapeDtypeStruct((bsz, seq, n_q), BF16),
                   jax.ShapeDtypeStruct((bsz, seq, MLA_KV_RANK), F32),
                   jax.ShapeDtypeStruct((bsz, seq, MLA_ROPE), F32)],
        compiler_params=_cparams(2, VMEM_LIMIT),
        name="mla_rows",
    )(h, p["q_norm"], p["kv_norm"], p["wq"], *tabs)


def _flash_body(*refs, heads, dq, dv, causal):
    if causal:
        qt_ref, k_ref, vt_ref, fq_ref, fk_ref, o_ref = refs
    else:
        qt_ref, k_ref, vt_ref, o_ref = refs
    tq = qt_ref.shape[2]
    n_tiles, tk = vt_ref.shape[2], vt_ref.shape[4]
    r0 = pl.program_id(1) * tq
    j_last = jnp.minimum((r0 + tq - 1) // tk, n_tiles - 1)
    q_row = r0 + lax.broadcasted_iota(jnp.int32, (1, tq), 1)
    last = q_row if causal else ((q_row >> CHUNK_SHIFT) << CHUNK_SHIFT) + (CHUNK - 1)
    first_last = r0 if causal else ((r0 >> CHUNK_SHIFT) << CHUNK_SHIFT) + (CHUNK - 1)
    n_open = jnp.clip((first_last + 1) // tk, 1, j_last + 1)
    q_t = [qt_ref[0, hd * dq:(hd + 1) * dq, :] for hd in range(heads)]

    def tile(j, carry, masked, wide):
        width = 2 * tk if wide else tk
        start = pl.multiple_of(j * tk, tk)
        if masked:
            k_row = start + lax.broadcasted_iota(jnp.int32, (width, 1), 0)
            ok = (k_row <= last) & (k_row >= PAD)
        scores = [jnp.dot(k_ref[0, pl.ds(start, width), hd * dq:(hd + 1) * dq], q_t[hd],
                          preferred_element_type=F32) for hd in range(heads)]
        probs = []
        for hd in range(heads):
            m, l, _ = carry[hd]
            s = scores[hd]
            if causal:
                s = s + (fq_ref[0, hd:hd + 1, :] - fk_ref[0, pl.ds(start, width), hd:hd + 1])
            if masked:
                s = jnp.where(ok, s, NEG)
            m_new = jnp.maximum(m, jnp.max(s, axis=0, keepdims=True))
            a = jnp.exp2(m - m_new)
            p = jnp.exp2(s - m_new)
            probs.append((m_new, a * l + jnp.sum(p, axis=0, keepdims=True), a, p.astype(BF16)))
        new = []
        for hd in range(heads):
            m_new, l, a, p = probs[hd]
            acc = a * carry[hd][2] + jnp.dot(vt_ref[0, hd, j], p[:tk], preferred_element_type=F32)
            if wide:
                acc = acc + jnp.dot(vt_ref[0, hd, j + 1], p[tk:], preferred_element_type=F32)
            new.append((m_new, l, acc))
        return tuple(new)

    init = tuple((jnp.full((1, tq), NEG, F32), jnp.zeros((1, tq), F32), jnp.zeros((dv, tq), F32))
                 for _ in range(heads))
    carry = tile(0, init, True, False)
    n_wide = (n_open - 1) // 2
    carry = lax.fori_loop(0, n_wide, lambda i, c: tile(1 + 2 * i, c, False, True), carry)
    carry = lax.fori_loop(1 + 2 * n_wide, n_open, functools.partial(tile, masked=False, wide=False), carry)
    carry = lax.fori_loop(n_open, j_last + 1, functools.partial(tile, masked=True, wide=False), carry)
    outs = [carry[hd][2] / carry[hd][1] for hd in range(heads)]
    per = LANE // dv
    for g in range(heads // per):
        o_ref[0, :, g * LANE:(g + 1) * LANE] = jnp.concatenate(
            outs[g * per:(g + 1) * per], axis=0).T.astype(o_ref.dtype)


def _flash(q_t, k, v_t, fq_t, fk, *, heads, dq, dv, causal):
    bsz, _, n_q = q_t.shape
    n_keys = k.shape[1]
    in_specs = [
        pl.BlockSpec((1, heads * dq, TILE), lambda b, i: (b, 0, i)),
        pl.BlockSpec((1, n_keys, heads * dq), lambda b, i: (b, 0, 0)),
        pl.BlockSpec((1,) + v_t.shape[1:], lambda b, i: (b, 0, 0, 0, 0)),
    ]
    args = [q_t, k, v_t]
    if causal:
        in_specs += [pl.BlockSpec((1, heads, TILE), lambda b, i: (b, 0, i)),
                     pl.BlockSpec((1, n_keys, heads), lambda b, i: (b, 0, 0))]
        args += [fq_t, fk]
    return pl.pallas_call(
        functools.partial(_flash_body, heads=heads, dq=dq, dv=dv, causal=causal),
        grid=(bsz, n_q // TILE),
        in_specs=in_specs,
        out_specs=pl.BlockSpec((1, TILE, heads * dv), lambda b, i: (b, i, 0)),
        out_shape=jax.ShapeDtypeStruct((bsz, n_q, heads * dv), BF16),
        compiler_params=_cparams(2, VMEM_LIMIT),
        name="flash_causal" if causal else "flash_chunk",
    )(*args)


def _online_softmax(s, m_ref, l_ref, idx):
    m_old = m_ref[idx]
    m_new = jnp.maximum(m_old, jnp.max(s, axis=-1, keepdims=True))
    a = jnp.exp2(m_old - m_new)
    p = jnp.exp2(s - m_new)
    m_ref[idx] = m_new
    l_ref[idx] = a * l_ref[idx] + jnp.sum(p, axis=-1, keepdims=True)
    return a, p


def _dot_nt(a, b):
    return lax.dot_general(a, b, (((1,), (1,)), ((), ())), preferred_element_type=F32)


def _fox_decode_body(q_ref, fq_ref, kc_ref, vc_ref, fkc_ref, kn_ref, vn_ref, fkn_ref, o_ref, m_ref, l_ref, acc_ref):
    t = pl.program_id(1)
    n_q = q_ref.shape[1]
    cols = [slice(hd * FOX_DIM, (hd + 1) * FOX_DIM) for hd in range(FOX_HEADS)]

    @pl.when(t == 0)
    def _():
        m_ref[...] = jnp.full(m_ref.shape, NEG, F32)
        l_ref[...] = jnp.zeros(l_ref.shape, F32)
        acc_ref[...] = jnp.zeros(acc_ref.shape, F32)

    def attend(scores, fk_rows, values, ok):
        probs = []
        for hd in range(FOX_HEADS):
            s = scores[hd] + (fq_ref[0, :, hd:hd + 1] - fk_rows[hd:hd + 1, :])
            if ok is not None:
                s = jnp.where(ok, s, NEG)
            probs.append(_online_softmax(s, m_ref, l_ref, hd))
        for hd in range(FOX_HEADS):
            a, p = probs[hd]
            acc_ref[hd] = a * acc_ref[hd] + values(hd, p.astype(BF16))

    attend([jnp.dot(q_ref[0, :, cols[hd]], kc_ref[0, 0, hd].astype(BF16), preferred_element_type=F32)
            for hd in range(FOX_HEADS)],
           fkc_ref[0], lambda hd, p: _dot_nt(p, vc_ref[0, 0, hd].astype(BF16)), None)

    @pl.when(t == pl.num_programs(1) - 1)
    def _():
        causal = (lax.broadcasted_iota(jnp.int32, (n_q, n_q), 1) <= lax.broadcasted_iota(jnp.int32, (n_q, n_q), 0))
        attend([_dot_nt(q_ref[0, :, sl], kn_ref[0, :, sl]) for sl in cols], fkn_ref[0],
               lambda hd, p: jnp.dot(p, vn_ref[0, :, cols[hd]], preferred_element_type=F32), causal)
        for hd in range(FOX_HEADS):
            o_ref[0, :, cols[hd]] = (acc_ref[hd] / l_ref[hd]).astype(o_ref.dtype)


def _fox_decode(q, k_new, v_new, cache_k, cache_v, layer, fcum, tk):
    bsz, n_q, width = q.shape
    n_past = cache_k.shape[2]
    rows_minor = lambda c: c.transpose(0, 1, 3, 4, 2)
    fk_t = fcum.transpose(0, 2, 1)
    row_spec = pl.BlockSpec((1, n_q, width), lambda b, t: (b, 0, 0))
    cache_spec = pl.BlockSpec((1, 1, FOX_HEADS, FOX_DIM, tk), lambda b, t: (layer, b, 0, 0, t))
    return pl.pallas_call(
        _fox_decode_body,
        grid=(bsz, n_past // tk),
        in_specs=[row_spec, pl.BlockSpec((1, n_q, FOX_HEADS), lambda b, t: (b, 0, 0)),
                  cache_spec, cache_spec, pl.BlockSpec((1, FOX_HEADS, tk), lambda b, t: (b, 0, t)),
                  row_spec, row_spec, pl.BlockSpec((1, FOX_HEADS, n_q), lambda b, t: (b, 0, 0))],
        out_specs=row_spec,
        out_shape=jax.ShapeDtypeStruct((bsz, n_q, width), BF16),
        scratch_shapes=[pltpu.VMEM((FOX_HEADS, n_q, 1), F32), pltpu.VMEM((FOX_HEADS, n_q, 1), F32),
                        pltpu.VMEM((FOX_HEADS, n_q, FOX_DIM), F32)],
        compiler_params=_cparams(2, VMEM_LIMIT),
        name="fox_decode",
    )(q, fcum[:, n_past:n_past + n_q], rows_minor(cache_k), rows_minor(cache_v), fk_t[:, :, :n_past],
      k_new, v_new, fk_t[:, :, n_past:n_past + n_q])


def _mla_decode_body(q_ref, wkn_ref, wv_ref, cc_ref, pc_ref, cn_ref, pn_ref, o_ref, qa_ref, qr_ref,
                     m_ref, l_ref, acc_ref):
    t = pl.program_id(1)
    n_q = q_ref.shape[1]

    @pl.when(t == 0)
    def _():
        for hd in range(MLA_HEADS):
            c0 = hd * MLA_QK_PAD
            rows = slice(hd * n_q, (hd + 1) * n_q)
            qa_ref[rows, :] = _dot_nt(q_ref[0, :, c0:c0 + MLA_NOPE],
                                      wkn_ref[:, c0:c0 + MLA_NOPE]).astype(BF16)
            qr_ref[rows, :] = q_ref[0, :, c0 + MLA_NOPE:c0 + MLA_NOPE + MLA_ROPE]
        m_ref[...] = jnp.full(m_ref.shape, NEG, F32)
        l_ref[...] = jnp.zeros(l_ref.shape, F32)
        acc_ref[...] = jnp.zeros(acc_ref.shape, F32)

    def attend(ckv, rope_scores):
        a, p = _online_softmax(_dot_nt(qa_ref[...], ckv) + rope_scores, m_ref, l_ref, 0)
        acc_ref[0] = a * acc_ref[0] + jnp.dot(p.astype(BF16), ckv, preferred_element_type=F32)

    attend(cc_ref[0, 0].astype(BF16),
           jnp.dot(qr_ref[...], pc_ref[0, 0].astype(BF16), preferred_element_type=F32))

    @pl.when(t == pl.num_programs(1) - 1)
    def _():
        attend(cn_ref[0].astype(BF16), _dot_nt(qr_ref[...], pn_ref[0].astype(BF16)))
        lat = (acc_ref[0] / l_ref[0]).astype(BF16)
        for hd in range(MLA_HEADS):
            cols = slice(hd * MLA_V, (hd + 1) * MLA_V)
            o_ref[0, :, cols] = jnp.dot(lat[hd * n_q:(hd + 1) * n_q, :], wv_ref[:, cols],
                                        preferred_element_type=F32).astype(o_ref.dtype)


def _mla_decode(q, ckv_new, kpe_new, cache_ckv, cache_kpe, layer, p, tk):
    bsz, n_q, _ = q.shape
    n_past = cache_ckv.shape[2]
    stacked = MLA_HEADS * n_q
    return pl.pallas_call(
        _mla_decode_body,
        grid=(bsz, n_past // tk),
        in_specs=[pl.BlockSpec((1, n_q, MLA_HEADS * MLA_QK_PAD), lambda b, t: (b, 0, 0)),
                  _const_spec(p["wkn"].shape), _const_spec(p["wv"].shape),
                  pl.BlockSpec((1, 1, tk, MLA_KV_RANK), lambda b, t: (layer, b, t, 0)),
                  pl.BlockSpec((1, 1, MLA_ROPE, tk), lambda b, t: (layer, b, 0, t)),
                  pl.BlockSpec((1, n_q, MLA_KV_RANK), lambda b, t: (b, 0, 0)),
                  pl.BlockSpec((1, n_q, MLA_ROPE), lambda b, t: (b, 0, 0))],
        out_specs=pl.BlockSpec((1, n_q, MLA_HEADS * MLA_V), lambda b, t: (b, 0, 0)),
        out_shape=jax.ShapeDtypeStruct((bsz, n_q, MLA_HEADS * MLA_V), BF16),
        scratch_shapes=[pltpu.VMEM((stacked, MLA_KV_RANK), BF16), pltpu.VMEM((stacked, MLA_ROPE), BF16),
                        pltpu.VMEM((1, stacked, 1), F32), pltpu.VMEM((1, stacked, 1), F32),
                        pltpu.VMEM((1, stacked, MLA_KV_RANK), F32)],
        compiler_params=_cparams(2, VMEM_LIMIT),
        name="mla_decode",
    )(q, p["wkn"], p["wv"], cache_ckv, cache_kpe.transpose(0, 1, 3, 2), ckv_new, kpe_new)


def _gate_body(x_ref, b_ref, lf_ref, fc_ref, *, new_start, first_row):
    n_rows = x_ref.shape[1]
    tri = (lax.broadcasted_iota(jnp.int32, (TILE, TILE), 0)
           >= lax.broadcasted_iota(jnp.int32, (TILE, TILE), 1)).astype(F32)
    carry = jnp.zeros((1, x_ref.shape[2]), F32)
    for i in range(n_rows // TILE):
        sl = slice(i * TILE, (i + 1) * TILE)
        x = x_ref[0, sl, :]
        rows = i * TILE + lax.broadcasted_iota(jnp.int32, (TILE, 1), 0)
        z = x + b_ref[...]
        lf = jnp.where(rows >= new_start, jnp.minimum(z, 0.0) - jnp.log1p(jnp.exp(-jnp.abs(z))), x)
        if first_row:
            lf = jnp.where(rows >= first_row, lf, 0.0)
        lf_ref[0, sl, :] = lf
        cs = jnp.dot(tri, lf, preferred_element_type=F32, precision=lax.Precision.HIGHEST) + carry
        fc_ref[0, sl, :] = cs
        carry = cs[TILE - 1:TILE, :]


def _gate(x, b_f, new_start, first_row):
    bsz, n_rows, heads = x.shape
    spec = pl.BlockSpec((1, n_rows, heads), lambda b: (b, 0, 0))
    return pl.pallas_call(
        functools.partial(_gate_body, new_start=new_start, first_row=first_row),
        grid=(bsz,),
        in_specs=[spec, _const_spec((1, heads))],
        out_specs=[spec, spec],
        out_shape=[jax.ShapeDtypeStruct(x.shape, F32)] * 2,
        compiler_params=_cparams(1),
        name="fox_gate",
    )(x, b_f)


RET_LOG_GAMMA = tuple(math.log(1.0 - 2.0 ** (-5.0 - h)) for h in range(RET_HEADS))


def _ret_body(rq_ref, rk_ref, rv_ref, rg_ref, s0_ref, dec_ref, cq_ref, s1q_ref, s2q_ref, ck_ref, s1k_ref, s2k_ref,
              o_ref, sl_ref, st_ref, *, n_tail):
    c = pl.program_id(1)
    ct = rq_ref.shape[1]

    @pl.when(c == 0)
    def _():
        st_ref[...] = s0_ref[0]

    q = _rope_lanes(rq_ref[0], cq_ref[...], s1q_ref[...], s2q_ref[...], RET_DK // 2)
    k = _rope_lanes(rk_ref[0], ck_ref[...], s1k_ref[...], s2k_ref[...], RET_DK // 2)
    v = rv_ref[0].astype(BF16)
    g = rg_ref[0]
    j = lax.broadcasted_iota(jnp.int32, (ct, 1), 0).astype(F32)
    for hd in range(RET_HEADS):
        lg = RET_LOG_GAMMA[hd]
        q_h = q[:, hd * RET_DK:(hd + 1) * RET_DK]
        k_h = k[:, hd * RET_DK:(hd + 1) * RET_DK]
        v_h = v[:, hd * RET_DV:(hd + 1) * RET_DV]
        scores = lax.dot_general(q_h.astype(BF16), k_h.astype(BF16), (((1,), (1,)), ((), ())),
                                 preferred_element_type=F32) * dec_ref[hd]
        s_h = st_ref[hd]
        out = jnp.dot(scores.astype(BF16), v_h, preferred_element_type=F32)
        out = out + jnp.dot((q_h * jnp.exp(lg * (j + 1.0))).astype(BF16), s_h.astype(BF16),
                            preferred_element_type=F32)
        k_dec = (k_h * jnp.exp(lg * (ct - 1.0 - j))).astype(BF16)
        st_ref[hd] = math.exp(lg * ct) * s_h + lax.dot_general(
            k_dec, v_h, (((0,), (0,)), ((), ())), preferred_element_type=F32)
        mu = jnp.mean(out, axis=-1, keepdims=True)
        oc = out - mu
        var = jnp.mean(oc * oc, axis=-1, keepdims=True)
        g_h = g[:, hd * RET_DV:(hd + 1) * RET_DV]
        o_ref[0, :, hd * RET_DV:(hd + 1) * RET_DV] = (
            g_h * jax.nn.sigmoid(g_h) * (oc * lax.rsqrt(var + EPS))).astype(o_ref.dtype)

    @pl.when(c == pl.num_programs(1) - 1)
    def _():
        for hd in range(RET_HEADS):
            sl_ref[0, hd] = st_ref[hd] * math.exp(-RET_LOG_GAMMA[hd] * n_tail)


def _retention(h, s0, tabs, ct, n_tail):
    bsz, seq, _ = h.shape
    tab_spec = pl.BlockSpec((ct, RET_QK), lambda b, c: (c, 0))
    st_spec = pl.BlockSpec((1, RET_HEADS, RET_DK, RET_DV), lambda b, c: (b, 0, 0, 0))
    rq_blk = 0
    rv_blk = (2 * RET_QK) // RET_VW
    diff = np.arange(ct, dtype=np.float32)[:, None] - np.arange(ct, dtype=np.float32)[None, :]
    decay = np.where(diff >= 0.0, np.exp(np.asarray(RET_LOG_GAMMA, np.float32)[:, None, None] * np.maximum(diff, 0.0)),
                     np.float32(0.0)).astype(np.float32)
    return pl.pallas_call(
        functools.partial(_ret_body, n_tail=n_tail),
        grid=(bsz, seq // ct),
        in_specs=[
            pl.BlockSpec((1, ct, RET_QK), lambda b, c: (b, c, rq_blk)),
            pl.BlockSpec((1, ct, RET_QK), lambda b, c: (b, c, rq_blk + 1)),
            pl.BlockSpec((1, ct, RET_VW), lambda b, c: (b, c, rv_blk)),
            pl.BlockSpec((1, ct, RET_VW), lambda b, c: (b, c, rv_blk + 1)),
            st_spec, _const_spec((RET_HEADS, ct, ct)),
        ] + [tab_spec] * 6,
        out_specs=[pl.BlockSpec((1, ct, RET_VW), lambda b, c: (b, c, 0)), st_spec],
        out_shape=[jax.ShapeDtypeStruct((bsz, seq, RET_VW), BF16),
                   jax.ShapeDtypeStruct((bsz, RET_HEADS, RET_DK, RET_DV), F32)],
        scratch_shapes=[pltpu.VMEM((RET_HEADS, RET_DK, RET_DV), F32)],
        compiler_params=_cparams(2, VMEM_LIMIT),
        name="retention",
    )(h, h, h, h, s0, decay, *tabs)


def _rope_tables(pos, half, width, group, offset, scale=1.0, valid=None):
    f32 = np.float32
    inv = f32(ROPE_BASE) ** (-np.arange(half, dtype=f32) / f32(half))
    ang = pos.astype(f32)[:, None] * inv[None, :]
    cos, sin = np.cos(ang), np.sin(ang)
    n = pos.shape[0]
    one, zero = np.ones((n, 1), f32), np.zeros((n, 1), f32)

    def lanes(first, second, other):
        grp = np.concatenate([np.broadcast_to(other, (n, offset)), first, second,
                              np.broadcast_to(other, (n, group - offset - 2 * half))], axis=1)
        return np.tile(grp, (1, width // group))

    tabs = (lanes(cos, cos, one), lanes(-sin, 0.0 * sin, zero), lanes(0.0 * sin, sin, zero))
    if valid is not None:
        tabs = tuple(np.where(valid[:, None], t, f32(0.0)) for t in tabs)
    return tuple((t * f32(scale)).astype(f32) for t in tabs)


def _even_params(e, w):
    p = {}
    p["w_in"] = jnp.pad(w["even_w_in"][e], ((0, 0), (0, EVEN_IN_PAD - EVEN_IN))).astype(BF16)
    p["w_out"] = w["even_w_out"][e].astype(BF16)
    lam_re, lam_im = w["s5_a_re"][e].astype(F32), w["s5_a_im"][e].astype(F32)
    dt = jnp.exp(w["s5_log_dt"][e].astype(F32))[:, None]
    p["lam_dt_re"], p["lam_dt_im"] = lam_re * dt, lam_im * dt
    mag = jnp.exp(lam_re * dt)
    abar_re, abar_im = mag * jnp.cos(lam_im * dt), mag * jnp.sin(lam_im * dt)
    den = lam_re * lam_re + lam_im * lam_im
    f_re = ((abar_re - 1.0) * lam_re + abar_im * lam_im) / den
    f_im = (abar_im * lam_re - (abar_re - 1.0) * lam_im) / den
    b_re, b_im = w["s5_b_re"][e].astype(F32), w["s5_b_im"][e].astype(F32)
    bb_re = f_re[..., None] * b_re - f_im[..., None] * b_im
    bb_im = f_re[..., None] * b_im + f_im[..., None] * b_re
    eye = jnp.eye(S5_GROUPS, dtype=F32)

    def in_blocks(x):
        return jnp.einsum("gnc,gh->gchn", x, eye).reshape(S5_WIDTH, S5_HALF)

    def out_blocks(x):
        return jnp.einsum("gcn,gh->gnhc", x, eye).reshape(S5_HALF, S5_WIDTH)

    p["bbig"] = jnp.concatenate([in_blocks(bb_re), in_blocks(bb_im)], axis=1).astype(BF16)
    p["cbig"] = jnp.concatenate([out_blocks(w["s5_c_re"][e].astype(F32)),
                                 out_blocks(-w["s5_c_im"][e].astype(F32))], axis=0).astype(BF16)
    p["a_re"] = abar_re.reshape(1, S5_HALF)
    p["a_im"] = abar_im.reshape(1, S5_HALF)
    p["d"] = w["s5_d"][e].astype(F32).reshape(1, S5_WIDTH)
    p["w_glu"] = w["s5_w_glu"][e].astype(BF16)
    p["b_glu"] = w["s5_b_glu"][e].astype(F32).reshape(1, S5_WIDTH)
    p["q_norm"] = w["mla_q_norm"][e].astype(F32).reshape(1, MLA_Q_RANK)
    p["kv_norm"] = w["mla_kv_norm"][e].astype(F32).reshape(1, MLA_KV_RANK)
    wq = w["mla_w_uq"][e].reshape(MLA_Q_RANK, MLA_HEADS, MLA_NOPE + MLA_ROPE)
    wq = jnp.pad(wq, ((0, 0), (0, 0), (0, MLA_QK_PAD - MLA_NOPE - MLA_ROPE)))
    p["wq"] = wq.reshape(MLA_Q_RANK, MLA_HEADS * MLA_QK_PAD).astype(BF16)
    wkv = w["mla_w_ukv"][e].reshape(MLA_KV_RANK, MLA_HEADS, MLA_NOPE + MLA_V)
    wkn = jnp.pad(wkv[:, :, :MLA_NOPE], ((0, 0), (0, 0), (0, MLA_QK_PAD - MLA_NOPE)))
    p["wkn"] = wkn.reshape(MLA_KV_RANK, MLA_HEADS * MLA_QK_PAD).astype(BF16)
    place = jnp.pad(jnp.eye(MLA_ROPE, dtype=F32), ((0, 0), (MLA_NOPE, MLA_QK_PAD - MLA_NOPE - MLA_ROPE)))
    p["wkp"] = jnp.tile(place, (1, MLA_HEADS)).astype(BF16)
    p["wv"] = wkv[:, :, MLA_NOPE:].reshape(MLA_KV_RANK, MLA_HEADS * MLA_V).astype(BF16)
    return p


def _odd_params(o, w):
    w_in = w["odd_w_in"][o]
    c_logit = 3 * FOX_W
    cols = jnp.concatenate([
        w_in[:, :FOX_W] * (FOX_DIM ** -0.5),
        w_in[:, FOX_W:c_logit],
        w_in[:, c_logit + FOX_HEADS:],
        w_in[:, c_logit:c_logit + FOX_HEADS],
    ], axis=1)
    p = {"w_in": jnp.pad(cols, ((0, 0), (0, ODD_IN_PAD - cols.shape[1]))).astype(BF16)}
    p["w_out"] = w["odd_w_out"][o].astype(BF16)
    p["b_f"] = w["fox_b_f"][o].astype(F32).reshape(1, FOX_HEADS)
    return p


def _cache_tile(n_past):
    for tk in (1024, 512, 256, 128):
        if n_past % tk == 0:
            return tk
    raise ValueError("cache length must be a multiple of 128")


def _trunk(x, pos, n_real, out_rows, past, w, ffn, evens, odds):
    bsz, seq, _ = x.shape
    rows = bsz * seq
    prompt = past is None
    first_row = PAD if prompt else 0
    if prompt:
        tm = next(m * TILE for m in (4, 2, 1) if rows % (m * TILE) == 0 and out_rows[1] % (m * TILE) == 0)
        rt = TILE
    else:
        n_past = past["cache_fox_k"].shape[2]
        tk = _cache_tile(n_past)
        tm, rt = rows, seq
    idx = np.arange(seq)
    valid = (idx >= first_row) & (idx < n_real)
    st = {n: [] for n in ("mla_ckv", "mla_kpe", "s5_re", "s5_im", "fox_k", "fox_v", "fox_logf", "ret")}
    rq_tabs = _rope_tables(pos, RET_DK // 2, RET_QK, RET_DK, 0)
    rk_tabs = _rope_tables(pos, RET_DK // 2, RET_QK, RET_DK, 0, scale=RET_DK ** -0.5, valid=valid)
    x2 = x.reshape(rows, D_MODEL)

    def ln(l, i):
        return w["ln_g"][l, i].reshape(1, D_MODEL), w["ln_b"][l, i].reshape(1, D_MODEL)

    for l in range(DEPTH):
        x2 = _ffn_ln(x2, ffn, (l, 0), *ln(l, 0), tm)
        if l % 2 == 0:
            e = l // 2
            p = evens[e]
            if prompt:
                u, q, ckv, kpe, k_att, v_att = _even_rows(x2.reshape(bsz, seq, D_MODEL), p, pos)
                h0r = h0i = jnp.zeros((bsz, 1, S5_HALF), F32)
                mla_out = _flash(q, k_att, v_att, None, None, heads=MLA_HEADS, dq=MLA_QK_PAD, dv=MLA_V,
                                 causal=False)
            else:
                u = _proj(x2, p["w_in"], tm).reshape(bsz, seq, EVEN_IN_PAD)
                h0r = past["state_s5_re"][e].astype(F32).reshape(bsz, 1, S5_HALF)
                h0i = past["state_s5_im"][e].astype(F32).reshape(bsz, 1, S5_HALF)
                q, ckv, kpe = _mla_rows(u, p, pos, rt)
                mla_out = _mla_decode(q, ckv, kpe, past["cache_mla_ckv"], past["cache_mla_kpe"], e, p, tk)
            s5_out, hlr, hli = _s5(u, h0r, h0i, p, rt, first_row, n_real)
            mix = (s5_out.reshape(rows, S5_WIDTH), mla_out.reshape(rows, MLA_HEADS * MLA_V), p["w_out"])
            st["mla_ckv"].append(ckv)
            st["mla_kpe"].append(kpe)
            st["s5_re"].append(hlr.reshape(bsz, S5_GROUPS, S5_STATE))
            st["s5_im"].append(hli.reshape(bsz, S5_GROUPS, S5_STATE))
        else:
            o = l // 2
            p = odds[o]
            h, k32, v32, fq, fk16, fv = _proj_odd(x2, p["w_in"], bsz, TILE if prompt else tm, attn_layout=prompt)
            h = h.reshape(bsz, seq, ODD_H)
            fk16 = fk16.reshape(bsz, seq, FOX_W)
            f_logit = h[:, :, ODD_LOGIT_COL:ODD_LOGIT_COL + FOX_HEADS]
            if prompt:
                q_off = 0
                logf, fcum = _gate(f_logit, p["b_f"], 0, PAD)
                fcum = fcum * LOG2E
                fox_out = _flash(fq, fk16, fv, fcum.transpose(0, 2, 1), fcum, heads=FOX_HEADS, dq=FOX_DIM,
                                 dv=FOX_DIM, causal=True)
                s0 = jnp.zeros((bsz, RET_HEADS, RET_DK, RET_DV), F32)
            else:
                q_off = n_past
                gates = jnp.concatenate([past["cache_fox_logf"][o].astype(F32), f_logit], axis=1)
                gates = jnp.pad(gates, ((0, 0), (0, _round_up(n_past + seq, TILE) - n_past - seq), (0, 0)))
                logf, fcum = _gate(gates, p["b_f"], n_past, 0)
                fox_out = _fox_decode(fq.reshape(bsz, seq, FOX_W), fk16, fv.reshape(bsz, seq, FOX_W),
                                      past["cache_fox_k"], past["cache_fox_v"], o, fcum * LOG2E, tk)
                s0 = past["state_ret"][o].astype(F32)
            ret_out, s_last = _retention(h, s0, rq_tabs + rk_tabs, rt, seq - n_real)
            mix = (fox_out.reshape(rows, FOX_W), ret_out.reshape(rows, RET_VW), p["w_out"])
            if prompt:
                k32, v32 = k32.transpose(0, 3, 1, 2), v32.transpose(0, 3, 1, 2)
            st["fox_k"].append(k32.reshape(bsz, seq, FOX_HEADS, FOX_DIM))
            st["fox_v"].append(v32.reshape(bsz, seq, FOX_HEADS, FOX_DIM))
            st["fox_logf"].append(logf[:, q_off:q_off + seq])
            st["ret"].append(s_last)
        keep = (seq,) + out_rows if l == DEPTH - 1 and out_rows != (0, seq) else None
        x2 = _mix_ffn_ln(x2, *mix, ln(l, 1), ffn, (l, 1), ln(l, 2), tm, keep)
    return x2.reshape(bsz, out_rows[1], D_MODEL), {n: jnp.stack(a) for n, a in st.items()}


def kernel(x_prompt, x_sample, cache_mla_ckv, cache_mla_kpe, cache_fox_k, cache_fox_v, cache_fox_logf,
           state_s5_re, state_s5_im, state_ret, meta_tokens, ln_g, ln_b, ffn_w_gate, ffn_w_up, ffn_w_down,
           even_w_in, even_w_out, s5_a_re, s5_a_im, s5_b_re, s5_b_im, s5_c_re, s5_c_im, s5_d, s5_log_dt,
           s5_w_glu, s5_b_glu, mla_q_norm, mla_kv_norm, mla_w_uq, mla_w_ukv, odd_w_in, odd_w_out, fox_b_f):
    w = dict(ln_g=ln_g.astype(F32), ln_b=ln_b.astype(F32), even_w_in=even_w_in, even_w_out=even_w_out,
             s5_a_re=s5_a_re, s5_a_im=s5_a_im, s5_b_re=s5_b_re, s5_b_im=s5_b_im, s5_c_re=s5_c_re,
             s5_c_im=s5_c_im, s5_d=s5_d, s5_log_dt=s5_log_dt, s5_w_glu=s5_w_glu, s5_b_glu=s5_b_glu,
             mla_q_norm=mla_q_norm, mla_kv_norm=mla_kv_norm, mla_w_uq=mla_w_uq, mla_w_ukv=mla_w_ukv,
             odd_w_in=odd_w_in, odd_w_out=odd_w_out, fox_b_f=fox_b_f)
    past = dict(cache_mla_ckv=cache_mla_ckv, cache_mla_kpe=cache_mla_kpe, cache_fox_k=cache_fox_k,
                cache_fox_v=cache_fox_v, cache_fox_logf=cache_fox_logf, state_s5_re=state_s5_re,
                state_s5_im=state_s5_im, state_ret=state_ret)
    ffn = (ffn_w_gate.astype(BF16), ffn_w_up.astype(BF16), ffn_w_down.astype(BF16))
    evens = [_even_params(e, w) for e in range((DEPTH + 1) // 2)]
    odds = [_odd_params(o, w) for o in range(DEPTH // 2)]

    bsz, seq, _ = x_prompt.shape
    n_real = PAD + N_META + seq
    n_rows = _round_up(n_real, TILE)
    meta = jnp.broadcast_to(meta_tokens[None].astype(x_prompt.dtype), (bsz, N_META, D_MODEL))
    xp = jnp.concatenate([jnp.zeros((bsz, PAD, D_MODEL), x_prompt.dtype), meta, x_prompt,
                          jnp.zeros((bsz, n_rows - n_real, D_MODEL), x_prompt.dtype)], axis=1)
    pos_p = np.maximum(np.arange(n_rows) - PAD, 0)
    y_p, st_p = _trunk(xp, pos_p, n_real, (PAD + N_META, seq), None, w, ffn, evens, odds)
    d_seq = x_sample.shape[1]
    pos_s = N_META + cache_fox_k.shape[2] + np.arange(d_seq)
    y_s, st_s = _trunk(x_sample, pos_s, d_seq, (0, d_seq), past, w, ffn, evens, odds)

    def real(a):
        return a[:, :, PAD:n_real]

    return (y_p, y_s,
            real(st_p["mla_ckv"]), real(st_p["mla_kpe"]), real(st_p["fox_k"]), real(st_p["fox_v"]),
            real(st_p["fox_logf"]), st_p["s5_re"], st_p["s5_im"], st_p["ret"],
            st_s["mla_ckv"], st_s["mla_kpe"], st_s["fox_k"], st_s["fox_v"], st_s["fox_logf"],
            st_s["s5_re"], st_s["s5_im"], st_s["ret"])
```

```python
import functools
import math

import jax
import jax.numpy as jnp
import numpy as np
from jax import lax
from jax.experimental import pallas as pl
from jax.experimental.pallas import tpu as pltpu

F32 = jnp.float32
BF16 = jnp.bfloat16

D_MODEL = 1024
DEPTH = 4
CHUNK = 64
CHUNK_SHIFT = 6
N_META = 16
S5_WIDTH = 512
S5_CH = 16
S5_GROUPS = S5_WIDTH // S5_CH
S5_STATE = 64
S5_HALF = S5_GROUPS * S5_STATE
MLA_HEADS = 8
MLA_Q_RANK = 256
MLA_KV_RANK = 128
MLA_NOPE = 64
MLA_ROPE = 32
MLA_V = 64
MLA_QK_PAD = 128
FOX_HEADS = 8
FOX_DIM = 64
FOX_W = FOX_HEADS * FOX_DIM
RET_HEADS = 4
RET_DK = 64
RET_DV = 128
RET_QK = RET_HEADS * RET_DK
RET_VW = RET_HEADS * RET_DV
D_FF = 2816
ROPE_BASE = 10000.0
ALPHA = (2.0 * DEPTH) ** 0.25
EPS = 1e-5
NEG = -1e30
LOG2E = math.log2(math.e)
EVEN_IN = S5_WIDTH + MLA_Q_RANK + MLA_KV_RANK + MLA_ROPE
EVEN_IN_PAD = 1024
ODD_IN_PAD = 3200
ODD_H = ODD_IN_PAD - 3 * FOX_W
ODD_LOGIT_COL = 2 * RET_QK + 2 * RET_VW

PAD = CHUNK - N_META
LANE = 128
SUBLANE = 8
TILE = 256
VMEM_LIMIT = 56 * 1024 * 1024


def _cparams(n_grid, vmem=None):
    return pltpu.CompilerParams(dimension_semantics=("arbitrary",) * n_grid, vmem_limit_bytes=vmem)


def _const_spec(shape):
    nd = len(shape)
    return pl.BlockSpec(shape, lambda *_: (0,) * nd)


def _round_up(n, m):
    return -(-n // m) * m


def _layer_norm_rows(z, g, b):
    mu = jnp.mean(z, axis=-1, keepdims=True)
    zc = z - mu
    var = jnp.mean(zc * zc, axis=-1, keepdims=True)
    return zc * lax.rsqrt(var + EPS) * g + b


def _ffn_rows(x, wg_ref, wu_ref, wd_ref, g_ref, b_ref, hid_ref):
    xb = x.astype(BF16)
    for c in range(D_FF // TILE):
        sl = slice(c * TILE, (c + 1) * TILE)
        hg = jnp.dot(xb, wg_ref[:, sl], preferred_element_type=F32)
        hu = jnp.dot(xb, wu_ref[:, sl], preferred_element_type=F32)
        hid_ref[:, sl] = (hg * jax.nn.sigmoid(hg) * hu).astype(BF16)
    y = jnp.dot(hid_ref[...], wd_ref[...], preferred_element_type=F32)
    return _layer_norm_rows(ALPHA * x + 0.5 * y, g_ref[...], b_ref[...])


def _ffn_body(x_ref, wg_ref, wu_ref, wd_ref, g_ref, b_ref, o_ref, hid_ref):
    o_ref[...] = _ffn_rows(x_ref[...], wg_ref, wu_ref, wd_ref, g_ref, b_ref, hid_ref)


def _ffn_weight_specs(which):
    once = dict(pipeline_mode=pl.Buffered(1))
    pick = lambda i: which + (0, 0)
    return [pl.BlockSpec((None, None, D_MODEL, D_FF), pick, **once),
            pl.BlockSpec((None, None, D_MODEL, D_FF), pick, **once),
            pl.BlockSpec((None, None, D_FF, D_MODEL), pick, **once),
            _const_spec((1, D_MODEL)), _const_spec((1, D_MODEL))]


def _ffn_ln(x, ffn, which, g, b, tm):
    rows = x.shape[0]
    return pl.pallas_call(
        _ffn_body,
        grid=(rows // tm,),
        in_specs=[pl.BlockSpec((tm, D_MODEL), lambda i: (i, 0))] + _ffn_weight_specs(which),
        out_specs=pl.BlockSpec((tm, D_MODEL), lambda i: (i, 0)),
        out_shape=jax.ShapeDtypeStruct((rows, D_MODEL), F32),
        scratch_shapes=[pltpu.VMEM((tm, D_FF), BF16)],
        compiler_params=_cparams(1, VMEM_LIMIT),
        name="ffn_ln",
    )(x, *ffn, g, b)


def _mix_ffn_body(x_ref, a1_ref, a2_ref, wo_ref, g1_ref, b1_ref, wg_ref, wu_ref, wd_ref, g2_ref, b2_ref,
                  o_ref, hid_ref):
    k1 = a1_ref.shape[1]
    y = jnp.dot(a1_ref[...], wo_ref[:k1, :], preferred_element_type=F32)
    y = y + jnp.dot(a2_ref[...], wo_ref[k1:, :], preferred_element_type=F32)
    x1 = _layer_norm_rows(ALPHA * x_ref[...] + y, g1_ref[...], b1_ref[...])
    o_ref[...] = _ffn_rows(x1, wg_ref, wu_ref, wd_ref, g2_ref, b2_ref, hid_ref)


def _mix_ffn_ln(x, a1, a2, w_out, ln1, ffn, which, ln2, tm, keep=None):
    rows = x.shape[0]
    if keep is None:
        n_steps = rows // tm
        row_spec = lambda n: pl.BlockSpec((tm, n), lambda i: (i, 0))
    else:
        seq, start, count = keep
        assert count % tm == 0
        per = count // tm
        n_steps = (rows // seq) * per
        align = math.gcd(seq, start, tm)
        row_spec = lambda n: pl.BlockSpec(
            (pl.Element(tm), pl.Element(n)),
            lambda i: (pl.multiple_of((i // per) * seq + start + (i % per) * tm, align), 0))
    return pl.pallas_call(
        _mix_ffn_body,
        grid=(n_steps,),
        in_specs=[row_spec(D_MODEL), row_spec(a1.shape[1]), row_spec(a2.shape[1]),
                  pl.BlockSpec(w_out.shape, lambda i: (0, 0), pipeline_mode=pl.Buffered(1)),
                  _const_spec((1, D_MODEL)), _const_spec((1, D_MODEL))] + _ffn_weight_specs(which),
        out_specs=pl.BlockSpec((tm, D_MODEL), lambda i: (i, 0)),
        out_shape=jax.ShapeDtypeStruct((n_steps * tm, D_MODEL), F32),
        scratch_shapes=[pltpu.VMEM((tm, D_FF), BF16)],
        compiler_params=_cparams(1, VMEM_LIMIT),
        name="mix_ffn_ln",
    )(x, a1, a2, w_out, *ln1, *ffn, *ln2)


def _proj_body(x_ref, w_ref, o_ref):
    o_ref[...] = jnp.dot(x_ref[...].astype(BF16), w_ref[...], preferred_element_type=F32)


def _proj(x, w, tm):
    rows, n = x.shape[0], w.shape[1]
    return pl.pallas_call(
        _proj_body,
        grid=(rows // tm,),
        in_specs=[pl.BlockSpec((tm, D_MODEL), lambda i: (i, 0)), _const_spec(w.shape)],
        out_specs=pl.BlockSpec((tm, n), lambda i: (i, 0)),
        out_shape=jax.ShapeDtypeStruct((rows, n), F32),
        compiler_params=_cparams(1, VMEM_LIMIT),
        name="in_proj_even",
    )(x, w)


def _proj_odd_body(x_ref, w_ref, h_ref, k32_ref, v32_ref, q_ref, k_ref, v_ref, *, attn_layout):
    y = jnp.dot(x_ref[...].astype(BF16), w_ref[...], preferred_element_type=F32)
    h_ref[...] = y[:, 3 * FOX_W:]
    q = y[:, :FOX_W] * LOG2E
    k, v = y[:, FOX_W:2 * FOX_W], y[:, 2 * FOX_W:3 * FOX_W]
    k_ref[...] = k.astype(BF16)
    if attn_layout:
        q_ref[0] = q.T.astype(BF16)
        k_t, v_t = k.T, v.T
        for hd in range(FOX_HEADS):
            rows = slice(hd * FOX_DIM, (hd + 1) * FOX_DIM)
            k32_ref[0, hd] = k_t[rows, :]
            v32_ref[0, hd] = v_t[rows, :]
            v_ref[0, hd, 0] = v_t[rows, :].astype(BF16)
    else:
        k32_ref[...] = k
        v32_ref[...] = v
        q_ref[...] = q.astype(BF16)
        v_ref[...] = v.astype(BF16)


def _proj_odd(x, w, bsz, tm, attn_layout):
    rows = x.shape[0]
    seq = rows // bsz
    row_spec = lambda n: pl.BlockSpec((tm, n), lambda i: (i, 0))
    if attn_layout:
        assert tm == TILE and seq % TILE == 0
        n_t = seq // TILE
        kv32_shape = jax.ShapeDtypeStruct((bsz, FOX_HEADS, FOX_DIM, seq), F32)
        kv32_spec = pl.BlockSpec((1, FOX_HEADS, FOX_DIM, TILE), lambda i: (i // n_t, 0, 0, i % n_t))
        q_shape = jax.ShapeDtypeStruct((bsz, FOX_W, seq), BF16)
        q_spec = pl.BlockSpec((1, FOX_W, TILE), lambda i: (i // n_t, 0, i % n_t))
        v_shape = jax.ShapeDtypeStruct((bsz, FOX_HEADS, n_t, FOX_DIM, TILE), BF16)
        v_spec = pl.BlockSpec((1, FOX_HEADS, 1, FOX_DIM, TILE), lambda i: (i // n_t, 0, i % n_t, 0, 0))
    else:
        kv32_shape, kv32_spec = jax.ShapeDtypeStruct((rows, FOX_W), F32), row_spec(FOX_W)
        q_shape = v_shape = jax.ShapeDtypeStruct((rows, FOX_W), BF16)
        q_spec = v_spec = row_spec(FOX_W)
    return pl.pallas_call(
        functools.partial(_proj_odd_body, attn_layout=attn_layout),
        grid=(rows // tm,),
        in_specs=[row_spec(D_MODEL), _const_spec(w.shape)],
        out_specs=[row_spec(ODD_H), kv32_spec, kv32_spec, q_spec, row_spec(FOX_W), v_spec],
        out_shape=[jax.ShapeDtypeStruct((rows, ODD_H), F32), kv32_shape, kv32_shape, q_shape,
                   jax.ShapeDtypeStruct((rows, FOX_W), BF16), v_shape],
        compiler_params=_cparams(1, VMEM_LIMIT),
        name="in_proj_odd",
    )(x, w)


def _s5_body(u_ref, h0r_ref, h0i_ref, ar_ref, ai_ref, asr_ref, asi_ref, bbig_ref, cbig_ref, d_ref, wglu_ref,
             bglu_ref, o_ref, hlr_ref, hli_ref, hs_ref, st_ref, *, first_row, last_seg):
    t = pl.program_id(1)
    rt = u_ref.shape[1]
    seg = rt // SUBLANE

    @pl.when(t == 0)
    def _():
        st_ref[0:1, :] = h0r_ref[0]
        st_ref[1:2, :] = h0i_ref[0]

    u = u_ref[0]
    if first_row:
        rows = t * rt + lax.broadcasted_iota(jnp.int32, (rt, 1), 0)
        u = jnp.where(rows >= first_row, u, 0.0)
    i0 = lax.broadcasted_iota(jnp.int32, (rt, rt), 0)
    i1 = lax.broadcasted_iota(jnp.int32, (rt, rt), 1)
    regroup = (i1 == (i0 & (SUBLANE - 1)) * seg + (i0 >> 3)).astype(BF16)
    restore = (i0 == (i1 & (SUBLANE - 1)) * seg + (i1 >> 3)).astype(BF16)
    ub = jnp.dot(regroup, u.astype(BF16), preferred_element_type=F32).astype(BF16)

    half_w, half_s = S5_WIDTH // 2, S5_HALF // 2
    for kb in range(2):
        for part in range(2):
            c0 = part * S5_HALF + kb * half_s
            hs_ref[:, c0:c0 + half_s] = jnp.dot(
                ub[:, kb * half_w:(kb + 1) * half_w], bbig_ref[kb * half_w:(kb + 1) * half_w, c0:c0 + half_s],
                preferred_element_type=F32)

    def scan(lanes, start, store):
        re_l = lanes
        im_l = slice(S5_HALF + lanes.start, S5_HALF + lanes.stop)
        ar = jnp.broadcast_to(ar_ref[:, lanes], (SUBLANE, lanes.stop - lanes.start))
        ai = jnp.broadcast_to(ai_ref[:, lanes], (SUBLANE, lanes.stop - lanes.start))

        def step(k, carry):
            hr, hi = carry
            base = k * SUBLANE
            nr = ar * hr - ai * hi + hs_ref[pl.ds(base, SUBLANE), re_l]
            ni = ar * hi + ai * hr + hs_ref[pl.ds(base, SUBLANE), im_l]
            if store:
                hs_ref[pl.ds(base, SUBLANE), re_l] = nr
                hs_ref[pl.ds(base, SUBLANE), im_l] = ni
            return nr, ni

        carry = start
        for k in range(seg):
            carry = step(k, carry)
        return carry

    slabs = [slice(c * half_s, (c + 1) * half_s) for c in range(2)]
    zero = jnp.zeros((SUBLANE, half_s), F32)
    ends = [scan(lanes, (zero, zero), False) for lanes in slabs]
    er = jnp.concatenate([e[0] for e in ends], axis=1)
    ei = jnp.concatenate([e[1] for e in ends], axis=1)
    asr, asi = asr_ref[...], asi_ref[...]
    sr, si = st_ref[0:1, :], st_ref[1:2, :]
    start_r, start_i = [], []
    for j in range(SUBLANE):
        start_r.append(sr)
        start_i.append(si)
        sr, si = asr * sr - asi * si + er[j:j + 1], asr * si + asi * sr + ei[j:j + 1]
    st_ref[0:1, :] = sr
    st_ref[1:2, :] = si
    start_r8, start_i8 = jnp.concatenate(start_r, axis=0), jnp.concatenate(start_i, axis=0)
    for lanes in slabs:
        scan(lanes, (start_r8[:, lanes], start_i8[:, lanes]), True)

    hb = hs_ref[...].astype(BF16)
    ys = []
    for kb in range(2):
        re_rows = slice(kb * half_s, (kb + 1) * half_s)
        im_rows = slice(S5_HALF + kb * half_s, S5_HALF + (kb + 1) * half_s)
        cols = slice(kb * half_w, (kb + 1) * half_w)
        ys.append(jnp.dot(hb[:, re_rows], cbig_ref[re_rows, cols], preferred_element_type=F32)
                  + jnp.dot(hb[:, im_rows], cbig_ref[im_rows, cols], preferred_element_type=F32))
    y = jnp.concatenate(ys, axis=1)
    y_hi = y.astype(BF16)
    y_lo = (y - y_hi.astype(F32)).astype(BF16)
    y = (jnp.dot(restore, y_hi, preferred_element_type=F32) + jnp.dot(restore, y_lo, preferred_element_type=F32)
         + d_ref[...] * u)
    g = jax.nn.gelu(y)
    gate = jnp.dot(g.astype(BF16), wglu_ref[...], preferred_element_type=F32) + bglu_ref[...]
    o_ref[0] = (g * jax.nn.sigmoid(gate)).astype(o_ref.dtype)

    @pl.when(t == pl.num_programs(1) - 1)
    def _():
        hlr_ref[0] = sr if last_seg == SUBLANE else start_r[last_seg]
        hli_ref[0] = si if last_seg == SUBLANE else start_i[last_seg]


def _s5(h, h0r, h0i, p, rt, first_row, n_real):
    bsz, seq, _ = h.shape
    seg = rt // SUBLANE
    real_in_last = n_real - (seq - rt)
    assert 0 < real_in_last <= rt and real_in_last % seg == 0
    as_re = (jnp.exp(p["lam_dt_re"] * seg) * jnp.cos(p["lam_dt_im"] * seg)).reshape(1, S5_HALF)
    as_im = (jnp.exp(p["lam_dt_re"] * seg) * jnp.sin(p["lam_dt_im"] * seg)).reshape(1, S5_HALF)
    state_spec = pl.BlockSpec((1, 1, S5_HALF), lambda b, t: (b, 0, 0))
    return pl.pallas_call(
        functools.partial(_s5_body, first_row=first_row, last_seg=real_in_last // seg),
        grid=(bsz, seq // rt),
        in_specs=[
            pl.BlockSpec((1, rt, S5_WIDTH), lambda b, t: (b, t, 0)),
            state_spec, state_spec,
            _const_spec((1, S5_HALF)), _const_spec((1, S5_HALF)), _const_spec((1, S5_HALF)), _const_spec((1, S5_HALF)),
            _const_spec((S5_WIDTH, 2 * S5_HALF)), _const_spec((2 * S5_HALF, S5_WIDTH)),
            _const_spec((1, S5_WIDTH)), _const_spec((S5_WIDTH, S5_WIDTH)), _const_spec((1, S5_WIDTH)),
        ],
        out_specs=[pl.BlockSpec((1, rt, S5_WIDTH), lambda b, t: (b, t, 0)), state_spec, state_spec],
        out_shape=[jax.ShapeDtypeStruct((bsz, seq, S5_WIDTH), BF16),
                   jax.ShapeDtypeStruct((bsz, 1, S5_HALF), F32),
                   jax.ShapeDtypeStruct((bsz, 1, S5_HALF), F32)],
        scratch_shapes=[pltpu.VMEM((rt, 2 * S5_HALF), F32), pltpu.VMEM((2, S5_HALF), F32)],
        compiler_params=_cparams(2, VMEM_LIMIT),
        name="s5",
    )(h, h0r, h0i, p["a_re"], p["a_im"], as_re, as_im, p["bbig"], p["cbig"], p["d"], p["w_glu"], p["b_glu"])


def _rope_lanes(x, c, s1, s2, half):
    n = x.shape[-1]
    return x * c + pltpu.roll(x, n - half, 1) * s1 + pltpu.roll(x, half, 1) * s2


def _rms_rows(x, g):
    return x * lax.rsqrt(jnp.mean(x * x, axis=-1, keepdims=True) + EPS) * g


def _mla_latents(h, kn_ref, ck_ref, s1k_ref, s2k_ref, ckv_ref, kpe_ref):
    ckv = _rms_rows(h[:, MLA_Q_RANK:MLA_Q_RANK + MLA_KV_RANK], kn_ref[...])
    kpe = _rope_lanes(h[:, MLA_Q_RANK + MLA_KV_RANK:], ck_ref[...], s1k_ref[...], s2k_ref[...],
                      MLA_ROPE // 2)[:, :MLA_ROPE]
    ckv_ref[0] = ckv
    kpe_ref[0] = kpe
    return ckv, kpe


def _mla_rows_body(h_ref, qn_ref, kn_ref, wq_ref, cq_ref, s1q_ref, s2q_ref, ck_ref, s1k_ref, s2k_ref,
                   q_ref, ckv_ref, kpe_ref):
    h = h_ref[0]
    q_lat = _rms_rows(h[:, :MLA_Q_RANK], qn_ref[...]).astype(BF16)
    q = jnp.dot(q_lat, wq_ref[...], preferred_element_type=F32)
    cq, s1q, s2q = cq_ref[...], s1q_ref[...], s2q_ref[...]
    for hd in range(MLA_HEADS):
        sl = slice(hd * MLA_QK_PAD, (hd + 1) * MLA_QK_PAD)
        q_ref[0, :, sl] = _rope_lanes(q[:, sl], cq, s1q, s2q, MLA_ROPE // 2).astype(BF16)
    _mla_latents(h, kn_ref, ck_ref, s1k_ref, s2k_ref, ckv_ref, kpe_ref)


def _even_rows_body(x_ref, win_ref, qn_ref, kn_ref, wqt_ref, cos_ref, sin_ref, ck_ref, s1k_ref, s2k_ref,
                    wkn_ref, wkp_ref, wv_ref, u_ref, q_ref, ckv_ref, kpe_ref, k_ref, v_ref, *, scale):
    y = jnp.dot(x_ref[0].astype(BF16), win_ref[...], preferred_element_type=F32)
    u_ref[0] = y[:, :S5_WIDTH]
    h = y[:, S5_WIDTH:]
    q_lat = _rms_rows(h[:, :MLA_Q_RANK], qn_ref[...]).astype(BF16)
    q_t = _dot_nt(wqt_ref[...], q_lat)
    cos_t, sin_t = cos_ref[...], sin_ref[...]
    half = MLA_ROPE // 2
    for hd in range(MLA_HEADS):
        r0 = hd * MLA_QK_PAD
        x1 = q_t[r0 + MLA_NOPE:r0 + MLA_NOPE + half, :]
        x2 = q_t[r0 + MLA_NOPE + half:r0 + MLA_NOPE + 2 * half, :]
        q_ref[0, r0:r0 + MLA_QK_PAD, :] = jnp.concatenate(
            [q_t[r0:r0 + MLA_NOPE, :] * scale, x1 * cos_t - x2 * sin_t, x1 * sin_t + x2 * cos_t,
             q_t[r0 + MLA_NOPE + 2 * half:r0 + MLA_QK_PAD, :]], axis=0).astype(BF16)
    ckv, kpe = _mla_latents(h, kn_ref, ck_ref, s1k_ref, s2k_ref, ckv_ref, kpe_ref)
    ckv = ckv.astype(BF16)
    k = jnp.dot(ckv, wkn_ref[...], preferred_element_type=F32)
    k = k + jnp.dot(kpe.astype(BF16), wkp_ref[...], preferred_element_type=F32)
    k_ref[0] = k.astype(BF16)
    v_t = jnp.dot(ckv, wv_ref[...], preferred_element_type=F32).T
    for hd in range(MLA_HEADS):
        v_ref[0, hd, 0] = v_t[hd * MLA_V:(hd + 1) * MLA_V, :].astype(BF16)


def _mla_scale():
    return (MLA_NOPE + MLA_ROPE) ** -0.5 * LOG2E


def _even_rows(x, p, pos):
    bsz, seq, _ = x.shape
    n_t = seq // TILE
    n_q = MLA_HEADS * MLA_QK_PAD
    half = MLA_ROPE // 2
    scale = _mla_scale()
    ang = (ROPE_BASE ** (-np.arange(half, dtype=np.float32) / half))[:, None] * pos.astype(np.float32)[None, :]
    q_tabs = ((np.cos(ang) * scale).astype(np.float32), (np.sin(ang) * scale).astype(np.float32))
    row_spec = lambda n: pl.BlockSpec((1, TILE, n), lambda b, t: (b, t, 0))
    consts = [p["w_in"], p["q_norm"], p["kv_norm"], p["wq"].T]
    weights = [p["wkn"], p["wkp"], p["wv"]]
    return pl.pallas_call(
        functools.partial(_even_rows_body, scale=scale),
        grid=(bsz, n_t),
        in_specs=[row_spec(D_MODEL)] + [_const_spec(c.shape) for c in consts]
        + [pl.BlockSpec((half, TILE), lambda b, t: (0, t))] * 2
        + [pl.BlockSpec((TILE, LANE), lambda b, t: (t, 0))] * 3 + [_const_spec(c.shape) for c in weights],
        out_specs=[row_spec(S5_WIDTH), pl.BlockSpec((1, n_q, TILE), lambda b, t: (b, 0, t)),
                   row_spec(MLA_KV_RANK), row_spec(MLA_ROPE), row_spec(n_q),
                   pl.BlockSpec((1, MLA_HEADS, 1, MLA_V, TILE), lambda b, t: (b, 0, t, 0, 0))],
        out_shape=[jax.ShapeDtypeStruct((bsz, seq, S5_WIDTH), F32),
                   jax.ShapeDtypeStruct((bsz, n_q, seq), BF16),
                   jax.ShapeDtypeStruct((bsz, seq, MLA_KV_RANK), F32),
                   jax.ShapeDtypeStruct((bsz, seq, MLA_ROPE), F32),
                   jax.ShapeDtypeStruct((bsz, seq, n_q), BF16),
                   jax.ShapeDtypeStruct((bsz, MLA_HEADS, n_t, MLA_V, TILE), BF16)],
        compiler_params=_cparams(2, VMEM_LIMIT),
        name="even_rows",
    )(x, *consts, *q_tabs, *_rope_tables(pos, half, LANE, LANE, 0), *weights)


def _mla_rows(h, p, pos, tm):
    bsz, seq, _ = h.shape
    tab_spec = pl.BlockSpec((tm, LANE), lambda b, t: (t, 0))
    n_q = MLA_HEADS * MLA_QK_PAD
    half = MLA_ROPE // 2
    tabs = (_rope_tables(pos, half, MLA_QK_PAD, MLA_QK_PAD, MLA_NOPE, scale=_mla_scale())
            + _rope_tables(pos, half, LANE, LANE, 0))
    return pl.pallas_call(
        _mla_rows_body,
        grid=(bsz, seq // tm),
        in_specs=[
            pl.BlockSpec((1, tm, EVEN_IN_PAD - S5_WIDTH), lambda b, t: (b, t, 1)),
            _const_spec((1, MLA_Q_RANK)), _const_spec((1, MLA_KV_RANK)),
            _const_spec(p["wq"].shape),
        ] + [tab_spec] * 6,
        out_specs=[
            pl.BlockSpec((1, tm, n_q), lambda b, t: (b, t, 0)),
            pl.BlockSpec((1, tm, MLA_KV_RANK), lambda b, t: (b, t, 0)),
            pl.BlockSpec((1, tm, MLA_ROPE), lambda b, t: (b, t, 0)),
        ],
        out_shape=[jax.ShapeDtypeStruct((bsz, seq, n_q), BF16),
                   jax.ShapeDtypeStruct((bsz, seq, MLA_KV_RANK), F32),
                   jax.ShapeDtypeStruct((bsz, seq, MLA_ROPE), F32)],
        compiler_params=_cparams(2, VMEM_LIMIT),
        name="mla_rows",
    )(h, p["q_norm"], p["kv_norm"], p["wq"], *tabs)


def _flash_body(*refs, heads, dq, dv, causal):
    if causal:
        qt_ref, k_ref, vt_ref, fq_ref, fk_ref, o_ref = refs
    else:
        qt_ref, k_ref, vt_ref, o_ref = refs
    tq = qt_ref.shape[2]
    n_tiles, tk = vt_ref.shape[2], vt_ref.shape[4]
    r0 = pl.program_id(1) * tq
    j_last = jnp.minimum((r0 + tq - 1) // tk, n_tiles - 1)
    q_row = r0 + lax.broadcasted_iota(jnp.int32, (1, tq), 1)
    last = q_row if causal else ((q_row >> CHUNK_SHIFT) << CHUNK_SHIFT) + (CHUNK - 1)
    first_last = r0 if causal else ((r0 >> CHUNK_SHIFT) << CHUNK_SHIFT) + (CHUNK - 1)
    n_open = jnp.clip((first_last + 1) // tk, 1, j_last + 1)
    q_t = [qt_ref[0, hd * dq:(hd + 1) * dq, :] for hd in range(heads)]

    def tile(j, carry, masked, wide):
        width = 2 * tk if wide else tk
        start = pl.multiple_of(j * tk, tk)
        if masked:
            k_row = start + lax.broadcasted_iota(jnp.int32, (width, 1), 0)
            ok = (k_row <= last) & (k_row >= PAD)
        scores = [jnp.dot(k_ref[0, pl.ds(start, width), hd * dq:(hd + 1) * dq], q_t[hd],
                          preferred_element_type=F32) for hd in range(heads)]
        probs = []
        for hd in range(heads):
            m, l, _ = carry[hd]
            s = scores[hd]
            if causal:
                s = s + (fq_ref[0, hd:hd + 1, :] - fk_ref[0, pl.ds(start, width), hd:hd + 1])
            if masked:
                s = jnp.where(ok, s, NEG)
            m_new = jnp.maximum(m, jnp.max(s, axis=0, keepdims=True))
            a = jnp.exp2(m - m_new)
            p = jnp.exp2(s - m_new)
            probs.append((m_new, a * l + jnp.sum(p, axis=0, keepdims=True), a, p.astype(BF16)))
        new = []
        for hd in range(heads):
            m_new, l, a, p = probs[hd]
            acc = a * carry[hd][2] + jnp.dot(vt_ref[0, hd, j], p[:tk], preferred_element_type=F32)
            if wide:
                acc = acc + jnp.dot(vt_ref[0, hd, j + 1], p[tk:], preferred_element_type=F32)
            new.append((m_new, l, acc))
        return tuple(new)

    init = tuple((jnp.full((1, tq), NEG, F32), jnp.zeros((1, tq), F32), jnp.zeros((dv, tq), F32))
                 for _ in range(heads))
    carry = tile(0, init, True, False)
    n_wide = (n_open - 1) // 2
    carry = lax.fori_loop(0, n_wide, lambda i, c: tile(1 + 2 * i, c, False, True), carry)
    carry = lax.fori_loop(1 + 2 * n_wide, n_open, functools.partial(tile, masked=False, wide=False), carry)
    carry = lax.fori_loop(n_open, j_last + 1, functools.partial(tile, masked=True, wide=False), carry)
    outs = [carry[hd][2] / carry[hd][1] for hd in range(heads)]
    per = LANE // dv
    for g in range(heads // per):
        o_ref[0, :, g * LANE:(g + 1) * LANE] = jnp.concatenate(
            outs[g * per:(g + 1) * per], axis=0).T.astype(o_ref.dtype)


def _flash(q_t, k, v_t, fq_t, fk, *, heads, dq, dv, causal):
    bsz, _, n_q = q_t.shape
    n_keys = k.shape[1]
    in_specs = [
        pl.BlockSpec((1, heads * dq, TILE), lambda b, i: (b, 0, i)),
        pl.BlockSpec((1, n_keys, heads * dq), lambda b, i: (b, 0, 0)),
        pl.BlockSpec((1,) + v_t.shape[1:], lambda b, i: (b, 0, 0, 0, 0)),
    ]
    args = [q_t, k, v_t]
    if causal:
        in_specs += [pl.BlockSpec((1, heads, TILE), lambda b, i: (b, 0, i)),
                     pl.BlockSpec((1, n_keys, heads), lambda b, i: (b, 0, 0))]
        args += [fq_t, fk]
    return pl.pallas_call(
        functools.partial(_flash_body, heads=heads, dq=dq, dv=dv, causal=causal),
        grid=(bsz, n_q // TILE),
        in_specs=in_specs,
        out_specs=pl.BlockSpec((1, TILE, heads * dv), lambda b, i: (b, i, 0)),
        out_shape=jax.ShapeDtypeStruct((bsz, n_q, heads * dv), BF16),
        compiler_params=_cparams(2, VMEM_LIMIT),
        name="flash_causal" if causal else "flash_chunk",
    )(*args)


def _online_softmax(s, m_ref, l_ref, idx):
    m_old = m_ref[idx]
    m_new = jnp.maximum(m_old, jnp.max(s, axis=-1, keepdims=True))
    a = jnp.exp2(m_old - m_new)
    p = jnp.exp2(s - m_new)
    m_ref[idx] = m_new
    l_ref[idx] = a * l_ref[idx] + jnp.sum(p, axis=-1, keepdims=True)
    return a, p


def _dot_nt(a, b):
    return lax.dot_general(a, b, (((1,), (1,)), ((), ())), preferred_element_type=F32)


def _fox_decode_body(q_ref, fq_ref, kc_ref, vc_ref, fkc_ref, kn_ref, vn_ref, fkn_ref, o_ref, m_ref, l_ref, acc_ref):
    t = pl.program_id(1)
    n_q = q_ref.shape[1]
    cols = [slice(hd * FOX_DIM, (hd + 1) * FOX_DIM) for hd in range(FOX_HEADS)]

    @pl.when(t == 0)
    def _():
        m_ref[...] = jnp.full(m_ref.shape, NEG, F32)
        l_ref[...] = jnp.zeros(l_ref.shape, F32)
        acc_ref[...] = jnp.zeros(acc_ref.shape, F32)

    def attend(scores, fk_rows, values, ok):
        probs = []
        for hd in range(FOX_HEADS):
            s = scores[hd] + (fq_ref[0, :, hd:hd + 1] - fk_rows[hd:hd + 1, :])
            if ok is not None:
                s = jnp.where(ok, s, NEG)
            probs.append(_online_softmax(s, m_ref, l_ref, hd))
        for hd in range(FOX_HEADS):
            a, p = probs[hd]
            acc_ref[hd] = a * acc_ref[hd] + values(hd, p.astype(BF16))

    attend([jnp.dot(q_ref[0, :, cols[hd]], kc_ref[0, 0, hd].astype(BF16), preferred_element_type=F32)
            for hd in range(FOX_HEADS)],
           fkc_ref[0], lambda hd, p: _dot_nt(p, vc_ref[0, 0, hd].astype(BF16)), None)

    @pl.when(t == pl.num_programs(1) - 1)
    def _():
        causal = (lax.broadcasted_iota(jnp.int32, (n_q, n_q), 1) <= lax.broadcasted_iota(jnp.int32, (n_q, n_q), 0))
        attend([_dot_nt(q_ref[0, :, sl], kn_ref[0, :, sl]) for sl in cols], fkn_ref[0],
               lambda hd, p: jnp.dot(p, vn_ref[0, :, cols[hd]], preferred_element_type=F32), causal)
        for hd in range(FOX_HEADS):
            o_ref[0, :, cols[hd]] = (acc_ref[hd] / l_ref[hd]).astype(o_ref.dtype)


def _fox_decode(q, k_new, v_new, cache_k, cache_v, layer, fcum, tk):
    bsz, n_q, width = q.shape
    n_past = cache_k.shape[2]
    rows_minor = lambda c: c.transpose(0, 1, 3, 4, 2)
    fk_t = fcum.transpose(0, 2, 1)
    row_spec = pl.BlockSpec((1, n_q, width), lambda b, t: (b, 0, 0))
    cache_spec = pl.BlockSpec((1, 1, FOX_HEADS, FOX_DIM, tk), lambda b, t: (layer, b, 0, 0, t))
    return pl.pallas_call(
        _fox_decode_body,
        grid=(bsz, n_past // tk),
        in_specs=[row_spec, pl.BlockSpec((1, n_q, FOX_HEADS), lambda b, t: (b, 0, 0)),
                  cache_spec, cache_spec, pl.BlockSpec((1, FOX_HEADS, tk), lambda b, t: (b, 0, t)),
                  row_spec, row_spec, pl.BlockSpec((1, FOX_HEADS, n_q), lambda b, t: (b, 0, 0))],
        out_specs=row_spec,
        out_shape=jax.ShapeDtypeStruct((bsz, n_q, width), BF16),
        scratch_shapes=[pltpu.VMEM((FOX_HEADS, n_q, 1), F32), pltpu.VMEM((FOX_HEADS, n_q, 1), F32),
                        pltpu.VMEM((FOX_HEADS, n_q, FOX_DIM), F32)],
        compiler_params=_cparams(2, VMEM_LIMIT),
        name="fox_decode",
    )(q, fcum[:, n_past:n_past + n_q], rows_minor(cache_k), rows_minor(cache_v), fk_t[:, :, :n_past],
      k_new, v_new, fk_t[:, :, n_past:n_past + n_q])


def _mla_decode_body(q_ref, wkn_ref, wv_ref, cc_ref, pc_ref, cn_ref, pn_ref, o_ref, qa_ref, qr_ref,
                     m_ref, l_ref, acc_ref):
    t = pl.program_id(1)
    n_q = q_ref.shape[1]

    @pl.when(t == 0)
    def _():
        for hd in range(MLA_HEADS):
            c0 = hd * MLA_QK_PAD
            rows = slice(hd * n_q, (hd + 1) * n_q)
            qa_ref[rows, :] = _dot_nt(q_ref[0, :, c0:c0 + MLA_NOPE],
                                      wkn_ref[:, c0:c0 + MLA_NOPE]).astype(BF16)
            qr_ref[rows, :] = q_ref[0, :, c0 + MLA_NOPE:c0 + MLA_NOPE + MLA_ROPE]
        m_ref[...] = jnp.full(m_ref.shape, NEG, F32)
        l_ref[...] = jnp.zeros(l_ref.shape, F32)
        acc_ref[...] = jnp.zeros(acc_ref.shape, F32)

    def attend(ckv, rope_scores):
        a, p = _online_softmax(_dot_nt(qa_ref[...], ckv) + rope_scores, m_ref, l_ref, 0)
        acc_ref[0] = a * acc_ref[0] + jnp.dot(p.astype(BF16), ckv, preferred_element_type=F32)

    attend(cc_ref[0, 0].astype(BF16),
           jnp.dot(qr_ref[...], pc_ref[0, 0].astype(BF16), preferred_element_type=F32))

    @pl.when(t == pl.num_programs(1) - 1)
    def _():
        attend(cn_ref[0].astype(BF16), _dot_nt(qr_ref[...], pn_ref[0].astype(BF16)))
        lat = (acc_ref[0] / l_ref[0]).astype(BF16)
        for hd in range(MLA_HEADS):
            cols = slice(hd * MLA_V, (hd + 1) * MLA_V)
            o_ref[0, :, cols] = jnp.dot(lat[hd * n_q:(hd + 1) * n_q, :], wv_ref[:, cols],
                                        preferred_element_type=F32).astype(o_ref.dtype)


def _mla_decode(q, ckv_new, kpe_new, cache_ckv, cache_kpe, layer, p, tk):
    bsz, n_q, _ = q.shape
    n_past = cache_ckv.shape[2]
    stacked = MLA_HEADS * n_q
    return pl.pallas_call(
        _mla_decode_body,
        grid=(bsz, n_past // tk),
        in_specs=[pl.BlockSpec((1, n_q, MLA_HEADS * MLA_QK_PAD), lambda b, t: (b, 0, 0)),
                  _const_spec(p["wkn"].shape), _const_spec(p["wv"].shape),
                  pl.BlockSpec((1, 1, tk, MLA_KV_RANK), lambda b, t: (layer, b, t, 0)),
                  pl.BlockSpec((1, 1, MLA_ROPE, tk), lambda b, t: (layer, b, 0, t)),
                  pl.BlockSpec((1, n_q, MLA_KV_RANK), lambda b, t: (b, 0, 0)),
                  pl.BlockSpec((1, n_q, MLA_ROPE), lambda b, t: (b, 0, 0))],
        out_specs=pl.BlockSpec((1, n_q, MLA_HEADS * MLA_V), lambda b, t: (b, 0, 0)),
        out_shape=jax.ShapeDtypeStruct((bsz, n_q, MLA_HEADS * MLA_V), BF16),
        scratch_shapes=[pltpu.VMEM((stacked, MLA_KV_RANK), BF16), pltpu.VMEM((stacked, MLA_ROPE), BF16),
                        pltpu.VMEM((1, stacked, 1), F32), pltpu.VMEM((1, stacked, 1), F32),
                        pltpu.VMEM((1, stacked, MLA_KV_RANK), F32)],
        compiler_params=_cparams(2, VMEM_LIMIT),
        name="mla_decode",
    )(q, p["wkn"], p["wv"], cache_ckv, cache_kpe.transpose(0, 1, 3, 2), ckv_new, kpe_new)


def _gate_body(x_ref, b_ref, lf_ref, fc_ref, *, new_start, first_row):
    n_rows = x_ref.shape[1]
    tri = (lax.broadcasted_iota(jnp.int32, (TILE, TILE), 0)
           >= lax.broadcasted_iota(jnp.int32, (TILE, TILE), 1)).astype(F32)
    carry = jnp.zeros((1, x_ref.shape[2]), F32)
    for i in range(n_rows // TILE):
        sl = slice(i * TILE, (i + 1) * TILE)
        x = x_ref[0, sl, :]
        rows = i * TILE + lax.broadcasted_iota(jnp.int32, (TILE, 1), 0)
        z = x + b_ref[...]
        lf = jnp.where(rows >= new_start, jnp.minimum(z, 0.0) - jnp.log1p(jnp.exp(-jnp.abs(z))), x)
        if first_row:
            lf = jnp.where(rows >= first_row, lf, 0.0)
        lf_ref[0, sl, :] = lf
        cs = jnp.dot(tri, lf, preferred_element_type=F32, precision=lax.Precision.HIGHEST) + carry
        fc_ref[0, sl, :] = cs
        carry = cs[TILE - 1:TILE, :]


def _gate(x, b_f, new_start, first_row):
    bsz, n_rows, heads = x.shape
    spec = pl.BlockSpec((1, n_rows, heads), lambda b: (b, 0, 0))
    return pl.pallas_call(
        functools.partial(_gate_body, new_start=new_start, first_row=first_row),
        grid=(bsz,),
        in_specs=[spec, _const_spec((1, heads))],
        out_specs=[spec, spec],
        out_shape=[jax.ShapeDtypeStruct(x.shape, F32)] * 2,
        compiler_params=_cparams(1),
        name="fox_gate",
    )(x, b_f)


RET_LOG_GAMMA = tuple(math.log(1.0 - 2.0 ** (-5.0 - h)) for h in range(RET_HEADS))


def _ret_body(rq_ref, rk_ref, rv_ref, rg_ref, s0_ref, dec_ref, cq_ref, s1q_ref, s2q_ref, ck_ref, s1k_ref, s2k_ref,
              o_ref, sl_ref, st_ref, *, n_tail):
    c = pl.program_id(1)
    ct = rq_ref.shape[1]

    @pl.when(c == 0)
    def _():
        st_ref[...] = s0_ref[0]

    q = _rope_lanes(rq_ref[0], cq_ref[...], s1q_ref[...], s2q_ref[...], RET_DK // 2)
    k = _rope_lanes(rk_ref[0], ck_ref[...], s1k_ref[...], s2k_ref[...], RET_DK // 2)
    v = rv_ref[0].astype(BF16)
    g = rg_ref[0]
    j = lax.broadcasted_iota(jnp.int32, (ct, 1), 0).astype(F32)
    for hd in range(RET_HEADS):
        lg = RET_LOG_GAMMA[hd]
        q_h = q[:, hd * RET_DK:(hd + 1) * RET_DK]
        k_h = k[:, hd * RET_DK:(hd + 1) * RET_DK]
        v_h = v[:, hd * RET_DV:(hd + 1) * RET_DV]
        scores = lax.dot_general(q_h.astype(BF16), k_h.astype(BF16), (((1,), (1,)), ((), ())),
                                 preferred_element_type=F32) * dec_ref[hd]
        s_h = st_ref[hd]
        out = jnp.dot(scores.astype(BF16), v_h, preferred_element_type=F32)
        out = out + jnp.dot((q_h * jnp.exp(lg * (j + 1.0))).astype(BF16), s_h.astype(BF16),
                            preferred_element_type=F32)
        k_dec = (k_h * jnp.exp(lg * (ct - 1.0 - j))).astype(BF16)
        st_ref[hd] = math.exp(lg * ct) * s_h + lax.dot_general(
            k_dec, v_h, (((0,), (0,)), ((), ())), preferred_element_type=F32)
        mu = jnp.mean(out, axis=-1, keepdims=True)
        oc = out - mu
        var = jnp.mean(oc * oc, axis=-1, keepdims=True)
        g_h = g[:, hd * RET_DV:(hd + 1) * RET_DV]
        o_ref[0, :, hd * RET_DV:(hd + 1) * RET_DV] = (
            g_h * jax.nn.sigmoid(g_h) * (oc * lax.rsqrt(var + EPS))).astype(o_ref.dtype)

    @pl.when(c == pl.num_programs(1) - 1)
    def _():
        for hd in range(RET_HEADS):
            sl_ref[0, hd] = st_ref[hd] * math.exp(-RET_LOG_GAMMA[hd] * n_tail)


def _retention(h, s0, tabs, ct, n_tail):
    bsz, seq, _ = h.shape
    tab_spec = pl.BlockSpec((ct, RET_QK), lambda b, c: (c, 0))
    st_spec = pl.BlockSpec((1, RET_HEADS, RET_DK, RET_DV), lambda b, c: (b, 0, 0, 0))
    rq_blk = 0
    rv_blk = (2 * RET_QK) // RET_VW
    diff = np.arange(ct, dtype=np.float32)[:, None] - np.arange(ct, dtype=np.float32)[None, :]
    decay = np.where(diff >= 0.0, np.exp(np.asarray(RET_LOG_GAMMA, np.float32)[:, None, None] * np.maximum(diff, 0.0)),
                     np.float32(0.0)).astype(np.float32)
    return pl.pallas_call(
        functools.partial(_ret_body, n_tail=n_tail),
        grid=(bsz, seq // ct),
        in_specs=[
            pl.BlockSpec((1, ct, RET_QK), lambda b, c: (b, c, rq_blk)),
            pl.BlockSpec((1, ct, RET_QK), lambda b, c: (b, c, rq_blk + 1)),
            pl.BlockSpec((1, ct, RET_VW), lambda b, c: (b, c, rv_blk)),
            pl.BlockSpec((1, ct, RET_VW), lambda b, c: (b, c, rv_blk + 1)),
            st_spec, _const_spec((RET_HEADS, ct, ct)),
        ] + [tab_spec] * 6,
        out_specs=[pl.BlockSpec((1, ct, RET_VW), lambda b, c: (b, c, 0)), st_spec],
        out_shape=[jax.ShapeDtypeStruct((bsz, seq, RET_VW), BF16),
                   jax.ShapeDtypeStruct((bsz, RET_HEADS, RET_DK, RET_DV), F32)],
        scratch_shapes=[pltpu.VMEM((RET_HEADS, RET_DK, RET_DV), F32)],
        compiler_params=_cparams(2, VMEM_LIMIT),
        name="retention",
    )(h, h, h, h, s0, decay, *tabs)


def _rope_tables(pos, half, width, group, offset, scale=1.0, valid=None):
    f32 = np.float32
    inv = f32(ROPE_BASE) ** (-np.arange(half, dtype=f32) / f32(half))
    ang = pos.astype(f32)[:, None] * inv[None, :]
    cos, sin = np.cos(ang), np.sin(ang)
    n = pos.shape[0]
    one, zero = np.ones((n, 1), f32), np.zeros((n, 1), f32)

    def lanes(first, second, other):
        grp = np.concatenate([np.broadcast_to(other, (n, offset)), first, second,
                              np.broadcast_to(other, (n, group - offset - 2 * half))], axis=1)
        return np.tile(grp, (1, width // group))

    tabs = (lanes(cos, cos, one), lanes(-sin, 0.0 * sin, zero), lanes(0.0 * sin, sin, zero))
    if valid is not None:
        tabs = tuple(np.where(valid[:, None], t, f32(0.0)) for t in tabs)
    return tuple((t * f32(scale)).astype(f32) for t in tabs)


def _even_params(e, w):
    p = {}
    p["w_in"] = jnp.pad(w["even_w_in"][e], ((0, 0), (0, EVEN_IN_PAD - EVEN_IN))).astype(BF16)
    p["w_out"] = w["even_w_out"][e].astype(BF16)
    lam_re, lam_im = w["s5_a_re"][e].astype(F32), w["s5_a_im"][e].astype(F32)
    dt = jnp.exp(w["s5_log_dt"][e].astype(F32))[:, None]
    p["lam_dt_re"], p["lam_dt_im"] = lam_re * dt, lam_im * dt
    mag = jnp.exp(lam_re * dt)
    abar_re, abar_im = mag * jnp.cos(lam_im * dt), mag * jnp.sin(lam_im * dt)
    den = lam_re * lam_re + lam_im * lam_im
    f_re = ((abar_re - 1.0) * lam_re + abar_im * lam_im) / den
    f_im = (abar_im * lam_re - (abar_re - 1.0) * lam_im) / den
    b_re, b_im = w["s5_b_re"][e].astype(F32), w["s5_b_im"][e].astype(F32)
    bb_re = f_re[..., None] * b_re - f_im[..., None] * b_im
    bb_im = f_re[..., None] * b_im + f_im[..., None] * b_re
    eye = jnp.eye(S5_GROUPS, dtype=F32)

    def in_blocks(x):
        return jnp.einsum("gnc,gh->gchn", x, eye).reshape(S5_WIDTH, S5_HALF)

    def out_blocks(x):
        return jnp.einsum("gcn,gh->gnhc", x, eye).reshape(S5_HALF, S5_WIDTH)

    p["bbig"] = jnp.concatenate([in_blocks(bb_re), in_blocks(bb_im)], axis=1).astype(BF16)
    p["cbig"] = jnp.concatenate([out_blocks(w["s5_c_re"][e].astype(F32)),
                                 out_blocks(-w["s5_c_im"][e].astype(F32))], axis=0).astype(BF16)
    p["a_re"] = abar_re.reshape(1, S5_HALF)
    p["a_im"] = abar_im.reshape(1, S5_HALF)
    p["d"] = w["s5_d"][e].astype(F32).reshape(1, S5_WIDTH)
    p["w_glu"] = w["s5_w_glu"][e].astype(BF16)
    p["b_glu"] = w["s5_b_glu"][e].astype(F32).reshape(1, S5_WIDTH)
    p["q_norm"] = w["mla_q_norm"][e].astype(F32).reshape(1, MLA_Q_RANK)
    p["kv_norm"] = w["mla_kv_norm"][e].astype(F32).reshape(1, MLA_KV_RANK)
    wq = w["mla_w_uq"][e].reshape(MLA_Q_RANK, MLA_HEADS, MLA_NOPE + MLA_ROPE)
    wq = jnp.pad(wq, ((0, 0), (0, 0), (0, MLA_QK_PAD - MLA_NOPE - MLA_ROPE)))
    p["wq"] = wq.reshape(MLA_Q_RANK, MLA_HEADS * MLA_QK_PAD).astype(BF16)
    wkv = w["mla_w_ukv"][e].reshape(MLA_KV_RANK, MLA_HEADS, MLA_NOPE + MLA_V)
    wkn = jnp.pad(wkv[:, :, :MLA_NOPE], ((0, 0), (0, 0), (0, MLA_QK_PAD - MLA_NOPE)))
    p["wkn"] = wkn.reshape(MLA_KV_RANK, MLA_HEADS * MLA_QK_PAD).astype(BF16)
    place = jnp.pad(jnp.eye(MLA_ROPE, dtype=F32), ((0, 0), (MLA_NOPE, MLA_QK_PAD - MLA_NOPE - MLA_ROPE)))
    p["wkp"] = jnp.tile(place, (1, MLA_HEADS)).astype(BF16)
    p["wv"] = wkv[:, :, MLA_NOPE:].reshape(MLA_KV_RANK, MLA_HEADS * MLA_V).astype(BF16)
    return p


def _odd_params(o, w):
    w_in = w["odd_w_in"][o]
    c_logit = 3 * FOX_W
    cols = jnp.concatenate([
        w_in[:, :FOX_W] * (FOX_DIM ** -0.5),
        w_in[:, FOX_W:c_logit],
        w_in[:, c_logit + FOX_HEADS:],
        w_in[:, c_logit:c_logit + FOX_HEADS],
    ], axis=1)
    p = {"w_in": jnp.pad(cols, ((0, 0), (0, ODD_IN_PAD - cols.shape[1]))).astype(BF16)}
    p["w_out"] = w["odd_w_out"][o].astype(BF16)
    p["b_f"] = w["fox_b_f"][o].astype(F32).reshape(1, FOX_HEADS)
    return p


def _cache_tile(n_past):
    for tk in (2048, 1024, 512, 256, 128):
        if n_past % tk == 0:
            return tk
    raise ValueError("cache length must be a multiple of 128")


def _trunk(x, pos, n_real, out_rows, past, w, ffn, evens, odds):
    bsz, seq, _ = x.shape
    rows = bsz * seq
    prompt = past is None
    first_row = PAD if prompt else 0
    if prompt:
        tm = next(m * TILE for m in (4, 2, 1) if rows % (m * TILE) == 0 and out_rows[1] % (m * TILE) == 0)
        rt = TILE
    else:
        n_past = past["cache_fox_k"].shape[2]
        tk = _cache_tile(n_past)
        tm, rt = rows, seq
    idx = np.arange(seq)
    valid = (idx >= first_row) & (idx < n_real)
    st = {n: [] for n in ("mla_ckv", "mla_kpe", "s5_re", "s5_im", "fox_k", "fox_v", "fox_logf", "ret")}
    rq_tabs = _rope_tables(pos, RET_DK // 2, RET_QK, RET_DK, 0)
    rk_tabs = _rope_tables(pos, RET_DK // 2, RET_QK, RET_DK, 0, scale=RET_DK ** -0.5, valid=valid)
    x2 = x.reshape(rows, D_MODEL)

    def ln(l, i):
        return w["ln_g"][l, i].reshape(1, D_MODEL), w["ln_b"][l, i].reshape(1, D_MODEL)

    for l in range(DEPTH):
        x2 = _ffn_ln(x2, ffn, (l, 0), *ln(l, 0), tm)
        if l % 2 == 0:
            e = l // 2
            p = evens[e]
            if prompt:
                u, q, ckv, kpe, k_att, v_att = _even_rows(x2.reshape(bsz, seq, D_MODEL), p, pos)
                h0r = h0i = jnp.zeros((bsz, 1, S5_HALF), F32)
                mla_out = _flash(q, k_att, v_att, None, None, heads=MLA_HEADS, dq=MLA_QK_PAD, dv=MLA_V,
                                 causal=False)
            else:
                u = _proj(x2, p["w_in"], tm).reshape(bsz, seq, EVEN_IN_PAD)
                h0r = past["state_s5_re"][e].astype(F32).reshape(bsz, 1, S5_HALF)
                h0i = past["state_s5_im"][e].astype(F32).reshape(bsz, 1, S5_HALF)
                q, ckv, kpe = _mla_rows(u, p, pos, rt)
                mla_out = _mla_decode(q, ckv, kpe, past["cache_mla_ckv"], past["cache_mla_kpe"], e, p, tk)
            s5_out, hlr, hli = _s5(u, h0r, h0i, p, rt, first_row, n_real)
            mix = (s5_out.reshape(rows, S5_WIDTH), mla_out.reshape(rows, MLA_HEADS * MLA_V), p["w_out"])
            st["mla_ckv"].append(ckv)
            st["mla_kpe"].append(kpe)
            st["s5_re"].append(hlr.reshape(bsz, S5_GROUPS, S5_STATE))
            st["s5_im"].append(hli.reshape(bsz, S5_GROUPS, S5_STATE))
        else:
            o = l // 2
            p = odds[o]
            h, k32, v32, fq, fk16, fv = _proj_odd(x2, p["w_in"], bsz, TILE if prompt else tm, attn_layout=prompt)
            h = h.reshape(bsz, seq, ODD_H)
            fk16 = fk16.reshape(bsz, seq, FOX_W)
            f_logit = h[:, :, ODD_LOGIT_COL:ODD_LOGIT_COL + FOX_HEADS]
            if prompt:
                q_off = 0
                logf, fcum = _gate(f_logit, p["b_f"], 0, PAD)
                fcum = fcum * LOG2E
                fox_out = _flash(fq, fk16, fv, fcum.transpose(0, 2, 1), fcum, heads=FOX_HEADS, dq=FOX_DIM,
                                 dv=FOX_DIM, causal=True)
                s0 = jnp.zeros((bsz, RET_HEADS, RET_DK, RET_DV), F32)
            else:
                q_off = n_past
                gates = jnp.concatenate([past["cache_fox_logf"][o].astype(F32), f_logit], axis=1)
                gates = jnp.pad(gates, ((0, 0), (0, _round_up(n_past + seq, TILE) - n_past - seq), (0, 0)))
                logf, fcum = _gate(gates, p["b_f"], n_past, 0)
                fox_out = _fox_decode(fq.reshape(bsz, seq, FOX_W), fk16, fv.reshape(bsz, seq, FOX_W),
                                      past["cache_fox_k"], past["cache_fox_v"], o, fcum * LOG2E, tk)
                s0 = past["state_ret"][o].astype(F32)
            ret_out, s_last = _retention(h, s0, rq_tabs + rk_tabs, rt, seq - n_real)
            mix = (fox_out.reshape(rows, FOX_W), ret_out.reshape(rows, RET_VW), p["w_out"])
            if prompt:
                k32, v32 = k32.transpose(0, 3, 1, 2), v32.transpose(0, 3, 1, 2)
            st["fox_k"].append(k32.reshape(bsz, seq, FOX_HEADS, FOX_DIM))
            st["fox_v"].append(v32.reshape(bsz, seq, FOX_HEADS, FOX_DIM))
            st["fox_logf"].append(logf[:, q_off:q_off + seq])
            st["ret"].append(s_last)
        keep = (seq,) + out_rows if l == DEPTH - 1 and out_rows != (0, seq) else None
        x2 = _mix_ffn_ln(x2, *mix, ln(l, 1), ffn, (l, 1), ln(l, 2), tm, keep)
    return x2.reshape(bsz, out_rows[1], D_MODEL), {n: jnp.stack(a) for n, a in st.items()}


def kernel(x_prompt, x_sample, cache_mla_ckv, cache_mla_kpe, cache_fox_k, cache_fox_v, cache_fox_logf,
           state_s5_re, state_s5_im, state_ret, meta_tokens, ln_g, ln_b, ffn_w_gate, ffn_w_up, ffn_w_down,
           even_w_in, even_w_out, s5_a_re, s5_a_im, s5_b_re, s5_b_im, s5_c_re, s5_c_im, s5_d, s5_log_dt,
           s5_w_glu, s5_b_glu, mla_q_norm, mla_kv_norm, mla_w_uq, mla_w_ukv, odd_w_in, odd_w_out, fox_b_f):
    w = dict(ln_g=ln_g.astype(F32), ln_b=ln_b.astype(F32), even_w_in=even_w_in, even_w_out=even_w_out,
             s5_a_re=s5_a_re, s5_a_im=s5_a_im, s5_b_re=s5_b_re, s5_b_im=s5_b_im, s5_c_re=s5_c_re,
             s5_c_im=s5_c_im, s5_d=s5_d, s5_log_dt=s5_log_dt, s5_w_glu=s5_w_glu, s5_b_glu=s5_b_glu,
             mla_q_norm=mla_q_norm, mla_kv_norm=mla_kv_norm, mla_w_uq=mla_w_uq, mla_w_ukv=mla_w_ukv,
             odd_w_in=odd_w_in, odd_w_out=odd_w_out, fox_b_f=fox_b_f)
    past = dict(cache_mla_ckv=cache_mla_ckv, cache_mla_kpe=cache_mla_kpe, cache_fox_k=cache_fox_k,
                cache_fox_v=cache_fox_v, cache_fox_logf=cache_fox_logf, state_s5_re=state_s5_re,
                state_s5_im=state_s5_im, state_ret=state_ret)
    ffn = (ffn_w_gate.astype(BF16), ffn_w_up.astype(BF16), ffn_w_down.astype(BF16))
    evens = [_even_params(e, w) for e in range((DEPTH + 1) // 2)]
    odds = [_odd_params(o, w) for o in range(DEPTH // 2)]

    bsz, seq, _ = x_prompt.shape
    n_real = PAD + N_META + seq
    n_rows = _round_up(n_real, TILE)
    meta = jnp.broadcast_to(meta_tokens[None].astype(x_prompt.dtype), (bsz, N_META, D_MODEL))
    xp = jnp.concatenate([jnp.zeros((bsz, PAD, D_MODEL), x_prompt.dtype), meta, x_prompt,
                          jnp.zeros((bsz, n_rows - n_real, D_MODEL), x_prompt.dtype)], axis=1)
    pos_p = np.maximum(np.arange(n_rows) - PAD, 0)
    y_p, st_p = _trunk(xp, pos_p, n_real, (PAD + N_META, seq), None, w, ffn, evens, odds)
    d_seq = x_sample.shape[1]
    pos_s = N_META + cache_fox_k.shape[2] + np.arange(d_seq)
    y_s, st_s = _trunk(x_sample, pos_s, d_seq, (0, d_seq), past, w, ffn, evens, odds)

    def real(a):
        return a[:, :, PAD:n_real]

    return (y_p, y_s,
            real(st_p["mla_ckv"]), real(st_p["mla_kpe"]), real(st_p["fox_k"]), real(st_p["fox_v"]),
            real(st_p["fox_logf"]), st_p["s5_re"], st_p["s5_im"], st_p["ret"],
            st_s["mla_ckv"], st_s["mla_kpe"], st_s["fox_k"], st_s["fox_v"], st_s["fox_logf"],
            st_s["s5_re"], st_s["s5_im"], st_s["ret"])
```

```python
import functools
import math

import jax
import jax.numpy as jnp
import numpy as np
from jax import lax
from jax.experimental import pallas as pl
from jax.experimental.pallas import tpu as pltpu

F32 = jnp.float32
BF16 = jnp.bfloat16

D_MODEL = 1024
DEPTH = 4
CHUNK = 64
CHUNK_SHIFT = 6
N_META = 16
S5_WIDTH = 512
S5_CH = 16
S5_GROUPS = S5_WIDTH // S5_CH
S5_STATE = 64
S5_HALF = S5_GROUPS * S5_STATE
MLA_HEADS = 8
MLA_Q_RANK = 256
MLA_KV_RANK = 128
MLA_NOPE = 64
MLA_ROPE = 32
MLA_V = 64
MLA_QK_PAD = 128
FOX_HEADS = 8
FOX_DIM = 64
FOX_W = FOX_HEADS * FOX_DIM
RET_HEADS = 4
RET_DK = 64
RET_DV = 128
RET_QK = RET_HEADS * RET_DK
RET_VW = RET_HEADS * RET_DV
D_FF = 2816
ROPE_BASE = 10000.0
ALPHA = (2.0 * DEPTH) ** 0.25
EPS = 1e-5
NEG = -1e30
LOG2E = math.log2(math.e)
EVEN_IN = S5_WIDTH + MLA_Q_RANK + MLA_KV_RANK + MLA_ROPE
EVEN_IN_PAD = 1024
ODD_IN_PAD = 3200
ODD_H = ODD_IN_PAD - 3 * FOX_W
ODD_LOGIT_COL = 2 * RET_QK + 2 * RET_VW

PAD = CHUNK - N_META
LANE = 128
SUBLANE = 8
TILE = 256
VMEM_LIMIT = 56 * 1024 * 1024


def _cparams(n_grid, vmem=None):
    return pltpu.CompilerParams(dimension_semantics=("arbitrary",) * n_grid, vmem_limit_bytes=vmem)


def _const_spec(shape):
    nd = len(shape)
    return pl.BlockSpec(shape, lambda *_: (0,) * nd)


def _round_up(n, m):
    return -(-n // m) * m


def _layer_norm_rows(z, g, b):
    mu = jnp.mean(z, axis=-1, keepdims=True)
    zc = z - mu
    var = jnp.mean(zc * zc, axis=-1, keepdims=True)
    return zc * lax.rsqrt(var + EPS) * g + b


def _ffn_rows(x, wg_ref, wu_ref, wd_ref, g_ref, b_ref, hid_ref):
    xb = x.astype(BF16)
    for c in range(D_FF // TILE):
        sl = slice(c * TILE, (c + 1) * TILE)
        hg = jnp.dot(xb, wg_ref[:, sl], preferred_element_type=F32)
        hu = jnp.dot(xb, wu_ref[:, sl], preferred_element_type=F32)
        hid_ref[:, sl] = (hg * jax.nn.sigmoid(hg) * hu).astype(BF16)
    y = jnp.dot(hid_ref[...], wd_ref[...], preferred_element_type=F32)
    return _layer_norm_rows(ALPHA * x + 0.5 * y, g_ref[...], b_ref[...])


def _ffn_body(x_ref, wg_ref, wu_ref, wd_ref, g_ref, b_ref, o_ref, hid_ref):
    o_ref[...] = _ffn_rows(x_ref[...], wg_ref, wu_ref, wd_ref, g_ref, b_ref, hid_ref)


def _ffn_weight_specs(which):
    once = dict(pipeline_mode=pl.Buffered(1))
    pick = lambda i: which + (0, 0)
    return [pl.BlockSpec((None, None, D_MODEL, D_FF), pick, **once),
            pl.BlockSpec((None, None, D_MODEL, D_FF), pick, **once),
            pl.BlockSpec((None, None, D_FF, D_MODEL), pick, **once),
            _const_spec((1, D_MODEL)), _const_spec((1, D_MODEL))]


def _ffn_ln(x, ffn, which, g, b, tm):
    rows = x.shape[0]
    return pl.pallas_call(
        _ffn_body,
        grid=(rows // tm,),
        in_specs=[pl.BlockSpec((tm, D_MODEL), lambda i: (i, 0))] + _ffn_weight_specs(which),
        out_specs=pl.BlockSpec((tm, D_MODEL), lambda i: (i, 0)),
        out_shape=jax.ShapeDtypeStruct((rows, D_MODEL), F32),
        scratch_shapes=[pltpu.VMEM((tm, D_FF), BF16)],
        compiler_params=_cparams(1, VMEM_LIMIT),
        name="ffn_ln",
    )(x, *ffn, g, b)


def _mix_ffn_body(x_ref, a1_ref, a2_ref, wo_ref, g1_ref, b1_ref, wg_ref, wu_ref, wd_ref, g2_ref, b2_ref,
                  o_ref, hid_ref):
    k1 = a1_ref.shape[1]
    y = jnp.dot(a1_ref[...], wo_ref[:k1, :], preferred_element_type=F32)
    y = y + jnp.dot(a2_ref[...], wo_ref[k1:, :], preferred_element_type=F32)
    x1 = _layer_norm_rows(ALPHA * x_ref[...] + y, g1_ref[...], b1_ref[...])
    o_ref[...] = _ffn_rows(x1, wg_ref, wu_ref, wd_ref, g2_ref, b2_ref, hid_ref)


def _mix_ffn_ln(x, a1, a2, w_out, ln1, ffn, which, ln2, tm, keep=None):
    rows = x.shape[0]
    if keep is None:
        n_steps = rows // tm
        row_spec = lambda n: pl.BlockSpec((tm, n), lambda i: (i, 0))
    else:
        seq, start, count = keep
        assert count % tm == 0
        per = count // tm
        n_steps = (rows // seq) * per
        align = math.gcd(seq, start, tm)
        row_spec = lambda n: pl.BlockSpec(
            (pl.Element(tm), pl.Element(n)),
            lambda i: (pl.multiple_of((i // per) * seq + start + (i % per) * tm, align), 0))
    return pl.pallas_call(
        _mix_ffn_body,
        grid=(n_steps,),
        in_specs=[row_spec(D_MODEL), row_spec(a1.shape[1]), row_spec(a2.shape[1]),
                  pl.BlockSpec(w_out.shape, lambda i: (0, 0), pipeline_mode=pl.Buffered(1)),
                  _const_spec((1, D_MODEL)), _const_spec((1, D_MODEL))] + _ffn_weight_specs(which),
        out_specs=pl.BlockSpec((tm, D_MODEL), lambda i: (i, 0)),
        out_shape=jax.ShapeDtypeStruct((n_steps * tm, D_MODEL), F32),
        scratch_shapes=[pltpu.VMEM((tm, D_FF), BF16)],
        compiler_params=_cparams(1, VMEM_LIMIT),
        name="mix_ffn_ln",
    )(x, a1, a2, w_out, *ln1, *ffn, *ln2)


def _proj_body(x_ref, w_ref, o_ref):
    o_ref[...] = jnp.dot(x_ref[...].astype(BF16), w_ref[...], preferred_element_type=F32)


def _proj(x, w, tm):
    rows, n = x.shape[0], w.shape[1]
    return pl.pallas_call(
        _proj_body,
        grid=(rows // tm,),
        in_specs=[pl.BlockSpec((tm, D_MODEL), lambda i: (i, 0)), _const_spec(w.shape)],
        out_specs=pl.BlockSpec((tm, n), lambda i: (i, 0)),
        out_shape=jax.ShapeDtypeStruct((rows, n), F32),
        compiler_params=_cparams(1, VMEM_LIMIT),
        name="in_proj_even",
    )(x, w)


def _proj_odd_body(x_ref, w_ref, h_ref, k32_ref, v32_ref, q_ref, k_ref, v_ref, *, attn_layout):
    y = jnp.dot(x_ref[...].astype(BF16), w_ref[...], preferred_element_type=F32)
    h_ref[...] = y[:, 3 * FOX_W:]
    q = y[:, :FOX_W] * LOG2E
    k, v = y[:, FOX_W:2 * FOX_W], y[:, 2 * FOX_W:3 * FOX_W]
    k_ref[...] = k.astype(BF16)
    if attn_layout:
        q_ref[0] = q.T.astype(BF16)
        k_t, v_t = k.T, v.T
        for hd in range(FOX_HEADS):
            rows = slice(hd * FOX_DIM, (hd + 1) * FOX_DIM)
            k32_ref[0, hd] = k_t[rows, :]
            v32_ref[0, hd] = v_t[rows, :]
            v_ref[0, hd, 0] = v_t[rows, :].astype(BF16)
    else:
        k32_ref[...] = k
        v32_ref[...] = v
        q_ref[...] = q.astype(BF16)
        v_ref[...] = v.astype(BF16)


def _proj_odd(x, w, bsz, tm, attn_layout):
    rows = x.shape[0]
    seq = rows // bsz
    row_spec = lambda n: pl.BlockSpec((tm, n), lambda i: (i, 0))
    if attn_layout:
        assert tm == TILE and seq % TILE == 0
        n_t = seq // TILE
        kv32_shape = jax.ShapeDtypeStruct((bsz, FOX_HEADS, FOX_DIM, seq), F32)
        kv32_spec = pl.BlockSpec((1, FOX_HEADS, FOX_DIM, TILE), lambda i: (i // n_t, 0, 0, i % n_t))
        q_shape = jax.ShapeDtypeStruct((bsz, FOX_W, seq), BF16)
        q_spec = pl.BlockSpec((1, FOX_W, TILE), lambda i: (i // n_t, 0, i % n_t))
        v_shape = jax.ShapeDtypeStruct((bsz, FOX_HEADS, n_t, FOX_DIM, TILE), BF16)
        v_spec = pl.BlockSpec((1, FOX_HEADS, 1, FOX_DIM, TILE), lambda i: (i // n_t, 0, i % n_t, 0, 0))
    else:
        kv32_shape, kv32_spec = jax.ShapeDtypeStruct((rows, FOX_W), F32), row_spec(FOX_W)
        q_shape = v_shape = jax.ShapeDtypeStruct((rows, FOX_W), BF16)
        q_spec = v_spec = row_spec(FOX_W)
    return pl.pallas_call(
        functools.partial(_proj_odd_body, attn_layout=attn_layout),
        grid=(rows // tm,),
        in_specs=[row_spec(D_MODEL), _const_spec(w.shape)],
        out_specs=[row_spec(ODD_H), kv32_spec, kv32_spec, q_spec, row_spec(FOX_W), v_spec],
        out_shape=[jax.ShapeDtypeStruct((rows, ODD_H), F32), kv32_shape, kv32_shape, q_shape,
                   jax.ShapeDtypeStruct((rows, FOX_W), BF16), v_shape],
        compiler_params=_cparams(1, VMEM_LIMIT),
        name="in_proj_odd",
    )(x, w)


def _s5_body(u_ref, h0r_ref, h0i_ref, ar_ref, ai_ref, asr_ref, asi_ref, bbig_ref, cbig_ref, d_ref, wglu_ref,
             bglu_ref, o_ref, hlr_ref, hli_ref, hs_ref, st_ref, *, first_row, last_seg):
    t = pl.program_id(1)
    rt = u_ref.shape[1]
    seg = rt // SUBLANE

    @pl.when(t == 0)
    def _():
        st_ref[0:1, :] = h0r_ref[0]
        st_ref[1:2, :] = h0i_ref[0]

    u = u_ref[0]
    if first_row:
        rows = t * rt + lax.broadcasted_iota(jnp.int32, (rt, 1), 0)
        u = jnp.where(rows >= first_row, u, 0.0)
    i0 = lax.broadcasted_iota(jnp.int32, (rt, rt), 0)
    i1 = lax.broadcasted_iota(jnp.int32, (rt, rt), 1)
    regroup = (i1 == (i0 & (SUBLANE - 1)) * seg + (i0 >> 3)).astype(BF16)
    restore = (i0 == (i1 & (SUBLANE - 1)) * seg + (i1 >> 3)).astype(BF16)
    ub = jnp.dot(regroup, u.astype(BF16), preferred_element_type=F32).astype(BF16)

    half_w, half_s = S5_WIDTH // 2, S5_HALF // 2
    for kb in range(2):
        for part in range(2):
            c0 = part * S5_HALF + kb * half_s
            hs_ref[:, c0:c0 + half_s] = jnp.dot(
                ub[:, kb * half_w:(kb + 1) * half_w], bbig_ref[kb * half_w:(kb + 1) * half_w, c0:c0 + half_s],
                preferred_element_type=F32)

    def scan(lanes, start, store):
        re_l = lanes
        im_l = slice(S5_HALF + lanes.start, S5_HALF + lanes.stop)
        ar = jnp.broadcast_to(ar_ref[:, lanes], (SUBLANE, lanes.stop - lanes.start))
        ai = jnp.broadcast_to(ai_ref[:, lanes], (SUBLANE, lanes.stop - lanes.start))

        def step(k, carry):
            hr, hi = carry
            base = k * SUBLANE
            nr = ar * hr - ai * hi + hs_ref[pl.ds(base, SUBLANE), re_l]
            ni = ar * hi + ai * hr + hs_ref[pl.ds(base, SUBLANE), im_l]
            if store:
                hs_ref[pl.ds(base, SUBLANE), re_l] = nr
                hs_ref[pl.ds(base, SUBLANE), im_l] = ni
            return nr, ni

        carry = start
        for k in range(seg):
            carry = step(k, carry)
        return carry

    slabs = [slice(c * half_s, (c + 1) * half_s) for c in range(2)]
    zero = jnp.zeros((SUBLANE, half_s), F32)
    ends = [scan(lanes, (zero, zero), False) for lanes in slabs]
    er = jnp.concatenate([e[0] for e in ends], axis=1)
    ei = jnp.concatenate([e[1] for e in ends], axis=1)
    asr, asi = asr_ref[...], asi_ref[...]
    sr, si = st_ref[0:1, :], st_ref[1:2, :]
    start_r, start_i = [], []
    for j in range(SUBLANE):
        start_r.append(sr)
        start_i.append(si)
        sr, si = asr * sr - asi * si + er[j:j + 1], asr * si + asi * sr + ei[j:j + 1]
    st_ref[0:1, :] = sr
    st_ref[1:2, :] = si
    start_r8, start_i8 = jnp.concatenate(start_r, axis=0), jnp.concatenate(start_i, axis=0)
    for lanes in slabs:
        scan(lanes, (start_r8[:, lanes], start_i8[:, lanes]), True)

    hb = hs_ref[...].astype(BF16)
    ys = []
    for kb in range(2):
        re_rows = slice(kb * half_s, (kb + 1) * half_s)
        im_rows = slice(S5_HALF + kb * half_s, S5_HALF + (kb + 1) * half_s)
        cols = slice(kb * half_w, (kb + 1) * half_w)
        ys.append(jnp.dot(hb[:, re_rows], cbig_ref[re_rows, cols], preferred_element_type=F32)
                  + jnp.dot(hb[:, im_rows], cbig_ref[im_rows, cols], preferred_element_type=F32))
    y = jnp.concatenate(ys, axis=1)
    y_hi = y.astype(BF16)
    y_lo = (y - y_hi.astype(F32)).astype(BF16)
    y = (jnp.dot(restore, y_hi, preferred_element_type=F32) + jnp.dot(restore, y_lo, preferred_element_type=F32)
         + d_ref[...] * u)
    g = jax.nn.gelu(y)
    gate = jnp.dot(g.astype(BF16), wglu_ref[...], preferred_element_type=F32) + bglu_ref[...]
    o_ref[0] = (g * jax.nn.sigmoid(gate)).astype(o_ref.dtype)

    @pl.when(t == pl.num_programs(1) - 1)
    def _():
        hlr_ref[0] = sr if last_seg == SUBLANE else start_r[last_seg]
        hli_ref[0] = si if last_seg == SUBLANE else start_i[last_seg]


def _s5(h, h0r, h0i, p, rt, first_row, n_real):
    bsz, seq, _ = h.shape
    seg = rt // SUBLANE
    real_in_last = n_real - (seq - rt)
    assert 0 < real_in_last <= rt and real_in_last % seg == 0
    as_re = (jnp.exp(p["lam_dt_re"] * seg) * jnp.cos(p["lam_dt_im"] * seg)).reshape(1, S5_HALF)
    as_im = (jnp.exp(p["lam_dt_re"] * seg) * jnp.sin(p["lam_dt_im"] * seg)).reshape(1, S5_HALF)
    state_spec = pl.BlockSpec((1, 1, S5_HALF), lambda b, t: (b, 0, 0))
    return pl.pallas_call(
        functools.partial(_s5_body, first_row=first_row, last_seg=real_in_last // seg),
        grid=(bsz, seq // rt),
        in_specs=[
            pl.BlockSpec((1, rt, S5_WIDTH), lambda b, t: (b, t, 0)),
            state_spec, state_spec,
            _const_spec((1, S5_HALF)), _const_spec((1, S5_HALF)), _const_spec((1, S5_HALF)), _const_spec((1, S5_HALF)),
            _const_spec((S5_WIDTH, 2 * S5_HALF)), _const_spec((2 * S5_HALF, S5_WIDTH)),
            _const_spec((1, S5_WIDTH)), _const_spec((S5_WIDTH, S5_WIDTH)), _const_spec((1, S5_WIDTH)),
        ],
        out_specs=[pl.BlockSpec((1, rt, S5_WIDTH), lambda b, t: (b, t, 0)), state_spec, state_spec],
        out_shape=[jax.ShapeDtypeStruct((bsz, seq, S5_WIDTH), BF16),
                   jax.ShapeDtypeStruct((bsz, 1, S5_HALF), F32),
                   jax.ShapeDtypeStruct((bsz, 1, S5_HALF), F32)],
        scratch_shapes=[pltpu.VMEM((rt, 2 * S5_HALF), F32), pltpu.VMEM((2, S5_HALF), F32)],
        compiler_params=_cparams(2, VMEM_LIMIT),
        name="s5",
    )(h, h0r, h0i, p["a_re"], p["a_im"], as_re, as_im, p["bbig"], p["cbig"], p["d"], p["w_glu"], p["b_glu"])


def _rope_lanes(x, c, s1, s2, half):
    n = x.shape[-1]
    return x * c + pltpu.roll(x, n - half, 1) * s1 + pltpu.roll(x, half, 1) * s2


def _rms_rows(x, g):
    return x * lax.rsqrt(jnp.mean(x * x, axis=-1, keepdims=True) + EPS) * g


def _mla_latents(h, kn_ref, ck_ref, s1k_ref, s2k_ref, ckv_ref, kpe_ref):
    ckv = _rms_rows(h[:, MLA_Q_RANK:MLA_Q_RANK + MLA_KV_RANK], kn_ref[...])
    kpe = _rope_lanes(h[:, MLA_Q_RANK + MLA_KV_RANK:], ck_ref[...], s1k_ref[...], s2k_ref[...],
                      MLA_ROPE // 2)[:, :MLA_ROPE]
    ckv_ref[0] = ckv
    kpe_ref[0] = kpe
    return ckv, kpe


def _mla_rows_body(h_ref, qn_ref, kn_ref, wq_ref, cq_ref, s1q_ref, s2q_ref, ck_ref, s1k_ref, s2k_ref,
                   q_ref, ckv_ref, kpe_ref):
    h = h_ref[0]
    q_lat = _rms_rows(h[:, :MLA_Q_RANK], qn_ref[...]).astype(BF16)
    q = jnp.dot(q_lat, wq_ref[...], preferred_element_type=F32)
    cq, s1q, s2q = cq_ref[...], s1q_ref[...], s2q_ref[...]
    for hd in range(MLA_HEADS):
        sl = slice(hd * MLA_QK_PAD, (hd + 1) * MLA_QK_PAD)
        q_ref[0, :, sl] = _rope_lanes(q[:, sl], cq, s1q, s2q, MLA_ROPE // 2).astype(BF16)
    _mla_latents(h, kn_ref, ck_ref, s1k_ref, s2k_ref, ckv_ref, kpe_ref)


def _even_rows_body(x_ref, win_ref, qn_ref, kn_ref, wqt_ref, cos_ref, sin_ref, ck_ref, s1k_ref, s2k_ref,
                    wkn_ref, wkp_ref, wv_ref, u_ref, q_ref, ckv_ref, kpe_ref, k_ref, v_ref, *, scale):
    y = jnp.dot(x_ref[0].astype(BF16), win_ref[...], preferred_element_type=F32)
    u_ref[0] = y[:, :S5_WIDTH]
    h = y[:, S5_WIDTH:]
    q_lat = _rms_rows(h[:, :MLA_Q_RANK], qn_ref[...]).astype(BF16)
    q_t = _dot_nt(wqt_ref[...], q_lat)
    cos_t, sin_t = cos_ref[...], sin_ref[...]
    half = MLA_ROPE // 2
    for hd in range(MLA_HEADS):
        r0 = hd * MLA_QK_PAD
        x1 = q_t[r0 + MLA_NOPE:r0 + MLA_NOPE + half, :]
        x2 = q_t[r0 + MLA_NOPE + half:r0 + MLA_NOPE + 2 * half, :]
        q_ref[0, r0:r0 + MLA_QK_PAD, :] = jnp.concatenate(
            [q_t[r0:r0 + MLA_NOPE, :] * scale, x1 * cos_t - x2 * sin_t, x1 * sin_t + x2 * cos_t,
             q_t[r0 + MLA_NOPE + 2 * half:r0 + MLA_QK_PAD, :]], axis=0).astype(BF16)
    ckv, kpe = _mla_latents(h, kn_ref, ck_ref, s1k_ref, s2k_ref, ckv_ref, kpe_ref)
    ckv = ckv.astype(BF16)
    k = jnp.dot(ckv, wkn_ref[...], preferred_element_type=F32)
    k = k + jnp.dot(kpe.astype(BF16), wkp_ref[...], preferred_element_type=F32)
    k_ref[0] = k.astype(BF16)
    v_t = jnp.dot(ckv, wv_ref[...], preferred_element_type=F32).T
    for hd in range(MLA_HEADS):
        v_ref[0, hd, 0] = v_t[hd * MLA_V:(hd + 1) * MLA_V, :].astype(BF16)


def _mla_scale():
    return (MLA_NOPE + MLA_ROPE) ** -0.5 * LOG2E


def _even_rows(x, p, pos):
    bsz, seq, _ = x.shape
    n_t = seq // TILE
    n_q = MLA_HEADS * MLA_QK_PAD
    half = MLA_ROPE // 2
    scale = _mla_scale()
    ang = (ROPE_BASE ** (-np.arange(half, dtype=np.float32) / half))[:, None] * pos.astype(np.float32)[None, :]
    q_tabs = ((np.cos(ang) * scale).astype(np.float32), (np.sin(ang) * scale).astype(np.float32))
    row_spec = lambda n: pl.BlockSpec((1, TILE, n), lambda b, t: (b, t, 0))
    consts = [p["w_in"], p["q_norm"], p["kv_norm"], p["wq"].T]
    weights = [p["wkn"], p["wkp"], p["wv"]]
    return pl.pallas_call(
        functools.partial(_even_rows_body, scale=scale),
        grid=(bsz, n_t),
        in_specs=[row_spec(D_MODEL)] + [_const_spec(c.shape) for c in consts]
        + [pl.BlockSpec((half, TILE), lambda b, t: (0, t))] * 2
        + [pl.BlockSpec((TILE, LANE), lambda b, t: (t, 0))] * 3 + [_const_spec(c.shape) for c in weights],
        out_specs=[row_spec(S5_WIDTH), pl.BlockSpec((1, n_q, TILE), lambda b, t: (b, 0, t)),
                   row_spec(MLA_KV_RANK), row_spec(MLA_ROPE), row_spec(n_q),
                   pl.BlockSpec((1, MLA_HEADS, 1, MLA_V, TILE), lambda b, t: (b, 0, t, 0, 0))],
        out_shape=[jax.ShapeDtypeStruct((bsz, seq, S5_WIDTH), F32),
                   jax.ShapeDtypeStruct((bsz, n_q, seq), BF16),
                   jax.ShapeDtypeStruct((bsz, seq, MLA_KV_RANK), F32),
                   jax.ShapeDtypeStruct((bsz, seq, MLA_ROPE), F32),
                   jax.ShapeDtypeStruct((bsz, seq, n_q), BF16),
                   jax.ShapeDtypeStruct((bsz, MLA_HEADS, n_t, MLA_V, TILE), BF16)],
        compiler_params=_cparams(2, VMEM_LIMIT),
        name="even_rows",
    )(x, *consts, *q_tabs, *_rope_tables(pos, half, LANE, LANE, 0), *weights)


def _mla_rows(h, p, pos, tm):
    bsz, seq, _ = h.shape
    tab_spec = pl.BlockSpec((tm, LANE), lambda b, t: (t, 0))
    n_q = MLA_HEADS * MLA_QK_PAD
    half = MLA_ROPE // 2
    tabs = (_rope_tables(pos, half, MLA_QK_PAD, MLA_QK_PAD, MLA_NOPE, scale=_mla_scale())
            + _rope_tables(pos, half, LANE, LANE, 0))
    return pl.pallas_call(
        _mla_rows_body,
        grid=(bsz, seq // tm),
        in_specs=[
            pl.BlockSpec((1, tm, EVEN_IN_PAD - S5_WIDTH), lambda b, t: (b, t, 1)),
            _const_spec((1, MLA_Q_RANK)), _const_spec((1, MLA_KV_RANK)),
            _const_spec(p["wq"].shape),
        ] + [tab_spec] * 6,
        out_specs=[
            pl.BlockSpec((1, tm, n_q), lambda b, t: (b, t, 0)),
            pl.BlockSpec((1, tm, MLA_KV_RANK), lambda b, t: (b, t, 0)),
            pl.BlockSpec((1, tm, MLA_ROPE), lambda b, t: (b, t, 0)),
        ],
        out_shape=[jax.ShapeDtypeStruct((bsz, seq, n_q), BF16),
                   jax.ShapeDtypeStruct((bsz, seq, MLA_KV_RANK), F32),
                   jax.ShapeDtypeStruct((bsz, seq, MLA_ROPE), F32)],
        compiler_params=_cparams(2, VMEM_LIMIT),
        name="mla_rows",
    )(h, p["q_norm"], p["kv_norm"], p["wq"], *tabs)


def _flash_body(*refs, heads, dq, dv, causal):
    if causal:
        qt_ref, k_ref, vt_ref, fq_ref, fk_ref, o_ref = refs
    else:
        qt_ref, k_ref, vt_ref, o_ref = refs
    tq = qt_ref.shape[2]
    n_tiles, tk = vt_ref.shape[2], vt_ref.shape[4]
    r0 = pl.program_id(1) * tq
    j_last = jnp.minimum((r0 + tq - 1) // tk, n_tiles - 1)
    q_row = r0 + lax.broadcasted_iota(jnp.int32, (1, tq), 1)
    last = q_row if causal else ((q_row >> CHUNK_SHIFT) << CHUNK_SHIFT) + (CHUNK - 1)
    first_last = r0 if causal else ((r0 >> CHUNK_SHIFT) << CHUNK_SHIFT) + (CHUNK - 1)
    n_open = jnp.clip((first_last + 1) // tk, 1, j_last + 1)
    q_t = [qt_ref[0, hd * dq:(hd + 1) * dq, :] for hd in range(heads)]

    def tile(j, carry, masked, wide):
        width = 2 * tk if wide else tk
        start = pl.multiple_of(j * tk, tk)
        if masked:
            k_row = start + lax.broadcasted_iota(jnp.int32, (width, 1), 0)
            ok = (k_row <= last) & (k_row >= PAD)
        scores = [jnp.dot(k_ref[0, pl.ds(start, width), hd * dq:(hd + 1) * dq], q_t[hd],
                          preferred_element_type=F32) for hd in range(heads)]
        probs = []
        for hd in range(heads):
            m, l, _ = carry[hd]
            s = scores[hd]
            if causal:
                s = s + (fq_ref[0, hd:hd + 1, :] - fk_ref[0, pl.ds(start, width), hd:hd + 1])
            if masked:
                s = jnp.where(ok, s, NEG)
            m_new = jnp.maximum(m, jnp.max(s, axis=0, keepdims=True))
            a = jnp.exp2(m - m_new)
            p = jnp.exp2(s - m_new)
            probs.append((m_new, a * l + jnp.sum(p, axis=0, keepdims=True), a, p.astype(BF16)))
        new = []
        for hd in range(heads):
            m_new, l, a, p = probs[hd]
            acc = a * carry[hd][2] + jnp.dot(vt_ref[0, hd, j], p[:tk], preferred_element_type=F32)
            if wide:
                acc = acc + jnp.dot(vt_ref[0, hd, j + 1], p[tk:], preferred_element_type=F32)
            new.append((m_new, l, acc))
        return tuple(new)

    init = tuple((jnp.full((1, tq), NEG, F32), jnp.zeros((1, tq), F32), jnp.zeros((dv, tq), F32))
                 for _ in range(heads))
    carry = tile(0, init, True, False)
    n_wide = (n_open - 1) // 2
    carry = lax.fori_loop(0, n_wide, lambda i, c: tile(1 + 2 * i, c, False, True), carry)
    carry = lax.fori_loop(1 + 2 * n_wide, n_open, functools.partial(tile, masked=False, wide=False), carry)
    carry = lax.fori_loop(n_open, j_last + 1, functools.partial(tile, masked=True, wide=False), carry)
    outs = [carry[hd][2] / carry[hd][1] for hd in range(heads)]
    per = LANE // dv
    for g in range(heads // per):
        o_ref[0, :, g * LANE:(g + 1) * LANE] = jnp.concatenate(
            outs[g * per:(g + 1) * per], axis=0).T.astype(o_ref.dtype)


def _flash(q_t, k, v_t, fq_t, fk, *, heads, dq, dv, causal):
    bsz, _, n_q = q_t.shape
    n_keys = k.shape[1]
    in_specs = [
        pl.BlockSpec((1, heads * dq, TILE), lambda b, i: (b, 0, i)),
        pl.BlockSpec((1, n_keys, heads * dq), lambda b, i: (b, 0, 0)),
        pl.BlockSpec((1,) + v_t.shape[1:], lambda b, i: (b, 0, 0, 0, 0)),
    ]
    args = [q_t, k, v_t]
    if causal:
        in_specs += [pl.BlockSpec((1, heads, TILE), lambda b, i: (b, 0, i)),
                     pl.BlockSpec((1, n_keys, heads), lambda b, i: (b, 0, 0))]
        args += [fq_t, fk]
    return pl.pallas_call(
        functools.partial(_flash_body, heads=heads, dq=dq, dv=dv, causal=causal),
        grid=(bsz, n_q // TILE),
        in_specs=in_specs,
        out_specs=pl.BlockSpec((1, TILE, heads * dv), lambda b, i: (b, i, 0)),
        out_shape=jax.ShapeDtypeStruct((bsz, n_q, heads * dv), BF16),
        compiler_params=_cparams(2, VMEM_LIMIT),
        name="flash_causal" if causal else "flash_chunk",
    )(*args)


def _online_softmax(s, m_ref, l_ref, idx):
    m_old = m_ref[idx]
    m_new = jnp.maximum(m_old, jnp.max(s, axis=-1, keepdims=True))
    a = jnp.exp2(m_old - m_new)
    p = jnp.exp2(s - m_new)
    m_ref[idx] = m_new
    l_ref[idx] = a * l_ref[idx] + jnp.sum(p, axis=-1, keepdims=True)
    return a, p


def _dot_nt(a, b):
    return lax.dot_general(a, b, (((1,), (1,)), ((), ())), preferred_element_type=F32)


def _fox_decode_body(q_ref, fq_ref, kc_ref, vc_ref, fkc_ref, kn_ref, vn_ref, fkn_ref, o_ref, m_ref, l_ref, acc_ref):
    t = pl.program_id(1)
    n_q = q_ref.shape[1]
    cols = [slice(hd * FOX_DIM, (hd + 1) * FOX_DIM) for hd in range(FOX_HEADS)]

    @pl.when(t == 0)
    def _():
        m_ref[...] = jnp.full(m_ref.shape, NEG, F32)
        l_ref[...] = jnp.zeros(l_ref.shape, F32)
        acc_ref[...] = jnp.zeros(acc_ref.shape, F32)

    def attend(scores, fk_rows, values, ok):
        probs = []
        for hd in range(FOX_HEADS):
            s = scores[hd] + (fq_ref[0, :, hd:hd + 1] - fk_rows[hd:hd + 1, :])
            if ok is not None:
                s = jnp.where(ok, s, NEG)
            probs.append(_online_softmax(s, m_ref, l_ref, hd))
        for hd in range(FOX_HEADS):
            a, p = probs[hd]
            acc_ref[hd] = a * acc_ref[hd] + values(hd, p.astype(BF16))

    attend([jnp.dot(q_ref[0, :, cols[hd]], kc_ref[0, 0, hd].astype(BF16), preferred_element_type=F32)
            for hd in range(FOX_HEADS)],
           fkc_ref[0], lambda hd, p: _dot_nt(p, vc_ref[0, 0, hd].astype(BF16)), None)

    @pl.when(t == pl.num_programs(1) - 1)
    def _():
        causal = (lax.broadcasted_iota(jnp.int32, (n_q, n_q), 1) <= lax.broadcasted_iota(jnp.int32, (n_q, n_q), 0))
        attend([_dot_nt(q_ref[0, :, sl], kn_ref[0, :, sl]) for sl in cols], fkn_ref[0],
               lambda hd, p: jnp.dot(p, vn_ref[0, :, cols[hd]], preferred_element_type=F32), causal)
        for hd in range(FOX_HEADS):
            o_ref[0, :, cols[hd]] = (acc_ref[hd] / l_ref[hd]).astype(o_ref.dtype)


def _fox_decode(q, k_new, v_new, cache_k, cache_v, layer, fcum, tk):
    bsz, n_q, width = q.shape
    n_past = cache_k.shape[2]
    rows_minor = lambda c: c.transpose(0, 1, 3, 4, 2)
    fk_t = fcum.transpose(0, 2, 1)
    row_spec = pl.BlockSpec((1, n_q, width), lambda b, t: (b, 0, 0))
    cache_spec = pl.BlockSpec((1, 1, FOX_HEADS, FOX_DIM, tk), lambda b, t: (layer, b, 0, 0, t))
    return pl.pallas_call(
        _fox_decode_body,
        grid=(bsz, n_past // tk),
        in_specs=[row_spec, pl.BlockSpec((1, n_q, FOX_HEADS), lambda b, t: (b, 0, 0)),
                  cache_spec, cache_spec, pl.BlockSpec((1, FOX_HEADS, tk), lambda b, t: (b, 0, t)),
                  row_spec, row_spec, pl.BlockSpec((1, FOX_HEADS, n_q), lambda b, t: (b, 0, 0))],
        out_specs=row_spec,
        out_shape=jax.ShapeDtypeStruct((bsz, n_q, width), BF16),
        scratch_shapes=[pltpu.VMEM((FOX_HEADS, n_q, 1), F32), pltpu.VMEM((FOX_HEADS, n_q, 1), F32),
                        pltpu.VMEM((FOX_HEADS, n_q, FOX_DIM), F32)],
        compiler_params=_cparams(2, VMEM_LIMIT),
        name="fox_decode",
    )(q, fcum[:, n_past:n_past + n_q], rows_minor(cache_k), rows_minor(cache_v), fk_t[:, :, :n_past],
      k_new, v_new, fk_t[:, :, n_past:n_past + n_q])


def _mla_decode_body(q_ref, wkn_ref, wv_ref, cc_ref, pc_ref, cn_ref, pn_ref, o_ref, qa_ref, qr_ref,
                     m_ref, l_ref, acc_ref):
    t = pl.program_id(1)
    n_q = q_ref.shape[1]

    @pl.when(t == 0)
    def _():
        for hd in range(MLA_HEADS):
            c0 = hd * MLA_QK_PAD
            rows = slice(hd * n_q, (hd + 1) * n_q)
            qa_ref[rows, :] = _dot_nt(q_ref[0, :, c0:c0 + MLA_NOPE],
                                      wkn_ref[:, c0:c0 + MLA_NOPE]).astype(BF16)
            qr_ref[rows, :] = q_ref[0, :, c0 + MLA_NOPE:c0 + MLA_NOPE + MLA_ROPE]
        m_ref[...] = jnp.full(m_ref.shape, NEG, F32)
        l_ref[...] = jnp.zeros(l_ref.shape, F32)
        acc_ref[...] = jnp.zeros(acc_ref.shape, F32)

    def attend(ckv, rope_scores):
        a, p = _online_softmax(_dot_nt(qa_ref[...], ckv) + rope_scores, m_ref, l_ref, 0)
        acc_ref[0] = a * acc_ref[0] + jnp.dot(p.astype(BF16), ckv, preferred_element_type=F32)

    attend(cc_ref[0, 0].astype(BF16),
           jnp.dot(qr_ref[...], pc_ref[0, 0].astype(BF16), preferred_element_type=F32))

    @pl.when(t == pl.num_programs(1) - 1)
    def _():
        attend(cn_ref[0].astype(BF16), _dot_nt(qr_ref[...], pn_ref[0].astype(BF16)))
        lat = (acc_ref[0] / l_ref[0]).astype(BF16)
        for hd in range(MLA_HEADS):
            cols = slice(hd * MLA_V, (hd + 1) * MLA_V)
            o_ref[0, :, cols] = jnp.dot(lat[hd * n_q:(hd + 1) * n_q, :], wv_ref[:, cols],
                                        preferred_element_type=F32).astype(o_ref.dtype)


def _mla_decode(q, ckv_new, kpe_new, cache_ckv, cache_kpe, layer, p, tk):
    bsz, n_q, _ = q.shape
    n_past = cache_ckv.shape[2]
    stacked = MLA_HEADS * n_q
    return pl.pallas_call(
        _mla_decode_body,
        grid=(bsz, n_past // tk),
        in_specs=[pl.BlockSpec((1, n_q, MLA_HEADS * MLA_QK_PAD), lambda b, t: (b, 0, 0)),
                  _const_spec(p["wkn"].shape), _const_spec(p["wv"].shape),
                  pl.BlockSpec((1, 1, tk, MLA_KV_RANK), lambda b, t: (layer, b, t, 0)),
                  pl.BlockSpec((1, 1, MLA_ROPE, tk), lambda b, t: (layer, b, 0, t)),
                  pl.BlockSpec((1, n_q, MLA_KV_RANK), lambda b, t: (b, 0, 0)),
                  pl.BlockSpec((1, n_q, MLA_ROPE), lambda b, t: (b, 0, 0))],
        out_specs=pl.BlockSpec((1, n_q, MLA_HEADS * MLA_V), lambda b, t: (b, 0, 0)),
        out_shape=jax.ShapeDtypeStruct((bsz, n_q, MLA_HEADS * MLA_V), BF16),
        scratch_shapes=[pltpu.VMEM((stacked, MLA_KV_RANK), BF16), pltpu.VMEM((stacked, MLA_ROPE), BF16),
                        pltpu.VMEM((1, stacked, 1), F32), pltpu.VMEM((1, stacked, 1), F32),
                        pltpu.VMEM((1, stacked, MLA_KV_RANK), F32)],
        compiler_params=_cparams(2, VMEM_LIMIT),
        name="mla_decode",
    )(q, p["wkn"], p["wv"], cache_ckv, cache_kpe.transpose(0, 1, 3, 2), ckv_new, kpe_new)


def _gate_body(x_ref, b_ref, lf_ref, fc_ref, *, new_start, first_row):
    n_rows = x_ref.shape[1]
    tri = (lax.broadcasted_iota(jnp.int32, (TILE, TILE), 0)
           >= lax.broadcasted_iota(jnp.int32, (TILE, TILE), 1)).astype(F32)
    carry = jnp.zeros((1, x_ref.shape[2]), F32)
    for i in range(n_rows // TILE):
        sl = slice(i * TILE, (i + 1) * TILE)
        x = x_ref[0, sl, :]
        rows = i * TILE + lax.broadcasted_iota(jnp.int32, (TILE, 1), 0)
        z = x + b_ref[...]
        lf = jnp.where(rows >= new_start, jnp.minimum(z, 0.0) - jnp.log1p(jnp.exp(-jnp.abs(z))), x)
        if first_row:
            lf = jnp.where(rows >= first_row, lf, 0.0)
        lf_ref[0, sl, :] = lf
        cs = jnp.dot(tri, lf, preferred_element_type=F32, precision=lax.Precision.HIGHEST) + carry
        fc_ref[0, sl, :] = cs
        carry = cs[TILE - 1:TILE, :]


def _gate(x, b_f, new_start, first_row):
    bsz, n_rows, heads = x.shape
    lanes = bsz * heads
    spec = pl.BlockSpec((1, n_rows, lanes), lambda b: (0, 0, 0))
    lf, fc = pl.pallas_call(
        functools.partial(_gate_body, new_start=new_start, first_row=first_row),
        grid=(1,),
        in_specs=[spec, _const_spec((1, lanes))],
        out_specs=[spec, spec],
        out_shape=[jax.ShapeDtypeStruct((1, n_rows, lanes), F32)] * 2,
        compiler_params=_cparams(1),
        name="fox_gate",
    )(x.transpose(1, 0, 2).reshape(1, n_rows, lanes), jnp.tile(b_f, (1, bsz)))
    per_batch = lambda a: a.reshape(n_rows, bsz, heads).transpose(1, 0, 2)
    return per_batch(lf), per_batch(fc)


RET_LOG_GAMMA = tuple(math.log(1.0 - 2.0 ** (-5.0 - h)) for h in range(RET_HEADS))


def _ret_body(rq_ref, rk_ref, rv_ref, rg_ref, s0_ref, dec_ref, cq_ref, s1q_ref, s2q_ref, ck_ref, s1k_ref, s2k_ref,
              o_ref, sl_ref, st_ref, *, n_tail):
    c = pl.program_id(1)
    ct = rq_ref.shape[1]

    @pl.when(c == 0)
    def _():
        st_ref[...] = s0_ref[0]

    q = _rope_lanes(rq_ref[0], cq_ref[...], s1q_ref[...], s2q_ref[...], RET_DK // 2)
    k = _rope_lanes(rk_ref[0], ck_ref[...], s1k_ref[...], s2k_ref[...], RET_DK // 2)
    v = rv_ref[0].astype(BF16)
    g = rg_ref[0]
    j = lax.broadcasted_iota(jnp.int32, (ct, 1), 0).astype(F32)
    for hd in range(RET_HEADS):
        lg = RET_LOG_GAMMA[hd]
        q_h = q[:, hd * RET_DK:(hd + 1) * RET_DK]
        k_h = k[:, hd * RET_DK:(hd + 1) * RET_DK]
        v_h = v[:, hd * RET_DV:(hd + 1) * RET_DV]
        scores = lax.dot_general(q_h.astype(BF16), k_h.astype(BF16), (((1,), (1,)), ((), ())),
                                 preferred_element_type=F32) * dec_ref[hd]
        s_h = st_ref[hd]
        out = jnp.dot(scores.astype(BF16), v_h, preferred_element_type=F32)
        out = out + jnp.dot((q_h * jnp.exp(lg * (j + 1.0))).astype(BF16), s_h.astype(BF16),
                            preferred_element_type=F32)
        k_dec = (k_h * jnp.exp(lg * (ct - 1.0 - j))).astype(BF16)
        st_ref[hd] = math.exp(lg * ct) * s_h + lax.dot_general(
            k_dec, v_h, (((0,), (0,)), ((), ())), preferred_element_type=F32)
        mu = jnp.mean(out, axis=-1, keepdims=True)
        oc = out - mu
        var = jnp.mean(oc * oc, axis=-1, keepdims=True)
        g_h = g[:, hd * RET_DV:(hd + 1) * RET_DV]
        o_ref[0, :, hd * RET_DV:(hd + 1) * RET_DV] = (
            g_h * jax.nn.sigmoid(g_h) * (oc * lax.rsqrt(var + EPS))).astype(o_ref.dtype)

    @pl.when(c == pl.num_programs(1) - 1)
    def _():
        for hd in range(RET_HEADS):
            sl_ref[0, hd] = st_ref[hd] * math.exp(-RET_LOG_GAMMA[hd] * n_tail)


def _retention(h, s0, tabs, ct, n_tail):
    bsz, seq, _ = h.shape
    tab_spec = pl.BlockSpec((ct, RET_QK), lambda b, c: (c, 0))
    st_spec = pl.BlockSpec((1, RET_HEADS, RET_DK, RET_DV), lambda b, c: (b, 0, 0, 0))
    rq_blk = 0
    rv_blk = (2 * RET_QK) // RET_VW
    diff = np.arange(ct, dtype=np.float32)[:, None] - np.arange(ct, dtype=np.float32)[None, :]
    decay = np.where(diff >= 0.0, np.exp(np.asarray(RET_LOG_GAMMA, np.float32)[:, None, None] * np.maximum(diff, 0.0)),
                     np.float32(0.0)).astype(np.float32)
    return pl.pallas_call(
        functools.partial(_ret_body, n_tail=n_tail),
        grid=(bsz, seq // ct),
        in_specs=[
            pl.BlockSpec((1, ct, RET_QK), lambda b, c: (b, c, rq_blk)),
            pl.BlockSpec((1, ct, RET_QK), lambda b, c: (b, c, rq_blk + 1)),
            pl.BlockSpec((1, ct, RET_VW), lambda b, c: (b, c, rv_blk)),
            pl.BlockSpec((1, ct, RET_VW), lambda b, c: (b, c, rv_blk + 1)),
            st_spec, _const_spec((RET_HEADS, ct, ct)),
        ] + [tab_spec] * 6,
        out_specs=[pl.BlockSpec((1, ct, RET_VW), lambda b, c: (b, c, 0)), st_spec],
        out_shape=[jax.ShapeDtypeStruct((bsz, seq, RET_VW), BF16),
                   jax.ShapeDtypeStruct((bsz, RET_HEADS, RET_DK, RET_DV), F32)],
        scratch_shapes=[pltpu.VMEM((RET_HEADS, RET_DK, RET_DV), F32)],
        compiler_params=_cparams(2, VMEM_LIMIT),
        name="retention",
    )(h, h, h, h, s0, decay, *tabs)


def _rope_tables(pos, half, width, group, offset, scale=1.0, valid=None):
    f32 = np.float32
    inv = f32(ROPE_BASE) ** (-np.arange(half, dtype=f32) / f32(half))
    ang = pos.astype(f32)[:, None] * inv[None, :]
    cos, sin = np.cos(ang), np.sin(ang)
    n = pos.shape[0]
    one, zero = np.ones((n, 1), f32), np.zeros((n, 1), f32)

    def lanes(first, second, other):
        grp = np.concatenate([np.broadcast_to(other, (n, offset)), first, second,
                              np.broadcast_to(other, (n, group - offset - 2 * half))], axis=1)
        return np.tile(grp, (1, width // group))

    tabs = (lanes(cos, cos, one), lanes(-sin, 0.0 * sin, zero), lanes(0.0 * sin, sin, zero))
    if valid is not None:
        tabs = tuple(np.where(valid[:, None], t, f32(0.0)) for t in tabs)
    return tuple((t * f32(scale)).astype(f32) for t in tabs)


def _even_params(e, w):
    p = {}
    p["w_in"] = jnp.pad(w["even_w_in"][e], ((0, 0), (0, EVEN_IN_PAD - EVEN_IN))).astype(BF16)
    p["w_out"] = w["even_w_out"][e].astype(BF16)
    lam_re, lam_im = w["s5_a_re"][e].astype(F32), w["s5_a_im"][e].astype(F32)
    dt = jnp.exp(w["s5_log_dt"][e].astype(F32))[:, None]
    p["lam_dt_re"], p["lam_dt_im"] = lam_re * dt, lam_im * dt
    mag = jnp.exp(lam_re * dt)
    abar_re, abar_im = mag * jnp.cos(lam_im * dt), mag * jnp.sin(lam_im * dt)
    den = lam_re * lam_re + lam_im * lam_im
    f_re = ((abar_re - 1.0) * lam_re + abar_im * lam_im) / den
    f_im = (abar_im * lam_re - (abar_re - 1.0) * lam_im) / den
    b_re, b_im = w["s5_b_re"][e].astype(F32), w["s5_b_im"][e].astype(F32)
    bb_re = f_re[..., None] * b_re - f_im[..., None] * b_im
    bb_im = f_re[..., None] * b_im + f_im[..., None] * b_re
    eye = jnp.eye(S5_GROUPS, dtype=F32)

    def in_blocks(x):
        return jnp.einsum("gnc,gh->gchn", x, eye).reshape(S5_WIDTH, S5_HALF)

    def out_blocks(x):
        return jnp.einsum("gcn,gh->gnhc", x, eye).reshape(S5_HALF, S5_WIDTH)

    p["bbig"] = jnp.concatenate([in_blocks(bb_re), in_blocks(bb_im)], axis=1).astype(BF16)
    p["cbig"] = jnp.concatenate([out_blocks(w["s5_c_re"][e].astype(F32)),
                                 out_blocks(-w["s5_c_im"][e].astype(F32))], axis=0).astype(BF16)
    p["a_re"] = abar_re.reshape(1, S5_HALF)
    p["a_im"] = abar_im.reshape(1, S5_HALF)
    p["d"] = w["s5_d"][e].astype(F32).reshape(1, S5_WIDTH)
    p["w_glu"] = w["s5_w_glu"][e].astype(BF16)
    p["b_glu"] = w["s5_b_glu"][e].astype(F32).reshape(1, S5_WIDTH)
    p["q_norm"] = w["mla_q_norm"][e].astype(F32).reshape(1, MLA_Q_RANK)
    p["kv_norm"] = w["mla_kv_norm"][e].astype(F32).reshape(1, MLA_KV_RANK)
    wq = w["mla_w_uq"][e].reshape(MLA_Q_RANK, MLA_HEADS, MLA_NOPE + MLA_ROPE)
    wq = jnp.pad(wq, ((0, 0), (0, 0), (0, MLA_QK_PAD - MLA_NOPE - MLA_ROPE)))
    p["wq"] = wq.reshape(MLA_Q_RANK, MLA_HEADS * MLA_QK_PAD).astype(BF16)
    wkv = w["mla_w_ukv"][e].reshape(MLA_KV_RANK, MLA_HEADS, MLA_NOPE + MLA_V)
    wkn = jnp.pad(wkv[:, :, :MLA_NOPE], ((0, 0), (0, 0), (0, MLA_QK_PAD - MLA_NOPE)))
    p["wkn"] = wkn.reshape(MLA_KV_RANK, MLA_HEADS * MLA_QK_PAD).astype(BF16)
    place = jnp.pad(jnp.eye(MLA_ROPE, dtype=F32), ((0, 0), (MLA_NOPE, MLA_QK_PAD - MLA_NOPE - MLA_ROPE)))
    p["wkp"] = jnp.tile(place, (1, MLA_HEADS)).astype(BF16)
    p["wv"] = wkv[:, :, MLA_NOPE:].reshape(MLA_KV_RANK, MLA_HEADS * MLA_V).astype(BF16)
    return p


def _odd_params(o, w):
    w_in = w["odd_w_in"][o]
    c_logit = 3 * FOX_W
    cols = jnp.concatenate([
        w_in[:, :FOX_W] * (FOX_DIM ** -0.5),
        w_in[:, FOX_W:c_logit],
        w_in[:, c_logit + FOX_HEADS:],
        w_in[:, c_logit:c_logit + FOX_HEADS],
    ], axis=1)
    p = {"w_in": jnp.pad(cols, ((0, 0), (0, ODD_IN_PAD - cols.shape[1]))).astype(BF16)}
    p["w_out"] = w["odd_w_out"][o].astype(BF16)
    p["b_f"] = w["fox_b_f"][o].astype(F32).reshape(1, FOX_HEADS)
    return p


def _cache_tile(n_past):
    for tk in (2048, 1024, 512, 256, 128):
        if n_past % tk == 0:
            return tk
    raise ValueError("cache length must be a multiple of 128")


def _trunk(x, pos, n_real, out_rows, past, w, ffn, evens, odds):
    bsz, seq, _ = x.shape
    rows = bsz * seq
    prompt = past is None
    first_row = PAD if prompt else 0
    if prompt:
        tm = next(m * TILE for m in (4, 2, 1) if rows % (m * TILE) == 0 and out_rows[1] % (m * TILE) == 0)
        rt = TILE
    else:
        n_past = past["cache_fox_k"].shape[2]
        tk = _cache_tile(n_past)
        tm, rt = rows, seq
    idx = np.arange(seq)
    valid = (idx >= first_row) & (idx < n_real)
    st = {n: [] for n in ("mla_ckv", "mla_kpe", "s5_re", "s5_im", "fox_k", "fox_v", "fox_logf", "ret")}
    rq_tabs = _rope_tables(pos, RET_DK // 2, RET_QK, RET_DK, 0)
    rk_tabs = _rope_tables(pos, RET_DK // 2, RET_QK, RET_DK, 0, scale=RET_DK ** -0.5, valid=valid)
    x2 = x.reshape(rows, D_MODEL)

    def ln(l, i):
        return w["ln_g"][l, i].reshape(1, D_MODEL), w["ln_b"][l, i].reshape(1, D_MODEL)

    for l in range(DEPTH):
        x2 = _ffn_ln(x2, ffn, (l, 0), *ln(l, 0), tm)
        if l % 2 == 0:
            e = l // 2
            p = evens[e]
            if prompt:
                u, q, ckv, kpe, k_att, v_att = _even_rows(x2.reshape(bsz, seq, D_MODEL), p, pos)
                h0r = h0i = jnp.zeros((bsz, 1, S5_HALF), F32)
                mla_out = _flash(q, k_att, v_att, None, None, heads=MLA_HEADS, dq=MLA_QK_PAD, dv=MLA_V,
                                 causal=False)
            else:
                u = _proj(x2, p["w_in"], tm).reshape(bsz, seq, EVEN_IN_PAD)
                h0r = past["state_s5_re"][e].astype(F32).reshape(bsz, 1, S5_HALF)
                h0i = past["state_s5_im"][e].astype(F32).reshape(bsz, 1, S5_HALF)
                q, ckv, kpe = _mla_rows(u, p, pos, rt)
                mla_out = _mla_decode(q, ckv, kpe, past["cache_mla_ckv"], past["cache_mla_kpe"], e, p, tk)
            s5_out, hlr, hli = _s5(u, h0r, h0i, p, rt, first_row, n_real)
            mix = (s5_out.reshape(rows, S5_WIDTH), mla_out.reshape(rows, MLA_HEADS * MLA_V), p["w_out"])
            st["mla_ckv"].append(ckv)
            st["mla_kpe"].append(kpe)
            st["s5_re"].append(hlr.reshape(bsz, S5_GROUPS, S5_STATE))
            st["s5_im"].append(hli.reshape(bsz, S5_GROUPS, S5_STATE))
        else:
            o = l // 2
            p = odds[o]
            h, k32, v32, fq, fk16, fv = _proj_odd(x2, p["w_in"], bsz, TILE if prompt else tm, attn_layout=prompt)
            h = h.reshape(bsz, seq, ODD_H)
            fk16 = fk16.reshape(bsz, seq, FOX_W)
            f_logit = h[:, :, ODD_LOGIT_COL:ODD_LOGIT_COL + FOX_HEADS]
            if prompt:
                q_off = 0
                logf, fcum = _gate(f_logit, p["b_f"], 0, PAD)
                fcum = fcum * LOG2E
                fox_out = _flash(fq, fk16, fv, fcum.transpose(0, 2, 1), fcum, heads=FOX_HEADS, dq=FOX_DIM,
                                 dv=FOX_DIM, causal=True)
                s0 = jnp.zeros((bsz, RET_HEADS, RET_DK, RET_DV), F32)
            else:
                q_off = n_past
                gates = jnp.concatenate([past["cache_fox_logf"][o].astype(F32), f_logit], axis=1)
                gates = jnp.pad(gates, ((0, 0), (0, _round_up(n_past + seq, TILE) - n_past - seq), (0, 0)))
                logf, fcum = _gate(gates, p["b_f"], n_past, 0)
                fox_out = _fox_decode(fq.reshape(bsz, seq, FOX_W), fk16, fv.reshape(bsz, seq, FOX_W),
                                      past["cache_fox_k"], past["cache_fox_v"], o, fcum * LOG2E, tk)
                s0 = past["state_ret"][o].astype(F32)
            ret_out, s_last = _retention(h, s0, rq_tabs + rk_tabs, rt, seq - n_real)
            mix = (fox_out.reshape(rows, FOX_W), ret_out.reshape(rows, RET_VW), p["w_out"])
            if prompt:
                k32, v32 = k32.transpose(0, 3, 1, 2), v32.transpose(0, 3, 1, 2)
            st["fox_k"].append(k32.reshape(bsz, seq, FOX_HEADS, FOX_DIM))
            st["fox_v"].append(v32.reshape(bsz, seq, FOX_HEADS, FOX_DIM))
            st["fox_logf"].append(logf[:, q_off:q_off + seq])
            st["ret"].append(s_last)
        keep = (seq,) + out_rows if l == DEPTH - 1 and out_rows != (0, seq) else None
        x2 = _mix_ffn_ln(x2, *mix, ln(l, 1), ffn, (l, 1), ln(l, 2), tm, keep)
    return x2.reshape(bsz, out_rows[1], D_MODEL), {n: jnp.stack(a) for n, a in st.items()}


def kernel(x_prompt, x_sample, cache_mla_ckv, cache_mla_kpe, cache_fox_k, cache_fox_v, cache_fox_logf,
           state_s5_re, state_s5_im, state_ret, meta_tokens, ln_g, ln_b, ffn_w_gate, ffn_w_up, ffn_w_down,
           even_w_in, even_w_out, s5_a_re, s5_a_im, s5_b_re, s5_b_im, s5_c_re, s5_c_im, s5_d, s5_log_dt,
           s5_w_glu, s5_b_glu, mla_q_norm, mla_kv_norm, mla_w_uq, mla_w_ukv, odd_w_in, odd_w_out, fox_b_f):
    w = dict(ln_g=ln_g.astype(F32), ln_b=ln_b.astype(F32), even_w_in=even_w_in, even_w_out=even_w_out,
             s5_a_re=s5_a_re, s5_a_im=s5_a_im, s5_b_re=s5_b_re, s5_b_im=s5_b_im, s5_c_re=s5_c_re,
             s5_c_im=s5_c_im, s5_d=s5_d, s5_log_dt=s5_log_dt, s5_w_glu=s5_w_glu, s5_b_glu=s5_b_glu,
             mla_q_norm=mla_q_norm, mla_kv_norm=mla_kv_norm, mla_w_uq=mla_w_uq, mla_w_ukv=mla_w_ukv,
             odd_w_in=odd_w_in, odd_w_out=odd_w_out, fox_b_f=fox_b_f)
    past = dict(cache_mla_ckv=cache_mla_ckv, cache_mla_kpe=cache_mla_kpe, cache_fox_k=cache_fox_k,
                cache_fox_v=cache_fox_v, cache_fox_logf=cache_fox_logf, state_s5_re=state_s5_re,
                state_s5_im=state_s5_im, state_ret=state_ret)
    ffn = (ffn_w_gate.astype(BF16), ffn_w_up.astype(BF16), ffn_w_down.astype(BF16))
    evens = [_even_params(e, w) for e in range((DEPTH + 1) // 2)]
    odds = [_odd_params(o, w) for o in range(DEPTH // 2)]

    bsz, seq, _ = x_prompt.shape
    n_real = PAD + N_META + seq
    n_rows = _round_up(n_real, TILE)
    meta = jnp.broadcast_to(meta_tokens[None].astype(x_prompt.dtype), (bsz, N_META, D_MODEL))
    xp = jnp.concatenate([jnp.zeros((bsz, PAD, D_MODEL), x_prompt.dtype), meta, x_prompt,
                          jnp.zeros((bsz, n_rows - n_real, D_MODEL), x_prompt.dtype)], axis=1)
    pos_p = np.maximum(np.arange(n_rows) - PAD, 0)
    y_p, st_p = _trunk(xp, pos_p, n_real, (PAD + N_META, seq), None, w, ffn, evens, odds)
    d_seq = x_sample.shape[1]
    pos_s = N_META + cache_fox_k.shape[2] + np.arange(d_seq)
    y_s, st_s = _trunk(x_sample, pos_s, d_seq, (0, d_seq), past, w, ffn, evens, odds)

    def real(a):
        return a[:, :, PAD:n_real]

    return (y_p, y_s,
            real(st_p["mla_ckv"]), real(st_p["mla_kpe"]), real(st_p["fox_k"]), real(st_p["fox_v"]),
            real(st_p["fox_logf"]), st_p["s5_re"], st_p["s5_im"], st_p["ret"],
            st_s["mla_ckv"], st_s["mla_kpe"], st_s["fox_k"], st_s["fox_v"], st_s["fox_logf"],
            st_s["s5_re"], st_s["s5_im"], st_s["ret"])
```
